```python
import math
import jax, jax.numpy as jnp
from jax import lax
import numpy as np

D_MODEL = 1024
BATCH = 16
SEQ = 2048
DEPTH = 2

D_MIX = D_MODEL
W_CONV = D_MIX // 4
W_GLA = D_MIX // 4
W_SSD = D_MIX // 4
W_DIFF = D_MIX - W_CONV - W_GLA - W_SSD

CONV_K = 3

GLA_HEADS = 4
GLA_DV = W_GLA // GLA_HEADS
GLA_DK = GLA_DV // 2
GLA_RANK = 16
GLA_TAU = 16.0
GLA_CHUNK = 64

SSD_HEADDIM = 64
SSD_HEADS = W_SSD // SSD_HEADDIM
SSD_GROUPS = 2
SSD_STATE = 128
SSD_CONV_K = 4
SSD_CHUNK = 128
SSD_XBC = W_SSD + 2 * SSD_GROUPS * SSD_STATE

DIFF_HEADS = 4
DIFF_DV = W_DIFF // DIFF_HEADS
DIFF_DQK = DIFF_DV // 2
ATTN_BLOCK = 128

N_GROUPS = 4
EXPERTS_PER_GROUP = 8
N_EXPERTS = N_GROUPS * EXPERTS_PER_GROUP
TOP_K = 2
D_EXPERT = 512
MOE_BLOCK = 128

DEEPNORM_ALPHA = (2 * DEPTH) ** 0.25
DEEPNORM_BETA = (8 * DEPTH) ** -0.25
LN_EPS = 1e-5
RMS_EPS = 1e-6

SPLIT_SIZES = (
    W_CONV, W_CONV, W_CONV,
    GLA_HEADS * GLA_DK, GLA_HEADS * GLA_DK,
    GLA_HEADS * GLA_DV, GLA_HEADS * GLA_DV, GLA_RANK,
    W_SSD, SSD_XBC, SSD_HEADS,
    DIFF_HEADS * 2 * DIFF_DQK, DIFF_HEADS * 2 * DIFF_DQK,
    DIFF_HEADS * DIFF_DV,
)
P_IN = sum(SPLIT_SIZES)

kernel_name = 'hybrid_parallel_heads_hmoe_deepnorm'


def layer_norm(x, g, b):
    xf = x.astype(jnp.float32)
    mu = jnp.mean(xf, axis=-1, keepdims=True)
    var = jnp.mean(jnp.square(xf - mu), axis=-1, keepdims=True)
    return ((xf - mu) * lax.rsqrt(var + LN_EPS) * g + b).astype(x.dtype)


def rms_norm(x, g):
    xf = x.astype(jnp.float32)
    return xf * lax.rsqrt(jnp.mean(jnp.square(xf), axis=-1, keepdims=True) + RMS_EPS) * g


def split_cols(p, sizes):
    outs, off = [], 0
    for sz in sizes:
        outs.append(p[..., off:off + sz])
        off += sz
    return outs


def causal_dwconv(x, w, b=None):
    k_w, s = w.shape[0], x.shape[1]
    xp = jnp.pad(x.astype(jnp.float32), ((0, 0), (k_w - 1, 0), (0, 0)))
    y = sum(xp[:, k:k + s] * w[k] for k in range(k_w))
    return y if b is None else y + b


def short_conv_mixer(u, gate_b, gate_c, conv_w):
    return gate_b.astype(jnp.float32) * causal_dwconv(gate_c * u, conv_w)


def gla_mixer(q, k, v, g, lr, w_lr, b_lr, norm_g):
    bsz, s, _ = q.shape
    h, dk, dv, c = GLA_HEADS, GLA_DK, GLA_DV, GLA_CHUNK
    nc = s // c
    f32 = jnp.float32
    q = q.astype(f32).reshape(bsz, nc, c, h, dk) * dk ** -0.5
    k = k.astype(f32).reshape(bsz, nc, c, h, dk)
    v = v.astype(f32).reshape(bsz, nc, c, h, dv)
    log_a = jax.nn.log_sigmoid(lr.astype(f32) @ w_lr.astype(f32) + b_lr) / GLA_TAU
    cum = jnp.cumsum(log_a.reshape(bsz, nc, c, h, dk), axis=2)
    cum_last = cum[:, :, -1]
    q_dec = q * jnp.exp(cum)
    k_inv = k * jnp.exp(-cum)
    causal = jnp.tril(jnp.ones((c, c), bool))
    att = jnp.where(causal, jnp.einsum('bnihd,bnjhd->bnhij', q_dec, k_inv), 0.0)
    o_intra = jnp.einsum('bnhij,bnjhe->bnihe', att, v)
    k_end = k * jnp.exp(cum_last[:, :, None] - cum)
    d_state = jnp.einsum('bnjhd,bnjhe->bnhde', k_end, v)

    def step(state, inp):
        dec, ds = inp
        return dec[..., None] * state + ds, state

    _, s_prev = lax.scan(step, jnp.zeros((bsz, h, dk, dv), f32),
                         (jnp.moveaxis(jnp.exp(cum_last), 1, 0), jnp.moveaxis(d_state, 1, 0)))
    s_prev = jnp.moveaxis(s_prev, 0, 1)
    o = o_intra + jnp.einsum('bnihd,bnhde->bnihe', q_dec, s_prev)
    o = rms_norm(o.reshape(bsz, s, h, dv), norm_g).reshape(bsz, s, h * dv)
    return o * jax.nn.silu(g.astype(f32))


def ssd_mixer(z, xbc, dt, conv_w, conv_b, a_log, d_skip, dt_bias, norm_g):
    bsz, s, _ = z.shape
    g, r, p, n, c = SSD_GROUPS, SSD_HEADS // SSD_GROUPS, SSD_HEADDIM, SSD_STATE, SSD_CHUNK
    nc = s // c
    f32 = jnp.float32
    xbc = jax.nn.silu(causal_dwconv(xbc, conv_w, conv_b))
    xs, bm, cm = split_cols(xbc, (W_SSD, g * n, g * n))
    x = xs.reshape(bsz, nc, c, g, r, p)
    bm = bm.reshape(bsz, nc, c, g, n)
    cm = cm.reshape(bsz, nc, c, g, n)
    dt = jax.nn.softplus(dt.astype(f32) + dt_bias).reshape(bsz, nc, c, g, r)
    a = -jnp.exp(a_log.astype(f32)).reshape(g, r)
    cum = jnp.cumsum(dt * a, axis=2)
    cum_t = jnp.moveaxis(cum, 2, -1)
    causal = jnp.tril(jnp.ones((c, c), bool))
    decay = jnp.exp(jnp.where(causal, cum_t[..., :, None] - cum_t[..., None, :], -jnp.inf))
    x_dt = x * dt[..., None]
    cb = jnp.einsum('bcign,bcjgn->bcgij', cm, bm)
    y_diag = jnp.einsum('bcgij,bcgrij,bcjgrp->bcigrp', cb, decay, x_dt)
    cum_last = cum[:, :, -1]
    st = jnp.einsum('bcjgn,bcjgr,bcjgrp->bcgrpn', bm, jnp.exp(cum_last[:, :, None] - cum), x_dt)

    def step(state, inp):
        dec, ds = inp
        return dec[..., None, None] * state + ds, state

    _, s_prev = lax.scan(step, jnp.zeros((bsz, g, r, p, n), f32),
                         (jnp.moveaxis(jnp.exp(cum_last), 1, 0), jnp.moveaxis(st, 1, 0)))
    s_prev = jnp.moveaxis(s_prev, 0, 1)
    y_off = jnp.einsum('bcign,bcgrpn,bcigr->bcigrp', cm, s_prev, jnp.exp(cum))
    y = y_diag + y_off + x * d_skip.astype(f32).reshape(g, r)[..., None]
    y = y.reshape(bsz, s, W_SSD) * jax.nn.silu(z.astype(f32))
    y = rms_norm(y.reshape(bsz, s, g, W_SSD // g), norm_g.reshape(g, W_SSD // g))
    return y.reshape(bsz, s, W_SSD)


def diff_attention(q, k, v, lq1, lk1, lq2, lk2, norm_g, layer_idx):
    bsz, s, _ = q.shape
    h, dqk, dv, bq = DIFF_HEADS, DIFF_DQK, DIFF_DV, ATTN_BLOCK
    f32 = jnp.float32
    q = q.astype(f32).reshape(bsz, s, h, 2, dqk) * dqk ** -0.5
    k = k.astype(f32).reshape(bsz, s, h, 2, dqk)
    v = v.astype(f32).reshape(bsz, s, h, dv)
    lam_init = 0.8 - 0.6 * math.exp(-0.3 * layer_idx)
    lam = (jnp.exp(jnp.sum(lq1.astype(f32) * lk1)) - jnp.exp(jnp.sum(lq2.astype(f32) * lk2))
           + lam_init)
    k_pos = jnp.arange(s)

    def block(i):
        qb = lax.dynamic_slice_in_dim(q, i * bq, bq, axis=1)
        q_pos = i * bq + jnp.arange(bq)
        sc = jnp.einsum('bqhcd,bkhcd->bhcqk', qb, k)
        sc = jnp.where(k_pos[None, :] <= q_pos[:, None], sc, -jnp.inf)
        pr = jax.nn.softmax(sc, axis=-1)
        w = pr[:, :, 0] - lam * pr[:, :, 1]
        return jnp.einsum('bhqk,bkhe->bqhe', w, v)

    o = lax.map(block, jnp.arange(s // bq))
    o = jnp.moveaxis(o, 0, 1).reshape(bsz, s, h, dv)
    o = rms_norm(o, norm_g) * (1.0 - lam_init)
    return o.reshape(bsz, s, h * dv)


def expert_dispatch(xf, expert, gate, w_gate, w_up, w_down):
    t, d = xf.shape
    k_sel = expert.shape[1]
    n_assign = t * k_sel
    e_n, blk = N_EXPERTS, MOE_BLOCK
    n_blocks = (n_assign + e_n * (blk - 1) + blk - 1) // blk
    n_slots = n_blocks * blk
    flat_e = expert.reshape(n_assign)
    order = jnp.argsort(flat_e)
    sorted_e = flat_e[order]
    counts = jnp.bincount(flat_e, length=e_n)
    padded = (counts + blk - 1) // blk * blk
    pad_end = jnp.cumsum(padded)
    pad_start = pad_end - padded
    start = jnp.cumsum(counts) - counts
    dest = pad_start[sorted_e] + (jnp.arange(n_assign) - start[sorted_e])
    slot_tok = jnp.full((n_slots,), t, jnp.int32).at[dest].set((order // k_sel).astype(jnp.int32))
    slot_gate = jnp.zeros((n_slots,), gate.dtype).at[dest].set(gate.reshape(n_assign)[order])
    block_e = jnp.minimum(jnp.searchsorted(pad_end, jnp.arange(n_blocks) * blk, side='right'), e_n - 1)
    x_pad = jnp.concatenate([xf, jnp.zeros((1, d), xf.dtype)], axis=0)

    def run_block(args):
        tok, e = args
        xb = x_pad[tok]
        hb = jax.nn.silu(xb @ w_gate[e]) * (xb @ w_up[e])
        return hb @ w_down[e]

    y = lax.map(run_block, (slot_tok.reshape(n_blocks, blk), block_e)).reshape(n_slots, d)
    y = y * slot_gate[:, None].astype(y.dtype)
    return jnp.zeros((t + 1, d), y.dtype).at[slot_tok].add(y)[:t]


def hierarchical_moe(x, router_g, router_e, w_gate, w_up, w_down):
    bsz, s, d = x.shape
    t = bsz * s
    xf = x.reshape(t, d)
    g_prob = jax.nn.softmax((xf @ router_g).astype(jnp.float32), axis=-1)
    p_grp, grp = lax.top_k(g_prob, 1)
    e_logits = jnp.einsum('td,dge->tge', xf, router_e).astype(jnp.float32)
    e_logits = jnp.take_along_axis(
        e_logits, jnp.broadcast_to(grp[:, :, None], (t, 1, EXPERTS_PER_GROUP)), axis=1)[:, 0]
    p_exp, idx = lax.top_k(jax.nn.softmax(e_logits, axis=-1), TOP_K)
    gate = p_grp * p_exp / jnp.sum(p_exp, axis=-1, keepdims=True)
    expert = grp * EXPERTS_PER_GROUP + idx
    y = expert_dispatch(xf, expert, gate, w_gate, w_up, w_down)
    return y.reshape(bsz, s, d)


def setup_inputs(seed: int = 0) -> dict:
    key = jax.random.key(seed)
    ks = jax.random.split(key, 32)
    L, D = DEPTH, D_MODEL

    def nrm(k, shape, scale):
        return jax.random.normal(k, shape, jnp.float32) * scale

    def gain(k, shape):
        return 1.0 + 0.02 * jax.random.normal(k, shape, jnp.float32)

    dt0 = jnp.exp(jax.random.uniform(ks[10], (L, SSD_HEADS), jnp.float32,
                                     minval=math.log(1e-3), maxval=math.log(1e-1)))
    return {
        'x': nrm(ks[0], (BATCH, SEQ, D), 1.0),
        'w_in': nrm(ks[1], (L, D, P_IN), D ** -0.5),
        'conv_w': nrm(ks[2], (L, CONV_K, W_CONV), CONV_K ** -0.5),
        'gla_w_lr': nrm(ks[3], (L, GLA_RANK, GLA_HEADS * GLA_DK), GLA_RANK ** -0.5),
        'gla_b_lr': nrm(ks[4], (L, GLA_HEADS * GLA_DK), 0.1),
        'gla_norm_g': gain(ks[5], (L, GLA_DV)),
        'ssd_conv_w': nrm(ks[6], (L, SSD_CONV_K, SSD_XBC), SSD_CONV_K ** -0.5),
        'ssd_conv_b': nrm(ks[7], (L, SSD_XBC), 0.02),
        'ssd_a_log': jnp.log(jax.random.uniform(ks[8], (L, SSD_HEADS), jnp.float32, minval=1.0, maxval=16.0)),
        'ssd_d': gain(ks[9], (L, SSD_HEADS)),
        'ssd_dt_bias': dt0 + jnp.log(-jnp.expm1(-dt0)),
        'ssd_norm_g': gain(ks[11], (L, W_SSD)),
        'diff_lq1': nrm(ks[12], (L, DIFF_DQK), 0.1),
        'diff_lk1': nrm(ks[13], (L, DIFF_DQK), 0.1),
        'diff_lq2': nrm(ks[14], (L, DIFF_DQK), 0.1),
        'diff_lk2': nrm(ks[15], (L, DIFF_DQK), 0.1),
        'diff_norm_g': gain(ks[16], (L, DIFF_DV)),
        'w_o': nrm(ks[17], (L, D_MIX, D), D_MIX ** -0.5 * DEEPNORM_BETA),
        'ln1_g': gain(ks[18], (L, D)),
        'ln1_b': nrm(ks[19], (L, D), 0.02),
        'router_g': nrm(ks[20], (L, D, N_GROUPS), D ** -0.5),
        'router_e': nrm(ks[21], (L, D, N_GROUPS, EXPERTS_PER_GROUP), D ** -0.5),
        'w_gate': nrm(ks[22], (L, N_EXPERTS, D, D_EXPERT), D ** -0.5),
        'w_up': nrm(ks[23], (L, N_EXPERTS, D, D_EXPERT), D ** -0.5),
        'w_down': nrm(ks[24], (L, N_EXPERTS, D_EXPERT, D), D_EXPERT ** -0.5 * DEEPNORM_BETA),
        'ln2_g': gain(ks[25], (L, D)),
        'ln2_b': nrm(ks[26], (L, D), 0.02),
    }


def reference(x, w_in, conv_w, gla_w_lr, gla_b_lr, gla_norm_g, ssd_conv_w, ssd_conv_b,
              ssd_a_log, ssd_d, ssd_dt_bias, ssd_norm_g, diff_lq1, diff_lk1, diff_lq2,
              diff_lk2, diff_norm_g, w_o, ln1_g, ln1_b, router_g, router_e, w_gate, w_up,
              w_down, ln2_g, ln2_b):
    for l in range(DEPTH):
        proj = x @ w_in[l]
        (cu, cgb, cgc, gq, gk, gv, gg, glr, sz, sxbc, sdt,
         dfq, dfk, dfv) = split_cols(proj, SPLIT_SIZES)
        y_conv = short_conv_mixer(cu, cgb, cgc, conv_w[l])
        y_gla = gla_mixer(gq, gk, gv, gg, glr, gla_w_lr[l], gla_b_lr[l], gla_norm_g[l])
        y_ssd = ssd_mixer(sz, sxbc, sdt, ssd_conv_w[l], ssd_conv_b[l], ssd_a_log[l], ssd_d[l],
                          ssd_dt_bias[l], ssd_norm_g[l])
        y_diff = diff_attention(dfq, dfk, dfv, diff_lq1[l], diff_lk1[l], diff_lq2[l],
                                diff_lk2[l], diff_norm_g[l], l)
        mix = jnp.concatenate([y_conv, y_gla, y_ssd, y_diff], axis=-1).astype(x.dtype) @ w_o[l]
        x = layer_norm(DEEPNORM_ALPHA * x + mix, ln1_g[l], ln1_b[l])
        moe = hierarchical_moe(x, router_g[l], router_e[l], w_gate[l], w_up[l], w_down[l])
        x = layer_norm(DEEPNORM_ALPHA * x + moe, ln2_g[l], ln2_b[l])
    return x
```

```python
import functools
import math

import jax
import jax.numpy as jnp
from jax import lax
from jax.experimental import pallas as pl
from jax.experimental.pallas import tpu as pltpu

F32 = jnp.float32
BF16 = jnp.bfloat16
HI = lax.Precision.HIGHEST

D_MODEL = 1024
DEPTH = 2
W_MIX = 256
GLA_HEADS, GLA_DK, GLA_DV, GLA_RANK, GLA_TAU, GLA_CHUNK = 4, 32, 64, 16, 16.0, 64
SSD_HEADS, SSD_GROUPS, SSD_HEADDIM, SSD_STATE, SSD_CONV_K, SSD_CHUNK = 4, 2, 64, 128, 4, 128
DIFF_HEADS, DIFF_DQK, DIFF_DV = 4, 32, 64
N_GROUPS, EXPERTS_PER_GROUP, N_EXPERTS, D_EXPERT = 4, 8, 32, 512
ALPHA = (2 * DEPTH) ** 0.25
LN_EPS = 1e-5
RMS_EPS = 1e-6

LANES = 128
PROJ_WIDTHS = (768, 896, 1152, 768)
VMEM_LIMIT = 56 * 1024 * 1024

MOE_BLK = 256
TOK_TILE = 256


def _cparams(sem):
    return pltpu.CompilerParams(dimension_semantics=sem, vmem_limit_bytes=VMEM_LIMIT)


def _sigmoid(x):
    return 1.0 / (1.0 + jnp.exp(-x))


def _softplus(x):
    return jnp.maximum(x, 0.0) + jnp.log(1.0 + jnp.exp(-jnp.abs(x)))


def _layer_norm(h, g, b):
    mu = jnp.mean(h, axis=-1, keepdims=True)
    d = h - mu
    var = jnp.mean(d * d, axis=-1, keepdims=True)
    return d * lax.rsqrt(var + LN_EPS) * g + b


def _dot_nt(a, b):
    return lax.dot_general(a, b, (((1,), (1,)), ((), ())), preferred_element_type=F32)


def _dot_tn(a, b, precision=None):
    return lax.dot_general(a, b, (((0,), (0,)), ((), ())), preferred_element_type=F32,
                           precision=precision)


def _proj_kernel(x_ref, w_ref, oc_ref, og_ref, os_ref, od_ref):
    xb = x_ref[...].astype(BF16)
    off = 0
    for o_ref in (oc_ref, og_ref, os_ref, od_ref):
        n = o_ref.shape[-1]
        o_ref[...] = jnp.dot(xb, w_ref[:, off:off + n], preferred_element_type=F32)
        off += n


def _in_proj(x2d, w_r):
    t = x2d.shape[0]
    tm = 512
    ncol = sum(PROJ_WIDTHS)
    return pl.pallas_call(
        _proj_kernel,
        grid=(t // tm,),
        in_specs=[pl.BlockSpec((tm, D_MODEL), lambda i: (i, 0)),
                  pl.BlockSpec((D_MODEL, ncol), lambda i: (0, 0))],
        out_specs=[pl.BlockSpec((tm, n), lambda i: (i, 0)) for n in PROJ_WIDTHS],
        out_shape=[jax.ShapeDtypeStruct((t, n), F32) for n in PROJ_WIDTHS],
        compiler_params=_cparams(("parallel",)),
        name="in_proj",
    )(x2d, w_r)


def _conv_kernel(p_ref, w_ref, o_ref):
    u = p_ref[0, :, 0:W_MIX]
    gb = p_ref[0, :, W_MIX:2 * W_MIX]
    gc = p_ref[0, :, 2 * W_MIX:3 * W_MIX]
    cu = gc * u
    row = lax.broadcasted_iota(jnp.int32, cu.shape, 0)
    acc = cu * w_ref[2:3, :]
    for s in (1, 2):
        sh = jnp.where(row >= s, pltpu.roll(cu, s, axis=0), 0.0)
        acc = acc + sh * w_ref[2 - s:3 - s, :]
    o_ref[0] = (gb * acc).astype(o_ref.dtype)


def _conv_mixer(pc, conv_w):
    b, s, _ = pc.shape
    return pl.pallas_call(
        _conv_kernel,
        grid=(b,),
        in_specs=[pl.BlockSpec((1, s, 3 * W_MIX), lambda i: (i, 0, 0)),
                  pl.BlockSpec((3, W_MIX), lambda i: (0, 0))],
        out_specs=pl.BlockSpec((1, s, W_MIX), lambda i: (i, 0, 0)),
        out_shape=jax.ShapeDtypeStruct((b, s, W_MIX), BF16),
        compiler_params=_cparams(("parallel",)),
        name="conv_mixer",
    )(pc, conv_w)


def _gla_kernel(p_ref, wlr_ref, blr_ref, ng_ref, o_ref, st_ref):
    c = GLA_CHUNK
    s_len = p_ref.shape[1]
    nh, dk, dv = GLA_HEADS, GLA_DK, GLA_DV
    st_ref[...] = jnp.zeros_like(st_ref)

    ri = lax.broadcasted_iota(jnp.int32, (c, c), 0)
    ci = lax.broadcasted_iota(jnp.int32, (c, c), 1)
    tri = (ci <= ri).astype(F32)
    klane_head = lax.broadcasted_iota(jnp.int32, (1, nh * dk), 1) // dk
    vlane_head = lax.broadcasted_iota(jnp.int32, (1, nh * dv), 1) // dv
    strow_head = lax.broadcasted_iota(jnp.int32, (nh * dv, 1), 0) // dv
    st_mask = strow_head == klane_head
    r4 = lax.broadcasted_iota(jnp.int32, (nh * c, c), 0) % c
    c4 = lax.broadcasted_iota(jnp.int32, (nh * c, c), 1)
    causal4 = c4 <= r4
    gi = lax.broadcasted_iota(jnp.int32, (nh * dv, nh * dv), 0) // dv
    gj = lax.broadcasted_iota(jnp.int32, (nh * dv, nh * dv), 1) // dv
    gmean = jnp.where(gi == gj, 1.0 / dv, 0.0).astype(F32)

    def body(n, carry):
        r0 = pl.multiple_of(n * c, c)
        rows = pl.ds(r0, c)
        q = p_ref[0, rows, 0:128] * (dk ** -0.5)
        k = p_ref[0, rows, 128:256]
        v = p_ref[0, rows, 256:512]
        g = p_ref[0, rows, 512:768]
        lr = p_ref[0, rows, 768:896]
        z = jnp.dot(lr, wlr_ref[...], precision=HI, preferred_element_type=F32) + blr_ref[...]
        log_a = (jnp.minimum(z, 0.0) - jnp.log(1.0 + jnp.exp(-jnp.abs(z)))) * (1.0 / GLA_TAU)
        cum = jnp.dot(tri, log_a, precision=HI, preferred_element_type=F32)
        cum_last = cum[c - 1:c, :]
        q_dec = q * jnp.exp(cum)
        k_inv = k * jnp.exp(-cum)
        k_end = k * jnp.exp(cum_last - cum)
        vb = v.astype(BF16)
        qs = jnp.concatenate([jnp.where(klane_head == h, q_dec, 0.0) for h in range(nh)],
                             axis=0).astype(BF16)
        att = jnp.where(causal4, _dot_nt(qs, k_inv.astype(BF16)), 0.0)
        r = jnp.dot(att.astype(BF16), vb, preferred_element_type=F32)
        o = jnp.where(vlane_head == 0, r[0:c], 0.0)
        for h in range(1, nh):
            o = o + jnp.where(vlane_head == h, r[h * c:(h + 1) * c], 0.0)
        st = st_ref[...]
        o = o + _dot_nt(q_dec.astype(BF16), st.astype(BF16))
        d_st = _dot_tn(vb, k_end.astype(BF16))
        st_ref[...] = st * jnp.exp(cum_last) + jnp.where(st_mask, d_st, 0.0)
        ms = jnp.dot(o * o, gmean, precision=HI, preferred_element_type=F32)
        o = o * lax.rsqrt(ms + RMS_EPS) * ng_ref[...]
        o_ref[0, rows, :] = (o * (g * _sigmoid(g))).astype(o_ref.dtype)
        return carry

    lax.fori_loop(0, s_len // c, body, 0)


def _gla_mixer(pg, w_lr_pad, b_lr, norm_g4):
    b, s, wp = pg.shape
    return pl.pallas_call(
        _gla_kernel,
        grid=(b,),
        in_specs=[pl.BlockSpec((1, s, wp), lambda i: (i, 0, 0)),
                  pl.BlockSpec((LANES, LANES), lambda i: (0, 0)),
                  pl.BlockSpec((1, LANES), lambda i: (0, 0)),
                  pl.BlockSpec((1, W_MIX), lambda i: (0, 0))],
        out_specs=pl.BlockSpec((1, s, W_MIX), lambda i: (i, 0, 0)),
        out_shape=jax.ShapeDtypeStruct((b, s, W_MIX), BF16),
        scratch_shapes=[pltpu.VMEM((GLA_HEADS * GLA_DV, GLA_HEADS * GLA_DK), F32)],
        compiler_params=_cparams(("parallel",)),
        name="gla_mixer",
    )(pg, w_lr_pad, b_lr, norm_g4)


def _ssd_kernel(p_ref, cw_ref, cb_ref, alog_ref, dtb_ref, dsk_ref, ng_ref, o_ref, st_ref):
    c = SSD_CHUNK
    s_len = p_ref.shape[1]
    n_st = SSD_STATE
    st_ref[...] = jnp.zeros_like(st_ref)

    ri = lax.broadcasted_iota(jnp.int32, (c, c), 0)
    ci = lax.broadcasted_iota(jnp.int32, (c, c), 1)
    causal = ci <= ri
    tri = causal.astype(F32)
    upper = (ri <= ci).astype(F32)
    lane_head = lax.broadcasted_iota(jnp.int32, (1, W_MIX), 1) // SSD_HEADDIM
    lane_group = lane_head // (SSD_HEADS // SSD_GROUPS)
    eh = lax.broadcasted_iota(jnp.int32, (LANES, W_MIX), 0)
    el = lax.broadcasted_iota(jnp.int32, (LANES, W_MIX), 1) // SSD_HEADDIM
    expand = (eh == el).astype(F32)
    row8 = lax.broadcasted_iota(jnp.int32, (8, 3 * W_MIX), 0)
    a_c = -jnp.exp(alog_ref[...])

    def body(n, carry):
        r0 = pl.multiple_of(n * c, c)
        rows = pl.ds(r0, c)
        cur = p_ref[0, rows, 256:1024]
        p0 = pl.multiple_of(jnp.maximum(r0 - 8, 0), 8)
        prev8 = p_ref[0, pl.ds(p0, 8), 256:1024]
        prev8 = jnp.where(n > 0, prev8, 0.0)
        acc = cur * cw_ref[3:4, :] + cb_ref[...]
        for s in (1, 2, 3):
            sh = pltpu.roll(cur, s, axis=0)
            top = jnp.where(row8 < s, pltpu.roll(prev8, s, axis=0), sh[0:8])
            sh = jnp.concatenate([top, sh[8:]], axis=0)
            acc = acc + sh * cw_ref[3 - s:4 - s, :]
        xbc = acc * _sigmoid(acc)
        x = xbc[:, 0:256]
        bm = xbc[:, 256:512].astype(BF16)
        cm = xbc[:, 512:768].astype(BF16)

        dt_c = _softplus(p_ref[0, rows, 1024:1152] + dtb_ref[...])
        da_c = dt_c * a_c
        dt_x = jnp.dot(dt_c, expand, precision=HI, preferred_element_type=F32)
        da_x = jnp.dot(da_c, expand, precision=HI, preferred_element_type=F32)
        cum_x = jnp.dot(tri, da_x, precision=HI, preferred_element_type=F32)
        cum_c = jnp.dot(tri, da_c, precision=HI, preferred_element_type=F32)
        cum_r = _dot_tn(da_c, upper, precision=HI)
        cl_x = cum_x[c - 1:c, :]
        x_dt = x * dt_x
        x_dt_b = x_dt.astype(BF16)
        xw_b = (x_dt * jnp.exp(cl_x - cum_x)).astype(BF16)

        y = x * dsk_ref[...]
        y_off = jnp.zeros((c, W_MIX), F32)
        for g in range(SSD_GROUPS):
            bg = bm[:, g * n_st:(g + 1) * n_st]
            cg = cm[:, g * n_st:(g + 1) * n_st]
            cb = _dot_nt(cg, bg)
            for r in range(SSD_HEADS // SSD_GROUPS):
                h = g * (SSD_HEADS // SSD_GROUPS) + r
                diff = cum_c[:, h:h + 1] - cum_r[h:h + 1, :]
                dec = jnp.exp(jnp.where(causal, diff, -jnp.inf))
                m = (cb * dec).astype(BF16)
                yh = jnp.dot(m, x_dt_b, preferred_element_type=F32)
                y = y + jnp.where(lane_head == h, yh, 0.0)
            st = st_ref[g]
            y_off = y_off + jnp.where(lane_group == g,
                                      jnp.dot(cg, st.astype(BF16), preferred_element_type=F32), 0.0)
            d_st = _dot_tn(bg, xw_b)
            st_ref[g] = st * jnp.exp(cl_x) + jnp.where(lane_group == g, d_st, 0.0)
        y = y + y_off * jnp.exp(cum_x)
        zg = p_ref[0, rows, 0:256]
        y = y * (zg * _sigmoid(zg))
        outs = []
        for g in range(SSD_GROUPS):
            yg = y[:, g * 128:(g + 1) * 128]
            ms = jnp.mean(yg * yg, axis=-1, keepdims=True)
            outs.append(yg * lax.rsqrt(ms + RMS_EPS))
        o_ref[0, rows, :] = (jnp.concatenate(outs, axis=-1) * ng_ref[...]).astype(o_ref.dtype)
        return carry

    lax.fori_loop(0, s_len // c, body, 0)


def _ssd_mixer(ps, conv_w, conv_b, a_log_c, dt_bias_c, d_x, norm_g):
    b, s, wp = ps.shape
    full2 = lambda i: (0, 0)
    return pl.pallas_call(
        _ssd_kernel,
        grid=(b,),
        in_specs=[pl.BlockSpec((1, s, wp), lambda i: (i, 0, 0)),
                  pl.BlockSpec((SSD_CONV_K, 3 * W_MIX), full2),
                  pl.BlockSpec((1, 3 * W_MIX), full2),
                  pl.BlockSpec((1, LANES), full2),
                  pl.BlockSpec((1, LANES), full2),
                  pl.BlockSpec((1, W_MIX), full2),
                  pl.BlockSpec((1, W_MIX), full2)],
        out_specs=pl.BlockSpec((1, s, W_MIX), lambda i: (i, 0, 0)),
        out_shape=jax.ShapeDtypeStruct((b, s, W_MIX), BF16),
        scratch_shapes=[pltpu.VMEM((SSD_GROUPS, SSD_STATE, W_MIX), F32)],
        compiler_params=_cparams(("parallel",)),
        name="ssd_mixer",
    )(ps, conv_w, conv_b, a_log_c, dt_bias_c, d_x, norm_g)


DIFF_TQ = 256
DIFF_TK = 512


def _diff_kernel(q_ref, k_ref, v_ref, lam_ref, ng_ref, o_ref, m_ref, l_ref, acc_ref, *, lam_init):
    tq, tk = DIFF_TQ, DIFF_TK
    nh = DIFF_HEADS
    nhc = 2 * nh
    i = pl.program_id(1)
    q = q_ref[0] * (DIFF_DQK ** -0.5)
    qlane = lax.broadcasted_iota(jnp.int32, (1, W_MIX), 1) // DIFF_DQK
    vlane_head = lax.broadcasted_iota(jnp.int32, (1, W_MIX), 1) // DIFF_DV
    qs = jnp.concatenate([jnp.where(qlane == hc, q, 0.0) for hc in range(nhc)],
                         axis=0).astype(BF16)
    m_ref[...] = jnp.full_like(m_ref, -jnp.inf)
    l_ref[...] = jnp.zeros_like(l_ref)
    acc_ref[...] = jnp.zeros_like(acc_ref)
    qpos = i * tq + lax.broadcasted_iota(jnp.int32, (nhc * tq, tk), 0) % tq
    kcol = lax.broadcasted_iota(jnp.int32, (nhc * tq, tk), 1)

    def expand(col, c):
        out = jnp.where(vlane_head == 0, col[c * tq:(c + 1) * tq], 0.0)
        for h in range(1, nh):
            hc = 2 * h + c
            out = out + jnp.where(vlane_head == h, col[hc * tq:(hc + 1) * tq], 0.0)
        return out

    def step(j, masked):
        k0 = pl.multiple_of(j * tk, tk)
        kt = k_ref[0, pl.ds(k0, tk), :].astype(BF16)
        vt = v_ref[0, pl.ds(k0, tk), :]
        s = _dot_nt(qs, kt)
        if masked:
            s = jnp.where(k0 + kcol <= qpos, s, -jnp.inf)
        m_prev = m_ref[...]
        m_new = jnp.maximum(m_prev, jnp.max(s, axis=-1, keepdims=True))
        alpha = jnp.exp(m_prev - m_new)
        p = jnp.exp(s - m_new)
        l_ref[...] = alpha * l_ref[...] + jnp.sum(p, axis=-1, keepdims=True)
        m_ref[...] = m_new
        pb = p.astype(BF16)
        vst = jnp.concatenate([jnp.where(vlane_head == h, vt, 0.0) for h in range(nh)],
                              axis=0).astype(BF16)
        for c in range(2):
            pc = jnp.concatenate([pb[(2 * h + c) * tq:(2 * h + c + 1) * tq] for h in range(nh)],
                                 axis=1)
            acc_ref[c] = acc_ref[c] * expand(alpha, c) + jnp.dot(pc, vst,
                                                                 preferred_element_type=F32)

    n_full = (i * tq) // tk

    def full_step(j, carry):
        step(j, False)
        return carry

    lax.fori_loop(0, n_full, full_step, 0)
    step(n_full, True)

    lam = (jnp.exp(jnp.sum(lam_ref[0:1, :] * lam_ref[1:2, :], axis=-1, keepdims=True))
           - jnp.exp(jnp.sum(lam_ref[2:3, :] * lam_ref[3:4, :], axis=-1, keepdims=True))
           + lam_init)
    l_all = l_ref[...]
    o = acc_ref[0] / expand(l_all, 0) - lam * (acc_ref[1] / expand(l_all, 1))
    gi = lax.broadcasted_iota(jnp.int32, (W_MIX, W_MIX), 0) // DIFF_DV
    gj = lax.broadcasted_iota(jnp.int32, (W_MIX, W_MIX), 1) // DIFF_DV
    gmean = jnp.where(gi == gj, 1.0 / DIFF_DV, 0.0).astype(F32)
    ms = jnp.dot(o * o, gmean, precision=HI, preferred_element_type=F32)
    o = o * lax.rsqrt(ms + RMS_EPS) * ng_ref[...] * (1.0 - lam_init)
    o_ref[0] = o.astype(o_ref.dtype)


def _diff_mixer(pd, lam_vecs, norm_g4, lam_init):
    b, s, _ = pd.shape
    tq = DIFF_TQ
    return pl.pallas_call(
        functools.partial(_diff_kernel, lam_init=lam_init),
        grid=(b, s // tq),
        in_specs=[pl.BlockSpec((1, tq, W_MIX), lambda bi, i: (bi, i, 0)),
                  pl.BlockSpec((1, s, W_MIX), lambda bi, i: (bi, 0, 1)),
                  pl.BlockSpec((1, s, W_MIX), lambda bi, i: (bi, 0, 2)),
                  pl.BlockSpec((4, LANES), lambda bi, i: (0, 0)),
                  pl.BlockSpec((1, W_MIX), lambda bi, i: (0, 0))],
        out_specs=pl.BlockSpec((1, tq, W_MIX), lambda bi, i: (bi, i, 0)),
        out_shape=jax.ShapeDtypeStruct((b, s, W_MIX), BF16),
        scratch_shapes=[pltpu.VMEM((2 * DIFF_HEADS * tq, 1), F32),
                        pltpu.VMEM((2 * DIFF_HEADS * tq, 1), F32),
                        pltpu.VMEM((2, tq, W_MIX), F32)],
        compiler_params=_cparams(("parallel", "parallel")),
        name="diff_attn",
    )(pd, pd, pd, lam_vecs, norm_g4)


def _oproj_kernel(yc_ref, yg_ref, ys_ref, yd_ref, x_ref, wo_ref, g_ref, b_ref, wr_ref,
                  xo_ref, route_ref):
    mix = jnp.concatenate([yc_ref[...], yg_ref[...], ys_ref[...], yd_ref[...]], axis=-1)
    h = ALPHA * x_ref[...] + jnp.dot(mix, wo_ref[...], preferred_element_type=F32)
    xn = _layer_norm(h, g_ref[...], b_ref[...])
    xo_ref[...] = xn

    logits = jnp.dot(xn, wr_ref[...], precision=HI, preferred_element_type=F32)
    lane = lax.broadcasted_iota(jnp.int32, logits.shape, 1).astype(F32)
    neg = -jnp.inf
    big = float(LANES)
    lg = jnp.where(lane < N_GROUPS, logits, neg)
    mg = jnp.max(lg, axis=-1, keepdims=True)
    sg = jnp.sum(jnp.exp(lg - mg), axis=-1, keepdims=True)
    grp = jnp.min(jnp.where(lg == mg, lane, big), axis=-1, keepdims=True)
    p_grp = 1.0 / sg
    lo = N_GROUPS + EXPERTS_PER_GROUP * grp
    in_g = jnp.logical_and(lane >= lo, lane < lo + EXPERTS_PER_GROUP)
    le = jnp.where(in_g, logits, neg)
    me = jnp.max(le, axis=-1, keepdims=True)
    ee = jnp.exp(le - me)
    pe = ee / jnp.sum(ee, axis=-1, keepdims=True)
    pe = jnp.where(in_g, pe, -1.0)
    p1 = jnp.max(pe, axis=-1, keepdims=True)
    i1 = jnp.min(jnp.where(pe == p1, lane, big), axis=-1, keepdims=True)
    pe2 = jnp.where(lane == i1, -1.0, pe)
    p2 = jnp.max(pe2, axis=-1, keepdims=True)
    i2 = jnp.min(jnp.where(pe2 == p2, lane, big), axis=-1, keepdims=True)
    den = p1 + p2
    g1 = p_grp * p1 / den
    g2 = p_grp * p2 / den
    e1 = i1 - N_GROUPS
    e2 = i2 - N_GROUPS
    route_ref[...] = jnp.where(lane == 0, e1, jnp.where(lane == 1, e2, jnp.where(
        lane == 2, g1, jnp.where(lane == 3, g2, 0.0))))


def _out_proj(ys, x2d, w_o, ln_g, ln_b, w_route):
    t = x2d.shape[0]
    tm = 512
    row = lambda i: (i, 0)
    full = lambda i: (0, 0)
    return pl.pallas_call(
        _oproj_kernel,
        grid=(t // tm,),
        in_specs=[pl.BlockSpec((tm, W_MIX), row)] * 4 + [
            pl.BlockSpec((tm, D_MODEL), row),
            pl.BlockSpec((D_MODEL, D_MODEL), full),
            pl.BlockSpec((1, D_MODEL), full),
            pl.BlockSpec((1, D_MODEL), full),
            pl.BlockSpec((D_MODEL, LANES), full)],
        out_specs=[pl.BlockSpec((tm, D_MODEL), row), pl.BlockSpec((tm, LANES), row)],
        out_shape=[jax.ShapeDtypeStruct((t, D_MODEL), F32),
                   jax.ShapeDtypeStruct((t, LANES), F32)],
        compiler_params=_cparams(("parallel",)),
        name="out_proj_ln_router",
    )(*ys, x2d, w_o, ln_g, ln_b, w_route)


def _scatter_kernel(dest_ref, x_ref, xs_in_ref, xs_ref, sem):
    del xs_in_ref
    tm = x_ref.shape[0]

    def issue(r, carry):
        for k in range(2):
            d = dest_ref[0, 0, 2 * r + k]
            pltpu.make_async_copy(x_ref.at[pl.ds(r, 1)], xs_ref.at[pl.ds(d, 1)], sem).start()
        return carry

    lax.fori_loop(0, tm, issue, 0)

    def drain(r, carry):
        for k in range(2):
            pltpu.make_async_copy(x_ref.at[pl.ds(0, 1)], xs_ref.at[pl.ds(0, 1)], sem).wait()
        return carry

    lax.fori_loop(0, tm, drain, 0)


def _dispatch(x2d, dest3, n_slots):
    t = x2d.shape[0]
    tm = TOK_TILE
    xs0 = jnp.zeros((n_slots, D_MODEL), F32)
    return pl.pallas_call(
        _scatter_kernel,
        grid=(t // tm,),
        in_specs=[pl.BlockSpec((1, 1, 2 * tm), lambda i: (i, 0, 0), memory_space=pltpu.SMEM),
                  pl.BlockSpec((tm, D_MODEL), lambda i: (i, 0)),
                  pl.BlockSpec(memory_space=pl.ANY)],
        out_specs=pl.BlockSpec(memory_space=pl.ANY),
        out_shape=jax.ShapeDtypeStruct((n_slots, D_MODEL), F32),
        scratch_shapes=[pltpu.SemaphoreType.DMA(())],
        input_output_aliases={2: 0},
        compiler_params=pltpu.CompilerParams(dimension_semantics=("arbitrary",),
                                             vmem_limit_bytes=VMEM_LIMIT,
                                             has_side_effects=True),
        name="moe_dispatch",
    )(dest3, x2d, xs0)


def _ffn_kernel(be_ref, nu_ref, xs_ref, wg_ref, wu_ref, wd_ref, y_ref):
    del be_ref
    i = pl.program_id(0)

    @pl.when(i < nu_ref[0])
    def _():
        xb = xs_ref[...].astype(BF16)
        a = jnp.dot(xb, wg_ref[0], preferred_element_type=F32)
        u = jnp.dot(xb, wu_ref[0], preferred_element_type=F32)
        h = (a * _sigmoid(a) * u).astype(BF16)
        y_ref[...] = jnp.dot(h, wd_ref[0], preferred_element_type=F32)

    @pl.when(i >= nu_ref[0])
    def _():
        y_ref[...] = jnp.zeros_like(y_ref)


def _expert_ffn(xs, block_e, n_used, w_gate, w_up, w_down):
    n_slots = xs.shape[0]
    blk = MOE_BLK
    grid_spec = pltpu.PrefetchScalarGridSpec(
        num_scalar_prefetch=2,
        grid=(n_slots // blk,),
        in_specs=[pl.BlockSpec((blk, D_MODEL), lambda i, be, nu: (i, 0)),
                  pl.BlockSpec((1, D_MODEL, D_EXPERT), lambda i, be, nu: (be[i], 0, 0)),
                  pl.BlockSpec((1, D_MODEL, D_EXPERT), lambda i, be, nu: (be[i], 0, 0)),
                  pl.BlockSpec((1, D_EXPERT, D_MODEL), lambda i, be, nu: (be[i], 0, 0))],
        out_specs=pl.BlockSpec((blk, D_MODEL), lambda i, be, nu: (i, 0)),
    )
    return pl.pallas_call(
        _ffn_kernel,
        grid_spec=grid_spec,
        out_shape=jax.ShapeDtypeStruct((n_slots, D_MODEL), F32),
        compiler_params=_cparams(("arbitrary",)),
        name="expert_ffn",
    )(block_e, n_used, xs, w_gate, w_up, w_down)


def _combine_kernel(dest_ref, route_ref, x_ref, y_ref, g_ref, b_ref, o_ref, buf_ref, sem):
    tm = x_ref.shape[0]

    def issue(r, carry):
        for k in range(2):
            d = dest_ref[0, 0, 2 * r + k]
            pltpu.make_async_copy(y_ref.at[pl.ds(d, 1)], buf_ref.at[k, pl.ds(r, 1)], sem).start()
        return carry

    lax.fori_loop(0, tm, issue, 0)

    def drain(r, carry):
        for k in range(2):
            pltpu.make_async_copy(y_ref.at[pl.ds(0, 1)], buf_ref.at[0, pl.ds(0, 1)], sem).wait()
        return carry

    lax.fori_loop(0, tm, drain, 0)
    moe = route_ref[:, 2:3] * buf_ref[0] + route_ref[:, 3:4] * buf_ref[1]
    h = ALPHA * x_ref[...] + moe
    o_ref[...] = _layer_norm(h, g_ref[...], b_ref[...])


def _combine(dest3, route, x2d, y, ln_g, ln_b):
    t = x2d.shape[0]
    tm = TOK_TILE
    row = lambda i: (i, 0)
    full = lambda i: (0, 0)
    return pl.pallas_call(
        _combine_kernel,
        grid=(t // tm,),
        in_specs=[pl.BlockSpec((1, 1, 2 * tm), lambda i: (i, 0, 0), memory_space=pltpu.SMEM),
                  pl.BlockSpec((tm, LANES), row),
                  pl.BlockSpec((tm, D_MODEL), row),
                  pl.BlockSpec(memory_space=pl.ANY),
                  pl.BlockSpec((1, D_MODEL), full),
                  pl.BlockSpec((1, D_MODEL), full)],
        out_specs=pl.BlockSpec((tm, D_MODEL), row),
        out_shape=jax.ShapeDtypeStruct((t, D_MODEL), F32),
        scratch_shapes=[pltpu.VMEM((2, tm, D_MODEL), F32), pltpu.SemaphoreType.DMA(())],
        compiler_params=_cparams(("arbitrary",)),
        name="moe_combine",
    )(dest3, route, x2d, y, ln_g, ln_b)


def _dispatch_plan(route, n_blocks):
    t = route.shape[0]
    blk = MOE_BLK
    flat_e = route[:, 0:2].astype(jnp.int32).reshape(2 * t)
    onehot = (flat_e[:, None] == jnp.arange(N_EXPERTS, dtype=jnp.int32)[None, :]).astype(jnp.int32)
    csum = jnp.cumsum(onehot, axis=0)
    rank = jnp.sum(csum * onehot, axis=1) - 1
    counts = csum[-1]
    padded = (counts + blk - 1) // blk * blk
    pad_end = jnp.cumsum(padded)
    pad_start = pad_end - padded
    dest = pad_start[flat_e] + rank
    block_e = jnp.minimum(
        jnp.searchsorted(pad_end, jnp.arange(n_blocks, dtype=jnp.int32) * blk, side='right'),
        N_EXPERTS - 1).astype(jnp.int32)
    n_used = (pad_end[-1] // blk).astype(jnp.int32).reshape(1)
    return dest.astype(jnp.int32).reshape(t // TOK_TILE, 1, 2 * TOK_TILE), block_e, n_used


def _pad_cols(w, n):
    return jnp.pad(w, ((0, 0), (0, n - w.shape[1])))


def kernel(x, w_in, conv_w, gla_w_lr, gla_b_lr, gla_norm_g, ssd_conv_w, ssd_conv_b, ssd_a_log,
           ssd_d, ssd_dt_bias, ssd_norm_g, diff_lq1, diff_lk1, diff_lq2, diff_lk2, diff_norm_g,
           w_o, ln1_g, ln1_b, router_g, router_e, w_gate, w_up, w_down, ln2_g, ln2_b):
    bsz, seq, d = x.shape
    t = bsz * seq
    n_assign = 2 * t
    n_blocks = (n_assign + N_EXPERTS * (MOE_BLK - 1)) // MOE_BLK + 1
    n_slots = n_blocks * MOE_BLK
    x2d = x.reshape(t, d)
    for l in range(DEPTH):
        w = w_in[l]
        w_r = jnp.concatenate([w[:, 0:768], _pad_cols(w[:, 768:1552], 896),
                               _pad_cols(w[:, 1552:2580], 1152), w[:, 2580:3348]],
                              axis=1).astype(BF16)
        pc, pg, ps, pd = _in_proj(x2d, w_r)

        y_conv = _conv_mixer(pc.reshape(bsz, seq, -1), conv_w[l])
        w_lr_pad = jnp.pad(gla_w_lr[l], ((0, LANES - GLA_RANK), (0, 0)))
        y_gla = _gla_mixer(pg.reshape(bsz, seq, -1), w_lr_pad, gla_b_lr[l].reshape(1, -1),
                           jnp.tile(gla_norm_g[l], GLA_HEADS).reshape(1, -1))
        pad4 = lambda v: jnp.pad(v, (0, LANES - SSD_HEADS)).reshape(1, LANES)
        y_ssd = _ssd_mixer(ps.reshape(bsz, seq, -1), ssd_conv_w[l], ssd_conv_b[l].reshape(1, -1),
                           pad4(ssd_a_log[l]), pad4(ssd_dt_bias[l]),
                           jnp.repeat(ssd_d[l], SSD_HEADDIM).reshape(1, -1),
                           ssd_norm_g[l].reshape(1, -1))
        lam_vecs = jnp.pad(jnp.stack([diff_lq1[l], diff_lk1[l], diff_lq2[l], diff_lk2[l]]),
                           ((0, 0), (0, LANES - DIFF_DQK)))
        lam_init = 0.8 - 0.6 * math.exp(-0.3 * l)
        y_diff = _diff_mixer(pd.reshape(bsz, seq, -1), lam_vecs,
                             jnp.tile(diff_norm_g[l], DIFF_HEADS).reshape(1, -1), lam_init)

        w_route = _pad_cols(jnp.concatenate(
            [router_g[l], router_e[l].reshape(d, N_EXPERTS)], axis=1), LANES)
        ys = [y.reshape(t, W_MIX) for y in (y_conv, y_gla, y_ssd, y_diff)]
        xn, route = _out_proj(ys, x2d, w_o[l].astype(BF16), ln1_g[l].reshape(1, -1),
                              ln1_b[l].reshape(1, -1), w_route)

        dest3, block_e, n_used = _dispatch_plan(route, n_blocks)
        xs = _dispatch(xn, dest3, n_slots)
        y = _expert_ffn(xs, block_e, n_used, w_gate[l].astype(BF16), w_up[l].astype(BF16),
                        w_down[l].astype(BF16))
        x2d = _combine(dest3, route, xn, y, ln2_g[l].reshape(1, -1), ln2_b[l].reshape(1, -1))
    return x2d.reshape(bsz, seq, d)
```

```python
import functools
import math

import jax
import jax.numpy as jnp
from jax import lax
from jax.experimental import pallas as pl
from jax.experimental.pallas import tpu as pltpu

F32 = jnp.float32
BF16 = jnp.bfloat16
HI = lax.Precision.HIGHEST

D_MODEL = 1024
DEPTH = 2
W_MIX = 256
GLA_HEADS, GLA_DK, GLA_DV, GLA_RANK, GLA_TAU, GLA_CHUNK = 4, 32, 64, 16, 16.0, 64
SSD_HEADS, SSD_GROUPS, SSD_HEADDIM, SSD_STATE, SSD_CONV_K, SSD_CHUNK = 4, 2, 64, 128, 4, 128
DIFF_HEADS, DIFF_DQK, DIFF_DV = 4, 32, 64
N_GROUPS, EXPERTS_PER_GROUP, N_EXPERTS, D_EXPERT = 4, 8, 32, 512
ALPHA = (2 * DEPTH) ** 0.25
LN_EPS = 1e-5
RMS_EPS = 1e-6

LANES = 128
PROJ_WIDTHS = (768, 896, 1152, 768)
VMEM_LIMIT = 56 * 1024 * 1024

MOE_BLK = 256
TOK_TILE = 256


def _cparams(sem):
    return pltpu.CompilerParams(dimension_semantics=sem, vmem_limit_bytes=VMEM_LIMIT)


def _sigmoid(x):
    return 1.0 / (1.0 + jnp.exp(-x))


def _softplus(x):
    return jnp.maximum(x, 0.0) + jnp.log(1.0 + jnp.exp(-jnp.abs(x)))


def _layer_norm(h, g, b):
    mu = jnp.mean(h, axis=-1, keepdims=True)
    d = h - mu
    var = jnp.mean(d * d, axis=-1, keepdims=True)
    return d * lax.rsqrt(var + LN_EPS) * g + b


def _dot_nt(a, b):
    return lax.dot_general(a, b, (((1,), (1,)), ((), ())), preferred_element_type=F32)


def _dot_tn(a, b, precision=None):
    return lax.dot_general(a, b, (((0,), (0,)), ((), ())), preferred_element_type=F32,
                           precision=precision)


def _proj_kernel(x_ref, w_ref, oc_ref, og_ref, os_ref, od_ref):
    xb = x_ref[...].astype(BF16)
    off = 0
    for o_ref in (oc_ref, og_ref, os_ref, od_ref):
        n = o_ref.shape[-1]
        o_ref[...] = jnp.dot(xb, w_ref[:, off:off + n], preferred_element_type=F32)
        off += n


def _in_proj(x2d, w_r, layer):
    t = x2d.shape[0]
    tm = 512
    ncol = sum(PROJ_WIDTHS)
    return pl.pallas_call(
        _proj_kernel,
        grid=(t // tm,),
        in_specs=[pl.BlockSpec((tm, D_MODEL), lambda i: (i, 0)),
                  pl.BlockSpec((None, D_MODEL, ncol), lambda i: (layer, 0, 0))],
        out_specs=[pl.BlockSpec((tm, n), lambda i: (i, 0)) for n in PROJ_WIDTHS],
        out_shape=[jax.ShapeDtypeStruct((t, n), F32) for n in PROJ_WIDTHS],
        compiler_params=_cparams(("parallel",)),
        name="in_proj",
    )(x2d, w_r)


def _conv_kernel(p_ref, w_ref, o_ref):
    u = p_ref[0, :, 0:W_MIX]
    gb = p_ref[0, :, W_MIX:2 * W_MIX]
    gc = p_ref[0, :, 2 * W_MIX:3 * W_MIX]
    cu = gc * u
    row = lax.broadcasted_iota(jnp.int32, cu.shape, 0)
    acc = cu * w_ref[2:3, :]
    for s in (1, 2):
        sh = jnp.where(row >= s, pltpu.roll(cu, s, axis=0), 0.0)
        acc = acc + sh * w_ref[2 - s:3 - s, :]
    o_ref[0] = (gb * acc).astype(o_ref.dtype)


def _conv_mixer(pc, conv_w):
    b, s, _ = pc.shape
    return pl.pallas_call(
        _conv_kernel,
        grid=(b,),
        in_specs=[pl.BlockSpec((1, s, 3 * W_MIX), lambda i: (i, 0, 0)),
                  pl.BlockSpec((3, W_MIX), lambda i: (0, 0))],
        out_specs=pl.BlockSpec((1, s, W_MIX), lambda i: (i, 0, 0)),
        out_shape=jax.ShapeDtypeStruct((b, s, W_MIX), BF16),
        compiler_params=_cparams(("parallel",)),
        name="conv_mixer",
    )(pc, conv_w)


def _gla_kernel(p_ref, wlr_ref, blr_ref, ng_ref, o_ref, st_ref):
    c = GLA_CHUNK
    s_len = p_ref.shape[1]
    nh, dk, dv = GLA_HEADS, GLA_DK, GLA_DV
    st_ref[...] = jnp.zeros_like(st_ref)

    ri = lax.broadcasted_iota(jnp.int32, (c, c), 0)
    ci = lax.broadcasted_iota(jnp.int32, (c, c), 1)
    tri = (ci <= ri).astype(F32)
    klane_head = lax.broadcasted_iota(jnp.int32, (1, nh * dk), 1) // dk
    vlane_head = lax.broadcasted_iota(jnp.int32, (1, nh * dv), 1) // dv
    strow_head = lax.broadcasted_iota(jnp.int32, (nh * dv, 1), 0) // dv
    st_mask = strow_head == klane_head
    r4 = lax.broadcasted_iota(jnp.int32, (nh * c, c), 0) % c
    c4 = lax.broadcasted_iota(jnp.int32, (nh * c, c), 1)
    causal4 = c4 <= r4
    gi = lax.broadcasted_iota(jnp.int32, (nh * dv, nh * dv), 0) // dv
    gj = lax.broadcasted_iota(jnp.int32, (nh * dv, nh * dv), 1) // dv
    gmean = jnp.where(gi == gj, 1.0 / dv, 0.0).astype(F32)

    def body(n, carry):
        r0 = pl.multiple_of(n * c, c)
        rows = pl.ds(r0, c)
        q = p_ref[0, rows, 0:128] * (dk ** -0.5)
        k = p_ref[0, rows, 128:256]
        v = p_ref[0, rows, 256:512]
        g = p_ref[0, rows, 512:768]
        lr = p_ref[0, rows, 768:896]
        z = jnp.dot(lr, wlr_ref[...], precision=HI, preferred_element_type=F32) + blr_ref[...]
        log_a = (jnp.minimum(z, 0.0) - jnp.log(1.0 + jnp.exp(-jnp.abs(z)))) * (1.0 / GLA_TAU)
        cum = jnp.dot(tri, log_a, precision=HI, preferred_element_type=F32)
        cum_last = cum[c - 1:c, :]
        q_dec = q * jnp.exp(cum)
        k_inv = k * jnp.exp(-cum)
        k_end = k * jnp.exp(cum_last - cum)
        vb = v.astype(BF16)
        qs = jnp.concatenate([jnp.where(klane_head == h, q_dec, 0.0) for h in range(nh)],
                             axis=0).astype(BF16)
        att = jnp.where(causal4, _dot_nt(qs, k_inv.astype(BF16)), 0.0)
        r = jnp.dot(att.astype(BF16), vb, preferred_element_type=F32)
        o = jnp.where(vlane_head == 0, r[0:c], 0.0)
        for h in range(1, nh):
            o = o + jnp.where(vlane_head == h, r[h * c:(h + 1) * c], 0.0)
        st = st_ref[...]
        o = o + _dot_nt(q_dec.astype(BF16), st.astype(BF16))
        d_st = _dot_tn(vb, k_end.astype(BF16))
        st_ref[...] = st * jnp.exp(cum_last) + jnp.where(st_mask, d_st, 0.0)
        ms = jnp.dot(o * o, gmean, precision=HI, preferred_element_type=F32)
        o = o * lax.rsqrt(ms + RMS_EPS) * ng_ref[...]
        o_ref[0, rows, :] = (o * (g * _sigmoid(g))).astype(o_ref.dtype)
        return carry

    lax.fori_loop(0, s_len // c, body, 0)


def _gla_mixer(pg, w_lr_pad, b_lr, norm_g4):
    b, s, wp = pg.shape
    return pl.pallas_call(
        _gla_kernel,
        grid=(b,),
        in_specs=[pl.BlockSpec((1, s, wp), lambda i: (i, 0, 0)),
                  pl.BlockSpec((LANES, LANES), lambda i: (0, 0)),
                  pl.BlockSpec((1, LANES), lambda i: (0, 0)),
                  pl.BlockSpec((1, W_MIX), lambda i: (0, 0))],
        out_specs=pl.BlockSpec((1, s, W_MIX), lambda i: (i, 0, 0)),
        out_shape=jax.ShapeDtypeStruct((b, s, W_MIX), BF16),
        scratch_shapes=[pltpu.VMEM((GLA_HEADS * GLA_DV, GLA_HEADS * GLA_DK), F32)],
        compiler_params=_cparams(("parallel",)),
        name="gla_mixer",
    )(pg, w_lr_pad, b_lr, norm_g4)


def _ssd_kernel(p_ref, cw_ref, cb_ref, alog_ref, dtb_ref, dsk_ref, ng_ref, o_ref, st_ref):
    c = SSD_CHUNK
    s_len = p_ref.shape[1]
    n_st = SSD_STATE
    st_ref[...] = jnp.zeros_like(st_ref)

    ri = lax.broadcasted_iota(jnp.int32, (c, c), 0)
    ci = lax.broadcasted_iota(jnp.int32, (c, c), 1)
    causal = ci <= ri
    tri = causal.astype(F32)
    upper = (ri <= ci).astype(F32)
    lane_head = lax.broadcasted_iota(jnp.int32, (1, W_MIX), 1) // SSD_HEADDIM
    lane_group = lane_head // (SSD_HEADS // SSD_GROUPS)
    eh = lax.broadcasted_iota(jnp.int32, (LANES, W_MIX), 0)
    el = lax.broadcasted_iota(jnp.int32, (LANES, W_MIX), 1) // SSD_HEADDIM
    expand = (eh == el).astype(F32)
    row8 = lax.broadcasted_iota(jnp.int32, (8, 3 * W_MIX), 0)
    a_c = -jnp.exp(alog_ref[...])

    def body(n, carry):
        r0 = pl.multiple_of(n * c, c)
        rows = pl.ds(r0, c)
        cur = p_ref[0, rows, 256:1024]
        p0 = pl.multiple_of(jnp.maximum(r0 - 8, 0), 8)
        prev8 = p_ref[0, pl.ds(p0, 8), 256:1024]
        prev8 = jnp.where(n > 0, prev8, 0.0)
        acc = cur * cw_ref[3:4, :] + cb_ref[...]
        for s in (1, 2, 3):
            sh = pltpu.roll(cur, s, axis=0)
            top = jnp.where(row8 < s, pltpu.roll(prev8, s, axis=0), sh[0:8])
            sh = jnp.concatenate([top, sh[8:]], axis=0)
            acc = acc + sh * cw_ref[3 - s:4 - s, :]
        xbc = acc * _sigmoid(acc)
        x = xbc[:, 0:256]
        bm = xbc[:, 256:512].astype(BF16)
        cm = xbc[:, 512:768].astype(BF16)

        dt_c = _softplus(p_ref[0, rows, 1024:1152] + dtb_ref[...])
        da_c = dt_c * a_c
        dt_x = jnp.dot(dt_c, expand, precision=HI, preferred_element_type=F32)
        da_x = jnp.dot(da_c, expand, precision=HI, preferred_element_type=F32)
        cum_x = jnp.dot(tri, da_x, precision=HI, preferred_element_type=F32)
        cum_c = jnp.dot(tri, da_c, precision=HI, preferred_element_type=F32)
        cum_r = _dot_tn(da_c, upper, precision=HI)
        cl_x = cum_x[c - 1:c, :]
        x_dt = x * dt_x
        x_dt_b = x_dt.astype(BF16)
        xw_b = (x_dt * jnp.exp(cl_x - cum_x)).astype(BF16)

        y = x * dsk_ref[...]
        y_off = jnp.zeros((c, W_MIX), F32)
        for g in range(SSD_GROUPS):
            bg = bm[:, g * n_st:(g + 1) * n_st]
            cg = cm[:, g * n_st:(g + 1) * n_st]
            cb = _dot_nt(cg, bg)
            for r in range(SSD_HEADS // SSD_GROUPS):
                h = g * (SSD_HEADS // SSD_GROUPS) + r
                diff = cum_c[:, h:h + 1] - cum_r[h:h + 1, :]
                dec = jnp.exp(jnp.where(causal, diff, -jnp.inf))
                m = (cb * dec).astype(BF16)
                yh = jnp.dot(m, x_dt_b, preferred_element_type=F32)
                y = y + jnp.where(lane_head == h, yh, 0.0)
            st = st_ref[g]
            y_off = y_off + jnp.where(lane_group == g,
                                      jnp.dot(cg, st.astype(BF16), preferred_element_type=F32), 0.0)
            d_st = _dot_tn(bg, xw_b)
            st_ref[g] = st * jnp.exp(cl_x) + jnp.where(lane_group == g, d_st, 0.0)
        y = y + y_off * jnp.exp(cum_x)
        zg = p_ref[0, rows, 0:256]
        y = y * (zg * _sigmoid(zg))
        outs = []
        for g in range(SSD_GROUPS):
            yg = y[:, g * 128:(g + 1) * 128]
            ms = jnp.mean(yg * yg, axis=-1, keepdims=True)
            outs.append(yg * lax.rsqrt(ms + RMS_EPS))
        o_ref[0, rows, :] = (jnp.concatenate(outs, axis=-1) * ng_ref[...]).astype(o_ref.dtype)
        return carry

    lax.fori_loop(0, s_len // c, body, 0)


def _ssd_mixer(ps, conv_w, conv_b, a_log_c, dt_bias_c, d_x, norm_g):
    b, s, wp = ps.shape
    full2 = lambda i: (0, 0)
    return pl.pallas_call(
        _ssd_kernel,
        grid=(b,),
        in_specs=[pl.BlockSpec((1, s, wp), lambda i: (i, 0, 0)),
                  pl.BlockSpec((SSD_CONV_K, 3 * W_MIX), full2),
                  pl.BlockSpec((1, 3 * W_MIX), full2),
                  pl.BlockSpec((1, LANES), full2),
                  pl.BlockSpec((1, LANES), full2),
                  pl.BlockSpec((1, W_MIX), full2),
                  pl.BlockSpec((1, W_MIX), full2)],
        out_specs=pl.BlockSpec((1, s, W_MIX), lambda i: (i, 0, 0)),
        out_shape=jax.ShapeDtypeStruct((b, s, W_MIX), BF16),
        scratch_shapes=[pltpu.VMEM((SSD_GROUPS, SSD_STATE, W_MIX), F32)],
        compiler_params=_cparams(("parallel",)),
        name="ssd_mixer",
    )(ps, conv_w, conv_b, a_log_c, dt_bias_c, d_x, norm_g)


DIFF_TQ = 256
DIFF_TK = 256
LOG2E = 1.4426950408889634


def _diff_kernel(q_ref, k_ref, v_ref, lam_ref, ng_ref, o_ref,
                 kb_ref, vt_ref, qs_ref, m_ref, l_ref, acc_ref, *, lam_init):
    tq, tk = DIFF_TQ, DIFF_TK
    nh, dv = DIFF_HEADS, DIFF_DV
    nhc = 2 * nh
    s_len = k_ref.shape[1]
    i = pl.program_id(1)

    @pl.when(i == 0)
    def _():
        kb_ref[...] = k_ref[0].astype(BF16)
        for cblk in range(s_len // tk):
            cols = slice(cblk * tk, (cblk + 1) * tk)
            vt_ref[:, cols] = v_ref[0, cols, :].T.astype(BF16)

    q = q_ref[0] * (DIFF_DQK ** -0.5 * LOG2E)
    qlane = lax.broadcasted_iota(jnp.int32, (1, W_MIX), 1) // DIFF_DQK
    for hc in range(nhc):
        qs_ref[hc * tq:(hc + 1) * tq, :] = jnp.where(qlane == hc, q, 0.0).astype(BF16)
    m_ref[...] = jnp.full_like(m_ref, -jnp.inf)
    l_ref[...] = jnp.zeros_like(l_ref)
    acc_ref[...] = jnp.zeros_like(acc_ref)
    krow = lax.broadcasted_iota(jnp.int32, (tk, nhc * tq), 0)
    qcol = lax.broadcasted_iota(jnp.int32, (tk, nhc * tq), 1) % tq
    diag_ok = krow <= qcol

    def step(j, masked):
        k0 = pl.multiple_of(j * tk, tk)
        st = _dot_nt(kb_ref[pl.ds(k0, tk), :], qs_ref[...])
        if masked:
            st = jnp.where(diag_ok, st, -jnp.inf)
        m_prev = m_ref[...]
        m_new = jnp.maximum(m_prev, jnp.max(st, axis=0, keepdims=True))
        alpha = jnp.exp2(m_prev - m_new)
        p = jnp.exp2(st - m_new)
        l_ref[...] = alpha * l_ref[...] + jnp.sum(p, axis=0, keepdims=True)
        m_ref[...] = m_new
        pb = p.astype(BF16)
        for hc in range(nhc):
            h = hc // 2
            lanes = slice(hc * tq, (hc + 1) * tq)
            pv = jnp.dot(vt_ref[h * dv:(h + 1) * dv, pl.ds(k0, tk)], pb[:, lanes],
                         preferred_element_type=F32)
            acc_ref[hc] = acc_ref[hc] * alpha[:, lanes] + pv

    def full_step(j, carry):
        step(j, False)
        return carry

    lax.fori_loop(0, i, full_step, 0)
    step(i, True)

    lam = (jnp.exp(jnp.sum(lam_ref[0:1, :] * lam_ref[1:2, :], axis=-1, keepdims=True))
           - jnp.exp(jnp.sum(lam_ref[2:3, :] * lam_ref[3:4, :], axis=-1, keepdims=True))
           + lam_init)
    heads = []
    for h in range(nh):
        o1 = acc_ref[2 * h] / l_ref[:, 2 * h * tq:(2 * h + 1) * tq]
        o2 = acc_ref[2 * h + 1] / l_ref[:, (2 * h + 1) * tq:(2 * h + 2) * tq]
        oh = o1 - lam * o2
        ms = jnp.mean(oh * oh, axis=0, keepdims=True)
        heads.append(oh * lax.rsqrt(ms + RMS_EPS))
    o = jnp.concatenate(heads, axis=0).T
    o_ref[0] = (o * ng_ref[...] * (1.0 - lam_init)).astype(o_ref.dtype)


def _diff_mixer(pd, lam_vecs, norm_g4, lam_init):
    b, s, _ = pd.shape
    tq = DIFF_TQ
    return pl.pallas_call(
        functools.partial(_diff_kernel, lam_init=lam_init),
        grid=(b, s // tq),
        in_specs=[pl.BlockSpec((1, tq, W_MIX), lambda bi, i: (bi, i, 0)),
                  pl.BlockSpec((1, s, W_MIX), lambda bi, i: (bi, 0, 1)),
                  pl.BlockSpec((1, s, W_MIX), lambda bi, i: (bi, 0, 2)),
                  pl.BlockSpec((4, LANES), lambda bi, i: (0, 0)),
                  pl.BlockSpec((1, W_MIX), lambda bi, i: (0, 0))],
        out_specs=pl.BlockSpec((1, tq, W_MIX), lambda bi, i: (bi, i, 0)),
        out_shape=jax.ShapeDtypeStruct((b, s, W_MIX), BF16),
        scratch_shapes=[pltpu.VMEM((s, W_MIX), BF16),
                        pltpu.VMEM((W_MIX, s), BF16),
                        pltpu.VMEM((2 * DIFF_HEADS * tq, W_MIX), BF16),
                        pltpu.VMEM((1, 2 * DIFF_HEADS * tq), F32),
                        pltpu.VMEM((1, 2 * DIFF_HEADS * tq), F32),
                        pltpu.VMEM((2 * DIFF_HEADS, DIFF_DV, tq), F32)],
        compiler_params=_cparams(("parallel", "arbitrary")),
        name="diff_attn",
    )(pd, pd, pd, lam_vecs, norm_g4)


def _oproj_kernel(yc_ref, yg_ref, ys_ref, yd_ref, x_ref, wo_ref, g_ref, b_ref, wr_ref,
                  xo_ref, route_ref, cnt_ref):
    mix = jnp.concatenate([yc_ref[...], yg_ref[...], ys_ref[...], yd_ref[...]], axis=-1)
    h = ALPHA * x_ref[...] + jnp.dot(mix, wo_ref[...], preferred_element_type=F32)
    xn = _layer_norm(h, g_ref[...], b_ref[...])
    xo_ref[...] = xn

    logits = jnp.dot(xn, wr_ref[...], precision=HI, preferred_element_type=F32)
    lane = lax.broadcasted_iota(jnp.int32, logits.shape, 1).astype(F32)
    neg = -jnp.inf
    big = float(LANES)
    lg = jnp.where(lane < N_GROUPS, logits, neg)
    mg = jnp.max(lg, axis=-1, keepdims=True)
    sg = jnp.sum(jnp.exp(lg - mg), axis=-1, keepdims=True)
    grp = jnp.min(jnp.where(lg == mg, lane, big), axis=-1, keepdims=True)
    p_grp = 1.0 / sg
    lo = N_GROUPS + EXPERTS_PER_GROUP * grp
    in_g = jnp.logical_and(lane >= lo, lane < lo + EXPERTS_PER_GROUP)
    le = jnp.where(in_g, logits, neg)
    me = jnp.max(le, axis=-1, keepdims=True)
    ee = jnp.exp(le - me)
    pe = ee / jnp.sum(ee, axis=-1, keepdims=True)
    pe = jnp.where(in_g, pe, -1.0)
    p1 = jnp.max(pe, axis=-1, keepdims=True)
    i1 = jnp.min(jnp.where(pe == p1, lane, big), axis=-1, keepdims=True)
    pe2 = jnp.where(lane == i1, -1.0, pe)
    p2 = jnp.max(pe2, axis=-1, keepdims=True)
    i2 = jnp.min(jnp.where(pe2 == p2, lane, big), axis=-1, keepdims=True)
    den = p1 + p2
    g1 = p_grp * p1 / den
    g2 = p_grp * p2 / den
    e1 = i1 - N_GROUPS
    e2 = i2 - N_GROUPS
    route_ref[...] = jnp.where(lane == 0, e1, jnp.where(lane == 1, e2, jnp.where(
        lane == 2, g1, jnp.where(lane == 3, g2, 0.0))))

    @pl.when(pl.program_id(0) == 0)
    def _():
        cnt_ref[...] = jnp.zeros_like(cnt_ref)

    hits = jnp.where(lane == e1, 1.0, 0.0) + jnp.where(lane == e2, 1.0, 0.0)
    cnt_ref[...] += jnp.sum(hits, axis=0, keepdims=True)


def _out_proj(ys, x2d, w_o, layer, ln_g, ln_b, w_route):
    t = x2d.shape[0]
    tm = 512
    row = lambda i: (i, 0)
    full = lambda i: (0, 0)
    return pl.pallas_call(
        _oproj_kernel,
        grid=(t // tm,),
        in_specs=[pl.BlockSpec((tm, W_MIX), row)] * 4 + [
            pl.BlockSpec((tm, D_MODEL), row),
            pl.BlockSpec((None, D_MODEL, D_MODEL), lambda i: (layer, 0, 0)),
            pl.BlockSpec((1, D_MODEL), full),
            pl.BlockSpec((1, D_MODEL), full),
            pl.BlockSpec((D_MODEL, LANES), full)],
        out_specs=[pl.BlockSpec((tm, D_MODEL), row), pl.BlockSpec((tm, LANES), row),
                   pl.BlockSpec((1, LANES), full)],
        out_shape=[jax.ShapeDtypeStruct((t, D_MODEL), F32),
                   jax.ShapeDtypeStruct((t, LANES), F32),
                   jax.ShapeDtypeStruct((1, LANES), F32)],
        compiler_params=_cparams(("arbitrary",)),
        name="out_proj_ln_router",
    )(*ys, x2d, w_o, ln_g, ln_b, w_route)


PLAN_TILE = 512


def _plan_kernel(route_ref, cnt_ref, dest_ref, meta_ref, carry_ref, pstart_ref):
    tm = route_ref.shape[0]
    lane = lax.broadcasted_iota(jnp.int32, (1, LANES), 1).astype(F32)

    @pl.when(pl.program_id(0) == 0)
    def _():
        cnt = cnt_ref[...]
        padded = jnp.ceil(cnt * (1.0 / MOE_BLK)) * MOE_BLK
        li = lax.broadcasted_iota(jnp.int32, (LANES, LANES), 0)
        lj = lax.broadcasted_iota(jnp.int32, (LANES, LANES), 1)
        before = (li < lj).astype(F32)
        pstart = jnp.dot(jnp.broadcast_to(padded, (8, LANES)), before, precision=HI,
                         preferred_element_type=F32)[0:1]
        pstart_ref[...] = pstart
        carry_ref[...] = jnp.zeros_like(carry_ref)
        meta_ref[...] = jnp.concatenate(
            [pstart + padded, pstart, cnt, jnp.zeros((5, LANES), F32)], axis=0)

    oh0 = jnp.where(lane == route_ref[:, 0:1], 1.0, 0.0)
    oh1 = jnp.where(lane == route_ref[:, 1:2], 1.0, 0.0)
    both = oh0 + oh1
    ri = lax.broadcasted_iota(jnp.int32, (tm, tm), 0)
    ci = lax.broadcasted_iota(jnp.int32, (tm, tm), 1)
    earlier = (ci < ri).astype(BF16)
    base = (jnp.dot(earlier, both.astype(BF16), preferred_element_type=F32)
            + carry_ref[...] + pstart_ref[...])
    d0 = jnp.sum(oh0 * base, axis=-1, keepdims=True)
    d1 = jnp.sum(oh1 * base, axis=-1, keepdims=True)
    dest_ref[...] = jnp.where(lane == 0, d0, jnp.where(lane == 1, d1, 0.0))
    carry_ref[...] += jnp.sum(both, axis=0, keepdims=True)


def _dispatch_plan(route, cnt, n_blocks):
    t = route.shape[0]
    tm = PLAN_TILE
    blk = MOE_BLK
    dest, meta = pl.pallas_call(
        _plan_kernel,
        grid=(t // tm,),
        in_specs=[pl.BlockSpec((tm, LANES), lambda i: (i, 0)),
                  pl.BlockSpec((1, LANES), lambda i: (0, 0))],
        out_specs=[pl.BlockSpec((tm, LANES), lambda i: (i, 0)),
                   pl.BlockSpec((8, LANES), lambda i: (0, 0))],
        out_shape=[jax.ShapeDtypeStruct((t, LANES), F32),
                   jax.ShapeDtypeStruct((8, LANES), F32)],
        scratch_shapes=[pltpu.VMEM((1, LANES), F32), pltpu.VMEM((1, LANES), F32)],
        compiler_params=_cparams(("arbitrary",)),
        name="moe_plan",
    )(route, cnt)
    pad_end = meta[0, :N_EXPERTS].astype(jnp.int32)
    starts = jnp.arange(n_blocks, dtype=jnp.int32) * blk
    block_e = jnp.minimum(jnp.sum((pad_end[None, :] <= starts[:, None]).astype(jnp.int32), axis=1),
                          N_EXPERTS - 1)
    n_used = (pad_end[N_EXPERTS - 1] // blk).reshape(1)
    dest3 = dest[:, 0:2].astype(jnp.int32).reshape(t // TOK_TILE, 1, 2 * TOK_TILE)
    return dest3, block_e, n_used


def _scatter_kernel(dest_ref, x_ref, xs_in_ref, xs_ref, sem):
    del xs_in_ref
    tm = x_ref.shape[0]

    def issue(r, carry):
        for k in range(2):
            d = dest_ref[0, 0, 2 * r + k]
            pltpu.make_async_copy(x_ref.at[pl.ds(r, 1)], xs_ref.at[pl.ds(d, 1)], sem).start()
        return carry

    lax.fori_loop(0, tm, issue, 0, unroll=8)
    for k in range(2):
        pltpu.make_async_copy(x_ref, xs_ref.at[pl.ds(0, tm)], sem).wait()


def _dispatch(x2d, dest3, n_slots):
    t = x2d.shape[0]
    tm = TOK_TILE
    xs0 = jnp.zeros((n_slots, D_MODEL), F32)
    return pl.pallas_call(
        _scatter_kernel,
        grid=(t // tm,),
        in_specs=[pl.BlockSpec((1, 1, 2 * tm), lambda i: (i, 0, 0), memory_space=pltpu.SMEM),
                  pl.BlockSpec((tm, D_MODEL), lambda i: (i, 0)),
                  pl.BlockSpec(memory_space=pl.ANY)],
        out_specs=pl.BlockSpec(memory_space=pl.ANY),
        out_shape=jax.ShapeDtypeStruct((n_slots, D_MODEL), F32),
        scratch_shapes=[pltpu.SemaphoreType.DMA(())],
        input_output_aliases={2: 0},
        compiler_params=pltpu.CompilerParams(dimension_semantics=("arbitrary",),
                                             vmem_limit_bytes=VMEM_LIMIT,
                                             has_side_effects=True),
        name="moe_dispatch",
    )(dest3, x2d, xs0)


def _ffn_kernel(be_ref, nu_ref, xs_ref, wg_ref, wu_ref, wd_ref, y_ref, wgb_ref, wub_ref, wdb_ref):
    i = pl.program_id(0)
    prev = be_ref[jnp.maximum(i - 1, 0)]

    @pl.when(jnp.logical_or(i == 0, be_ref[i] != prev))
    def _():
        wgb_ref[...] = wg_ref[...].astype(BF16)
        wub_ref[...] = wu_ref[...].astype(BF16)
        wdb_ref[...] = wd_ref[...].astype(BF16)

    @pl.when(i < nu_ref[0])
    def _():
        xb = xs_ref[...].astype(BF16)
        a = jnp.dot(xb, wgb_ref[...], preferred_element_type=F32)
        u = jnp.dot(xb, wub_ref[...], preferred_element_type=F32)
        h = (a * _sigmoid(a) * u).astype(BF16)
        y_ref[...] = jnp.dot(h, wdb_ref[...], preferred_element_type=F32)

    @pl.when(i >= nu_ref[0])
    def _():
        y_ref[...] = jnp.zeros_like(y_ref)


def _expert_ffn(xs, block_e, n_used, w_gate, w_up, w_down, layer):
    n_slots = xs.shape[0]
    blk = MOE_BLK
    grid_spec = pltpu.PrefetchScalarGridSpec(
        num_scalar_prefetch=2,
        grid=(n_slots // blk,),
        in_specs=[pl.BlockSpec((blk, D_MODEL), lambda i, be, nu: (i, 0)),
                  pl.BlockSpec((None, None, D_MODEL, D_EXPERT),
                               lambda i, be, nu: (layer, be[i], 0, 0)),
                  pl.BlockSpec((None, None, D_MODEL, D_EXPERT),
                               lambda i, be, nu: (layer, be[i], 0, 0)),
                  pl.BlockSpec((None, None, D_EXPERT, D_MODEL),
                               lambda i, be, nu: (layer, be[i], 0, 0))],
        out_specs=pl.BlockSpec((blk, D_MODEL), lambda i, be, nu: (i, 0)),
        scratch_shapes=[pltpu.VMEM((D_MODEL, D_EXPERT), BF16),
                        pltpu.VMEM((D_MODEL, D_EXPERT), BF16),
                        pltpu.VMEM((D_EXPERT, D_MODEL), BF16)],
    )
    return pl.pallas_call(
        _ffn_kernel,
        grid_spec=grid_spec,
        out_shape=jax.ShapeDtypeStruct((n_slots, D_MODEL), F32),
        compiler_params=_cparams(("arbitrary",)),
        name="expert_ffn",
    )(block_e, n_used, xs, w_gate, w_up, w_down)


def _combine_kernel(dest_ref, route_ref, x_ref, y_ref, g_ref, b_ref, o_ref, buf_ref, sem):
    tm = x_ref.shape[0]

    def issue(r, carry):
        for k in range(2):
            d = dest_ref[0, 0, 2 * r + k]
            pltpu.make_async_copy(y_ref.at[pl.ds(d, 1)], buf_ref.at[k, pl.ds(r, 1)], sem).start()
        return carry

    lax.fori_loop(0, tm, issue, 0, unroll=8)
    for k in range(2):
        pltpu.make_async_copy(y_ref.at[pl.ds(0, tm)], buf_ref.at[k], sem).wait()
    moe =route_ref[:, 2:3] * buf_ref[0] + route_ref[:, 3:4] * buf_ref[1]
    h = ALPHA * x_ref[...] + moe
    o_ref[...] = _layer_norm(h, g_ref[...], b_ref[...])


def _combine(dest3, route, x2d, y, ln_g, ln_b):
    t = x2d.shape[0]
    tm = TOK_TILE
    row = lambda i: (i, 0)
    full = lambda i: (0, 0)
    return pl.pallas_call(
        _combine_kernel,
        grid=(t // tm,),
        in_specs=[pl.BlockSpec((1, 1, 2 * tm), lambda i: (i, 0, 0), memory_space=pltpu.SMEM),
                  pl.BlockSpec((tm, LANES), row),
                  pl.BlockSpec((tm, D_MODEL), row),
                  pl.BlockSpec(memory_space=pl.ANY),
                  pl.BlockSpec((1, D_MODEL), full),
                  pl.BlockSpec((1, D_MODEL), full)],
        out_specs=pl.BlockSpec((tm, D_MODEL), row),
        out_shape=jax.ShapeDtypeStruct((t, D_MODEL), F32),
        scratch_shapes=[pltpu.VMEM((2, tm, D_MODEL), F32), pltpu.SemaphoreType.DMA(())],
        compiler_params=_cparams(("arbitrary",)),
        name="moe_combine",
    )(dest3, route, x2d, y, ln_g, ln_b)


def _pad_cols(w, n):
    return jnp.pad(w, [(0, 0)] * (w.ndim - 1) + [(0, n - w.shape[-1])])


def kernel(x, w_in, conv_w, gla_w_lr, gla_b_lr, gla_norm_g, ssd_conv_w, ssd_conv_b, ssd_a_log,
           ssd_d, ssd_dt_bias, ssd_norm_g, diff_lq1, diff_lk1, diff_lq2, diff_lk2, diff_norm_g,
           w_o, ln1_g, ln1_b, router_g, router_e, w_gate, w_up, w_down, ln2_g, ln2_b):
    bsz, seq, d = x.shape
    t = bsz * seq
    n_assign = 2 * t
    n_blocks = (n_assign + N_EXPERTS * (MOE_BLK - 1)) // MOE_BLK + 1
    n_slots = n_blocks * MOE_BLK
    x2d = x.reshape(t, d)
    w_in_r = jnp.concatenate([w_in[..., 0:768], _pad_cols(w_in[..., 768:1552], 896),
                              _pad_cols(w_in[..., 1552:2580], 1152), w_in[..., 2580:3348]],
                             axis=-1).astype(BF16)
    w_o_b = w_o.astype(BF16)
    for l in range(DEPTH):
        pc, pg, ps, pd = _in_proj(x2d, w_in_r, l)

        y_conv = _conv_mixer(pc.reshape(bsz, seq, -1), conv_w[l])
        w_lr_pad = jnp.pad(gla_w_lr[l], ((0, LANES - GLA_RANK), (0, 0)))
        y_gla = _gla_mixer(pg.reshape(bsz, seq, -1), w_lr_pad, gla_b_lr[l].reshape(1, -1),
                           jnp.tile(gla_norm_g[l], GLA_HEADS).reshape(1, -1))
        pad4 = lambda v: jnp.pad(v, (0, LANES - SSD_HEADS)).reshape(1, LANES)
        y_ssd = _ssd_mixer(ps.reshape(bsz, seq, -1), ssd_conv_w[l], ssd_conv_b[l].reshape(1, -1),
                           pad4(ssd_a_log[l]), pad4(ssd_dt_bias[l]),
                           jnp.repeat(ssd_d[l], SSD_HEADDIM).reshape(1, -1),
                           ssd_norm_g[l].reshape(1, -1))
        lam_vecs = jnp.pad(jnp.stack([diff_lq1[l], diff_lk1[l], diff_lq2[l], diff_lk2[l]]),
                           ((0, 0), (0, LANES - DIFF_DQK)))
        lam_init = 0.8 - 0.6 * math.exp(-0.3 * l)
        y_diff = _diff_mixer(pd.reshape(bsz, seq, -1), lam_vecs,
                             jnp.tile(diff_norm_g[l], DIFF_HEADS).reshape(1, -1), lam_init)

        w_route = _pad_cols(jnp.concatenate(
            [router_g[l], router_e[l].reshape(d, N_EXPERTS)], axis=1), LANES)
        ys = [y.reshape(t, W_MIX) for y in (y_conv, y_gla, y_ssd, y_diff)]
        xn, route, cnt = _out_proj(ys, x2d, w_o_b, l, ln1_g[l].reshape(1, -1),
                                   ln1_b[l].reshape(1, -1), w_route)

        dest3, block_e, n_used = _dispatch_plan(route, cnt, n_blocks)
        xs = _dispatch(xn, dest3, n_slots)
        y = _expert_ffn(xs, block_e, n_used, w_gate, w_up, w_down, l)
        x2d = _combine(dest3, route, xn, y, ln2_g[l].reshape(1, -1), ln2_b[l].reshape(1, -1))
    return x2d.reshape(bsz, seq, d)
```

```python
import functools
import math

import jax
import jax.numpy as jnp
from jax import lax
from jax.experimental import pallas as pl
from jax.experimental.pallas import tpu as pltpu

F32 = jnp.float32
BF16 = jnp.bfloat16
HI = lax.Precision.HIGHEST

D_MODEL = 1024
DEPTH = 2
W_MIX = 256
GLA_HEADS, GLA_DK, GLA_DV, GLA_RANK, GLA_TAU, GLA_CHUNK = 4, 32, 64, 16, 16.0, 64
GLA_ROWS = 256
SSD_HEADS, SSD_GROUPS, SSD_HEADDIM, SSD_STATE, SSD_CONV_K, SSD_CHUNK = 4, 2, 64, 128, 4, 128
DIFF_HEADS, DIFF_DQK, DIFF_DV = 4, 32, 64
N_GROUPS, EXPERTS_PER_GROUP, N_EXPERTS, D_EXPERT = 4, 8, 32, 512
ALPHA = (2 * DEPTH) ** 0.25
LN_EPS = 1e-5
RMS_EPS = 1e-6

LANES = 128
PROJ_WIDTHS = (768, 896, 1152, 768)
VMEM_LIMIT = 56 * 1024 * 1024

MOE_BLK = 512
TOK_TILE = 256


def _cparams(sem):
    return pltpu.CompilerParams(dimension_semantics=sem, vmem_limit_bytes=VMEM_LIMIT)


def _sigmoid(x):
    return 1.0 / (1.0 + jnp.exp(-x))


def _softplus(x):
    return jnp.maximum(x, 0.0) + jnp.log(1.0 + jnp.exp(-jnp.abs(x)))


def _layer_norm(h, g, b):
    mu = jnp.mean(h, axis=-1, keepdims=True)
    d = h - mu
    var = jnp.mean(d * d, axis=-1, keepdims=True)
    return d * lax.rsqrt(var + LN_EPS) * g + b


def _dot_nt(a, b):
    return lax.dot_general(a, b, (((1,), (1,)), ((), ())), preferred_element_type=F32)


def _dot_tn(a, b, precision=None):
    return lax.dot_general(a, b, (((0,), (0,)), ((), ())), preferred_element_type=F32,
                           precision=precision)


def _split_bf16(x, parts):
    out = []
    for _ in range(parts - 1):
        hi = x.astype(BF16)
        out.append(hi)
        x = x - hi.astype(F32)
    out.append(x.astype(BF16))
    return out


def _dot(a, b):
    return jnp.dot(a, b, preferred_element_type=F32)


def _dot_split_lhs(a, b_exact, parts, dot=_dot):
    acc = None
    for term in _split_bf16(a, parts):
        d = dot(term, b_exact)
        acc = d if acc is None else acc + d
    return acc


def _dot_split_rhs(a_exact, b, parts):
    acc = None
    for term in _split_bf16(b, parts):
        d = jnp.dot(a_exact, term, preferred_element_type=F32)
        acc = d if acc is None else acc + d
    return acc


def _proj_kernel(x_ref, w_ref, oc_ref, og_ref, os_ref, od_ref):
    xb = x_ref[...].astype(BF16)
    off = 0
    for o_ref in (oc_ref, og_ref, os_ref, od_ref):
        n = o_ref.shape[-1]
        o_ref[...] = jnp.dot(xb, w_ref[:, off:off + n], preferred_element_type=F32)
        off += n


def _in_proj(x2d, w_r, layer):
    t = x2d.shape[0]
    tm = 512
    ncol = sum(PROJ_WIDTHS)
    return pl.pallas_call(
        _proj_kernel,
        grid=(t // tm,),
        in_specs=[pl.BlockSpec((tm, D_MODEL), lambda i: (i, 0)),
                  pl.BlockSpec((None, D_MODEL, ncol), lambda i: (layer, 0, 0))],
        out_specs=[pl.BlockSpec((tm, n), lambda i: (i, 0)) for n in PROJ_WIDTHS],
        out_shape=[jax.ShapeDtypeStruct((t, n), F32) for n in PROJ_WIDTHS],
        compiler_params=_cparams(("parallel",)),
        name="in_proj",
    )(x2d, w_r)


def _conv_kernel(p_ref, w_ref, o_ref):
    u = p_ref[0, :, 0:W_MIX]
    gb = p_ref[0, :, W_MIX:2 * W_MIX]
    gc = p_ref[0, :, 2 * W_MIX:3 * W_MIX]
    cu = gc * u
    row = lax.broadcasted_iota(jnp.int32, cu.shape, 0)
    acc = cu * w_ref[2:3, :]
    for s in (1, 2):
        sh = jnp.where(row >= s, pltpu.roll(cu, s, axis=0), 0.0)
        acc = acc + sh * w_ref[2 - s:3 - s, :]
    o_ref[0] = (gb * acc).astype(o_ref.dtype)


def _conv_mixer(pc, conv_w):
    b, s, _ = pc.shape
    return pl.pallas_call(
        _conv_kernel,
        grid=(b,),
        in_specs=[pl.BlockSpec((1, s, 3 * W_MIX), lambda i: (i, 0, 0)),
                  pl.BlockSpec((3, W_MIX), lambda i: (0, 0))],
        out_specs=pl.BlockSpec((1, s, W_MIX), lambda i: (i, 0, 0)),
        out_shape=jax.ShapeDtypeStruct((b, s, W_MIX), BF16),
        compiler_params=_cparams(("parallel",)),
        name="conv_mixer",
    )(pc, conv_w)


def _gla_kernel(p_ref, wlr_ref, blr_ref, ng_ref, o_ref, st_ref):
    c = GLA_CHUNK
    s_len = p_ref.shape[1]
    nh, dk, dv = GLA_HEADS, GLA_DK, GLA_DV
    st_ref[...] = jnp.zeros_like(st_ref)

    rb = GLA_ROWS
    ncb = rb // c
    ri = lax.broadcasted_iota(jnp.int32, (rb, rb), 0)
    ci = lax.broadcasted_iota(jnp.int32, (rb, rb), 1)
    tri = (ci <= ri).astype(BF16)
    klane_head = lax.broadcasted_iota(jnp.int32, (1, nh * dk), 1) // dk
    vlane_head = lax.broadcasted_iota(jnp.int32, (1, nh * dv), 1) // dv
    strow_head = lax.broadcasted_iota(jnp.int32, (nh * dv, 1), 0) // dv
    st_mask = strow_head == klane_head
    r4 = lax.broadcasted_iota(jnp.int32, (nh * c, c), 0) % c
    c4 = lax.broadcasted_iota(jnp.int32, (nh * c, c), 1)
    causal4 = c4 <= r4
    gi = lax.broadcasted_iota(jnp.int32, (nh * dv, nh * dv), 0) // dv
    gj = lax.broadcasted_iota(jnp.int32, (nh * dv, nh * dv), 1) // dv
    gmean = jnp.where(gi == gj, 1.0 / dv, 0.0).astype(BF16)
    wlr_hi, wlr_lo = _split_bf16(wlr_ref[...], 2)

    def body(n, carry):
        r0 = pl.multiple_of(n * rb, rb)
        rows = pl.ds(r0, rb)
        q = p_ref[0, rows, 0:128] * (dk ** -0.5)
        k = p_ref[0, rows, 128:256]
        vb = p_ref[0, rows, 256:512].astype(BF16)
        g = p_ref[0, rows, 512:768]
        lr = p_ref[0, rows, 768:896]
        lr_hi, lr_lo = _split_bf16(lr, 2)
        z = (jnp.dot(lr_hi, wlr_hi, preferred_element_type=F32)
             + jnp.dot(lr_hi, wlr_lo, preferred_element_type=F32)
             + jnp.dot(lr_lo, wlr_hi, preferred_element_type=F32)) + blr_ref[...]
        log_a = (jnp.minimum(z, 0.0) - jnp.log(1.0 + jnp.exp(-jnp.abs(z)))) * (1.0 / GLA_TAU)
        cumb = _dot_split_rhs(tri, log_a, 3)
        ends = [cumb[(j + 1) * c - 1:(j + 1) * c, :] for j in range(ncb)]
        starts = [jnp.zeros_like(ends[0])] + ends[:-1]
        cum = cumb - jnp.concatenate([jnp.broadcast_to(s0, (c, nh * dk)) for s0 in starts], axis=0)
        lasts = [e - s0 for e, s0 in zip(ends, starts)]
        cl = jnp.concatenate([jnp.broadcast_to(x, (c, nh * dk)) for x in lasts], axis=0)
        q_dec = q * jnp.exp(cum)
        k_inv = (k * jnp.exp(-cum)).astype(BF16)
        k_end = (k * jnp.exp(cl - cum)).astype(BF16)
        st = st_ref[...]
        outs = []
        for j in range(ncb):
            sl = slice(j * c, (j + 1) * c)
            qd = q_dec[sl]
            qs = jnp.concatenate([jnp.where(klane_head == h, qd, 0.0) for h in range(nh)],
                                 axis=0).astype(BF16)
            att = jnp.where(causal4, _dot_nt(qs, k_inv[sl]), 0.0)
            r = jnp.dot(att.astype(BF16), vb[sl], preferred_element_type=F32)
            o = jnp.where(vlane_head == 0, r[0:c], 0.0)
            for h in range(1, nh):
                o = o + jnp.where(vlane_head == h, r[h * c:(h + 1) * c], 0.0)
            outs.append(o + _dot_nt(qd.astype(BF16), st.astype(BF16)))
            d_st = _dot_tn(vb[sl], k_end[sl])
            st = st * jnp.exp(lasts[j]) + jnp.where(st_mask, d_st, 0.0)
        st_ref[...] = st
        o = jnp.concatenate(outs, axis=0)
        ms = _dot_split_lhs(o * o, gmean, 2)
        o = o * lax.rsqrt(ms + RMS_EPS) * ng_ref[...]
        o_ref[0, rows, :] = (o * (g * _sigmoid(g))).astype(o_ref.dtype)
        return carry

    lax.fori_loop(0, s_len // rb, body, 0)


def _gla_mixer(pg, w_lr_pad, b_lr, norm_g4):
    b, s, wp = pg.shape
    return pl.pallas_call(
        _gla_kernel,
        grid=(b,),
        in_specs=[pl.BlockSpec((1, s, wp), lambda i: (i, 0, 0)),
                  pl.BlockSpec((LANES, LANES), lambda i: (0, 0)),
                  pl.BlockSpec((1, LANES), lambda i: (0, 0)),
                  pl.BlockSpec((1, W_MIX), lambda i: (0, 0))],
        out_specs=pl.BlockSpec((1, s, W_MIX), lambda i: (i, 0, 0)),
        out_shape=jax.ShapeDtypeStruct((b, s, W_MIX), BF16),
        scratch_shapes=[pltpu.VMEM((GLA_HEADS * GLA_DV, GLA_HEADS * GLA_DK), F32)],
        compiler_params=_cparams(("parallel",)),
        name="gla_mixer",
    )(pg, w_lr_pad, b_lr, norm_g4)


def _ssd_kernel(p_ref, cw_ref, cb_ref, alog_ref, dtb_ref, dsk_ref, ng_ref, o_ref, st_ref):
    c = SSD_CHUNK
    s_len = p_ref.shape[1]
    n_st = SSD_STATE
    st_ref[...] = jnp.zeros_like(st_ref)

    ri = lax.broadcasted_iota(jnp.int32, (c, c), 0)
    ci = lax.broadcasted_iota(jnp.int32, (c, c), 1)
    causal = ci <= ri
    tri = causal.astype(BF16)
    upper = (ri <= ci).astype(BF16)
    lane_head = lax.broadcasted_iota(jnp.int32, (1, W_MIX), 1) // SSD_HEADDIM
    lane_group = lane_head // (SSD_HEADS // SSD_GROUPS)
    eh = lax.broadcasted_iota(jnp.int32, (LANES, W_MIX), 0)
    el = lax.broadcasted_iota(jnp.int32, (LANES, W_MIX), 1) // SSD_HEADDIM
    expand = (eh == el).astype(BF16)
    row8 = lax.broadcasted_iota(jnp.int32, (8, 3 * W_MIX), 0)
    a_c = -jnp.exp(alog_ref[...])

    def body(n, carry):
        r0 = pl.multiple_of(n * c, c)
        rows = pl.ds(r0, c)
        cur = p_ref[0, rows, 256:1024]
        p0 = pl.multiple_of(jnp.maximum(r0 - 8, 0), 8)
        prev8 = p_ref[0, pl.ds(p0, 8), 256:1024]
        prev8 = jnp.where(n > 0, prev8, 0.0)
        acc = cur * cw_ref[3:4, :] + cb_ref[...]
        for s in (1, 2, 3):
            sh = pltpu.roll(cur, s, axis=0)
            top = jnp.where(row8 < s, pltpu.roll(prev8, s, axis=0), sh[0:8])
            sh = jnp.concatenate([top, sh[8:]], axis=0)
            acc = acc + sh * cw_ref[3 - s:4 - s, :]
        xbc = acc * _sigmoid(acc)
        x = xbc[:, 0:256]
        bm = xbc[:, 256:512].astype(BF16)
        cm = xbc[:, 512:768].astype(BF16)

        dt_c = _softplus(p_ref[0, rows, 1024:1152] + dtb_ref[...])
        da_c = dt_c * a_c
        cum_c = _dot_split_rhs(tri, da_c, 3)
        cum_r = _dot_split_lhs(da_c, upper, 3, dot=_dot_tn)
        both_x = _dot_split_lhs(jnp.concatenate([dt_c, cum_c], axis=0), expand, 3)
        dt_x = both_x[0:c]
        cum_x = both_x[c:2 * c]
        cl_x = cum_x[c - 1:c, :]
        x_dt = x * dt_x
        x_dt_b = x_dt.astype(BF16)
        xw_b = (x_dt * jnp.exp(cl_x - cum_x)).astype(BF16)

        y = x * dsk_ref[...]
        y_off = jnp.zeros((c, W_MIX), F32)
        for g in range(SSD_GROUPS):
            bg = bm[:, g * n_st:(g + 1) * n_st]
            cg = cm[:, g * n_st:(g + 1) * n_st]
            cb = _dot_nt(cg, bg)
            for r in range(SSD_HEADS // SSD_GROUPS):
                h = g * (SSD_HEADS // SSD_GROUPS) + r
                diff = cum_c[:, h:h + 1] - cum_r[h:h + 1, :]
                dec = jnp.exp(jnp.where(causal, diff, -jnp.inf))
                m = (cb * dec).astype(BF16)
                yh = jnp.dot(m, x_dt_b, preferred_element_type=F32)
                y = y + jnp.where(lane_head == h, yh, 0.0)
            st = st_ref[g]
            y_off = y_off + jnp.where(lane_group == g,
                                      jnp.dot(cg, st.astype(BF16), preferred_element_type=F32), 0.0)
            d_st = _dot_tn(bg, xw_b)
            st_ref[g] = st * jnp.exp(cl_x) + jnp.where(lane_group == g, d_st, 0.0)
        y = y + y_off * jnp.exp(cum_x)
        zg = p_ref[0, rows, 0:256]
        y = y * (zg * _sigmoid(zg))
        outs = []
        for g in range(SSD_GROUPS):
            yg = y[:, g * 128:(g + 1) * 128]
            ms = jnp.mean(yg * yg, axis=-1, keepdims=True)
            outs.append(yg * lax.rsqrt(ms + RMS_EPS))
        o_ref[0, rows, :] = (jnp.concatenate(outs, axis=-1) * ng_ref[...]).astype(o_ref.dtype)
        return carry

    lax.fori_loop(0, s_len // c, body, 0)


def _ssd_mixer(ps, conv_w, conv_b, a_log_c, dt_bias_c, d_x, norm_g):
    b, s, wp = ps.shape
    full2 = lambda i: (0, 0)
    return pl.pallas_call(
        _ssd_kernel,
        grid=(b,),
        in_specs=[pl.BlockSpec((1, s, wp), lambda i: (i, 0, 0)),
                  pl.BlockSpec((SSD_CONV_K, 3 * W_MIX), full2),
                  pl.BlockSpec((1, 3 * W_MIX), full2),
                  pl.BlockSpec((1, LANES), full2),
                  pl.BlockSpec((1, LANES), full2),
                  pl.BlockSpec((1, W_MIX), full2),
                  pl.BlockSpec((1, W_MIX), full2)],
        out_specs=pl.BlockSpec((1, s, W_MIX), lambda i: (i, 0, 0)),
        out_shape=jax.ShapeDtypeStruct((b, s, W_MIX), BF16),
        scratch_shapes=[pltpu.VMEM((SSD_GROUPS, SSD_STATE, W_MIX), F32)],
        compiler_params=_cparams(("parallel",)),
        name="ssd_mixer",
    )(ps, conv_w, conv_b, a_log_c, dt_bias_c, d_x, norm_g)


DIFF_TQ = 256
DIFF_TK = 256
LOG2E = 1.4426950408889634


def _diff_kernel(q_ref, k_ref, v_ref, lam_ref, ng_ref, o_ref,
                 kb_ref, vt_ref, qs_ref, st_ref, m_ref, l_ref, acc_ref, *, lam_init):
    tq, tk = DIFF_TQ, DIFF_TK
    nh, dv = DIFF_HEADS, DIFF_DV
    nhc = 2 * nh
    s_len = k_ref.shape[1]
    i = pl.program_id(1)

    @pl.when(i == 0)
    def _():
        kb_ref[...] = k_ref[0].astype(BF16)
        for cblk in range(s_len // tk):
            cols = slice(cblk * tk, (cblk + 1) * tk)
            vt_ref[:, cols] = v_ref[0, cols, :].T.astype(BF16)

    q = q_ref[0] * (DIFF_DQK ** -0.5 * LOG2E)
    qlane = lax.broadcasted_iota(jnp.int32, (1, W_MIX), 1) // DIFF_DQK
    for hc in range(nhc):
        qs_ref[hc * tq:(hc + 1) * tq, :] = jnp.where(qlane == hc, q, 0.0).astype(BF16)
    m_ref[...] = jnp.full_like(m_ref, -jnp.inf)
    l_ref[...] = jnp.zeros_like(l_ref)
    acc_ref[...] = jnp.zeros_like(acc_ref)
    krow = lax.broadcasted_iota(jnp.int32, (tk, nhc * tq), 0)
    qcol = lax.broadcasted_iota(jnp.int32, (tk, nhc * tq), 1) % tq
    diag_ok = krow <= qcol

    def scores(j, slot):
        k0 = pl.multiple_of(j * tk, tk)
        st_ref[slot] = _dot_nt(kb_ref[pl.ds(k0, tk), :], qs_ref[...])

    def softmax_pv(j, slot, masked):
        k0 = pl.multiple_of(j * tk, tk)
        st = st_ref[slot]
        if masked:
            st = jnp.where(diag_ok, st, -jnp.inf)
        m_prev = m_ref[...]
        m_new = jnp.maximum(m_prev, jnp.max(st, axis=0, keepdims=True))
        alpha = jnp.exp2(m_prev - m_new)
        p = jnp.exp2(st - m_new)
        l_ref[...] = alpha * l_ref[...] + jnp.sum(p, axis=0, keepdims=True)
        m_ref[...] = m_new
        pb = p.astype(BF16)
        for hc in range(nhc):
            h = hc // 2
            lanes = slice(hc * tq, (hc + 1) * tq)
            pv = jnp.dot(vt_ref[h * dv:(h + 1) * dv, pl.ds(k0, tk)], pb[:, lanes],
                         preferred_element_type=F32)
            acc_ref[hc] = acc_ref[hc] * alpha[:, lanes] + pv

    scores(0, 0)
    n_pairs = i // 2

    def pair_step(u, carry):
        scores(2 * u + 1, 1)
        softmax_pv(2 * u, 0, False)
        scores(2 * u + 2, 0)
        softmax_pv(2 * u + 1, 1, False)
        return carry

    lax.fori_loop(0, n_pairs, pair_step, 0)

    @pl.when(i % 2 == 0)
    def _():
        softmax_pv(i, 0, True)

    @pl.when(i % 2 == 1)
    def _():
        scores(i, 1)
        softmax_pv(i - 1, 0, False)
        softmax_pv(i, 1, True)

    lam = (jnp.exp(jnp.sum(lam_ref[0:1, :] * lam_ref[1:2, :], axis=-1, keepdims=True))
           - jnp.exp(jnp.sum(lam_ref[2:3, :] * lam_ref[3:4, :], axis=-1, keepdims=True))
           + lam_init)
    heads = []
    for h in range(nh):
        o1 = acc_ref[2 * h] / l_ref[:, 2 * h * tq:(2 * h + 1) * tq]
        o2 = acc_ref[2 * h + 1] / l_ref[:, (2 * h + 1) * tq:(2 * h + 2) * tq]
        oh = o1 - lam * o2
        ms = jnp.mean(oh * oh, axis=0, keepdims=True)
        heads.append(oh * lax.rsqrt(ms + RMS_EPS))
    o = jnp.concatenate(heads, axis=0).T
    o_ref[0] = (o * ng_ref[...] * (1.0 - lam_init)).astype(o_ref.dtype)


def _diff_mixer(pd, lam_vecs, norm_g4, lam_init):
    b, s, _ = pd.shape
    tq = DIFF_TQ
    return pl.pallas_call(
        functools.partial(_diff_kernel, lam_init=lam_init),
        grid=(b, s // tq),
        in_specs=[pl.BlockSpec((1, tq, W_MIX), lambda bi, i: (bi, i, 0)),
                  pl.BlockSpec((1, s, W_MIX), lambda bi, i: (bi, 0, 1)),
                  pl.BlockSpec((1, s, W_MIX), lambda bi, i: (bi, 0, 2)),
                  pl.BlockSpec((4, LANES), lambda bi, i: (0, 0)),
                  pl.BlockSpec((1, W_MIX), lambda bi, i: (0, 0))],
        out_specs=pl.BlockSpec((1, tq, W_MIX), lambda bi, i: (bi, i, 0)),
        out_shape=jax.ShapeDtypeStruct((b, s, W_MIX), BF16),
        scratch_shapes=[pltpu.VMEM((s, W_MIX), BF16),
                        pltpu.VMEM((W_MIX, s), BF16),
                        pltpu.VMEM((2 * DIFF_HEADS * tq, W_MIX), BF16),
                        pltpu.VMEM((2, DIFF_TK, 2 * DIFF_HEADS * tq), F32),
                        pltpu.VMEM((1, 2 * DIFF_HEADS * tq), F32),
                        pltpu.VMEM((1, 2 * DIFF_HEADS * tq), F32),
                        pltpu.VMEM((2 * DIFF_HEADS, DIFF_DV, tq), F32)],
        compiler_params=_cparams(("parallel", "arbitrary")),
        name="diff_attn",
    )(pd, pd, pd, lam_vecs, norm_g4)


def _oproj_kernel(yc_ref, yg_ref, ys_ref, yd_ref, x_ref, wo_ref, g_ref, b_ref, wr_ref,
                  xo_ref, route_ref, cnt_ref):
    mix = jnp.concatenate([yc_ref[...], yg_ref[...], ys_ref[...], yd_ref[...]], axis=-1)
    h = ALPHA * x_ref[...] + jnp.dot(mix, wo_ref[...], preferred_element_type=F32)
    xn = _layer_norm(h, g_ref[...], b_ref[...])
    xo_ref[...] = xn

    xn_hi, xn_lo = _split_bf16(xn, 2)
    both = _dot(xn_hi, wr_ref[...])
    logits = both[:, 0:LANES] + both[:, LANES:2 * LANES] + _dot(xn_lo, wr_ref[:, 0:LANES])
    lane = lax.broadcasted_iota(jnp.int32, logits.shape, 1).astype(F32)
    neg = -jnp.inf
    big = float(LANES)
    lg = jnp.where(lane < N_GROUPS, logits, neg)
    mg = jnp.max(lg, axis=-1, keepdims=True)
    sg = jnp.sum(jnp.exp(lg - mg), axis=-1, keepdims=True)
    grp = jnp.min(jnp.where(lg == mg, lane, big), axis=-1, keepdims=True)
    p_grp = 1.0 / sg
    lo = N_GROUPS + EXPERTS_PER_GROUP * grp
    in_g = jnp.logical_and(lane >= lo, lane < lo + EXPERTS_PER_GROUP)
    le = jnp.where(in_g, logits, neg)
    me = jnp.max(le, axis=-1, keepdims=True)
    ee = jnp.exp(le - me)
    pe = ee / jnp.sum(ee, axis=-1, keepdims=True)
    pe = jnp.where(in_g, pe, -1.0)
    p1 = jnp.max(pe, axis=-1, keepdims=True)
    i1 = jnp.min(jnp.where(pe == p1, lane, big), axis=-1, keepdims=True)
    pe2 = jnp.where(lane == i1, -1.0, pe)
    p2 = jnp.max(pe2, axis=-1, keepdims=True)
    i2 = jnp.min(jnp.where(pe2 == p2, lane, big), axis=-1, keepdims=True)
    den = p1 + p2
    g1 = p_grp * p1 / den
    g2 = p_grp * p2 / den
    e1 = i1 - N_GROUPS
    e2 = i2 - N_GROUPS
    route_ref[...] = jnp.where(lane == 0, e1, jnp.where(lane == 1, e2, jnp.where(
        lane == 2, g1, jnp.where(lane == 3, g2, 0.0))))

    @pl.when(pl.program_id(0) == 0)
    def _():
        cnt_ref[...] = jnp.zeros_like(cnt_ref)

    hits = jnp.where(lane == e1, 1.0, 0.0) + jnp.where(lane == e2, 1.0, 0.0)
    cnt_ref[...] += jnp.sum(hits, axis=0, keepdims=True)


def _out_proj(ys, x2d, w_o, layer, ln_g, ln_b, w_route):
    t = x2d.shape[0]
    tm = 512
    row = lambda i: (i, 0)
    full = lambda i: (0, 0)
    return pl.pallas_call(
        _oproj_kernel,
        grid=(t // tm,),
        in_specs=[pl.BlockSpec((tm, W_MIX), row)] * 4 + [
            pl.BlockSpec((tm, D_MODEL), row),
            pl.BlockSpec((None, D_MODEL, D_MODEL), lambda i: (layer, 0, 0)),
            pl.BlockSpec((1, D_MODEL), full),
            pl.BlockSpec((1, D_MODEL), full),
            pl.BlockSpec((D_MODEL, 2 * LANES), full)],
        out_specs=[pl.BlockSpec((tm, D_MODEL), row), pl.BlockSpec((tm, LANES), row),
                   pl.BlockSpec((1, LANES), full)],
        out_shape=[jax.ShapeDtypeStruct((t, D_MODEL), F32),
                   jax.ShapeDtypeStruct((t, LANES), F32),
                   jax.ShapeDtypeStruct((1, LANES), F32)],
        compiler_params=_cparams(("arbitrary",)),
        name="out_proj_ln_router",
    )(*ys, x2d, w_o, ln_g, ln_b, w_route)


PLAN_TILE = 512


def _plan_kernel(route_ref, cnt_ref, dest_ref, meta_ref, carry_ref, pstart_ref):
    tm = route_ref.shape[0]
    lane = lax.broadcasted_iota(jnp.int32, (1, LANES), 1).astype(F32)

    @pl.when(pl.program_id(0) == 0)
    def _():
        cnt = cnt_ref[...]
        padded = jnp.ceil(cnt * (1.0 / MOE_BLK)) * MOE_BLK
        li = lax.broadcasted_iota(jnp.int32, (LANES, LANES), 0)
        lj = lax.broadcasted_iota(jnp.int32, (LANES, LANES), 1)
        before = (li < lj).astype(F32)
        pstart = jnp.dot(jnp.broadcast_to(padded, (8, LANES)), before, precision=HI,
                         preferred_element_type=F32)[0:1]
        pstart_ref[...] = pstart
        carry_ref[...] = jnp.zeros_like(carry_ref)
        meta_ref[...] = jnp.concatenate(
            [pstart + padded, pstart, cnt, jnp.zeros((5, LANES), F32)], axis=0)

    oh0 = jnp.where(lane == route_ref[:, 0:1], 1.0, 0.0)
    oh1 = jnp.where(lane == route_ref[:, 1:2], 1.0, 0.0)
    both = oh0 + oh1
    ri = lax.broadcasted_iota(jnp.int32, (tm, tm), 0)
    ci = lax.broadcasted_iota(jnp.int32, (tm, tm), 1)
    earlier = (ci < ri).astype(BF16)
    base = (jnp.dot(earlier, both.astype(BF16), preferred_element_type=F32)
            + carry_ref[...] + pstart_ref[...])
    d0 = jnp.sum(oh0 * base, axis=-1, keepdims=True)
    d1 = jnp.sum(oh1 * base, axis=-1, keepdims=True)
    dest_ref[...] = jnp.where(lane == 0, d0, jnp.where(lane == 1, d1, 0.0))
    carry_ref[...] += jnp.sum(both, axis=0, keepdims=True)


def _dispatch_plan(route, cnt, n_blocks):
    t = route.shape[0]
    tm = PLAN_TILE
    blk = MOE_BLK
    dest, meta = pl.pallas_call(
        _plan_kernel,
        grid=(t // tm,),
        in_specs=[pl.BlockSpec((tm, LANES), lambda i: (i, 0)),
                  pl.BlockSpec((1, LANES), lambda i: (0, 0))],
        out_specs=[pl.BlockSpec((tm, LANES), lambda i: (i, 0)),
                   pl.BlockSpec((8, LANES), lambda i: (0, 0))],
        out_shape=[jax.ShapeDtypeStruct((t, LANES), F32),
                   jax.ShapeDtypeStruct((8, LANES), F32)],
        scratch_shapes=[pltpu.VMEM((1, LANES), F32), pltpu.VMEM((1, LANES), F32)],
        compiler_params=_cparams(("arbitrary",)),
        name="moe_plan",
    )(route, cnt)
    pad_end = meta[0, :N_EXPERTS].astype(jnp.int32)
    starts = jnp.arange(n_blocks, dtype=jnp.int32) * blk
    block_e = jnp.minimum(jnp.sum((pad_end[None, :] <= starts[:, None]).astype(jnp.int32), axis=1),
                          N_EXPERTS - 1)
    n_used = (pad_end[N_EXPERTS - 1] // blk).reshape(1)
    dest3 = dest[:, 0:2].astype(jnp.int32).reshape(t // TOK_TILE, 1, 2 * TOK_TILE)
    return dest3, block_e, n_used


def _scatter_kernel(dest_ref, x_ref, xs_in_ref, xs_ref, sem):
    del xs_in_ref
    tm = x_ref.shape[0]

    def issue(r, carry):
        for k in range(2):
            d = dest_ref[0, 0, 2 * r + k]
            pltpu.make_async_copy(x_ref.at[pl.ds(r, 1)], xs_ref.at[pl.ds(d, 1)],
                                  sem).start(priority=k)
        return carry

    lax.fori_loop(0, tm, issue, 0, unroll=8)
    for k in range(2):
        pltpu.make_async_copy(x_ref, xs_ref.at[pl.ds(0, tm)], sem).wait()


def _dispatch(x2d, dest3, n_slots):
    t = x2d.shape[0]
    tm = TOK_TILE
    xs0 = jnp.zeros((n_slots, D_MODEL), F32)
    return pl.pallas_call(
        _scatter_kernel,
        grid=(t // tm,),
        in_specs=[pl.BlockSpec((1, 1, 2 * tm), lambda i: (i, 0, 0), memory_space=pltpu.SMEM),
                  pl.BlockSpec((tm, D_MODEL), lambda i: (i, 0)),
                  pl.BlockSpec(memory_space=pl.ANY)],
        out_specs=pl.BlockSpec(memory_space=pl.ANY),
        out_shape=jax.ShapeDtypeStruct((n_slots, D_MODEL), F32),
        scratch_shapes=[pltpu.SemaphoreType.DMA(())],
        input_output_aliases={2: 0},
        compiler_params=pltpu.CompilerParams(dimension_semantics=("arbitrary",),
                                             vmem_limit_bytes=VMEM_LIMIT,
                                             has_side_effects=True),
        name="moe_dispatch",
    )(dest3, x2d, xs0)


def _ffn_kernel(be_ref, nu_ref, xs_ref, wg_ref, wu_ref, wd_ref, y_ref, wgb_ref, wub_ref, wdb_ref):
    i = pl.program_id(0)
    prev = be_ref[jnp.maximum(i - 1, 0)]

    @pl.when(jnp.logical_or(i == 0, be_ref[i] != prev))
    def _():
        wgb_ref[...] = wg_ref[...].astype(BF16)
        wub_ref[...] = wu_ref[...].astype(BF16)
        wdb_ref[...] = wd_ref[...].astype(BF16)

    @pl.when(i < nu_ref[0])
    def _():
        xb = xs_ref[...].astype(BF16)
        a = jnp.dot(xb, wgb_ref[...], preferred_element_type=F32)
        u = jnp.dot(xb, wub_ref[...], preferred_element_type=F32)
        h = (a * _sigmoid(a) * u).astype(BF16)
        y_ref[...] = jnp.dot(h, wdb_ref[...], preferred_element_type=F32)

    @pl.when(i >= nu_ref[0])
    def _():
        y_ref[...] = jnp.zeros_like(y_ref)


def _expert_ffn(xs, block_e, n_used, w_gate, w_up, w_down, layer):
    n_slots = xs.shape[0]
    blk = MOE_BLK
    grid_spec = pltpu.PrefetchScalarGridSpec(
        num_scalar_prefetch=2,
        grid=(n_slots // blk,),
        in_specs=[pl.BlockSpec((blk, D_MODEL), lambda i, be, nu: (i, 0)),
                  pl.BlockSpec((None, None, D_MODEL, D_EXPERT),
                               lambda i, be, nu: (layer, be[i], 0, 0)),
                  pl.BlockSpec((None, None, D_MODEL, D_EXPERT),
                               lambda i, be, nu: (layer, be[i], 0, 0)),
                  pl.BlockSpec((None, None, D_EXPERT, D_MODEL),
                               lambda i, be, nu: (layer, be[i], 0, 0))],
        out_specs=pl.BlockSpec((blk, D_MODEL), lambda i, be, nu: (i, 0)),
        scratch_shapes=[pltpu.VMEM((D_MODEL, D_EXPERT), BF16),
                        pltpu.VMEM((D_MODEL, D_EXPERT), BF16),
                        pltpu.VMEM((D_EXPERT, D_MODEL), BF16)],
    )
    return pl.pallas_call(
        _ffn_kernel,
        grid_spec=grid_spec,
        out_shape=jax.ShapeDtypeStruct((n_slots, D_MODEL), F32),
        compiler_params=_cparams(("arbitrary",)),
        name="expert_ffn",
    )(block_e, n_used, xs, w_gate, w_up, w_down)


def _combine_kernel(dest_ref, dnext_ref, route_ref, x_ref, y_ref, g_ref, b_ref, o_ref, buf_ref, sem):
    tm = x_ref.shape[0]
    i = pl.program_id(0)
    slot = i % 2

    def issue(idx_ref, s):
        def body(r, carry):
            for k in range(2):
                d = idx_ref[0, 0, 2 * r + k]
                pltpu.make_async_copy(y_ref.at[pl.ds(d, 1)], buf_ref.at[s, k, pl.ds(r, 1)],
                                      sem.at[s]).start(priority=k)
            return carry

        lax.fori_loop(0, tm, body, 0, unroll=8)

    @pl.when(i == 0)
    def _():
        issue(dest_ref, slot)

    @pl.when(i + 1 < pl.num_programs(0))
    def _():
        issue(dnext_ref, 1 - slot)

    for k in range(2):
        pltpu.make_async_copy(y_ref.at[pl.ds(0, tm)], buf_ref.at[slot, k], sem.at[slot]).wait()
    moe = route_ref[:, 2:3] * buf_ref[slot, 0] + route_ref[:, 3:4] * buf_ref[slot, 1]
    h = ALPHA * x_ref[...] + moe
    o_ref[...] = _layer_norm(h, g_ref[...], b_ref[...])


def _combine(dest3, route, x2d, y, ln_g, ln_b):
    t = x2d.shape[0]
    tm = TOK_TILE
    row = lambda i: (i, 0)
    full = lambda i: (0, 0)
    last = t // tm - 1
    return pl.pallas_call(
        _combine_kernel,
        grid=(t // tm,),
        in_specs=[pl.BlockSpec((1, 1, 2 * tm), lambda i: (i, 0, 0), memory_space=pltpu.SMEM),
                  pl.BlockSpec((1, 1, 2 * tm), lambda i: (jnp.minimum(i + 1, last), 0, 0),
                               memory_space=pltpu.SMEM),
                  pl.BlockSpec((tm, LANES), row),
                  pl.BlockSpec((tm, D_MODEL), row),
                  pl.BlockSpec(memory_space=pl.ANY),
                  pl.BlockSpec((1, D_MODEL), full),
                  pl.BlockSpec((1, D_MODEL), full)],
        out_specs=pl.BlockSpec((tm, D_MODEL), row),
        out_shape=jax.ShapeDtypeStruct((t, D_MODEL), F32),
        scratch_shapes=[pltpu.VMEM((2, 2, tm, D_MODEL), F32), pltpu.SemaphoreType.DMA((2,))],
        compiler_params=_cparams(("arbitrary",)),
        name="moe_combine",
    )(dest3, dest3, route, x2d, y, ln_g, ln_b)


def _pad_cols(w, n):
    return jnp.pad(w, [(0, 0)] * (w.ndim - 1) + [(0, n - w.shape[-1])])


def kernel(x, w_in, conv_w, gla_w_lr, gla_b_lr, gla_norm_g, ssd_conv_w, ssd_conv_b, ssd_a_log,
           ssd_d, ssd_dt_bias, ssd_norm_g, diff_lq1, diff_lk1, diff_lq2, diff_lk2, diff_norm_g,
           w_o, ln1_g, ln1_b, router_g, router_e, w_gate, w_up, w_down, ln2_g, ln2_b):
    bsz, seq, d = x.shape
    t = bsz * seq
    n_assign = 2 * t
    n_blocks = (n_assign + N_EXPERTS * (MOE_BLK - 1)) // MOE_BLK + 1
    n_slots = n_blocks * MOE_BLK
    x2d = x.reshape(t, d)
    w_in_r = jnp.concatenate([w_in[..., 0:768], _pad_cols(w_in[..., 768:1552], 896),
                              _pad_cols(w_in[..., 1552:2580], 1152), w_in[..., 2580:3348]],
                             axis=-1).astype(BF16)
    w_o_b = w_o.astype(BF16)
    for l in range(DEPTH):
        pc, pg, ps, pd = _in_proj(x2d, w_in_r, l)

        y_conv = _conv_mixer(pc.reshape(bsz, seq, -1), conv_w[l])
        w_lr_pad = jnp.pad(gla_w_lr[l], ((0, LANES - GLA_RANK), (0, 0)))
        y_gla = _gla_mixer(pg.reshape(bsz, seq, -1), w_lr_pad, gla_b_lr[l].reshape(1, -1),
                           jnp.tile(gla_norm_g[l], GLA_HEADS).reshape(1, -1))
        pad4 = lambda v: jnp.pad(v, (0, LANES - SSD_HEADS)).reshape(1, LANES)
        y_ssd = _ssd_mixer(ps.reshape(bsz, seq, -1), ssd_conv_w[l], ssd_conv_b[l].reshape(1, -1),
                           pad4(ssd_a_log[l]), pad4(ssd_dt_bias[l]),
                           jnp.repeat(ssd_d[l], SSD_HEADDIM).reshape(1, -1),
                           ssd_norm_g[l].reshape(1, -1))
        lam_vecs = jnp.pad(jnp.stack([diff_lq1[l], diff_lk1[l], diff_lq2[l], diff_lk2[l]]),
                           ((0, 0), (0, LANES - DIFF_DQK)))
        lam_init = 0.8 - 0.6 * math.exp(-0.3 * l)
        y_diff = _diff_mixer(pd.reshape(bsz, seq, -1), lam_vecs,
                             jnp.tile(diff_norm_g[l], DIFF_HEADS).reshape(1, -1), lam_init)

        w_route = _pad_cols(jnp.concatenate(
            [router_g[l], router_e[l].reshape(d, N_EXPERTS)], axis=1), LANES)
        w_route_hi = w_route.astype(BF16)
        w_route = jnp.concatenate(
            [w_route_hi, (w_route - w_route_hi.astype(F32)).astype(BF16)], axis=1)
        ys = [y.reshape(t, W_MIX) for y in (y_conv, y_gla, y_ssd, y_diff)]
        xn, route, cnt = _out_proj(ys, x2d, w_o_b, l, ln1_g[l].reshape(1, -1),
                                   ln1_b[l].reshape(1, -1), w_route)

        dest3, block_e, n_used = _dispatch_plan(route, cnt, n_blocks)
        xs = _dispatch(xn, dest3, n_slots)
        y = _expert_ffn(xs, block_e, n_used, w_gate, w_up, w_down, l)
        x2d = _combine(dest3, route, xn, y, ln2_g[l].reshape(1, -1), ln2_b[l].reshape(1, -1))
    return x2d.reshape(bsz, seq, d)
```

```python
import functools
import math

import jax
import jax.numpy as jnp
from jax import lax
from jax.experimental import pallas as pl
from jax.experimental.pallas import tpu as pltpu

F32 = jnp.float32
BF16 = jnp.bfloat16
HI = lax.Precision.HIGHEST

D_MODEL = 1024
DEPTH = 2
W_MIX = 256
GLA_HEADS, GLA_DK, GLA_DV, GLA_RANK, GLA_TAU, GLA_CHUNK = 4, 32, 64, 16, 16.0, 64
GLA_ROWS = 256
SSD_HEADS, SSD_GROUPS, SSD_HEADDIM, SSD_STATE, SSD_CONV_K, SSD_CHUNK = 4, 2, 64, 128, 4, 128
DIFF_HEADS, DIFF_DQK, DIFF_DV = 4, 32, 64
N_GROUPS, EXPERTS_PER_GROUP, N_EXPERTS, D_EXPERT = 4, 8, 32, 512
ALPHA = (2 * DEPTH) ** 0.25
LN_EPS = 1e-5
RMS_EPS = 1e-6

LANES = 128
SUBLANES = 8
PROJ_WIDTHS = (768, 896, 1152, 768)
VMEM_LIMIT = 56 * 1024 * 1024

MOE_BLK = 512
TOK_TILE = 256


def _cparams(sem):
    return pltpu.CompilerParams(dimension_semantics=sem, vmem_limit_bytes=VMEM_LIMIT)


def _sigmoid(x):
    return 1.0 / (1.0 + jnp.exp(-x))


def _softplus(x):
    return jnp.maximum(x, 0.0) + jnp.log(1.0 + jnp.exp(-jnp.abs(x)))


def _layer_norm(h, g, b):
    mu = jnp.mean(h, axis=-1, keepdims=True)
    d = h - mu
    var = jnp.mean(d * d, axis=-1, keepdims=True)
    return d * lax.rsqrt(var + LN_EPS) * g + b


def _dot_nt(a, b):
    return lax.dot_general(a, b, (((1,), (1,)), ((), ())), preferred_element_type=F32)


def _dot_tn(a, b, precision=None):
    return lax.dot_general(a, b, (((0,), (0,)), ((), ())), preferred_element_type=F32,
                           precision=precision)


def _split_bf16(x, parts):
    out = []
    for _ in range(parts - 1):
        hi = x.astype(BF16)
        out.append(hi)
        x = x - hi.astype(F32)
    out.append(x.astype(BF16))
    return out


def _dot(a, b):
    return jnp.dot(a, b, preferred_element_type=F32)


def _dot_split_lhs(a, b_exact, parts, dot=_dot):
    acc = None
    for term in _split_bf16(a, parts):
        d = dot(term, b_exact)
        acc = d if acc is None else acc + d
    return acc


def _dot_split_rhs(a_exact, b, parts):
    acc = None
    for term in _split_bf16(b, parts):
        d = jnp.dot(a_exact, term, preferred_element_type=F32)
        acc = d if acc is None else acc + d
    return acc


def _proj_kernel(x_ref, w_ref, oc_ref, og_ref, os_ref, od_ref):
    xb = x_ref[...].astype(BF16)
    off = 0
    for o_ref in (oc_ref, og_ref, os_ref, od_ref):
        n = o_ref.shape[-1]
        o_ref[...] = jnp.dot(xb, w_ref[:, off:off + n], preferred_element_type=F32)
        off += n


def _in_proj(x2d, w_r, layer):
    t = x2d.shape[0]
    tm = 512
    ncol = sum(PROJ_WIDTHS)
    return pl.pallas_call(
        _proj_kernel,
        grid=(t // tm,),
        in_specs=[pl.BlockSpec((tm, D_MODEL), lambda i: (i, 0)),
                  pl.BlockSpec((None, D_MODEL, ncol), lambda i: (layer, 0, 0))],
        out_specs=[pl.BlockSpec((tm, n), lambda i: (i, 0)) for n in PROJ_WIDTHS],
        out_shape=[jax.ShapeDtypeStruct((t, n), F32) for n in PROJ_WIDTHS],
        compiler_params=_cparams(("parallel",)),
        name="in_proj",
    )(x2d, w_r)


def _conv_kernel(p_ref, w_ref, o_ref):
    u = p_ref[0, :, 0:W_MIX]
    gb = p_ref[0, :, W_MIX:2 * W_MIX]
    gc = p_ref[0, :, 2 * W_MIX:3 * W_MIX]
    cu = gc * u
    row = lax.broadcasted_iota(jnp.int32, cu.shape, 0)
    acc = cu * w_ref[2:3, :]
    for s in (1, 2):
        sh = jnp.where(row >= s, pltpu.roll(cu, s, axis=0), 0.0)
        acc = acc + sh * w_ref[2 - s:3 - s, :]
    o_ref[0] = (gb * acc).astype(o_ref.dtype)


def _conv_mixer(pc, conv_w):
    b, s, _ = pc.shape
    return pl.pallas_call(
        _conv_kernel,
        grid=(b,),
        in_specs=[pl.BlockSpec((1, s, 3 * W_MIX), lambda i: (i, 0, 0)),
                  pl.BlockSpec((3, W_MIX), lambda i: (0, 0))],
        out_specs=pl.BlockSpec((1, s, W_MIX), lambda i: (i, 0, 0)),
        out_shape=jax.ShapeDtypeStruct((b, s, W_MIX), BF16),
        compiler_params=_cparams(("parallel",)),
        name="conv_mixer",
    )(pc, conv_w)


def _gla_kernel(p_ref, wlr_ref, blr_ref, ng_ref, o_ref, st_ref):
    c = GLA_CHUNK
    s_len = p_ref.shape[1]
    nh, dk, dv = GLA_HEADS, GLA_DK, GLA_DV
    st_ref[...] = jnp.zeros_like(st_ref)

    rb = GLA_ROWS
    ncb = rb // c
    ri = lax.broadcasted_iota(jnp.int32, (rb, rb), 0)
    ci = lax.broadcasted_iota(jnp.int32, (rb, rb), 1)
    tri = (ci <= ri).astype(BF16)
    klane_head = lax.broadcasted_iota(jnp.int32, (1, nh * dk), 1) // dk
    vlane_head = lax.broadcasted_iota(jnp.int32, (1, nh * dv), 1) // dv
    strow_head = lax.broadcasted_iota(jnp.int32, (nh * dv, 1), 0) // dv
    st_mask = strow_head == klane_head
    r4 = lax.broadcasted_iota(jnp.int32, (nh * c, c), 0) % c
    c4 = lax.broadcasted_iota(jnp.int32, (nh * c, c), 1)
    causal4 = c4 <= r4
    gi = lax.broadcasted_iota(jnp.int32, (nh * dv, nh * dv), 0) // dv
    gj = lax.broadcasted_iota(jnp.int32, (nh * dv, nh * dv), 1) // dv
    gmean = jnp.where(gi == gj, 1.0 / dv, 0.0).astype(BF16)
    wlr_hi, wlr_lo = _split_bf16(wlr_ref[...], 2)

    def body(n, carry):
        r0 = pl.multiple_of(n * rb, rb)
        rows = pl.ds(r0, rb)
        q = p_ref[0, rows, 0:128] * (dk ** -0.5)
        k = p_ref[0, rows, 128:256]
        vb = p_ref[0, rows, 256:512].astype(BF16)
        g = p_ref[0, rows, 512:768]
        lr = p_ref[0, rows, 768:896]
        lr_hi, lr_lo = _split_bf16(lr, 2)
        z = (jnp.dot(lr_hi, wlr_hi, preferred_element_type=F32)
             + jnp.dot(lr_hi, wlr_lo, preferred_element_type=F32)
             + jnp.dot(lr_lo, wlr_hi, preferred_element_type=F32)) + blr_ref[...]
        log_a = (jnp.minimum(z, 0.0) - jnp.log(1.0 + jnp.exp(-jnp.abs(z)))) * (1.0 / GLA_TAU)
        cumb = _dot_split_rhs(tri, log_a, 3)
        ends = [cumb[(j + 1) * c - 1:(j + 1) * c, :] for j in range(ncb)]
        starts = [jnp.zeros_like(ends[0])] + ends[:-1]
        cum = cumb - jnp.concatenate([jnp.broadcast_to(s0, (c, nh * dk)) for s0 in starts], axis=0)
        lasts = [e - s0 for e, s0 in zip(ends, starts)]
        cl = jnp.concatenate([jnp.broadcast_to(x, (c, nh * dk)) for x in lasts], axis=0)
        q_dec = q * jnp.exp(cum)
        k_inv = (k * jnp.exp(-cum)).astype(BF16)
        k_end = (k * jnp.exp(cl - cum)).astype(BF16)
        st = st_ref[...]
        outs = []
        for j in range(ncb):
            sl = slice(j * c, (j + 1) * c)
            qd = q_dec[sl]
            qs = jnp.concatenate([jnp.where(klane_head == h, qd, 0.0) for h in range(nh)],
                                 axis=0).astype(BF16)
            att = jnp.where(causal4, _dot_nt(qs, k_inv[sl]), 0.0)
            r = jnp.dot(att.astype(BF16), vb[sl], preferred_element_type=F32)
            o = jnp.where(vlane_head == 0, r[0:c], 0.0)
            for h in range(1, nh):
                o = o + jnp.where(vlane_head == h, r[h * c:(h + 1) * c], 0.0)
            outs.append(o + _dot_nt(qd.astype(BF16), st.astype(BF16)))
            d_st = _dot_tn(vb[sl], k_end[sl])
            st = st * jnp.exp(lasts[j]) + jnp.where(st_mask, d_st, 0.0)
        st_ref[...] = st
        o = jnp.concatenate(outs, axis=0)
        ms = _dot_split_lhs(o * o, gmean, 2)
        o = o * lax.rsqrt(ms + RMS_EPS) * ng_ref[...]
        o_ref[0, rows, :] = (o * (g * _sigmoid(g))).astype(o_ref.dtype)
        return carry

    lax.fori_loop(0, s_len // rb, body, 0)


def _gla_mixer(pg, w_lr_pad, b_lr, norm_g4):
    b, s, wp = pg.shape
    return pl.pallas_call(
        _gla_kernel,
        grid=(b,),
        in_specs=[pl.BlockSpec((1, s, wp), lambda i: (i, 0, 0)),
                  pl.BlockSpec((LANES, LANES), lambda i: (0, 0)),
                  pl.BlockSpec((1, LANES), lambda i: (0, 0)),
                  pl.BlockSpec((1, W_MIX), lambda i: (0, 0))],
        out_specs=pl.BlockSpec((1, s, W_MIX), lambda i: (i, 0, 0)),
        out_shape=jax.ShapeDtypeStruct((b, s, W_MIX), BF16),
        scratch_shapes=[pltpu.VMEM((GLA_HEADS * GLA_DV, GLA_HEADS * GLA_DK), F32)],
        compiler_params=_cparams(("parallel",)),
        name="gla_mixer",
    )(pg, w_lr_pad, b_lr, norm_g4)


def _ssd_kernel(p_ref, cw_ref, cb_ref, alog_ref, dtb_ref, dsk_ref, ng_ref, o_ref, st_ref):
    c = SSD_CHUNK
    s_len = p_ref.shape[1]
    n_st = SSD_STATE
    st_ref[...] = jnp.zeros_like(st_ref)

    ri = lax.broadcasted_iota(jnp.int32, (c, c), 0)
    ci = lax.broadcasted_iota(jnp.int32, (c, c), 1)
    causal = ci <= ri
    tri = causal.astype(BF16)
    upper = (ri <= ci).astype(BF16)
    lane_head = lax.broadcasted_iota(jnp.int32, (1, W_MIX), 1) // SSD_HEADDIM
    lane_group = lane_head // (SSD_HEADS // SSD_GROUPS)
    eh = lax.broadcasted_iota(jnp.int32, (LANES, W_MIX), 0)
    el = lax.broadcasted_iota(jnp.int32, (LANES, W_MIX), 1) // SSD_HEADDIM
    expand = (eh == el).astype(BF16)
    row8 = lax.broadcasted_iota(jnp.int32, (8, 3 * W_MIX), 0)
    a_c = -jnp.exp(alog_ref[...])

    def body(n, carry):
        r0 = pl.multiple_of(n * c, c)
        rows = pl.ds(r0, c)
        cur = p_ref[0, rows, 256:1024]
        p0 = pl.multiple_of(jnp.maximum(r0 - 8, 0), 8)
        prev8 = p_ref[0, pl.ds(p0, 8), 256:1024]
        prev8 = jnp.where(n > 0, prev8, 0.0)
        acc = cur * cw_ref[3:4, :] + cb_ref[...]
        for s in (1, 2, 3):
            sh = pltpu.roll(cur, s, axis=0)
            top = jnp.where(row8 < s, pltpu.roll(prev8, s, axis=0), sh[0:8])
            sh = jnp.concatenate([top, sh[8:]], axis=0)
            acc = acc + sh * cw_ref[3 - s:4 - s, :]
        xbc = acc * _sigmoid(acc)
        x = xbc[:, 0:256]
        bm = xbc[:, 256:512].astype(BF16)
        cm = xbc[:, 512:768].astype(BF16)

        dt_c = _softplus(p_ref[0, rows, 1024:1152] + dtb_ref[...])
        da_c = dt_c * a_c
        cum_c = _dot_split_rhs(tri, da_c, 3)
        cum_r = _dot_split_lhs(da_c, upper, 3, dot=_dot_tn)
        both_x = _dot_split_lhs(jnp.concatenate([dt_c, cum_c], axis=0), expand, 3)
        dt_x = both_x[0:c]
        cum_x = both_x[c:2 * c]
        cl_x = cum_x[c - 1:c, :]
        x_dt = x * dt_x
        x_dt_b = x_dt.astype(BF16)
        xw_b = (x_dt * jnp.exp(cl_x - cum_x)).astype(BF16)

        y = x * dsk_ref[...]
        y_off = jnp.zeros((c, W_MIX), F32)
        for g in range(SSD_GROUPS):
            bg = bm[:, g * n_st:(g + 1) * n_st]
            cg = cm[:, g * n_st:(g + 1) * n_st]
            cb = _dot_nt(cg, bg)
            for r in range(SSD_HEADS // SSD_GROUPS):
                h = g * (SSD_HEADS // SSD_GROUPS) + r
                diff = cum_c[:, h:h + 1] - cum_r[h:h + 1, :]
                dec = jnp.exp(jnp.where(causal, diff, -jnp.inf))
                m = (cb * dec).astype(BF16)
                yh = jnp.dot(m, x_dt_b, preferred_element_type=F32)
                y = y + jnp.where(lane_head == h, yh, 0.0)
            st = st_ref[g]
            y_off = y_off + jnp.where(lane_group == g,
                                      jnp.dot(cg, st.astype(BF16), preferred_element_type=F32), 0.0)
            d_st = _dot_tn(bg, xw_b)
            st_ref[g] = st * jnp.exp(cl_x) + jnp.where(lane_group == g, d_st, 0.0)
        y = y + y_off * jnp.exp(cum_x)
        zg = p_ref[0, rows, 0:256]
        y = y * (zg * _sigmoid(zg))
        outs = []
        for g in range(SSD_GROUPS):
            yg = y[:, g * 128:(g + 1) * 128]
            ms = jnp.mean(yg * yg, axis=-1, keepdims=True)
            outs.append(yg * lax.rsqrt(ms + RMS_EPS))
        o_ref[0, rows, :] = (jnp.concatenate(outs, axis=-1) * ng_ref[...]).astype(o_ref.dtype)
        return carry

    def pair(u, carry):
        body(2 * u, carry)
        return body(2 * u + 1, carry)

    lax.fori_loop(0, s_len // (2 * c), pair, 0)


def _ssd_mixer(ps, conv_w, conv_b, a_log_c, dt_bias_c, d_x, norm_g):
    b, s, wp = ps.shape
    full2 = lambda i: (0, 0)
    return pl.pallas_call(
        _ssd_kernel,
        grid=(b,),
        in_specs=[pl.BlockSpec((1, s, wp), lambda i: (i, 0, 0)),
                  pl.BlockSpec((SSD_CONV_K, 3 * W_MIX), full2),
                  pl.BlockSpec((1, 3 * W_MIX), full2),
                  pl.BlockSpec((1, LANES), full2),
                  pl.BlockSpec((1, LANES), full2),
                  pl.BlockSpec((1, W_MIX), full2),
                  pl.BlockSpec((1, W_MIX), full2)],
        out_specs=pl.BlockSpec((1, s, W_MIX), lambda i: (i, 0, 0)),
        out_shape=jax.ShapeDtypeStruct((b, s, W_MIX), BF16),
        scratch_shapes=[pltpu.VMEM((SSD_GROUPS, SSD_STATE, W_MIX), F32)],
        compiler_params=_cparams(("parallel",)),
        name="ssd_mixer",
    )(ps, conv_w, conv_b, a_log_c, dt_bias_c, d_x, norm_g)


DIFF_TQ = 256
DIFF_TK = 256
LOG2E = 1.4426950408889634
DIFF_VPAD = DIFF_DV + 16


def _diff_kernel(q_ref, k_ref, v_ref, lam_ref, ng_ref, o_ref,
                 kb_ref, vt_ref, qs_ref, st_ref, m_ref, acc_ref, *, lam_init):
    tq, tk = DIFF_TQ, DIFF_TK
    nh, dv = DIFF_HEADS, DIFF_DV
    nhc = 2 * nh
    s_len = k_ref.shape[1]
    i = pl.program_id(1)

    @pl.when(i == 0)
    def _():
        kb_ref[...] = k_ref[0].astype(BF16)
        for cblk in range(s_len // tk):
            cols = slice(cblk * tk, (cblk + 1) * tk)
            vt = v_ref[0, cols, :].T.astype(BF16)
            for h in range(nh):
                vt_ref[h, 0:dv, cols] = vt[h * dv:(h + 1) * dv]
        vt_ref[:, dv:, :] = jnp.ones((nh, DIFF_VPAD - dv, s_len), BF16)

    q = q_ref[0] * (DIFF_DQK ** -0.5 * LOG2E)
    qlane = lax.broadcasted_iota(jnp.int32, (1, W_MIX), 1) // DIFF_DQK
    for hc in range(nhc):
        qs_ref[hc * tq:(hc + 1) * tq, :] = jnp.where(qlane == hc, q, 0.0).astype(BF16)
    m_ref[...] = jnp.full_like(m_ref, -jnp.inf)
    acc_ref[...] = jnp.zeros_like(acc_ref)
    krow = lax.broadcasted_iota(jnp.int32, (tk, nhc * tq), 0)
    qcol = lax.broadcasted_iota(jnp.int32, (tk, nhc * tq), 1) % tq
    diag_ok = krow <= qcol

    def scores(j, slot):
        k0 = pl.multiple_of(j * tk, tk)
        st_ref[slot] = _dot_nt(kb_ref[pl.ds(k0, tk), :], qs_ref[...])

    def softmax_pv(j, slot, masked):
        k0 = pl.multiple_of(j * tk, tk)
        st = st_ref[slot]
        if masked:
            st = jnp.where(diag_ok, st, -jnp.inf)
        m_prev = m_ref[...]
        m_new = jnp.maximum(m_prev, jnp.max(st, axis=0, keepdims=True))
        alpha = jnp.exp2(m_prev - m_new)
        p = jnp.exp2(st - m_new)
        m_ref[...] = m_new
        pb = p.astype(BF16)
        for hc in range(nhc):
            h = hc // 2
            lanes = slice(hc * tq, (hc + 1) * tq)
            pv = jnp.dot(vt_ref[h, :, pl.ds(k0, tk)], pb[:, lanes],
                         preferred_element_type=F32)
            acc_ref[hc] = acc_ref[hc] * alpha[:, lanes] + pv

    scores(0, 0)
    n_pairs = i // 2

    def pair_step(u, carry):
        scores(2 * u + 1, 1)
        softmax_pv(2 * u, 0, False)
        scores(2 * u + 2, 0)
        softmax_pv(2 * u + 1, 1, False)
        return carry

    lax.fori_loop(0, n_pairs, pair_step, 0)

    @pl.when(i % 2 == 0)
    def _():
        softmax_pv(i, 0, True)

    @pl.when(i % 2 == 1)
    def _():
        scores(i, 1)
        softmax_pv(i - 1, 0, False)
        softmax_pv(i, 1, True)

    lam = (jnp.exp(jnp.sum(lam_ref[0:1, :] * lam_ref[1:2, :], axis=-1, keepdims=True))
           - jnp.exp(jnp.sum(lam_ref[2:3, :] * lam_ref[3:4, :], axis=-1, keepdims=True))
           + lam_init)
    heads = []
    for h in range(nh):
        o1 = acc_ref[2 * h, 0:dv] / acc_ref[2 * h, dv:dv + 1]
        o2 = acc_ref[2 * h + 1, 0:dv] / acc_ref[2 * h + 1, dv:dv + 1]
        oh = o1 - lam * o2
        ms = jnp.mean(oh * oh, axis=0, keepdims=True)
        heads.append(oh * lax.rsqrt(ms + RMS_EPS))
    o = jnp.concatenate(heads, axis=0).T
    o_ref[0] = (o * ng_ref[...] * (1.0 - lam_init)).astype(o_ref.dtype)


def _diff_mixer(pd, lam_vecs, norm_g4, lam_init):
    b, s, _ = pd.shape
    tq = DIFF_TQ
    return pl.pallas_call(
        functools.partial(_diff_kernel, lam_init=lam_init),
        grid=(b, s // tq),
        in_specs=[pl.BlockSpec((1, tq, W_MIX), lambda bi, i: (bi, i, 0)),
                  pl.BlockSpec((1, s, W_MIX), lambda bi, i: (bi, 0, 1)),
                  pl.BlockSpec((1, s, W_MIX), lambda bi, i: (bi, 0, 2)),
                  pl.BlockSpec((4, LANES), lambda bi, i: (0, 0)),
                  pl.BlockSpec((1, W_MIX), lambda bi, i: (0, 0))],
        out_specs=pl.BlockSpec((1, tq, W_MIX), lambda bi, i: (bi, i, 0)),
        out_shape=jax.ShapeDtypeStruct((b, s, W_MIX), BF16),
        scratch_shapes=[pltpu.VMEM((s, W_MIX), BF16),
                        pltpu.VMEM((DIFF_HEADS, DIFF_VPAD, s), BF16),
                        pltpu.VMEM((2 * DIFF_HEADS * tq, W_MIX), BF16),
                        pltpu.VMEM((2, DIFF_TK, 2 * DIFF_HEADS * tq), F32),
                        pltpu.VMEM((1, 2 * DIFF_HEADS * tq), F32),
                        pltpu.VMEM((2 * DIFF_HEADS, DIFF_VPAD, tq), F32)],
        compiler_params=_cparams(("parallel", "arbitrary")),
        name="diff_attn",
    )(pd, pd, pd, lam_vecs, norm_g4)


def _oproj_kernel(yc_ref, yg_ref, ys_ref, yd_ref, x_ref, wo_ref, g_ref, b_ref, wr_ref,
                  xo_ref, route_ref, cnt_ref):
    mix = jnp.concatenate([yc_ref[...], yg_ref[...], ys_ref[...], yd_ref[...]], axis=-1)
    h = ALPHA * x_ref[...] + jnp.dot(mix, wo_ref[...], preferred_element_type=F32)
    xn = _layer_norm(h, g_ref[...], b_ref[...])
    xo_ref[...] = xn

    xn_hi, xn_lo = _split_bf16(xn, 2)
    both = _dot(xn_hi, wr_ref[...])
    logits = both[:, 0:LANES] + both[:, LANES:2 * LANES] + _dot(xn_lo, wr_ref[:, 0:LANES])
    lane = lax.broadcasted_iota(jnp.int32, logits.shape, 1).astype(F32)
    neg = -jnp.inf
    big = float(LANES)
    lg = jnp.where(lane < N_GROUPS, logits, neg)
    mg = jnp.max(lg, axis=-1, keepdims=True)
    sg = jnp.sum(jnp.exp(lg - mg), axis=-1, keepdims=True)
    grp = jnp.min(jnp.where(lg == mg, lane, big), axis=-1, keepdims=True)
    p_grp = 1.0 / sg
    lo = N_GROUPS + EXPERTS_PER_GROUP * grp
    in_g = jnp.logical_and(lane >= lo, lane < lo + EXPERTS_PER_GROUP)
    le = jnp.where(in_g, logits, neg)
    me = jnp.max(le, axis=-1, keepdims=True)
    ee = jnp.exp(le - me)
    pe = ee / jnp.sum(ee, axis=-1, keepdims=True)
    pe = jnp.where(in_g, pe, -1.0)
    p1 = jnp.max(pe, axis=-1, keepdims=True)
    i1 = jnp.min(jnp.where(pe == p1, lane, big), axis=-1, keepdims=True)
    pe2 = jnp.where(lane == i1, -1.0, pe)
    p2 = jnp.max(pe2, axis=-1, keepdims=True)
    i2 = jnp.min(jnp.where(pe2 == p2, lane, big), axis=-1, keepdims=True)
    den = p1 + p2
    g1 = p_grp * p1 / den
    g2 = p_grp * p2 / den
    e1 = i1 - N_GROUPS
    e2 = i2 - N_GROUPS
    route_ref[...] = jnp.where(lane == 0, e1, jnp.where(lane == 1, e2, jnp.where(
        lane == 2, g1, jnp.where(lane == 3, g2, 0.0))))

    @pl.when(pl.program_id(0) == 0)
    def _():
        cnt_ref[...] = jnp.zeros_like(cnt_ref)

    hits = jnp.where(lane == e1, 1.0, 0.0) + jnp.where(lane == e2, 1.0, 0.0)
    cnt_ref[...] += jnp.sum(hits, axis=0, keepdims=True)


def _out_proj(ys, x2d, w_o, layer, ln_g, ln_b, w_route):
    t = x2d.shape[0]
    tm = 512
    row = lambda i: (i, 0)
    full = lambda i: (0, 0)
    return pl.pallas_call(
        _oproj_kernel,
        grid=(t // tm,),
        in_specs=[pl.BlockSpec((tm, W_MIX), row)] * 4 + [
            pl.BlockSpec((tm, D_MODEL), row),
            pl.BlockSpec((None, D_MODEL, D_MODEL), lambda i: (layer, 0, 0)),
            pl.BlockSpec((1, D_MODEL), full),
            pl.BlockSpec((1, D_MODEL), full),
            pl.BlockSpec((D_MODEL, 2 * LANES), full)],
        out_specs=[pl.BlockSpec((tm, D_MODEL), row), pl.BlockSpec((tm, LANES), row),
                   pl.BlockSpec((1, LANES), full)],
        out_shape=[jax.ShapeDtypeStruct((t, D_MODEL), F32),
                   jax.ShapeDtypeStruct((t, LANES), F32),
                   jax.ShapeDtypeStruct((1, LANES), F32)],
        compiler_params=_cparams(("arbitrary",)),
        name="out_proj_ln_router",
    )(*ys, x2d, w_o, ln_g, ln_b, w_route)


PLAN_TILE = 512


def _plan_kernel(route_ref, cnt_ref, dest_ref, meta_ref, carry_ref, pstart_ref):
    tm = route_ref.shape[0]
    lane = lax.broadcasted_iota(jnp.int32, (1, LANES), 1).astype(F32)

    @pl.when(pl.program_id(0) == 0)
    def _():
        cnt = cnt_ref[...]
        padded = jnp.ceil(cnt * (1.0 / MOE_BLK)) * MOE_BLK
        li = lax.broadcasted_iota(jnp.int32, (LANES, LANES), 0)
        lj = lax.broadcasted_iota(jnp.int32, (LANES, LANES), 1)
        before = (li < lj).astype(F32)
        pstart = jnp.dot(jnp.broadcast_to(padded, (8, LANES)), before, precision=HI,
                         preferred_element_type=F32)[0:1]
        pstart_ref[...] = pstart
        carry_ref[...] = jnp.zeros_like(carry_ref)
        meta_ref[...] = jnp.concatenate(
            [pstart + padded, pstart, cnt, jnp.zeros((5, LANES), F32)], axis=0)

    oh0 = jnp.where(lane == route_ref[:, 0:1], 1.0, 0.0)
    oh1 = jnp.where(lane == route_ref[:, 1:2], 1.0, 0.0)
    both = oh0 + oh1
    ri = lax.broadcasted_iota(jnp.int32, (tm, tm), 0)
    ci = lax.broadcasted_iota(jnp.int32, (tm, tm), 1)
    earlier = (ci < ri).astype(BF16)
    base = (jnp.dot(earlier, both.astype(BF16), preferred_element_type=F32)
            + carry_ref[...] + pstart_ref[...])
    d0 = jnp.sum(oh0 * base, axis=-1, keepdims=True)
    d1 = jnp.sum(oh1 * base, axis=-1, keepdims=True)
    dest_ref[...] = jnp.where(lane == 0, d0, jnp.where(lane == 1, d1, 0.0))
    carry_ref[...] += jnp.sum(both, axis=0, keepdims=True)


def _dispatch_plan(route, cnt, n_blocks):
    t = route.shape[0]
    tm = PLAN_TILE
    blk = MOE_BLK
    dest, meta = pl.pallas_call(
        _plan_kernel,
        grid=(t // tm,),
        in_specs=[pl.BlockSpec((tm, LANES), lambda i: (i, 0)),
                  pl.BlockSpec((1, LANES), lambda i: (0, 0))],
        out_specs=[pl.BlockSpec((tm, LANES), lambda i: (i, 0)),
                   pl.BlockSpec((8, LANES), lambda i: (0, 0))],
        out_shape=[jax.ShapeDtypeStruct((t, LANES), F32),
                   jax.ShapeDtypeStruct((8, LANES), F32)],
        scratch_shapes=[pltpu.VMEM((1, LANES), F32), pltpu.VMEM((1, LANES), F32)],
        compiler_params=_cparams(("arbitrary",)),
        name="moe_plan",
    )(route, cnt)
    pad_end = meta[0, :N_EXPERTS].astype(jnp.int32)
    starts = jnp.arange(n_blocks, dtype=jnp.int32) * blk
    block_e = jnp.minimum(jnp.sum((pad_end[None, :] <= starts[:, None]).astype(jnp.int32), axis=1),
                          N_EXPERTS - 1)
    n_used = (pad_end[N_EXPERTS - 1] // blk).reshape(1)
    seg_end = (meta[1, :N_EXPERTS] + meta[2, :N_EXPERTS]).astype(jnp.int32)
    n_valid = jnp.clip(seg_end[block_e] - starts, 0, blk)
    pads = _pad_cols(jnp.stack([seg_end, pad_end]), LANES)
    dest3 = dest[:, 0:2].astype(jnp.int32).reshape(t // TOK_TILE, 1, 2 * TOK_TILE)
    return dest3, block_e, n_used, n_valid, pads


ZERO_ROWS = MOE_BLK // 2


def _scatter_kernel(pads_ref, dest_ref, x_ref, xs_ref, zero_ref, sem, zsem):
    tm = x_ref.shape[0]
    n_slots = xs_ref.shape[0]

    def zero_fill(wait):
        def go(src, dst):
            cp = pltpu.make_async_copy(src, dst, zsem)
            if wait:
                cp.wait()
            else:
                cp.start()

        def per_expert(e, carry):
            pos = pads_ref[0, e]
            head = (-pos) & (SUBLANES - 1)
            for r in range(SUBLANES - 1):
                @pl.when(r < head)
                def _():
                    go(zero_ref.at[pl.ds(0, 1)], xs_ref.at[pl.ds(pos + r, 1)])

            base = pos + head
            nrem = pads_ref[1, e] - base
            p = ZERO_ROWS
            while p >= SUBLANES:
                off = pl.multiple_of(base + (nrem & (-2 * p)), SUBLANES)

                @pl.when((nrem & p) != 0)
                def _():
                    go(zero_ref.at[pl.ds(0, p)], xs_ref.at[pl.ds(off, p)])

                p //= 2
            return carry

        lax.fori_loop(0, N_EXPERTS, per_expert, 0)
        tail0 = pads_ref[1, N_EXPERTS - 1]

        def tail(b, carry):
            row0 = pl.multiple_of(tail0 + b * ZERO_ROWS, ZERO_ROWS)
            go(zero_ref, xs_ref.at[pl.ds(row0, ZERO_ROWS)])
            return carry

        lax.fori_loop(0, (n_slots - tail0) // ZERO_ROWS, tail, 0)

    @pl.when(pl.program_id(0) == 0)
    def _():
        zero_ref[...] = jnp.zeros_like(zero_ref)
        zero_fill(False)
        zero_fill(True)

    def issue(r, carry):
        for k in range(2):
            d = dest_ref[0, 0, 2 * r + k]
            pltpu.make_async_copy(x_ref.at[pl.ds(r, 1)], xs_ref.at[pl.ds(d, 1)],
                                  sem).start(priority=k)
        return carry

    lax.fori_loop(0, tm, issue, 0, unroll=8)
    for k in range(2):
        pltpu.make_async_copy(x_ref, xs_ref.at[pl.ds(0, tm)], sem).wait()


def _dispatch(x2d, dest3, pads, n_slots):
    t = x2d.shape[0]
    tm = TOK_TILE
    return pl.pallas_call(
        _scatter_kernel,
        grid=(t // tm,),
        in_specs=[pl.BlockSpec(memory_space=pltpu.SMEM),
                  pl.BlockSpec((1, 1, 2 * tm), lambda i: (i, 0, 0), memory_space=pltpu.SMEM),
                  pl.BlockSpec((tm, D_MODEL), lambda i: (i, 0))],
        out_specs=pl.BlockSpec(memory_space=pl.ANY),
        out_shape=jax.ShapeDtypeStruct((n_slots, D_MODEL), F32),
        scratch_shapes=[pltpu.VMEM((ZERO_ROWS, D_MODEL), F32),
                        pltpu.SemaphoreType.DMA(()), pltpu.SemaphoreType.DMA(())],
        compiler_params=pltpu.CompilerParams(dimension_semantics=("arbitrary",),
                                             vmem_limit_bytes=VMEM_LIMIT,
                                             has_side_effects=True),
        name="moe_dispatch",
    )(pads, dest3, x2d)


def _ffn_kernel(be_ref, nu_ref, nv_ref, xs_ref, wg_ref, wu_ref, wd_ref, y_ref,
                wgb_ref, wub_ref, wdb_ref):
    i = pl.program_id(0)
    prev = be_ref[jnp.maximum(i - 1, 0)]

    @pl.when(jnp.logical_or(i == 0, be_ref[i] != prev))
    def _():
        wgb_ref[...] = wg_ref[...].astype(BF16)
        wub_ref[...] = wu_ref[...].astype(BF16)
        wdb_ref[...] = wd_ref[...].astype(BF16)

    @pl.when(i < nu_ref[0])
    def _():
        row = lax.broadcasted_iota(jnp.int32, (xs_ref.shape[0], 1), 0)
        xb = jnp.where(row < nv_ref[i], xs_ref[...], 0.0).astype(BF16)
        a = jnp.dot(xb, wgb_ref[...], preferred_element_type=F32)
        u = jnp.dot(xb, wub_ref[...], preferred_element_type=F32)
        h = (a * _sigmoid(a) * u).astype(BF16)
        y_ref[...] = jnp.dot(h, wdb_ref[...], preferred_element_type=F32)

    @pl.when(i >= nu_ref[0])
    def _():
        y_ref[...] = jnp.zeros_like(y_ref)


def _expert_ffn(xs, block_e, n_used, n_valid, w_gate, w_up, w_down, layer):
    n_slots = xs.shape[0]
    blk = MOE_BLK
    w_map = lambda i, be, nu, nv: (layer, be[i], 0, 0)
    grid_spec = pltpu.PrefetchScalarGridSpec(
        num_scalar_prefetch=3,
        grid=(n_slots // blk,),
        in_specs=[pl.BlockSpec((blk, D_MODEL),
                               lambda i, be, nu, nv: (jnp.minimum(i, nu[0] - 1), 0)),
                  pl.BlockSpec((None, None, D_MODEL, D_EXPERT), w_map),
                  pl.BlockSpec((None, None, D_MODEL, D_EXPERT), w_map),
                  pl.BlockSpec((None, None, D_EXPERT, D_MODEL), w_map)],
        out_specs=pl.BlockSpec((blk, D_MODEL), lambda i, be, nu, nv: (i, 0)),
        scratch_shapes=[pltpu.VMEM((D_MODEL, D_EXPERT), BF16),
                        pltpu.VMEM((D_MODEL, D_EXPERT), BF16),
                        pltpu.VMEM((D_EXPERT, D_MODEL), BF16)],
    )
    return pl.pallas_call(
        _ffn_kernel,
        grid_spec=grid_spec,
        out_shape=jax.ShapeDtypeStruct((n_slots, D_MODEL), F32),
        compiler_params=_cparams(("arbitrary",)),
        name="expert_ffn",
    )(block_e, n_used, n_valid, xs, w_gate, w_up, w_down)


def _combine_kernel(dest_ref, dnext_ref, route_ref, x_ref, y_ref, g_ref, b_ref, o_ref, buf_ref, sem):
    tm = x_ref.shape[0]
    i = pl.program_id(0)
    slot = i % 2

    def issue(idx_ref, s):
        def body(r, carry):
            for k in range(2):
                d = idx_ref[0, 0, 2 * r + k]
                pltpu.make_async_copy(y_ref.at[pl.ds(d, 1)], buf_ref.at[s, k, pl.ds(r, 1)],
                                      sem.at[s]).start(priority=k)
            return carry

        lax.fori_loop(0, tm, body, 0, unroll=8)

    @pl.when(i == 0)
    def _():
        issue(dest_ref, slot)

    @pl.when(i + 1 < pl.num_programs(0))
    def _():
        issue(dnext_ref, 1 - slot)

    for k in range(2):
        pltpu.make_async_copy(y_ref.at[pl.ds(0, tm)], buf_ref.at[slot, k], sem.at[slot]).wait()
    moe = route_ref[:, 2:3] * buf_ref[slot, 0] + route_ref[:, 3:4] * buf_ref[slot, 1]
    h = ALPHA * x_ref[...] + moe
    o_ref[...] = _layer_norm(h, g_ref[...], b_ref[...])


def _combine(dest3, route, x2d, y, ln_g, ln_b):
    t = x2d.shape[0]
    tm = TOK_TILE
    row = lambda i: (i, 0)
    full = lambda i: (0, 0)
    last = t // tm - 1
    return pl.pallas_call(
        _combine_kernel,
        grid=(t // tm,),
        in_specs=[pl.BlockSpec((1, 1, 2 * tm), lambda i: (i, 0, 0), memory_space=pltpu.SMEM),
                  pl.BlockSpec((1, 1, 2 * tm), lambda i: (jnp.minimum(i + 1, last), 0, 0),
                               memory_space=pltpu.SMEM),
                  pl.BlockSpec((tm, LANES), row),
                  pl.BlockSpec((tm, D_MODEL), row),
                  pl.BlockSpec(memory_space=pl.ANY),
                  pl.BlockSpec((1, D_MODEL), full),
                  pl.BlockSpec((1, D_MODEL), full)],
        out_specs=pl.BlockSpec((tm, D_MODEL), row),
        out_shape=jax.ShapeDtypeStruct((t, D_MODEL), F32),
        scratch_shapes=[pltpu.VMEM((2, 2, tm, D_MODEL), F32), pltpu.SemaphoreType.DMA((2,))],
        compiler_params=_cparams(("arbitrary",)),
        name="moe_combine",
    )(dest3, dest3, route, x2d, y, ln_g, ln_b)


def _pad_cols(w, n):
    return jnp.pad(w, [(0, 0)] * (w.ndim - 1) + [(0, n - w.shape[-1])])


def kernel(x, w_in, conv_w, gla_w_lr, gla_b_lr, gla_norm_g, ssd_conv_w, ssd_conv_b, ssd_a_log,
           ssd_d, ssd_dt_bias, ssd_norm_g, diff_lq1, diff_lk1, diff_lq2, diff_lk2, diff_norm_g,
           w_o, ln1_g, ln1_b, router_g, router_e, w_gate, w_up, w_down, ln2_g, ln2_b):
    bsz, seq, d = x.shape
    t = bsz * seq
    n_assign = 2 * t
    n_blocks = (n_assign + N_EXPERTS * (MOE_BLK - 1)) // MOE_BLK + 1
    n_slots = n_blocks * MOE_BLK
    x2d = x.reshape(t, d)
    w_in_r = jnp.concatenate([w_in[..., 0:768], _pad_cols(w_in[..., 768:1552], 896),
                              _pad_cols(w_in[..., 1552:2580], 1152), w_in[..., 2580:3348]],
                             axis=-1).astype(BF16)
    w_o_b = w_o.astype(BF16)
    for l in range(DEPTH):
        pc, pg, ps, pd = _in_proj(x2d, w_in_r, l)

        y_conv = _conv_mixer(pc.reshape(bsz, seq, -1), conv_w[l])
        w_lr_pad = jnp.pad(gla_w_lr[l], ((0, LANES - GLA_RANK), (0, 0)))
        y_gla = _gla_mixer(pg.reshape(bsz, seq, -1), w_lr_pad, gla_b_lr[l].reshape(1, -1),
                           jnp.tile(gla_norm_g[l], GLA_HEADS).reshape(1, -1))
        pad4 = lambda v: jnp.pad(v, (0, LANES - SSD_HEADS)).reshape(1, LANES)
        y_ssd = _ssd_mixer(ps.reshape(bsz, seq, -1), ssd_conv_w[l], ssd_conv_b[l].reshape(1, -1),
                           pad4(ssd_a_log[l]), pad4(ssd_dt_bias[l]),
                           jnp.repeat(ssd_d[l], SSD_HEADDIM).reshape(1, -1),
                           ssd_norm_g[l].reshape(1, -1))
        lam_vecs = jnp.pad(jnp.stack([diff_lq1[l], diff_lk1[l], diff_lq2[l], diff_lk2[l]]),
                           ((0, 0), (0, LANES - DIFF_DQK)))
        lam_init = 0.8 - 0.6 * math.exp(-0.3 * l)
        y_diff = _diff_mixer(pd.reshape(bsz, seq, -1), lam_vecs,
                             jnp.tile(diff_norm_g[l], DIFF_HEADS).reshape(1, -1), lam_init)

        w_route = _pad_cols(jnp.concatenate(
            [router_g[l], router_e[l].reshape(d, N_EXPERTS)], axis=1), LANES)
        w_route_hi = w_route.astype(BF16)
        w_route = jnp.concatenate(
            [w_route_hi, (w_route - w_route_hi.astype(F32)).astype(BF16)], axis=1)
        ys = [y.reshape(t, W_MIX) for y in (y_conv, y_gla, y_ssd, y_diff)]
        xn, route, cnt = _out_proj(ys, x2d, w_o_b, l, ln1_g[l].reshape(1, -1),
                                   ln1_b[l].reshape(1, -1), w_route)

        dest3, block_e, n_used, n_valid, pads = _dispatch_plan(route, cnt, n_blocks)
        xs = _dispatch(xn, dest3, pads, n_slots)
        y = _expert_ffn(xs, block_e, n_used, n_valid, w_gate, w_up, w_down, l)
        x2d = _combine(dest3, route, xn, y, ln2_g[l].reshape(1, -1), ln2_b[l].reshape(1, -1))
    return x2d.reshape(bsz, seq, d)
```

```python
import functools
import math

import jax
import jax.numpy as jnp
from jax import lax
from jax.experimental import pallas as pl
from jax.experimental.pallas import tpu as pltpu
from jax.experimental.pallas import tpu_sc as plsc

F32 = jnp.float32
BF16 = jnp.bfloat16
HI = lax.Precision.HIGHEST

D_MODEL = 1024
DEPTH = 2
W_MIX = 256
GLA_HEADS, GLA_DK, GLA_DV, GLA_RANK, GLA_TAU, GLA_CHUNK = 4, 32, 64, 16, 16.0, 64
GLA_ROWS = 256
SSD_HEADS, SSD_GROUPS, SSD_HEADDIM, SSD_STATE, SSD_CONV_K, SSD_CHUNK = 4, 2, 64, 128, 4, 128
DIFF_HEADS, DIFF_DQK, DIFF_DV = 4, 32, 64
N_GROUPS, EXPERTS_PER_GROUP, N_EXPERTS, D_EXPERT = 4, 8, 32, 512
ALPHA = (2 * DEPTH) ** 0.25
LN_EPS = 1e-5
RMS_EPS = 1e-6

LANES = 128
SUBLANES = 8
PROJ_WIDTHS = (768, 896, 1152, 768)
VMEM_LIMIT = 56 * 1024 * 1024

MOE_BLK = 512
TOK_TILE = 256


def _cparams(sem):
    return pltpu.CompilerParams(dimension_semantics=sem, vmem_limit_bytes=VMEM_LIMIT)


def _sigmoid(x):
    return 1.0 / (1.0 + jnp.exp(-x))


def _softplus(x):
    return jnp.maximum(x, 0.0) + jnp.log(1.0 + jnp.exp(-jnp.abs(x)))


def _layer_norm(h, g, b):
    mu = jnp.mean(h, axis=-1, keepdims=True)
    d = h - mu
    var = jnp.mean(d * d, axis=-1, keepdims=True)
    return d * lax.rsqrt(var + LN_EPS) * g + b


def _dot_nt(a, b):
    return lax.dot_general(a, b, (((1,), (1,)), ((), ())), preferred_element_type=F32)


def _dot_tn(a, b, precision=None):
    return lax.dot_general(a, b, (((0,), (0,)), ((), ())), preferred_element_type=F32,
                           precision=precision)


def _split_bf16(x, parts):
    out = []
    for _ in range(parts - 1):
        hi = x.astype(BF16)
        out.append(hi)
        x = x - hi.astype(F32)
    out.append(x.astype(BF16))
    return out


def _dot(a, b):
    return jnp.dot(a, b, preferred_element_type=F32)


def _dot_split_lhs(a, b_exact, parts, dot=_dot):
    acc = None
    for term in _split_bf16(a, parts):
        d = dot(term, b_exact)
        acc = d if acc is None else acc + d
    return acc


def _dot_split_rhs(a_exact, b, parts):
    acc = None
    for term in _split_bf16(b, parts):
        d = jnp.dot(a_exact, term, preferred_element_type=F32)
        acc = d if acc is None else acc + d
    return acc


def _proj_kernel(x_ref, w_ref, oc_ref, og_ref, os_ref, od_ref):
    xb = x_ref[...].astype(BF16)
    off = 0
    for o_ref in (oc_ref, og_ref, os_ref, od_ref):
        n = o_ref.shape[-1]
        o_ref[...] = jnp.dot(xb, w_ref[:, off:off + n], preferred_element_type=F32)
        off += n


def _in_proj(x2d, w_r, layer):
    t = x2d.shape[0]
    tm = 512
    ncol = sum(PROJ_WIDTHS)
    return pl.pallas_call(
        _proj_kernel,
        grid=(t // tm,),
        in_specs=[pl.BlockSpec((tm, D_MODEL), lambda i: (i, 0)),
                  pl.BlockSpec((None, D_MODEL, ncol), lambda i: (layer, 0, 0))],
        out_specs=[pl.BlockSpec((tm, n), lambda i: (i, 0)) for n in PROJ_WIDTHS],
        out_shape=[jax.ShapeDtypeStruct((t, n), F32) for n in PROJ_WIDTHS],
        compiler_params=_cparams(("parallel",)),
        name="in_proj",
    )(x2d, w_r)


def _conv_kernel(p_ref, w_ref, o_ref):
    u = p_ref[0, :, 0:W_MIX]
    gb = p_ref[0, :, W_MIX:2 * W_MIX]
    gc = p_ref[0, :, 2 * W_MIX:3 * W_MIX]
    cu = gc * u
    row = lax.broadcasted_iota(jnp.int32, cu.shape, 0)
    acc = cu * w_ref[2:3, :]
    for s in (1, 2):
        sh = jnp.where(row >= s, pltpu.roll(cu, s, axis=0), 0.0)
        acc = acc + sh * w_ref[2 - s:3 - s, :]
    o_ref[0] = (gb * acc).astype(o_ref.dtype)


def _conv_mixer(pc, conv_w):
    b, s, _ = pc.shape
    return pl.pallas_call(
        _conv_kernel,
        grid=(b,),
        in_specs=[pl.BlockSpec((1, s, 3 * W_MIX), lambda i: (i, 0, 0)),
                  pl.BlockSpec((3, W_MIX), lambda i: (0, 0))],
        out_specs=pl.BlockSpec((1, s, W_MIX), lambda i: (i, 0, 0)),
        out_shape=jax.ShapeDtypeStruct((b, s, W_MIX), BF16),
        compiler_params=_cparams(("parallel",)),
        name="conv_mixer",
    )(pc, conv_w)


def _gla_kernel(p_ref, wlr_ref, blr_ref, ng_ref, o_ref, st_ref):
    c = GLA_CHUNK
    s_len = p_ref.shape[1]
    nh, dk, dv = GLA_HEADS, GLA_DK, GLA_DV
    st_ref[...] = jnp.zeros_like(st_ref)

    rb = GLA_ROWS
    ncb = rb // c
    ri = lax.broadcasted_iota(jnp.int32, (rb, rb), 0)
    ci = lax.broadcasted_iota(jnp.int32, (rb, rb), 1)
    tri = (ci <= ri).astype(BF16)
    klane_head = lax.broadcasted_iota(jnp.int32, (1, nh * dk), 1) // dk
    vlane_head = lax.broadcasted_iota(jnp.int32, (1, nh * dv), 1) // dv
    strow_head = lax.broadcasted_iota(jnp.int32, (nh * dv, 1), 0) // dv
    st_mask = strow_head == klane_head
    r4 = lax.broadcasted_iota(jnp.int32, (nh * c, c), 0) % c
    c4 = lax.broadcasted_iota(jnp.int32, (nh * c, c), 1)
    causal4 = c4 <= r4
    gi = lax.broadcasted_iota(jnp.int32, (nh * dv, nh * dv), 0) // dv
    gj = lax.broadcasted_iota(jnp.int32, (nh * dv, nh * dv), 1) // dv
    gmean = jnp.where(gi == gj, 1.0 / dv, 0.0).astype(BF16)
    wlr_hi, wlr_lo = _split_bf16(wlr_ref[...], 2)

    def body(n, carry):
        r0 = pl.multiple_of(n * rb, rb)
        rows = pl.ds(r0, rb)
        q = p_ref[0, rows, 0:128] * (dk ** -0.5)
        k = p_ref[0, rows, 128:256]
        vb = p_ref[0, rows, 256:512].astype(BF16)
        g = p_ref[0, rows, 512:768]
        lr = p_ref[0, rows, 768:896]
        lr_hi, lr_lo = _split_bf16(lr, 2)
        z = (jnp.dot(lr_hi, wlr_hi, preferred_element_type=F32)
             + jnp.dot(lr_hi, wlr_lo, preferred_element_type=F32)
             + jnp.dot(lr_lo, wlr_hi, preferred_element_type=F32)) + blr_ref[...]
        log_a = (jnp.minimum(z, 0.0) - jnp.log(1.0 + jnp.exp(-jnp.abs(z)))) * (1.0 / GLA_TAU)
        cumb = _dot_split_rhs(tri, log_a, 3)
        ends = [cumb[(j + 1) * c - 1:(j + 1) * c, :] for j in range(ncb)]
        starts = [jnp.zeros_like(ends[0])] + ends[:-1]
        cum = cumb - jnp.concatenate([jnp.broadcast_to(s0, (c, nh * dk)) for s0 in starts], axis=0)
        lasts = [e - s0 for e, s0 in zip(ends, starts)]
        cl = jnp.concatenate([jnp.broadcast_to(x, (c, nh * dk)) for x in lasts], axis=0)
        q_dec = q * jnp.exp(cum)
        k_inv = (k * jnp.exp(-cum)).astype(BF16)
        k_end = (k * jnp.exp(cl - cum)).astype(BF16)
        st = st_ref[...]
        outs = []
        for j in range(ncb):
            sl = slice(j * c, (j + 1) * c)
            qd = q_dec[sl]
            qs = jnp.concatenate([jnp.where(klane_head == h, qd, 0.0) for h in range(nh)],
                                 axis=0).astype(BF16)
            att = jnp.where(causal4, _dot_nt(qs, k_inv[sl]), 0.0)
            r = jnp.dot(att.astype(BF16), vb[sl], preferred_element_type=F32)
            o = jnp.where(vlane_head == 0, r[0:c], 0.0)
            for h in range(1, nh):
                o = o + jnp.where(vlane_head == h, r[h * c:(h + 1) * c], 0.0)
            outs.append(o + _dot_nt(qd.astype(BF16), st.astype(BF16)))
            d_st = _dot_tn(vb[sl], k_end[sl])
            st = st * jnp.exp(lasts[j]) + jnp.where(st_mask, d_st, 0.0)
        st_ref[...] = st
        o = jnp.concatenate(outs, axis=0)
        ms = _dot_split_lhs(o * o, gmean, 2)
        o = o * lax.rsqrt(ms + RMS_EPS) * ng_ref[...]
        o_ref[0, rows, :] = (o * (g * _sigmoid(g))).astype(o_ref.dtype)
        return carry

    lax.fori_loop(0, s_len // rb, body, 0)


def _gla_mixer(pg, w_lr_pad, b_lr, norm_g4):
    b, s, wp = pg.shape
    return pl.pallas_call(
        _gla_kernel,
        grid=(b,),
        in_specs=[pl.BlockSpec((1, s, wp), lambda i: (i, 0, 0)),
                  pl.BlockSpec((LANES, LANES), lambda i: (0, 0)),
                  pl.BlockSpec((1, LANES), lambda i: (0, 0)),
                  pl.BlockSpec((1, W_MIX), lambda i: (0, 0))],
        out_specs=pl.BlockSpec((1, s, W_MIX), lambda i: (i, 0, 0)),
        out_shape=jax.ShapeDtypeStruct((b, s, W_MIX), BF16),
        scratch_shapes=[pltpu.VMEM((GLA_HEADS * GLA_DV, GLA_HEADS * GLA_DK), F32)],
        compiler_params=_cparams(("parallel",)),
        name="gla_mixer",
    )(pg, w_lr_pad, b_lr, norm_g4)


def _ssd_kernel(p_ref, cw_ref, cb_ref, alog_ref, dtb_ref, dsk_ref, ng_ref, o_ref, st_ref):
    c = SSD_CHUNK
    s_len = p_ref.shape[1]
    n_st = SSD_STATE
    st_ref[...] = jnp.zeros_like(st_ref)

    ri = lax.broadcasted_iota(jnp.int32, (c, c), 0)
    ci = lax.broadcasted_iota(jnp.int32, (c, c), 1)
    causal = ci <= ri
    tri = causal.astype(BF16)
    upper = (ri <= ci).astype(BF16)
    lane_head = lax.broadcasted_iota(jnp.int32, (1, W_MIX), 1) // SSD_HEADDIM
    lane_group = lane_head // (SSD_HEADS // SSD_GROUPS)
    eh = lax.broadcasted_iota(jnp.int32, (LANES, W_MIX), 0)
    el = lax.broadcasted_iota(jnp.int32, (LANES, W_MIX), 1) // SSD_HEADDIM
    expand = (eh == el).astype(BF16)
    row8 = lax.broadcasted_iota(jnp.int32, (8, 3 * W_MIX), 0)
    a_c = -jnp.exp(alog_ref[...])

    def body(n, carry):
        r0 = pl.multiple_of(n * c, c)
        rows = pl.ds(r0, c)
        cur = p_ref[0, rows, 256:1024]
        p0 = pl.multiple_of(jnp.maximum(r0 - 8, 0), 8)
        prev8 = p_ref[0, pl.ds(p0, 8), 256:1024]
        prev8 = jnp.where(n > 0, prev8, 0.0)
        acc = cur * cw_ref[3:4, :] + cb_ref[...]
        for s in (1, 2, 3):
            sh = pltpu.roll(cur, s, axis=0)
            top = jnp.where(row8 < s, pltpu.roll(prev8, s, axis=0), sh[0:8])
            sh = jnp.concatenate([top, sh[8:]], axis=0)
            acc = acc + sh * cw_ref[3 - s:4 - s, :]
        xbc = acc * _sigmoid(acc)
        x = xbc[:, 0:256]
        bm = xbc[:, 256:512].astype(BF16)
        cm = xbc[:, 512:768].astype(BF16)

        dt_c = _softplus(p_ref[0, rows, 1024:1152] + dtb_ref[...])
        da_c = dt_c * a_c
        cum_c = _dot_split_rhs(tri, da_c, 3)
        cum_r = _dot_split_lhs(da_c, upper, 3, dot=_dot_tn)
        both_x = _dot_split_lhs(jnp.concatenate([dt_c, cum_c], axis=0), expand, 3)
        dt_x = both_x[0:c]
        cum_x = both_x[c:2 * c]
        cl_x = cum_x[c - 1:c, :]
        x_dt = x * dt_x
        x_dt_b = x_dt.astype(BF16)
        xw_b = (x_dt * jnp.exp(cl_x - cum_x)).astype(BF16)

        y = x * dsk_ref[...]
        y_off = jnp.zeros((c, W_MIX), F32)
        for g in range(SSD_GROUPS):
            bg = bm[:, g * n_st:(g + 1) * n_st]
            cg = cm[:, g * n_st:(g + 1) * n_st]
            cb = _dot_nt(cg, bg)
            for r in range(SSD_HEADS // SSD_GROUPS):
                h = g * (SSD_HEADS // SSD_GROUPS) + r
                diff = cum_c[:, h:h + 1] - cum_r[h:h + 1, :]
                dec = jnp.exp(jnp.where(causal, diff, -jnp.inf))
                m = (cb * dec).astype(BF16)
                yh = jnp.dot(m, x_dt_b, preferred_element_type=F32)
                y = y + jnp.where(lane_head == h, yh, 0.0)
            st = st_ref[g]
            y_off = y_off + jnp.where(lane_group == g,
                                      jnp.dot(cg, st.astype(BF16), preferred_element_type=F32), 0.0)
            d_st = _dot_tn(bg, xw_b)
            st_ref[g] = st * jnp.exp(cl_x) + jnp.where(lane_group == g, d_st, 0.0)
        y = y + y_off * jnp.exp(cum_x)
        zg = p_ref[0, rows, 0:256]
        y = y * (zg * _sigmoid(zg))
        outs = []
        for g in range(SSD_GROUPS):
            yg = y[:, g * 128:(g + 1) * 128]
            ms = jnp.mean(yg * yg, axis=-1, keepdims=True)
            outs.append(yg * lax.rsqrt(ms + RMS_EPS))
        o_ref[0, rows, :] = (jnp.concatenate(outs, axis=-1) * ng_ref[...]).astype(o_ref.dtype)
        return carry

    def pair(u, carry):
        body(2 * u, carry)
        return body(2 * u + 1, carry)

    lax.fori_loop(0, s_len // (2 * c), pair, 0)


def _ssd_mixer(ps, conv_w, conv_b, a_log_c, dt_bias_c, d_x, norm_g):
    b, s, wp = ps.shape
    full2 = lambda i: (0, 0)
    return pl.pallas_call(
        _ssd_kernel,
        grid=(b,),
        in_specs=[pl.BlockSpec((1, s, wp), lambda i: (i, 0, 0)),
                  pl.BlockSpec((SSD_CONV_K, 3 * W_MIX), full2),
                  pl.BlockSpec((1, 3 * W_MIX), full2),
                  pl.BlockSpec((1, LANES), full2),
                  pl.BlockSpec((1, LANES), full2),
                  pl.BlockSpec((1, W_MIX), full2),
                  pl.BlockSpec((1, W_MIX), full2)],
        out_specs=pl.BlockSpec((1, s, W_MIX), lambda i: (i, 0, 0)),
        out_shape=jax.ShapeDtypeStruct((b, s, W_MIX), BF16),
        scratch_shapes=[pltpu.VMEM((SSD_GROUPS, SSD_STATE, W_MIX), F32)],
        compiler_params=_cparams(("parallel",)),
        name="ssd_mixer",
    )(ps, conv_w, conv_b, a_log_c, dt_bias_c, d_x, norm_g)


DIFF_TQ = 256
DIFF_TK = 256
LOG2E = 1.4426950408889634
DIFF_VPAD = DIFF_DV + 16


def _diff_kernel(q_ref, k_ref, v_ref, lam_ref, ng_ref, o_ref,
                 kb_ref, vt_ref, qs_ref, st_ref, m_ref, acc_ref, *, lam_init):
    tq, tk = DIFF_TQ, DIFF_TK
    nh, dv = DIFF_HEADS, DIFF_DV
    nhc = 2 * nh
    s_len = k_ref.shape[1]
    i = pl.program_id(1)

    @pl.when(i == 0)
    def _():
        kb_ref[...] = k_ref[0].astype(BF16)
        for cblk in range(s_len // tk):
            cols = slice(cblk * tk, (cblk + 1) * tk)
            vt = v_ref[0, cols, :].T.astype(BF16)
            for h in range(nh):
                vt_ref[h, 0:dv, cols] = vt[h * dv:(h + 1) * dv]
        vt_ref[:, dv:, :] = jnp.ones((nh, DIFF_VPAD - dv, s_len), BF16)

    q = q_ref[0] * (DIFF_DQK ** -0.5 * LOG2E)
    qlane = lax.broadcasted_iota(jnp.int32, (1, W_MIX), 1) // DIFF_DQK
    for hc in range(nhc):
        qs_ref[hc * tq:(hc + 1) * tq, :] = jnp.where(qlane == hc, q, 0.0).astype(BF16)
    m_ref[...] = jnp.full_like(m_ref, -jnp.inf)
    acc_ref[...] = jnp.zeros_like(acc_ref)
    krow = lax.broadcasted_iota(jnp.int32, (tk, nhc * tq), 0)
    qcol = lax.broadcasted_iota(jnp.int32, (tk, nhc * tq), 1) % tq
    diag_ok = krow <= qcol

    def scores(j, slot):
        k0 = pl.multiple_of(j * tk, tk)
        st_ref[slot] = _dot_nt(kb_ref[pl.ds(k0, tk), :], qs_ref[...])

    def softmax_pv(j, slot, masked):
        k0 = pl.multiple_of(j * tk, tk)
        st = st_ref[slot]
        if masked:
            st = jnp.where(diag_ok, st, -jnp.inf)
        m_prev = m_ref[...]
        m_new = jnp.maximum(m_prev, jnp.max(st, axis=0, keepdims=True))
        alpha = jnp.exp2(m_prev - m_new)
        p = jnp.exp2(st - m_new)
        m_ref[...] = m_new
        pb = p.astype(BF16)
        for hc in range(nhc):
            h = hc // 2
            lanes = slice(hc * tq, (hc + 1) * tq)
            pv = jnp.dot(vt_ref[h, :, pl.ds(k0, tk)], pb[:, lanes],
                         preferred_element_type=F32)
            acc_ref[hc] = acc_ref[hc] * alpha[:, lanes] + pv

    scores(0, 0)
    n_pairs = i // 2

    def pair_step(u, carry):
        scores(2 * u + 1, 1)
        softmax_pv(2 * u, 0, False)
        scores(2 * u + 2, 0)
        softmax_pv(2 * u + 1, 1, False)
        return carry

    lax.fori_loop(0, n_pairs, pair_step, 0)

    @pl.when(i % 2 == 0)
    def _():
        softmax_pv(i, 0, True)

    @pl.when(i % 2 == 1)
    def _():
        scores(i, 1)
        softmax_pv(i - 1, 0, False)
        softmax_pv(i, 1, True)

    lam = (jnp.exp(jnp.sum(lam_ref[0:1, :] * lam_ref[1:2, :], axis=-1, keepdims=True))
           - jnp.exp(jnp.sum(lam_ref[2:3, :] * lam_ref[3:4, :], axis=-1, keepdims=True))
           + lam_init)
    heads = []
    for h in range(nh):
        o1 = acc_ref[2 * h, 0:dv] / acc_ref[2 * h, dv:dv + 1]
        o2 = acc_ref[2 * h + 1, 0:dv] / acc_ref[2 * h + 1, dv:dv + 1]
        oh = o1 - lam * o2
        ms = jnp.mean(oh * oh, axis=0, keepdims=True)
        heads.append(oh * lax.rsqrt(ms + RMS_EPS))
    o = jnp.concatenate(heads, axis=0).T
    o_ref[0] = (o * ng_ref[...] * (1.0 - lam_init)).astype(o_ref.dtype)


def _diff_mixer(pd, lam_vecs, norm_g4, lam_init):
    b, s, _ = pd.shape
    tq = DIFF_TQ
    return pl.pallas_call(
        functools.partial(_diff_kernel, lam_init=lam_init),
        grid=(b, s // tq),
        in_specs=[pl.BlockSpec((1, tq, W_MIX), lambda bi, i: (bi, i, 0)),
                  pl.BlockSpec((1, s, W_MIX), lambda bi, i: (bi, 0, 1)),
                  pl.BlockSpec((1, s, W_MIX), lambda bi, i: (bi, 0, 2)),
                  pl.BlockSpec((4, LANES), lambda bi, i: (0, 0)),
                  pl.BlockSpec((1, W_MIX), lambda bi, i: (0, 0))],
        out_specs=pl.BlockSpec((1, tq, W_MIX), lambda bi, i: (bi, i, 0)),
        out_shape=jax.ShapeDtypeStruct((b, s, W_MIX), BF16),
        scratch_shapes=[pltpu.VMEM((s, W_MIX), BF16),
                        pltpu.VMEM((DIFF_HEADS, DIFF_VPAD, s), BF16),
                        pltpu.VMEM((2 * DIFF_HEADS * tq, W_MIX), BF16),
                        pltpu.VMEM((2, DIFF_TK, 2 * DIFF_HEADS * tq), F32),
                        pltpu.VMEM((1, 2 * DIFF_HEADS * tq), F32),
                        pltpu.VMEM((2 * DIFF_HEADS, DIFF_VPAD, tq), F32)],
        compiler_params=_cparams(("parallel", "arbitrary")),
        name="diff_attn",
    )(pd, pd, pd, lam_vecs, norm_g4)


def _oproj_kernel(yc_ref, yg_ref, ys_ref, yd_ref, x_ref, wo_ref, g_ref, b_ref, wr_ref,
                  xo_ref, route_ref, cnt_ref):
    mix = jnp.concatenate([yc_ref[...], yg_ref[...], ys_ref[...], yd_ref[...]], axis=-1)
    h = ALPHA * x_ref[...] + jnp.dot(mix, wo_ref[...], preferred_element_type=F32)
    xn = _layer_norm(h, g_ref[...], b_ref[...])
    xo_ref[...] = xn

    xn_hi, xn_lo = _split_bf16(xn, 2)
    both = _dot(xn_hi, wr_ref[...])
    logits = both[:, 0:LANES] + both[:, LANES:2 * LANES] + _dot(xn_lo, wr_ref[:, 0:LANES])
    lane = lax.broadcasted_iota(jnp.int32, logits.shape, 1).astype(F32)
    neg = -jnp.inf
    big = float(LANES)
    lg = jnp.where(lane < N_GROUPS, logits, neg)
    mg = jnp.max(lg, axis=-1, keepdims=True)
    sg = jnp.sum(jnp.exp(lg - mg), axis=-1, keepdims=True)
    grp = jnp.min(jnp.where(lg == mg, lane, big), axis=-1, keepdims=True)
    p_grp = 1.0 / sg
    lo = N_GROUPS + EXPERTS_PER_GROUP * grp
    in_g = jnp.logical_and(lane >= lo, lane < lo + EXPERTS_PER_GROUP)
    le = jnp.where(in_g, logits, neg)
    me = jnp.max(le, axis=-1, keepdims=True)
    ee = jnp.exp(le - me)
    pe = ee / jnp.sum(ee, axis=-1, keepdims=True)
    pe = jnp.where(in_g, pe, -1.0)
    p1 = jnp.max(pe, axis=-1, keepdims=True)
    i1 = jnp.min(jnp.where(pe == p1, lane, big), axis=-1, keepdims=True)
    pe2 = jnp.where(lane == i1, -1.0, pe)
    p2 = jnp.max(pe2, axis=-1, keepdims=True)
    i2 = jnp.min(jnp.where(pe2 == p2, lane, big), axis=-1, keepdims=True)
    den = p1 + p2
    g1 = p_grp * p1 / den
    g2 = p_grp * p2 / den
    e1 = i1 - N_GROUPS
    e2 = i2 - N_GROUPS
    route_ref[...] = jnp.where(lane == 0, e1, jnp.where(lane == 1, e2, jnp.where(
        lane == 2, g1, jnp.where(lane == 3, g2, 0.0))))

    @pl.when(pl.program_id(0) == 0)
    def _():
        cnt_ref[...] = jnp.zeros_like(cnt_ref)

    hits = jnp.where(lane == e1, 1.0, 0.0) + jnp.where(lane == e2, 1.0, 0.0)
    cnt_ref[...] += jnp.sum(hits, axis=0, keepdims=True)


def _out_proj(ys, x2d, w_o, layer, ln_g, ln_b, w_route):
    t = x2d.shape[0]
    tm = 512
    row = lambda i: (i, 0)
    full = lambda i: (0, 0)
    return pl.pallas_call(
        _oproj_kernel,
        grid=(t // tm,),
        in_specs=[pl.BlockSpec((tm, W_MIX), row)] * 4 + [
            pl.BlockSpec((tm, D_MODEL), row),
            pl.BlockSpec((None, D_MODEL, D_MODEL), lambda i: (layer, 0, 0)),
            pl.BlockSpec((1, D_MODEL), full),
            pl.BlockSpec((1, D_MODEL), full),
            pl.BlockSpec((D_MODEL, 2 * LANES), full)],
        out_specs=[pl.BlockSpec((tm, D_MODEL), row), pl.BlockSpec((tm, LANES), row),
                   pl.BlockSpec((1, LANES), full)],
        out_shape=[jax.ShapeDtypeStruct((t, D_MODEL), F32),
                   jax.ShapeDtypeStruct((t, LANES), F32),
                   jax.ShapeDtypeStruct((1, LANES), F32)],
        compiler_params=_cparams(("arbitrary",)),
        name="out_proj_ln_router",
    )(*ys, x2d, w_o, ln_g, ln_b, w_route)


PLAN_TILE = 512


def _plan_kernel(route_ref, cnt_ref, dest_ref, meta_ref, carry_ref, pstart_ref):
    tm = route_ref.shape[0]
    lane = lax.broadcasted_iota(jnp.int32, (1, LANES), 1).astype(F32)

    @pl.when(pl.program_id(0) == 0)
    def _():
        cnt = cnt_ref[...]
        padded = jnp.ceil(cnt * (1.0 / MOE_BLK)) * MOE_BLK
        li = lax.broadcasted_iota(jnp.int32, (LANES, LANES), 0)
        lj = lax.broadcasted_iota(jnp.int32, (LANES, LANES), 1)
        before = (li < lj).astype(F32)
        pstart = jnp.dot(jnp.broadcast_to(padded, (8, LANES)), before, precision=HI,
                         preferred_element_type=F32)[0:1]
        pstart_ref[...] = pstart
        carry_ref[...] = jnp.zeros_like(carry_ref)
        meta_ref[...] = jnp.concatenate(
            [pstart + padded, pstart, cnt, jnp.zeros((5, LANES), F32)], axis=0)

    oh0 = jnp.where(lane == route_ref[:, 0:1], 1.0, 0.0)
    oh1 = jnp.where(lane == route_ref[:, 1:2], 1.0, 0.0)
    both = oh0 + oh1
    ri = lax.broadcasted_iota(jnp.int32, (tm, tm), 0)
    ci = lax.broadcasted_iota(jnp.int32, (tm, tm), 1)
    earlier = (ci < ri).astype(BF16)
    base = (jnp.dot(earlier, both.astype(BF16), preferred_element_type=F32)
            + carry_ref[...] + pstart_ref[...])
    d0 = jnp.sum(oh0 * base, axis=-1, keepdims=True)
    d1 = jnp.sum(oh1 * base, axis=-1, keepdims=True)
    dest_ref[...] = jnp.where(lane == 0, d0, jnp.where(lane == 1, d1, 0.0))
    carry_ref[...] += jnp.sum(both, axis=0, keepdims=True)


def _dispatch_plan(route, cnt, n_blocks):
    t = route.shape[0]
    tm = PLAN_TILE
    blk = MOE_BLK
    dest, meta = pl.pallas_call(
        _plan_kernel,
        grid=(t // tm,),
        in_specs=[pl.BlockSpec((tm, LANES), lambda i: (i, 0)),
                  pl.BlockSpec((1, LANES), lambda i: (0, 0))],
        out_specs=[pl.BlockSpec((tm, LANES), lambda i: (i, 0)),
                   pl.BlockSpec((8, LANES), lambda i: (0, 0))],
        out_shape=[jax.ShapeDtypeStruct((t, LANES), F32),
                   jax.ShapeDtypeStruct((8, LANES), F32)],
        scratch_shapes=[pltpu.VMEM((1, LANES), F32), pltpu.VMEM((1, LANES), F32)],
        compiler_params=_cparams(("arbitrary",)),
        name="moe_plan",
    )(route, cnt)
    pad_end = meta[0, :N_EXPERTS].astype(jnp.int32)
    starts = jnp.arange(n_blocks, dtype=jnp.int32) * blk
    block_e = jnp.minimum(jnp.sum((pad_end[None, :] <= starts[:, None]).astype(jnp.int32), axis=1),
                          N_EXPERTS - 1)
    n_used = (pad_end[N_EXPERTS - 1] // blk).reshape(1)
    seg_end = (meta[1, :N_EXPERTS] + meta[2, :N_EXPERTS]).astype(jnp.int32)
    n_valid = jnp.clip(seg_end[block_e] - starts, 0, blk)
    pads = _pad_cols(jnp.stack([seg_end, pad_end]), LANES)
    dest3 = dest[:, 0:2].astype(jnp.int32).reshape(t // TOK_TILE, 1, 2 * TOK_TILE)
    return dest3, block_e, n_used, n_valid, pads


ZERO_ROWS = MOE_BLK // 2


def _scatter_kernel(pads_ref, dest_ref, x_ref, xs_ref, zero_ref, sem, zsem):
    tm = x_ref.shape[0]
    n_slots = xs_ref.shape[0]

    def zero_fill(wait):
        def go(src, dst):
            cp = pltpu.make_async_copy(src, dst, zsem)
            if wait:
                cp.wait()
            else:
                cp.start()

        def per_expert(e, carry):
            pos = pads_ref[0, e]
            head = (-pos) & (SUBLANES - 1)
            for r in range(SUBLANES - 1):
                @pl.when(r < head)
                def _():
                    go(zero_ref.at[pl.ds(0, 1)], xs_ref.at[pl.ds(pos + r, 1)])

            base = pos + head
            nrem = pads_ref[1, e] - base
            p = ZERO_ROWS
            while p >= SUBLANES:
                off = pl.multiple_of(base + (nrem & (-2 * p)), SUBLANES)

                @pl.when((nrem & p) != 0)
                def _():
                    go(zero_ref.at[pl.ds(0, p)], xs_ref.at[pl.ds(off, p)])

                p //= 2
            return carry

        lax.fori_loop(0, N_EXPERTS, per_expert, 0)
        tail0 = pads_ref[1, N_EXPERTS - 1]

        def tail(b, carry):
            row0 = pl.multiple_of(tail0 + b * ZERO_ROWS, ZERO_ROWS)
            go(zero_ref, xs_ref.at[pl.ds(row0, ZERO_ROWS)])
            return carry

        lax.fori_loop(0, (n_slots - tail0) // ZERO_ROWS, tail, 0)

    @pl.when(pl.program_id(0) == 0)
    def _():
        zero_ref[...] = jnp.zeros_like(zero_ref)
        zero_fill(False)
        zero_fill(True)

    def issue(r, carry):
        for k in range(2):
            d = dest_ref[0, 0, 2 * r + k]
            pltpu.make_async_copy(x_ref.at[pl.ds(r, 1)], xs_ref.at[pl.ds(d, 1)],
                                  sem).start(priority=k)
        return carry

    lax.fori_loop(0, tm, issue, 0, unroll=8)
    for k in range(2):
        pltpu.make_async_copy(x_ref, xs_ref.at[pl.ds(0, tm)], sem).wait()


def _dispatch(x2d, dest3, pads, n_slots):
    t = x2d.shape[0]
    tm = TOK_TILE
    return pl.pallas_call(
        _scatter_kernel,
        grid=(t // tm,),
        in_specs=[pl.BlockSpec(memory_space=pltpu.SMEM),
                  pl.BlockSpec((1, 1, 2 * tm), lambda i: (i, 0, 0), memory_space=pltpu.SMEM),
                  pl.BlockSpec((tm, D_MODEL), lambda i: (i, 0))],
        out_specs=pl.BlockSpec(memory_space=pl.ANY),
        out_shape=jax.ShapeDtypeStruct((n_slots, D_MODEL), F32),
        scratch_shapes=[pltpu.VMEM((ZERO_ROWS, D_MODEL), F32),
                        pltpu.SemaphoreType.DMA(()), pltpu.SemaphoreType.DMA(())],
        compiler_params=pltpu.CompilerParams(dimension_semantics=("arbitrary",),
                                             vmem_limit_bytes=VMEM_LIMIT,
                                             has_side_effects=True),
        name="moe_dispatch",
    )(pads, dest3, x2d)


def _ffn_kernel(be_ref, nu_ref, nv_ref, xs_ref, wg_ref, wu_ref, wd_ref, y_ref,
                wgb_ref, wub_ref, wdb_ref):
    i = pl.program_id(0)
    prev = be_ref[jnp.maximum(i - 1, 0)]

    @pl.when(jnp.logical_or(i == 0, be_ref[i] != prev))
    def _():
        wgb_ref[...] = wg_ref[...].astype(BF16)
        wub_ref[...] = wu_ref[...].astype(BF16)
        wdb_ref[...] = wd_ref[...].astype(BF16)

    @pl.when(i < nu_ref[0])
    def _():
        row = lax.broadcasted_iota(jnp.int32, (xs_ref.shape[0], 1), 0)
        xb = jnp.where(row < nv_ref[i], xs_ref[...], 0.0).astype(BF16)
        a = jnp.dot(xb, wgb_ref[...], preferred_element_type=F32)
        u = jnp.dot(xb, wub_ref[...], preferred_element_type=F32)
        h = (a * _sigmoid(a) * u).astype(BF16)
        y_ref[...] = jnp.dot(h, wdb_ref[...], preferred_element_type=F32)

    @pl.when(i >= nu_ref[0])
    def _():
        y_ref[...] = jnp.zeros_like(y_ref)


def _expert_ffn(xs, block_e, n_used, n_valid, w_gate, w_up, w_down, layer):
    n_slots = xs.shape[0]
    blk = MOE_BLK
    w_map = lambda i, be, nu, nv: (layer, be[i], 0, 0)
    grid_spec = pltpu.PrefetchScalarGridSpec(
        num_scalar_prefetch=3,
        grid=(n_slots // blk,),
        in_specs=[pl.BlockSpec((blk, D_MODEL),
                               lambda i, be, nu, nv: (jnp.minimum(i, nu[0] - 1), 0)),
                  pl.BlockSpec((None, None, D_MODEL, D_EXPERT), w_map),
                  pl.BlockSpec((None, None, D_MODEL, D_EXPERT), w_map),
                  pl.BlockSpec((None, None, D_EXPERT, D_MODEL), w_map)],
        out_specs=pl.BlockSpec((blk, D_MODEL), lambda i, be, nu, nv: (i, 0)),
        scratch_shapes=[pltpu.VMEM((D_MODEL, D_EXPERT), BF16),
                        pltpu.VMEM((D_MODEL, D_EXPERT), BF16),
                        pltpu.VMEM((D_EXPERT, D_MODEL), BF16)],
    )
    return pl.pallas_call(
        _ffn_kernel,
        grid_spec=grid_spec,
        out_shape=jax.ShapeDtypeStruct((n_slots, D_MODEL), F32),
        compiler_params=_cparams(("arbitrary",)),
        name="expert_ffn",
    )(block_e, n_used, n_valid, xs, w_gate, w_up, w_down)


def _combine_kernel(dest_ref, dnext_ref, route_ref, x_ref, y_ref, g_ref, b_ref, o_ref, buf_ref, sem):
    tm = x_ref.shape[0]
    i = pl.program_id(0)
    slot = i % 2

    def issue(idx_ref, s):
        def body(r, carry):
            for k in range(2):
                d = idx_ref[0, 0, 2 * r + k]
                pltpu.make_async_copy(y_ref.at[pl.ds(d, 1)], buf_ref.at[s, k, pl.ds(r, 1)],
                                      sem.at[s]).start(priority=k)
            return carry

        lax.fori_loop(0, tm, body, 0, unroll=8)

    @pl.when(i == 0)
    def _():
        issue(dest_ref, slot)

    @pl.when(i + 1 < pl.num_programs(0))
    def _():
        issue(dnext_ref, 1 - slot)

    for k in range(2):
        pltpu.make_async_copy(y_ref.at[pl.ds(0, tm)], buf_ref.at[slot, k], sem.at[slot]).wait()
    moe = route_ref[:, 2:3] * buf_ref[slot, 0] + route_ref[:, 3:4] * buf_ref[slot, 1]
    h = ALPHA * x_ref[...] + moe
    o_ref[...] = _layer_norm(h, g_ref[...], b_ref[...])


def _combine(dest3, route, x2d, y, ln_g, ln_b):
    t = x2d.shape[0]
    tm = TOK_TILE
    row = lambda i: (i, 0)
    full = lambda i: (0, 0)
    last = t // tm - 1
    return pl.pallas_call(
        _combine_kernel,
        grid=(t // tm,),
        in_specs=[pl.BlockSpec((1, 1, 2 * tm), lambda i: (i, 0, 0), memory_space=pltpu.SMEM),
                  pl.BlockSpec((1, 1, 2 * tm), lambda i: (jnp.minimum(i + 1, last), 0, 0),
                               memory_space=pltpu.SMEM),
                  pl.BlockSpec((tm, LANES), row),
                  pl.BlockSpec((tm, D_MODEL), row),
                  pl.BlockSpec(memory_space=pl.ANY),
                  pl.BlockSpec((1, D_MODEL), full),
                  pl.BlockSpec((1, D_MODEL), full)],
        out_specs=pl.BlockSpec((tm, D_MODEL), row),
        out_shape=jax.ShapeDtypeStruct((t, D_MODEL), F32),
        scratch_shapes=[pltpu.VMEM((2, 2, tm, D_MODEL), F32), pltpu.SemaphoreType.DMA((2,))],
        compiler_params=_cparams(("arbitrary",)),
        name="moe_combine",
    )(dest3, dest3, route, x2d, y, ln_g, ln_b)


SC_CORES = 2
SC_SUBCORES = 16
SC_ROWS = 32


def _sc_gather_rows(table, idx):
    b = idx.shape[0]
    d = table.shape[1]
    per_w = b // (SC_CORES * SC_SUBCORES)
    mesh = plsc.VectorSubcoreMesh(core_axis_name="c", subcore_axis_name="s")

    @functools.partial(
        pl.kernel, mesh=mesh,
        out_type=jax.ShapeDtypeStruct((b, d), table.dtype),
        scratch_types=[pltpu.VMEM((SC_ROWS,), jnp.int32),
                       pltpu.VMEM((SC_ROWS, d), table.dtype),
                       pltpu.SemaphoreType.DMA],
        name="sc_gather_rows",
    )
    def gather(table_hbm, idx_hbm, out_hbm, idx_v, rows_v, sem):
        wid = lax.axis_index("s") * SC_CORES + lax.axis_index("c")
        base = wid * per_w

        @pl.loop(0, per_w // SC_ROWS)
        def _(c):
            off = pl.multiple_of(base + c * SC_ROWS, SC_ROWS)
            pltpu.sync_copy(idx_hbm.at[pl.ds(off, SC_ROWS)], idx_v)
            pltpu.async_copy(table_hbm.at[idx_v], rows_v, sem).wait()
            pltpu.sync_copy(rows_v, out_hbm.at[pl.ds(off, SC_ROWS)])

    return gather(table, idx)


def _combine_dense_kernel(route_ref, x_ref, y2_ref, g_ref, b_ref, o_ref):
    moe = (route_ref[:, 2:3] * y2_ref[:, 0:D_MODEL]
           + route_ref[:, 3:4] * y2_ref[:, D_MODEL:2 * D_MODEL])
    h = ALPHA * x_ref[...] + moe
    o_ref[...] = _layer_norm(h, g_ref[...], b_ref[...])


def _combine_dense(route, x2d, y2, ln_g, ln_b):
    t = x2d.shape[0]
    tm = 512
    row = lambda i: (i, 0)
    full = lambda i: (0, 0)
    return pl.pallas_call(
        _combine_dense_kernel,
        grid=(t // tm,),
        in_specs=[pl.BlockSpec((tm, LANES), row),
                  pl.BlockSpec((tm, D_MODEL), row),
                  pl.BlockSpec((tm, 2 * D_MODEL), row),
                  pl.BlockSpec((1, D_MODEL), full),
                  pl.BlockSpec((1, D_MODEL), full)],
        out_specs=pl.BlockSpec((tm, D_MODEL), row),
        out_shape=jax.ShapeDtypeStruct((t, D_MODEL), F32),
        compiler_params=_cparams(("parallel",)),
        name="moe_combine_dense",
    )(route, x2d, y2, ln_g, ln_b)


def _pad_cols(w, n):
    return jnp.pad(w, [(0, 0)] * (w.ndim - 1) + [(0, n - w.shape[-1])])


def kernel(x, w_in, conv_w, gla_w_lr, gla_b_lr, gla_norm_g, ssd_conv_w, ssd_conv_b, ssd_a_log,
           ssd_d, ssd_dt_bias, ssd_norm_g, diff_lq1, diff_lk1, diff_lq2, diff_lk2, diff_norm_g,
           w_o, ln1_g, ln1_b, router_g, router_e, w_gate, w_up, w_down, ln2_g, ln2_b):
    bsz, seq, d = x.shape
    t = bsz * seq
    n_assign = 2 * t
    n_blocks = (n_assign + N_EXPERTS * (MOE_BLK - 1)) // MOE_BLK + 1
    n_slots = n_blocks * MOE_BLK
    x2d = x.reshape(t, d)
    w_in_r = jnp.concatenate([w_in[..., 0:768], _pad_cols(w_in[..., 768:1552], 896),
                              _pad_cols(w_in[..., 1552:2580], 1152), w_in[..., 2580:3348]],
                             axis=-1).astype(BF16)
    w_o_b = w_o.astype(BF16)
    for l in range(DEPTH):
        pc, pg, ps, pd = _in_proj(x2d, w_in_r, l)

        y_conv = _conv_mixer(pc.reshape(bsz, seq, -1), conv_w[l])
        w_lr_pad = jnp.pad(gla_w_lr[l], ((0, LANES - GLA_RANK), (0, 0)))
        y_gla = _gla_mixer(pg.reshape(bsz, seq, -1), w_lr_pad, gla_b_lr[l].reshape(1, -1),
                           jnp.tile(gla_norm_g[l], GLA_HEADS).reshape(1, -1))
        pad4 = lambda v: jnp.pad(v, (0, LANES - SSD_HEADS)).reshape(1, LANES)
        y_ssd = _ssd_mixer(ps.reshape(bsz, seq, -1), ssd_conv_w[l], ssd_conv_b[l].reshape(1, -1),
                           pad4(ssd_a_log[l]), pad4(ssd_dt_bias[l]),
                           jnp.repeat(ssd_d[l], SSD_HEADDIM).reshape(1, -1),
                           ssd_norm_g[l].reshape(1, -1))
        lam_vecs = jnp.pad(jnp.stack([diff_lq1[l], diff_lk1[l], diff_lq2[l], diff_lk2[l]]),
                           ((0, 0), (0, LANES - DIFF_DQK)))
        lam_init = 0.8 - 0.6 * math.exp(-0.3 * l)
        y_diff = _diff_mixer(pd.reshape(bsz, seq, -1), lam_vecs,
                             jnp.tile(diff_norm_g[l], DIFF_HEADS).reshape(1, -1), lam_init)

        w_route = _pad_cols(jnp.concatenate(
            [router_g[l], router_e[l].reshape(d, N_EXPERTS)], axis=1), LANES)
        w_route_hi = w_route.astype(BF16)
        w_route = jnp.concatenate(
            [w_route_hi, (w_route - w_route_hi.astype(F32)).astype(BF16)], axis=1)
        ys = [y.reshape(t, W_MIX) for y in (y_conv, y_gla, y_ssd, y_diff)]
        xn, route, cnt = _out_proj(ys, x2d, w_o_b, l, ln1_g[l].reshape(1, -1),
                                   ln1_b[l].reshape(1, -1), w_route)

        dest3, block_e, n_used, n_valid, pads = _dispatch_plan(route, cnt, n_blocks)
        xs = _dispatch(xn, dest3, pads, n_slots)
        y = _expert_ffn(xs, block_e, n_used, n_valid, w_gate, w_up, w_down, l)
        y2 = _sc_gather_rows(y, dest3.reshape(2 * t)).reshape(t, 2 * D_MODEL)
        x2d = _combine_dense(route, xn, y2, ln2_g[l].reshape(1, -1), ln2_b[l].reshape(1, -1))
    return x2d.reshape(bsz, seq, d)
```

```python
import functools
import math

import jax
import jax.numpy as jnp
from jax import lax
from jax.experimental import pallas as pl
from jax.experimental.pallas import tpu as pltpu
from jax.experimental.pallas import tpu_sc as plsc

F32 = jnp.float32
BF16 = jnp.bfloat16
HI = lax.Precision.HIGHEST

D_MODEL = 1024
DEPTH = 2
W_MIX = 256
GLA_HEADS, GLA_DK, GLA_DV, GLA_RANK, GLA_TAU, GLA_CHUNK = 4, 32, 64, 16, 16.0, 64
GLA_ROWS = 256
SSD_HEADS, SSD_GROUPS, SSD_HEADDIM, SSD_STATE, SSD_CONV_K, SSD_CHUNK = 4, 2, 64, 128, 4, 128
DIFF_HEADS, DIFF_DQK, DIFF_DV = 4, 32, 64
N_GROUPS, EXPERTS_PER_GROUP, N_EXPERTS, D_EXPERT = 4, 8, 32, 512
ALPHA = (2 * DEPTH) ** 0.25
LN_EPS = 1e-5
RMS_EPS = 1e-6

LANES = 128
SUBLANES = 8
PROJ_WIDTHS = (768, 896, 1152, 768)
VMEM_LIMIT = 56 * 1024 * 1024

MOE_BLK = 512
TOK_TILE = 256


def _cparams(sem):
    return pltpu.CompilerParams(dimension_semantics=sem, vmem_limit_bytes=VMEM_LIMIT)


def _sigmoid(x):
    return 1.0 / (1.0 + jnp.exp(-x))


def _softplus(x):
    return jnp.maximum(x, 0.0) + jnp.log(1.0 + jnp.exp(-jnp.abs(x)))


def _layer_norm(h, g, b):
    mu = jnp.mean(h, axis=-1, keepdims=True)
    d = h - mu
    var = jnp.mean(d * d, axis=-1, keepdims=True)
    return d * lax.rsqrt(var + LN_EPS) * g + b


def _dot_nt(a, b):
    return lax.dot_general(a, b, (((1,), (1,)), ((), ())), preferred_element_type=F32)


def _dot_tn(a, b, precision=None):
    return lax.dot_general(a, b, (((0,), (0,)), ((), ())), preferred_element_type=F32,
                           precision=precision)


def _split_bf16(x, parts):
    out = []
    for _ in range(parts - 1):
        hi = x.astype(BF16)
        out.append(hi)
        x = x - hi.astype(F32)
    out.append(x.astype(BF16))
    return out


def _dot(a, b):
    return jnp.dot(a, b, preferred_element_type=F32)


def _dot_split_lhs(a, b_exact, parts, dot=_dot):
    acc = None
    for term in _split_bf16(a, parts):
        d = dot(term, b_exact)
        acc = d if acc is None else acc + d
    return acc


def _dot_split_rhs(a_exact, b, parts):
    acc = None
    for term in _split_bf16(b, parts):
        d = jnp.dot(a_exact, term, preferred_element_type=F32)
        acc = d if acc is None else acc + d
    return acc


def _proj_kernel(x_ref, w_ref, oc_ref, og_ref, os_ref, od_ref):
    xb = x_ref[...].astype(BF16)
    off = 0
    for o_ref in (oc_ref, og_ref, os_ref, od_ref):
        n = o_ref.shape[-1]
        o_ref[...] = jnp.dot(xb, w_ref[:, off:off + n], preferred_element_type=F32)
        off += n


def _in_proj(x2d, w_r, layer):
    t = x2d.shape[0]
    tm = 512
    ncol = sum(PROJ_WIDTHS)
    return pl.pallas_call(
        _proj_kernel,
        grid=(t // tm,),
        in_specs=[pl.BlockSpec((tm, D_MODEL), lambda i: (i, 0)),
                  pl.BlockSpec((None, D_MODEL, ncol), lambda i: (layer, 0, 0))],
        out_specs=[pl.BlockSpec((tm, n), lambda i: (i, 0)) for n in PROJ_WIDTHS],
        out_shape=[jax.ShapeDtypeStruct((t, n), F32) for n in PROJ_WIDTHS],
        compiler_params=_cparams(("parallel",)),
        name="in_proj",
    )(x2d, w_r)


def _conv_kernel(p_ref, w_ref, o_ref):
    u = p_ref[0, :, 0:W_MIX]
    gb = p_ref[0, :, W_MIX:2 * W_MIX]
    gc = p_ref[0, :, 2 * W_MIX:3 * W_MIX]
    cu = gc * u
    row = lax.broadcasted_iota(jnp.int32, cu.shape, 0)
    acc = cu * w_ref[2:3, :]
    for s in (1, 2):
        sh = jnp.where(row >= s, pltpu.roll(cu, s, axis=0), 0.0)
        acc = acc + sh * w_ref[2 - s:3 - s, :]
    o_ref[0] = (gb * acc).astype(o_ref.dtype)


def _conv_mixer(pc, conv_w):
    b, s, _ = pc.shape
    return pl.pallas_call(
        _conv_kernel,
        grid=(b,),
        in_specs=[pl.BlockSpec((1, s, 3 * W_MIX), lambda i: (i, 0, 0)),
                  pl.BlockSpec((3, W_MIX), lambda i: (0, 0))],
        out_specs=pl.BlockSpec((1, s, W_MIX), lambda i: (i, 0, 0)),
        out_shape=jax.ShapeDtypeStruct((b, s, W_MIX), BF16),
        compiler_params=_cparams(("parallel",)),
        name="conv_mixer",
    )(pc, conv_w)


def _gla_kernel(p_ref, wlr_ref, blr_ref, ng_ref, o_ref, st_ref):
    c = GLA_CHUNK
    s_len = p_ref.shape[1]
    nh, dk, dv = GLA_HEADS, GLA_DK, GLA_DV
    st_ref[...] = jnp.zeros_like(st_ref)

    rb = GLA_ROWS
    ncb = rb // c
    ri = lax.broadcasted_iota(jnp.int32, (rb, rb), 0)
    ci = lax.broadcasted_iota(jnp.int32, (rb, rb), 1)
    tri = (ci <= ri).astype(BF16)
    klane_head = lax.broadcasted_iota(jnp.int32, (1, nh * dk), 1) // dk
    vlane_head = lax.broadcasted_iota(jnp.int32, (1, nh * dv), 1) // dv
    strow_head = lax.broadcasted_iota(jnp.int32, (nh * dv, 1), 0) // dv
    st_mask = strow_head == klane_head
    r4 = lax.broadcasted_iota(jnp.int32, (nh * c, c), 0) % c
    c4 = lax.broadcasted_iota(jnp.int32, (nh * c, c), 1)
    causal4 = c4 <= r4
    gi = lax.broadcasted_iota(jnp.int32, (nh * dv, nh * dv), 0) // dv
    gj = lax.broadcasted_iota(jnp.int32, (nh * dv, nh * dv), 1) // dv
    gmean = jnp.where(gi == gj, 1.0 / dv, 0.0).astype(BF16)
    wlr_hi, wlr_lo = _split_bf16(wlr_ref[...], 2)

    def body(n, carry):
        r0 = pl.multiple_of(n * rb, rb)
        rows = pl.ds(r0, rb)
        q = p_ref[0, rows, 0:128] * (dk ** -0.5)
        k = p_ref[0, rows, 128:256]
        vb = p_ref[0, rows, 256:512].astype(BF16)
        g = p_ref[0, rows, 512:768]
        lr = p_ref[0, rows, 768:896]
        lr_hi, lr_lo = _split_bf16(lr, 2)
        z = (jnp.dot(lr_hi, wlr_hi, preferred_element_type=F32)
             + jnp.dot(lr_hi, wlr_lo, preferred_element_type=F32)
             + jnp.dot(lr_lo, wlr_hi, preferred_element_type=F32)) + blr_ref[...]
        log_a = (jnp.minimum(z, 0.0) - jnp.log(1.0 + jnp.exp(-jnp.abs(z)))) * (1.0 / GLA_TAU)
        cumb = _dot_split_rhs(tri, log_a, 3)
        ends = [cumb[(j + 1) * c - 1:(j + 1) * c, :] for j in range(ncb)]
        starts = [jnp.zeros_like(ends[0])] + ends[:-1]
        cum = cumb - jnp.concatenate([jnp.broadcast_to(s0, (c, nh * dk)) for s0 in starts], axis=0)
        lasts = [e - s0 for e, s0 in zip(ends, starts)]
        cl = jnp.concatenate([jnp.broadcast_to(x, (c, nh * dk)) for x in lasts], axis=0)
        q_dec = q * jnp.exp(cum)
        k_inv = (k * jnp.exp(-cum)).astype(BF16)
        k_end = (k * jnp.exp(cl - cum)).astype(BF16)
        st = st_ref[...]
        outs = []
        for j in range(ncb):
            sl = slice(j * c, (j + 1) * c)
            qd = q_dec[sl]
            qs = jnp.concatenate([jnp.where(klane_head == h, qd, 0.0) for h in range(nh)],
                                 axis=0).astype(BF16)
            att = jnp.where(causal4, _dot_nt(qs, k_inv[sl]), 0.0)
            r = jnp.dot(att.astype(BF16), vb[sl], preferred_element_type=F32)
            o = jnp.where(vlane_head == 0, r[0:c], 0.0)
            for h in range(1, nh):
                o = o + jnp.where(vlane_head == h, r[h * c:(h + 1) * c], 0.0)
            outs.append(o + _dot_nt(qd.astype(BF16), st.astype(BF16)))
            d_st = _dot_tn(vb[sl], k_end[sl])
            st = st * jnp.exp(lasts[j]) + jnp.where(st_mask, d_st, 0.0)
        st_ref[...] = st
        o = jnp.concatenate(outs, axis=0)
        ms = _dot_split_lhs(o * o, gmean, 2)
        o = o * lax.rsqrt(ms + RMS_EPS) * ng_ref[...]
        o_ref[0, rows, :] = (o * (g * _sigmoid(g))).astype(o_ref.dtype)
        return carry

    lax.fori_loop(0, s_len // rb, body, 0)


def _gla_mixer(pg, w_lr_pad, b_lr, norm_g4):
    b, s, wp = pg.shape
    return pl.pallas_call(
        _gla_kernel,
        grid=(b,),
        in_specs=[pl.BlockSpec((1, s, wp), lambda i: (i, 0, 0)),
                  pl.BlockSpec((LANES, LANES), lambda i: (0, 0)),
                  pl.BlockSpec((1, LANES), lambda i: (0, 0)),
                  pl.BlockSpec((1, W_MIX), lambda i: (0, 0))],
        out_specs=pl.BlockSpec((1, s, W_MIX), lambda i: (i, 0, 0)),
        out_shape=jax.ShapeDtypeStruct((b, s, W_MIX), BF16),
        scratch_shapes=[pltpu.VMEM((GLA_HEADS * GLA_DV, GLA_HEADS * GLA_DK), F32)],
        compiler_params=_cparams(("parallel",)),
        name="gla_mixer",
    )(pg, w_lr_pad, b_lr, norm_g4)


def _ssd_kernel(p_ref, cw_ref, cb_ref, alog_ref, dtb_ref, dsk_ref, ng_ref, o_ref, st_ref):
    c = SSD_CHUNK
    s_len = p_ref.shape[1]
    n_st = SSD_STATE
    st_ref[...] = jnp.zeros_like(st_ref)

    ri = lax.broadcasted_iota(jnp.int32, (c, c), 0)
    ci = lax.broadcasted_iota(jnp.int32, (c, c), 1)
    causal = ci <= ri
    tri = causal.astype(BF16)
    upper = (ri <= ci).astype(BF16)
    lane_head = lax.broadcasted_iota(jnp.int32, (1, W_MIX), 1) // SSD_HEADDIM
    lane_group = lane_head // (SSD_HEADS // SSD_GROUPS)
    eh = lax.broadcasted_iota(jnp.int32, (LANES, W_MIX), 0)
    el = lax.broadcasted_iota(jnp.int32, (LANES, W_MIX), 1) // SSD_HEADDIM
    expand = (eh == el).astype(BF16)
    row8 = lax.broadcasted_iota(jnp.int32, (8, 3 * W_MIX), 0)
    a_c = -jnp.exp(alog_ref[...])

    def body(n, carry):
        r0 = pl.multiple_of(n * c, c)
        rows = pl.ds(r0, c)
        cur = p_ref[0, rows, 256:1024]
        p0 = pl.multiple_of(jnp.maximum(r0 - 8, 0), 8)
        prev8 = p_ref[0, pl.ds(p0, 8), 256:1024]
        prev8 = jnp.where(n > 0, prev8, 0.0)
        acc = cur * cw_ref[3:4, :] + cb_ref[...]
        for s in (1, 2, 3):
            sh = pltpu.roll(cur, s, axis=0)
            top = jnp.where(row8 < s, pltpu.roll(prev8, s, axis=0), sh[0:8])
            sh = jnp.concatenate([top, sh[8:]], axis=0)
            acc = acc + sh * cw_ref[3 - s:4 - s, :]
        xbc = acc * _sigmoid(acc)
        x = xbc[:, 0:256]
        bm = xbc[:, 256:512].astype(BF16)
        cm = xbc[:, 512:768].astype(BF16)

        dt_c = _softplus(p_ref[0, rows, 1024:1152] + dtb_ref[...])
        da_c = dt_c * a_c
        cum_c = _dot_split_rhs(tri, da_c, 3)
        cum_r = _dot_split_lhs(da_c, upper, 3, dot=_dot_tn)
        both_x = _dot_split_lhs(jnp.concatenate([dt_c, cum_c], axis=0), expand, 3)
        dt_x = both_x[0:c]
        cum_x = both_x[c:2 * c]
        cl_x = cum_x[c - 1:c, :]
        x_dt = x * dt_x
        x_dt_b = x_dt.astype(BF16)
        xw_b = (x_dt * jnp.exp(cl_x - cum_x)).astype(BF16)

        y = x * dsk_ref[...]
        y_off = jnp.zeros((c, W_MIX), F32)
        for g in range(SSD_GROUPS):
            bg = bm[:, g * n_st:(g + 1) * n_st]
            cg = cm[:, g * n_st:(g + 1) * n_st]
            cb = _dot_nt(cg, bg)
            for r in range(SSD_HEADS // SSD_GROUPS):
                h = g * (SSD_HEADS // SSD_GROUPS) + r
                diff = cum_c[:, h:h + 1] - cum_r[h:h + 1, :]
                dec = jnp.exp(jnp.where(causal, diff, -jnp.inf))
                m = (cb * dec).astype(BF16)
                yh = jnp.dot(m, x_dt_b, preferred_element_type=F32)
                y = y + jnp.where(lane_head == h, yh, 0.0)
            st = st_ref[g]
            y_off = y_off + jnp.where(lane_group == g,
                                      jnp.dot(cg, st.astype(BF16), preferred_element_type=F32), 0.0)
            d_st = _dot_tn(bg, xw_b)
            st_ref[g] = st * jnp.exp(cl_x) + jnp.where(lane_group == g, d_st, 0.0)
        y = y + y_off * jnp.exp(cum_x)
        zg = p_ref[0, rows, 0:256]
        y = y * (zg * _sigmoid(zg))
        outs = []
        for g in range(SSD_GROUPS):
            yg = y[:, g * 128:(g + 1) * 128]
            ms = jnp.mean(yg * yg, axis=-1, keepdims=True)
            outs.append(yg * lax.rsqrt(ms + RMS_EPS))
        o_ref[0, rows, :] = (jnp.concatenate(outs, axis=-1) * ng_ref[...]).astype(o_ref.dtype)
        return carry

    def pair(u, carry):
        body(2 * u, carry)
        return body(2 * u + 1, carry)

    lax.fori_loop(0, s_len // (2 * c), pair, 0)


def _ssd_mixer(ps, conv_w, conv_b, a_log_c, dt_bias_c, d_x, norm_g):
    b, s, wp = ps.shape
    full2 = lambda i: (0, 0)
    return pl.pallas_call(
        _ssd_kernel,
        grid=(b,),
        in_specs=[pl.BlockSpec((1, s, wp), lambda i: (i, 0, 0)),
                  pl.BlockSpec((SSD_CONV_K, 3 * W_MIX), full2),
                  pl.BlockSpec((1, 3 * W_MIX), full2),
                  pl.BlockSpec((1, LANES), full2),
                  pl.BlockSpec((1, LANES), full2),
                  pl.BlockSpec((1, W_MIX), full2),
                  pl.BlockSpec((1, W_MIX), full2)],
        out_specs=pl.BlockSpec((1, s, W_MIX), lambda i: (i, 0, 0)),
        out_shape=jax.ShapeDtypeStruct((b, s, W_MIX), BF16),
        scratch_shapes=[pltpu.VMEM((SSD_GROUPS, SSD_STATE, W_MIX), F32)],
        compiler_params=_cparams(("parallel",)),
        name="ssd_mixer",
    )(ps, conv_w, conv_b, a_log_c, dt_bias_c, d_x, norm_g)


DIFF_TQ = 256
DIFF_TK = 256
LOG2E = 1.4426950408889634
DIFF_VPAD = DIFF_DV + 16


def _diff_kernel(q_ref, k_ref, v_ref, lam_ref, ng_ref, o_ref,
                 kb_ref, vt_ref, qs_ref, st_ref, m_ref, acc_ref, *, lam_init):
    tq, tk = DIFF_TQ, DIFF_TK
    nh, dv = DIFF_HEADS, DIFF_DV
    nhc = 2 * nh
    s_len = k_ref.shape[1]
    i = pl.program_id(1)

    @pl.when(i == 0)
    def _():
        kb_ref[...] = k_ref[0].astype(BF16)
        for cblk in range(s_len // tk):
            cols = slice(cblk * tk, (cblk + 1) * tk)
            vt = v_ref[0, cols, :].T.astype(BF16)
            for h in range(nh):
                vt_ref[h, 0:dv, cols] = vt[h * dv:(h + 1) * dv]
        vt_ref[:, dv:, :] = jnp.ones((nh, DIFF_VPAD - dv, s_len), BF16)

    q = q_ref[0] * (DIFF_DQK ** -0.5 * LOG2E)
    qlane = lax.broadcasted_iota(jnp.int32, (1, W_MIX), 1) // DIFF_DQK
    for hc in range(nhc):
        qs_ref[hc * tq:(hc + 1) * tq, :] = jnp.where(qlane == hc, q, 0.0).astype(BF16)
    m_ref[...] = jnp.full_like(m_ref, -jnp.inf)
    acc_ref[...] = jnp.zeros_like(acc_ref)
    krow = lax.broadcasted_iota(jnp.int32, (tk, nhc * tq), 0)
    qcol = lax.broadcasted_iota(jnp.int32, (tk, nhc * tq), 1) % tq
    diag_ok = krow <= qcol

    def scores(j, slot):
        k0 = pl.multiple_of(j * tk, tk)
        st_ref[slot] = _dot_nt(kb_ref[pl.ds(k0, tk), :], qs_ref[...])

    def softmax_pv(j, slot, masked):
        k0 = pl.multiple_of(j * tk, tk)
        st = st_ref[slot]
        if masked:
            st = jnp.where(diag_ok, st, -jnp.inf)
        m_prev = m_ref[...]
        m_new = jnp.maximum(m_prev, jnp.max(st, axis=0, keepdims=True))
        alpha = jnp.exp2(m_prev - m_new)
        p = jnp.exp2(st - m_new)
        m_ref[...] = m_new
        pb = p.astype(BF16)
        for hc in range(nhc):
            h = hc // 2
            lanes = slice(hc * tq, (hc + 1) * tq)
            pv = jnp.dot(vt_ref[h, :, pl.ds(k0, tk)], pb[:, lanes],
                         preferred_element_type=F32)
            acc_ref[hc] = acc_ref[hc] * alpha[:, lanes] + pv

    scores(0, 0)
    n_pairs = i // 2

    def pair_step(u, carry):
        scores(2 * u + 1, 1)
        softmax_pv(2 * u, 0, False)
        scores(2 * u + 2, 0)
        softmax_pv(2 * u + 1, 1, False)
        return carry

    lax.fori_loop(0, n_pairs, pair_step, 0)

    @pl.when(i % 2 == 0)
    def _():
        softmax_pv(i, 0, True)

    @pl.when(i % 2 == 1)
    def _():
        scores(i, 1)
        softmax_pv(i - 1, 0, False)
        softmax_pv(i, 1, True)

    lam = (jnp.exp(jnp.sum(lam_ref[0:1, :] * lam_ref[1:2, :], axis=-1, keepdims=True))
           - jnp.exp(jnp.sum(lam_ref[2:3, :] * lam_ref[3:4, :], axis=-1, keepdims=True))
           + lam_init)
    heads = []
    for h in range(nh):
        o1 = acc_ref[2 * h, 0:dv] / acc_ref[2 * h, dv:dv + 1]
        o2 = acc_ref[2 * h + 1, 0:dv] / acc_ref[2 * h + 1, dv:dv + 1]
        oh = o1 - lam * o2
        ms = jnp.mean(oh * oh, axis=0, keepdims=True)
        heads.append(oh * lax.rsqrt(ms + RMS_EPS))
    o = jnp.concatenate(heads, axis=0).T
    o_ref[0] = (o * ng_ref[...] * (1.0 - lam_init)).astype(o_ref.dtype)


def _diff_mixer(pd, lam_vecs, norm_g4, lam_init):
    b, s, _ = pd.shape
    tq = DIFF_TQ
    return pl.pallas_call(
        functools.partial(_diff_kernel, lam_init=lam_init),
        grid=(b, s // tq),
        in_specs=[pl.BlockSpec((1, tq, W_MIX), lambda bi, i: (bi, i, 0)),
                  pl.BlockSpec((1, s, W_MIX), lambda bi, i: (bi, 0, 1)),
                  pl.BlockSpec((1, s, W_MIX), lambda bi, i: (bi, 0, 2)),
                  pl.BlockSpec((4, LANES), lambda bi, i: (0, 0)),
                  pl.BlockSpec((1, W_MIX), lambda bi, i: (0, 0))],
        out_specs=pl.BlockSpec((1, tq, W_MIX), lambda bi, i: (bi, i, 0)),
        out_shape=jax.ShapeDtypeStruct((b, s, W_MIX), BF16),
        scratch_shapes=[pltpu.VMEM((s, W_MIX), BF16),
                        pltpu.VMEM((DIFF_HEADS, DIFF_VPAD, s), BF16),
                        pltpu.VMEM((2 * DIFF_HEADS * tq, W_MIX), BF16),
                        pltpu.VMEM((2, DIFF_TK, 2 * DIFF_HEADS * tq), F32),
                        pltpu.VMEM((1, 2 * DIFF_HEADS * tq), F32),
                        pltpu.VMEM((2 * DIFF_HEADS, DIFF_VPAD, tq), F32)],
        compiler_params=_cparams(("parallel", "arbitrary")),
        name="diff_attn",
    )(pd, pd, pd, lam_vecs, norm_g4)


def _oproj_kernel(yc_ref, yg_ref, ys_ref, yd_ref, x_ref, wo_ref, g_ref, b_ref, wr_ref,
                  xo_ref, route_ref, cnt_ref):
    mix = jnp.concatenate([yc_ref[...], yg_ref[...], ys_ref[...], yd_ref[...]], axis=-1)
    h = ALPHA * x_ref[...] + jnp.dot(mix, wo_ref[...], preferred_element_type=F32)
    xn = _layer_norm(h, g_ref[...], b_ref[...])
    xo_ref[...] = xn

    xn_hi, xn_lo = _split_bf16(xn, 2)
    both = _dot(xn_hi, wr_ref[...])
    logits = both[:, 0:LANES] + both[:, LANES:2 * LANES] + _dot(xn_lo, wr_ref[:, 0:LANES])
    lane = lax.broadcasted_iota(jnp.int32, logits.shape, 1).astype(F32)
    neg = -jnp.inf
    big = float(LANES)
    lg = jnp.where(lane < N_GROUPS, logits, neg)
    mg = jnp.max(lg, axis=-1, keepdims=True)
    sg = jnp.sum(jnp.exp(lg - mg), axis=-1, keepdims=True)
    grp = jnp.min(jnp.where(lg == mg, lane, big), axis=-1, keepdims=True)
    p_grp = 1.0 / sg
    lo = N_GROUPS + EXPERTS_PER_GROUP * grp
    in_g = jnp.logical_and(lane >= lo, lane < lo + EXPERTS_PER_GROUP)
    le = jnp.where(in_g, logits, neg)
    me = jnp.max(le, axis=-1, keepdims=True)
    ee = jnp.exp(le - me)
    pe = ee / jnp.sum(ee, axis=-1, keepdims=True)
    pe = jnp.where(in_g, pe, -1.0)
    p1 = jnp.max(pe, axis=-1, keepdims=True)
    i1 = jnp.min(jnp.where(pe == p1, lane, big), axis=-1, keepdims=True)
    pe2 = jnp.where(lane == i1, -1.0, pe)
    p2 = jnp.max(pe2, axis=-1, keepdims=True)
    i2 = jnp.min(jnp.where(pe2 == p2, lane, big), axis=-1, keepdims=True)
    den = p1 + p2
    g1 = p_grp * p1 / den
    g2 = p_grp * p2 / den
    e1 = i1 - N_GROUPS
    e2 = i2 - N_GROUPS
    route_ref[...] = jnp.where(lane == 0, e1, jnp.where(lane == 1, e2, jnp.where(
        lane == 2, g1, jnp.where(lane == 3, g2, 0.0))))

    @pl.when(pl.program_id(0) == 0)
    def _():
        cnt_ref[...] = jnp.zeros_like(cnt_ref)

    hits = jnp.where(lane == e1, 1.0, 0.0) + jnp.where(lane == e2, 1.0, 0.0)
    cnt_ref[...] += jnp.sum(hits, axis=0, keepdims=True)


def _out_proj(ys, x2d, w_o, layer, ln_g, ln_b, w_route):
    t = x2d.shape[0]
    tm = 512
    row = lambda i: (i, 0)
    full = lambda i: (0, 0)
    return pl.pallas_call(
        _oproj_kernel,
        grid=(t // tm,),
        in_specs=[pl.BlockSpec((tm, W_MIX), row)] * 4 + [
            pl.BlockSpec((tm, D_MODEL), row),
            pl.BlockSpec((None, D_MODEL, D_MODEL), lambda i: (layer, 0, 0)),
            pl.BlockSpec((1, D_MODEL), full),
            pl.BlockSpec((1, D_MODEL), full),
            pl.BlockSpec((D_MODEL, 2 * LANES), full)],
        out_specs=[pl.BlockSpec((tm, D_MODEL), row), pl.BlockSpec((tm, LANES), row),
                   pl.BlockSpec((1, LANES), full)],
        out_shape=[jax.ShapeDtypeStruct((t, D_MODEL), F32),
                   jax.ShapeDtypeStruct((t, LANES), F32),
                   jax.ShapeDtypeStruct((1, LANES), F32)],
        compiler_params=_cparams(("arbitrary",)),
        name="out_proj_ln_router",
    )(*ys, x2d, w_o, ln_g, ln_b, w_route)


PLAN_TILE = 512


def _plan_kernel(route_ref, cnt_ref, dest_ref, meta_ref, carry_ref, pstart_ref):
    tm = route_ref.shape[0]
    lane = lax.broadcasted_iota(jnp.int32, (1, LANES), 1).astype(F32)

    @pl.when(pl.program_id(0) == 0)
    def _():
        cnt = cnt_ref[...]
        padded = jnp.ceil(cnt * (1.0 / MOE_BLK)) * MOE_BLK
        li = lax.broadcasted_iota(jnp.int32, (LANES, LANES), 0)
        lj = lax.broadcasted_iota(jnp.int32, (LANES, LANES), 1)
        before = (li < lj).astype(F32)
        pstart = jnp.dot(jnp.broadcast_to(padded, (8, LANES)), before, precision=HI,
                         preferred_element_type=F32)[0:1]
        pstart_ref[...] = pstart
        carry_ref[...] = jnp.zeros_like(carry_ref)
        meta_ref[...] = jnp.concatenate(
            [pstart + padded, pstart, cnt, jnp.zeros((5, LANES), F32)], axis=0)

    oh0 = jnp.where(lane == route_ref[:, 0:1], 1.0, 0.0)
    oh1 = jnp.where(lane == route_ref[:, 1:2], 1.0, 0.0)
    both = oh0 + oh1
    ri = lax.broadcasted_iota(jnp.int32, (tm, tm), 0)
    ci = lax.broadcasted_iota(jnp.int32, (tm, tm), 1)
    earlier = (ci < ri).astype(BF16)
    base = (jnp.dot(earlier, both.astype(BF16), preferred_element_type=F32)
            + carry_ref[...] + pstart_ref[...])
    d0 = jnp.sum(oh0 * base, axis=-1, keepdims=True)
    d1 = jnp.sum(oh1 * base, axis=-1, keepdims=True)
    dest_ref[...] = jnp.where(lane == 0, d0, jnp.where(lane == 1, d1, 0.0))
    carry_ref[...] += jnp.sum(both, axis=0, keepdims=True)


def _dispatch_plan(route, cnt, n_blocks):
    t = route.shape[0]
    tm = PLAN_TILE
    blk = MOE_BLK
    dest, meta = pl.pallas_call(
        _plan_kernel,
        grid=(t // tm,),
        in_specs=[pl.BlockSpec((tm, LANES), lambda i: (i, 0)),
                  pl.BlockSpec((1, LANES), lambda i: (0, 0))],
        out_specs=[pl.BlockSpec((tm, LANES), lambda i: (i, 0)),
                   pl.BlockSpec((8, LANES), lambda i: (0, 0))],
        out_shape=[jax.ShapeDtypeStruct((t, LANES), F32),
                   jax.ShapeDtypeStruct((8, LANES), F32)],
        scratch_shapes=[pltpu.VMEM((1, LANES), F32), pltpu.VMEM((1, LANES), F32)],
        compiler_params=_cparams(("arbitrary",)),
        name="moe_plan",
    )(route, cnt)
    pad_end = meta[0, :N_EXPERTS].astype(jnp.int32)
    starts = jnp.arange(n_blocks, dtype=jnp.int32) * blk
    block_e = jnp.minimum(jnp.sum((pad_end[None, :] <= starts[:, None]).astype(jnp.int32), axis=1),
                          N_EXPERTS - 1)
    n_used = (pad_end[N_EXPERTS - 1] // blk).reshape(1)
    seg_end = (meta[1, :N_EXPERTS] + meta[2, :N_EXPERTS]).astype(jnp.int32)
    n_valid = jnp.clip(seg_end[block_e] - starts, 0, blk)
    idx = dest[:, 0:2].astype(jnp.int32).T.reshape(2 * t)
    return idx, block_e, n_used, n_valid


ZERO_ROWS = MOE_BLK // 2


def _scatter_kernel(pads_ref, dest_ref, x_ref, xs_ref, zero_ref, sem, zsem):
    tm = x_ref.shape[0]
    n_slots = xs_ref.shape[0]

    def zero_fill(wait):
        def go(src, dst):
            cp = pltpu.make_async_copy(src, dst, zsem)
            if wait:
                cp.wait()
            else:
                cp.start()

        def per_expert(e, carry):
            pos = pads_ref[0, e]
            head = (-pos) & (SUBLANES - 1)
            for r in range(SUBLANES - 1):
                @pl.when(r < head)
                def _():
                    go(zero_ref.at[pl.ds(0, 1)], xs_ref.at[pl.ds(pos + r, 1)])

            base = pos + head
            nrem = pads_ref[1, e] - base
            p = ZERO_ROWS
            while p >= SUBLANES:
                off = pl.multiple_of(base + (nrem & (-2 * p)), SUBLANES)

                @pl.when((nrem & p) != 0)
                def _():
                    go(zero_ref.at[pl.ds(0, p)], xs_ref.at[pl.ds(off, p)])

                p //= 2
            return carry

        lax.fori_loop(0, N_EXPERTS, per_expert, 0)
        tail0 = pads_ref[1, N_EXPERTS - 1]

        def tail(b, carry):
            row0 = pl.multiple_of(tail0 + b * ZERO_ROWS, ZERO_ROWS)
            go(zero_ref, xs_ref.at[pl.ds(row0, ZERO_ROWS)])
            return carry

        lax.fori_loop(0, (n_slots - tail0) // ZERO_ROWS, tail, 0)

    @pl.when(pl.program_id(0) == 0)
    def _():
        zero_ref[...] = jnp.zeros_like(zero_ref)
        zero_fill(False)
        zero_fill(True)

    def issue(r, carry):
        for k in range(2):
            d = dest_ref[0, 0, 2 * r + k]
            pltpu.make_async_copy(x_ref.at[pl.ds(r, 1)], xs_ref.at[pl.ds(d, 1)],
                                  sem).start(priority=k)
        return carry

    lax.fori_loop(0, tm, issue, 0, unroll=8)
    for k in range(2):
        pltpu.make_async_copy(x_ref, xs_ref.at[pl.ds(0, tm)], sem).wait()


def _dispatch(x2d, dest3, pads, n_slots):
    t = x2d.shape[0]
    tm = TOK_TILE
    return pl.pallas_call(
        _scatter_kernel,
        grid=(t // tm,),
        in_specs=[pl.BlockSpec(memory_space=pltpu.SMEM),
                  pl.BlockSpec((1, 1, 2 * tm), lambda i: (i, 0, 0), memory_space=pltpu.SMEM),
                  pl.BlockSpec((tm, D_MODEL), lambda i: (i, 0))],
        out_specs=pl.BlockSpec(memory_space=pl.ANY),
        out_shape=jax.ShapeDtypeStruct((n_slots, D_MODEL), F32),
        scratch_shapes=[pltpu.VMEM((ZERO_ROWS, D_MODEL), F32),
                        pltpu.SemaphoreType.DMA(()), pltpu.SemaphoreType.DMA(())],
        compiler_params=pltpu.CompilerParams(dimension_semantics=("arbitrary",),
                                             vmem_limit_bytes=VMEM_LIMIT,
                                             has_side_effects=True),
        name="moe_dispatch",
    )(pads, dest3, x2d)


def _ffn_kernel(be_ref, nu_ref, nv_ref, xs_ref, wg_ref, wu_ref, wd_ref, y_ref,
                wgb_ref, wub_ref, wdb_ref):
    i = pl.program_id(0)
    prev = be_ref[jnp.maximum(i - 1, 0)]

    @pl.when(jnp.logical_or(i == 0, be_ref[i] != prev))
    def _():
        wgb_ref[...] = wg_ref[...].astype(BF16)
        wub_ref[...] = wu_ref[...].astype(BF16)
        wdb_ref[...] = wd_ref[...].astype(BF16)

    @pl.when(i < nu_ref[0])
    def _():
        row = lax.broadcasted_iota(jnp.int32, (xs_ref.shape[0], 1), 0)
        xb = jnp.where(row < nv_ref[i], xs_ref[...], 0.0).astype(BF16)
        a = jnp.dot(xb, wgb_ref[...], preferred_element_type=F32)
        u = jnp.dot(xb, wub_ref[...], preferred_element_type=F32)
        h = (a * _sigmoid(a) * u).astype(BF16)
        y_ref[...] = jnp.dot(h, wdb_ref[...], preferred_element_type=F32)

    @pl.when(i >= nu_ref[0])
    def _():
        y_ref[...] = jnp.zeros_like(y_ref)


def _expert_ffn(xs, block_e, n_used, n_valid, w_gate, w_up, w_down, layer):
    n_slots = xs.shape[0]
    blk = MOE_BLK
    w_map = lambda i, be, nu, nv: (layer, be[i], 0, 0)
    grid_spec = pltpu.PrefetchScalarGridSpec(
        num_scalar_prefetch=3,
        grid=(n_slots // blk,),
        in_specs=[pl.BlockSpec((blk, D_MODEL),
                               lambda i, be, nu, nv: (jnp.minimum(i, nu[0] - 1), 0)),
                  pl.BlockSpec((None, None, D_MODEL, D_EXPERT), w_map),
                  pl.BlockSpec((None, None, D_MODEL, D_EXPERT), w_map),
                  pl.BlockSpec((None, None, D_EXPERT, D_MODEL), w_map)],
        out_specs=pl.BlockSpec((blk, D_MODEL), lambda i, be, nu, nv: (i, 0)),
        scratch_shapes=[pltpu.VMEM((D_MODEL, D_EXPERT), BF16),
                        pltpu.VMEM((D_MODEL, D_EXPERT), BF16),
                        pltpu.VMEM((D_EXPERT, D_MODEL), BF16)],
    )
    return pl.pallas_call(
        _ffn_kernel,
        grid_spec=grid_spec,
        out_shape=jax.ShapeDtypeStruct((n_slots, D_MODEL), F32),
        compiler_params=_cparams(("arbitrary",)),
        name="expert_ffn",
    )(block_e, n_used, n_valid, xs, w_gate, w_up, w_down)


def _combine_kernel(dest_ref, dnext_ref, route_ref, x_ref, y_ref, g_ref, b_ref, o_ref, buf_ref, sem):
    tm = x_ref.shape[0]
    i = pl.program_id(0)
    slot = i % 2

    def issue(idx_ref, s):
        def body(r, carry):
            for k in range(2):
                d = idx_ref[0, 0, 2 * r + k]
                pltpu.make_async_copy(y_ref.at[pl.ds(d, 1)], buf_ref.at[s, k, pl.ds(r, 1)],
                                      sem.at[s]).start(priority=k)
            return carry

        lax.fori_loop(0, tm, body, 0, unroll=8)

    @pl.when(i == 0)
    def _():
        issue(dest_ref, slot)

    @pl.when(i + 1 < pl.num_programs(0))
    def _():
        issue(dnext_ref, 1 - slot)

    for k in range(2):
        pltpu.make_async_copy(y_ref.at[pl.ds(0, tm)], buf_ref.at[slot, k], sem.at[slot]).wait()
    moe = route_ref[:, 2:3] * buf_ref[slot, 0] + route_ref[:, 3:4] * buf_ref[slot, 1]
    h = ALPHA * x_ref[...] + moe
    o_ref[...] = _layer_norm(h, g_ref[...], b_ref[...])


def _combine(dest3, route, x2d, y, ln_g, ln_b):
    t = x2d.shape[0]
    tm = TOK_TILE
    row = lambda i: (i, 0)
    full = lambda i: (0, 0)
    last = t // tm - 1
    return pl.pallas_call(
        _combine_kernel,
        grid=(t // tm,),
        in_specs=[pl.BlockSpec((1, 1, 2 * tm), lambda i: (i, 0, 0), memory_space=pltpu.SMEM),
                  pl.BlockSpec((1, 1, 2 * tm), lambda i: (jnp.minimum(i + 1, last), 0, 0),
                               memory_space=pltpu.SMEM),
                  pl.BlockSpec((tm, LANES), row),
                  pl.BlockSpec((tm, D_MODEL), row),
                  pl.BlockSpec(memory_space=pl.ANY),
                  pl.BlockSpec((1, D_MODEL), full),
                  pl.BlockSpec((1, D_MODEL), full)],
        out_specs=pl.BlockSpec((tm, D_MODEL), row),
        out_shape=jax.ShapeDtypeStruct((t, D_MODEL), F32),
        scratch_shapes=[pltpu.VMEM((2, 2, tm, D_MODEL), F32), pltpu.SemaphoreType.DMA((2,))],
        compiler_params=_cparams(("arbitrary",)),
        name="moe_combine",
    )(dest3, dest3, route, x2d, y, ln_g, ln_b)


SC_CORES = 2
SC_SUBCORES = 16
SC_ROWS = 32


def _sc_gather_rows(table, idx):
    b = idx.shape[0]
    d = table.shape[1]
    per_w = b // (SC_CORES * SC_SUBCORES)
    mesh = plsc.VectorSubcoreMesh(core_axis_name="c", subcore_axis_name="s")

    n_chunks = per_w // SC_ROWS

    @functools.partial(
        pl.kernel, mesh=mesh,
        out_type=jax.ShapeDtypeStruct((b, d), table.dtype),
        scratch_types=[pltpu.VMEM((SC_ROWS,), jnp.int32), pltpu.VMEM((SC_ROWS,), jnp.int32),
                       pltpu.VMEM((SC_ROWS, d), table.dtype),
                       pltpu.VMEM((SC_ROWS, d), table.dtype),
                       pltpu.SemaphoreType.DMA, pltpu.SemaphoreType.DMA,
                       pltpu.SemaphoreType.DMA, pltpu.SemaphoreType.DMA],
        name="sc_gather_rows",
    )
    def gather(table_hbm, idx_hbm, out_hbm, idx0, idx1, rows0, rows1, gs0, gs1, ws0, ws1):
        idx_v, rows_v, gsem, wsem = (idx0, idx1), (rows0, rows1), (gs0, gs1), (ws0, ws1)
        wid = lax.axis_index("s") * SC_CORES + lax.axis_index("c")
        base = wid * per_w

        def rows_of(c):
            return pl.ds(pl.multiple_of(base + c * SC_ROWS, SC_ROWS), SC_ROWS)

        def start_gather(c, s):
            pltpu.sync_copy(idx_hbm.at[rows_of(c)], idx_v[s])
            pltpu.async_copy(table_hbm.at[idx_v[s]], rows_v[s], gsem[s])

        def write_back(c, s):
            pltpu.make_async_copy(table_hbm.at[idx_v[s]], rows_v[s], gsem[s]).wait()
            pltpu.async_copy(rows_v[s], out_hbm.at[rows_of(c)], wsem[s]).wait()

        start_gather(0, 0)

        @pl.loop(0, n_chunks, step=2)
        def _(c):
            start_gather(c + 1, 1)
            write_back(c, 0)

            @pl.when(c + 2 < n_chunks)
            def _():
                start_gather(c + 2, 0)

            write_back(c + 1, 1)

    return gather(table, idx)


def _sc_scatter_rows(x2d, idx, n_slots):
    t, d = x2d.shape
    per_w = t // (SC_CORES * SC_SUBCORES)
    mesh = plsc.VectorSubcoreMesh(core_axis_name="c", subcore_axis_name="s")

    @functools.partial(
        pl.kernel, mesh=mesh,
        out_type=jax.ShapeDtypeStruct((n_slots, d), x2d.dtype),
        scratch_types=[pltpu.VMEM((SC_ROWS,), jnp.int32), pltpu.VMEM((SC_ROWS,), jnp.int32),
                       pltpu.VMEM((SC_ROWS, d), x2d.dtype),
                       pltpu.SemaphoreType.DMA, pltpu.SemaphoreType.DMA],
        name="sc_scatter_rows",
    )
    def scatter(x_hbm, idx_hbm, out_hbm, idx0, idx1, rows_v, s0, s1):
        wid = lax.axis_index("s") * SC_CORES + lax.axis_index("c")
        base = wid * per_w

        @pl.loop(0, per_w // SC_ROWS)
        def _(c):
            off = pl.multiple_of(base + c * SC_ROWS, SC_ROWS)
            pltpu.sync_copy(x_hbm.at[pl.ds(off, SC_ROWS)], rows_v)
            pltpu.sync_copy(idx_hbm.at[pl.ds(off, SC_ROWS)], idx0)
            pltpu.sync_copy(idx_hbm.at[pl.ds(t + off, SC_ROWS)], idx1)
            cp0 = pltpu.async_copy(rows_v, out_hbm.at[idx0], s0)
            cp1 = pltpu.async_copy(rows_v, out_hbm.at[idx1], s1)
            cp0.wait()
            cp1.wait()

    return scatter(x2d, idx)


def _combine_dense_kernel(route_ref, x_ref, y0_ref, y1_ref, g_ref, b_ref, o_ref):
    moe = route_ref[:, 2:3] * y0_ref[...] + route_ref[:, 3:4] * y1_ref[...]
    h = ALPHA * x_ref[...] + moe
    o_ref[...] = _layer_norm(h, g_ref[...], b_ref[...])


def _combine_dense(route, x2d, yg, ln_g, ln_b):
    t = x2d.shape[0]
    tm = 512
    nt = t // tm
    row = lambda i: (i, 0)
    full = lambda i: (0, 0)
    return pl.pallas_call(
        _combine_dense_kernel,
        grid=(nt,),
        in_specs=[pl.BlockSpec((tm, LANES), row),
                  pl.BlockSpec((tm, D_MODEL), row),
                  pl.BlockSpec((tm, D_MODEL), row),
                  pl.BlockSpec((tm, D_MODEL), lambda i: (i + nt, 0)),
                  pl.BlockSpec((1, D_MODEL), full),
                  pl.BlockSpec((1, D_MODEL), full)],
        out_specs=pl.BlockSpec((tm, D_MODEL), row),
        out_shape=jax.ShapeDtypeStruct((t, D_MODEL), F32),
        compiler_params=_cparams(("parallel",)),
        name="moe_combine_dense",
    )(route, x2d, yg, yg, ln_g, ln_b)


def _pad_cols(w, n):
    return jnp.pad(w, [(0, 0)] * (w.ndim - 1) + [(0, n - w.shape[-1])])


def kernel(x, w_in, conv_w, gla_w_lr, gla_b_lr, gla_norm_g, ssd_conv_w, ssd_conv_b, ssd_a_log,
           ssd_d, ssd_dt_bias, ssd_norm_g, diff_lq1, diff_lk1, diff_lq2, diff_lk2, diff_norm_g,
           w_o, ln1_g, ln1_b, router_g, router_e, w_gate, w_up, w_down, ln2_g, ln2_b):
    bsz, seq, d = x.shape
    t = bsz * seq
    n_assign = 2 * t
    n_blocks = (n_assign + N_EXPERTS * (MOE_BLK - 1)) // MOE_BLK + 1
    n_slots = n_blocks * MOE_BLK
    x2d = x.reshape(t, d)
    w_in_r = jnp.concatenate([w_in[..., 0:768], _pad_cols(w_in[..., 768:1552], 896),
                              _pad_cols(w_in[..., 1552:2580], 1152), w_in[..., 2580:3348]],
                             axis=-1).astype(BF16)
    w_o_b = w_o.astype(BF16)
    for l in range(DEPTH):
        pc, pg, ps, pd = _in_proj(x2d, w_in_r, l)

        y_conv = _conv_mixer(pc.reshape(bsz, seq, -1), conv_w[l])
        w_lr_pad = jnp.pad(gla_w_lr[l], ((0, LANES - GLA_RANK), (0, 0)))
        y_gla = _gla_mixer(pg.reshape(bsz, seq, -1), w_lr_pad, gla_b_lr[l].reshape(1, -1),
                           jnp.tile(gla_norm_g[l], GLA_HEADS).reshape(1, -1))
        pad4 = lambda v: jnp.pad(v, (0, LANES - SSD_HEADS)).reshape(1, LANES)
        y_ssd = _ssd_mixer(ps.reshape(bsz, seq, -1), ssd_conv_w[l], ssd_conv_b[l].reshape(1, -1),
                           pad4(ssd_a_log[l]), pad4(ssd_dt_bias[l]),
                           jnp.repeat(ssd_d[l], SSD_HEADDIM).reshape(1, -1),
                           ssd_norm_g[l].reshape(1, -1))
        lam_vecs = jnp.pad(jnp.stack([diff_lq1[l], diff_lk1[l], diff_lq2[l], diff_lk2[l]]),
                           ((0, 0), (0, LANES - DIFF_DQK)))
        lam_init = 0.8 - 0.6 * math.exp(-0.3 * l)
        y_diff = _diff_mixer(pd.reshape(bsz, seq, -1), lam_vecs,
                             jnp.tile(diff_norm_g[l], DIFF_HEADS).reshape(1, -1), lam_init)

        w_route = _pad_cols(jnp.concatenate(
            [router_g[l], router_e[l].reshape(d, N_EXPERTS)], axis=1), LANES)
        w_route_hi = w_route.astype(BF16)
        w_route = jnp.concatenate(
            [w_route_hi, (w_route - w_route_hi.astype(F32)).astype(BF16)], axis=1)
        ys = [y.reshape(t, W_MIX) for y in (y_conv, y_gla, y_ssd, y_diff)]
        xn, route, cnt = _out_proj(ys, x2d, w_o_b, l, ln1_g[l].reshape(1, -1),
                                   ln1_b[l].reshape(1, -1), w_route)

        idx, block_e, n_used, n_valid = _dispatch_plan(route, cnt, n_blocks)
        xs = _sc_scatter_rows(xn, idx, n_slots)
        y = _expert_ffn(xs, block_e, n_used, n_valid, w_gate, w_up, w_down, l)
        yg = _sc_gather_rows(y, idx)
        x2d = _combine_dense(route, xn, yg, ln2_g[l].reshape(1, -1), ln2_b[l].reshape(1, -1))
    return x2d.reshape(bsz, seq, d)
```

```python
import functools
import math

import jax
import jax.numpy as jnp
from jax import lax
from jax.experimental import pallas as pl
from jax.experimental.pallas import tpu as pltpu
from jax.experimental.pallas import tpu_sc as plsc

F32 = jnp.float32
BF16 = jnp.bfloat16
HI = lax.Precision.HIGHEST

D_MODEL = 1024
DEPTH = 2
W_MIX = 256
GLA_HEADS, GLA_DK, GLA_DV, GLA_RANK, GLA_TAU, GLA_CHUNK = 4, 32, 64, 16, 16.0, 64
GLA_ROWS = 256
SSD_HEADS, SSD_GROUPS, SSD_HEADDIM, SSD_STATE, SSD_CONV_K, SSD_CHUNK = 4, 2, 64, 128, 4, 128
DIFF_HEADS, DIFF_DQK, DIFF_DV = 4, 32, 64
N_GROUPS, EXPERTS_PER_GROUP, N_EXPERTS, D_EXPERT = 4, 8, 32, 512
ALPHA = (2 * DEPTH) ** 0.25
LN_EPS = 1e-5
RMS_EPS = 1e-6

LANES = 128
SUBLANES = 8
PROJ_WIDTHS = (768, 896, 1152, 768)
VMEM_LIMIT = 56 * 1024 * 1024

MOE_BLK = 512
TOK_TILE = 256


def _cparams(sem):
    return pltpu.CompilerParams(dimension_semantics=sem, vmem_limit_bytes=VMEM_LIMIT)


def _sigmoid(x):
    return 1.0 / (1.0 + jnp.exp(-x))


def _softplus(x):
    return jnp.maximum(x, 0.0) + jnp.log(1.0 + jnp.exp(-jnp.abs(x)))


def _layer_norm(h, g, b):
    mu = jnp.mean(h, axis=-1, keepdims=True)
    d = h - mu
    var = jnp.mean(d * d, axis=-1, keepdims=True)
    return d * lax.rsqrt(var + LN_EPS) * g + b


def _dot_nt(a, b):
    return lax.dot_general(a, b, (((1,), (1,)), ((), ())), preferred_element_type=F32)


def _dot_tn(a, b, precision=None):
    return lax.dot_general(a, b, (((0,), (0,)), ((), ())), preferred_element_type=F32,
                           precision=precision)


def _split_bf16(x, parts):
    out = []
    for _ in range(parts - 1):
        hi = x.astype(BF16)
        out.append(hi)
        x = x - hi.astype(F32)
    out.append(x.astype(BF16))
    return out


def _dot(a, b):
    return jnp.dot(a, b, preferred_element_type=F32)


U32 = jnp.uint32


def _pack_halves(x):
    w = x.shape[1] // 2
    hi = lax.bitcast_convert_type(x[:, :w].astype(BF16).astype(F32), U32)
    lo = lax.bitcast_convert_type(x[:, w:].astype(BF16).astype(F32), U32)
    return hi | lax.shift_right_logical(lo, U32(16))


def _unpack_halves(p):
    hi = lax.bitcast_convert_type(p & U32(0xFFFF0000), F32)
    lo = lax.bitcast_convert_type(lax.shift_left(p, U32(16)), F32)
    return hi, lo


def _dot_split_lhs(a, b_exact, parts, dot=_dot):
    acc = None
    for term in _split_bf16(a, parts):
        d = dot(term, b_exact)
        acc = d if acc is None else acc + d
    return acc


def _dot_split_rhs(a_exact, b, parts):
    acc = None
    for term in _split_bf16(b, parts):
        d = jnp.dot(a_exact, term, preferred_element_type=F32)
        acc = d if acc is None else acc + d
    return acc


def _proj_kernel(x_ref, w_ref, oc_ref, og_ref, os_ref, od_ref):
    xb = x_ref[...].astype(BF16)
    off = 0
    for o_ref in (oc_ref, og_ref, os_ref, od_ref):
        n = o_ref.shape[-1]
        o_ref[...] = jnp.dot(xb, w_ref[:, off:off + n], preferred_element_type=F32)
        off += n


def _in_proj(x2d, w_r, layer):
    t = x2d.shape[0]
    tm = 512
    ncol = sum(PROJ_WIDTHS)
    return pl.pallas_call(
        _proj_kernel,
        grid=(t // tm,),
        in_specs=[pl.BlockSpec((tm, D_MODEL), lambda i: (i, 0)),
                  pl.BlockSpec((None, D_MODEL, ncol), lambda i: (layer, 0, 0))],
        out_specs=[pl.BlockSpec((tm, n), lambda i: (i, 0)) for n in PROJ_WIDTHS],
        out_shape=[jax.ShapeDtypeStruct((t, n), F32) for n in PROJ_WIDTHS],
        compiler_params=_cparams(("parallel",)),
        name="in_proj",
    )(x2d, w_r)


def _conv_kernel(p_ref, w_ref, o_ref):
    u = p_ref[0, :, 0:W_MIX]
    gb = p_ref[0, :, W_MIX:2 * W_MIX]
    gc = p_ref[0, :, 2 * W_MIX:3 * W_MIX]
    cu = gc * u
    row = lax.broadcasted_iota(jnp.int32, cu.shape, 0)
    acc = cu * w_ref[2:3, :]
    for s in (1, 2):
        sh = jnp.where(row >= s, pltpu.roll(cu, s, axis=0), 0.0)
        acc = acc + sh * w_ref[2 - s:3 - s, :]
    o_ref[0] = (gb * acc).astype(o_ref.dtype)


def _conv_mixer(pc, conv_w):
    b, s, _ = pc.shape
    return pl.pallas_call(
        _conv_kernel,
        grid=(b,),
        in_specs=[pl.BlockSpec((1, s, 3 * W_MIX), lambda i: (i, 0, 0)),
                  pl.BlockSpec((3, W_MIX), lambda i: (0, 0))],
        out_specs=pl.BlockSpec((1, s, W_MIX), lambda i: (i, 0, 0)),
        out_shape=jax.ShapeDtypeStruct((b, s, W_MIX), BF16),
        compiler_params=_cparams(("parallel",)),
        name="conv_mixer",
    )(pc, conv_w)


def _gla_kernel(p_ref, wlr_ref, blr_ref, ng_ref, o_ref, st_ref):
    c = GLA_CHUNK
    s_len = p_ref.shape[1]
    nh, dk, dv = GLA_HEADS, GLA_DK, GLA_DV
    st_ref[...] = jnp.zeros_like(st_ref)

    rb = GLA_ROWS
    ncb = rb // c
    ri = lax.broadcasted_iota(jnp.int32, (rb, rb), 0)
    ci = lax.broadcasted_iota(jnp.int32, (rb, rb), 1)
    tri = (ci <= ri).astype(BF16)
    klane_head = lax.broadcasted_iota(jnp.int32, (1, nh * dk), 1) // dk
    vlane_head = lax.broadcasted_iota(jnp.int32, (1, nh * dv), 1) // dv
    strow_head = lax.broadcasted_iota(jnp.int32, (nh * dv, 1), 0) // dv
    st_mask = strow_head == klane_head
    r4 = lax.broadcasted_iota(jnp.int32, (nh * c, c), 0) % c
    c4 = lax.broadcasted_iota(jnp.int32, (nh * c, c), 1)
    causal4 = c4 <= r4
    gi = lax.broadcasted_iota(jnp.int32, (nh * dv, nh * dv), 0) // dv
    gj = lax.broadcasted_iota(jnp.int32, (nh * dv, nh * dv), 1) // dv
    gmean = jnp.where(gi == gj, 1.0 / dv, 0.0).astype(BF16)
    wlr_hi, wlr_lo = _split_bf16(wlr_ref[...], 2)

    def body(n, carry):
        r0 = pl.multiple_of(n * rb, rb)
        rows = pl.ds(r0, rb)
        q = p_ref[0, rows, 0:128] * (dk ** -0.5)
        k = p_ref[0, rows, 128:256]
        vb = p_ref[0, rows, 256:512].astype(BF16)
        g = p_ref[0, rows, 512:768]
        lr = p_ref[0, rows, 768:896]
        lr_hi, lr_lo = _split_bf16(lr, 2)
        z = (jnp.dot(lr_hi, wlr_hi, preferred_element_type=F32)
             + jnp.dot(lr_hi, wlr_lo, preferred_element_type=F32)
             + jnp.dot(lr_lo, wlr_hi, preferred_element_type=F32)) + blr_ref[...]
        log_a = (jnp.minimum(z, 0.0) - jnp.log(1.0 + jnp.exp(-jnp.abs(z)))) * (1.0 / GLA_TAU)
        cumb = _dot_split_rhs(tri, log_a, 3)
        ends = [cumb[(j + 1) * c - 1:(j + 1) * c, :] for j in range(ncb)]
        starts = [jnp.zeros_like(ends[0])] + ends[:-1]
        cum = cumb - jnp.concatenate([jnp.broadcast_to(s0, (c, nh * dk)) for s0 in starts], axis=0)
        lasts = [e - s0 for e, s0 in zip(ends, starts)]
        cl = jnp.concatenate([jnp.broadcast_to(x, (c, nh * dk)) for x in lasts], axis=0)
        q_dec = q * jnp.exp(cum)
        k_inv = (k * jnp.exp(-cum)).astype(BF16)
        k_end = (k * jnp.exp(cl - cum)).astype(BF16)
        st = st_ref[...]
        outs = []
        for j in range(ncb):
            sl = slice(j * c, (j + 1) * c)
            qd = q_dec[sl]
            qs = jnp.concatenate([jnp.where(klane_head == h, qd, 0.0) for h in range(nh)],
                                 axis=0).astype(BF16)
            att = jnp.where(causal4, _dot_nt(qs, k_inv[sl]), 0.0)
            r = jnp.dot(att.astype(BF16), vb[sl], preferred_element_type=F32)
            o = jnp.where(vlane_head == 0, r[0:c], 0.0)
            for h in range(1, nh):
                o = o + jnp.where(vlane_head == h, r[h * c:(h + 1) * c], 0.0)
            outs.append(o + _dot_nt(qd.astype(BF16), st.astype(BF16)))
            d_st = _dot_tn(vb[sl], k_end[sl])
            st = st * jnp.exp(lasts[j]) + jnp.where(st_mask, d_st, 0.0)
        st_ref[...] = st
        o = jnp.concatenate(outs, axis=0)
        ms = _dot_split_lhs(o * o, gmean, 2)
        o = o * lax.rsqrt(ms + RMS_EPS) * ng_ref[...]
        o_ref[0, rows, :] = (o * (g * _sigmoid(g))).astype(o_ref.dtype)
        return carry

    lax.fori_loop(0, s_len // rb, body, 0)


def _gla_mixer(pg, w_lr_pad, b_lr, norm_g4):
    b, s, wp = pg.shape
    return pl.pallas_call(
        _gla_kernel,
        grid=(b,),
        in_specs=[pl.BlockSpec((1, s, wp), lambda i: (i, 0, 0)),
                  pl.BlockSpec((LANES, LANES), lambda i: (0, 0)),
                  pl.BlockSpec((1, LANES), lambda i: (0, 0)),
                  pl.BlockSpec((1, W_MIX), lambda i: (0, 0))],
        out_specs=pl.BlockSpec((1, s, W_MIX), lambda i: (i, 0, 0)),
        out_shape=jax.ShapeDtypeStruct((b, s, W_MIX), BF16),
        scratch_shapes=[pltpu.VMEM((GLA_HEADS * GLA_DV, GLA_HEADS * GLA_DK), F32)],
        compiler_params=_cparams(("parallel",)),
        name="gla_mixer",
    )(pg, w_lr_pad, b_lr, norm_g4)


def _ssd_kernel(p_ref, cw_ref, cb_ref, alog_ref, dtb_ref, dsk_ref, ng_ref, o_ref, st_ref):
    c = SSD_CHUNK
    s_len = p_ref.shape[1]
    n_st = SSD_STATE
    st_ref[...] = jnp.zeros_like(st_ref)

    ri = lax.broadcasted_iota(jnp.int32, (c, c), 0)
    ci = lax.broadcasted_iota(jnp.int32, (c, c), 1)
    causal = ci <= ri
    tri = causal.astype(BF16)
    upper = (ri <= ci).astype(BF16)
    lane_head = lax.broadcasted_iota(jnp.int32, (1, W_MIX), 1) // SSD_HEADDIM
    lane_group = lane_head // (SSD_HEADS // SSD_GROUPS)
    eh = lax.broadcasted_iota(jnp.int32, (LANES, W_MIX), 0)
    el = lax.broadcasted_iota(jnp.int32, (LANES, W_MIX), 1) // SSD_HEADDIM
    expand = (eh == el).astype(BF16)
    row8 = lax.broadcasted_iota(jnp.int32, (8, 3 * W_MIX), 0)
    a_c = -jnp.exp(alog_ref[...])

    def body(n, carry):
        r0 = pl.multiple_of(n * c, c)
        rows = pl.ds(r0, c)
        cur = p_ref[0, rows, 256:1024]
        p0 = pl.multiple_of(jnp.maximum(r0 - 8, 0), 8)
        prev8 = p_ref[0, pl.ds(p0, 8), 256:1024]
        prev8 = jnp.where(n > 0, prev8, 0.0)
        acc = cur * cw_ref[3:4, :] + cb_ref[...]
        for s in (1, 2, 3):
            sh = pltpu.roll(cur, s, axis=0)
            top = jnp.where(row8 < s, pltpu.roll(prev8, s, axis=0), sh[0:8])
            sh = jnp.concatenate([top, sh[8:]], axis=0)
            acc = acc + sh * cw_ref[3 - s:4 - s, :]
        xbc = acc * _sigmoid(acc)
        x = xbc[:, 0:256]
        bm = xbc[:, 256:512].astype(BF16)
        cm = xbc[:, 512:768].astype(BF16)

        dt_c = _softplus(p_ref[0, rows, 1024:1152] + dtb_ref[...])
        da_c = dt_c * a_c
        cum_c = _dot_split_rhs(tri, da_c, 3)
        cum_r = _dot_split_lhs(da_c, upper, 3, dot=_dot_tn)
        both_x = _dot_split_lhs(jnp.concatenate([dt_c, cum_c], axis=0), expand, 3)
        dt_x = both_x[0:c]
        cum_x = both_x[c:2 * c]
        cl_x = cum_x[c - 1:c, :]
        x_dt = x * dt_x
        x_dt_b = x_dt.astype(BF16)
        xw_b = (x_dt * jnp.exp(cl_x - cum_x)).astype(BF16)

        y = x * dsk_ref[...]
        y_off = jnp.zeros((c, W_MIX), F32)
        for g in range(SSD_GROUPS):
            bg = bm[:, g * n_st:(g + 1) * n_st]
            cg = cm[:, g * n_st:(g + 1) * n_st]
            cb = _dot_nt(cg, bg)
            for r in range(SSD_HEADS // SSD_GROUPS):
                h = g * (SSD_HEADS // SSD_GROUPS) + r
                diff = cum_c[:, h:h + 1] - cum_r[h:h + 1, :]
                dec = jnp.exp(jnp.where(causal, diff, -jnp.inf))
                m = (cb * dec).astype(BF16)
                yh = jnp.dot(m, x_dt_b, preferred_element_type=F32)
                y = y + jnp.where(lane_head == h, yh, 0.0)
            st = st_ref[g]
            y_off = y_off + jnp.where(lane_group == g,
                                      jnp.dot(cg, st.astype(BF16), preferred_element_type=F32), 0.0)
            d_st = _dot_tn(bg, xw_b)
            st_ref[g] = st * jnp.exp(cl_x) + jnp.where(lane_group == g, d_st, 0.0)
        y = y + y_off * jnp.exp(cum_x)
        zg = p_ref[0, rows, 0:256]
        y = y * (zg * _sigmoid(zg))
        outs = []
        for g in range(SSD_GROUPS):
            yg = y[:, g * 128:(g + 1) * 128]
            ms = jnp.mean(yg * yg, axis=-1, keepdims=True)
            outs.append(yg * lax.rsqrt(ms + RMS_EPS))
        o_ref[0, rows, :] = (jnp.concatenate(outs, axis=-1) * ng_ref[...]).astype(o_ref.dtype)
        return carry

    def pair(u, carry):
        body(2 * u, carry)
        return body(2 * u + 1, carry)

    lax.fori_loop(0, s_len // (2 * c), pair, 0)


def _ssd_mixer(ps, conv_w, conv_b, a_log_c, dt_bias_c, d_x, norm_g):
    b, s, wp = ps.shape
    full2 = lambda i: (0, 0)
    return pl.pallas_call(
        _ssd_kernel,
        grid=(b,),
        in_specs=[pl.BlockSpec((1, s, wp), lambda i: (i, 0, 0)),
                  pl.BlockSpec((SSD_CONV_K, 3 * W_MIX), full2),
                  pl.BlockSpec((1, 3 * W_MIX), full2),
                  pl.BlockSpec((1, LANES), full2),
                  pl.BlockSpec((1, LANES), full2),
                  pl.BlockSpec((1, W_MIX), full2),
                  pl.BlockSpec((1, W_MIX), full2)],
        out_specs=pl.BlockSpec((1, s, W_MIX), lambda i: (i, 0, 0)),
        out_shape=jax.ShapeDtypeStruct((b, s, W_MIX), BF16),
        scratch_shapes=[pltpu.VMEM((SSD_GROUPS, SSD_STATE, W_MIX), F32)],
        compiler_params=_cparams(("parallel",)),
        name="ssd_mixer",
    )(ps, conv_w, conv_b, a_log_c, dt_bias_c, d_x, norm_g)


DIFF_TQ = 256
DIFF_TK = 256
LOG2E = 1.4426950408889634
DIFF_VPAD = DIFF_DV + 16


def _diff_kernel(q_ref, k_ref, v_ref, lam_ref, ng_ref, o_ref,
                 kb_ref, vt_ref, qs_ref, st_ref, m_ref, acc_ref, *, lam_init):
    tq, tk = DIFF_TQ, DIFF_TK
    nh, dv = DIFF_HEADS, DIFF_DV
    nhc = 2 * nh
    s_len = k_ref.shape[1]
    i = pl.program_id(1)

    @pl.when(i == 0)
    def _():
        kb_ref[...] = k_ref[0].astype(BF16)
        for cblk in range(s_len // tk):
            cols = slice(cblk * tk, (cblk + 1) * tk)
            vt = v_ref[0, cols, :].T.astype(BF16)
            for h in range(nh):
                vt_ref[h, 0:dv, cols] = vt[h * dv:(h + 1) * dv]
        vt_ref[:, dv:, :] = jnp.ones((nh, DIFF_VPAD - dv, s_len), BF16)

    q = q_ref[0] * (DIFF_DQK ** -0.5 * LOG2E)
    qlane = lax.broadcasted_iota(jnp.int32, (1, W_MIX), 1) // DIFF_DQK
    for hc in range(nhc):
        qs_ref[hc * tq:(hc + 1) * tq, :] = jnp.where(qlane == hc, q, 0.0).astype(BF16)
    m_ref[...] = jnp.full_like(m_ref, -jnp.inf)
    acc_ref[...] = jnp.zeros_like(acc_ref)
    krow = lax.broadcasted_iota(jnp.int32, (tk, nhc * tq), 0)
    qcol = lax.broadcasted_iota(jnp.int32, (tk, nhc * tq), 1) % tq
    diag_ok = krow <= qcol

    def scores(j, slot):
        k0 = pl.multiple_of(j * tk, tk)
        st_ref[slot] = _dot_nt(kb_ref[pl.ds(k0, tk), :], qs_ref[...])

    def softmax_pv(j, slot, masked):
        k0 = pl.multiple_of(j * tk, tk)
        st = st_ref[slot]
        if masked:
            st = jnp.where(diag_ok, st, -jnp.inf)
        m_prev = m_ref[...]
        m_new = jnp.maximum(m_prev, jnp.max(st, axis=0, keepdims=True))
        alpha = jnp.exp2(m_prev - m_new)
        p = jnp.exp2(st - m_new)
        m_ref[...] = m_new
        pb = p.astype(BF16)
        for hc in range(nhc):
            h = hc // 2
            lanes = slice(hc * tq, (hc + 1) * tq)
            pv = jnp.dot(vt_ref[h, :, pl.ds(k0, tk)], pb[:, lanes],
                         preferred_element_type=F32)
            acc_ref[hc] = acc_ref[hc] * alpha[:, lanes] + pv

    scores(0, 0)
    n_pairs = i // 2

    def pair_step(u, carry):
        scores(2 * u + 1, 1)
        softmax_pv(2 * u, 0, False)
        scores(2 * u + 2, 0)
        softmax_pv(2 * u + 1, 1, False)
        return carry

    lax.fori_loop(0, n_pairs, pair_step, 0)

    @pl.when(i % 2 == 0)
    def _():
        softmax_pv(i, 0, True)

    @pl.when(i % 2 == 1)
    def _():
        scores(i, 1)
        softmax_pv(i - 1, 0, False)
        softmax_pv(i, 1, True)

    lam = (jnp.exp(jnp.sum(lam_ref[0:1, :] * lam_ref[1:2, :], axis=-1, keepdims=True))
           - jnp.exp(jnp.sum(lam_ref[2:3, :] * lam_ref[3:4, :], axis=-1, keepdims=True))
           + lam_init)
    heads = []
    for h in range(nh):
        o1 = acc_ref[2 * h, 0:dv] / acc_ref[2 * h, dv:dv + 1]
        o2 = acc_ref[2 * h + 1, 0:dv] / acc_ref[2 * h + 1, dv:dv + 1]
        oh = o1 - lam * o2
        ms = jnp.mean(oh * oh, axis=0, keepdims=True)
        heads.append(oh * lax.rsqrt(ms + RMS_EPS))
    o = jnp.concatenate(heads, axis=0).T
    o_ref[0] = (o * ng_ref[...] * (1.0 - lam_init)).astype(o_ref.dtype)


def _diff_mixer(pd, lam_vecs, norm_g4, lam_init):
    b, s, _ = pd.shape
    tq = DIFF_TQ
    return pl.pallas_call(
        functools.partial(_diff_kernel, lam_init=lam_init),
        grid=(b, s // tq),
        in_specs=[pl.BlockSpec((1, tq, W_MIX), lambda bi, i: (bi, i, 0)),
                  pl.BlockSpec((1, s, W_MIX), lambda bi, i: (bi, 0, 1)),
                  pl.BlockSpec((1, s, W_MIX), lambda bi, i: (bi, 0, 2)),
                  pl.BlockSpec((4, LANES), lambda bi, i: (0, 0)),
                  pl.BlockSpec((1, W_MIX), lambda bi, i: (0, 0))],
        out_specs=pl.BlockSpec((1, tq, W_MIX), lambda bi, i: (bi, i, 0)),
        out_shape=jax.ShapeDtypeStruct((b, s, W_MIX), BF16),
        scratch_shapes=[pltpu.VMEM((s, W_MIX), BF16),
                        pltpu.VMEM((DIFF_HEADS, DIFF_VPAD, s), BF16),
                        pltpu.VMEM((2 * DIFF_HEADS * tq, W_MIX), BF16),
                        pltpu.VMEM((2, DIFF_TK, 2 * DIFF_HEADS * tq), F32),
                        pltpu.VMEM((1, 2 * DIFF_HEADS * tq), F32),
                        pltpu.VMEM((2 * DIFF_HEADS, DIFF_VPAD, tq), F32)],
        compiler_params=_cparams(("parallel", "arbitrary")),
        name="diff_attn",
    )(pd, pd, pd, lam_vecs, norm_g4)


def _oproj_kernel(yc_ref, yg_ref, ys_ref, yd_ref, x_ref, wo_ref, g_ref, b_ref, wr_ref,
                  xo_ref, xp_ref, route_ref, cnt_ref):
    mix = jnp.concatenate([yc_ref[...], yg_ref[...], ys_ref[...], yd_ref[...]], axis=-1)
    h = ALPHA * x_ref[...] + jnp.dot(mix, wo_ref[...], preferred_element_type=F32)
    xn = _layer_norm(h, g_ref[...], b_ref[...])
    xo_ref[...] = xn
    xp_ref[...] = _pack_halves(xn)

    xn_hi, xn_lo = _split_bf16(xn, 2)
    both = _dot(xn_hi, wr_ref[...])
    logits = both[:, 0:LANES] + both[:, LANES:2 * LANES] + _dot(xn_lo, wr_ref[:, 0:LANES])
    lane = lax.broadcasted_iota(jnp.int32, logits.shape, 1).astype(F32)
    neg = -jnp.inf
    big = float(LANES)
    lg = jnp.where(lane < N_GROUPS, logits, neg)
    mg = jnp.max(lg, axis=-1, keepdims=True)
    sg = jnp.sum(jnp.exp(lg - mg), axis=-1, keepdims=True)
    grp = jnp.min(jnp.where(lg == mg, lane, big), axis=-1, keepdims=True)
    p_grp = 1.0 / sg
    lo = N_GROUPS + EXPERTS_PER_GROUP * grp
    in_g = jnp.logical_and(lane >= lo, lane < lo + EXPERTS_PER_GROUP)
    le = jnp.where(in_g, logits, neg)
    me = jnp.max(le, axis=-1, keepdims=True)
    ee = jnp.exp(le - me)
    pe = ee / jnp.sum(ee, axis=-1, keepdims=True)
    pe = jnp.where(in_g, pe, -1.0)
    p1 = jnp.max(pe, axis=-1, keepdims=True)
    i1 = jnp.min(jnp.where(pe == p1, lane, big), axis=-1, keepdims=True)
    pe2 = jnp.where(lane == i1, -1.0, pe)
    p2 = jnp.max(pe2, axis=-1, keepdims=True)
    i2 = jnp.min(jnp.where(pe2 == p2, lane, big), axis=-1, keepdims=True)
    den = p1 + p2
    g1 = p_grp * p1 / den
    g2 = p_grp * p2 / den
    e1 = i1 - N_GROUPS
    e2 = i2 - N_GROUPS
    route_ref[...] = jnp.where(lane == 0, e1, jnp.where(lane == 1, e2, jnp.where(
        lane == 2, g1, jnp.where(lane == 3, g2, 0.0))))

    @pl.when(pl.program_id(0) == 0)
    def _():
        cnt_ref[...] = jnp.zeros_like(cnt_ref)

    hits = jnp.where(lane == e1, 1.0, 0.0) + jnp.where(lane == e2, 1.0, 0.0)
    cnt_ref[...] += jnp.sum(hits, axis=0, keepdims=True)


def _out_proj(ys, x2d, w_o, layer, ln_g, ln_b, w_route):
    t = x2d.shape[0]
    tm = 512
    row = lambda i: (i, 0)
    full = lambda i: (0, 0)
    return pl.pallas_call(
        _oproj_kernel,
        grid=(t // tm,),
        in_specs=[pl.BlockSpec((tm, W_MIX), row)] * 4 + [
            pl.BlockSpec((tm, D_MODEL), row),
            pl.BlockSpec((None, D_MODEL, D_MODEL), lambda i: (layer, 0, 0)),
            pl.BlockSpec((1, D_MODEL), full),
            pl.BlockSpec((1, D_MODEL), full),
            pl.BlockSpec((D_MODEL, 2 * LANES), full)],
        out_specs=[pl.BlockSpec((tm, D_MODEL), row), pl.BlockSpec((tm, D_MODEL // 2), row),
                   pl.BlockSpec((tm, LANES), row), pl.BlockSpec((1, LANES), full)],
        out_shape=[jax.ShapeDtypeStruct((t, D_MODEL), F32),
                   jax.ShapeDtypeStruct((t, D_MODEL // 2), U32),
                   jax.ShapeDtypeStruct((t, LANES), F32),
                   jax.ShapeDtypeStruct((1, LANES), F32)],
        compiler_params=_cparams(("arbitrary",)),
        name="out_proj_ln_router",
    )(*ys, x2d, w_o, ln_g, ln_b, w_route)


PLAN_TILE = 512


def _plan_kernel(route_ref, cnt_ref, dest_ref, meta_ref, carry_ref, pstart_ref):
    tm = route_ref.shape[0]
    lane = lax.broadcasted_iota(jnp.int32, (1, LANES), 1).astype(F32)

    @pl.when(pl.program_id(0) == 0)
    def _():
        cnt = cnt_ref[...]
        padded = jnp.ceil(cnt * (1.0 / MOE_BLK)) * MOE_BLK
        li = lax.broadcasted_iota(jnp.int32, (LANES, LANES), 0)
        lj = lax.broadcasted_iota(jnp.int32, (LANES, LANES), 1)
        before = (li < lj).astype(F32)
        pstart = jnp.dot(jnp.broadcast_to(padded, (8, LANES)), before, precision=HI,
                         preferred_element_type=F32)[0:1]
        pstart_ref[...] = pstart
        carry_ref[...] = jnp.zeros_like(carry_ref)
        meta_ref[...] = jnp.concatenate(
            [pstart + padded, pstart, cnt, jnp.zeros((5, LANES), F32)], axis=0)

    oh0 = jnp.where(lane == route_ref[:, 0:1], 1.0, 0.0)
    oh1 = jnp.where(lane == route_ref[:, 1:2], 1.0, 0.0)
    both = oh0 + oh1
    ri = lax.broadcasted_iota(jnp.int32, (tm, tm), 0)
    ci = lax.broadcasted_iota(jnp.int32, (tm, tm), 1)
    earlier = (ci < ri).astype(BF16)
    base = (jnp.dot(earlier, both.astype(BF16), preferred_element_type=F32)
            + carry_ref[...] + pstart_ref[...])
    d0 = jnp.sum(oh0 * base, axis=-1, keepdims=True)
    d1 = jnp.sum(oh1 * base, axis=-1, keepdims=True)
    dest_ref[...] = jnp.where(lane == 0, d0, jnp.where(lane == 1, d1, 0.0))
    carry_ref[...] += jnp.sum(both, axis=0, keepdims=True)


def _dispatch_plan(route, cnt, n_blocks):
    t = route.shape[0]
    tm = PLAN_TILE
    blk = MOE_BLK
    dest, meta = pl.pallas_call(
        _plan_kernel,
        grid=(t // tm,),
        in_specs=[pl.BlockSpec((tm, LANES), lambda i: (i, 0)),
                  pl.BlockSpec((1, LANES), lambda i: (0, 0))],
        out_specs=[pl.BlockSpec((tm, LANES), lambda i: (i, 0)),
                   pl.BlockSpec((8, LANES), lambda i: (0, 0))],
        out_shape=[jax.ShapeDtypeStruct((t, LANES), F32),
                   jax.ShapeDtypeStruct((8, LANES), F32)],
        scratch_shapes=[pltpu.VMEM((1, LANES), F32), pltpu.VMEM((1, LANES), F32)],
        compiler_params=_cparams(("arbitrary",)),
        name="moe_plan",
    )(route, cnt)
    pad_end = meta[0, :N_EXPERTS].astype(jnp.int32)
    starts = jnp.arange(n_blocks, dtype=jnp.int32) * blk
    block_e = jnp.minimum(jnp.sum((pad_end[None, :] <= starts[:, None]).astype(jnp.int32), axis=1),
                          N_EXPERTS - 1)
    n_used = (pad_end[N_EXPERTS - 1] // blk).reshape(1)
    seg_end = (meta[1, :N_EXPERTS] + meta[2, :N_EXPERTS]).astype(jnp.int32)
    n_valid = jnp.clip(seg_end[block_e] - starts, 0, blk)
    idx = dest[:, 0:2].astype(jnp.int32).T.reshape(2 * t)
    return idx, block_e, n_used, n_valid


def _ffn_kernel(be_ref, nu_ref, nv_ref, xs_ref, wg_ref, wu_ref, wd_ref, y_ref,
                wgb_ref, wub_ref, wdb_ref):
    i = pl.program_id(0)
    prev = be_ref[jnp.maximum(i - 1, 0)]

    @pl.when(jnp.logical_or(i == 0, be_ref[i] != prev))
    def _():
        wgb_ref[...] = wg_ref[...].astype(BF16)
        wub_ref[...] = wu_ref[...].astype(BF16)
        wdb_ref[...] = wd_ref[...].astype(BF16)

    @pl.when(i < nu_ref[0])
    def _():
        half = xs_ref.shape[0] // 2
        for r in range(2):
            rows = slice(r * half, (r + 1) * half)
            row = lax.broadcasted_iota(jnp.int32, (half, 1), 0) + r * half
            xp = jnp.where(row < nv_ref[i], xs_ref[rows, :], U32(0))
            x_hi, x_lo = _unpack_halves(xp)
            xb = jnp.concatenate([x_hi.astype(BF16), x_lo.astype(BF16)], axis=1)
            a = jnp.dot(xb, wgb_ref[...], preferred_element_type=F32)
            u = jnp.dot(xb, wub_ref[...], preferred_element_type=F32)
            h = (a * _sigmoid(a) * u).astype(BF16)
            y_ref[rows, :] = _pack_halves(jnp.dot(h, wdb_ref[...], preferred_element_type=F32))

    @pl.when(i >= nu_ref[0])
    def _():
        y_ref[...] = jnp.zeros_like(y_ref)


def _expert_ffn(xs, block_e, n_used, n_valid, w_gate, w_up, w_down, layer):
    n_slots = xs.shape[0]
    blk = MOE_BLK
    w_map = lambda i, be, nu, nv: (layer, be[i], 0, 0)
    grid_spec = pltpu.PrefetchScalarGridSpec(
        num_scalar_prefetch=3,
        grid=(n_slots // blk,),
        in_specs=[pl.BlockSpec((blk, D_MODEL // 2),
                               lambda i, be, nu, nv: (jnp.minimum(i, nu[0] - 1), 0)),
                  pl.BlockSpec((None, None, D_MODEL, D_EXPERT), w_map),
                  pl.BlockSpec((None, None, D_MODEL, D_EXPERT), w_map),
                  pl.BlockSpec((None, None, D_EXPERT, D_MODEL), w_map)],
        out_specs=pl.BlockSpec((blk, D_MODEL // 2), lambda i, be, nu, nv: (i, 0)),
        scratch_shapes=[pltpu.VMEM((D_MODEL, D_EXPERT), BF16),
                        pltpu.VMEM((D_MODEL, D_EXPERT), BF16),
                        pltpu.VMEM((D_EXPERT, D_MODEL), BF16)],
    )
    return pl.pallas_call(
        _ffn_kernel,
        grid_spec=grid_spec,
        out_shape=jax.ShapeDtypeStruct((n_slots, D_MODEL // 2), U32),
        compiler_params=_cparams(("arbitrary",)),
        name="expert_ffn",
    )(block_e, n_used, n_valid, xs, w_gate, w_up, w_down)


SC_CORES = 2
SC_SUBCORES = 16
SC_ROWS = 64


def _sc_gather_rows(table, idx):
    b = idx.shape[0]
    d = table.shape[1]
    per_w = b // (SC_CORES * SC_SUBCORES)
    mesh = plsc.VectorSubcoreMesh(core_axis_name="c", subcore_axis_name="s")

    n_chunks = per_w // SC_ROWS

    @functools.partial(
        pl.kernel, mesh=mesh,
        out_type=jax.ShapeDtypeStruct((b, d), table.dtype),
        scratch_types=[pltpu.VMEM((SC_ROWS,), jnp.int32), pltpu.VMEM((SC_ROWS,), jnp.int32),
                       pltpu.VMEM((SC_ROWS, d), table.dtype),
                       pltpu.VMEM((SC_ROWS, d), table.dtype),
                       pltpu.SemaphoreType.DMA, pltpu.SemaphoreType.DMA,
                       pltpu.SemaphoreType.DMA, pltpu.SemaphoreType.DMA],
        name="sc_gather_rows",
    )
    def gather(table_hbm, idx_hbm, out_hbm, idx0, idx1, rows0, rows1, gs0, gs1, ws0, ws1):
        idx_v, rows_v, gsem, wsem = (idx0, idx1), (rows0, rows1), (gs0, gs1), (ws0, ws1)
        wid = lax.axis_index("s") * SC_CORES + lax.axis_index("c")
        base = wid * per_w

        def rows_of(c):
            return pl.ds(pl.multiple_of(base + c * SC_ROWS, SC_ROWS), SC_ROWS)

        def start_gather(c, s):
            pltpu.sync_copy(idx_hbm.at[rows_of(c)], idx_v[s])
            pltpu.async_copy(table_hbm.at[idx_v[s]], rows_v[s], gsem[s])

        def write_back(c, s):
            pltpu.make_async_copy(table_hbm.at[idx_v[s]], rows_v[s], gsem[s]).wait()
            pltpu.async_copy(rows_v[s], out_hbm.at[rows_of(c)], wsem[s]).wait()

        start_gather(0, 0)

        @pl.loop(0, n_chunks, step=2)
        def _(c):
            start_gather(c + 1, 1)
            write_back(c, 0)

            @pl.when(c + 2 < n_chunks)
            def _():
                start_gather(c + 2, 0)

            write_back(c + 1, 1)

    return gather(table, idx)


def _sc_scatter_rows(x2d, idx, n_slots):
    t, d = x2d.shape
    per_w = t // (SC_CORES * SC_SUBCORES)
    mesh = plsc.VectorSubcoreMesh(core_axis_name="c", subcore_axis_name="s")

    @functools.partial(
        pl.kernel, mesh=mesh,
        out_type=jax.ShapeDtypeStruct((n_slots, d), x2d.dtype),
        scratch_types=[pltpu.VMEM((SC_ROWS,), jnp.int32), pltpu.VMEM((SC_ROWS,), jnp.int32),
                       pltpu.VMEM((SC_ROWS, d), x2d.dtype),
                       pltpu.SemaphoreType.DMA, pltpu.SemaphoreType.DMA],
        name="sc_scatter_rows",
    )
    def scatter(x_hbm, idx_hbm, out_hbm, idx0, idx1, rows_v, s0, s1):
        wid = lax.axis_index("s") * SC_CORES + lax.axis_index("c")
        base = wid * per_w

        @pl.loop(0, per_w // SC_ROWS)
        def _(c):
            off = pl.multiple_of(base + c * SC_ROWS, SC_ROWS)
            pltpu.sync_copy(x_hbm.at[pl.ds(off, SC_ROWS)], rows_v)
            pltpu.sync_copy(idx_hbm.at[pl.ds(off, SC_ROWS)], idx0)
            pltpu.sync_copy(idx_hbm.at[pl.ds(t + off, SC_ROWS)], idx1)
            cp0 = pltpu.async_copy(rows_v, out_hbm.at[idx0], s0)
            cp1 = pltpu.async_copy(rows_v, out_hbm.at[idx1], s1)
            cp0.wait()
            cp1.wait()

    return scatter(x2d, idx)


def _combine_dense_kernel(route_ref, x_ref, y0_ref, y1_ref, g_ref, b_ref, o_ref):
    y0 = jnp.concatenate(_unpack_halves(y0_ref[...]), axis=1)
    y1 = jnp.concatenate(_unpack_halves(y1_ref[...]), axis=1)
    moe = route_ref[:, 2:3] * y0 + route_ref[:, 3:4] * y1
    h = ALPHA * x_ref[...] + moe
    o_ref[...] = _layer_norm(h, g_ref[...], b_ref[...])


def _combine_dense(route, x2d, yg, ln_g, ln_b):
    t = x2d.shape[0]
    tm = 512
    nt = t // tm
    row = lambda i: (i, 0)
    full = lambda i: (0, 0)
    return pl.pallas_call(
        _combine_dense_kernel,
        grid=(nt,),
        in_specs=[pl.BlockSpec((tm, LANES), row),
                  pl.BlockSpec((tm, D_MODEL), row),
                  pl.BlockSpec((tm, D_MODEL // 2), row),
                  pl.BlockSpec((tm, D_MODEL // 2), lambda i: (i + nt, 0)),
                  pl.BlockSpec((1, D_MODEL), full),
                  pl.BlockSpec((1, D_MODEL), full)],
        out_specs=pl.BlockSpec((tm, D_MODEL), row),
        out_shape=jax.ShapeDtypeStruct((t, D_MODEL), F32),
        compiler_params=_cparams(("parallel",)),
        name="moe_combine_dense",
    )(route, x2d, yg, yg, ln_g, ln_b)


def _pad_cols(w, n):
    return jnp.pad(w, [(0, 0)] * (w.ndim - 1) + [(0, n - w.shape[-1])])


def kernel(x, w_in, conv_w, gla_w_lr, gla_b_lr, gla_norm_g, ssd_conv_w, ssd_conv_b, ssd_a_log,
           ssd_d, ssd_dt_bias, ssd_norm_g, diff_lq1, diff_lk1, diff_lq2, diff_lk2, diff_norm_g,
           w_o, ln1_g, ln1_b, router_g, router_e, w_gate, w_up, w_down, ln2_g, ln2_b):
    bsz, seq, d = x.shape
    t = bsz * seq
    n_assign = 2 * t
    n_blocks = (n_assign + N_EXPERTS * (MOE_BLK - 1)) // MOE_BLK + 1
    n_slots = n_blocks * MOE_BLK
    x2d = x.reshape(t, d)
    w_in_r = jnp.concatenate([w_in[..., 0:768], _pad_cols(w_in[..., 768:1552], 896),
                              _pad_cols(w_in[..., 1552:2580], 1152), w_in[..., 2580:3348]],
                             axis=-1).astype(BF16)
    w_o_b = w_o.astype(BF16)
    for l in range(DEPTH):
        pc, pg, ps, pd = _in_proj(x2d, w_in_r, l)

        y_conv = _conv_mixer(pc.reshape(bsz, seq, -1), conv_w[l])
        w_lr_pad = jnp.pad(gla_w_lr[l], ((0, LANES - GLA_RANK), (0, 0)))
        y_gla = _gla_mixer(pg.reshape(bsz, seq, -1), w_lr_pad, gla_b_lr[l].reshape(1, -1),
                           jnp.tile(gla_norm_g[l], GLA_HEADS).reshape(1, -1))
        pad4 = lambda v: jnp.pad(v, (0, LANES - SSD_HEADS)).reshape(1, LANES)
        y_ssd = _ssd_mixer(ps.reshape(bsz, seq, -1), ssd_conv_w[l], ssd_conv_b[l].reshape(1, -1),
                           pad4(ssd_a_log[l]), pad4(ssd_dt_bias[l]),
                           jnp.repeat(ssd_d[l], SSD_HEADDIM).reshape(1, -1),
                           ssd_norm_g[l].reshape(1, -1))
        lam_vecs = jnp.pad(jnp.stack([diff_lq1[l], diff_lk1[l], diff_lq2[l], diff_lk2[l]]),
                           ((0, 0), (0, LANES - DIFF_DQK)))
        lam_init = 0.8 - 0.6 * math.exp(-0.3 * l)
        y_diff = _diff_mixer(pd.reshape(bsz, seq, -1), lam_vecs,
                             jnp.tile(diff_norm_g[l], DIFF_HEADS).reshape(1, -1), lam_init)

        w_route = _pad_cols(jnp.concatenate(
            [router_g[l], router_e[l].reshape(d, N_EXPERTS)], axis=1), LANES)
        w_route_hi = w_route.astype(BF16)
        w_route = jnp.concatenate(
            [w_route_hi, (w_route - w_route_hi.astype(F32)).astype(BF16)], axis=1)
        ys = [y.reshape(t, W_MIX) for y in (y_conv, y_gla, y_ssd, y_diff)]
        xn, xn_p, route, cnt = _out_proj(ys, x2d, w_o_b, l, ln1_g[l].reshape(1, -1),
                                         ln1_b[l].reshape(1, -1), w_route)

        idx, block_e, n_used, n_valid = _dispatch_plan(route, cnt, n_blocks)
        xs = _sc_scatter_rows(xn_p, idx, n_slots)
        y = _expert_ffn(xs, block_e, n_used, n_valid, w_gate, w_up, w_down, l)
        yg = _sc_gather_rows(y, idx)
        x2d = _combine_dense(route, xn, yg, ln2_g[l].reshape(1, -1), ln2_b[l].reshape(1, -1))
    return x2d.reshape(bsz, seq, d)
```

```python
import functools
import math

import jax
import jax.numpy as jnp
from jax import lax
from jax.experimental import pallas as pl
from jax.experimental.pallas import tpu as pltpu
from jax.experimental.pallas import tpu_sc as plsc

F32 = jnp.float32
BF16 = jnp.bfloat16
HI = lax.Precision.HIGHEST

D_MODEL = 1024
DEPTH = 2
W_MIX = 256
GLA_HEADS, GLA_DK, GLA_DV, GLA_RANK, GLA_TAU, GLA_CHUNK = 4, 32, 64, 16, 16.0, 64
GLA_ROWS = 256
SSD_HEADS, SSD_GROUPS, SSD_HEADDIM, SSD_STATE, SSD_CONV_K, SSD_CHUNK = 4, 2, 64, 128, 4, 128
DIFF_HEADS, DIFF_DQK, DIFF_DV = 4, 32, 64
N_GROUPS, EXPERTS_PER_GROUP, N_EXPERTS, D_EXPERT = 4, 8, 32, 512
ALPHA = (2 * DEPTH) ** 0.25
LN_EPS = 1e-5
RMS_EPS = 1e-6

LANES = 128
SUBLANES = 8
PROJ_WIDTHS = (768, 768, 128, 1024, 128, 768)
PROJ_DTYPES = (BF16, BF16, F32, BF16, F32, BF16)
VMEM_LIMIT = 56 * 1024 * 1024

MOE_BLK = 512
TOK_TILE = 256


def _cparams(sem):
    return pltpu.CompilerParams(dimension_semantics=sem, vmem_limit_bytes=VMEM_LIMIT)


def _sigmoid(x):
    return 1.0 / (1.0 + jnp.exp(-x))


def _softplus(x):
    return jnp.maximum(x, 0.0) + jnp.log(1.0 + jnp.exp(-jnp.abs(x)))


def _layer_norm(h, g, b):
    mu = jnp.mean(h, axis=-1, keepdims=True)
    d = h - mu
    var = jnp.mean(d * d, axis=-1, keepdims=True)
    return d * lax.rsqrt(var + LN_EPS) * g + b


def _dot_nt(a, b):
    return lax.dot_general(a, b, (((1,), (1,)), ((), ())), preferred_element_type=F32)


def _dot_tn(a, b, precision=None):
    return lax.dot_general(a, b, (((0,), (0,)), ((), ())), preferred_element_type=F32,
                           precision=precision)


def _split_bf16(x, parts):
    out = []
    for _ in range(parts - 1):
        hi = x.astype(BF16)
        out.append(hi)
        x = x - hi.astype(F32)
    out.append(x.astype(BF16))
    return out


def _dot(a, b):
    return jnp.dot(a, b, preferred_element_type=F32)


U32 = jnp.uint32


def _pack_halves(x):
    w = x.shape[1] // 2
    hi = lax.bitcast_convert_type(x[:, :w].astype(BF16).astype(F32), U32)
    lo = lax.bitcast_convert_type(x[:, w:].astype(BF16).astype(F32), U32)
    return hi | lax.shift_right_logical(lo, U32(16))


def _unpack_halves(p):
    hi = lax.bitcast_convert_type(p & U32(0xFFFF0000), F32)
    lo = lax.bitcast_convert_type(lax.shift_left(p, U32(16)), F32)
    return hi, lo


def _dot_split_lhs(a, b_exact, parts, dot=_dot):
    acc = None
    for term in _split_bf16(a, parts):
        d = dot(term, b_exact)
        acc = d if acc is None else acc + d
    return acc


def _dot_split_rhs(a_exact, b, parts):
    acc = None
    for term in _split_bf16(b, parts):
        d = jnp.dot(a_exact, term, preferred_element_type=F32)
        acc = d if acc is None else acc + d
    return acc


def _proj_kernel(x_ref, w_ref, *o_refs):
    xb = x_ref[...].astype(BF16)
    off = 0
    for o_ref in o_refs:
        n = o_ref.shape[-1]
        o_ref[...] = jnp.dot(xb, w_ref[:, off:off + n],
                             preferred_element_type=F32).astype(o_ref.dtype)
        off += n


def _in_proj(x2d, w_r, layer):
    t = x2d.shape[0]
    tm = 512
    ncol = sum(PROJ_WIDTHS)
    return pl.pallas_call(
        _proj_kernel,
        grid=(t // tm,),
        in_specs=[pl.BlockSpec((tm, D_MODEL), lambda i: (i, 0)),
                  pl.BlockSpec((None, D_MODEL, ncol), lambda i: (layer, 0, 0))],
        out_specs=[pl.BlockSpec((tm, n), lambda i: (i, 0)) for n in PROJ_WIDTHS],
        out_shape=[jax.ShapeDtypeStruct((t, n), dt) for n, dt in zip(PROJ_WIDTHS, PROJ_DTYPES)],
        compiler_params=_cparams(("parallel",)),
        name="in_proj",
    )(x2d, w_r)


def _conv_kernel(p_ref, w_ref, o_ref):
    u = p_ref[0, :, 0:W_MIX].astype(F32)
    gb = p_ref[0, :, W_MIX:2 * W_MIX].astype(F32)
    gc = p_ref[0, :, 2 * W_MIX:3 * W_MIX].astype(F32)
    cu = gc * u
    row = lax.broadcasted_iota(jnp.int32, cu.shape, 0)
    acc = cu * w_ref[2:3, :]
    for s in (1, 2):
        sh = jnp.where(row >= s, pltpu.roll(cu, s, axis=0), 0.0)
        acc = acc + sh * w_ref[2 - s:3 - s, :]
    o_ref[0] = (gb * acc).astype(o_ref.dtype)


def _conv_mixer(pc, conv_w):
    b, s, _ = pc.shape
    return pl.pallas_call(
        _conv_kernel,
        grid=(b,),
        in_specs=[pl.BlockSpec((1, s, 3 * W_MIX), lambda i: (i, 0, 0)),
                  pl.BlockSpec((3, W_MIX), lambda i: (0, 0))],
        out_specs=pl.BlockSpec((1, s, W_MIX), lambda i: (i, 0, 0)),
        out_shape=jax.ShapeDtypeStruct((b, s, W_MIX), BF16),
        compiler_params=_cparams(("parallel",)),
        name="conv_mixer",
    )(pc, conv_w)


def _gla_kernel(p_ref, lr_ref, wlr_ref, blr_ref, ng_ref, o_ref, st_ref):
    c = GLA_CHUNK
    s_len = p_ref.shape[1]
    nh, dk, dv = GLA_HEADS, GLA_DK, GLA_DV
    st_ref[...] = jnp.zeros_like(st_ref)

    rb = GLA_ROWS
    ncb = rb // c
    ri = lax.broadcasted_iota(jnp.int32, (rb, rb), 0)
    ci = lax.broadcasted_iota(jnp.int32, (rb, rb), 1)
    tri = (ci <= ri).astype(BF16)
    klane_head = lax.broadcasted_iota(jnp.int32, (1, nh * dk), 1) // dk
    vlane_head = lax.broadcasted_iota(jnp.int32, (1, nh * dv), 1) // dv
    strow_head = lax.broadcasted_iota(jnp.int32, (nh * dv, 1), 0) // dv
    st_mask = strow_head == klane_head
    r4 = lax.broadcasted_iota(jnp.int32, (nh * c, c), 0) % c
    c4 = lax.broadcasted_iota(jnp.int32, (nh * c, c), 1)
    causal4 = c4 <= r4
    gi = lax.broadcasted_iota(jnp.int32, (nh * dv, nh * dv), 0) // dv
    gj = lax.broadcasted_iota(jnp.int32, (nh * dv, nh * dv), 1) // dv
    gmean = jnp.where(gi == gj, 1.0 / dv, 0.0).astype(BF16)
    wlr_hi, wlr_lo = _split_bf16(wlr_ref[...], 2)

    def body(n, carry):
        r0 = pl.multiple_of(n * rb, rb)
        rows = pl.ds(r0, rb)
        q = p_ref[0, rows, 0:128].astype(F32) * (dk ** -0.5)
        k = p_ref[0, rows, 128:256].astype(F32)
        vb = p_ref[0, rows, 256:512]
        g = p_ref[0, rows, 512:768].astype(F32)
        lr = lr_ref[0, rows, :]
        lr_hi, lr_lo = _split_bf16(lr, 2)
        z = (jnp.dot(lr_hi, wlr_hi, preferred_element_type=F32)
             + jnp.dot(lr_hi, wlr_lo, preferred_element_type=F32)
             + jnp.dot(lr_lo, wlr_hi, preferred_element_type=F32)) + blr_ref[...]
        log_a = (jnp.minimum(z, 0.0) - jnp.log(1.0 + jnp.exp(-jnp.abs(z)))) * (1.0 / GLA_TAU)
        cumb = _dot_split_rhs(tri, log_a, 3)
        ends = [cumb[(j + 1) * c - 1:(j + 1) * c, :] for j in range(ncb)]
        starts = [jnp.zeros_like(ends[0])] + ends[:-1]
        cum = cumb - jnp.concatenate([jnp.broadcast_to(s0, (c, nh * dk)) for s0 in starts], axis=0)
        lasts = [e - s0 for e, s0 in zip(ends, starts)]
        cl = jnp.concatenate([jnp.broadcast_to(x, (c, nh * dk)) for x in lasts], axis=0)
        q_dec = q * jnp.exp(cum)
        k_inv = (k * jnp.exp(-cum)).astype(BF16)
        k_end = (k * jnp.exp(cl - cum)).astype(BF16)
        st = st_ref[...]
        outs = []
        for j in range(ncb):
            sl = slice(j * c, (j + 1) * c)
            qd = q_dec[sl]
            qs = jnp.concatenate([jnp.where(klane_head == h, qd, 0.0) for h in range(nh)],
                                 axis=0).astype(BF16)
            att = jnp.where(causal4, _dot_nt(qs, k_inv[sl]), 0.0)
            r = jnp.dot(att.astype(BF16), vb[sl], preferred_element_type=F32)
            o = jnp.where(vlane_head == 0, r[0:c], 0.0)
            for h in range(1, nh):
                o = o + jnp.where(vlane_head == h, r[h * c:(h + 1) * c], 0.0)
            outs.append(o + _dot_nt(qd.astype(BF16), st.astype(BF16)))
            d_st = _dot_tn(vb[sl], k_end[sl])
            st = st * jnp.exp(lasts[j]) + jnp.where(st_mask, d_st, 0.0)
        st_ref[...] = st
        o = jnp.concatenate(outs, axis=0)
        ms = _dot_split_lhs(o * o, gmean, 2)
        o = o * lax.rsqrt(ms + RMS_EPS) * ng_ref[...]
        o_ref[0, rows, :] = (o * (g * _sigmoid(g))).astype(o_ref.dtype)
        return carry

    lax.fori_loop(0, s_len // rb, body, 0)


def _gla_mixer(pg, plr, w_lr_pad, b_lr, norm_g4):
    b, s, wp = pg.shape
    return pl.pallas_call(
        _gla_kernel,
        grid=(b,),
        in_specs=[pl.BlockSpec((1, s, wp), lambda i: (i, 0, 0)),
                  pl.BlockSpec((1, s, LANES), lambda i: (i, 0, 0)),
                  pl.BlockSpec((LANES, LANES), lambda i: (0, 0)),
                  pl.BlockSpec((1, LANES), lambda i: (0, 0)),
                  pl.BlockSpec((1, W_MIX), lambda i: (0, 0))],
        out_specs=pl.BlockSpec((1, s, W_MIX), lambda i: (i, 0, 0)),
        out_shape=jax.ShapeDtypeStruct((b, s, W_MIX), BF16),
        scratch_shapes=[pltpu.VMEM((GLA_HEADS * GLA_DV, GLA_HEADS * GLA_DK), F32)],
        compiler_params=_cparams(("parallel",)),
        name="gla_mixer",
    )(pg, plr, w_lr_pad, b_lr, norm_g4)


def _ssd_kernel(p_ref, dt_ref, cw_ref, cb_ref, alog_ref, dtb_ref, dsk_ref, ng_ref, o_ref, st_ref):
    c = SSD_CHUNK
    s_len = p_ref.shape[1]
    n_st = SSD_STATE
    st_ref[...] = jnp.zeros_like(st_ref)

    ri = lax.broadcasted_iota(jnp.int32, (c, c), 0)
    ci = lax.broadcasted_iota(jnp.int32, (c, c), 1)
    causal = ci <= ri
    tri = causal.astype(BF16)
    upper = (ri <= ci).astype(BF16)
    lane_head = lax.broadcasted_iota(jnp.int32, (1, W_MIX), 1) // SSD_HEADDIM
    lane_group = lane_head // (SSD_HEADS // SSD_GROUPS)
    eh = lax.broadcasted_iota(jnp.int32, (LANES, W_MIX), 0)
    el = lax.broadcasted_iota(jnp.int32, (LANES, W_MIX), 1) // SSD_HEADDIM
    expand = (eh == el).astype(BF16)
    row8 = lax.broadcasted_iota(jnp.int32, (8, 3 * W_MIX), 0)
    a_c = -jnp.exp(alog_ref[...])

    def body(n, carry):
        r0 = pl.multiple_of(n * c, c)
        rows = pl.ds(r0, c)
        cur = p_ref[0, rows, 256:1024].astype(F32)
        p0 = pl.multiple_of(jnp.maximum(r0 - 2 * SUBLANES, 0), 2 * SUBLANES)
        prev8 = p_ref[0, pl.ds(p0, 2 * SUBLANES), 256:1024].astype(F32)[SUBLANES:]
        prev8 = jnp.where(n > 0, prev8, 0.0)
        acc = cur * cw_ref[3:4, :] + cb_ref[...]
        for s in (1, 2, 3):
            sh = pltpu.roll(cur, s, axis=0)
            top = jnp.where(row8 < s, pltpu.roll(prev8, s, axis=0), sh[0:8])
            sh = jnp.concatenate([top, sh[8:]], axis=0)
            acc = acc + sh * cw_ref[3 - s:4 - s, :]
        xbc = acc * _sigmoid(acc)
        x = xbc[:, 0:256]
        bm = xbc[:, 256:512].astype(BF16)
        cm = xbc[:, 512:768].astype(BF16)

        dt_c = _softplus(dt_ref[0, rows, :] + dtb_ref[...])
        da_c = dt_c * a_c
        cum_c = _dot_split_rhs(tri, da_c, 3)
        cum_r = _dot_split_lhs(da_c, upper, 3, dot=_dot_tn)
        both_x = _dot_split_lhs(jnp.concatenate([dt_c, cum_c], axis=0), expand, 3)
        dt_x = both_x[0:c]
        cum_x = both_x[c:2 * c]
        cl_x = cum_x[c - 1:c, :]
        x_dt = x * dt_x
        x_dt_b = x_dt.astype(BF16)
        xw_b = (x_dt * jnp.exp(cl_x - cum_x)).astype(BF16)

        y = x * dsk_ref[...]
        y_off = jnp.zeros((c, W_MIX), F32)
        for g in range(SSD_GROUPS):
            bg = bm[:, g * n_st:(g + 1) * n_st]
            cg = cm[:, g * n_st:(g + 1) * n_st]
            cb = _dot_nt(cg, bg)
            for r in range(SSD_HEADS // SSD_GROUPS):
                h = g * (SSD_HEADS // SSD_GROUPS) + r
                diff = cum_c[:, h:h + 1] - cum_r[h:h + 1, :]
                dec = jnp.exp(jnp.where(causal, diff, -jnp.inf))
                m = (cb * dec).astype(BF16)
                yh = jnp.dot(m, x_dt_b, preferred_element_type=F32)
                y = y + jnp.where(lane_head == h, yh, 0.0)
            st = st_ref[g]
            y_off = y_off + jnp.where(lane_group == g,
                                      jnp.dot(cg, st.astype(BF16), preferred_element_type=F32), 0.0)
            d_st = _dot_tn(bg, xw_b)
            st_ref[g] = st * jnp.exp(cl_x) + jnp.where(lane_group == g, d_st, 0.0)
        y = y + y_off * jnp.exp(cum_x)
        zg = p_ref[0, rows, 0:256].astype(F32)
        y = y * (zg * _sigmoid(zg))
        outs = []
        for g in range(SSD_GROUPS):
            yg = y[:, g * 128:(g + 1) * 128]
            ms = jnp.mean(yg * yg, axis=-1, keepdims=True)
            outs.append(yg * lax.rsqrt(ms + RMS_EPS))
        o_ref[0, rows, :] = (jnp.concatenate(outs, axis=-1) * ng_ref[...]).astype(o_ref.dtype)
        return carry

    def pair(u, carry):
        body(2 * u, carry)
        return body(2 * u + 1, carry)

    lax.fori_loop(0, s_len // (2 * c), pair, 0)


def _ssd_mixer(ps, pdt, conv_w, conv_b, a_log_c, dt_bias_c, d_x, norm_g):
    b, s, wp = ps.shape
    full2 = lambda i: (0, 0)
    return pl.pallas_call(
        _ssd_kernel,
        grid=(b,),
        in_specs=[pl.BlockSpec((1, s, wp), lambda i: (i, 0, 0)),
                  pl.BlockSpec((1, s, LANES), lambda i: (i, 0, 0)),
                  pl.BlockSpec((SSD_CONV_K, 3 * W_MIX), full2),
                  pl.BlockSpec((1, 3 * W_MIX), full2),
                  pl.BlockSpec((1, LANES), full2),
                  pl.BlockSpec((1, LANES), full2),
                  pl.BlockSpec((1, W_MIX), full2),
                  pl.BlockSpec((1, W_MIX), full2)],
        out_specs=pl.BlockSpec((1, s, W_MIX), lambda i: (i, 0, 0)),
        out_shape=jax.ShapeDtypeStruct((b, s, W_MIX), BF16),
        scratch_shapes=[pltpu.VMEM((SSD_GROUPS, SSD_STATE, W_MIX), F32)],
        compiler_params=_cparams(("parallel",)),
        name="ssd_mixer",
    )(ps, pdt, conv_w, conv_b, a_log_c, dt_bias_c, d_x, norm_g)


DIFF_TQ = 256
DIFF_TK = 256
LOG2E = 1.4426950408889634
DIFF_VPAD = DIFF_DV + 16


def _diff_kernel(q_ref, k_ref, v_ref, lam_ref, ng_ref, o_ref,
                 vt_ref, qs_ref, st_ref, m_ref, acc_ref, *, lam_init):
    tq, tk = DIFF_TQ, DIFF_TK
    nh, dv = DIFF_HEADS, DIFF_DV
    nhc = 2 * nh
    s_len = k_ref.shape[1]
    i = pl.program_id(1)

    @pl.when(i == 0)
    def _():
        for cblk in range(s_len // tk):
            cols = slice(cblk * tk, (cblk + 1) * tk)
            vt = v_ref[0, cols, :].astype(F32).T.astype(BF16)
            for h in range(nh):
                vt_ref[h, 0:dv, cols] = vt[h * dv:(h + 1) * dv]
        vt_ref[:, dv:, :] = jnp.ones((nh, DIFF_VPAD - dv, s_len), BF16)

    q = q_ref[0].astype(F32) * (DIFF_DQK ** -0.5 * LOG2E)
    qlane = lax.broadcasted_iota(jnp.int32, (1, W_MIX), 1) // DIFF_DQK
    for hc in range(nhc):
        qs_ref[hc * tq:(hc + 1) * tq, :] = jnp.where(qlane == hc, q, 0.0).astype(BF16)
    m_ref[...] = jnp.full_like(m_ref, -jnp.inf)
    acc_ref[...] = jnp.zeros_like(acc_ref)
    krow = lax.broadcasted_iota(jnp.int32, (tk, nhc * tq), 0)
    qcol = lax.broadcasted_iota(jnp.int32, (tk, nhc * tq), 1) % tq
    diag_ok = krow <= qcol

    def scores(j, slot):
        k0 = pl.multiple_of(j * tk, tk)
        st_ref[slot] = _dot_nt(k_ref[0, pl.ds(k0, tk), :], qs_ref[...])

    def softmax_pv(j, slot, masked):
        k0 = pl.multiple_of(j * tk, tk)
        st = st_ref[slot]
        if masked:
            st = jnp.where(diag_ok, st, -jnp.inf)
        m_prev = m_ref[...]
        m_new = jnp.maximum(m_prev, jnp.max(st, axis=0, keepdims=True))
        alpha = jnp.exp2(m_prev - m_new)
        p = jnp.exp2(st - m_new)
        m_ref[...] = m_new
        pb = p.astype(BF16)
        for hc in range(nhc):
            h = hc // 2
            lanes = slice(hc * tq, (hc + 1) * tq)
            pv = jnp.dot(vt_ref[h, :, pl.ds(k0, tk)], pb[:, lanes],
                         preferred_element_type=F32)
            acc_ref[hc] = acc_ref[hc] * alpha[:, lanes] + pv

    scores(0, 0)
    n_pairs = i // 2

    def pair_step(u, carry):
        scores(2 * u + 1, 1)
        softmax_pv(2 * u, 0, False)
        scores(2 * u + 2, 0)
        softmax_pv(2 * u + 1, 1, False)
        return carry

    lax.fori_loop(0, n_pairs, pair_step, 0)

    @pl.when(i % 2 == 0)
    def _():
        softmax_pv(i, 0, True)

    @pl.when(i % 2 == 1)
    def _():
        scores(i, 1)
        softmax_pv(i - 1, 0, False)
        softmax_pv(i, 1, True)

    lam = (jnp.exp(jnp.sum(lam_ref[0:1, :] * lam_ref[1:2, :], axis=-1, keepdims=True))
           - jnp.exp(jnp.sum(lam_ref[2:3, :] * lam_ref[3:4, :], axis=-1, keepdims=True))
           + lam_init)
    heads = []
    for h in range(nh):
        o1 = acc_ref[2 * h, 0:dv] / acc_ref[2 * h, dv:dv + 1]
        o2 = acc_ref[2 * h + 1, 0:dv] / acc_ref[2 * h + 1, dv:dv + 1]
        oh = o1 - lam * o2
        ms = jnp.mean(oh * oh, axis=0, keepdims=True)
        heads.append(oh * lax.rsqrt(ms + RMS_EPS))
    o = jnp.concatenate(heads, axis=0).T
    o_ref[0] = (o * ng_ref[...] * (1.0 - lam_init)).astype(o_ref.dtype)


def _diff_mixer(pd, lam_vecs, norm_g4, lam_init):
    b, s, _ = pd.shape
    tq = DIFF_TQ
    return pl.pallas_call(
        functools.partial(_diff_kernel, lam_init=lam_init),
        grid=(b, s // tq),
        in_specs=[pl.BlockSpec((1, tq, W_MIX), lambda bi, i: (bi, i, 0)),
                  pl.BlockSpec((1, s, W_MIX), lambda bi, i: (bi, 0, 1)),
                  pl.BlockSpec((1, s, W_MIX), lambda bi, i: (bi, 0, 2)),
                  pl.BlockSpec((4, LANES), lambda bi, i: (0, 0)),
                  pl.BlockSpec((1, W_MIX), lambda bi, i: (0, 0))],
        out_specs=pl.BlockSpec((1, tq, W_MIX), lambda bi, i: (bi, i, 0)),
        out_shape=jax.ShapeDtypeStruct((b, s, W_MIX), BF16),
        scratch_shapes=[pltpu.VMEM((DIFF_HEADS, DIFF_VPAD, s), BF16),
                        pltpu.VMEM((2 * DIFF_HEADS * tq, W_MIX), BF16),
                        pltpu.VMEM((2, DIFF_TK, 2 * DIFF_HEADS * tq), F32),
                        pltpu.VMEM((1, 2 * DIFF_HEADS * tq), F32),
                        pltpu.VMEM((2 * DIFF_HEADS, DIFF_VPAD, tq), F32)],
        compiler_params=_cparams(("parallel", "arbitrary")),
        name="diff_attn",
    )(pd, pd, pd, lam_vecs, norm_g4)


def _oproj_kernel(yc_ref, yg_ref, ys_ref, yd_ref, x_ref, wo_ref, g_ref, b_ref, wr_ref,
                  xo_ref, xp_ref, route_ref, cnt_ref):
    mix = jnp.concatenate([yc_ref[...], yg_ref[...], ys_ref[...], yd_ref[...]], axis=-1)
    h = ALPHA * x_ref[...] + jnp.dot(mix, wo_ref[...], preferred_element_type=F32)
    xn = _layer_norm(h, g_ref[...], b_ref[...])
    xo_ref[...] = xn
    xp_ref[...] = _pack_halves(xn)

    xn_hi, xn_lo = _split_bf16(xn, 2)
    both = _dot(xn_hi, wr_ref[...])
    logits = both[:, 0:LANES] + both[:, LANES:2 * LANES] + _dot(xn_lo, wr_ref[:, 0:LANES])
    lane = lax.broadcasted_iota(jnp.int32, logits.shape, 1).astype(F32)
    neg = -jnp.inf
    big = float(LANES)
    lg = jnp.where(lane < N_GROUPS, logits, neg)
    mg = jnp.max(lg, axis=-1, keepdims=True)
    sg = jnp.sum(jnp.exp(lg - mg), axis=-1, keepdims=True)
    grp = jnp.min(jnp.where(lg == mg, lane, big), axis=-1, keepdims=True)
    p_grp = 1.0 / sg
    lo = N_GROUPS + EXPERTS_PER_GROUP * grp
    in_g = jnp.logical_and(lane >= lo, lane < lo + EXPERTS_PER_GROUP)
    le = jnp.where(in_g, logits, neg)
    me = jnp.max(le, axis=-1, keepdims=True)
    ee = jnp.exp(le - me)
    pe = ee / jnp.sum(ee, axis=-1, keepdims=True)
    pe = jnp.where(in_g, pe, -1.0)
    p1 = jnp.max(pe, axis=-1, keepdims=True)
    i1 = jnp.min(jnp.where(pe == p1, lane, big), axis=-1, keepdims=True)
    pe2 = jnp.where(lane == i1, -1.0, pe)
    p2 = jnp.max(pe2, axis=-1, keepdims=True)
    i2 = jnp.min(jnp.where(pe2 == p2, lane, big), axis=-1, keepdims=True)
    den = p1 + p2
    g1 = p_grp * p1 / den
    g2 = p_grp * p2 / den
    e1 = i1 - N_GROUPS
    e2 = i2 - N_GROUPS
    route_ref[...] = jnp.where(lane == 0, e1, jnp.where(lane == 1, e2, jnp.where(
        lane == 2, g1, jnp.where(lane == 3, g2, 0.0))))

    @pl.when(pl.program_id(0) == 0)
    def _():
        cnt_ref[...] = jnp.zeros_like(cnt_ref)

    hits = jnp.where(lane == e1, 1.0, 0.0) + jnp.where(lane == e2, 1.0, 0.0)
    cnt_ref[...] += jnp.sum(hits, axis=0, keepdims=True)


def _out_proj(ys, x2d, w_o, layer, ln_g, ln_b, w_route):
    t = x2d.shape[0]
    tm = 512
    row = lambda i: (i, 0)
    full = lambda i: (0, 0)
    return pl.pallas_call(
        _oproj_kernel,
        grid=(t // tm,),
        in_specs=[pl.BlockSpec((tm, W_MIX), row)] * 4 + [
            pl.BlockSpec((tm, D_MODEL), row),
            pl.BlockSpec((None, D_MODEL, D_MODEL), lambda i: (layer, 0, 0)),
            pl.BlockSpec((1, D_MODEL), full),
            pl.BlockSpec((1, D_MODEL), full),
            pl.BlockSpec((D_MODEL, 2 * LANES), full)],
        out_specs=[pl.BlockSpec((tm, D_MODEL), row), pl.BlockSpec((tm, D_MODEL // 2), row),
                   pl.BlockSpec((tm, LANES), row), pl.BlockSpec((1, LANES), full)],
        out_shape=[jax.ShapeDtypeStruct((t, D_MODEL), F32),
                   jax.ShapeDtypeStruct((t, D_MODEL // 2), U32),
                   jax.ShapeDtypeStruct((t, LANES), F32),
                   jax.ShapeDtypeStruct((1, LANES), F32)],
        compiler_params=_cparams(("arbitrary",)),
        name="out_proj_ln_router",
    )(*ys, x2d, w_o, ln_g, ln_b, w_route)


PLAN_TILE = 512


def _plan_kernel(route_ref, cnt_ref, dest_ref, meta_ref, carry_ref, pstart_ref):
    tm = route_ref.shape[0]
    lane = lax.broadcasted_iota(jnp.int32, (1, LANES), 1).astype(F32)

    @pl.when(pl.program_id(0) == 0)
    def _():
        cnt = cnt_ref[...]
        padded = jnp.ceil(cnt * (1.0 / MOE_BLK)) * MOE_BLK
        li = lax.broadcasted_iota(jnp.int32, (LANES, LANES), 0)
        lj = lax.broadcasted_iota(jnp.int32, (LANES, LANES), 1)
        before = (li < lj).astype(F32)
        pstart = jnp.dot(jnp.broadcast_to(padded, (8, LANES)), before, precision=HI,
                         preferred_element_type=F32)[0:1]
        pstart_ref[...] = pstart
        carry_ref[...] = jnp.zeros_like(carry_ref)
        meta_ref[...] = jnp.concatenate(
            [pstart + padded, pstart, cnt, jnp.zeros((5, LANES), F32)], axis=0)

    oh0 = jnp.where(lane == route_ref[:, 0:1], 1.0, 0.0)
    oh1 = jnp.where(lane == route_ref[:, 1:2], 1.0, 0.0)
    both = oh0 + oh1
    ri = lax.broadcasted_iota(jnp.int32, (tm, tm), 0)
    ci = lax.broadcasted_iota(jnp.int32, (tm, tm), 1)
    earlier = (ci < ri).astype(BF16)
    base = (jnp.dot(earlier, both.astype(BF16), preferred_element_type=F32)
            + carry_ref[...] + pstart_ref[...])
    d0 = jnp.sum(oh0 * base, axis=-1, keepdims=True)
    d1 = jnp.sum(oh1 * base, axis=-1, keepdims=True)
    dest_ref[...] = jnp.where(lane == 0, d0, jnp.where(lane == 1, d1, 0.0))
    carry_ref[...] += jnp.sum(both, axis=0, keepdims=True)


def _dispatch_plan(route, cnt, n_blocks):
    t = route.shape[0]
    tm = PLAN_TILE
    blk = MOE_BLK
    dest, meta = pl.pallas_call(
        _plan_kernel,
        grid=(t // tm,),
        in_specs=[pl.BlockSpec((tm, LANES), lambda i: (i, 0)),
                  pl.BlockSpec((1, LANES), lambda i: (0, 0))],
        out_specs=[pl.BlockSpec((tm, LANES), lambda i: (i, 0)),
                   pl.BlockSpec((8, LANES), lambda i: (0, 0))],
        out_shape=[jax.ShapeDtypeStruct((t, LANES), F32),
                   jax.ShapeDtypeStruct((8, LANES), F32)],
        scratch_shapes=[pltpu.VMEM((1, LANES), F32), pltpu.VMEM((1, LANES), F32)],
        compiler_params=_cparams(("arbitrary",)),
        name="moe_plan",
    )(route, cnt)
    pad_end = meta[0, :N_EXPERTS].astype(jnp.int32)
    starts = jnp.arange(n_blocks, dtype=jnp.int32) * blk
    block_e = jnp.minimum(jnp.sum((pad_end[None, :] <= starts[:, None]).astype(jnp.int32), axis=1),
                          N_EXPERTS - 1)
    n_used = (pad_end[N_EXPERTS - 1] // blk).reshape(1)
    seg_end = (meta[1, :N_EXPERTS] + meta[2, :N_EXPERTS]).astype(jnp.int32)
    n_valid = jnp.clip(seg_end[block_e] - starts, 0, blk)
    idx = dest[:, 0:2].astype(jnp.int32).T.reshape(2 * t)
    return idx, block_e, n_used, n_valid


def _ffn_kernel(be_ref, nu_ref, nv_ref, xs_ref, wg_ref, wu_ref, wd_ref, y_ref,
                wgb_ref, wub_ref, wdb_ref):
    i = pl.program_id(0)
    prev = be_ref[jnp.maximum(i - 1, 0)]

    @pl.when(jnp.logical_or(i == 0, be_ref[i] != prev))
    def _():
        wgb_ref[...] = wg_ref[...].astype(BF16)
        wub_ref[...] = wu_ref[...].astype(BF16)
        wdb_ref[...] = wd_ref[...].astype(BF16)

    @pl.when(i < nu_ref[0])
    def _():
        half = xs_ref.shape[0] // 2
        for r in range(2):
            rows = slice(r * half, (r + 1) * half)
            row = lax.broadcasted_iota(jnp.int32, (half, 1), 0) + r * half
            xp = jnp.where(row < nv_ref[i], xs_ref[rows, :], U32(0))
            x_hi, x_lo = _unpack_halves(xp)
            xb = jnp.concatenate([x_hi.astype(BF16), x_lo.astype(BF16)], axis=1)
            a = jnp.dot(xb, wgb_ref[...], preferred_element_type=F32)
            u = jnp.dot(xb, wub_ref[...], preferred_element_type=F32)
            h = (a * _sigmoid(a) * u).astype(BF16)
            y_ref[rows, :] = _pack_halves(jnp.dot(h, wdb_ref[...], preferred_element_type=F32))

    @pl.when(i >= nu_ref[0])
    def _():
        y_ref[...] = jnp.zeros_like(y_ref)


def _expert_ffn(xs, block_e, n_used, n_valid, w_gate, w_up, w_down, layer):
    n_slots = xs.shape[0]
    blk = MOE_BLK
    w_map = lambda i, be, nu, nv: (layer, be[i], 0, 0)
    grid_spec = pltpu.PrefetchScalarGridSpec(
        num_scalar_prefetch=3,
        grid=(n_slots // blk,),
        in_specs=[pl.BlockSpec((blk, D_MODEL // 2),
                               lambda i, be, nu, nv: (jnp.minimum(i, nu[0] - 1), 0)),
                  pl.BlockSpec((None, None, D_MODEL, D_EXPERT), w_map),
                  pl.BlockSpec((None, None, D_MODEL, D_EXPERT), w_map),
                  pl.BlockSpec((None, None, D_EXPERT, D_MODEL), w_map)],
        out_specs=pl.BlockSpec((blk, D_MODEL // 2), lambda i, be, nu, nv: (i, 0)),
        scratch_shapes=[pltpu.VMEM((D_MODEL, D_EXPERT), BF16),
                        pltpu.VMEM((D_MODEL, D_EXPERT), BF16),
                        pltpu.VMEM((D_EXPERT, D_MODEL), BF16)],
    )
    return pl.pallas_call(
        _ffn_kernel,
        grid_spec=grid_spec,
        out_shape=jax.ShapeDtypeStruct((n_slots, D_MODEL // 2), U32),
        compiler_params=_cparams(("arbitrary",)),
        name="expert_ffn",
    )(block_e, n_used, n_valid, xs, w_gate, w_up, w_down)


SC_CORES = 2
SC_SUBCORES = 16
SC_ROWS = 64


def _sc_gather_rows(table, idx):
    b = idx.shape[0]
    d = table.shape[1]
    per_w = b // (SC_CORES * SC_SUBCORES)
    mesh = plsc.VectorSubcoreMesh(core_axis_name="c", subcore_axis_name="s")

    n_chunks = per_w // SC_ROWS

    @functools.partial(
        pl.kernel, mesh=mesh,
        out_type=jax.ShapeDtypeStruct((b, d), table.dtype),
        scratch_types=[pltpu.VMEM((SC_ROWS,), jnp.int32), pltpu.VMEM((SC_ROWS,), jnp.int32),
                       pltpu.VMEM((SC_ROWS, d), table.dtype),
                       pltpu.VMEM((SC_ROWS, d), table.dtype),
                       pltpu.SemaphoreType.DMA, pltpu.SemaphoreType.DMA,
                       pltpu.SemaphoreType.DMA, pltpu.SemaphoreType.DMA],
        name="sc_gather_rows",
    )
    def gather(table_hbm, idx_hbm, out_hbm, idx0, idx1, rows0, rows1, gs0, gs1, ws0, ws1):
        idx_v, rows_v, gsem, wsem = (idx0, idx1), (rows0, rows1), (gs0, gs1), (ws0, ws1)
        wid = lax.axis_index("s") * SC_CORES + lax.axis_index("c")
        base = wid * per_w

        def rows_of(c):
            return pl.ds(pl.multiple_of(base + c * SC_ROWS, SC_ROWS), SC_ROWS)

        def start_gather(c, s):
            pltpu.sync_copy(idx_hbm.at[rows_of(c)], idx_v[s])
            pltpu.async_copy(table_hbm.at[idx_v[s]], rows_v[s], gsem[s])

        def write_back(c, s):
            pltpu.make_async_copy(table_hbm.at[idx_v[s]], rows_v[s], gsem[s]).wait()
            pltpu.async_copy(rows_v[s], out_hbm.at[rows_of(c)], wsem[s]).wait()

        start_gather(0, 0)

        @pl.loop(0, n_chunks, step=2)
        def _(c):
            start_gather(c + 1, 1)
            write_back(c, 0)

            @pl.when(c + 2 < n_chunks)
            def _():
                start_gather(c + 2, 0)

            write_back(c + 1, 1)

    return gather(table, idx)


def _sc_scatter_rows(x2d, idx, n_slots):
    t, d = x2d.shape
    per_w = t // (SC_CORES * SC_SUBCORES)
    mesh = plsc.VectorSubcoreMesh(core_axis_name="c", subcore_axis_name="s")

    @functools.partial(
        pl.kernel, mesh=mesh,
        out_type=jax.ShapeDtypeStruct((n_slots, d), x2d.dtype),
        scratch_types=[pltpu.VMEM((SC_ROWS,), jnp.int32), pltpu.VMEM((SC_ROWS,), jnp.int32),
                       pltpu.VMEM((SC_ROWS, d), x2d.dtype),
                       pltpu.SemaphoreType.DMA, pltpu.SemaphoreType.DMA],
        name="sc_scatter_rows",
    )
    def scatter(x_hbm, idx_hbm, out_hbm, idx0, idx1, rows_v, s0, s1):
        wid = lax.axis_index("s") * SC_CORES + lax.axis_index("c")
        base = wid * per_w

        @pl.loop(0, per_w // SC_ROWS)
        def _(c):
            off = pl.multiple_of(base + c * SC_ROWS, SC_ROWS)
            pltpu.sync_copy(x_hbm.at[pl.ds(off, SC_ROWS)], rows_v)
            pltpu.sync_copy(idx_hbm.at[pl.ds(off, SC_ROWS)], idx0)
            pltpu.sync_copy(idx_hbm.at[pl.ds(t + off, SC_ROWS)], idx1)
            cp0 = pltpu.async_copy(rows_v, out_hbm.at[idx0], s0)
            cp1 = pltpu.async_copy(rows_v, out_hbm.at[idx1], s1)
            cp0.wait()
            cp1.wait()

    return scatter(x2d, idx)


def _combine_dense_kernel(route_ref, x_ref, y0_ref, y1_ref, g_ref, b_ref, o_ref):
    y0 = jnp.concatenate(_unpack_halves(y0_ref[...]), axis=1)
    y1 = jnp.concatenate(_unpack_halves(y1_ref[...]), axis=1)
    moe = route_ref[:, 2:3] * y0 + route_ref[:, 3:4] * y1
    h = ALPHA * x_ref[...] + moe
    o_ref[...] = _layer_norm(h, g_ref[...], b_ref[...])


def _combine_dense(route, x2d, yg, ln_g, ln_b):
    t = x2d.shape[0]
    tm = 512
    nt = t // tm
    row = lambda i: (i, 0)
    full = lambda i: (0, 0)
    return pl.pallas_call(
        _combine_dense_kernel,
        grid=(nt,),
        in_specs=[pl.BlockSpec((tm, LANES), row),
                  pl.BlockSpec((tm, D_MODEL), row),
                  pl.BlockSpec((tm, D_MODEL // 2), row),
                  pl.BlockSpec((tm, D_MODEL // 2), lambda i: (i + nt, 0)),
                  pl.BlockSpec((1, D_MODEL), full),
                  pl.BlockSpec((1, D_MODEL), full)],
        out_specs=pl.BlockSpec((tm, D_MODEL), row),
        out_shape=jax.ShapeDtypeStruct((t, D_MODEL), F32),
        compiler_params=_cparams(("parallel",)),
        name="moe_combine_dense",
    )(route, x2d, yg, yg, ln_g, ln_b)


def _pad_cols(w, n):
    return jnp.pad(w, [(0, 0)] * (w.ndim - 1) + [(0, n - w.shape[-1])])


def kernel(x, w_in, conv_w, gla_w_lr, gla_b_lr, gla_norm_g, ssd_conv_w, ssd_conv_b, ssd_a_log,
           ssd_d, ssd_dt_bias, ssd_norm_g, diff_lq1, diff_lk1, diff_lq2, diff_lk2, diff_norm_g,
           w_o, ln1_g, ln1_b, router_g, router_e, w_gate, w_up, w_down, ln2_g, ln2_b):
    bsz, seq, d = x.shape
    t = bsz * seq
    n_assign = 2 * t
    n_blocks = (n_assign + N_EXPERTS * (MOE_BLK - 1)) // MOE_BLK + 1
    n_slots = n_blocks * MOE_BLK
    x2d = x.reshape(t, d)
    w_in_r = jnp.concatenate([w_in[..., 0:768], _pad_cols(w_in[..., 768:1552], 896),
                              _pad_cols(w_in[..., 1552:2580], 1152), w_in[..., 2580:3348]],
                             axis=-1).astype(BF16)
    w_o_b = w_o.astype(BF16)
    for l in range(DEPTH):
        pc, pg, plr, ps, pdt, pd = [p.reshape(bsz, seq, -1) for p in _in_proj(x2d, w_in_r, l)]

        y_conv = _conv_mixer(pc, conv_w[l])
        w_lr_pad = jnp.pad(gla_w_lr[l], ((0, LANES - GLA_RANK), (0, 0)))
        y_gla = _gla_mixer(pg, plr, w_lr_pad, gla_b_lr[l].reshape(1, -1),
                           jnp.tile(gla_norm_g[l], GLA_HEADS).reshape(1, -1))
        pad4 = lambda v: jnp.pad(v, (0, LANES - SSD_HEADS)).reshape(1, LANES)
        y_ssd = _ssd_mixer(ps, pdt, ssd_conv_w[l], ssd_conv_b[l].reshape(1, -1),
                           pad4(ssd_a_log[l]), pad4(ssd_dt_bias[l]),
                           jnp.repeat(ssd_d[l], SSD_HEADDIM).reshape(1, -1),
                           ssd_norm_g[l].reshape(1, -1))
        lam_vecs = jnp.pad(jnp.stack([diff_lq1[l], diff_lk1[l], diff_lq2[l], diff_lk2[l]]),
                           ((0, 0), (0, LANES - DIFF_DQK)))
        lam_init = 0.8 - 0.6 * math.exp(-0.3 * l)
        y_diff = _diff_mixer(pd, lam_vecs,
                             jnp.tile(diff_norm_g[l], DIFF_HEADS).reshape(1, -1), lam_init)

        w_route = _pad_cols(jnp.concatenate(
            [router_g[l], router_e[l].reshape(d, N_EXPERTS)], axis=1), LANES)
        w_route_hi = w_route.astype(BF16)
        w_route = jnp.concatenate(
            [w_route_hi, (w_route - w_route_hi.astype(F32)).astype(BF16)], axis=1)
        ys = [y.reshape(t, W_MIX) for y in (y_conv, y_gla, y_ssd, y_diff)]
        xn, xn_p, route, cnt = _out_proj(ys, x2d, w_o_b, l, ln1_g[l].reshape(1, -1),
                                         ln1_b[l].reshape(1, -1), w_route)

        idx, block_e, n_used, n_valid = _dispatch_plan(route, cnt, n_blocks)
        xs = _sc_scatter_rows(xn_p, idx, n_slots)
        y = _expert_ffn(xs, block_e, n_used, n_valid, w_gate, w_up, w_down, l)
        yg = _sc_gather_rows(y, idx)
        x2d = _combine_dense(route, xn, yg, ln2_g[l].reshape(1, -1), ln2_b[l].reshape(1, -1))
    return x2d.reshape(bsz, seq, d)
```

```python
import functools
import math

import jax
import jax.numpy as jnp
from jax import lax
from jax.experimental import pallas as pl
from jax.experimental.pallas import tpu as pltpu
from jax.experimental.pallas import tpu_sc as plsc

F32 = jnp.float32
BF16 = jnp.bfloat16
HI = lax.Precision.HIGHEST

D_MODEL = 1024
DEPTH = 2
W_MIX = 256
GLA_HEADS, GLA_DK, GLA_DV, GLA_RANK, GLA_TAU, GLA_CHUNK = 4, 32, 64, 16, 16.0, 64
GLA_ROWS = 256
GLA_SEQS = 2
SSD_SEQS = 2
SSD_HEADS, SSD_GROUPS, SSD_HEADDIM, SSD_STATE, SSD_CONV_K, SSD_CHUNK = 4, 2, 64, 128, 4, 128
DIFF_HEADS, DIFF_DQK, DIFF_DV = 4, 32, 64
N_GROUPS, EXPERTS_PER_GROUP, N_EXPERTS, D_EXPERT = 4, 8, 32, 512
ALPHA = (2 * DEPTH) ** 0.25
LN_EPS = 1e-5
RMS_EPS = 1e-6

LANES = 128
SUBLANES = 8
PROJ_WIDTHS = (768, 768, 128, 1024, 128, 768)
PROJ_DTYPES = (BF16, BF16, F32, BF16, F32, BF16)
VMEM_LIMIT = 56 * 1024 * 1024

MOE_BLK = 512
TOK_TILE = 256


def _cparams(sem):
    return pltpu.CompilerParams(dimension_semantics=sem, vmem_limit_bytes=VMEM_LIMIT)


def _sigmoid(x):
    return 1.0 / (1.0 + jnp.exp(-x))


def _softplus(x):
    return jnp.maximum(x, 0.0) + jnp.log(1.0 + jnp.exp(-jnp.abs(x)))


def _layer_norm(h, g, b):
    mu = jnp.mean(h, axis=-1, keepdims=True)
    d = h - mu
    var = jnp.mean(d * d, axis=-1, keepdims=True)
    return d * lax.rsqrt(var + LN_EPS) * g + b


def _dot_nt(a, b):
    return lax.dot_general(a, b, (((1,), (1,)), ((), ())), preferred_element_type=F32)


def _dot_tn(a, b, precision=None):
    return lax.dot_general(a, b, (((0,), (0,)), ((), ())), preferred_element_type=F32,
                           precision=precision)


def _split_bf16(x, parts):
    out = []
    for _ in range(parts - 1):
        hi = x.astype(BF16)
        out.append(hi)
        x = x - hi.astype(F32)
    out.append(x.astype(BF16))
    return out


def _dot(a, b):
    return jnp.dot(a, b, preferred_element_type=F32)


U32 = jnp.uint32


def _pack_halves(x):
    w = x.shape[1] // 2
    hi = lax.bitcast_convert_type(x[:, :w].astype(BF16).astype(F32), U32)
    lo = lax.bitcast_convert_type(x[:, w:].astype(BF16).astype(F32), U32)
    return hi | lax.shift_right_logical(lo, U32(16))


def _unpack_halves(p):
    hi = lax.bitcast_convert_type(p & U32(0xFFFF0000), F32)
    lo = lax.bitcast_convert_type(lax.shift_left(p, U32(16)), F32)
    return hi, lo


def _dot_split_lhs(a, b_exact, parts, dot=_dot):
    acc = None
    for term in _split_bf16(a, parts):
        d = dot(term, b_exact)
        acc = d if acc is None else acc + d
    return acc


def _dot_split_rhs(a_exact, b, parts):
    acc = None
    for term in _split_bf16(b, parts):
        d = jnp.dot(a_exact, term, preferred_element_type=F32)
        acc = d if acc is None else acc + d
    return acc


def _proj_kernel(x_ref, w_ref, *o_refs):
    xb = x_ref[...].astype(BF16)
    off = 0
    for o_ref in o_refs:
        n = o_ref.shape[-1]
        o_ref[...] = jnp.dot(xb, w_ref[:, off:off + n],
                             preferred_element_type=F32).astype(o_ref.dtype)
        off += n


def _in_proj(x2d, w_r, layer):
    t = x2d.shape[0]
    tm = 512
    ncol = sum(PROJ_WIDTHS)
    return pl.pallas_call(
        _proj_kernel,
        grid=(t // tm,),
        in_specs=[pl.BlockSpec((tm, D_MODEL), lambda i: (i, 0)),
                  pl.BlockSpec((None, D_MODEL, ncol), lambda i: (layer, 0, 0))],
        out_specs=[pl.BlockSpec((tm, n), lambda i: (i, 0)) for n in PROJ_WIDTHS],
        out_shape=[jax.ShapeDtypeStruct((t, n), dt) for n, dt in zip(PROJ_WIDTHS, PROJ_DTYPES)],
        compiler_params=_cparams(("parallel",)),
        name="in_proj",
    )(x2d, w_r)


def _conv_kernel(p_ref, w_ref, o_ref):
    u = p_ref[0, :, 0:W_MIX].astype(F32)
    gb = p_ref[0, :, W_MIX:2 * W_MIX].astype(F32)
    gc = p_ref[0, :, 2 * W_MIX:3 * W_MIX].astype(F32)
    cu = gc * u
    row = lax.broadcasted_iota(jnp.int32, cu.shape, 0)
    acc = cu * w_ref[2:3, :]
    for s in (1, 2):
        sh = jnp.where(row >= s, pltpu.roll(cu, s, axis=0), 0.0)
        acc = acc + sh * w_ref[2 - s:3 - s, :]
    o_ref[0] = (gb * acc).astype(o_ref.dtype)


def _conv_mixer(pc, conv_w):
    b, s, _ = pc.shape
    return pl.pallas_call(
        _conv_kernel,
        grid=(b,),
        in_specs=[pl.BlockSpec((1, s, 3 * W_MIX), lambda i: (i, 0, 0)),
                  pl.BlockSpec((3, W_MIX), lambda i: (0, 0))],
        out_specs=pl.BlockSpec((1, s, W_MIX), lambda i: (i, 0, 0)),
        out_shape=jax.ShapeDtypeStruct((b, s, W_MIX), BF16),
        compiler_params=_cparams(("parallel",)),
        name="conv_mixer",
    )(pc, conv_w)


def _gla_kernel(p_ref, lr_ref, wlr_ref, blr_ref, ng_ref, o_ref, st_ref):
    c = GLA_CHUNK
    s_len = p_ref.shape[1]
    nh, dk, dv = GLA_HEADS, GLA_DK, GLA_DV
    st_ref[...] = jnp.zeros_like(st_ref)

    rb = GLA_ROWS
    ncb = rb // c
    ri = lax.broadcasted_iota(jnp.int32, (rb, rb), 0)
    ci = lax.broadcasted_iota(jnp.int32, (rb, rb), 1)
    tri = (ci <= ri).astype(BF16)
    klane_head = lax.broadcasted_iota(jnp.int32, (1, nh * dk), 1) // dk
    vlane_head = lax.broadcasted_iota(jnp.int32, (1, nh * dv), 1) // dv
    strow_head = lax.broadcasted_iota(jnp.int32, (nh * dv, 1), 0) // dv
    st_mask = strow_head == klane_head
    r4 = lax.broadcasted_iota(jnp.int32, (nh * c, c), 0) % c
    c4 = lax.broadcasted_iota(jnp.int32, (nh * c, c), 1)
    causal4 = c4 <= r4
    gi = lax.broadcasted_iota(jnp.int32, (nh * dv, nh * dv), 0) // dv
    gj = lax.broadcasted_iota(jnp.int32, (nh * dv, nh * dv), 1) // dv
    gmean = jnp.where(gi == gj, 1.0 / dv, 0.0).astype(BF16)
    wlr_hi, wlr_lo = _split_bf16(wlr_ref[...], 2)

    def one_seq(bb, rows):
        q = p_ref[bb, rows, 0:128].astype(F32) * (dk ** -0.5)
        k = p_ref[bb, rows, 128:256].astype(F32)
        vb = p_ref[bb, rows, 256:512]
        g = p_ref[bb, rows, 512:768].astype(F32)
        lr = lr_ref[bb, rows, :]
        lr_hi, lr_lo = _split_bf16(lr, 2)
        z = (jnp.dot(lr_hi, wlr_hi, preferred_element_type=F32)
             + jnp.dot(lr_hi, wlr_lo, preferred_element_type=F32)
             + jnp.dot(lr_lo, wlr_hi, preferred_element_type=F32)) + blr_ref[...]
        log_a = (jnp.minimum(z, 0.0) - jnp.log(1.0 + jnp.exp(-jnp.abs(z)))) * (1.0 / GLA_TAU)
        cumb = _dot_split_rhs(tri, log_a, 3)
        ends = [cumb[(j + 1) * c - 1:(j + 1) * c, :] for j in range(ncb)]
        starts = [jnp.zeros_like(ends[0])] + ends[:-1]
        cum = cumb - jnp.concatenate([jnp.broadcast_to(s0, (c, nh * dk)) for s0 in starts], axis=0)
        lasts = [e - s0 for e, s0 in zip(ends, starts)]
        cl = jnp.concatenate([jnp.broadcast_to(x, (c, nh * dk)) for x in lasts], axis=0)
        q_dec = q * jnp.exp(cum)
        k_inv = (k * jnp.exp(-cum)).astype(BF16)
        k_end = (k * jnp.exp(cl - cum)).astype(BF16)
        st = st_ref[bb]
        outs = []
        for j in range(ncb):
            sl = slice(j * c, (j + 1) * c)
            qd = q_dec[sl]
            qs = jnp.concatenate([jnp.where(klane_head == h, qd, 0.0) for h in range(nh)],
                                 axis=0).astype(BF16)
            att = jnp.where(causal4, _dot_nt(qs, k_inv[sl]), 0.0)
            r = jnp.dot(att.astype(BF16), vb[sl], preferred_element_type=F32)
            o = jnp.where(vlane_head == 0, r[0:c], 0.0)
            for h in range(1, nh):
                o = o + jnp.where(vlane_head == h, r[h * c:(h + 1) * c], 0.0)
            outs.append(o + _dot_nt(qd.astype(BF16), st.astype(BF16)))
            d_st = _dot_tn(vb[sl], k_end[sl])
            st = st * jnp.exp(lasts[j]) + jnp.where(st_mask, d_st, 0.0)
        st_ref[bb] = st
        o = jnp.concatenate(outs, axis=0)
        ms = _dot_split_lhs(o * o, gmean, 2)
        o = o * lax.rsqrt(ms + RMS_EPS) * ng_ref[...]
        o_ref[bb, rows, :] = (o * (g * _sigmoid(g))).astype(o_ref.dtype)

    def body(n, carry):
        rows = pl.ds(pl.multiple_of(n * rb, rb), rb)
        for bb in range(p_ref.shape[0]):
            one_seq(bb, rows)
        return carry

    lax.fori_loop(0, s_len // rb, body, 0)


def _gla_mixer(pg, plr, w_lr_pad, b_lr, norm_g4):
    b, s, wp = pg.shape
    nb = GLA_SEQS
    return pl.pallas_call(
        _gla_kernel,
        grid=(b // nb,),
        in_specs=[pl.BlockSpec((nb, s, wp), lambda i: (i, 0, 0)),
                  pl.BlockSpec((nb, s, LANES), lambda i: (i, 0, 0)),
                  pl.BlockSpec((LANES, LANES), lambda i: (0, 0)),
                  pl.BlockSpec((1, LANES), lambda i: (0, 0)),
                  pl.BlockSpec((1, W_MIX), lambda i: (0, 0))],
        out_specs=pl.BlockSpec((nb, s, W_MIX), lambda i: (i, 0, 0)),
        out_shape=jax.ShapeDtypeStruct((b, s, W_MIX), BF16),
        scratch_shapes=[pltpu.VMEM((nb, GLA_HEADS * GLA_DV, GLA_HEADS * GLA_DK), F32)],
        compiler_params=_cparams(("parallel",)),
        name="gla_mixer",
    )(pg, plr, w_lr_pad, b_lr, norm_g4)


def _ssd_kernel(p_ref, dt_ref, cw_ref, cb_ref, alog_ref, dtb_ref, dsk_ref, ng_ref, o_ref, st_ref):
    c = SSD_CHUNK
    s_len = p_ref.shape[1]
    n_st = SSD_STATE
    st_ref[...] = jnp.zeros_like(st_ref)

    ri = lax.broadcasted_iota(jnp.int32, (c, c), 0)
    ci = lax.broadcasted_iota(jnp.int32, (c, c), 1)
    causal = ci <= ri
    tri = causal.astype(BF16)
    upper = (ri <= ci).astype(BF16)
    lane_head = lax.broadcasted_iota(jnp.int32, (1, W_MIX), 1) // SSD_HEADDIM
    lane_group = lane_head // (SSD_HEADS // SSD_GROUPS)
    eh = lax.broadcasted_iota(jnp.int32, (LANES, W_MIX), 0)
    el = lax.broadcasted_iota(jnp.int32, (LANES, W_MIX), 1) // SSD_HEADDIM
    expand = (eh == el).astype(BF16)
    row8 = lax.broadcasted_iota(jnp.int32, (8, 3 * W_MIX), 0)
    a_c = -jnp.exp(alog_ref[...])

    def one_chunk(n, bb):
        r0 = pl.multiple_of(n * c, c)
        rows = pl.ds(r0, c)
        cur = p_ref[bb, rows, 256:1024].astype(F32)
        p0 = pl.multiple_of(jnp.maximum(r0 - 2 * SUBLANES, 0), 2 * SUBLANES)
        prev8 = p_ref[bb, pl.ds(p0, 2 * SUBLANES), 256:1024].astype(F32)[SUBLANES:]
        prev8 = jnp.where(n > 0, prev8, 0.0)
        acc = cur * cw_ref[3:4, :] + cb_ref[...]
        for s in (1, 2, 3):
            sh = pltpu.roll(cur, s, axis=0)
            top = jnp.where(row8 < s, pltpu.roll(prev8, s, axis=0), sh[0:8])
            sh = jnp.concatenate([top, sh[8:]], axis=0)
            acc = acc + sh * cw_ref[3 - s:4 - s, :]
        xbc = acc * _sigmoid(acc)
        x = xbc[:, 0:256]
        bm = xbc[:, 256:512].astype(BF16)
        cm = xbc[:, 512:768].astype(BF16)

        dt_c = _softplus(dt_ref[bb, rows, :] + dtb_ref[...])
        da_c = dt_c * a_c
        cum_c = _dot_split_rhs(tri, da_c, 3)
        cum_r = _dot_split_lhs(da_c, upper, 3, dot=_dot_tn)
        both_x = _dot_split_lhs(jnp.concatenate([dt_c, cum_c], axis=0), expand, 3)
        dt_x = both_x[0:c]
        cum_x = both_x[c:2 * c]
        cl_x = cum_x[c - 1:c, :]
        x_dt = x * dt_x
        x_dt_b = x_dt.astype(BF16)
        xw_b = (x_dt * jnp.exp(cl_x - cum_x)).astype(BF16)

        y = x * dsk_ref[...]
        y_off = jnp.zeros((c, W_MIX), F32)
        for g in range(SSD_GROUPS):
            bg = bm[:, g * n_st:(g + 1) * n_st]
            cg = cm[:, g * n_st:(g + 1) * n_st]
            cb = _dot_nt(cg, bg)
            for r in range(SSD_HEADS // SSD_GROUPS):
                h = g * (SSD_HEADS // SSD_GROUPS) + r
                diff = cum_c[:, h:h + 1] - cum_r[h:h + 1, :]
                dec = jnp.exp(jnp.where(causal, diff, -jnp.inf))
                m = (cb * dec).astype(BF16)
                yh = jnp.dot(m, x_dt_b, preferred_element_type=F32)
                y = y + jnp.where(lane_head == h, yh, 0.0)
            st = st_ref[bb, g]
            y_off = y_off + jnp.where(lane_group == g,
                                      jnp.dot(cg, st.astype(BF16), preferred_element_type=F32), 0.0)
            d_st = _dot_tn(bg, xw_b)
            st_ref[bb, g] = st * jnp.exp(cl_x) + jnp.where(lane_group == g, d_st, 0.0)
        y = y + y_off * jnp.exp(cum_x)
        zg = p_ref[bb, rows, 0:256].astype(F32)
        y = y * (zg * _sigmoid(zg))
        outs = []
        for g in range(SSD_GROUPS):
            yg = y[:, g * 128:(g + 1) * 128]
            ms = jnp.mean(yg * yg, axis=-1, keepdims=True)
            outs.append(yg * lax.rsqrt(ms + RMS_EPS))
        o_ref[bb, rows, :] = (jnp.concatenate(outs, axis=-1) * ng_ref[...]).astype(o_ref.dtype)

    def body(n, carry):
        for bb in range(p_ref.shape[0]):
            one_chunk(n, bb)
        return carry

    lax.fori_loop(0, s_len // c, body, 0)


def _ssd_mixer(ps, pdt, conv_w, conv_b, a_log_c, dt_bias_c, d_x, norm_g):
    b, s, wp = ps.shape
    nb = SSD_SEQS
    full2 = lambda i: (0, 0)
    return pl.pallas_call(
        _ssd_kernel,
        grid=(b // nb,),
        in_specs=[pl.BlockSpec((nb, s, wp), lambda i: (i, 0, 0)),
                  pl.BlockSpec((nb, s, LANES), lambda i: (i, 0, 0)),
                  pl.BlockSpec((SSD_CONV_K, 3 * W_MIX), full2),
                  pl.BlockSpec((1, 3 * W_MIX), full2),
                  pl.BlockSpec((1, LANES), full2),
                  pl.BlockSpec((1, LANES), full2),
                  pl.BlockSpec((1, W_MIX), full2),
                  pl.BlockSpec((1, W_MIX), full2)],
        out_specs=pl.BlockSpec((nb, s, W_MIX), lambda i: (i, 0, 0)),
        out_shape=jax.ShapeDtypeStruct((b, s, W_MIX), BF16),
        scratch_shapes=[pltpu.VMEM((nb, SSD_GROUPS, SSD_STATE, W_MIX), F32)],
        compiler_params=_cparams(("parallel",)),
        name="ssd_mixer",
    )(ps, pdt, conv_w, conv_b, a_log_c, dt_bias_c, d_x, norm_g)


DIFF_TQ = 256
DIFF_TK = 256
LOG2E = 1.4426950408889634
DIFF_VPAD = DIFF_DV + 16


def _diff_kernel(q_ref, k_ref, v_ref, lam_ref, ng_ref, o_ref,
                 vt_ref, qs_ref, st_ref, m_ref, acc_ref, *, lam_init):
    tq, tk = DIFF_TQ, DIFF_TK
    nh, dv = DIFF_HEADS, DIFF_DV
    nhc = 2 * nh
    s_len = k_ref.shape[1]
    i = pl.program_id(1)

    @pl.when(i == 0)
    def _():
        for cblk in range(s_len // tk):
            cols = slice(cblk * tk, (cblk + 1) * tk)
            vt = v_ref[0, cols, :].astype(F32).T.astype(BF16)
            for h in range(nh):
                vt_ref[h, 0:dv, cols] = vt[h * dv:(h + 1) * dv]
        vt_ref[:, dv:, :] = jnp.ones((nh, DIFF_VPAD - dv, s_len), BF16)

    q = q_ref[0].astype(F32) * (DIFF_DQK ** -0.5 * LOG2E)
    qlane = lax.broadcasted_iota(jnp.int32, (1, W_MIX), 1) // DIFF_DQK
    for hc in range(nhc):
        qs_ref[hc * tq:(hc + 1) * tq, :] = jnp.where(qlane == hc, q, 0.0).astype(BF16)
    m_ref[...] = jnp.full_like(m_ref, -jnp.inf)
    acc_ref[...] = jnp.zeros_like(acc_ref)
    krow = lax.broadcasted_iota(jnp.int32, (tk, nhc * tq), 0)
    qcol = lax.broadcasted_iota(jnp.int32, (tk, nhc * tq), 1) % tq
    diag_ok = krow <= qcol

    def scores(j, slot):
        k0 = pl.multiple_of(j * tk, tk)
        st_ref[slot] = _dot_nt(k_ref[0, pl.ds(k0, tk), :], qs_ref[...])

    def softmax_pv(j, slot, masked):
        k0 = pl.multiple_of(j * tk, tk)
        st = st_ref[slot]
        if masked:
            st = jnp.where(diag_ok, st, -jnp.inf)
        m_prev = m_ref[...]
        m_new = jnp.maximum(m_prev, jnp.max(st, axis=0, keepdims=True))
        alpha = jnp.exp2(m_prev - m_new)
        p = jnp.exp2(st - m_new)
        m_ref[...] = m_new
        pb = p.astype(BF16)
        for hc in range(nhc):
            h = hc // 2
            lanes = slice(hc * tq, (hc + 1) * tq)
            pv = jnp.dot(vt_ref[h, :, pl.ds(k0, tk)], pb[:, lanes],
                         preferred_element_type=F32)
            acc_ref[hc] = acc_ref[hc] * alpha[:, lanes] + pv

    scores(0, 0)
    n_pairs = i // 2

    def pair_step(u, carry):
        scores(2 * u + 1, 1)
        softmax_pv(2 * u, 0, False)
        scores(2 * u + 2, 0)
        softmax_pv(2 * u + 1, 1, False)
        return carry

    lax.fori_loop(0, n_pairs, pair_step, 0)

    @pl.when(i % 2 == 0)
    def _():
        softmax_pv(i, 0, True)

    @pl.when(i % 2 == 1)
    def _():
        scores(i, 1)
        softmax_pv(i - 1, 0, False)
        softmax_pv(i, 1, True)

    lam = (jnp.exp(jnp.sum(lam_ref[0:1, :] * lam_ref[1:2, :], axis=-1, keepdims=True))
           - jnp.exp(jnp.sum(lam_ref[2:3, :] * lam_ref[3:4, :], axis=-1, keepdims=True))
           + lam_init)
    heads = []
    for h in range(nh):
        o1 = acc_ref[2 * h, 0:dv] / acc_ref[2 * h, dv:dv + 1]
        o2 = acc_ref[2 * h + 1, 0:dv] / acc_ref[2 * h + 1, dv:dv + 1]
        oh = o1 - lam * o2
        ms = jnp.mean(oh * oh, axis=0, keepdims=True)
        heads.append(oh * lax.rsqrt(ms + RMS_EPS))
    o = jnp.concatenate(heads, axis=0).T
    o_ref[0] = (o * ng_ref[...] * (1.0 - lam_init)).astype(o_ref.dtype)


def _diff_mixer(pd, lam_vecs, norm_g4, lam_init):
    b, s, _ = pd.shape
    tq = DIFF_TQ
    return pl.pallas_call(
        functools.partial(_diff_kernel, lam_init=lam_init),
        grid=(b, s // tq),
        in_specs=[pl.BlockSpec((1, tq, W_MIX), lambda bi, i: (bi, i, 0)),
                  pl.BlockSpec((1, s, W_MIX), lambda bi, i: (bi, 0, 1)),
                  pl.BlockSpec((1, s, W_MIX), lambda bi, i: (bi, 0, 2)),
                  pl.BlockSpec((4, LANES), lambda bi, i: (0, 0)),
                  pl.BlockSpec((1, W_MIX), lambda bi, i: (0, 0))],
        out_specs=pl.BlockSpec((1, tq, W_MIX), lambda bi, i: (bi, i, 0)),
        out_shape=jax.ShapeDtypeStruct((b, s, W_MIX), BF16),
        scratch_shapes=[pltpu.VMEM((DIFF_HEADS, DIFF_VPAD, s), BF16),
                        pltpu.VMEM((2 * DIFF_HEADS * tq, W_MIX), BF16),
                        pltpu.VMEM((2, DIFF_TK, 2 * DIFF_HEADS * tq), F32),
                        pltpu.VMEM((1, 2 * DIFF_HEADS * tq), F32),
                        pltpu.VMEM((2 * DIFF_HEADS, DIFF_VPAD, tq), F32)],
        compiler_params=_cparams(("parallel", "arbitrary")),
        name="diff_attn",
    )(pd, pd, pd, lam_vecs, norm_g4)


def _oproj_kernel(yc_ref, yg_ref, ys_ref, yd_ref, x_ref, wo_ref, g_ref, b_ref, wr_ref,
                  xo_ref, xp_ref, route_ref, cnt_ref):
    mix = jnp.concatenate([yc_ref[...], yg_ref[...], ys_ref[...], yd_ref[...]], axis=-1)
    h = ALPHA * x_ref[...] + jnp.dot(mix, wo_ref[...], preferred_element_type=F32)
    xn = _layer_norm(h, g_ref[...], b_ref[...])
    xo_ref[...] = xn
    xp_ref[...] = _pack_halves(xn)

    xn_hi, xn_lo = _split_bf16(xn, 2)
    both = _dot(xn_hi, wr_ref[...])
    logits = both[:, 0:LANES] + both[:, LANES:2 * LANES] + _dot(xn_lo, wr_ref[:, 0:LANES])
    lane = lax.broadcasted_iota(jnp.int32, logits.shape, 1).astype(F32)
    neg = -jnp.inf
    big = float(LANES)
    lg = jnp.where(lane < N_GROUPS, logits, neg)
    mg = jnp.max(lg, axis=-1, keepdims=True)
    sg = jnp.sum(jnp.exp(lg - mg), axis=-1, keepdims=True)
    grp = jnp.min(jnp.where(lg == mg, lane, big), axis=-1, keepdims=True)
    p_grp = 1.0 / sg
    lo = N_GROUPS + EXPERTS_PER_GROUP * grp
    in_g = jnp.logical_and(lane >= lo, lane < lo + EXPERTS_PER_GROUP)
    le = jnp.where(in_g, logits, neg)
    me = jnp.max(le, axis=-1, keepdims=True)
    ee = jnp.exp(le - me)
    pe = ee / jnp.sum(ee, axis=-1, keepdims=True)
    pe = jnp.where(in_g, pe, -1.0)
    p1 = jnp.max(pe, axis=-1, keepdims=True)
    i1 = jnp.min(jnp.where(pe == p1, lane, big), axis=-1, keepdims=True)
    pe2 = jnp.where(lane == i1, -1.0, pe)
    p2 = jnp.max(pe2, axis=-1, keepdims=True)
    i2 = jnp.min(jnp.where(pe2 == p2, lane, big), axis=-1, keepdims=True)
    den = p1 + p2
    g1 = p_grp * p1 / den
    g2 = p_grp * p2 / den
    e1 = i1 - N_GROUPS
    e2 = i2 - N_GROUPS
    route_ref[...] = jnp.where(lane == 0, e1, jnp.where(lane == 1, e2, jnp.where(
        lane == 2, g1, jnp.where(lane == 3, g2, 0.0))))

    @pl.when(pl.program_id(0) == 0)
    def _():
        cnt_ref[...] = jnp.zeros_like(cnt_ref)

    hits = jnp.where(lane == e1, 1.0, 0.0) + jnp.where(lane == e2, 1.0, 0.0)
    cnt_ref[...] += jnp.sum(hits, axis=0, keepdims=True)


def _out_proj(ys, x2d, w_o, layer, ln_g, ln_b, w_route):
    t = x2d.shape[0]
    tm = 512
    row = lambda i: (i, 0)
    full = lambda i: (0, 0)
    return pl.pallas_call(
        _oproj_kernel,
        grid=(t // tm,),
        in_specs=[pl.BlockSpec((tm, W_MIX), row)] * 4 + [
            pl.BlockSpec((tm, D_MODEL), row),
            pl.BlockSpec((None, D_MODEL, D_MODEL), lambda i: (layer, 0, 0)),
            pl.BlockSpec((1, D_MODEL), full),
            pl.BlockSpec((1, D_MODEL), full),
            pl.BlockSpec((D_MODEL, 2 * LANES), full)],
        out_specs=[pl.BlockSpec((tm, D_MODEL), row), pl.BlockSpec((tm, D_MODEL // 2), row),
                   pl.BlockSpec((tm, LANES), row), pl.BlockSpec((1, LANES), full)],
        out_shape=[jax.ShapeDtypeStruct((t, D_MODEL), F32),
                   jax.ShapeDtypeStruct((t, D_MODEL // 2), U32),
                   jax.ShapeDtypeStruct((t, LANES), F32),
                   jax.ShapeDtypeStruct((1, LANES), F32)],
        compiler_params=_cparams(("arbitrary",)),
        name="out_proj_ln_router",
    )(*ys, x2d, w_o, ln_g, ln_b, w_route)


PLAN_TILE = 512


def _plan_kernel(route_ref, cnt_ref, dest_ref, meta_ref, carry_ref, pstart_ref):
    tm = route_ref.shape[0]
    lane = lax.broadcasted_iota(jnp.int32, (1, LANES), 1).astype(F32)

    @pl.when(pl.program_id(0) == 0)
    def _():
        cnt = cnt_ref[...]
        padded = jnp.ceil(cnt * (1.0 / MOE_BLK)) * MOE_BLK
        li = lax.broadcasted_iota(jnp.int32, (LANES, LANES), 0)
        lj = lax.broadcasted_iota(jnp.int32, (LANES, LANES), 1)
        before = (li < lj).astype(F32)
        pstart = jnp.dot(jnp.broadcast_to(padded, (8, LANES)), before, precision=HI,
                         preferred_element_type=F32)[0:1]
        pstart_ref[...] = pstart
        carry_ref[...] = jnp.zeros_like(carry_ref)
        meta_ref[...] = jnp.concatenate(
            [pstart + padded, pstart, cnt, jnp.zeros((5, LANES), F32)], axis=0)

    oh0 = jnp.where(lane == route_ref[:, 0:1], 1.0, 0.0)
    oh1 = jnp.where(lane == route_ref[:, 1:2], 1.0, 0.0)
    both = oh0 + oh1
    ri = lax.broadcasted_iota(jnp.int32, (tm, tm), 0)
    ci = lax.broadcasted_iota(jnp.int32, (tm, tm), 1)
    earlier = (ci < ri).astype(BF16)
    base = (jnp.dot(earlier, both.astype(BF16), preferred_element_type=F32)
            + carry_ref[...] + pstart_ref[...])
    d0 = jnp.sum(oh0 * base, axis=-1, keepdims=True)
    d1 = jnp.sum(oh1 * base, axis=-1, keepdims=True)
    dest = jnp.where(lane == 0, d0, jnp.where(lane == 1, d1, 0.0))
    dest_ref[...] = dest.T[0:2, :].astype(jnp.int32)
    carry_ref[...] += jnp.sum(both, axis=0, keepdims=True)


def _dispatch_plan(route, cnt, n_blocks):
    t = route.shape[0]
    tm = PLAN_TILE
    blk = MOE_BLK
    dest, meta = pl.pallas_call(
        _plan_kernel,
        grid=(t // tm,),
        in_specs=[pl.BlockSpec((tm, LANES), lambda i: (i, 0)),
                  pl.BlockSpec((1, LANES), lambda i: (0, 0))],
        out_specs=[pl.BlockSpec((2, tm), lambda i: (0, i)),
                   pl.BlockSpec((8, LANES), lambda i: (0, 0))],
        out_shape=[jax.ShapeDtypeStruct((2, t), jnp.int32),
                   jax.ShapeDtypeStruct((8, LANES), F32)],
        scratch_shapes=[pltpu.VMEM((1, LANES), F32), pltpu.VMEM((1, LANES), F32)],
        compiler_params=_cparams(("arbitrary",)),
        name="moe_plan",
    )(route, cnt)
    pad_end = meta[0, :N_EXPERTS].astype(jnp.int32)
    starts = jnp.arange(n_blocks, dtype=jnp.int32) * blk
    block_e = jnp.minimum(jnp.sum((pad_end[None, :] <= starts[:, None]).astype(jnp.int32), axis=1),
                          N_EXPERTS - 1)
    n_used = (pad_end[N_EXPERTS - 1] // blk).reshape(1)
    seg_end = (meta[1, :N_EXPERTS] + meta[2, :N_EXPERTS]).astype(jnp.int32)
    n_valid = jnp.clip(seg_end[block_e] - starts, 0, blk)
    return dest.reshape(2 * t), block_e, n_used, n_valid


def _ffn_kernel(be_ref, nu_ref, nv_ref, xs_ref, wg_ref, wu_ref, wd_ref, y_ref,
                wgb_ref, wub_ref, wdb_ref):
    i = pl.program_id(0)
    prev = be_ref[jnp.maximum(i - 1, 0)]

    @pl.when(jnp.logical_or(i == 0, be_ref[i] != prev))
    def _():
        wgb_ref[...] = wg_ref[...].astype(BF16)
        wub_ref[...] = wu_ref[...].astype(BF16)
        wdb_ref[...] = wd_ref[...].astype(BF16)

    @pl.when(i < nu_ref[0])
    def _():
        half = xs_ref.shape[0] // 2
        for r in range(2):
            rows = slice(r * half, (r + 1) * half)
            row = lax.broadcasted_iota(jnp.int32, (half, 1), 0) + r * half
            xp = jnp.where(row < nv_ref[i], xs_ref[rows, :], U32(0))
            x_hi, x_lo = _unpack_halves(xp)
            xb = jnp.concatenate([x_hi.astype(BF16), x_lo.astype(BF16)], axis=1)
            a = jnp.dot(xb, wgb_ref[...], preferred_element_type=F32)
            u = jnp.dot(xb, wub_ref[...], preferred_element_type=F32)
            h = (a * _sigmoid(a) * u).astype(BF16)
            y_ref[rows, :] = _pack_halves(jnp.dot(h, wdb_ref[...], preferred_element_type=F32))

    @pl.when(i >= nu_ref[0])
    def _():
        y_ref[...] = jnp.zeros_like(y_ref)


def _expert_ffn(xs, block_e, n_used, n_valid, w_gate, w_up, w_down, layer):
    n_slots = xs.shape[0]
    blk = MOE_BLK
    w_map = lambda i, be, nu, nv: (layer, be[i], 0, 0)
    grid_spec = pltpu.PrefetchScalarGridSpec(
        num_scalar_prefetch=3,
        grid=(n_slots // blk,),
        in_specs=[pl.BlockSpec((blk, D_MODEL // 2),
                               lambda i, be, nu, nv: (jnp.minimum(i, nu[0] - 1), 0)),
                  pl.BlockSpec((None, None, D_MODEL, D_EXPERT), w_map),
                  pl.BlockSpec((None, None, D_MODEL, D_EXPERT), w_map),
                  pl.BlockSpec((None, None, D_EXPERT, D_MODEL), w_map)],
        out_specs=pl.BlockSpec((blk, D_MODEL // 2), lambda i, be, nu, nv: (i, 0)),
        scratch_shapes=[pltpu.VMEM((D_MODEL, D_EXPERT), BF16),
                        pltpu.VMEM((D_MODEL, D_EXPERT), BF16),
                        pltpu.VMEM((D_EXPERT, D_MODEL), BF16)],
    )
    return pl.pallas_call(
        _ffn_kernel,
        grid_spec=grid_spec,
        out_shape=jax.ShapeDtypeStruct((n_slots, D_MODEL // 2), U32),
        compiler_params=_cparams(("arbitrary",)),
        name="expert_ffn",
    )(block_e, n_used, n_valid, xs, w_gate, w_up, w_down)


SC_CORES = 2
SC_SUBCORES = 16
SC_ROWS = 64


def _sc_gather_rows(table, idx):
    b = idx.shape[0]
    d = table.shape[1]
    per_w = b // (SC_CORES * SC_SUBCORES)
    mesh = plsc.VectorSubcoreMesh(core_axis_name="c", subcore_axis_name="s")

    n_chunks = per_w // SC_ROWS

    @functools.partial(
        pl.kernel, mesh=mesh,
        out_type=jax.ShapeDtypeStruct((b, d), table.dtype),
        scratch_types=[pltpu.VMEM((SC_ROWS,), jnp.int32), pltpu.VMEM((SC_ROWS,), jnp.int32),
                       pltpu.VMEM((SC_ROWS, d), table.dtype),
                       pltpu.VMEM((SC_ROWS, d), table.dtype),
                       pltpu.SemaphoreType.DMA, pltpu.SemaphoreType.DMA,
                       pltpu.SemaphoreType.DMA, pltpu.SemaphoreType.DMA],
        name="sc_gather_rows",
    )
    def gather(table_hbm, idx_hbm, out_hbm, idx0, idx1, rows0, rows1, gs0, gs1, ws0, ws1):
        idx_v, rows_v, gsem, wsem = (idx0, idx1), (rows0, rows1), (gs0, gs1), (ws0, ws1)
        wid = lax.axis_index("s") * SC_CORES + lax.axis_index("c")
        base = wid * per_w

        def rows_of(c):
            return pl.ds(pl.multiple_of(base + c * SC_ROWS, SC_ROWS), SC_ROWS)

        def start_gather(c, s):
            pltpu.sync_copy(idx_hbm.at[rows_of(c)], idx_v[s])
            pltpu.async_copy(table_hbm.at[idx_v[s]], rows_v[s], gsem[s])

        def write_back(c, s):
            pltpu.make_async_copy(table_hbm.at[idx_v[s]], rows_v[s], gsem[s]).wait()
            pltpu.async_copy(rows_v[s], out_hbm.at[rows_of(c)], wsem[s]).wait()

        start_gather(0, 0)

        @pl.loop(0, n_chunks, step=2)
        def _(c):
            start_gather(c + 1, 1)
            write_back(c, 0)

            @pl.when(c + 2 < n_chunks)
            def _():
                start_gather(c + 2, 0)

            write_back(c + 1, 1)

    return gather(table, idx)


def _sc_scatter_rows(x2d, idx, n_slots):
    t, d = x2d.shape
    per_w = t // (SC_CORES * SC_SUBCORES)
    mesh = plsc.VectorSubcoreMesh(core_axis_name="c", subcore_axis_name="s")

    @functools.partial(
        pl.kernel, mesh=mesh,
        out_type=jax.ShapeDtypeStruct((n_slots, d), x2d.dtype),
        scratch_types=[pltpu.VMEM((SC_ROWS,), jnp.int32), pltpu.VMEM((SC_ROWS,), jnp.int32),
                       pltpu.VMEM((SC_ROWS, d), x2d.dtype),
                       pltpu.SemaphoreType.DMA, pltpu.SemaphoreType.DMA],
        name="sc_scatter_rows",
    )
    def scatter(x_hbm, idx_hbm, out_hbm, idx0, idx1, rows_v, s0, s1):
        wid = lax.axis_index("s") * SC_CORES + lax.axis_index("c")
        base = wid * per_w

        @pl.loop(0, per_w // SC_ROWS)
        def _(c):
            off = pl.multiple_of(base + c * SC_ROWS, SC_ROWS)
            pltpu.sync_copy(x_hbm.at[pl.ds(off, SC_ROWS)], rows_v)
            pltpu.sync_copy(idx_hbm.at[pl.ds(off, SC_ROWS)], idx0)
            pltpu.sync_copy(idx_hbm.at[pl.ds(t + off, SC_ROWS)], idx1)
            cp0 = pltpu.async_copy(rows_v, out_hbm.at[idx0], s0)
            cp1 = pltpu.async_copy(rows_v, out_hbm.at[idx1], s1)
            cp0.wait()
            cp1.wait()

    return scatter(x2d, idx)


def _combine_dense_kernel(route_ref, x_ref, y0_ref, y1_ref, g_ref, b_ref, o_ref):
    y0 = jnp.concatenate(_unpack_halves(y0_ref[...]), axis=1)
    y1 = jnp.concatenate(_unpack_halves(y1_ref[...]), axis=1)
    moe = route_ref[:, 2:3] * y0 + route_ref[:, 3:4] * y1
    h = ALPHA * x_ref[...] + moe
    o_ref[...] = _layer_norm(h, g_ref[...], b_ref[...])


def _combine_dense(route, x2d, yg, ln_g, ln_b):
    t = x2d.shape[0]
    tm = 512
    nt = t // tm
    row = lambda i: (i, 0)
    full = lambda i: (0, 0)
    return pl.pallas_call(
        _combine_dense_kernel,
        grid=(nt,),
        in_specs=[pl.BlockSpec((tm, LANES), row),
                  pl.BlockSpec((tm, D_MODEL), row),
                  pl.BlockSpec((tm, D_MODEL // 2), row),
                  pl.BlockSpec((tm, D_MODEL // 2), lambda i: (i + nt, 0)),
                  pl.BlockSpec((1, D_MODEL), full),
                  pl.BlockSpec((1, D_MODEL), full)],
        out_specs=pl.BlockSpec((tm, D_MODEL), row),
        out_shape=jax.ShapeDtypeStruct((t, D_MODEL), F32),
        compiler_params=_cparams(("parallel",)),
        name="moe_combine_dense",
    )(route, x2d, yg, yg, ln_g, ln_b)


def _pad_cols(w, n):
    return jnp.pad(w, [(0, 0)] * (w.ndim - 1) + [(0, n - w.shape[-1])])


def kernel(x, w_in, conv_w, gla_w_lr, gla_b_lr, gla_norm_g, ssd_conv_w, ssd_conv_b, ssd_a_log,
           ssd_d, ssd_dt_bias, ssd_norm_g, diff_lq1, diff_lk1, diff_lq2, diff_lk2, diff_norm_g,
           w_o, ln1_g, ln1_b, router_g, router_e, w_gate, w_up, w_down, ln2_g, ln2_b):
    bsz, seq, d = x.shape
    t = bsz * seq
    n_assign = 2 * t
    n_blocks = (n_assign + N_EXPERTS * (MOE_BLK - 1)) // MOE_BLK + 1
    n_slots = n_blocks * MOE_BLK
    x2d = x.reshape(t, d)
    w_in_r = jnp.concatenate([w_in[..., 0:768], _pad_cols(w_in[..., 768:1552], 896),
                              _pad_cols(w_in[..., 1552:2580], 1152), w_in[..., 2580:3348]],
                             axis=-1).astype(BF16)
    w_o_b = w_o.astype(BF16)
    for l in range(DEPTH):
        pc, pg, plr, ps, pdt, pd = [p.reshape(bsz, seq, -1) for p in _in_proj(x2d, w_in_r, l)]

        y_conv = _conv_mixer(pc, conv_w[l])
        w_lr_pad = jnp.pad(gla_w_lr[l], ((0, LANES - GLA_RANK), (0, 0)))
        y_gla = _gla_mixer(pg, plr, w_lr_pad, gla_b_lr[l].reshape(1, -1),
                           jnp.tile(gla_norm_g[l], GLA_HEADS).reshape(1, -1))
        pad4 = lambda v: jnp.pad(v, (0, LANES - SSD_HEADS)).reshape(1, LANES)
        y_ssd = _ssd_mixer(ps, pdt, ssd_conv_w[l], ssd_conv_b[l].reshape(1, -1),
                           pad4(ssd_a_log[l]), pad4(ssd_dt_bias[l]),
                           jnp.repeat(ssd_d[l], SSD_HEADDIM).reshape(1, -1),
                           ssd_norm_g[l].reshape(1, -1))
        lam_vecs = jnp.pad(jnp.stack([diff_lq1[l], diff_lk1[l], diff_lq2[l], diff_lk2[l]]),
                           ((0, 0), (0, LANES - DIFF_DQK)))
        lam_init = 0.8 - 0.6 * math.exp(-0.3 * l)
        y_diff = _diff_mixer(pd, lam_vecs,
                             jnp.tile(diff_norm_g[l], DIFF_HEADS).reshape(1, -1), lam_init)

        w_route = _pad_cols(jnp.concatenate(
            [router_g[l], router_e[l].reshape(d, N_EXPERTS)], axis=1), LANES)
        w_route_hi = w_route.astype(BF16)
        w_route = jnp.concatenate(
            [w_route_hi, (w_route - w_route_hi.astype(F32)).astype(BF16)], axis=1)
        ys = [y.reshape(t, W_MIX) for y in (y_conv, y_gla, y_ssd, y_diff)]
        xn, xn_p, route, cnt = _out_proj(ys, x2d, w_o_b, l, ln1_g[l].reshape(1, -1),
                                         ln1_b[l].reshape(1, -1), w_route)

        idx, block_e, n_used, n_valid = _dispatch_plan(route, cnt, n_blocks)
        xs = _sc_scatter_rows(xn_p, idx, n_slots)
        y = _expert_ffn(xs, block_e, n_used, n_valid, w_gate, w_up, w_down, l)
        yg = _sc_gather_rows(y, idx)
        x2d = _combine_dense(route, xn, yg, ln2_g[l].reshape(1, -1), ln2_b[l].reshape(1, -1))
    return x2d.reshape(bsz, seq, d)
```

```python
import functools
import math

import jax
import jax.numpy as jnp
from jax import lax
from jax.experimental import pallas as pl
from jax.experimental.pallas import tpu as pltpu
from jax.experimental.pallas import tpu_sc as plsc

F32 = jnp.float32
BF16 = jnp.bfloat16
HI = lax.Precision.HIGHEST

D_MODEL = 1024
DEPTH = 2
W_MIX = 256
GLA_HEADS, GLA_DK, GLA_DV, GLA_RANK, GLA_TAU, GLA_CHUNK = 4, 32, 64, 16, 16.0, 64
GLA_ROWS = 256
GLA_SEQS = 2
SSD_SEQS = 2
SSD_HEADS, SSD_GROUPS, SSD_HEADDIM, SSD_STATE, SSD_CONV_K, SSD_CHUNK = 4, 2, 64, 128, 4, 128
DIFF_HEADS, DIFF_DQK, DIFF_DV = 4, 32, 64
N_GROUPS, EXPERTS_PER_GROUP, N_EXPERTS, D_EXPERT = 4, 8, 32, 512
ALPHA = (2 * DEPTH) ** 0.25
LN_EPS = 1e-5
RMS_EPS = 1e-6

LANES = 128
SUBLANES = 8
PROJ_WIDTHS = (768, 768, 128, 1024, 128, 768)
PROJ_DTYPES = (BF16, BF16, F32, BF16, F32, BF16)
VMEM_LIMIT = 56 * 1024 * 1024

MOE_BLK = 256
TOK_TILE = 256


def _cparams(sem):
    return pltpu.CompilerParams(dimension_semantics=sem, vmem_limit_bytes=VMEM_LIMIT)


def _sigmoid(x):
    return 1.0 / (1.0 + jnp.exp(-x))


def _softplus(x):
    return jnp.maximum(x, 0.0) + jnp.log(1.0 + jnp.exp(-jnp.abs(x)))


def _layer_norm(h, g, b):
    mu = jnp.mean(h, axis=-1, keepdims=True)
    d = h - mu
    var = jnp.mean(d * d, axis=-1, keepdims=True)
    return d * lax.rsqrt(var + LN_EPS) * g + b


def _dot_nt(a, b):
    return lax.dot_general(a, b, (((1,), (1,)), ((), ())), preferred_element_type=F32)


def _dot_tn(a, b, precision=None):
    return lax.dot_general(a, b, (((0,), (0,)), ((), ())), preferred_element_type=F32,
                           precision=precision)


def _split_bf16(x, parts):
    out = []
    for _ in range(parts - 1):
        hi = x.astype(BF16)
        out.append(hi)
        x = x - hi.astype(F32)
    out.append(x.astype(BF16))
    return out


def _dot(a, b):
    return jnp.dot(a, b, preferred_element_type=F32)


U32 = jnp.uint32


def _pack_halves(x):
    w = x.shape[1] // 2
    hi = lax.bitcast_convert_type(x[:, :w].astype(BF16).astype(F32), U32)
    lo = lax.bitcast_convert_type(x[:, w:].astype(BF16).astype(F32), U32)
    return hi | lax.shift_right_logical(lo, U32(16))


def _unpack_halves(p):
    hi = lax.bitcast_convert_type(p & U32(0xFFFF0000), F32)
    lo = lax.bitcast_convert_type(lax.shift_left(p, U32(16)), F32)
    return hi, lo


def _dot_split_lhs(a, b_exact, parts, dot=_dot):
    acc = None
    for term in _split_bf16(a, parts):
        d = dot(term, b_exact)
        acc = d if acc is None else acc + d
    return acc


def _dot_split_rhs(a_exact, b, parts):
    acc = None
    for term in _split_bf16(b, parts):
        d = jnp.dot(a_exact, term, preferred_element_type=F32)
        acc = d if acc is None else acc + d
    return acc


def _proj_kernel(x_ref, w_ref, *o_refs):
    xb = x_ref[...].astype(BF16)
    off = 0
    for o_ref in o_refs:
        n = o_ref.shape[-1]
        o_ref[...] = jnp.dot(xb, w_ref[:, off:off + n],
                             preferred_element_type=F32).astype(o_ref.dtype)
        off += n


def _in_proj(x2d, w_r, layer):
    t = x2d.shape[0]
    tm = 512
    ncol = sum(PROJ_WIDTHS)
    return pl.pallas_call(
        _proj_kernel,
        grid=(t // tm,),
        in_specs=[pl.BlockSpec((tm, D_MODEL), lambda i: (i, 0)),
                  pl.BlockSpec((None, D_MODEL, ncol), lambda i: (layer, 0, 0))],
        out_specs=[pl.BlockSpec((tm, n), lambda i: (i, 0)) for n in PROJ_WIDTHS],
        out_shape=[jax.ShapeDtypeStruct((t, n), dt) for n, dt in zip(PROJ_WIDTHS, PROJ_DTYPES)],
        compiler_params=_cparams(("parallel",)),
        name="in_proj",
    )(x2d, w_r)


def _conv_kernel(p_ref, w_ref, o_ref):
    u = p_ref[0, :, 0:W_MIX].astype(F32)
    gb = p_ref[0, :, W_MIX:2 * W_MIX].astype(F32)
    gc = p_ref[0, :, 2 * W_MIX:3 * W_MIX].astype(F32)
    cu = gc * u
    row = lax.broadcasted_iota(jnp.int32, cu.shape, 0)
    acc = cu * w_ref[2:3, :]
    for s in (1, 2):
        sh = jnp.where(row >= s, pltpu.roll(cu, s, axis=0), 0.0)
        acc = acc + sh * w_ref[2 - s:3 - s, :]
    o_ref[0] = (gb * acc).astype(o_ref.dtype)


def _conv_mixer(pc, conv_w):
    b, s, _ = pc.shape
    return pl.pallas_call(
        _conv_kernel,
        grid=(b,),
        in_specs=[pl.BlockSpec((1, s, 3 * W_MIX), lambda i: (i, 0, 0)),
                  pl.BlockSpec((3, W_MIX), lambda i: (0, 0))],
        out_specs=pl.BlockSpec((1, s, W_MIX), lambda i: (i, 0, 0)),
        out_shape=jax.ShapeDtypeStruct((b, s, W_MIX), BF16),
        compiler_params=_cparams(("parallel",)),
        name="conv_mixer",
    )(pc, conv_w)


def _gla_kernel(p_ref, lr_ref, wlr_ref, blr_ref, ng_ref, o_ref, st_ref):
    c = GLA_CHUNK
    s_len = p_ref.shape[1]
    nh, dk, dv = GLA_HEADS, GLA_DK, GLA_DV
    st_ref[...] = jnp.zeros_like(st_ref)

    rb = GLA_ROWS
    ncb = rb // c
    ri = lax.broadcasted_iota(jnp.int32, (rb, rb), 0)
    ci = lax.broadcasted_iota(jnp.int32, (rb, rb), 1)
    tri = (ci <= ri).astype(BF16)
    klane_head = lax.broadcasted_iota(jnp.int32, (1, nh * dk), 1) // dk
    vlane_head = lax.broadcasted_iota(jnp.int32, (1, nh * dv), 1) // dv
    strow_head = lax.broadcasted_iota(jnp.int32, (nh * dv, 1), 0) // dv
    st_mask = strow_head == klane_head
    r4 = lax.broadcasted_iota(jnp.int32, (nh * c, c), 0) % c
    c4 = lax.broadcasted_iota(jnp.int32, (nh * c, c), 1)
    causal4 = c4 <= r4
    gi = lax.broadcasted_iota(jnp.int32, (nh * dv, nh * dv), 0) // dv
    gj = lax.broadcasted_iota(jnp.int32, (nh * dv, nh * dv), 1) // dv
    gmean = jnp.where(gi == gj, 1.0 / dv, 0.0).astype(BF16)
    wlr_hi, wlr_lo = _split_bf16(wlr_ref[...], 2)

    def one_seq(bb, rows):
        q = p_ref[bb, rows, 0:128].astype(F32) * (dk ** -0.5)
        k = p_ref[bb, rows, 128:256].astype(F32)
        vb = p_ref[bb, rows, 256:512]
        g = p_ref[bb, rows, 512:768].astype(F32)
        lr = lr_ref[bb, rows, :]
        lr_hi, lr_lo = _split_bf16(lr, 2)
        z = (jnp.dot(lr_hi, wlr_hi, preferred_element_type=F32)
             + jnp.dot(lr_hi, wlr_lo, preferred_element_type=F32)
             + jnp.dot(lr_lo, wlr_hi, preferred_element_type=F32)) + blr_ref[...]
        log_a = (jnp.minimum(z, 0.0) - jnp.log(1.0 + jnp.exp(-jnp.abs(z)))) * (1.0 / GLA_TAU)
        cumb = _dot_split_rhs(tri, log_a, 3)
        ends = [cumb[(j + 1) * c - 1:(j + 1) * c, :] for j in range(ncb)]
        starts = [jnp.zeros_like(ends[0])] + ends[:-1]
        cum = cumb - jnp.concatenate([jnp.broadcast_to(s0, (c, nh * dk)) for s0 in starts], axis=0)
        lasts = [e - s0 for e, s0 in zip(ends, starts)]
        cl = jnp.concatenate([jnp.broadcast_to(x, (c, nh * dk)) for x in lasts], axis=0)
        q_dec = q * jnp.exp(cum)
        k_inv = (k * jnp.exp(-cum)).astype(BF16)
        k_end = (k * jnp.exp(cl - cum)).astype(BF16)
        st = st_ref[bb]
        outs = []
        for j in range(ncb):
            sl = slice(j * c, (j + 1) * c)
            qd = q_dec[sl]
            qs = jnp.concatenate([jnp.where(klane_head == h, qd, 0.0) for h in range(nh)],
                                 axis=0).astype(BF16)
            att = jnp.where(causal4, _dot_nt(qs, k_inv[sl]), 0.0)
            r = jnp.dot(att.astype(BF16), vb[sl], preferred_element_type=F32)
            o = jnp.where(vlane_head == 0, r[0:c], 0.0)
            for h in range(1, nh):
                o = o + jnp.where(vlane_head == h, r[h * c:(h + 1) * c], 0.0)
            outs.append(o + _dot_nt(qd.astype(BF16), st.astype(BF16)))
            d_st = _dot_tn(vb[sl], k_end[sl])
            st = st * jnp.exp(lasts[j]) + jnp.where(st_mask, d_st, 0.0)
        st_ref[bb] = st
        o = jnp.concatenate(outs, axis=0)
        ms = _dot_split_lhs(o * o, gmean, 2)
        o = o * lax.rsqrt(ms + RMS_EPS) * ng_ref[...]
        o_ref[bb, rows, :] = (o * (g * _sigmoid(g))).astype(o_ref.dtype)

    def body(n, carry):
        rows = pl.ds(pl.multiple_of(n * rb, rb), rb)
        for bb in range(p_ref.shape[0]):
            one_seq(bb, rows)
        return carry

    lax.fori_loop(0, s_len // rb, body, 0)


def _gla_mixer(pg, plr, w_lr_pad, b_lr, norm_g4):
    b, s, wp = pg.shape
    nb = GLA_SEQS
    return pl.pallas_call(
        _gla_kernel,
        grid=(b // nb,),
        in_specs=[pl.BlockSpec((nb, s, wp), lambda i: (i, 0, 0)),
                  pl.BlockSpec((nb, s, LANES), lambda i: (i, 0, 0)),
                  pl.BlockSpec((LANES, LANES), lambda i: (0, 0)),
                  pl.BlockSpec((1, LANES), lambda i: (0, 0)),
                  pl.BlockSpec((1, W_MIX), lambda i: (0, 0))],
        out_specs=pl.BlockSpec((nb, s, W_MIX), lambda i: (i, 0, 0)),
        out_shape=jax.ShapeDtypeStruct((b, s, W_MIX), BF16),
        scratch_shapes=[pltpu.VMEM((nb, GLA_HEADS * GLA_DV, GLA_HEADS * GLA_DK), F32)],
        compiler_params=_cparams(("parallel",)),
        name="gla_mixer",
    )(pg, plr, w_lr_pad, b_lr, norm_g4)


def _ssd_kernel(p_ref, dt_ref, cw_ref, cb_ref, alog_ref, dtb_ref, dsk_ref, ng_ref, o_ref, st_ref):
    c = SSD_CHUNK
    s_len = p_ref.shape[1]
    n_st = SSD_STATE
    st_ref[...] = jnp.zeros_like(st_ref)

    ri = lax.broadcasted_iota(jnp.int32, (c, c), 0)
    ci = lax.broadcasted_iota(jnp.int32, (c, c), 1)
    causal = ci <= ri
    tri = causal.astype(BF16)
    upper = (ri <= ci).astype(BF16)
    lane_head = lax.broadcasted_iota(jnp.int32, (1, W_MIX), 1) // SSD_HEADDIM
    lane_group = lane_head // (SSD_HEADS // SSD_GROUPS)
    eh = lax.broadcasted_iota(jnp.int32, (LANES, W_MIX), 0)
    el = lax.broadcasted_iota(jnp.int32, (LANES, W_MIX), 1) // SSD_HEADDIM
    expand = (eh == el).astype(BF16)
    row8 = lax.broadcasted_iota(jnp.int32, (8, 3 * W_MIX), 0)
    a_c = -jnp.exp(alog_ref[...])

    def one_chunk(n, bb):
        r0 = pl.multiple_of(n * c, c)
        rows = pl.ds(r0, c)
        cur = p_ref[bb, rows, 256:1024].astype(F32)
        p0 = pl.multiple_of(jnp.maximum(r0 - 2 * SUBLANES, 0), 2 * SUBLANES)
        prev8 = p_ref[bb, pl.ds(p0, 2 * SUBLANES), 256:1024].astype(F32)[SUBLANES:]
        prev8 = jnp.where(n > 0, prev8, 0.0)
        acc = cur * cw_ref[3:4, :] + cb_ref[...]
        for s in (1, 2, 3):
            sh = pltpu.roll(cur, s, axis=0)
            top = jnp.where(row8 < s, pltpu.roll(prev8, s, axis=0), sh[0:8])
            sh = jnp.concatenate([top, sh[8:]], axis=0)
            acc = acc + sh * cw_ref[3 - s:4 - s, :]
        xbc = acc * _sigmoid(acc)
        x = xbc[:, 0:256]
        bm = xbc[:, 256:512].astype(BF16)
        cm = xbc[:, 512:768].astype(BF16)

        dt_c = _softplus(dt_ref[bb, rows, :] + dtb_ref[...])
        da_c = dt_c * a_c
        cum_c = _dot_split_rhs(tri, da_c, 3)
        cum_r = _dot_split_lhs(da_c, upper, 3, dot=_dot_tn)
        both_x = _dot_split_lhs(jnp.concatenate([dt_c, cum_c], axis=0), expand, 3)
        dt_x = both_x[0:c]
        cum_x = both_x[c:2 * c]
        cl_x = cum_x[c - 1:c, :]
        x_dt = x * dt_x
        x_dt_b = x_dt.astype(BF16)
        xw_b = (x_dt * jnp.exp(cl_x - cum_x)).astype(BF16)

        y = x * dsk_ref[...]
        y_off = jnp.zeros((c, W_MIX), F32)
        for g in range(SSD_GROUPS):
            bg = bm[:, g * n_st:(g + 1) * n_st]
            cg = cm[:, g * n_st:(g + 1) * n_st]
            cb = _dot_nt(cg, bg)
            for r in range(SSD_HEADS // SSD_GROUPS):
                h = g * (SSD_HEADS // SSD_GROUPS) + r
                diff = cum_c[:, h:h + 1] - cum_r[h:h + 1, :]
                dec = jnp.exp(jnp.where(causal, diff, -jnp.inf))
                m = (cb * dec).astype(BF16)
                yh = jnp.dot(m, x_dt_b, preferred_element_type=F32)
                y = y + jnp.where(lane_head == h, yh, 0.0)
            st = st_ref[bb, g]
            y_off = y_off + jnp.where(lane_group == g,
                                      jnp.dot(cg, st.astype(BF16), preferred_element_type=F32), 0.0)
            d_st = _dot_tn(bg, xw_b)
            st_ref[bb, g] = st * jnp.exp(cl_x) + jnp.where(lane_group == g, d_st, 0.0)
        y = y + y_off * jnp.exp(cum_x)
        zg = p_ref[bb, rows, 0:256].astype(F32)
        y = y * (zg * _sigmoid(zg))
        outs = []
        for g in range(SSD_GROUPS):
            yg = y[:, g * 128:(g + 1) * 128]
            ms = jnp.mean(yg * yg, axis=-1, keepdims=True)
            outs.append(yg * lax.rsqrt(ms + RMS_EPS))
        o_ref[bb, rows, :] = (jnp.concatenate(outs, axis=-1) * ng_ref[...]).astype(o_ref.dtype)

    def body(n, carry):
        for bb in range(p_ref.shape[0]):
            one_chunk(n, bb)
        return carry

    lax.fori_loop(0, s_len // c, body, 0)


def _ssd_mixer(ps, pdt, conv_w, conv_b, a_log_c, dt_bias_c, d_x, norm_g):
    b, s, wp = ps.shape
    nb = SSD_SEQS
    full2 = lambda i: (0, 0)
    return pl.pallas_call(
        _ssd_kernel,
        grid=(b // nb,),
        in_specs=[pl.BlockSpec((nb, s, wp), lambda i: (i, 0, 0)),
                  pl.BlockSpec((nb, s, LANES), lambda i: (i, 0, 0)),
                  pl.BlockSpec((SSD_CONV_K, 3 * W_MIX), full2),
                  pl.BlockSpec((1, 3 * W_MIX), full2),
                  pl.BlockSpec((1, LANES), full2),
                  pl.BlockSpec((1, LANES), full2),
                  pl.BlockSpec((1, W_MIX), full2),
                  pl.BlockSpec((1, W_MIX), full2)],
        out_specs=pl.BlockSpec((nb, s, W_MIX), lambda i: (i, 0, 0)),
        out_shape=jax.ShapeDtypeStruct((b, s, W_MIX), BF16),
        scratch_shapes=[pltpu.VMEM((nb, SSD_GROUPS, SSD_STATE, W_MIX), F32)],
        compiler_params=_cparams(("parallel",)),
        name="ssd_mixer",
    )(ps, pdt, conv_w, conv_b, a_log_c, dt_bias_c, d_x, norm_g)


DIFF_TQ = 256
DIFF_TK = 256
LOG2E = 1.4426950408889634
DIFF_VPAD = DIFF_DV + 16


def _diff_kernel(q_ref, k_ref, v_ref, lam_ref, ng_ref, o_ref,
                 vt_ref, qs_ref, st_ref, m_ref, acc_ref, *, lam_init):
    tq, tk = DIFF_TQ, DIFF_TK
    nh, dv = DIFF_HEADS, DIFF_DV
    nhc = 2 * nh
    s_len = k_ref.shape[1]
    i = pl.program_id(1)

    @pl.when(i == 0)
    def _():
        for cblk in range(s_len // tk):
            cols = slice(cblk * tk, (cblk + 1) * tk)
            vt = v_ref[0, cols, :].astype(F32).T.astype(BF16)
            for h in range(nh):
                vt_ref[h, 0:dv, cols] = vt[h * dv:(h + 1) * dv]
        vt_ref[:, dv:, :] = jnp.ones((nh, DIFF_VPAD - dv, s_len), BF16)

    q = q_ref[0].astype(F32) * (DIFF_DQK ** -0.5 * LOG2E)
    qlane = lax.broadcasted_iota(jnp.int32, (1, W_MIX), 1) // DIFF_DQK
    for hc in range(nhc):
        qs_ref[hc * tq:(hc + 1) * tq, :] = jnp.where(qlane == hc, q, 0.0).astype(BF16)
    m_ref[...] = jnp.full_like(m_ref, -jnp.inf)
    acc_ref[...] = jnp.zeros_like(acc_ref)
    krow = lax.broadcasted_iota(jnp.int32, (tk, nhc * tq), 0)
    qcol = lax.broadcasted_iota(jnp.int32, (tk, nhc * tq), 1) % tq
    diag_ok = krow <= qcol

    def scores(j, slot):
        k0 = pl.multiple_of(j * tk, tk)
        st_ref[slot] = _dot_nt(k_ref[0, pl.ds(k0, tk), :], qs_ref[...])

    def softmax_pv(j, slot, masked):
        k0 = pl.multiple_of(j * tk, tk)
        st = st_ref[slot]
        if masked:
            st = jnp.where(diag_ok, st, -jnp.inf)
        m_prev = m_ref[...]
        m_new = jnp.maximum(m_prev, jnp.max(st, axis=0, keepdims=True))
        alpha = jnp.exp2(m_prev - m_new)
        p = jnp.exp2(st - m_new)
        m_ref[...] = m_new
        pb = p.astype(BF16)
        for hc in range(nhc):
            h = hc // 2
            lanes = slice(hc * tq, (hc + 1) * tq)
            pv = jnp.dot(vt_ref[h, :, pl.ds(k0, tk)], pb[:, lanes],
                         preferred_element_type=F32)
            acc_ref[hc] = acc_ref[hc] * alpha[:, lanes] + pv

    scores(0, 0)
    n_pairs = i // 2

    def pair_step(u, carry):
        scores(2 * u + 1, 1)
        softmax_pv(2 * u, 0, False)
        scores(2 * u + 2, 0)
        softmax_pv(2 * u + 1, 1, False)
        return carry

    lax.fori_loop(0, n_pairs, pair_step, 0)

    @pl.when(i % 2 == 0)
    def _():
        softmax_pv(i, 0, True)

    @pl.when(i % 2 == 1)
    def _():
        scores(i, 1)
        softmax_pv(i - 1, 0, False)
        softmax_pv(i, 1, True)

    lam = (jnp.exp(jnp.sum(lam_ref[0:1, :] * lam_ref[1:2, :], axis=-1, keepdims=True))
           - jnp.exp(jnp.sum(lam_ref[2:3, :] * lam_ref[3:4, :], axis=-1, keepdims=True))
           + lam_init)
    heads = []
    for h in range(nh):
        o1 = acc_ref[2 * h, 0:dv] / acc_ref[2 * h, dv:dv + 1]
        o2 = acc_ref[2 * h + 1, 0:dv] / acc_ref[2 * h + 1, dv:dv + 1]
        oh = o1 - lam * o2
        ms = jnp.mean(oh * oh, axis=0, keepdims=True)
        heads.append(oh * lax.rsqrt(ms + RMS_EPS))
    o = jnp.concatenate(heads, axis=0).T
    o_ref[0] = (o * ng_ref[...] * (1.0 - lam_init)).astype(o_ref.dtype)


def _diff_mixer(pd, lam_vecs, norm_g4, lam_init):
    b, s, _ = pd.shape
    tq = DIFF_TQ
    return pl.pallas_call(
        functools.partial(_diff_kernel, lam_init=lam_init),
        grid=(b, s // tq),
        in_specs=[pl.BlockSpec((1, tq, W_MIX), lambda bi, i: (bi, i, 0)),
                  pl.BlockSpec((1, s, W_MIX), lambda bi, i: (bi, 0, 1)),
                  pl.BlockSpec((1, s, W_MIX), lambda bi, i: (bi, 0, 2)),
                  pl.BlockSpec((4, LANES), lambda bi, i: (0, 0)),
                  pl.BlockSpec((1, W_MIX), lambda bi, i: (0, 0))],
        out_specs=pl.BlockSpec((1, tq, W_MIX), lambda bi, i: (bi, i, 0)),
        out_shape=jax.ShapeDtypeStruct((b, s, W_MIX), BF16),
        scratch_shapes=[pltpu.VMEM((DIFF_HEADS, DIFF_VPAD, s), BF16),
                        pltpu.VMEM((2 * DIFF_HEADS * tq, W_MIX), BF16),
                        pltpu.VMEM((2, DIFF_TK, 2 * DIFF_HEADS * tq), F32),
                        pltpu.VMEM((1, 2 * DIFF_HEADS * tq), F32),
                        pltpu.VMEM((2 * DIFF_HEADS, DIFF_VPAD, tq), F32)],
        compiler_params=_cparams(("parallel", "arbitrary")),
        name="diff_attn",
    )(pd, pd, pd, lam_vecs, norm_g4)


def _oproj_kernel(yc_ref, yg_ref, ys_ref, yd_ref, x_ref, wo_ref, g_ref, b_ref, wr_ref,
                  xo_ref, xp_ref, route_ref, cnt_ref):
    mix = jnp.concatenate([yc_ref[...], yg_ref[...], ys_ref[...], yd_ref[...]], axis=-1)
    h = ALPHA * x_ref[...] + jnp.dot(mix, wo_ref[...], preferred_element_type=F32)
    xn = _layer_norm(h, g_ref[...], b_ref[...])
    xo_ref[...] = xn
    xp_ref[...] = _pack_halves(xn)

    xn_hi, xn_lo = _split_bf16(xn, 2)
    both = _dot(xn_hi, wr_ref[...])
    logits = both[:, 0:LANES] + both[:, LANES:2 * LANES] + _dot(xn_lo, wr_ref[:, 0:LANES])
    lane = lax.broadcasted_iota(jnp.int32, logits.shape, 1).astype(F32)
    neg = -jnp.inf
    big = float(LANES)
    lg = jnp.where(lane < N_GROUPS, logits, neg)
    mg = jnp.max(lg, axis=-1, keepdims=True)
    sg = jnp.sum(jnp.exp(lg - mg), axis=-1, keepdims=True)
    grp = jnp.min(jnp.where(lg == mg, lane, big), axis=-1, keepdims=True)
    p_grp = 1.0 / sg
    lo = N_GROUPS + EXPERTS_PER_GROUP * grp
    in_g = jnp.logical_and(lane >= lo, lane < lo + EXPERTS_PER_GROUP)
    le = jnp.where(in_g, logits, neg)
    me = jnp.max(le, axis=-1, keepdims=True)
    ee = jnp.exp(le - me)
    pe = ee / jnp.sum(ee, axis=-1, keepdims=True)
    pe = jnp.where(in_g, pe, -1.0)
    p1 = jnp.max(pe, axis=-1, keepdims=True)
    i1 = jnp.min(jnp.where(pe == p1, lane, big), axis=-1, keepdims=True)
    pe2 = jnp.where(lane == i1, -1.0, pe)
    p2 = jnp.max(pe2, axis=-1, keepdims=True)
    i2 = jnp.min(jnp.where(pe2 == p2, lane, big), axis=-1, keepdims=True)
    den = p1 + p2
    g1 = p_grp * p1 / den
    g2 = p_grp * p2 / den
    e1 = i1 - N_GROUPS
    e2 = i2 - N_GROUPS
    route_ref[...] = jnp.where(lane == 0, e1, jnp.where(lane == 1, e2, jnp.where(
        lane == 2, g1, jnp.where(lane == 3, g2, 0.0))))

    @pl.when(pl.program_id(0) == 0)
    def _():
        cnt_ref[...] = jnp.zeros_like(cnt_ref)

    hits = jnp.where(lane == e1, 1.0, 0.0) + jnp.where(lane == e2, 1.0, 0.0)
    cnt_ref[...] += jnp.sum(hits, axis=0, keepdims=True)


def _out_proj(ys, x2d, w_o, layer, ln_g, ln_b, w_route):
    t = x2d.shape[0]
    tm = 512
    row = lambda i: (i, 0)
    full = lambda i: (0, 0)
    return pl.pallas_call(
        _oproj_kernel,
        grid=(t // tm,),
        in_specs=[pl.BlockSpec((tm, W_MIX), row)] * 4 + [
            pl.BlockSpec((tm, D_MODEL), row),
            pl.BlockSpec((None, D_MODEL, D_MODEL), lambda i: (layer, 0, 0)),
            pl.BlockSpec((1, D_MODEL), full),
            pl.BlockSpec((1, D_MODEL), full),
            pl.BlockSpec((D_MODEL, 2 * LANES), full)],
        out_specs=[pl.BlockSpec((tm, D_MODEL), row), pl.BlockSpec((tm, D_MODEL // 2), row),
                   pl.BlockSpec((tm, LANES), row), pl.BlockSpec((1, LANES), full)],
        out_shape=[jax.ShapeDtypeStruct((t, D_MODEL), F32),
                   jax.ShapeDtypeStruct((t, D_MODEL // 2), U32),
                   jax.ShapeDtypeStruct((t, LANES), F32),
                   jax.ShapeDtypeStruct((1, LANES), F32)],
        compiler_params=_cparams(("arbitrary",)),
        name="out_proj_ln_router",
    )(*ys, x2d, w_o, ln_g, ln_b, w_route)


PLAN_TILE = 512


def _plan_kernel(route_ref, cnt_ref, dest_ref, meta_ref, carry_ref, pstart_ref):
    tm = route_ref.shape[0]
    lane = lax.broadcasted_iota(jnp.int32, (1, LANES), 1).astype(F32)

    @pl.when(pl.program_id(0) == 0)
    def _():
        cnt = cnt_ref[...]
        padded = jnp.ceil(cnt * (1.0 / MOE_BLK)) * MOE_BLK
        li = lax.broadcasted_iota(jnp.int32, (LANES, LANES), 0)
        lj = lax.broadcasted_iota(jnp.int32, (LANES, LANES), 1)
        before = (li < lj).astype(F32)
        pstart = jnp.dot(jnp.broadcast_to(padded, (8, LANES)), before, precision=HI,
                         preferred_element_type=F32)[0:1]
        pstart_ref[...] = pstart
        carry_ref[...] = jnp.zeros_like(carry_ref)
        meta_ref[...] = jnp.concatenate(
            [pstart + padded, pstart, cnt, jnp.zeros((5, LANES), F32)], axis=0)

    oh0 = jnp.where(lane == route_ref[:, 0:1], 1.0, 0.0)
    oh1 = jnp.where(lane == route_ref[:, 1:2], 1.0, 0.0)
    both = oh0 + oh1
    ri = lax.broadcasted_iota(jnp.int32, (tm, tm), 0)
    ci = lax.broadcasted_iota(jnp.int32, (tm, tm), 1)
    earlier = (ci < ri).astype(BF16)
    base = (jnp.dot(earlier, both.astype(BF16), preferred_element_type=F32)
            + carry_ref[...] + pstart_ref[...])
    d0 = jnp.sum(oh0 * base, axis=-1, keepdims=True)
    d1 = jnp.sum(oh1 * base, axis=-1, keepdims=True)
    dest = jnp.where(lane == 0, d0, jnp.where(lane == 1, d1, 0.0))
    dest_ref[...] = dest.T[0:2, :].astype(jnp.int32)
    carry_ref[...] += jnp.sum(both, axis=0, keepdims=True)


def _dispatch_plan(route, cnt):
    t = route.shape[0]
    tm = PLAN_TILE
    blk = MOE_BLK
    dest, meta = pl.pallas_call(
        _plan_kernel,
        grid=(t // tm,),
        in_specs=[pl.BlockSpec((tm, LANES), lambda i: (i, 0)),
                  pl.BlockSpec((1, LANES), lambda i: (0, 0))],
        out_specs=[pl.BlockSpec((2, tm), lambda i: (0, i)),
                   pl.BlockSpec((8, LANES), lambda i: (0, 0))],
        out_shape=[jax.ShapeDtypeStruct((2, t), jnp.int32),
                   jax.ShapeDtypeStruct((8, LANES), F32)],
        scratch_shapes=[pltpu.VMEM((1, LANES), F32), pltpu.VMEM((1, LANES), F32)],
        compiler_params=_cparams(("arbitrary",)),
        name="moe_plan",
    )(route, cnt)
    meta_i = meta[:, :N_EXPERTS].astype(jnp.int32)
    seg_start, seg_rows = meta_i[1], meta_i[2]
    seg_blocks = (meta_i[0] - seg_start) // blk
    return dest.reshape(2 * t), seg_start, seg_blocks, seg_rows


def _ffn_kernel(ps_ref, nb_ref, cnt_ref, xs_ref, wg_ref, wu_ref, wd_ref, y_ref,
                wgb_ref, wub_ref, wdb_ref, xbuf_ref, ybuf_ref, xsem, ysem):
    e = pl.program_id(0)
    blk = xbuf_ref.shape[1]
    base = ps_ref[e]
    n = nb_ref[e]
    n_rows = cnt_ref[e]

    wgb_ref[...] = wg_ref[...].astype(BF16)
    wub_ref[...] = wu_ref[...].astype(BF16)
    wdb_ref[...] = wd_ref[...].astype(BF16)

    def rows_of(j):
        return pl.ds(pl.multiple_of(base + j * blk, blk), blk)

    def fetch(j, s):
        return pltpu.make_async_copy(xs_ref.at[rows_of(j)], xbuf_ref.at[s], xsem.at[s])

    def write_back(j, s):
        return pltpu.make_async_copy(ybuf_ref.at[s], y_ref.at[rows_of(j)], ysem.at[s])

    @pl.when(n > 0)
    def _():
        fetch(0, 0).start()

    def block(j, carry):
        s = j % 2
        fetch(j, s).wait()

        @pl.when(j + 1 < n)
        def _():
            fetch(j + 1, 1 - s).start()

        @pl.when(j >= 2)
        def _():
            write_back(j - 2, s).wait()

        row = lax.broadcasted_iota(jnp.int32, (blk, 1), 0) + j * blk
        xp = jnp.where(row < n_rows, xbuf_ref[s], U32(0))
        x_hi, x_lo = _unpack_halves(xp)
        xb = jnp.concatenate([x_hi.astype(BF16), x_lo.astype(BF16)], axis=1)
        a = jnp.dot(xb, wgb_ref[...], preferred_element_type=F32)
        u = jnp.dot(xb, wub_ref[...], preferred_element_type=F32)
        h = (a * _sigmoid(a) * u).astype(BF16)
        ybuf_ref[s] = _pack_halves(jnp.dot(h, wdb_ref[...], preferred_element_type=F32))
        write_back(j, s).start()
        return carry

    lax.fori_loop(0, n, block, 0)

    @pl.when(n >= 2)
    def _():
        write_back(n - 2, n % 2).wait()

    @pl.when(n >= 1)
    def _():
        write_back(n - 1, (n - 1) % 2).wait()

    @pl.when(e == pl.num_programs(0) - 1)
    def _():
        ybuf_ref[0] = jnp.zeros(ybuf_ref.shape[1:], ybuf_ref.dtype)
        first = (base + n * blk) // blk

        def clear(b, carry):
            cp = pltpu.make_async_copy(
                ybuf_ref.at[0], y_ref.at[pl.ds(pl.multiple_of(b * blk, blk), blk)], ysem.at[0])
            cp.start()
            cp.wait()
            return carry

        lax.fori_loop(first, y_ref.shape[0] // blk, clear, 0)


def _expert_ffn(xs, seg_start, seg_blocks, seg_rows, w_gate, w_up, w_down, layer):
    n_slots = xs.shape[0]
    blk = MOE_BLK
    w_map = lambda e, ps, nb, cn: (layer, e, 0, 0)
    grid_spec = pltpu.PrefetchScalarGridSpec(
        num_scalar_prefetch=3,
        grid=(N_EXPERTS,),
        in_specs=[pl.BlockSpec(memory_space=pl.ANY),
                  pl.BlockSpec((None, None, D_MODEL, D_EXPERT), w_map),
                  pl.BlockSpec((None, None, D_MODEL, D_EXPERT), w_map),
                  pl.BlockSpec((None, None, D_EXPERT, D_MODEL), w_map)],
        out_specs=pl.BlockSpec(memory_space=pl.ANY),
        scratch_shapes=[pltpu.VMEM((D_MODEL, D_EXPERT), BF16),
                        pltpu.VMEM((D_MODEL, D_EXPERT), BF16),
                        pltpu.VMEM((D_EXPERT, D_MODEL), BF16),
                        pltpu.VMEM((2, blk, D_MODEL // 2), U32),
                        pltpu.VMEM((2, blk, D_MODEL // 2), U32),
                        pltpu.SemaphoreType.DMA((2,)),
                        pltpu.SemaphoreType.DMA((2,))],
    )
    return pl.pallas_call(
        _ffn_kernel,
        grid_spec=grid_spec,
        out_shape=jax.ShapeDtypeStruct((n_slots, D_MODEL // 2), U32),
        compiler_params=pltpu.CompilerParams(dimension_semantics=("arbitrary",),
                                             vmem_limit_bytes=VMEM_LIMIT,
                                             has_side_effects=True),
        name="expert_ffn",
    )(seg_start, seg_blocks, seg_rows, xs, w_gate, w_up, w_down)


SC_CORES = 2
SC_SUBCORES = 16
SC_ROWS = 64


def _sc_gather_rows(table, idx):
    b = idx.shape[0]
    d = table.shape[1]
    per_w = b // (SC_CORES * SC_SUBCORES)
    mesh = plsc.VectorSubcoreMesh(core_axis_name="c", subcore_axis_name="s")

    n_chunks = per_w // SC_ROWS

    @functools.partial(
        pl.kernel, mesh=mesh,
        out_type=jax.ShapeDtypeStruct((b, d), table.dtype),
        scratch_types=[pltpu.VMEM((SC_ROWS,), jnp.int32), pltpu.VMEM((SC_ROWS,), jnp.int32),
                       pltpu.VMEM((SC_ROWS, d), table.dtype),
                       pltpu.VMEM((SC_ROWS, d), table.dtype),
                       pltpu.SemaphoreType.DMA, pltpu.SemaphoreType.DMA,
                       pltpu.SemaphoreType.DMA, pltpu.SemaphoreType.DMA],
        name="sc_gather_rows",
    )
    def gather(table_hbm, idx_hbm, out_hbm, idx0, idx1, rows0, rows1, gs0, gs1, ws0, ws1):
        idx_v, rows_v, gsem, wsem = (idx0, idx1), (rows0, rows1), (gs0, gs1), (ws0, ws1)
        wid = lax.axis_index("s") * SC_CORES + lax.axis_index("c")
        base = wid * per_w

        def rows_of(c):
            return pl.ds(pl.multiple_of(base + c * SC_ROWS, SC_ROWS), SC_ROWS)

        def start_gather(c, s):
            pltpu.sync_copy(idx_hbm.at[rows_of(c)], idx_v[s])
            pltpu.async_copy(table_hbm.at[idx_v[s]], rows_v[s], gsem[s])

        def write_back(c, s):
            pltpu.make_async_copy(table_hbm.at[idx_v[s]], rows_v[s], gsem[s]).wait()
            pltpu.async_copy(rows_v[s], out_hbm.at[rows_of(c)], wsem[s]).wait()

        start_gather(0, 0)

        @pl.loop(0, n_chunks, step=2)
        def _(c):
            start_gather(c + 1, 1)
            write_back(c, 0)

            @pl.when(c + 2 < n_chunks)
            def _():
                start_gather(c + 2, 0)

            write_back(c + 1, 1)

    return gather(table, idx)


def _sc_scatter_rows(x2d, idx, n_slots):
    t, d = x2d.shape
    per_w = t // (SC_CORES * SC_SUBCORES)
    mesh = plsc.VectorSubcoreMesh(core_axis_name="c", subcore_axis_name="s")

    @functools.partial(
        pl.kernel, mesh=mesh,
        out_type=jax.ShapeDtypeStruct((n_slots, d), x2d.dtype),
        scratch_types=[pltpu.VMEM((SC_ROWS,), jnp.int32), pltpu.VMEM((SC_ROWS,), jnp.int32),
                       pltpu.VMEM((SC_ROWS, d), x2d.dtype),
                       pltpu.SemaphoreType.DMA, pltpu.SemaphoreType.DMA],
        name="sc_scatter_rows",
    )
    def scatter(x_hbm, idx_hbm, out_hbm, idx0, idx1, rows_v, s0, s1):
        wid = lax.axis_index("s") * SC_CORES + lax.axis_index("c")
        base = wid * per_w

        @pl.loop(0, per_w // SC_ROWS)
        def _(c):
            off = pl.multiple_of(base + c * SC_ROWS, SC_ROWS)
            pltpu.sync_copy(x_hbm.at[pl.ds(off, SC_ROWS)], rows_v)
            pltpu.sync_copy(idx_hbm.at[pl.ds(off, SC_ROWS)], idx0)
            pltpu.sync_copy(idx_hbm.at[pl.ds(t + off, SC_ROWS)], idx1)
            cp0 = pltpu.async_copy(rows_v, out_hbm.at[idx0], s0)
            cp1 = pltpu.async_copy(rows_v, out_hbm.at[idx1], s1)
            cp0.wait()
            cp1.wait()

    return scatter(x2d, idx)


def _combine_dense_kernel(route_ref, x_ref, y0_ref, y1_ref, g_ref, b_ref, o_ref):
    y0 = jnp.concatenate(_unpack_halves(y0_ref[...]), axis=1)
    y1 = jnp.concatenate(_unpack_halves(y1_ref[...]), axis=1)
    moe = route_ref[:, 2:3] * y0 + route_ref[:, 3:4] * y1
    h = ALPHA * x_ref[...] + moe
    o_ref[...] = _layer_norm(h, g_ref[...], b_ref[...])


def _combine_dense(route, x2d, yg, ln_g, ln_b):
    t = x2d.shape[0]
    tm = 512
    nt = t // tm
    row = lambda i: (i, 0)
    full = lambda i: (0, 0)
    return pl.pallas_call(
        _combine_dense_kernel,
        grid=(nt,),
        in_specs=[pl.BlockSpec((tm, LANES), row),
                  pl.BlockSpec((tm, D_MODEL), row),
                  pl.BlockSpec((tm, D_MODEL // 2), row),
                  pl.BlockSpec((tm, D_MODEL // 2), lambda i: (i + nt, 0)),
                  pl.BlockSpec((1, D_MODEL), full),
                  pl.BlockSpec((1, D_MODEL), full)],
        out_specs=pl.BlockSpec((tm, D_MODEL), row),
        out_shape=jax.ShapeDtypeStruct((t, D_MODEL), F32),
        compiler_params=_cparams(("parallel",)),
        name="moe_combine_dense",
    )(route, x2d, yg, yg, ln_g, ln_b)


def _pad_cols(w, n):
    return jnp.pad(w, [(0, 0)] * (w.ndim - 1) + [(0, n - w.shape[-1])])


def kernel(x, w_in, conv_w, gla_w_lr, gla_b_lr, gla_norm_g, ssd_conv_w, ssd_conv_b, ssd_a_log,
           ssd_d, ssd_dt_bias, ssd_norm_g, diff_lq1, diff_lk1, diff_lq2, diff_lk2, diff_norm_g,
           w_o, ln1_g, ln1_b, router_g, router_e, w_gate, w_up, w_down, ln2_g, ln2_b):
    bsz, seq, d = x.shape
    t = bsz * seq
    n_assign = 2 * t
    n_blocks = (n_assign + N_EXPERTS * (MOE_BLK - 1)) // MOE_BLK + 1
    n_slots = n_blocks * MOE_BLK
    x2d = x.reshape(t, d)
    w_in_r = jnp.concatenate([w_in[..., 0:768], _pad_cols(w_in[..., 768:1552], 896),
                              _pad_cols(w_in[..., 1552:2580], 1152), w_in[..., 2580:3348]],
                             axis=-1).astype(BF16)
    w_o_b = w_o.astype(BF16)
    for l in range(DEPTH):
        pc, pg, plr, ps, pdt, pd = [p.reshape(bsz, seq, -1) for p in _in_proj(x2d, w_in_r, l)]

        y_conv = _conv_mixer(pc, conv_w[l])
        w_lr_pad = jnp.pad(gla_w_lr[l], ((0, LANES - GLA_RANK), (0, 0)))
        y_gla = _gla_mixer(pg, plr, w_lr_pad, gla_b_lr[l].reshape(1, -1),
                           jnp.tile(gla_norm_g[l], GLA_HEADS).reshape(1, -1))
        pad4 = lambda v: jnp.pad(v, (0, LANES - SSD_HEADS)).reshape(1, LANES)
        y_ssd = _ssd_mixer(ps, pdt, ssd_conv_w[l], ssd_conv_b[l].reshape(1, -1),
                           pad4(ssd_a_log[l]), pad4(ssd_dt_bias[l]),
                           jnp.repeat(ssd_d[l], SSD_HEADDIM).reshape(1, -1),
                           ssd_norm_g[l].reshape(1, -1))
        lam_vecs = jnp.pad(jnp.stack([diff_lq1[l], diff_lk1[l], diff_lq2[l], diff_lk2[l]]),
                           ((0, 0), (0, LANES - DIFF_DQK)))
        lam_init = 0.8 - 0.6 * math.exp(-0.3 * l)
        y_diff = _diff_mixer(pd, lam_vecs,
                             jnp.tile(diff_norm_g[l], DIFF_HEADS).reshape(1, -1), lam_init)

        w_route = _pad_cols(jnp.concatenate(
            [router_g[l], router_e[l].reshape(d, N_EXPERTS)], axis=1), LANES)
        w_route_hi = w_route.astype(BF16)
        w_route = jnp.concatenate(
            [w_route_hi, (w_route - w_route_hi.astype(F32)).astype(BF16)], axis=1)
        ys = [y.reshape(t, W_MIX) for y in (y_conv, y_gla, y_ssd, y_diff)]
        xn, xn_p, route, cnt = _out_proj(ys, x2d, w_o_b, l, ln1_g[l].reshape(1, -1),
                                         ln1_b[l].reshape(1, -1), w_route)

        idx, seg_start, seg_blocks, seg_rows = _dispatch_plan(route, cnt)
        xs = _sc_scatter_rows(xn_p, idx, n_slots)
        y = _expert_ffn(xs, seg_start, seg_blocks, seg_rows, w_gate, w_up, w_down, l)
        yg = _sc_gather_rows(y, idx)
        x2d = _combine_dense(route, xn, yg, ln2_g[l].reshape(1, -1), ln2_b[l].reshape(1, -1))
    return x2d.reshape(bsz, seq, d)
```

```python
import functools
import math

import jax
import jax.numpy as jnp
from jax import lax
from jax.experimental import pallas as pl
from jax.experimental.pallas import tpu as pltpu
from jax.experimental.pallas import tpu_sc as plsc

F32 = jnp.float32
BF16 = jnp.bfloat16
HI = lax.Precision.HIGHEST

D_MODEL = 1024
DEPTH = 2
W_MIX = 256
GLA_HEADS, GLA_DK, GLA_DV, GLA_RANK, GLA_TAU, GLA_CHUNK = 4, 32, 64, 16, 16.0, 64
GLA_ROWS = 256
GLA_SEQS = 4
SSD_SEQS = 2
SSD_HEADS, SSD_GROUPS, SSD_HEADDIM, SSD_STATE, SSD_CONV_K, SSD_CHUNK = 4, 2, 64, 128, 4, 128
DIFF_HEADS, DIFF_DQK, DIFF_DV = 4, 32, 64
N_GROUPS, EXPERTS_PER_GROUP, N_EXPERTS, D_EXPERT = 4, 8, 32, 512
ALPHA = (2 * DEPTH) ** 0.25
LN_EPS = 1e-5
RMS_EPS = 1e-6

LANES = 128
SUBLANES = 8
PROJ_WIDTHS = (768, 768, 128, 1024, 128, 768)
PROJ_DTYPES = (BF16, BF16, F32, BF16, F32, BF16)
VMEM_LIMIT = 56 * 1024 * 1024

MOE_BLK = 512
TOK_TILE = 256


def _cparams(sem):
    return pltpu.CompilerParams(dimension_semantics=sem, vmem_limit_bytes=VMEM_LIMIT)


def _sigmoid(x):
    return 1.0 / (1.0 + jnp.exp(-x))


def _softplus(x):
    return jnp.maximum(x, 0.0) + jnp.log(1.0 + jnp.exp(-jnp.abs(x)))


def _layer_norm(h, g, b):
    mu = jnp.mean(h, axis=-1, keepdims=True)
    d = h - mu
    var = jnp.mean(d * d, axis=-1, keepdims=True)
    return d * lax.rsqrt(var + LN_EPS) * g + b


def _dot_nt(a, b):
    return lax.dot_general(a, b, (((1,), (1,)), ((), ())), preferred_element_type=F32)


def _dot_tn(a, b, precision=None):
    return lax.dot_general(a, b, (((0,), (0,)), ((), ())), preferred_element_type=F32,
                           precision=precision)


def _split_bf16(x, parts):
    out = []
    for _ in range(parts - 1):
        hi = x.astype(BF16)
        out.append(hi)
        x = x - hi.astype(F32)
    out.append(x.astype(BF16))
    return out


def _dot(a, b):
    return jnp.dot(a, b, preferred_element_type=F32)


U32 = jnp.uint32


def _pack_halves(x):
    w = x.shape[1] // 2
    hi = lax.bitcast_convert_type(x[:, :w].astype(BF16).astype(F32), U32)
    lo = lax.bitcast_convert_type(x[:, w:].astype(BF16).astype(F32), U32)
    return hi | lax.shift_right_logical(lo, U32(16))


def _unpack_halves(p):
    hi = lax.bitcast_convert_type(p & U32(0xFFFF0000), F32)
    lo = lax.bitcast_convert_type(lax.shift_left(p, U32(16)), F32)
    return hi, lo


def _dot_split_lhs(a, b_exact, parts, dot=_dot):
    acc = None
    for term in _split_bf16(a, parts):
        d = dot(term, b_exact)
        acc = d if acc is None else acc + d
    return acc


def _dot_split_rhs(a_exact, b, parts):
    acc = None
    for term in _split_bf16(b, parts):
        d = jnp.dot(a_exact, term, preferred_element_type=F32)
        acc = d if acc is None else acc + d
    return acc


def _proj_kernel(x_ref, w_ref, *o_refs):
    xb = x_ref[...].astype(BF16)
    off = 0
    for o_ref in o_refs:
        n = o_ref.shape[-1]
        o_ref[...] = jnp.dot(xb, w_ref[:, off:off + n],
                             preferred_element_type=F32).astype(o_ref.dtype)
        off += n


def _in_proj(x2d, w_r, layer):
    t = x2d.shape[0]
    tm = 1024
    ncol = sum(PROJ_WIDTHS)
    return pl.pallas_call(
        _proj_kernel,
        grid=(t // tm,),
        in_specs=[pl.BlockSpec((tm, D_MODEL), lambda i: (i, 0)),
                  pl.BlockSpec((None, D_MODEL, ncol), lambda i: (layer, 0, 0))],
        out_specs=[pl.BlockSpec((tm, n), lambda i: (i, 0)) for n in PROJ_WIDTHS],
        out_shape=[jax.ShapeDtypeStruct((t, n), dt) for n, dt in zip(PROJ_WIDTHS, PROJ_DTYPES)],
        compiler_params=_cparams(("parallel",)),
        name="in_proj",
    )(x2d, w_r)


def _conv_kernel(p_ref, w_ref, o_ref):
    u = p_ref[0, :, 0:W_MIX].astype(F32)
    gb = p_ref[0, :, W_MIX:2 * W_MIX].astype(F32)
    gc = p_ref[0, :, 2 * W_MIX:3 * W_MIX].astype(F32)
    cu = gc * u
    row = lax.broadcasted_iota(jnp.int32, cu.shape, 0)
    acc = cu * w_ref[2:3, :]
    for s in (1, 2):
        sh = jnp.where(row >= s, pltpu.roll(cu, s, axis=0), 0.0)
        acc = acc + sh * w_ref[2 - s:3 - s, :]
    o_ref[0] = (gb * acc).astype(o_ref.dtype)


def _conv_mixer(pc, conv_w):
    b, s, _ = pc.shape
    return pl.pallas_call(
        _conv_kernel,
        grid=(b,),
        in_specs=[pl.BlockSpec((1, s, 3 * W_MIX), lambda i: (i, 0, 0)),
                  pl.BlockSpec((3, W_MIX), lambda i: (0, 0))],
        out_specs=pl.BlockSpec((1, s, W_MIX), lambda i: (i, 0, 0)),
        out_shape=jax.ShapeDtypeStruct((b, s, W_MIX), BF16),
        compiler_params=_cparams(("parallel",)),
        name="conv_mixer",
    )(pc, conv_w)


def _gla_kernel(p_ref, lr_ref, wlr_ref, blr_ref, ng_ref, o_ref, st_ref):
    c = GLA_CHUNK
    s_len = p_ref.shape[1]
    nh, dk, dv = GLA_HEADS, GLA_DK, GLA_DV
    st_ref[...] = jnp.zeros_like(st_ref)

    rb = GLA_ROWS
    ncb = rb // c
    ri = lax.broadcasted_iota(jnp.int32, (rb, rb), 0)
    ci = lax.broadcasted_iota(jnp.int32, (rb, rb), 1)
    tri = (ci <= ri).astype(BF16)
    klane_head = lax.broadcasted_iota(jnp.int32, (1, nh * dk), 1) // dk
    vlane_head = lax.broadcasted_iota(jnp.int32, (1, nh * dv), 1) // dv
    strow_head = lax.broadcasted_iota(jnp.int32, (nh * dv, 1), 0) // dv
    st_mask = strow_head == klane_head
    r4 = lax.broadcasted_iota(jnp.int32, (nh * c, c), 0) % c
    c4 = lax.broadcasted_iota(jnp.int32, (nh * c, c), 1)
    causal4 = c4 <= r4
    gi = lax.broadcasted_iota(jnp.int32, (nh * dv, nh * dv), 0) // dv
    gj = lax.broadcasted_iota(jnp.int32, (nh * dv, nh * dv), 1) // dv
    gmean = jnp.where(gi == gj, 1.0 / dv, 0.0).astype(BF16)
    wlr_hi, wlr_lo = _split_bf16(wlr_ref[...], 2)

    def one_seq(bb, rows):
        q = p_ref[bb, rows, 0:128].astype(F32) * (dk ** -0.5)
        k = p_ref[bb, rows, 128:256].astype(F32)
        vb = p_ref[bb, rows, 256:512]
        g = p_ref[bb, rows, 512:768].astype(F32)
        lr = lr_ref[bb, rows, :]
        lr_hi, lr_lo = _split_bf16(lr, 2)
        z = (jnp.dot(lr_hi, wlr_hi, preferred_element_type=F32)
             + jnp.dot(lr_hi, wlr_lo, preferred_element_type=F32)
             + jnp.dot(lr_lo, wlr_hi, preferred_element_type=F32)) + blr_ref[...]
        log_a = (jnp.minimum(z, 0.0) - jnp.log(1.0 + jnp.exp(-jnp.abs(z)))) * (1.0 / GLA_TAU)
        cumb = _dot_split_rhs(tri, log_a, 3)
        ends = [cumb[(j + 1) * c - 1:(j + 1) * c, :] for j in range(ncb)]
        starts = [jnp.zeros_like(ends[0])] + ends[:-1]
        cum = cumb - jnp.concatenate([jnp.broadcast_to(s0, (c, nh * dk)) for s0 in starts], axis=0)
        lasts = [e - s0 for e, s0 in zip(ends, starts)]
        cl = jnp.concatenate([jnp.broadcast_to(x, (c, nh * dk)) for x in lasts], axis=0)
        q_dec = q * jnp.exp(cum)
        k_inv = (k * jnp.exp(-cum)).astype(BF16)
        k_end = (k * jnp.exp(cl - cum)).astype(BF16)
        st = st_ref[bb]
        outs = []
        for j in range(ncb):
            sl = slice(j * c, (j + 1) * c)
            qd = q_dec[sl]
            qs = jnp.concatenate([jnp.where(klane_head == h, qd, 0.0) for h in range(nh)],
                                 axis=0).astype(BF16)
            att = jnp.where(causal4, _dot_nt(qs, k_inv[sl]), 0.0)
            r = jnp.dot(att.astype(BF16), vb[sl], preferred_element_type=F32)
            o = jnp.where(vlane_head == 0, r[0:c], 0.0)
            for h in range(1, nh):
                o = o + jnp.where(vlane_head == h, r[h * c:(h + 1) * c], 0.0)
            outs.append(o + _dot_nt(qd.astype(BF16), st.astype(BF16)))
            d_st = _dot_tn(vb[sl], k_end[sl])
            st = st * jnp.exp(lasts[j]) + jnp.where(st_mask, d_st, 0.0)
        st_ref[bb] = st
        o = jnp.concatenate(outs, axis=0)
        ms = _dot_split_lhs(o * o, gmean, 2)
        o = o * lax.rsqrt(ms + RMS_EPS) * ng_ref[...]
        o_ref[bb, rows, :] = (o * (g * _sigmoid(g))).astype(o_ref.dtype)

    def body(n, carry):
        rows = pl.ds(pl.multiple_of(n * rb, rb), rb)
        for bb in range(p_ref.shape[0]):
            one_seq(bb, rows)
        return carry

    lax.fori_loop(0, s_len // rb, body, 0)


def _gla_mixer(pg, plr, w_lr_pad, b_lr, norm_g4):
    b, s, wp = pg.shape
    nb = GLA_SEQS
    return pl.pallas_call(
        _gla_kernel,
        grid=(b // nb,),
        in_specs=[pl.BlockSpec((nb, s, wp), lambda i: (i, 0, 0)),
                  pl.BlockSpec((nb, s, LANES), lambda i: (i, 0, 0)),
                  pl.BlockSpec((LANES, LANES), lambda i: (0, 0)),
                  pl.BlockSpec((1, LANES), lambda i: (0, 0)),
                  pl.BlockSpec((1, W_MIX), lambda i: (0, 0))],
        out_specs=pl.BlockSpec((nb, s, W_MIX), lambda i: (i, 0, 0)),
        out_shape=jax.ShapeDtypeStruct((b, s, W_MIX), BF16),
        scratch_shapes=[pltpu.VMEM((nb, GLA_HEADS * GLA_DV, GLA_HEADS * GLA_DK), F32)],
        compiler_params=_cparams(("parallel",)),
        name="gla_mixer",
    )(pg, plr, w_lr_pad, b_lr, norm_g4)


def _ssd_kernel(p_ref, dt_ref, cw_ref, cb_ref, alog_ref, dtb_ref, dsk_ref, ng_ref, o_ref, st_ref):
    c = SSD_CHUNK
    s_len = p_ref.shape[1]
    n_st = SSD_STATE
    st_ref[...] = jnp.zeros_like(st_ref)

    ri = lax.broadcasted_iota(jnp.int32, (c, c), 0)
    ci = lax.broadcasted_iota(jnp.int32, (c, c), 1)
    causal = ci <= ri
    tri = causal.astype(BF16)
    upper = (ri <= ci).astype(BF16)
    lane_head = lax.broadcasted_iota(jnp.int32, (1, W_MIX), 1) // SSD_HEADDIM
    lane_group = lane_head // (SSD_HEADS // SSD_GROUPS)
    eh = lax.broadcasted_iota(jnp.int32, (LANES, W_MIX), 0)
    el = lax.broadcasted_iota(jnp.int32, (LANES, W_MIX), 1) // SSD_HEADDIM
    expand = (eh == el).astype(BF16)
    row8 = lax.broadcasted_iota(jnp.int32, (8, 3 * W_MIX), 0)
    a_c = -jnp.exp(alog_ref[...])

    def one_chunk(n, bb):
        r0 = pl.multiple_of(n * c, c)
        rows = pl.ds(r0, c)
        cur = p_ref[bb, rows, 256:1024].astype(F32)
        p0 = pl.multiple_of(jnp.maximum(r0 - 2 * SUBLANES, 0), 2 * SUBLANES)
        prev8 = p_ref[bb, pl.ds(p0, 2 * SUBLANES), 256:1024].astype(F32)[SUBLANES:]
        prev8 = jnp.where(n > 0, prev8, 0.0)
        acc = cur * cw_ref[3:4, :] + cb_ref[...]
        for s in (1, 2, 3):
            sh = pltpu.roll(cur, s, axis=0)
            top = jnp.where(row8 < s, pltpu.roll(prev8, s, axis=0), sh[0:8])
            sh = jnp.concatenate([top, sh[8:]], axis=0)
            acc = acc + sh * cw_ref[3 - s:4 - s, :]
        xbc = acc * _sigmoid(acc)
        x = xbc[:, 0:256]
        bm = xbc[:, 256:512].astype(BF16)
        cm = xbc[:, 512:768].astype(BF16)

        dt_c = _softplus(dt_ref[bb, rows, :] + dtb_ref[...])
        da_c = dt_c * a_c
        cum_c = _dot_split_rhs(tri, da_c, 3)
        cum_r = _dot_split_lhs(da_c, upper, 3, dot=_dot_tn)
        both_x = _dot_split_lhs(jnp.concatenate([dt_c, cum_c], axis=0), expand, 3)
        dt_x = both_x[0:c]
        cum_x = both_x[c:2 * c]
        cl_x = cum_x[c - 1:c, :]
        x_dt = x * dt_x
        x_dt_b = x_dt.astype(BF16)
        xw_b = (x_dt * jnp.exp(cl_x - cum_x)).astype(BF16)

        y = x * dsk_ref[...]
        y_off = jnp.zeros((c, W_MIX), F32)
        for g in range(SSD_GROUPS):
            bg = bm[:, g * n_st:(g + 1) * n_st]
            cg = cm[:, g * n_st:(g + 1) * n_st]
            cb = _dot_nt(cg, bg)
            for r in range(SSD_HEADS // SSD_GROUPS):
                h = g * (SSD_HEADS // SSD_GROUPS) + r
                diff = cum_c[:, h:h + 1] - cum_r[h:h + 1, :]
                dec = jnp.exp(jnp.where(causal, diff, -jnp.inf))
                m = (cb * dec).astype(BF16)
                yh = jnp.dot(m, x_dt_b, preferred_element_type=F32)
                y = y + jnp.where(lane_head == h, yh, 0.0)
            st = st_ref[bb, g]
            y_off = y_off + jnp.where(lane_group == g,
                                      jnp.dot(cg, st.astype(BF16), preferred_element_type=F32), 0.0)
            d_st = _dot_tn(bg, xw_b)
            st_ref[bb, g] = st * jnp.exp(cl_x) + jnp.where(lane_group == g, d_st, 0.0)
        y = y + y_off * jnp.exp(cum_x)
        zg = p_ref[bb, rows, 0:256].astype(F32)
        y = y * (zg * _sigmoid(zg))
        outs = []
        for g in range(SSD_GROUPS):
            yg = y[:, g * 128:(g + 1) * 128]
            ms = jnp.mean(yg * yg, axis=-1, keepdims=True)
            outs.append(yg * lax.rsqrt(ms + RMS_EPS))
        o_ref[bb, rows, :] = (jnp.concatenate(outs, axis=-1) * ng_ref[...]).astype(o_ref.dtype)

    def body(n, carry):
        for bb in range(p_ref.shape[0]):
            one_chunk(n, bb)
        return carry

    lax.fori_loop(0, s_len // c, body, 0)


def _ssd_mixer(ps, pdt, conv_w, conv_b, a_log_c, dt_bias_c, d_x, norm_g):
    b, s, wp = ps.shape
    nb = SSD_SEQS
    full2 = lambda i: (0, 0)
    return pl.pallas_call(
        _ssd_kernel,
        grid=(b // nb,),
        in_specs=[pl.BlockSpec((nb, s, wp), lambda i: (i, 0, 0)),
                  pl.BlockSpec((nb, s, LANES), lambda i: (i, 0, 0)),
                  pl.BlockSpec((SSD_CONV_K, 3 * W_MIX), full2),
                  pl.BlockSpec((1, 3 * W_MIX), full2),
                  pl.BlockSpec((1, LANES), full2),
                  pl.BlockSpec((1, LANES), full2),
                  pl.BlockSpec((1, W_MIX), full2),
                  pl.BlockSpec((1, W_MIX), full2)],
        out_specs=pl.BlockSpec((nb, s, W_MIX), lambda i: (i, 0, 0)),
        out_shape=jax.ShapeDtypeStruct((b, s, W_MIX), BF16),
        scratch_shapes=[pltpu.VMEM((nb, SSD_GROUPS, SSD_STATE, W_MIX), F32)],
        compiler_params=_cparams(("parallel",)),
        name="ssd_mixer",
    )(ps, pdt, conv_w, conv_b, a_log_c, dt_bias_c, d_x, norm_g)


DIFF_TQ = 256
DIFF_TK = 256
LOG2E = 1.4426950408889634
DIFF_VPAD = DIFF_DV + 16


def _diff_kernel(q_ref, k_ref, v_ref, lam_ref, ng_ref, o_ref,
                 vt_ref, qs_ref, st_ref, m_ref, acc_ref, *, lam_init):
    tq, tk = DIFF_TQ, DIFF_TK
    nh, dv = DIFF_HEADS, DIFF_DV
    nhc = 2 * nh
    s_len = k_ref.shape[1]
    i = pl.program_id(1)

    @pl.when(i == 0)
    def _():
        for cblk in range(s_len // tk):
            cols = slice(cblk * tk, (cblk + 1) * tk)
            vt = v_ref[0, cols, :].astype(F32).T.astype(BF16)
            for h in range(nh):
                vt_ref[h, 0:dv, cols] = vt[h * dv:(h + 1) * dv]
        vt_ref[:, dv:, :] = jnp.ones((nh, DIFF_VPAD - dv, s_len), BF16)

    q = q_ref[0].astype(F32) * (DIFF_DQK ** -0.5 * LOG2E)
    qlane = lax.broadcasted_iota(jnp.int32, (1, W_MIX), 1) // DIFF_DQK
    for hc in range(nhc):
        qs_ref[hc * tq:(hc + 1) * tq, :] = jnp.where(qlane == hc, q, 0.0).astype(BF16)
    m_ref[...] = jnp.full_like(m_ref, -jnp.inf)
    acc_ref[...] = jnp.zeros_like(acc_ref)
    krow = lax.broadcasted_iota(jnp.int32, (tk, nhc * tq), 0)
    qcol = lax.broadcasted_iota(jnp.int32, (tk, nhc * tq), 1) % tq
    diag_ok = krow <= qcol

    def scores(j, slot):
        k0 = pl.multiple_of(j * tk, tk)
        st_ref[slot] = _dot_nt(k_ref[0, pl.ds(k0, tk), :], qs_ref[...])

    def softmax_pv(j, slot, masked):
        k0 = pl.multiple_of(j * tk, tk)
        st = st_ref[slot]
        if masked:
            st = jnp.where(diag_ok, st, -jnp.inf)
        m_prev = m_ref[...]
        m_new = jnp.maximum(m_prev, jnp.max(st, axis=0, keepdims=True))
        alpha = jnp.exp2(m_prev - m_new)
        p = jnp.exp2(st - m_new)
        m_ref[...] = m_new
        pb = p.astype(BF16)
        for hc in range(nhc):
            h = hc // 2
            lanes = slice(hc * tq, (hc + 1) * tq)
            pv = jnp.dot(vt_ref[h, :, pl.ds(k0, tk)], pb[:, lanes],
                         preferred_element_type=F32)
            acc_ref[hc] = acc_ref[hc] * alpha[:, lanes] + pv

    scores(0, 0)
    n_pairs = i // 2

    def pair_step(u, carry):
        scores(2 * u + 1, 1)
        softmax_pv(2 * u, 0, False)
        scores(2 * u + 2, 0)
        softmax_pv(2 * u + 1, 1, False)
        return carry

    lax.fori_loop(0, n_pairs, pair_step, 0)

    @pl.when(i % 2 == 0)
    def _():
        softmax_pv(i, 0, True)

    @pl.when(i % 2 == 1)
    def _():
        scores(i, 1)
        softmax_pv(i - 1, 0, False)
        softmax_pv(i, 1, True)

    lam = (jnp.exp(jnp.sum(lam_ref[0:1, :] * lam_ref[1:2, :], axis=-1, keepdims=True))
           - jnp.exp(jnp.sum(lam_ref[2:3, :] * lam_ref[3:4, :], axis=-1, keepdims=True))
           + lam_init)
    heads = []
    for h in range(nh):
        o1 = acc_ref[2 * h, 0:dv] / acc_ref[2 * h, dv:dv + 1]
        o2 = acc_ref[2 * h + 1, 0:dv] / acc_ref[2 * h + 1, dv:dv + 1]
        oh = o1 - lam * o2
        ms = jnp.mean(oh * oh, axis=0, keepdims=True)
        heads.append(oh * lax.rsqrt(ms + RMS_EPS))
    o = jnp.concatenate(heads, axis=0).T
    o_ref[0] = (o * ng_ref[...] * (1.0 - lam_init)).astype(o_ref.dtype)


def _diff_mixer(pd, lam_vecs, norm_g4, lam_init):
    b, s, _ = pd.shape
    tq = DIFF_TQ
    return pl.pallas_call(
        functools.partial(_diff_kernel, lam_init=lam_init),
        grid=(b, s // tq),
        in_specs=[pl.BlockSpec((1, tq, W_MIX), lambda bi, i: (bi, i, 0)),
                  pl.BlockSpec((1, s, W_MIX), lambda bi, i: (bi, 0, 1)),
                  pl.BlockSpec((1, s, W_MIX), lambda bi, i: (bi, 0, 2)),
                  pl.BlockSpec((4, LANES), lambda bi, i: (0, 0)),
                  pl.BlockSpec((1, W_MIX), lambda bi, i: (0, 0))],
        out_specs=pl.BlockSpec((1, tq, W_MIX), lambda bi, i: (bi, i, 0)),
        out_shape=jax.ShapeDtypeStruct((b, s, W_MIX), BF16),
        scratch_shapes=[pltpu.VMEM((DIFF_HEADS, DIFF_VPAD, s), BF16),
                        pltpu.VMEM((2 * DIFF_HEADS * tq, W_MIX), BF16),
                        pltpu.VMEM((2, DIFF_TK, 2 * DIFF_HEADS * tq), F32),
                        pltpu.VMEM((1, 2 * DIFF_HEADS * tq), F32),
                        pltpu.VMEM((2 * DIFF_HEADS, DIFF_VPAD, tq), F32)],
        compiler_params=_cparams(("parallel", "arbitrary")),
        name="diff_attn",
    )(pd, pd, pd, lam_vecs, norm_g4)


def _oproj_kernel(yc_ref, yg_ref, ys_ref, yd_ref, x_ref, wo_ref, g_ref, b_ref, wr_ref,
                  xo_ref, xp_ref, route_ref, cnt_ref):
    mix = jnp.concatenate([yc_ref[...], yg_ref[...], ys_ref[...], yd_ref[...]], axis=-1)
    h = ALPHA * x_ref[...] + jnp.dot(mix, wo_ref[...], preferred_element_type=F32)
    xn = _layer_norm(h, g_ref[...], b_ref[...])
    xo_ref[...] = xn
    xp_ref[...] = _pack_halves(xn)

    xn_hi, xn_lo = _split_bf16(xn, 2)
    both = _dot(xn_hi, wr_ref[...])
    logits = both[:, 0:LANES] + both[:, LANES:2 * LANES] + _dot(xn_lo, wr_ref[:, 0:LANES])
    lane = lax.broadcasted_iota(jnp.int32, logits.shape, 1).astype(F32)
    neg = -jnp.inf
    big = float(LANES)
    lg = jnp.where(lane < N_GROUPS, logits, neg)
    mg = jnp.max(lg, axis=-1, keepdims=True)
    sg = jnp.sum(jnp.exp(lg - mg), axis=-1, keepdims=True)
    grp = jnp.min(jnp.where(lg == mg, lane, big), axis=-1, keepdims=True)
    p_grp = 1.0 / sg
    lo = N_GROUPS + EXPERTS_PER_GROUP * grp
    in_g = jnp.logical_and(lane >= lo, lane < lo + EXPERTS_PER_GROUP)
    le = jnp.where(in_g, logits, neg)
    me = jnp.max(le, axis=-1, keepdims=True)
    ee = jnp.exp(le - me)
    pe = ee / jnp.sum(ee, axis=-1, keepdims=True)
    pe = jnp.where(in_g, pe, -1.0)
    p1 = jnp.max(pe, axis=-1, keepdims=True)
    i1 = jnp.min(jnp.where(pe == p1, lane, big), axis=-1, keepdims=True)
    pe2 = jnp.where(lane == i1, -1.0, pe)
    p2 = jnp.max(pe2, axis=-1, keepdims=True)
    i2 = jnp.min(jnp.where(pe2 == p2, lane, big), axis=-1, keepdims=True)
    den = p1 + p2
    g1 = p_grp * p1 / den
    g2 = p_grp * p2 / den
    e1 = i1 - N_GROUPS
    e2 = i2 - N_GROUPS
    route_ref[...] = jnp.where(lane == 0, e1, jnp.where(lane == 1, e2, jnp.where(
        lane == 2, g1, jnp.where(lane == 3, g2, 0.0))))

    @pl.when(pl.program_id(0) == 0)
    def _():
        cnt_ref[...] = jnp.zeros_like(cnt_ref)

    hits = jnp.where(lane == e1, 1.0, 0.0) + jnp.where(lane == e2, 1.0, 0.0)
    cnt_ref[...] += jnp.sum(hits, axis=0, keepdims=True)


def _out_proj(ys, x2d, w_o, layer, ln_g, ln_b, w_route):
    t = x2d.shape[0]
    tm = 1024
    row = lambda i: (i, 0)
    full = lambda i: (0, 0)
    return pl.pallas_call(
        _oproj_kernel,
        grid=(t // tm,),
        in_specs=[pl.BlockSpec((tm, W_MIX), row)] * 4 + [
            pl.BlockSpec((tm, D_MODEL), row),
            pl.BlockSpec((None, D_MODEL, D_MODEL), lambda i: (layer, 0, 0)),
            pl.BlockSpec((1, D_MODEL), full),
            pl.BlockSpec((1, D_MODEL), full),
            pl.BlockSpec((D_MODEL, 2 * LANES), full)],
        out_specs=[pl.BlockSpec((tm, D_MODEL), row), pl.BlockSpec((tm, D_MODEL // 2), row),
                   pl.BlockSpec((tm, LANES), row), pl.BlockSpec((1, LANES), full)],
        out_shape=[jax.ShapeDtypeStruct((t, D_MODEL), F32),
                   jax.ShapeDtypeStruct((t, D_MODEL // 2), U32),
                   jax.ShapeDtypeStruct((t, LANES), F32),
                   jax.ShapeDtypeStruct((1, LANES), F32)],
        compiler_params=_cparams(("arbitrary",)),
        name="out_proj_ln_router",
    )(*ys, x2d, w_o, ln_g, ln_b, w_route)


PLAN_TILE = 512


def _plan_kernel(route_ref, cnt_ref, dest_ref, meta_ref, carry_ref, pstart_ref):
    tm = route_ref.shape[0]
    lane = lax.broadcasted_iota(jnp.int32, (1, LANES), 1).astype(F32)

    @pl.when(pl.program_id(0) == 0)
    def _():
        cnt = cnt_ref[...]
        padded = jnp.ceil(cnt * (1.0 / MOE_BLK)) * MOE_BLK
        li = lax.broadcasted_iota(jnp.int32, (LANES, LANES), 0)
        lj = lax.broadcasted_iota(jnp.int32, (LANES, LANES), 1)
        before = (li < lj).astype(F32)
        pstart = jnp.dot(jnp.broadcast_to(padded, (8, LANES)), before, precision=HI,
                         preferred_element_type=F32)[0:1]
        pstart_ref[...] = pstart
        carry_ref[...] = jnp.zeros_like(carry_ref)
        meta_ref[...] = jnp.concatenate(
            [pstart + padded, pstart, cnt, jnp.zeros((5, LANES), F32)], axis=0)

    oh0 = jnp.where(lane == route_ref[:, 0:1], 1.0, 0.0)
    oh1 = jnp.where(lane == route_ref[:, 1:2], 1.0, 0.0)
    both = oh0 + oh1
    ri = lax.broadcasted_iota(jnp.int32, (tm, tm), 0)
    ci = lax.broadcasted_iota(jnp.int32, (tm, tm), 1)
    earlier = (ci < ri).astype(BF16)
    base = (jnp.dot(earlier, both.astype(BF16), preferred_element_type=F32)
            + carry_ref[...] + pstart_ref[...])
    d0 = jnp.sum(oh0 * base, axis=-1, keepdims=True)
    d1 = jnp.sum(oh1 * base, axis=-1, keepdims=True)
    dest = jnp.where(lane == 0, d0, jnp.where(lane == 1, d1, 0.0))
    dest_ref[...] = dest.T[0:2, :].astype(jnp.int32)
    carry_ref[...] += jnp.sum(both, axis=0, keepdims=True)


def _dispatch_plan(route, cnt, n_blocks):
    t = route.shape[0]
    tm = PLAN_TILE
    blk = MOE_BLK
    dest, meta = pl.pallas_call(
        _plan_kernel,
        grid=(t // tm,),
        in_specs=[pl.BlockSpec((tm, LANES), lambda i: (i, 0)),
                  pl.BlockSpec((1, LANES), lambda i: (0, 0))],
        out_specs=[pl.BlockSpec((2, tm), lambda i: (0, i)),
                   pl.BlockSpec((8, LANES), lambda i: (0, 0))],
        out_shape=[jax.ShapeDtypeStruct((2, t), jnp.int32),
                   jax.ShapeDtypeStruct((8, LANES), F32)],
        scratch_shapes=[pltpu.VMEM((1, LANES), F32), pltpu.VMEM((1, LANES), F32)],
        compiler_params=_cparams(("arbitrary",)),
        name="moe_plan",
    )(route, cnt)
    pad_end = meta[0, :N_EXPERTS].astype(jnp.int32)
    starts = jnp.arange(n_blocks, dtype=jnp.int32) * blk
    block_e = jnp.minimum(jnp.sum((pad_end[None, :] <= starts[:, None]).astype(jnp.int32), axis=1),
                          N_EXPERTS - 1)
    n_used = (pad_end[N_EXPERTS - 1] // blk).reshape(1)
    seg_end = (meta[1, :N_EXPERTS] + meta[2, :N_EXPERTS]).astype(jnp.int32)
    n_valid = jnp.clip(seg_end[block_e] - starts, 0, blk)
    return dest.reshape(2 * t), block_e, n_used, n_valid


def _ffn_kernel(be_ref, nu_ref, nv_ref, xs_ref, wg_ref, wu_ref, wd_ref, y_ref,
                wgb_ref, wub_ref, wdb_ref):
    i = pl.program_id(0)
    prev = be_ref[jnp.maximum(i - 1, 0)]

    @pl.when(jnp.logical_or(i == 0, be_ref[i] != prev))
    def _():
        wgb_ref[...] = wg_ref[...].astype(BF16)
        wub_ref[...] = wu_ref[...].astype(BF16)
        wdb_ref[...] = wd_ref[...].astype(BF16)

    @pl.when(i < nu_ref[0])
    def _():
        half = xs_ref.shape[0] // 2
        for r in range(2):
            rows = slice(r * half, (r + 1) * half)
            row = lax.broadcasted_iota(jnp.int32, (half, 1), 0) + r * half
            xp = jnp.where(row < nv_ref[i], xs_ref[rows, :], U32(0))
            x_hi, x_lo = _unpack_halves(xp)
            xb = jnp.concatenate([x_hi.astype(BF16), x_lo.astype(BF16)], axis=1)
            a = jnp.dot(xb, wgb_ref[...], preferred_element_type=F32)
            u = jnp.dot(xb, wub_ref[...], preferred_element_type=F32)
            h = (a * _sigmoid(a) * u).astype(BF16)
            y_ref[rows, :] = _pack_halves(jnp.dot(h, wdb_ref[...], preferred_element_type=F32))

    @pl.when(i >= nu_ref[0])
    def _():
        y_ref[...] = jnp.zeros_like(y_ref)


def _expert_ffn(xs, block_e, n_used, n_valid, w_gate, w_up, w_down, layer):
    n_slots = xs.shape[0]
    blk = MOE_BLK
    w_map = lambda i, be, nu, nv: (layer, be[i], 0, 0)
    grid_spec = pltpu.PrefetchScalarGridSpec(
        num_scalar_prefetch=3,
        grid=(n_slots // blk,),
        in_specs=[pl.BlockSpec((blk, D_MODEL // 2),
                               lambda i, be, nu, nv: (jnp.minimum(i, nu[0] - 1), 0)),
                  pl.BlockSpec((None, None, D_MODEL, D_EXPERT), w_map),
                  pl.BlockSpec((None, None, D_MODEL, D_EXPERT), w_map),
                  pl.BlockSpec((None, None, D_EXPERT, D_MODEL), w_map)],
        out_specs=pl.BlockSpec((blk, D_MODEL // 2), lambda i, be, nu, nv: (i, 0)),
        scratch_shapes=[pltpu.VMEM((D_MODEL, D_EXPERT), BF16),
                        pltpu.VMEM((D_MODEL, D_EXPERT), BF16),
                        pltpu.VMEM((D_EXPERT, D_MODEL), BF16)],
    )
    return pl.pallas_call(
        _ffn_kernel,
        grid_spec=grid_spec,
        out_shape=jax.ShapeDtypeStruct((n_slots, D_MODEL // 2), U32),
        compiler_params=_cparams(("arbitrary",)),
        name="expert_ffn",
    )(block_e, n_used, n_valid, xs, w_gate, w_up, w_down)


SC_CORES = 2
SC_SUBCORES = 16
SC_ROWS = 64


def _sc_gather_rows(table, idx):
    b = idx.shape[0]
    d = table.shape[1]
    per_w = b // (SC_CORES * SC_SUBCORES)
    mesh = plsc.VectorSubcoreMesh(core_axis_name="c", subcore_axis_name="s")

    n_chunks = per_w // SC_ROWS

    @functools.partial(
        pl.kernel, mesh=mesh,
        out_type=jax.ShapeDtypeStruct((b, d), table.dtype),
        scratch_types=[pltpu.VMEM((SC_ROWS,), jnp.int32), pltpu.VMEM((SC_ROWS,), jnp.int32),
                       pltpu.VMEM((SC_ROWS, d), table.dtype),
                       pltpu.VMEM((SC_ROWS, d), table.dtype),
                       pltpu.SemaphoreType.DMA, pltpu.SemaphoreType.DMA,
                       pltpu.SemaphoreType.DMA, pltpu.SemaphoreType.DMA],
        name="sc_gather_rows",
    )
    def gather(table_hbm, idx_hbm, out_hbm, idx0, idx1, rows0, rows1, gs0, gs1, ws0, ws1):
        idx_v, rows_v, gsem, wsem = (idx0, idx1), (rows0, rows1), (gs0, gs1), (ws0, ws1)
        wid = lax.axis_index("s") * SC_CORES + lax.axis_index("c")
        base = wid * per_w

        def rows_of(c):
            return pl.ds(pl.multiple_of(base + c * SC_ROWS, SC_ROWS), SC_ROWS)

        def start_gather(c, s):
            pltpu.sync_copy(idx_hbm.at[rows_of(c)], idx_v[s])
            pltpu.async_copy(table_hbm.at[idx_v[s]], rows_v[s], gsem[s])

        def write_back(c, s):
            pltpu.make_async_copy(table_hbm.at[idx_v[s]], rows_v[s], gsem[s]).wait()
            pltpu.async_copy(rows_v[s], out_hbm.at[rows_of(c)], wsem[s]).wait()

        start_gather(0, 0)

        @pl.loop(0, n_chunks, step=2)
        def _(c):
            start_gather(c + 1, 1)
            write_back(c, 0)

            @pl.when(c + 2 < n_chunks)
            def _():
                start_gather(c + 2, 0)

            write_back(c + 1, 1)

    return gather(table, idx)


def _sc_scatter_rows(x2d, idx, n_slots):
    t, d = x2d.shape
    per_w = t // (SC_CORES * SC_SUBCORES)
    mesh = plsc.VectorSubcoreMesh(core_axis_name="c", subcore_axis_name="s")

    @functools.partial(
        pl.kernel, mesh=mesh,
        out_type=jax.ShapeDtypeStruct((n_slots, d), x2d.dtype),
        scratch_types=[pltpu.VMEM((SC_ROWS,), jnp.int32), pltpu.VMEM((SC_ROWS,), jnp.int32),
                       pltpu.VMEM((SC_ROWS, d), x2d.dtype),
                       pltpu.SemaphoreType.DMA, pltpu.SemaphoreType.DMA],
        name="sc_scatter_rows",
    )
    def scatter(x_hbm, idx_hbm, out_hbm, idx0, idx1, rows_v, s0, s1):
        wid = lax.axis_index("s") * SC_CORES + lax.axis_index("c")
        base = wid * per_w

        @pl.loop(0, per_w // SC_ROWS)
        def _(c):
            off = pl.multiple_of(base + c * SC_ROWS, SC_ROWS)
            pltpu.sync_copy(x_hbm.at[pl.ds(off, SC_ROWS)], rows_v)
            pltpu.sync_copy(idx_hbm.at[pl.ds(off, SC_ROWS)], idx0)
            pltpu.sync_copy(idx_hbm.at[pl.ds(t + off, SC_ROWS)], idx1)
            cp0 = pltpu.async_copy(rows_v, out_hbm.at[idx0], s0)
            cp1 = pltpu.async_copy(rows_v, out_hbm.at[idx1], s1)
            cp0.wait()
            cp1.wait()

    return scatter(x2d, idx)


def _combine_dense_kernel(route_ref, x_ref, y0_ref, y1_ref, g_ref, b_ref, o_ref):
    y0 = jnp.concatenate(_unpack_halves(y0_ref[...]), axis=1)
    y1 = jnp.concatenate(_unpack_halves(y1_ref[...]), axis=1)
    moe = route_ref[:, 2:3] * y0 + route_ref[:, 3:4] * y1
    h = ALPHA * x_ref[...] + moe
    o_ref[...] = _layer_norm(h, g_ref[...], b_ref[...])


def _combine_dense(route, x2d, yg, ln_g, ln_b):
    t = x2d.shape[0]
    tm = 1024
    nt = t // tm
    row = lambda i: (i, 0)
    full = lambda i: (0, 0)
    return pl.pallas_call(
        _combine_dense_kernel,
        grid=(nt,),
        in_specs=[pl.BlockSpec((tm, LANES), row),
                  pl.BlockSpec((tm, D_MODEL), row),
                  pl.BlockSpec((tm, D_MODEL // 2), row),
                  pl.BlockSpec((tm, D_MODEL // 2), lambda i: (i + nt, 0)),
                  pl.BlockSpec((1, D_MODEL), full),
                  pl.BlockSpec((1, D_MODEL), full)],
        out_specs=pl.BlockSpec((tm, D_MODEL), row),
        out_shape=jax.ShapeDtypeStruct((t, D_MODEL), F32),
        compiler_params=_cparams(("parallel",)),
        name="moe_combine_dense",
    )(route, x2d, yg, yg, ln_g, ln_b)


def _pad_cols(w, n):
    return jnp.pad(w, [(0, 0)] * (w.ndim - 1) + [(0, n - w.shape[-1])])


def kernel(x, w_in, conv_w, gla_w_lr, gla_b_lr, gla_norm_g, ssd_conv_w, ssd_conv_b, ssd_a_log,
           ssd_d, ssd_dt_bias, ssd_norm_g, diff_lq1, diff_lk1, diff_lq2, diff_lk2, diff_norm_g,
           w_o, ln1_g, ln1_b, router_g, router_e, w_gate, w_up, w_down, ln2_g, ln2_b):
    bsz, seq, d = x.shape
    t = bsz * seq
    n_assign = 2 * t
    n_blocks = (n_assign + N_EXPERTS * (MOE_BLK - 1)) // MOE_BLK + 1
    n_slots = n_blocks * MOE_BLK
    x2d = x.reshape(t, d)
    w_in_r = jnp.concatenate([w_in[..., 0:768], _pad_cols(w_in[..., 768:1552], 896),
                              _pad_cols(w_in[..., 1552:2580], 1152), w_in[..., 2580:3348]],
                             axis=-1).astype(BF16)
    w_o_b = w_o.astype(BF16)
    for l in range(DEPTH):
        pc, pg, plr, ps, pdt, pd = [p.reshape(bsz, seq, -1) for p in _in_proj(x2d, w_in_r, l)]

        y_conv = _conv_mixer(pc, conv_w[l])
        w_lr_pad = jnp.pad(gla_w_lr[l], ((0, LANES - GLA_RANK), (0, 0)))
        y_gla = _gla_mixer(pg, plr, w_lr_pad, gla_b_lr[l].reshape(1, -1),
                           jnp.tile(gla_norm_g[l], GLA_HEADS).reshape(1, -1))
        pad4 = lambda v: jnp.pad(v, (0, LANES - SSD_HEADS)).reshape(1, LANES)
        y_ssd = _ssd_mixer(ps, pdt, ssd_conv_w[l], ssd_conv_b[l].reshape(1, -1),
                           pad4(ssd_a_log[l]), pad4(ssd_dt_bias[l]),
                           jnp.repeat(ssd_d[l], SSD_HEADDIM).reshape(1, -1),
                           ssd_norm_g[l].reshape(1, -1))
        lam_vecs = jnp.pad(jnp.stack([diff_lq1[l], diff_lk1[l], diff_lq2[l], diff_lk2[l]]),
                           ((0, 0), (0, LANES - DIFF_DQK)))
        lam_init = 0.8 - 0.6 * math.exp(-0.3 * l)
        y_diff = _diff_mixer(pd, lam_vecs,
                             jnp.tile(diff_norm_g[l], DIFF_HEADS).reshape(1, -1), lam_init)

        w_route = _pad_cols(jnp.concatenate(
            [router_g[l], router_e[l].reshape(d, N_EXPERTS)], axis=1), LANES)
        w_route_hi = w_route.astype(BF16)
        w_route = jnp.concatenate(
            [w_route_hi, (w_route - w_route_hi.astype(F32)).astype(BF16)], axis=1)
        ys = [y.reshape(t, W_MIX) for y in (y_conv, y_gla, y_ssd, y_diff)]
        xn, xn_p, route, cnt = _out_proj(ys, x2d, w_o_b, l, ln1_g[l].reshape(1, -1),
                                         ln1_b[l].reshape(1, -1), w_route)

        idx, block_e, n_used, n_valid = _dispatch_plan(route, cnt, n_blocks)
        xs = _sc_scatter_rows(xn_p, idx, n_slots)
        y = _expert_ffn(xs, block_e, n_used, n_valid, w_gate, w_up, w_down, l)
        yg = _sc_gather_rows(y, idx)
        x2d = _combine_dense(route, xn, yg, ln2_g[l].reshape(1, -1), ln2_b[l].reshape(1, -1))
    return x2d.reshape(bsz, seq, d)
```

```python
import functools
import math

import jax
import jax.numpy as jnp
from jax import lax
from jax.experimental import pallas as pl
from jax.experimental.pallas import tpu as pltpu
from jax.experimental.pallas import tpu_sc as plsc

F32 = jnp.float32
BF16 = jnp.bfloat16
HI = lax.Precision.HIGHEST

D_MODEL = 1024
DEPTH = 2
W_MIX = 256
GLA_HEADS, GLA_DK, GLA_DV, GLA_RANK, GLA_TAU, GLA_CHUNK = 4, 32, 64, 16, 16.0, 64
GLA_ROWS = 256
GLA_SEQS = 4
SSD_SEQS = 2
SSD_HEADS, SSD_GROUPS, SSD_HEADDIM, SSD_STATE, SSD_CONV_K, SSD_CHUNK = 4, 2, 64, 128, 4, 128
DIFF_HEADS, DIFF_DQK, DIFF_DV = 4, 32, 64
N_GROUPS, EXPERTS_PER_GROUP, N_EXPERTS, D_EXPERT = 4, 8, 32, 512
ALPHA = (2 * DEPTH) ** 0.25
LN_EPS = 1e-5
RMS_EPS = 1e-6

LANES = 128
SUBLANES = 8
PROJ_WIDTHS = (768, 768, 128, 1024, 128, 768)
PROJ_DTYPES = (BF16, BF16, F32, BF16, F32, BF16)
VMEM_LIMIT = 56 * 1024 * 1024

MOE_BLK = 512
TOK_TILE = 256


def _cparams(sem):
    return pltpu.CompilerParams(dimension_semantics=sem, vmem_limit_bytes=VMEM_LIMIT)


def _sigmoid(x):
    return 1.0 / (1.0 + jnp.exp(-x))


def _softplus(x):
    return jnp.maximum(x, 0.0) + jnp.log(1.0 + jnp.exp(-jnp.abs(x)))


def _layer_norm(h, g, b):
    mu = jnp.mean(h, axis=-1, keepdims=True)
    d = h - mu
    var = jnp.mean(d * d, axis=-1, keepdims=True)
    return d * lax.rsqrt(var + LN_EPS) * g + b


def _dot_nt(a, b):
    return lax.dot_general(a, b, (((1,), (1,)), ((), ())), preferred_element_type=F32)


def _dot_tn(a, b, precision=None):
    return lax.dot_general(a, b, (((0,), (0,)), ((), ())), preferred_element_type=F32,
                           precision=precision)


def _split_bf16(x, parts):
    out = []
    for _ in range(parts - 1):
        hi = x.astype(BF16)
        out.append(hi)
        x = x - hi.astype(F32)
    out.append(x.astype(BF16))
    return out


def _dot(a, b):
    return jnp.dot(a, b, preferred_element_type=F32)


U32 = jnp.uint32


def _pack_halves(x):
    w = x.shape[1] // 2
    hi = lax.bitcast_convert_type(x[:, :w].astype(BF16).astype(F32), U32)
    lo = lax.bitcast_convert_type(x[:, w:].astype(BF16).astype(F32), U32)
    return hi | lax.shift_right_logical(lo, U32(16))


def _unpack_halves(p):
    hi = lax.bitcast_convert_type(p & U32(0xFFFF0000), F32)
    lo = lax.bitcast_convert_type(lax.shift_left(p, U32(16)), F32)
    return hi, lo


def _dot_split_lhs(a, b_exact, parts, dot=_dot):
    acc = None
    for term in _split_bf16(a, parts):
        d = dot(term, b_exact)
        acc = d if acc is None else acc + d
    return acc


def _dot_split_rhs(a_exact, b, parts):
    acc = None
    for term in _split_bf16(b, parts):
        d = jnp.dot(a_exact, term, preferred_element_type=F32)
        acc = d if acc is None else acc + d
    return acc


def _proj_kernel(x_ref, w_ref, *o_refs):
    xb = x_ref[...].astype(BF16)
    off = 0
    for o_ref in o_refs:
        n = o_ref.shape[-1]
        o_ref[...] = jnp.dot(xb, w_ref[:, off:off + n],
                             preferred_element_type=F32).astype(o_ref.dtype)
        off += n


def _in_proj(x2d, w_r, layer):
    t = x2d.shape[0]
    tm = 1024
    ncol = sum(PROJ_WIDTHS)
    return pl.pallas_call(
        _proj_kernel,
        grid=(t // tm,),
        in_specs=[pl.BlockSpec((tm, D_MODEL), lambda i: (i, 0)),
                  pl.BlockSpec((None, D_MODEL, ncol), lambda i: (layer, 0, 0))],
        out_specs=[pl.BlockSpec((tm, n), lambda i: (i, 0)) for n in PROJ_WIDTHS],
        out_shape=[jax.ShapeDtypeStruct((t, n), dt) for n, dt in zip(PROJ_WIDTHS, PROJ_DTYPES)],
        compiler_params=_cparams(("parallel",)),
        name="in_proj",
    )(x2d, w_r)


def _conv_kernel(p_ref, w_ref, o_ref):
    u = p_ref[0, :, 0:W_MIX].astype(F32)
    gb = p_ref[0, :, W_MIX:2 * W_MIX].astype(F32)
    gc = p_ref[0, :, 2 * W_MIX:3 * W_MIX].astype(F32)
    cu = gc * u
    row = lax.broadcasted_iota(jnp.int32, cu.shape, 0)
    acc = cu * w_ref[2:3, :]
    for s in (1, 2):
        sh = jnp.where(row >= s, pltpu.roll(cu, s, axis=0), 0.0)
        acc = acc + sh * w_ref[2 - s:3 - s, :]
    o_ref[0] = (gb * acc).astype(o_ref.dtype)


def _conv_mixer(pc, conv_w):
    b, s, _ = pc.shape
    return pl.pallas_call(
        _conv_kernel,
        grid=(b,),
        in_specs=[pl.BlockSpec((1, s, 3 * W_MIX), lambda i: (i, 0, 0)),
                  pl.BlockSpec((3, W_MIX), lambda i: (0, 0))],
        out_specs=pl.BlockSpec((1, s, W_MIX), lambda i: (i, 0, 0)),
        out_shape=jax.ShapeDtypeStruct((b, s, W_MIX), BF16),
        compiler_params=_cparams(("parallel",)),
        name="conv_mixer",
    )(pc, conv_w)


def _gla_kernel(p_ref, lr_ref, wlr_ref, blr_ref, ng_ref, o_ref, st_ref):
    c = GLA_CHUNK
    s_len = p_ref.shape[1]
    nh, dk, dv = GLA_HEADS, GLA_DK, GLA_DV
    st_ref[...] = jnp.zeros_like(st_ref)

    rb = GLA_ROWS
    ncb = rb // c
    ri = lax.broadcasted_iota(jnp.int32, (rb, rb), 0)
    ci = lax.broadcasted_iota(jnp.int32, (rb, rb), 1)
    tri = (ci <= ri).astype(BF16)
    klane_head = lax.broadcasted_iota(jnp.int32, (1, nh * dk), 1) // dk
    vlane_head = lax.broadcasted_iota(jnp.int32, (1, nh * dv), 1) // dv
    strow_head = lax.broadcasted_iota(jnp.int32, (nh * dv, 1), 0) // dv
    st_mask = strow_head == klane_head
    r4 = lax.broadcasted_iota(jnp.int32, (nh * c, c), 0) % c
    c4 = lax.broadcasted_iota(jnp.int32, (nh * c, c), 1)
    causal4 = c4 <= r4
    gi = lax.broadcasted_iota(jnp.int32, (nh * dv, nh * dv), 0) // dv
    gj = lax.broadcasted_iota(jnp.int32, (nh * dv, nh * dv), 1) // dv
    gmean = jnp.where(gi == gj, 1.0 / dv, 0.0).astype(BF16)
    wlr_hi, wlr_lo = _split_bf16(wlr_ref[...], 2)

    def one_seq(bb, rows):
        q = p_ref[bb, rows, 0:128].astype(F32) * (dk ** -0.5)
        k = p_ref[bb, rows, 128:256].astype(F32)
        vb = p_ref[bb, rows, 256:512]
        g = p_ref[bb, rows, 512:768].astype(F32)
        lr = lr_ref[bb, rows, :]
        lr_hi, lr_lo = _split_bf16(lr, 2)
        z = (jnp.dot(lr_hi, wlr_hi, preferred_element_type=F32)
             + jnp.dot(lr_hi, wlr_lo, preferred_element_type=F32)
             + jnp.dot(lr_lo, wlr_hi, preferred_element_type=F32)) + blr_ref[...]
        log_a = (jnp.minimum(z, 0.0) - jnp.log(1.0 + jnp.exp(-jnp.abs(z)))) * (1.0 / GLA_TAU)
        cumb = _dot_split_rhs(tri, log_a, 3)
        ends = [cumb[(j + 1) * c - 1:(j + 1) * c, :] for j in range(ncb)]
        starts = [jnp.zeros_like(ends[0])] + ends[:-1]
        cum = cumb - jnp.concatenate([jnp.broadcast_to(s0, (c, nh * dk)) for s0 in starts], axis=0)
        lasts = [e - s0 for e, s0 in zip(ends, starts)]
        cl = jnp.concatenate([jnp.broadcast_to(x, (c, nh * dk)) for x in lasts], axis=0)
        q_dec = q * jnp.exp(cum)
        k_inv = (k * jnp.exp(-cum)).astype(BF16)
        k_end = (k * jnp.exp(cl - cum)).astype(BF16)
        st = st_ref[bb]
        outs = []
        for j in range(ncb):
            sl = slice(j * c, (j + 1) * c)
            qd = q_dec[sl]
            qs = jnp.concatenate([jnp.where(klane_head == h, qd, 0.0) for h in range(nh)],
                                 axis=0).astype(BF16)
            att = jnp.where(causal4, _dot_nt(qs, k_inv[sl]), 0.0)
            r = jnp.dot(att.astype(BF16), vb[sl], preferred_element_type=F32)
            o = jnp.where(vlane_head == 0, r[0:c], 0.0)
            for h in range(1, nh):
                o = o + jnp.where(vlane_head == h, r[h * c:(h + 1) * c], 0.0)
            outs.append(o + _dot_nt(qd.astype(BF16), st.astype(BF16)))
            d_st = _dot_tn(vb[sl], k_end[sl])
            st = st * jnp.exp(lasts[j]) + jnp.where(st_mask, d_st, 0.0)
        st_ref[bb] = st
        o = jnp.concatenate(outs, axis=0)
        ms = _dot_split_lhs(o * o, gmean, 2)
        o = o * lax.rsqrt(ms + RMS_EPS) * ng_ref[...]
        o_ref[bb, rows, :] = (o * (g * _sigmoid(g))).astype(o_ref.dtype)

    def body(n, carry):
        rows = pl.ds(pl.multiple_of(n * rb, rb), rb)
        for bb in range(p_ref.shape[0]):
            one_seq(bb, rows)
        return carry

    lax.fori_loop(0, s_len // rb, body, 0)


def _gla_mixer(pg, plr, w_lr_pad, b_lr, norm_g4):
    b, s, wp = pg.shape
    nb = GLA_SEQS
    return pl.pallas_call(
        _gla_kernel,
        grid=(b // nb,),
        in_specs=[pl.BlockSpec((nb, s, wp), lambda i: (i, 0, 0)),
                  pl.BlockSpec((nb, s, LANES), lambda i: (i, 0, 0)),
                  pl.BlockSpec((LANES, LANES), lambda i: (0, 0)),
                  pl.BlockSpec((1, LANES), lambda i: (0, 0)),
                  pl.BlockSpec((1, W_MIX), lambda i: (0, 0))],
        out_specs=pl.BlockSpec((nb, s, W_MIX), lambda i: (i, 0, 0)),
        out_shape=jax.ShapeDtypeStruct((b, s, W_MIX), BF16),
        scratch_shapes=[pltpu.VMEM((nb, GLA_HEADS * GLA_DV, GLA_HEADS * GLA_DK), F32)],
        compiler_params=_cparams(("parallel",)),
        name="gla_mixer",
    )(pg, plr, w_lr_pad, b_lr, norm_g4)


def _ssd_kernel(p_ref, dt_ref, cw_ref, cb_ref, alog_ref, dtb_ref, dsk_ref, ng_ref, o_ref, st_ref):
    c = SSD_CHUNK
    s_len = p_ref.shape[1]
    n_st = SSD_STATE
    st_ref[...] = jnp.zeros_like(st_ref)

    ri = lax.broadcasted_iota(jnp.int32, (c, c), 0)
    ci = lax.broadcasted_iota(jnp.int32, (c, c), 1)
    causal = ci <= ri
    tri = causal.astype(BF16)
    upper = (ri <= ci).astype(BF16)
    lane_head = lax.broadcasted_iota(jnp.int32, (1, W_MIX), 1) // SSD_HEADDIM
    lane_group = lane_head // (SSD_HEADS // SSD_GROUPS)
    eh = lax.broadcasted_iota(jnp.int32, (LANES, W_MIX), 0)
    el = lax.broadcasted_iota(jnp.int32, (LANES, W_MIX), 1) // SSD_HEADDIM
    expand = (eh == el).astype(BF16)
    row8 = lax.broadcasted_iota(jnp.int32, (8, 3 * W_MIX), 0)
    a_c = -jnp.exp(alog_ref[...])

    def one_chunk(n, bb):
        r0 = pl.multiple_of(n * c, c)
        rows = pl.ds(r0, c)
        cur = p_ref[bb, rows, 256:1024].astype(F32)
        p0 = pl.multiple_of(jnp.maximum(r0 - 2 * SUBLANES, 0), 2 * SUBLANES)
        prev8 = p_ref[bb, pl.ds(p0, 2 * SUBLANES), 256:1024].astype(F32)[SUBLANES:]
        prev8 = jnp.where(n > 0, prev8, 0.0)
        acc = cur * cw_ref[3:4, :] + cb_ref[...]
        for s in (1, 2, 3):
            sh = pltpu.roll(cur, s, axis=0)
            top = jnp.where(row8 < s, pltpu.roll(prev8, s, axis=0), sh[0:8])
            sh = jnp.concatenate([top, sh[8:]], axis=0)
            acc = acc + sh * cw_ref[3 - s:4 - s, :]
        xbc = acc * _sigmoid(acc)
        x = xbc[:, 0:256]
        bm = xbc[:, 256:512].astype(BF16)
        cm = xbc[:, 512:768].astype(BF16)

        dt_c = _softplus(dt_ref[bb, rows, :] + dtb_ref[...])
        da_c = dt_c * a_c
        cum_c = _dot_split_rhs(tri, da_c, 3)
        cum_r = _dot_split_lhs(da_c, upper, 3, dot=_dot_tn)
        both_x = _dot_split_lhs(jnp.concatenate([dt_c, cum_c], axis=0), expand, 3)
        dt_x = both_x[0:c]
        cum_x = both_x[c:2 * c]
        cl_x = cum_x[c - 1:c, :]
        x_dt = x * dt_x
        x_dt_b = x_dt.astype(BF16)
        xw_b = (x_dt * jnp.exp(cl_x - cum_x)).astype(BF16)

        y = x * dsk_ref[...]
        y_off = jnp.zeros((c, W_MIX), F32)
        for g in range(SSD_GROUPS):
            bg = bm[:, g * n_st:(g + 1) * n_st]
            cg = cm[:, g * n_st:(g + 1) * n_st]
            cb = _dot_nt(cg, bg)
            for r in range(SSD_HEADS // SSD_GROUPS):
                h = g * (SSD_HEADS // SSD_GROUPS) + r
                diff = cum_c[:, h:h + 1] - cum_r[h:h + 1, :]
                dec = jnp.exp(jnp.where(causal, diff, -jnp.inf))
                m = (cb * dec).astype(BF16)
                yh = jnp.dot(m, x_dt_b, preferred_element_type=F32)
                y = y + jnp.where(lane_head == h, yh, 0.0)
            st = st_ref[bb, g]
            y_off = y_off + jnp.where(lane_group == g,
                                      jnp.dot(cg, st.astype(BF16), preferred_element_type=F32), 0.0)
            d_st = _dot_tn(bg, xw_b)
            st_ref[bb, g] = st * jnp.exp(cl_x) + jnp.where(lane_group == g, d_st, 0.0)
        y = y + y_off * jnp.exp(cum_x)
        zg = p_ref[bb, rows, 0:256].astype(F32)
        y = y * (zg * _sigmoid(zg))
        outs = []
        for g in range(SSD_GROUPS):
            yg = y[:, g * 128:(g + 1) * 128]
            ms = jnp.mean(yg * yg, axis=-1, keepdims=True)
            outs.append(yg * lax.rsqrt(ms + RMS_EPS))
        o_ref[bb, rows, :] = (jnp.concatenate(outs, axis=-1) * ng_ref[...]).astype(o_ref.dtype)

    def body(n, carry):
        for bb in range(p_ref.shape[0]):
            one_chunk(n, bb)
        return carry

    lax.fori_loop(0, s_len // c, body, 0)


def _ssd_mixer(ps, pdt, conv_w, conv_b, a_log_c, dt_bias_c, d_x, norm_g):
    b, s, wp = ps.shape
    nb = SSD_SEQS
    full2 = lambda i: (0, 0)
    return pl.pallas_call(
        _ssd_kernel,
        grid=(b // nb,),
        in_specs=[pl.BlockSpec((nb, s, wp), lambda i: (i, 0, 0)),
                  pl.BlockSpec((nb, s, LANES), lambda i: (i, 0, 0)),
                  pl.BlockSpec((SSD_CONV_K, 3 * W_MIX), full2),
                  pl.BlockSpec((1, 3 * W_MIX), full2),
                  pl.BlockSpec((1, LANES), full2),
                  pl.BlockSpec((1, LANES), full2),
                  pl.BlockSpec((1, W_MIX), full2),
                  pl.BlockSpec((1, W_MIX), full2)],
        out_specs=pl.BlockSpec((nb, s, W_MIX), lambda i: (i, 0, 0)),
        out_shape=jax.ShapeDtypeStruct((b, s, W_MIX), BF16),
        scratch_shapes=[pltpu.VMEM((nb, SSD_GROUPS, SSD_STATE, W_MIX), F32)],
        compiler_params=_cparams(("parallel",)),
        name="ssd_mixer",
    )(ps, pdt, conv_w, conv_b, a_log_c, dt_bias_c, d_x, norm_g)


DIFF_TQ = 256
DIFF_TK = 256
LOG2E = 1.4426950408889634
DIFF_VPAD = DIFF_DV + 16


def _diff_kernel(q_ref, k_ref, v_ref, lam_ref, ng_ref, o_ref,
                 vt_ref, qs_ref, st_ref, m_ref, acc_ref, *, lam_init):
    tq, tk = DIFF_TQ, DIFF_TK
    nh, dv = DIFF_HEADS, DIFF_DV
    nhc = 2 * nh
    s_len = k_ref.shape[1]
    i = pl.program_id(1)

    @pl.when(i == 0)
    def _():
        for cblk in range(s_len // tk):
            cols = slice(cblk * tk, (cblk + 1) * tk)
            vt = v_ref[0, cols, :].astype(F32).T.astype(BF16)
            for h in range(nh):
                vt_ref[h, 0:dv, cols] = vt[h * dv:(h + 1) * dv]
        vt_ref[:, dv:, :] = jnp.ones((nh, DIFF_VPAD - dv, s_len), BF16)

    q = q_ref[0].astype(F32) * (DIFF_DQK ** -0.5 * LOG2E)
    qlane = lax.broadcasted_iota(jnp.int32, (1, W_MIX), 1) // DIFF_DQK
    for hc in range(nhc):
        qs_ref[hc * tq:(hc + 1) * tq, :] = jnp.where(qlane == hc, q, 0.0).astype(BF16)
    m_ref[...] = jnp.full_like(m_ref, -jnp.inf)
    acc_ref[...] = jnp.zeros_like(acc_ref)
    krow = lax.broadcasted_iota(jnp.int32, (tk, nhc * tq), 0)
    qcol = lax.broadcasted_iota(jnp.int32, (tk, nhc * tq), 1) % tq
    diag_ok = krow <= qcol

    def scores(j, slot):
        k0 = pl.multiple_of(j * tk, tk)
        st_ref[slot] = _dot_nt(k_ref[0, pl.ds(k0, tk), :], qs_ref[...])

    def softmax_pv(j, slot, masked):
        k0 = pl.multiple_of(j * tk, tk)
        st = st_ref[slot]
        if masked:
            st = jnp.where(diag_ok, st, -jnp.inf)
        m_prev = m_ref[...]
        m_new = jnp.maximum(m_prev, jnp.max(st, axis=0, keepdims=True))
        alpha = jnp.exp2(m_prev - m_new)
        p = jnp.exp2(st - m_new)
        m_ref[...] = m_new
        pb = p.astype(BF16)
        for hc in range(nhc):
            h = hc // 2
            lanes = slice(hc * tq, (hc + 1) * tq)
            pv = jnp.dot(vt_ref[h, :, pl.ds(k0, tk)], pb[:, lanes],
                         preferred_element_type=F32)
            acc_ref[hc] = acc_ref[hc] * alpha[:, lanes] + pv

    scores(0, 0)
    n_pairs = i // 2

    def pair_step(u, carry):
        scores(2 * u + 1, 1)
        softmax_pv(2 * u, 0, False)
        scores(2 * u + 2, 0)
        softmax_pv(2 * u + 1, 1, False)
        return carry

    lax.fori_loop(0, n_pairs, pair_step, 0)

    @pl.when(i % 2 == 0)
    def _():
        softmax_pv(i, 0, True)

    @pl.when(i % 2 == 1)
    def _():
        scores(i, 1)
        softmax_pv(i - 1, 0, False)
        softmax_pv(i, 1, True)

    lam = (jnp.exp(jnp.sum(lam_ref[0:1, :] * lam_ref[1:2, :], axis=-1, keepdims=True))
           - jnp.exp(jnp.sum(lam_ref[2:3, :] * lam_ref[3:4, :], axis=-1, keepdims=True))
           + lam_init)
    heads = []
    for h in range(nh):
        o1 = acc_ref[2 * h, 0:dv] / acc_ref[2 * h, dv:dv + 1]
        o2 = acc_ref[2 * h + 1, 0:dv] / acc_ref[2 * h + 1, dv:dv + 1]
        oh = o1 - lam * o2
        ms = jnp.mean(oh * oh, axis=0, keepdims=True)
        heads.append(oh * lax.rsqrt(ms + RMS_EPS))
    o = jnp.concatenate(heads, axis=0).T
    o_ref[0] = (o * ng_ref[...] * (1.0 - lam_init)).astype(o_ref.dtype)


def _diff_mixer(pd, lam_vecs, norm_g4, lam_init):
    b, s, _ = pd.shape
    tq = DIFF_TQ
    return pl.pallas_call(
        functools.partial(_diff_kernel, lam_init=lam_init),
        grid=(b, s // tq),
        in_specs=[pl.BlockSpec((1, tq, W_MIX), lambda bi, i: (bi, i, 0)),
                  pl.BlockSpec((1, s, W_MIX), lambda bi, i: (bi, 0, 1)),
                  pl.BlockSpec((1, s, W_MIX), lambda bi, i: (bi, 0, 2)),
                  pl.BlockSpec((4, LANES), lambda bi, i: (0, 0)),
                  pl.BlockSpec((1, W_MIX), lambda bi, i: (0, 0))],
        out_specs=pl.BlockSpec((1, tq, W_MIX), lambda bi, i: (bi, i, 0)),
        out_shape=jax.ShapeDtypeStruct((b, s, W_MIX), BF16),
        scratch_shapes=[pltpu.VMEM((DIFF_HEADS, DIFF_VPAD, s), BF16),
                        pltpu.VMEM((2 * DIFF_HEADS * tq, W_MIX), BF16),
                        pltpu.VMEM((2, DIFF_TK, 2 * DIFF_HEADS * tq), F32),
                        pltpu.VMEM((1, 2 * DIFF_HEADS * tq), F32),
                        pltpu.VMEM((2 * DIFF_HEADS, DIFF_VPAD, tq), F32)],
        compiler_params=_cparams(("parallel", "arbitrary")),
        name="diff_attn",
    )(pd, pd, pd, lam_vecs, norm_g4)


def _oproj_kernel(yc_ref, yg_ref, ys_ref, yd_ref, x_ref, wo_ref, g_ref, b_ref, wr_ref,
                  xo_ref, xp_ref, route_ref, cnt_ref):
    mix = jnp.concatenate([yc_ref[...], yg_ref[...], ys_ref[...], yd_ref[...]], axis=-1)
    h = ALPHA * x_ref[...] + jnp.dot(mix, wo_ref[...], preferred_element_type=F32)
    xn = _layer_norm(h, g_ref[...], b_ref[...])
    xo_ref[...] = xn
    xp_ref[...] = _pack_halves(xn)

    xn_hi, xn_lo = _split_bf16(xn, 2)
    both = _dot(xn_hi, wr_ref[...])
    logits = both[:, 0:LANES] + both[:, LANES:2 * LANES] + _dot(xn_lo, wr_ref[:, 0:LANES])
    lane = lax.broadcasted_iota(jnp.int32, logits.shape, 1).astype(F32)
    neg = -jnp.inf
    big = float(LANES)
    lg = jnp.where(lane < N_GROUPS, logits, neg)
    mg = jnp.max(lg, axis=-1, keepdims=True)
    sg = jnp.sum(jnp.exp(lg - mg), axis=-1, keepdims=True)
    grp = jnp.min(jnp.where(lg == mg, lane, big), axis=-1, keepdims=True)
    p_grp = 1.0 / sg
    lo = N_GROUPS + EXPERTS_PER_GROUP * grp
    in_g = jnp.logical_and(lane >= lo, lane < lo + EXPERTS_PER_GROUP)
    le = jnp.where(in_g, logits, neg)
    me = jnp.max(le, axis=-1, keepdims=True)
    ee = jnp.exp(le - me)
    pe = ee / jnp.sum(ee, axis=-1, keepdims=True)
    pe = jnp.where(in_g, pe, -1.0)
    p1 = jnp.max(pe, axis=-1, keepdims=True)
    i1 = jnp.min(jnp.where(pe == p1, lane, big), axis=-1, keepdims=True)
    pe2 = jnp.where(lane == i1, -1.0, pe)
    p2 = jnp.max(pe2, axis=-1, keepdims=True)
    i2 = jnp.min(jnp.where(pe2 == p2, lane, big), axis=-1, keepdims=True)
    den = p1 + p2
    g1 = p_grp * p1 / den
    g2 = p_grp * p2 / den
    e1 = i1 - N_GROUPS
    e2 = i2 - N_GROUPS
    route_ref[...] = jnp.where(lane == 0, e1, jnp.where(lane == 1, e2, jnp.where(
        lane == 2, g1, jnp.where(lane == 3, g2, 0.0))))

    @pl.when(pl.program_id(0) == 0)
    def _():
        cnt_ref[...] = jnp.zeros_like(cnt_ref)

    hits = jnp.where(lane == e1, 1.0, 0.0) + jnp.where(lane == e2, 1.0, 0.0)
    cnt_ref[...] += jnp.sum(hits, axis=0, keepdims=True)


def _out_proj(ys, x2d, w_o, layer, ln_g, ln_b, w_route):
    t = x2d.shape[0]
    tm = 1024
    row = lambda i: (i, 0)
    full = lambda i: (0, 0)
    return pl.pallas_call(
        _oproj_kernel,
        grid=(t // tm,),
        in_specs=[pl.BlockSpec((tm, W_MIX), row)] * 4 + [
            pl.BlockSpec((tm, D_MODEL), row),
            pl.BlockSpec((None, D_MODEL, D_MODEL), lambda i: (layer, 0, 0)),
            pl.BlockSpec((1, D_MODEL), full),
            pl.BlockSpec((1, D_MODEL), full),
            pl.BlockSpec((D_MODEL, 2 * LANES), full)],
        out_specs=[pl.BlockSpec((tm, D_MODEL), row), pl.BlockSpec((tm, D_MODEL // 2), row),
                   pl.BlockSpec((tm, LANES), row), pl.BlockSpec((1, LANES), full)],
        out_shape=[jax.ShapeDtypeStruct((t, D_MODEL), F32),
                   jax.ShapeDtypeStruct((t, D_MODEL // 2), U32),
                   jax.ShapeDtypeStruct((t, LANES), F32),
                   jax.ShapeDtypeStruct((1, LANES), F32)],
        compiler_params=_cparams(("arbitrary",)),
        name="out_proj_ln_router",
    )(*ys, x2d, w_o, ln_g, ln_b, w_route)


PLAN_TILE = 512


def _plan_kernel(route_ref, cnt_ref, dest_ref, meta_ref, carry_ref, pstart_ref):
    tm = route_ref.shape[0]
    lane = lax.broadcasted_iota(jnp.int32, (1, LANES), 1).astype(F32)

    @pl.when(pl.program_id(0) == 0)
    def _():
        cnt = cnt_ref[...]
        padded = jnp.ceil(cnt * (1.0 / MOE_BLK)) * MOE_BLK
        li = lax.broadcasted_iota(jnp.int32, (LANES, LANES), 0)
        lj = lax.broadcasted_iota(jnp.int32, (LANES, LANES), 1)
        before = (li < lj).astype(F32)
        pstart = jnp.dot(jnp.broadcast_to(padded, (8, LANES)), before, precision=HI,
                         preferred_element_type=F32)[0:1]
        pstart_ref[...] = pstart
        carry_ref[...] = jnp.zeros_like(carry_ref)
        meta_ref[...] = jnp.concatenate(
            [pstart + padded, pstart, cnt, jnp.zeros((5, LANES), F32)], axis=0)

    oh0 = jnp.where(lane == route_ref[:, 0:1], 1.0, 0.0)
    oh1 = jnp.where(lane == route_ref[:, 1:2], 1.0, 0.0)
    both = oh0 + oh1
    ri = lax.broadcasted_iota(jnp.int32, (tm, tm), 0)
    ci = lax.broadcasted_iota(jnp.int32, (tm, tm), 1)
    earlier = (ci < ri).astype(BF16)
    base = (jnp.dot(earlier, both.astype(BF16), preferred_element_type=F32)
            + carry_ref[...] + pstart_ref[...])
    d0 = jnp.sum(oh0 * base, axis=-1, keepdims=True)
    d1 = jnp.sum(oh1 * base, axis=-1, keepdims=True)
    dest = jnp.where(lane == 0, d0, jnp.where(lane == 1, d1, 0.0))
    dest_ref[...] = dest.T[0:2, :].astype(jnp.int32)
    carry_ref[...] += jnp.sum(both, axis=0, keepdims=True)


def _dispatch_plan(route, cnt, n_blocks):
    t = route.shape[0]
    tm = PLAN_TILE
    blk = MOE_BLK
    dest, meta = pl.pallas_call(
        _plan_kernel,
        grid=(t // tm,),
        in_specs=[pl.BlockSpec((tm, LANES), lambda i: (i, 0)),
                  pl.BlockSpec((1, LANES), lambda i: (0, 0))],
        out_specs=[pl.BlockSpec((2, tm), lambda i: (0, i)),
                   pl.BlockSpec((8, LANES), lambda i: (0, 0))],
        out_shape=[jax.ShapeDtypeStruct((2, t), jnp.int32),
                   jax.ShapeDtypeStruct((8, LANES), F32)],
        scratch_shapes=[pltpu.VMEM((1, LANES), F32), pltpu.VMEM((1, LANES), F32)],
        compiler_params=_cparams(("arbitrary",)),
        name="moe_plan",
    )(route, cnt)
    meta_i = meta[:, :N_EXPERTS].astype(jnp.int32)
    pad_end, pad_start, seg_end = meta_i[0], meta_i[1], meta_i[1] + meta_i[2]
    starts = jnp.arange(n_blocks, dtype=jnp.int32)[:, None] * blk
    member = jnp.logical_and(starts >= pad_start[None, :], starts < pad_end[None, :])
    expert_ids = jnp.arange(N_EXPERTS, dtype=jnp.int32)[None, :]
    block_e = jnp.where(jnp.any(member, axis=1), jnp.sum(jnp.where(member, expert_ids, 0), axis=1),
                        N_EXPERTS - 1)
    n_used = (pad_end[N_EXPERTS - 1] // blk).reshape(1)
    n_valid = jnp.clip(jnp.sum(jnp.where(member, seg_end[None, :], 0), axis=1) - starts[:, 0],
                       0, blk)
    return dest.reshape(2 * t), block_e, n_used, n_valid


def _ffn_kernel(be_ref, nu_ref, nv_ref, xs_ref, wg_ref, wu_ref, wd_ref, y_ref,
                wgb_ref, wub_ref, wdb_ref):
    i = pl.program_id(0)
    prev = be_ref[jnp.maximum(i - 1, 0)]

    @pl.when(jnp.logical_or(i == 0, be_ref[i] != prev))
    def _():
        wgb_ref[...] = wg_ref[...].astype(BF16)
        wub_ref[...] = wu_ref[...].astype(BF16)
        wdb_ref[...] = wd_ref[...].astype(BF16)

    @pl.when(i < nu_ref[0])
    def _():
        half = xs_ref.shape[0] // 2
        for r in range(2):
            rows = slice(r * half, (r + 1) * half)
            row = lax.broadcasted_iota(jnp.int32, (half, 1), 0) + r * half
            xp = jnp.where(row < nv_ref[i], xs_ref[rows, :], U32(0))
            x_hi, x_lo = _unpack_halves(xp)
            xb = jnp.concatenate([x_hi.astype(BF16), x_lo.astype(BF16)], axis=1)
            a = jnp.dot(xb, wgb_ref[...], preferred_element_type=F32)
            u = jnp.dot(xb, wub_ref[...], preferred_element_type=F32)
            h = (a * _sigmoid(a) * u).astype(BF16)
            y_ref[rows, :] = _pack_halves(jnp.dot(h, wdb_ref[...], preferred_element_type=F32))

    @pl.when(i >= nu_ref[0])
    def _():
        y_ref[...] = jnp.zeros_like(y_ref)


def _expert_ffn(xs, block_e, n_used, n_valid, w_gate, w_up, w_down, layer):
    n_slots = xs.shape[0]
    blk = MOE_BLK
    w_map = lambda i, be, nu, nv: (layer, be[i], 0, 0)
    grid_spec = pltpu.PrefetchScalarGridSpec(
        num_scalar_prefetch=3,
        grid=(n_slots // blk,),
        in_specs=[pl.BlockSpec((blk, D_MODEL // 2),
                               lambda i, be, nu, nv: (jnp.minimum(i, nu[0] - 1), 0)),
                  pl.BlockSpec((None, None, D_MODEL, D_EXPERT), w_map),
                  pl.BlockSpec((None, None, D_MODEL, D_EXPERT), w_map),
                  pl.BlockSpec((None, None, D_EXPERT, D_MODEL), w_map)],
        out_specs=pl.BlockSpec((blk, D_MODEL // 2), lambda i, be, nu, nv: (i, 0)),
        scratch_shapes=[pltpu.VMEM((D_MODEL, D_EXPERT), BF16),
                        pltpu.VMEM((D_MODEL, D_EXPERT), BF16),
                        pltpu.VMEM((D_EXPERT, D_MODEL), BF16)],
    )
    return pl.pallas_call(
        _ffn_kernel,
        grid_spec=grid_spec,
        out_shape=jax.ShapeDtypeStruct((n_slots, D_MODEL // 2), U32),
        compiler_params=_cparams(("arbitrary",)),
        name="expert_ffn",
    )(block_e, n_used, n_valid, xs, w_gate, w_up, w_down)


SC_CORES = 2
SC_SUBCORES = 16
SC_ROWS = 64


def _sc_gather_rows(table, idx):
    b = idx.shape[0]
    d = table.shape[1]
    per_w = b // (SC_CORES * SC_SUBCORES)
    mesh = plsc.VectorSubcoreMesh(core_axis_name="c", subcore_axis_name="s")

    n_chunks = per_w // SC_ROWS

    @functools.partial(
        pl.kernel, mesh=mesh,
        out_type=jax.ShapeDtypeStruct((b, d), table.dtype),
        scratch_types=[pltpu.VMEM((SC_ROWS,), jnp.int32), pltpu.VMEM((SC_ROWS,), jnp.int32),
                       pltpu.VMEM((SC_ROWS, d), table.dtype),
                       pltpu.VMEM((SC_ROWS, d), table.dtype),
                       pltpu.SemaphoreType.DMA, pltpu.SemaphoreType.DMA,
                       pltpu.SemaphoreType.DMA, pltpu.SemaphoreType.DMA],
        name="sc_gather_rows",
    )
    def gather(table_hbm, idx_hbm, out_hbm, idx0, idx1, rows0, rows1, gs0, gs1, ws0, ws1):
        idx_v, rows_v, gsem, wsem = (idx0, idx1), (rows0, rows1), (gs0, gs1), (ws0, ws1)
        wid = lax.axis_index("s") * SC_CORES + lax.axis_index("c")
        base = wid * per_w

        def rows_of(c):
            return pl.ds(pl.multiple_of(base + c * SC_ROWS, SC_ROWS), SC_ROWS)

        def start_gather(c, s):
            pltpu.sync_copy(idx_hbm.at[rows_of(c)], idx_v[s])
            pltpu.async_copy(table_hbm.at[idx_v[s]], rows_v[s], gsem[s])

        def write_back(c, s):
            pltpu.make_async_copy(table_hbm.at[idx_v[s]], rows_v[s], gsem[s]).wait()
            pltpu.async_copy(rows_v[s], out_hbm.at[rows_of(c)], wsem[s]).wait()

        start_gather(0, 0)

        @pl.loop(0, n_chunks, step=2)
        def _(c):
            start_gather(c + 1, 1)
            write_back(c, 0)

            @pl.when(c + 2 < n_chunks)
            def _():
                start_gather(c + 2, 0)

            write_back(c + 1, 1)

    return gather(table, idx)


def _sc_scatter_rows(x2d, idx, n_slots):
    t, d = x2d.shape
    per_w = t // (SC_CORES * SC_SUBCORES)
    mesh = plsc.VectorSubcoreMesh(core_axis_name="c", subcore_axis_name="s")

    @functools.partial(
        pl.kernel, mesh=mesh,
        out_type=jax.ShapeDtypeStruct((n_slots, d), x2d.dtype),
        scratch_types=[pltpu.VMEM((SC_ROWS,), jnp.int32), pltpu.VMEM((SC_ROWS,), jnp.int32),
                       pltpu.VMEM((SC_ROWS, d), x2d.dtype),
                       pltpu.SemaphoreType.DMA, pltpu.SemaphoreType.DMA],
        name="sc_scatter_rows",
    )
    def scatter(x_hbm, idx_hbm, out_hbm, idx0, idx1, rows_v, s0, s1):
        wid = lax.axis_index("s") * SC_CORES + lax.axis_index("c")
        base = wid * per_w

        @pl.loop(0, per_w // SC_ROWS)
        def _(c):
            off = pl.multiple_of(base + c * SC_ROWS, SC_ROWS)
            pltpu.sync_copy(x_hbm.at[pl.ds(off, SC_ROWS)], rows_v)
            pltpu.sync_copy(idx_hbm.at[pl.ds(off, SC_ROWS)], idx0)
            pltpu.sync_copy(idx_hbm.at[pl.ds(t + off, SC_ROWS)], idx1)
            cp0 = pltpu.async_copy(rows_v, out_hbm.at[idx0], s0)
            cp1 = pltpu.async_copy(rows_v, out_hbm.at[idx1], s1)
            cp0.wait()
            cp1.wait()

    return scatter(x2d, idx)


def _combine_dense_kernel(route_ref, x_ref, y0_ref, y1_ref, g_ref, b_ref, o_ref):
    y0 = jnp.concatenate(_unpack_halves(y0_ref[...]), axis=1)
    y1 = jnp.concatenate(_unpack_halves(y1_ref[...]), axis=1)
    moe = route_ref[:, 2:3] * y0 + route_ref[:, 3:4] * y1
    h = ALPHA * x_ref[...] + moe
    o_ref[...] = _layer_norm(h, g_ref[...], b_ref[...])


def _combine_dense(route, x2d, yg, ln_g, ln_b):
    t = x2d.shape[0]
    tm = 1024
    nt = t // tm
    row = lambda i: (i, 0)
    full = lambda i: (0, 0)
    return pl.pallas_call(
        _combine_dense_kernel,
        grid=(nt,),
        in_specs=[pl.BlockSpec((tm, LANES), row),
                  pl.BlockSpec((tm, D_MODEL), row),
                  pl.BlockSpec((tm, D_MODEL // 2), row),
                  pl.BlockSpec((tm, D_MODEL // 2), lambda i: (i + nt, 0)),
                  pl.BlockSpec((1, D_MODEL), full),
                  pl.BlockSpec((1, D_MODEL), full)],
        out_specs=pl.BlockSpec((tm, D_MODEL), row),
        out_shape=jax.ShapeDtypeStruct((t, D_MODEL), F32),
        compiler_params=_cparams(("parallel",)),
        name="moe_combine_dense",
    )(route, x2d, yg, yg, ln_g, ln_b)


def _pad_cols(w, n):
    return jnp.pad(w, [(0, 0)] * (w.ndim - 1) + [(0, n - w.shape[-1])])


def kernel(x, w_in, conv_w, gla_w_lr, gla_b_lr, gla_norm_g, ssd_conv_w, ssd_conv_b, ssd_a_log,
           ssd_d, ssd_dt_bias, ssd_norm_g, diff_lq1, diff_lk1, diff_lq2, diff_lk2, diff_norm_g,
           w_o, ln1_g, ln1_b, router_g, router_e, w_gate, w_up, w_down, ln2_g, ln2_b):
    bsz, seq, d = x.shape
    t = bsz * seq
    n_assign = 2 * t
    n_blocks = (n_assign + N_EXPERTS * (MOE_BLK - 1)) // MOE_BLK + 1
    n_slots = n_blocks * MOE_BLK
    x2d = x.reshape(t, d)
    w_in_r = jnp.concatenate([w_in[..., 0:768], _pad_cols(w_in[..., 768:1552], 896),
                              _pad_cols(w_in[..., 1552:2580], 1152), w_in[..., 2580:3348]],
                             axis=-1).astype(BF16)
    w_o_b = w_o.astype(BF16)
    for l in range(DEPTH):
        pc, pg, plr, ps, pdt, pd = [p.reshape(bsz, seq, -1) for p in _in_proj(x2d, w_in_r, l)]

        y_conv = _conv_mixer(pc, conv_w[l])
        w_lr_pad = jnp.pad(gla_w_lr[l], ((0, LANES - GLA_RANK), (0, 0)))
        y_gla = _gla_mixer(pg, plr, w_lr_pad, gla_b_lr[l].reshape(1, -1),
                           jnp.tile(gla_norm_g[l], GLA_HEADS).reshape(1, -1))
        pad4 = lambda v: jnp.pad(v, (0, LANES - SSD_HEADS)).reshape(1, LANES)
        y_ssd = _ssd_mixer(ps, pdt, ssd_conv_w[l], ssd_conv_b[l].reshape(1, -1),
                           pad4(ssd_a_log[l]), pad4(ssd_dt_bias[l]),
                           jnp.repeat(ssd_d[l], SSD_HEADDIM).reshape(1, -1),
                           ssd_norm_g[l].reshape(1, -1))
        lam_vecs = jnp.pad(jnp.stack([diff_lq1[l], diff_lk1[l], diff_lq2[l], diff_lk2[l]]),
                           ((0, 0), (0, LANES - DIFF_DQK)))
        lam_init = 0.8 - 0.6 * math.exp(-0.3 * l)
        y_diff = _diff_mixer(pd, lam_vecs,
                             jnp.tile(diff_norm_g[l], DIFF_HEADS).reshape(1, -1), lam_init)

        w_route = _pad_cols(jnp.concatenate(
            [router_g[l], router_e[l].reshape(d, N_EXPERTS)], axis=1), LANES)
        w_route_hi = w_route.astype(BF16)
        w_route = jnp.concatenate(
            [w_route_hi, (w_route - w_route_hi.astype(F32)).astype(BF16)], axis=1)
        ys = [y.reshape(t, W_MIX) for y in (y_conv, y_gla, y_ssd, y_diff)]
        xn, xn_p, route, cnt = _out_proj(ys, x2d, w_o_b, l, ln1_g[l].reshape(1, -1),
                                         ln1_b[l].reshape(1, -1), w_route)

        idx, block_e, n_used, n_valid = _dispatch_plan(route, cnt, n_blocks)
        xs = _sc_scatter_rows(xn_p, idx, n_slots)
        y = _expert_ffn(xs, block_e, n_used, n_valid, w_gate, w_up, w_down, l)
        yg = _sc_gather_rows(y, idx)
        x2d = _combine_dense(route, xn, yg, ln2_g[l].reshape(1, -1), ln2_b[l].reshape(1, -1))
    return x2d.reshape(bsz, seq, d)
```

```python
import functools
import math

import jax
import jax.numpy as jnp
from jax import lax
from jax.experimental import pallas as pl
from jax.experimental.pallas import tpu as pltpu
from jax.experimental.pallas import tpu_sc as plsc

F32 = jnp.float32
BF16 = jnp.bfloat16
HI = lax.Precision.HIGHEST

D_MODEL = 1024
DEPTH = 2
W_MIX = 256
GLA_HEADS, GLA_DK, GLA_DV, GLA_RANK, GLA_TAU, GLA_CHUNK = 4, 32, 64, 16, 16.0, 64
GLA_ROWS = 256
GLA_SEQS = 4
SSD_SEQS = 2
SSD_HEADS, SSD_GROUPS, SSD_HEADDIM, SSD_STATE, SSD_CONV_K, SSD_CHUNK = 4, 2, 64, 128, 4, 128
DIFF_HEADS, DIFF_DQK, DIFF_DV = 4, 32, 64
N_GROUPS, EXPERTS_PER_GROUP, N_EXPERTS, D_EXPERT = 4, 8, 32, 512
ALPHA = (2 * DEPTH) ** 0.25
LN_EPS = 1e-5
RMS_EPS = 1e-6

LANES = 128
SUBLANES = 8
PROJ_WIDTHS = (768, 768, 128, 1024, 128, 768)
PROJ_DTYPES = (BF16, BF16, F32, BF16, F32, BF16)
VMEM_LIMIT = 56 * 1024 * 1024

MOE_BLK = 512
TOK_TILE = 256


def _cparams(sem):
    return pltpu.CompilerParams(dimension_semantics=sem, vmem_limit_bytes=VMEM_LIMIT)


def _sigmoid(x):
    return 1.0 / (1.0 + jnp.exp(-x))


def _softplus(x):
    return jnp.maximum(x, 0.0) + jnp.log(1.0 + jnp.exp(-jnp.abs(x)))


def _layer_norm(h, g, b):
    mu = jnp.mean(h, axis=-1, keepdims=True)
    d = h - mu
    var = jnp.mean(d * d, axis=-1, keepdims=True)
    return d * lax.rsqrt(var + LN_EPS) * g + b


def _dot_nt(a, b):
    return lax.dot_general(a, b, (((1,), (1,)), ((), ())), preferred_element_type=F32)


def _dot_tn(a, b, precision=None):
    return lax.dot_general(a, b, (((0,), (0,)), ((), ())), preferred_element_type=F32,
                           precision=precision)


def _split_bf16(x, parts):
    out = []
    for _ in range(parts - 1):
        hi = x.astype(BF16)
        out.append(hi)
        x = x - hi.astype(F32)
    out.append(x.astype(BF16))
    return out


def _dot(a, b):
    return jnp.dot(a, b, preferred_element_type=F32)


U32 = jnp.uint32


def _pack_halves(x):
    w = x.shape[1] // 2
    hi = lax.bitcast_convert_type(x[:, :w].astype(BF16).astype(F32), U32)
    lo = lax.bitcast_convert_type(x[:, w:].astype(BF16).astype(F32), U32)
    return hi | lax.shift_right_logical(lo, U32(16))


def _unpack_halves(p):
    hi = lax.bitcast_convert_type(p & U32(0xFFFF0000), F32)
    lo = lax.bitcast_convert_type(lax.shift_left(p, U32(16)), F32)
    return hi, lo


def _dot_split_lhs(a, b_exact, parts, dot=_dot):
    acc = None
    for term in _split_bf16(a, parts):
        d = dot(term, b_exact)
        acc = d if acc is None else acc + d
    return acc


def _dot_split_rhs(a_exact, b, parts):
    acc = None
    for term in _split_bf16(b, parts):
        d = jnp.dot(a_exact, term, preferred_element_type=F32)
        acc = d if acc is None else acc + d
    return acc


def _proj_kernel(x_ref, w_ref, *o_refs):
    xb = x_ref[...].astype(BF16)
    off = 0
    for o_ref in o_refs:
        n = o_ref.shape[-1]
        o_ref[...] = jnp.dot(xb, w_ref[:, off:off + n],
                             preferred_element_type=F32).astype(o_ref.dtype)
        off += n


def _in_proj(x2d, w_r, layer):
    t = x2d.shape[0]
    tm = 1024
    ncol = sum(PROJ_WIDTHS)
    return pl.pallas_call(
        _proj_kernel,
        grid=(t // tm,),
        in_specs=[pl.BlockSpec((tm, D_MODEL), lambda i: (i, 0)),
                  pl.BlockSpec((None, D_MODEL, ncol), lambda i: (layer, 0, 0))],
        out_specs=[pl.BlockSpec((tm, n), lambda i: (i, 0)) for n in PROJ_WIDTHS],
        out_shape=[jax.ShapeDtypeStruct((t, n), dt) for n, dt in zip(PROJ_WIDTHS, PROJ_DTYPES)],
        compiler_params=_cparams(("parallel",)),
        name="in_proj",
    )(x2d, w_r)


def _conv_kernel(p_ref, w_ref, o_ref):
    u = p_ref[0, :, 0:W_MIX].astype(F32)
    gb = p_ref[0, :, W_MIX:2 * W_MIX].astype(F32)
    gc = p_ref[0, :, 2 * W_MIX:3 * W_MIX].astype(F32)
    cu = gc * u
    row = lax.broadcasted_iota(jnp.int32, cu.shape, 0)
    acc = cu * w_ref[2:3, :]
    for s in (1, 2):
        sh = jnp.where(row >= s, pltpu.roll(cu, s, axis=0), 0.0)
        acc = acc + sh * w_ref[2 - s:3 - s, :]
    o_ref[0] = (gb * acc).astype(o_ref.dtype)


def _conv_mixer(pc, conv_w):
    b, s, _ = pc.shape
    return pl.pallas_call(
        _conv_kernel,
        grid=(b,),
        in_specs=[pl.BlockSpec((1, s, 3 * W_MIX), lambda i: (i, 0, 0)),
                  pl.BlockSpec((3, W_MIX), lambda i: (0, 0))],
        out_specs=pl.BlockSpec((1, s, W_MIX), lambda i: (i, 0, 0)),
        out_shape=jax.ShapeDtypeStruct((b, s, W_MIX), BF16),
        compiler_params=_cparams(("parallel",)),
        name="conv_mixer",
    )(pc, conv_w)


def _gla_kernel(p_ref, lr_ref, wlr_ref, blr_ref, ng_ref, o_ref, st_ref):
    c = GLA_CHUNK
    s_len = p_ref.shape[1]
    nh, dk, dv = GLA_HEADS, GLA_DK, GLA_DV
    st_ref[...] = jnp.zeros_like(st_ref)

    rb = GLA_ROWS
    ncb = rb // c
    ri = lax.broadcasted_iota(jnp.int32, (rb, rb), 0)
    ci = lax.broadcasted_iota(jnp.int32, (rb, rb), 1)
    tri = (ci <= ri).astype(BF16)
    klane_head = lax.broadcasted_iota(jnp.int32, (1, nh * dk), 1) // dk
    vlane_head = lax.broadcasted_iota(jnp.int32, (1, nh * dv), 1) // dv
    strow_head = lax.broadcasted_iota(jnp.int32, (nh * dv, 1), 0) // dv
    st_mask = strow_head == klane_head
    r4 = lax.broadcasted_iota(jnp.int32, (nh * c, c), 0) % c
    c4 = lax.broadcasted_iota(jnp.int32, (nh * c, c), 1)
    causal4 = c4 <= r4
    gi = lax.broadcasted_iota(jnp.int32, (nh * dv, nh * dv), 0) // dv
    gj = lax.broadcasted_iota(jnp.int32, (nh * dv, nh * dv), 1) // dv
    gmean = jnp.where(gi == gj, 1.0 / dv, 0.0).astype(BF16)
    wlr_hi, wlr_lo = _split_bf16(wlr_ref[...], 2)

    def one_seq(bb, rows):
        q = p_ref[bb, rows, 0:128].astype(F32) * (dk ** -0.5)
        k = p_ref[bb, rows, 128:256].astype(F32)
        vb = p_ref[bb, rows, 256:512]
        g = p_ref[bb, rows, 512:768].astype(F32)
        lr = lr_ref[bb, rows, :]
        lr_hi, lr_lo = _split_bf16(lr, 2)
        z = (jnp.dot(lr_hi, wlr_hi, preferred_element_type=F32)
             + jnp.dot(lr_hi, wlr_lo, preferred_element_type=F32)
             + jnp.dot(lr_lo, wlr_hi, preferred_element_type=F32)) + blr_ref[...]
        log_a = (jnp.minimum(z, 0.0) - jnp.log(1.0 + jnp.exp(-jnp.abs(z)))) * (1.0 / GLA_TAU)
        cumb = _dot_split_rhs(tri, log_a, 3)
        ends = [cumb[(j + 1) * c - 1:(j + 1) * c, :] for j in range(ncb)]
        starts = [jnp.zeros_like(ends[0])] + ends[:-1]
        cum = cumb - jnp.concatenate([jnp.broadcast_to(s0, (c, nh * dk)) for s0 in starts], axis=0)
        lasts = [e - s0 for e, s0 in zip(ends, starts)]
        cl = jnp.concatenate([jnp.broadcast_to(x, (c, nh * dk)) for x in lasts], axis=0)
        q_dec = q * jnp.exp(cum)
        k_inv = (k * jnp.exp(-cum)).astype(BF16)
        k_end = (k * jnp.exp(cl - cum)).astype(BF16)
        st = st_ref[bb]
        outs = []
        for j in range(ncb):
            sl = slice(j * c, (j + 1) * c)
            qd = q_dec[sl]
            qs = jnp.concatenate([jnp.where(klane_head == h, qd, 0.0) for h in range(nh)],
                                 axis=0).astype(BF16)
            att = jnp.where(causal4, _dot_nt(qs, k_inv[sl]), 0.0)
            r = jnp.dot(att.astype(BF16), vb[sl], preferred_element_type=F32)
            o = jnp.where(vlane_head == 0, r[0:c], 0.0)
            for h in range(1, nh):
                o = o + jnp.where(vlane_head == h, r[h * c:(h + 1) * c], 0.0)
            outs.append(o + _dot_nt(qd.astype(BF16), st.astype(BF16)))
            d_st = _dot_tn(vb[sl], k_end[sl])
            st = st * jnp.exp(lasts[j]) + jnp.where(st_mask, d_st, 0.0)
        st_ref[bb] = st
        o = jnp.concatenate(outs, axis=0)
        ms = _dot_split_lhs(o * o, gmean, 2)
        o = o * lax.rsqrt(ms + RMS_EPS) * ng_ref[...]
        o_ref[bb, rows, :] = (o * (g * _sigmoid(g))).astype(o_ref.dtype)

    def body(n, carry):
        rows = pl.ds(pl.multiple_of(n * rb, rb), rb)
        for bb in range(p_ref.shape[0]):
            one_seq(bb, rows)
        return carry

    lax.fori_loop(0, s_len // rb, body, 0)


def _gla_mixer(pg, plr, w_lr_pad, b_lr, norm_g4):
    b, s, wp = pg.shape
    nb = GLA_SEQS
    return pl.pallas_call(
        _gla_kernel,
        grid=(b // nb,),
        in_specs=[pl.BlockSpec((nb, s, wp), lambda i: (i, 0, 0)),
                  pl.BlockSpec((nb, s, LANES), lambda i: (i, 0, 0)),
                  pl.BlockSpec((LANES, LANES), lambda i: (0, 0)),
                  pl.BlockSpec((1, LANES), lambda i: (0, 0)),
                  pl.BlockSpec((1, W_MIX), lambda i: (0, 0))],
        out_specs=pl.BlockSpec((nb, s, W_MIX), lambda i: (i, 0, 0)),
        out_shape=jax.ShapeDtypeStruct((b, s, W_MIX), BF16),
        scratch_shapes=[pltpu.VMEM((nb, GLA_HEADS * GLA_DV, GLA_HEADS * GLA_DK), F32)],
        compiler_params=_cparams(("parallel",)),
        name="gla_mixer",
    )(pg, plr, w_lr_pad, b_lr, norm_g4)


def _ssd_kernel(p_ref, dt_ref, cw_ref, cb_ref, alog_ref, dtb_ref, dsk_ref, ng_ref, o_ref, st_ref):
    c = SSD_CHUNK
    s_len = p_ref.shape[1]
    n_st = SSD_STATE
    st_ref[...] = jnp.zeros_like(st_ref)

    ri = lax.broadcasted_iota(jnp.int32, (c, c), 0)
    ci = lax.broadcasted_iota(jnp.int32, (c, c), 1)
    causal = ci <= ri
    tri = causal.astype(BF16)
    upper = (ri <= ci).astype(BF16)
    lane_head = lax.broadcasted_iota(jnp.int32, (1, W_MIX), 1) // SSD_HEADDIM
    lane_group = lane_head // (SSD_HEADS // SSD_GROUPS)
    eh = lax.broadcasted_iota(jnp.int32, (LANES, W_MIX), 0)
    el = lax.broadcasted_iota(jnp.int32, (LANES, W_MIX), 1) // SSD_HEADDIM
    expand = (eh == el).astype(BF16)
    row8 = lax.broadcasted_iota(jnp.int32, (8, 3 * W_MIX), 0)
    a_c = -jnp.exp(alog_ref[...])

    def one_chunk(n, bb):
        r0 = pl.multiple_of(n * c, c)
        rows = pl.ds(r0, c)
        cur = p_ref[bb, rows, 256:1024].astype(F32)
        p0 = pl.multiple_of(jnp.maximum(r0 - 2 * SUBLANES, 0), 2 * SUBLANES)
        prev8 = p_ref[bb, pl.ds(p0, 2 * SUBLANES), 256:1024].astype(F32)[SUBLANES:]
        prev8 = jnp.where(n > 0, prev8, 0.0)
        acc = cur * cw_ref[3:4, :] + cb_ref[...]
        for s in (1, 2, 3):
            sh = pltpu.roll(cur, s, axis=0)
            top = jnp.where(row8 < s, pltpu.roll(prev8, s, axis=0), sh[0:8])
            sh = jnp.concatenate([top, sh[8:]], axis=0)
            acc = acc + sh * cw_ref[3 - s:4 - s, :]
        xbc = acc * _sigmoid(acc)
        x = xbc[:, 0:256]
        bm = xbc[:, 256:512].astype(BF16)
        cm = xbc[:, 512:768].astype(BF16)

        dt_c = _softplus(dt_ref[bb, rows, :] + dtb_ref[...])
        da_c = dt_c * a_c
        cum_c = _dot_split_rhs(tri, da_c, 3)
        cum_r = _dot_split_lhs(da_c, upper, 3, dot=_dot_tn)
        both_x = _dot_split_lhs(jnp.concatenate([dt_c, cum_c], axis=0), expand, 3)
        dt_x = both_x[0:c]
        cum_x = both_x[c:2 * c]
        cl_x = cum_x[c - 1:c, :]
        x_dt = x * dt_x
        x_dt_b = x_dt.astype(BF16)
        xw_b = (x_dt * jnp.exp(cl_x - cum_x)).astype(BF16)

        y = x * dsk_ref[...]
        y_off = jnp.zeros((c, W_MIX), F32)
        for g in range(SSD_GROUPS):
            bg = bm[:, g * n_st:(g + 1) * n_st]
            cg = cm[:, g * n_st:(g + 1) * n_st]
            cb = _dot_nt(cg, bg)
            for r in range(SSD_HEADS // SSD_GROUPS):
                h = g * (SSD_HEADS // SSD_GROUPS) + r
                diff = cum_c[:, h:h + 1] - cum_r[h:h + 1, :]
                dec = jnp.exp(jnp.where(causal, diff, -jnp.inf))
                m = (cb * dec).astype(BF16)
                yh = jnp.dot(m, x_dt_b, preferred_element_type=F32)
                y = y + jnp.where(lane_head == h, yh, 0.0)
            st = st_ref[bb, g]
            y_off = y_off + jnp.where(lane_group == g,
                                      jnp.dot(cg, st.astype(BF16), preferred_element_type=F32), 0.0)
            d_st = _dot_tn(bg, xw_b)
            st_ref[bb, g] = st * jnp.exp(cl_x) + jnp.where(lane_group == g, d_st, 0.0)
        y = y + y_off * jnp.exp(cum_x)
        zg = p_ref[bb, rows, 0:256].astype(F32)
        y = y * (zg * _sigmoid(zg))
        outs = []
        for g in range(SSD_GROUPS):
            yg = y[:, g * 128:(g + 1) * 128]
            ms = jnp.mean(yg * yg, axis=-1, keepdims=True)
            outs.append(yg * lax.rsqrt(ms + RMS_EPS))
        o_ref[bb, rows, :] = (jnp.concatenate(outs, axis=-1) * ng_ref[...]).astype(o_ref.dtype)

    def body(n, carry):
        for bb in range(p_ref.shape[0]):
            one_chunk(n, bb)
        return carry

    lax.fori_loop(0, s_len // c, body, 0)


def _ssd_mixer(ps, pdt, conv_w, conv_b, a_log_c, dt_bias_c, d_x, norm_g):
    b, s, wp = ps.shape
    nb = SSD_SEQS
    full2 = lambda i: (0, 0)
    return pl.pallas_call(
        _ssd_kernel,
        grid=(b // nb,),
        in_specs=[pl.BlockSpec((nb, s, wp), lambda i: (i, 0, 0)),
                  pl.BlockSpec((nb, s, LANES), lambda i: (i, 0, 0)),
                  pl.BlockSpec((SSD_CONV_K, 3 * W_MIX), full2),
                  pl.BlockSpec((1, 3 * W_MIX), full2),
                  pl.BlockSpec((1, LANES), full2),
                  pl.BlockSpec((1, LANES), full2),
                  pl.BlockSpec((1, W_MIX), full2),
                  pl.BlockSpec((1, W_MIX), full2)],
        out_specs=pl.BlockSpec((nb, s, W_MIX), lambda i: (i, 0, 0)),
        out_shape=jax.ShapeDtypeStruct((b, s, W_MIX), BF16),
        scratch_shapes=[pltpu.VMEM((nb, SSD_GROUPS, SSD_STATE, W_MIX), F32)],
        compiler_params=_cparams(("parallel",)),
        name="ssd_mixer",
    )(ps, pdt, conv_w, conv_b, a_log_c, dt_bias_c, d_x, norm_g)


DIFF_TQ = 256
DIFF_TK = 256
LOG2E = 1.4426950408889634
DIFF_VPAD = DIFF_DV + 16


def _diff_kernel(q_ref, k_ref, v_ref, lam_ref, ng_ref, o_ref,
                 vt_ref, qs_ref, st_ref, m_ref, acc_ref, *, lam_init):
    tq, tk = DIFF_TQ, DIFF_TK
    nh, dv = DIFF_HEADS, DIFF_DV
    nhc = 2 * nh
    s_len = k_ref.shape[1]
    i = pl.program_id(1)

    @pl.when(i == 0)
    def _():
        for cblk in range(s_len // tk):
            cols = slice(cblk * tk, (cblk + 1) * tk)
            vt = v_ref[0, cols, :].astype(F32).T.astype(BF16)
            for h in range(nh):
                vt_ref[h, 0:dv, cols] = vt[h * dv:(h + 1) * dv]
        vt_ref[:, dv:, :] = jnp.ones((nh, DIFF_VPAD - dv, s_len), BF16)

    q = q_ref[0].astype(F32) * (DIFF_DQK ** -0.5 * LOG2E)
    qlane = lax.broadcasted_iota(jnp.int32, (1, W_MIX), 1) // DIFF_DQK
    for hc in range(nhc):
        qs_ref[hc * tq:(hc + 1) * tq, :] = jnp.where(qlane == hc, q, 0.0).astype(BF16)
    m_ref[...] = jnp.full_like(m_ref, -jnp.inf)
    acc_ref[...] = jnp.zeros_like(acc_ref)
    krow = lax.broadcasted_iota(jnp.int32, (tk, nhc * tq), 0)
    qcol = lax.broadcasted_iota(jnp.int32, (tk, nhc * tq), 1) % tq
    diag_ok = krow <= qcol

    def scores(j, slot):
        k0 = pl.multiple_of(j * tk, tk)
        st_ref[slot, :, 0:nhc * tq] = _dot_nt(k_ref[0, pl.ds(k0, tk), :], qs_ref[...])

    def softmax_pv(j, slot, masked):
        k0 = pl.multiple_of(j * tk, tk)
        st = st_ref[slot, :, 0:nhc * tq]
        if masked:
            st = jnp.where(diag_ok, st, -jnp.inf)
        m_prev = m_ref[...]
        m_new = jnp.maximum(m_prev, jnp.max(st, axis=0, keepdims=True))
        alpha = jnp.exp2(m_prev - m_new)
        p = jnp.exp2(st - m_new)
        m_ref[...] = m_new
        pb = p.astype(BF16)
        for hc in range(nhc):
            h = hc // 2
            lanes = slice(hc * tq, (hc + 1) * tq)
            pv = jnp.dot(vt_ref[h, :, pl.ds(k0, tk)], pb[:, lanes],
                         preferred_element_type=F32)
            acc_ref[hc] = acc_ref[hc] * alpha[:, lanes] + pv

    scores(0, 0)
    n_pairs = i // 2

    def pair_step(u, carry):
        scores(2 * u + 1, 1)
        softmax_pv(2 * u, 0, False)
        scores(2 * u + 2, 0)
        softmax_pv(2 * u + 1, 1, False)
        return carry

    lax.fori_loop(0, n_pairs, pair_step, 0)

    @pl.when(i % 2 == 0)
    def _():
        softmax_pv(i, 0, True)

    @pl.when(i % 2 == 1)
    def _():
        scores(i, 1)
        softmax_pv(i - 1, 0, False)
        softmax_pv(i, 1, True)

    lam = (jnp.exp(jnp.sum(lam_ref[0:1, :] * lam_ref[1:2, :], axis=-1, keepdims=True))
           - jnp.exp(jnp.sum(lam_ref[2:3, :] * lam_ref[3:4, :], axis=-1, keepdims=True))
           + lam_init)
    heads = []
    for h in range(nh):
        o1 = acc_ref[2 * h, 0:dv] / acc_ref[2 * h, dv:dv + 1]
        o2 = acc_ref[2 * h + 1, 0:dv] / acc_ref[2 * h + 1, dv:dv + 1]
        oh = o1 - lam * o2
        ms = jnp.mean(oh * oh, axis=0, keepdims=True)
        heads.append(oh * lax.rsqrt(ms + RMS_EPS))
    o = jnp.concatenate(heads, axis=0).T
    o_ref[0] = (o * ng_ref[...] * (1.0 - lam_init)).astype(o_ref.dtype)


def _diff_mixer(pd, lam_vecs, norm_g4, lam_init):
    b, s, _ = pd.shape
    tq = DIFF_TQ
    return pl.pallas_call(
        functools.partial(_diff_kernel, lam_init=lam_init),
        grid=(b, s // tq),
        in_specs=[pl.BlockSpec((1, tq, W_MIX), lambda bi, i: (bi, i, 0)),
                  pl.BlockSpec((1, s, W_MIX), lambda bi, i: (bi, 0, 1)),
                  pl.BlockSpec((1, s, W_MIX), lambda bi, i: (bi, 0, 2)),
                  pl.BlockSpec((4, LANES), lambda bi, i: (0, 0)),
                  pl.BlockSpec((1, W_MIX), lambda bi, i: (0, 0))],
        out_specs=pl.BlockSpec((1, tq, W_MIX), lambda bi, i: (bi, i, 0)),
        out_shape=jax.ShapeDtypeStruct((b, s, W_MIX), BF16),
        scratch_shapes=[pltpu.VMEM((DIFF_HEADS, DIFF_VPAD, s), BF16),
                        pltpu.VMEM((2 * DIFF_HEADS * tq, W_MIX), BF16),
                        pltpu.VMEM((2, DIFF_TK, 2 * DIFF_HEADS * tq + LANES), F32),
                        pltpu.VMEM((1, 2 * DIFF_HEADS * tq), F32),
                        pltpu.VMEM((2 * DIFF_HEADS, DIFF_VPAD, tq), F32)],
        compiler_params=_cparams(("parallel", "arbitrary")),
        name="diff_attn",
    )(pd, pd, pd, lam_vecs, norm_g4)


def _oproj_kernel(yc_ref, yg_ref, ys_ref, yd_ref, x_ref, wo_ref, g_ref, b_ref, wr_ref,
                  xo_ref, xp_ref, route_ref, cnt_ref):
    mix = jnp.concatenate([yc_ref[...], yg_ref[...], ys_ref[...], yd_ref[...]], axis=-1)
    h = ALPHA * x_ref[...] + jnp.dot(mix, wo_ref[...], preferred_element_type=F32)
    xn = _layer_norm(h, g_ref[...], b_ref[...])
    xo_ref[...] = xn
    xp_ref[...] = _pack_halves(xn)

    xn_hi, xn_lo = _split_bf16(xn, 2)
    both = _dot(xn_hi, wr_ref[...])
    logits = both[:, 0:LANES] + both[:, LANES:2 * LANES] + _dot(xn_lo, wr_ref[:, 0:LANES])
    lane = lax.broadcasted_iota(jnp.int32, logits.shape, 1).astype(F32)
    neg = -jnp.inf
    big = float(LANES)
    lg = jnp.where(lane < N_GROUPS, logits, neg)
    mg = jnp.max(lg, axis=-1, keepdims=True)
    sg = jnp.sum(jnp.exp(lg - mg), axis=-1, keepdims=True)
    grp = jnp.min(jnp.where(lg == mg, lane, big), axis=-1, keepdims=True)
    p_grp = 1.0 / sg
    lo = N_GROUPS + EXPERTS_PER_GROUP * grp
    in_g = jnp.logical_and(lane >= lo, lane < lo + EXPERTS_PER_GROUP)
    le = jnp.where(in_g, logits, neg)
    me = jnp.max(le, axis=-1, keepdims=True)
    ee = jnp.exp(le - me)
    pe = ee / jnp.sum(ee, axis=-1, keepdims=True)
    pe = jnp.where(in_g, pe, -1.0)
    p1 = jnp.max(pe, axis=-1, keepdims=True)
    i1 = jnp.min(jnp.where(pe == p1, lane, big), axis=-1, keepdims=True)
    pe2 = jnp.where(lane == i1, -1.0, pe)
    p2 = jnp.max(pe2, axis=-1, keepdims=True)
    i2 = jnp.min(jnp.where(pe2 == p2, lane, big), axis=-1, keepdims=True)
    den = p1 + p2
    g1 = p_grp * p1 / den
    g2 = p_grp * p2 / den
    e1 = i1 - N_GROUPS
    e2 = i2 - N_GROUPS
    route_ref[...] = jnp.where(lane == 0, e1, jnp.where(lane == 1, e2, jnp.where(
        lane == 2, g1, jnp.where(lane == 3, g2, 0.0))))

    @pl.when(pl.program_id(0) == 0)
    def _():
        cnt_ref[...] = jnp.zeros_like(cnt_ref)

    hits = jnp.where(lane == e1, 1.0, 0.0) + jnp.where(lane == e2, 1.0, 0.0)
    cnt_ref[...] += jnp.sum(hits, axis=0, keepdims=True)


def _out_proj(ys, x2d, w_o, layer, ln_g, ln_b, w_route):
    t = x2d.shape[0]
    tm = 1024
    row = lambda i: (i, 0)
    full = lambda i: (0, 0)
    return pl.pallas_call(
        _oproj_kernel,
        grid=(t // tm,),
        in_specs=[pl.BlockSpec((tm, W_MIX), row)] * 4 + [
            pl.BlockSpec((tm, D_MODEL), row),
            pl.BlockSpec((None, D_MODEL, D_MODEL), lambda i: (layer, 0, 0)),
            pl.BlockSpec((1, D_MODEL), full),
            pl.BlockSpec((1, D_MODEL), full),
            pl.BlockSpec((D_MODEL, 2 * LANES), full)],
        out_specs=[pl.BlockSpec((tm, D_MODEL), row), pl.BlockSpec((tm, D_MODEL // 2), row),
                   pl.BlockSpec((tm, LANES), row), pl.BlockSpec((1, LANES), full)],
        out_shape=[jax.ShapeDtypeStruct((t, D_MODEL), F32),
                   jax.ShapeDtypeStruct((t, D_MODEL // 2), U32),
                   jax.ShapeDtypeStruct((t, LANES), F32),
                   jax.ShapeDtypeStruct((1, LANES), F32)],
        compiler_params=_cparams(("arbitrary",)),
        name="out_proj_ln_router",
    )(*ys, x2d, w_o, ln_g, ln_b, w_route)


PLAN_TILE = 512


def _plan_kernel(route_ref, cnt_ref, dest_ref, meta_ref, carry_ref, pstart_ref):
    tm = route_ref.shape[0]
    lane = lax.broadcasted_iota(jnp.int32, (1, LANES), 1).astype(F32)

    @pl.when(pl.program_id(0) == 0)
    def _():
        cnt = cnt_ref[...]
        padded = jnp.ceil(cnt * (1.0 / MOE_BLK)) * MOE_BLK
        li = lax.broadcasted_iota(jnp.int32, (LANES, LANES), 0)
        lj = lax.broadcasted_iota(jnp.int32, (LANES, LANES), 1)
        before = (li < lj).astype(F32)
        pstart = jnp.dot(jnp.broadcast_to(padded, (8, LANES)), before, precision=HI,
                         preferred_element_type=F32)[0:1]
        pstart_ref[...] = pstart
        carry_ref[...] = jnp.zeros_like(carry_ref)
        meta_ref[...] = jnp.concatenate(
            [pstart + padded, pstart, cnt, jnp.zeros((5, LANES), F32)], axis=0)

    oh0 = jnp.where(lane == route_ref[:, 0:1], 1.0, 0.0)
    oh1 = jnp.where(lane == route_ref[:, 1:2], 1.0, 0.0)
    both = oh0 + oh1
    ri = lax.broadcasted_iota(jnp.int32, (tm, tm), 0)
    ci = lax.broadcasted_iota(jnp.int32, (tm, tm), 1)
    earlier = (ci < ri).astype(BF16)
    base = (jnp.dot(earlier, both.astype(BF16), preferred_element_type=F32)
            + carry_ref[...] + pstart_ref[...])
    d0 = jnp.sum(oh0 * base, axis=-1, keepdims=True)
    d1 = jnp.sum(oh1 * base, axis=-1, keepdims=True)
    dest = jnp.where(lane == 0, d0, jnp.where(lane == 1, d1, 0.0))
    dest_ref[...] = dest.T[0:2, :].astype(jnp.int32)
    carry_ref[...] += jnp.sum(both, axis=0, keepdims=True)


def _dispatch_plan(route, cnt, n_blocks):
    t = route.shape[0]
    tm = PLAN_TILE
    blk = MOE_BLK
    dest, meta = pl.pallas_call(
        _plan_kernel,
        grid=(t // tm,),
        in_specs=[pl.BlockSpec((tm, LANES), lambda i: (i, 0)),
                  pl.BlockSpec((1, LANES), lambda i: (0, 0))],
        out_specs=[pl.BlockSpec((2, tm), lambda i: (0, i)),
                   pl.BlockSpec((8, LANES), lambda i: (0, 0))],
        out_shape=[jax.ShapeDtypeStruct((2, t), jnp.int32),
                   jax.ShapeDtypeStruct((8, LANES), F32)],
        scratch_shapes=[pltpu.VMEM((1, LANES), F32), pltpu.VMEM((1, LANES), F32)],
        compiler_params=_cparams(("arbitrary",)),
        name="moe_plan",
    )(route, cnt)
    meta_i = meta[:, :N_EXPERTS].astype(jnp.int32)
    pad_end, pad_start, seg_end = meta_i[0], meta_i[1], meta_i[1] + meta_i[2]
    starts = jnp.arange(n_blocks, dtype=jnp.int32)[:, None] * blk
    member = jnp.logical_and(starts >= pad_start[None, :], starts < pad_end[None, :])
    expert_ids = jnp.arange(N_EXPERTS, dtype=jnp.int32)[None, :]
    block_e = jnp.where(jnp.any(member, axis=1), jnp.sum(jnp.where(member, expert_ids, 0), axis=1),
                        N_EXPERTS - 1)
    n_used = (pad_end[N_EXPERTS - 1] // blk).reshape(1)
    n_valid = jnp.clip(jnp.sum(jnp.where(member, seg_end[None, :], 0), axis=1) - starts[:, 0],
                       0, blk)
    return dest.reshape(2 * t), block_e, n_used, n_valid


def _ffn_kernel(be_ref, nu_ref, nv_ref, xs_ref, wg_ref, wu_ref, wd_ref, y_ref,
                wgb_ref, wub_ref, wdb_ref):
    i = pl.program_id(0)
    prev = be_ref[jnp.maximum(i - 1, 0)]

    @pl.when(jnp.logical_or(i == 0, be_ref[i] != prev))
    def _():
        wgb_ref[...] = wg_ref[...].astype(BF16)
        wub_ref[...] = wu_ref[...].astype(BF16)
        wdb_ref[...] = wd_ref[...].astype(BF16)

    @pl.when(i < nu_ref[0])
    def _():
        half = xs_ref.shape[0] // 2
        for r in range(2):
            rows = slice(r * half, (r + 1) * half)
            row = lax.broadcasted_iota(jnp.int32, (half, 1), 0) + r * half
            xp = jnp.where(row < nv_ref[i], xs_ref[rows, :], U32(0))
            x_hi, x_lo = _unpack_halves(xp)
            xb = jnp.concatenate([x_hi.astype(BF16), x_lo.astype(BF16)], axis=1)
            a = jnp.dot(xb, wgb_ref[...], preferred_element_type=F32)
            u = jnp.dot(xb, wub_ref[...], preferred_element_type=F32)
            h = (a * _sigmoid(a) * u).astype(BF16)
            y_ref[rows, :] = _pack_halves(jnp.dot(h, wdb_ref[...], preferred_element_type=F32))

    @pl.when(i >= nu_ref[0])
    def _():
        y_ref[...] = jnp.zeros_like(y_ref)


def _expert_ffn(xs, block_e, n_used, n_valid, w_gate, w_up, w_down, layer):
    n_slots = xs.shape[0]
    blk = MOE_BLK
    w_map = lambda i, be, nu, nv: (layer, be[i], 0, 0)
    grid_spec = pltpu.PrefetchScalarGridSpec(
        num_scalar_prefetch=3,
        grid=(n_slots // blk,),
        in_specs=[pl.BlockSpec((blk, D_MODEL // 2),
                               lambda i, be, nu, nv: (jnp.minimum(i, nu[0] - 1), 0)),
                  pl.BlockSpec((None, None, D_MODEL, D_EXPERT), w_map),
                  pl.BlockSpec((None, None, D_MODEL, D_EXPERT), w_map),
                  pl.BlockSpec((None, None, D_EXPERT, D_MODEL), w_map)],
        out_specs=pl.BlockSpec((blk, D_MODEL // 2), lambda i, be, nu, nv: (i, 0)),
        scratch_shapes=[pltpu.VMEM((D_MODEL, D_EXPERT), BF16),
                        pltpu.VMEM((D_MODEL, D_EXPERT), BF16),
                        pltpu.VMEM((D_EXPERT, D_MODEL), BF16)],
    )
    return pl.pallas_call(
        _ffn_kernel,
        grid_spec=grid_spec,
        out_shape=jax.ShapeDtypeStruct((n_slots, D_MODEL // 2), U32),
        compiler_params=_cparams(("arbitrary",)),
        name="expert_ffn",
    )(block_e, n_used, n_valid, xs, w_gate, w_up, w_down)


SC_CORES = 2
SC_SUBCORES = 16
SC_ROWS = 64


def _sc_gather_rows(table, idx):
    b = idx.shape[0]
    d = table.shape[1]
    per_w = b // (SC_CORES * SC_SUBCORES)
    mesh = plsc.VectorSubcoreMesh(core_axis_name="c", subcore_axis_name="s")

    n_chunks = per_w // SC_ROWS

    @functools.partial(
        pl.kernel, mesh=mesh,
        out_type=jax.ShapeDtypeStruct((b, d), table.dtype),
        scratch_types=[pltpu.VMEM((SC_ROWS,), jnp.int32), pltpu.VMEM((SC_ROWS,), jnp.int32),
                       pltpu.VMEM((SC_ROWS, d), table.dtype),
                       pltpu.VMEM((SC_ROWS, d), table.dtype),
                       pltpu.SemaphoreType.DMA, pltpu.SemaphoreType.DMA,
                       pltpu.SemaphoreType.DMA, pltpu.SemaphoreType.DMA],
        name="sc_gather_rows",
    )
    def gather(table_hbm, idx_hbm, out_hbm, idx0, idx1, rows0, rows1, gs0, gs1, ws0, ws1):
        idx_v, rows_v, gsem, wsem = (idx0, idx1), (rows0, rows1), (gs0, gs1), (ws0, ws1)
        wid = lax.axis_index("s") * SC_CORES + lax.axis_index("c")
        base = wid * per_w

        def rows_of(c):
            return pl.ds(pl.multiple_of(base + c * SC_ROWS, SC_ROWS), SC_ROWS)

        def start_gather(c, s):
            pltpu.sync_copy(idx_hbm.at[rows_of(c)], idx_v[s])
            pltpu.async_copy(table_hbm.at[idx_v[s]], rows_v[s], gsem[s])

        def write_back(c, s):
            pltpu.make_async_copy(table_hbm.at[idx_v[s]], rows_v[s], gsem[s]).wait()
            pltpu.async_copy(rows_v[s], out_hbm.at[rows_of(c)], wsem[s]).wait()

        start_gather(0, 0)

        @pl.loop(0, n_chunks, step=2)
        def _(c):
            start_gather(c + 1, 1)
            write_back(c, 0)

            @pl.when(c + 2 < n_chunks)
            def _():
                start_gather(c + 2, 0)

            write_back(c + 1, 1)

    return gather(table, idx)


def _sc_scatter_rows(x2d, idx, n_slots):
    t, d = x2d.shape
    per_w = t // (SC_CORES * SC_SUBCORES)
    mesh = plsc.VectorSubcoreMesh(core_axis_name="c", subcore_axis_name="s")

    @functools.partial(
        pl.kernel, mesh=mesh,
        out_type=jax.ShapeDtypeStruct((n_slots, d), x2d.dtype),
        scratch_types=[pltpu.VMEM((SC_ROWS,), jnp.int32), pltpu.VMEM((SC_ROWS,), jnp.int32),
                       pltpu.VMEM((SC_ROWS, d), x2d.dtype),
                       pltpu.SemaphoreType.DMA, pltpu.SemaphoreType.DMA],
        name="sc_scatter_rows",
    )
    def scatter(x_hbm, idx_hbm, out_hbm, idx0, idx1, rows_v, s0, s1):
        wid = lax.axis_index("s") * SC_CORES + lax.axis_index("c")
        base = wid * per_w

        @pl.loop(0, per_w // SC_ROWS)
        def _(c):
            off = pl.multiple_of(base + c * SC_ROWS, SC_ROWS)
            pltpu.sync_copy(x_hbm.at[pl.ds(off, SC_ROWS)], rows_v)
            pltpu.sync_copy(idx_hbm.at[pl.ds(off, SC_ROWS)], idx0)
            pltpu.sync_copy(idx_hbm.at[pl.ds(t + off, SC_ROWS)], idx1)
            cp0 = pltpu.async_copy(rows_v, out_hbm.at[idx0], s0)
            cp1 = pltpu.async_copy(rows_v, out_hbm.at[idx1], s1)
            cp0.wait()
            cp1.wait()

    return scatter(x2d, idx)


def _combine_dense_kernel(route_ref, x_ref, y0_ref, y1_ref, g_ref, b_ref, o_ref):
    y0 = jnp.concatenate(_unpack_halves(y0_ref[...]), axis=1)
    y1 = jnp.concatenate(_unpack_halves(y1_ref[...]), axis=1)
    moe = route_ref[:, 2:3] * y0 + route_ref[:, 3:4] * y1
    h = ALPHA * x_ref[...] + moe
    o_ref[...] = _layer_norm(h, g_ref[...], b_ref[...])


def _combine_dense(route, x2d, yg, ln_g, ln_b):
    t = x2d.shape[0]
    tm = 1024
    nt = t // tm
    row = lambda i: (i, 0)
    full = lambda i: (0, 0)
    return pl.pallas_call(
        _combine_dense_kernel,
        grid=(nt,),
        in_specs=[pl.BlockSpec((tm, LANES), row),
                  pl.BlockSpec((tm, D_MODEL), row),
                  pl.BlockSpec((tm, D_MODEL // 2), row),
                  pl.BlockSpec((tm, D_MODEL // 2), lambda i: (i + nt, 0)),
                  pl.BlockSpec((1, D_MODEL), full),
                  pl.BlockSpec((1, D_MODEL), full)],
        out_specs=pl.BlockSpec((tm, D_MODEL), row),
        out_shape=jax.ShapeDtypeStruct((t, D_MODEL), F32),
        compiler_params=_cparams(("parallel",)),
        name="moe_combine_dense",
    )(route, x2d, yg, yg, ln_g, ln_b)


def _pad_cols(w, n):
    return jnp.pad(w, [(0, 0)] * (w.ndim - 1) + [(0, n - w.shape[-1])])


def kernel(x, w_in, conv_w, gla_w_lr, gla_b_lr, gla_norm_g, ssd_conv_w, ssd_conv_b, ssd_a_log,
           ssd_d, ssd_dt_bias, ssd_norm_g, diff_lq1, diff_lk1, diff_lq2, diff_lk2, diff_norm_g,
           w_o, ln1_g, ln1_b, router_g, router_e, w_gate, w_up, w_down, ln2_g, ln2_b):
    bsz, seq, d = x.shape
    t = bsz * seq
    n_assign = 2 * t
    n_blocks = (n_assign + N_EXPERTS * (MOE_BLK - 1)) // MOE_BLK + 1
    n_slots = n_blocks * MOE_BLK
    x2d = x.reshape(t, d)
    w_in_r = jnp.concatenate([w_in[..., 0:768], _pad_cols(w_in[..., 768:1552], 896),
                              _pad_cols(w_in[..., 1552:2580], 1152), w_in[..., 2580:3348]],
                             axis=-1).astype(BF16)
    w_o_b = w_o.astype(BF16)
    for l in range(DEPTH):
        pc, pg, plr, ps, pdt, pd = [p.reshape(bsz, seq, -1) for p in _in_proj(x2d, w_in_r, l)]

        y_conv = _conv_mixer(pc, conv_w[l])
        w_lr_pad = jnp.pad(gla_w_lr[l], ((0, LANES - GLA_RANK), (0, 0)))
        y_gla = _gla_mixer(pg, plr, w_lr_pad, gla_b_lr[l].reshape(1, -1),
                           jnp.tile(gla_norm_g[l], GLA_HEADS).reshape(1, -1))
        pad4 = lambda v: jnp.pad(v, (0, LANES - SSD_HEADS)).reshape(1, LANES)
        y_ssd = _ssd_mixer(ps, pdt, ssd_conv_w[l], ssd_conv_b[l].reshape(1, -1),
                           pad4(ssd_a_log[l]), pad4(ssd_dt_bias[l]),
                           jnp.repeat(ssd_d[l], SSD_HEADDIM).reshape(1, -1),
                           ssd_norm_g[l].reshape(1, -1))
        lam_vecs = jnp.pad(jnp.stack([diff_lq1[l], diff_lk1[l], diff_lq2[l], diff_lk2[l]]),
                           ((0, 0), (0, LANES - DIFF_DQK)))
        lam_init = 0.8 - 0.6 * math.exp(-0.3 * l)
        y_diff = _diff_mixer(pd, lam_vecs,
                             jnp.tile(diff_norm_g[l], DIFF_HEADS).reshape(1, -1), lam_init)

        w_route = _pad_cols(jnp.concatenate(
            [router_g[l], router_e[l].reshape(d, N_EXPERTS)], axis=1), LANES)
        w_route_hi = w_route.astype(BF16)
        w_route = jnp.concatenate(
            [w_route_hi, (w_route - w_route_hi.astype(F32)).astype(BF16)], axis=1)
        ys = [y.reshape(t, W_MIX) for y in (y_conv, y_gla, y_ssd, y_diff)]
        xn, xn_p, route, cnt = _out_proj(ys, x2d, w_o_b, l, ln1_g[l].reshape(1, -1),
                                         ln1_b[l].reshape(1, -1), w_route)

        idx, block_e, n_used, n_valid = _dispatch_plan(route, cnt, n_blocks)
        xs = _sc_scatter_rows(xn_p, idx, n_slots)
        y = _expert_ffn(xs, block_e, n_used, n_valid, w_gate, w_up, w_down, l)
        yg = _sc_gather_rows(y, idx)
        x2d = _combine_dense(route, xn, yg, ln2_g[l].reshape(1, -1), ln2_b[l].reshape(1, -1))
    return x2d.reshape(bsz, seq, d)
```

```python
import functools
import math

import jax
import jax.numpy as jnp
from jax import lax
from jax.experimental import pallas as pl
from jax.experimental.pallas import tpu as pltpu
from jax.experimental.pallas import tpu_sc as plsc

F32 = jnp.float32
BF16 = jnp.bfloat16
HI = lax.Precision.HIGHEST

D_MODEL = 1024
DEPTH = 2
W_MIX = 256
GLA_HEADS, GLA_DK, GLA_DV, GLA_RANK, GLA_TAU, GLA_CHUNK = 4, 32, 64, 16, 16.0, 64
GLA_ROWS = 256
REC_SEQS = 2
SSD_HEADS, SSD_GROUPS, SSD_HEADDIM, SSD_STATE, SSD_CONV_K, SSD_CHUNK = 4, 2, 64, 128, 4, 128
DIFF_HEADS, DIFF_DQK, DIFF_DV = 4, 32, 64
N_GROUPS, EXPERTS_PER_GROUP, N_EXPERTS, D_EXPERT = 4, 8, 32, 512
ALPHA = (2 * DEPTH) ** 0.25
LN_EPS = 1e-5
RMS_EPS = 1e-6

LANES = 128
SUBLANES = 8
PROJ_WIDTHS = (768, 768, 128, 1024, 128, 768)
PROJ_DTYPES = (BF16, BF16, F32, BF16, F32, BF16)
VMEM_LIMIT = 56 * 1024 * 1024

MOE_BLK = 512
TOK_TILE = 256


def _cparams(sem):
    return pltpu.CompilerParams(dimension_semantics=sem, vmem_limit_bytes=VMEM_LIMIT)


def _sigmoid(x):
    return 1.0 / (1.0 + jnp.exp(-x))


def _softplus(x):
    return jnp.maximum(x, 0.0) + jnp.log(1.0 + jnp.exp(-jnp.abs(x)))


def _layer_norm(h, g, b):
    mu = jnp.mean(h, axis=-1, keepdims=True)
    d = h - mu
    var = jnp.mean(d * d, axis=-1, keepdims=True)
    return d * lax.rsqrt(var + LN_EPS) * g + b


def _dot_nt(a, b):
    return lax.dot_general(a, b, (((1,), (1,)), ((), ())), preferred_element_type=F32)


def _dot_tn(a, b, precision=None):
    return lax.dot_general(a, b, (((0,), (0,)), ((), ())), preferred_element_type=F32,
                           precision=precision)


def _split_bf16(x, parts):
    out = []
    for _ in range(parts - 1):
        hi = x.astype(BF16)
        out.append(hi)
        x = x - hi.astype(F32)
    out.append(x.astype(BF16))
    return out


def _dot(a, b):
    return jnp.dot(a, b, preferred_element_type=F32)


U32 = jnp.uint32


def _pack_halves(x):
    w = x.shape[1] // 2
    hi = lax.bitcast_convert_type(x[:, :w].astype(BF16).astype(F32), U32)
    lo = lax.bitcast_convert_type(x[:, w:].astype(BF16).astype(F32), U32)
    return hi | lax.shift_right_logical(lo, U32(16))


def _unpack_halves(p):
    hi = lax.bitcast_convert_type(p & U32(0xFFFF0000), F32)
    lo = lax.bitcast_convert_type(lax.shift_left(p, U32(16)), F32)
    return hi, lo


def _dot_split_lhs(a, b_exact, parts, dot=_dot):
    acc = None
    for term in _split_bf16(a, parts):
        d = dot(term, b_exact)
        acc = d if acc is None else acc + d
    return acc


def _dot_split_rhs(a_exact, b, parts):
    acc = None
    for term in _split_bf16(b, parts):
        d = jnp.dot(a_exact, term, preferred_element_type=F32)
        acc = d if acc is None else acc + d
    return acc


def _proj_kernel(x_ref, w_ref, *o_refs):
    xb = x_ref[...].astype(BF16)
    off = 0
    for o_ref in o_refs:
        n = o_ref.shape[-1]
        o_ref[...] = jnp.dot(xb, w_ref[:, off:off + n],
                             preferred_element_type=F32).astype(o_ref.dtype)
        off += n


def _in_proj(x2d, w_r, layer):
    t = x2d.shape[0]
    tm = 1024
    ncol = sum(PROJ_WIDTHS)
    return pl.pallas_call(
        _proj_kernel,
        grid=(t // tm,),
        in_specs=[pl.BlockSpec((tm, D_MODEL), lambda i: (i, 0)),
                  pl.BlockSpec((None, D_MODEL, ncol), lambda i: (layer, 0, 0))],
        out_specs=[pl.BlockSpec((tm, n), lambda i: (i, 0)) for n in PROJ_WIDTHS],
        out_shape=[jax.ShapeDtypeStruct((t, n), dt) for n, dt in zip(PROJ_WIDTHS, PROJ_DTYPES)],
        compiler_params=_cparams(("parallel",)),
        name="in_proj",
    )(x2d, w_r)


def _conv_kernel(p_ref, w_ref, o_ref):
    u = p_ref[0, :, 0:W_MIX].astype(F32)
    gb = p_ref[0, :, W_MIX:2 * W_MIX].astype(F32)
    gc = p_ref[0, :, 2 * W_MIX:3 * W_MIX].astype(F32)
    cu = gc * u
    row = lax.broadcasted_iota(jnp.int32, cu.shape, 0)
    acc = cu * w_ref[2:3, :]
    for s in (1, 2):
        sh = jnp.where(row >= s, pltpu.roll(cu, s, axis=0), 0.0)
        acc = acc + sh * w_ref[2 - s:3 - s, :]
    o_ref[0] = (gb * acc).astype(o_ref.dtype)


def _conv_mixer(pc, conv_w):
    b, s, _ = pc.shape
    return pl.pallas_call(
        _conv_kernel,
        grid=(b,),
        in_specs=[pl.BlockSpec((1, s, 3 * W_MIX), lambda i: (i, 0, 0)),
                  pl.BlockSpec((3, W_MIX), lambda i: (0, 0))],
        out_specs=pl.BlockSpec((1, s, W_MIX), lambda i: (i, 0, 0)),
        out_shape=jax.ShapeDtypeStruct((b, s, W_MIX), BF16),
        compiler_params=_cparams(("parallel",)),
        name="conv_mixer",
    )(pc, conv_w)


def _gla_setup(p_ref, lr_ref, wlr_ref, blr_ref, ng_ref, o_ref, st_ref):
    c = GLA_CHUNK
    s_len = p_ref.shape[1]
    nh, dk, dv = GLA_HEADS, GLA_DK, GLA_DV
    st_ref[...] = jnp.zeros_like(st_ref)

    rb = GLA_ROWS
    ncb = rb // c
    ri = lax.broadcasted_iota(jnp.int32, (rb, rb), 0)
    ci = lax.broadcasted_iota(jnp.int32, (rb, rb), 1)
    tri = (ci <= ri).astype(BF16)
    klane_head = lax.broadcasted_iota(jnp.int32, (1, nh * dk), 1) // dk
    vlane_head = lax.broadcasted_iota(jnp.int32, (1, nh * dv), 1) // dv
    strow_head = lax.broadcasted_iota(jnp.int32, (nh * dv, 1), 0) // dv
    st_mask = strow_head == klane_head
    r4 = lax.broadcasted_iota(jnp.int32, (nh * c, c), 0) % c
    c4 = lax.broadcasted_iota(jnp.int32, (nh * c, c), 1)
    causal4 = c4 <= r4
    gi = lax.broadcasted_iota(jnp.int32, (nh * dv, nh * dv), 0) // dv
    gj = lax.broadcasted_iota(jnp.int32, (nh * dv, nh * dv), 1) // dv
    gmean = jnp.where(gi == gj, 1.0 / dv, 0.0).astype(BF16)
    wlr_hi, wlr_lo = _split_bf16(wlr_ref[...], 2)

    def one_seq(bb, rows):
        q = p_ref[bb, rows, 0:128].astype(F32) * (dk ** -0.5)
        k = p_ref[bb, rows, 128:256].astype(F32)
        vb = p_ref[bb, rows, 256:512]
        g = p_ref[bb, rows, 512:768].astype(F32)
        lr = lr_ref[bb, rows, :]
        lr_hi, lr_lo = _split_bf16(lr, 2)
        z = (jnp.dot(lr_hi, wlr_hi, preferred_element_type=F32)
             + jnp.dot(lr_hi, wlr_lo, preferred_element_type=F32)
             + jnp.dot(lr_lo, wlr_hi, preferred_element_type=F32)) + blr_ref[...]
        log_a = (jnp.minimum(z, 0.0) - jnp.log(1.0 + jnp.exp(-jnp.abs(z)))) * (1.0 / GLA_TAU)
        cumb = _dot_split_rhs(tri, log_a, 3)
        ends = [cumb[(j + 1) * c - 1:(j + 1) * c, :] for j in range(ncb)]
        starts = [jnp.zeros_like(ends[0])] + ends[:-1]
        cum = cumb - jnp.concatenate([jnp.broadcast_to(s0, (c, nh * dk)) for s0 in starts], axis=0)
        lasts = [e - s0 for e, s0 in zip(ends, starts)]
        cl = jnp.concatenate([jnp.broadcast_to(x, (c, nh * dk)) for x in lasts], axis=0)
        q_dec = q * jnp.exp(cum)
        k_inv = (k * jnp.exp(-cum)).astype(BF16)
        k_end = (k * jnp.exp(cl - cum)).astype(BF16)
        st = st_ref[bb]
        outs = []
        for j in range(ncb):
            sl = slice(j * c, (j + 1) * c)
            qd = q_dec[sl]
            qs = jnp.concatenate([jnp.where(klane_head == h, qd, 0.0) for h in range(nh)],
                                 axis=0).astype(BF16)
            att = jnp.where(causal4, _dot_nt(qs, k_inv[sl]), 0.0)
            r = jnp.dot(att.astype(BF16), vb[sl], preferred_element_type=F32)
            o = jnp.where(vlane_head == 0, r[0:c], 0.0)
            for h in range(1, nh):
                o = o + jnp.where(vlane_head == h, r[h * c:(h + 1) * c], 0.0)
            outs.append(o + _dot_nt(qd.astype(BF16), st.astype(BF16)))
            d_st = _dot_tn(vb[sl], k_end[sl])
            st = st * jnp.exp(lasts[j]) + jnp.where(st_mask, d_st, 0.0)
        st_ref[bb] = st
        o = jnp.concatenate(outs, axis=0)
        ms = _dot_split_lhs(o * o, gmean, 2)
        o = o * lax.rsqrt(ms + RMS_EPS) * ng_ref[...]
        o_ref[bb, rows, :] = (o * (g * _sigmoid(g))).astype(o_ref.dtype)

    return one_seq


def _ssd_setup(p_ref, dt_ref, cw_ref, cb_ref, alog_ref, dtb_ref, dsk_ref, ng_ref, o_ref, st_ref):
    c = SSD_CHUNK
    s_len = p_ref.shape[1]
    n_st = SSD_STATE
    st_ref[...] = jnp.zeros_like(st_ref)

    ri = lax.broadcasted_iota(jnp.int32, (c, c), 0)
    ci = lax.broadcasted_iota(jnp.int32, (c, c), 1)
    causal = ci <= ri
    tri = causal.astype(BF16)
    upper = (ri <= ci).astype(BF16)
    lane_head = lax.broadcasted_iota(jnp.int32, (1, W_MIX), 1) // SSD_HEADDIM
    lane_group = lane_head // (SSD_HEADS // SSD_GROUPS)
    eh = lax.broadcasted_iota(jnp.int32, (LANES, W_MIX), 0)
    el = lax.broadcasted_iota(jnp.int32, (LANES, W_MIX), 1) // SSD_HEADDIM
    expand = (eh == el).astype(BF16)
    row8 = lax.broadcasted_iota(jnp.int32, (8, 3 * W_MIX), 0)
    a_c = -jnp.exp(alog_ref[...])

    def one_chunk(n, bb):
        r0 = pl.multiple_of(n * c, c)
        rows = pl.ds(r0, c)
        cur = p_ref[bb, rows, 256:1024].astype(F32)
        p0 = pl.multiple_of(jnp.maximum(r0 - 2 * SUBLANES, 0), 2 * SUBLANES)
        prev8 = p_ref[bb, pl.ds(p0, 2 * SUBLANES), 256:1024].astype(F32)[SUBLANES:]
        prev8 = jnp.where(n > 0, prev8, 0.0)
        acc = cur * cw_ref[3:4, :] + cb_ref[...]
        for s in (1, 2, 3):
            sh = pltpu.roll(cur, s, axis=0)
            top = jnp.where(row8 < s, pltpu.roll(prev8, s, axis=0), sh[0:8])
            sh = jnp.concatenate([top, sh[8:]], axis=0)
            acc = acc + sh * cw_ref[3 - s:4 - s, :]
        xbc = acc * _sigmoid(acc)
        x = xbc[:, 0:256]
        bm = xbc[:, 256:512].astype(BF16)
        cm = xbc[:, 512:768].astype(BF16)

        dt_c = _softplus(dt_ref[bb, rows, :] + dtb_ref[...])
        da_c = dt_c * a_c
        cum_c = _dot_split_rhs(tri, da_c, 3)
        cum_r = _dot_split_lhs(da_c, upper, 3, dot=_dot_tn)
        both_x = _dot_split_lhs(jnp.concatenate([dt_c, cum_c], axis=0), expand, 3)
        dt_x = both_x[0:c]
        cum_x = both_x[c:2 * c]
        cl_x = cum_x[c - 1:c, :]
        x_dt = x * dt_x
        x_dt_b = x_dt.astype(BF16)
        xw_b = (x_dt * jnp.exp(cl_x - cum_x)).astype(BF16)

        y = x * dsk_ref[...]
        y_off = jnp.zeros((c, W_MIX), F32)
        for g in range(SSD_GROUPS):
            bg = bm[:, g * n_st:(g + 1) * n_st]
            cg = cm[:, g * n_st:(g + 1) * n_st]
            cb = _dot_nt(cg, bg)
            for r in range(SSD_HEADS // SSD_GROUPS):
                h = g * (SSD_HEADS // SSD_GROUPS) + r
                diff = cum_c[:, h:h + 1] - cum_r[h:h + 1, :]
                dec = jnp.exp(jnp.where(causal, diff, -jnp.inf))
                m = (cb * dec).astype(BF16)
                yh = jnp.dot(m, x_dt_b, preferred_element_type=F32)
                y = y + jnp.where(lane_head == h, yh, 0.0)
            st = st_ref[bb, g]
            y_off = y_off + jnp.where(lane_group == g,
                                      jnp.dot(cg, st.astype(BF16), preferred_element_type=F32), 0.0)
            d_st = _dot_tn(bg, xw_b)
            st_ref[bb, g] = st * jnp.exp(cl_x) + jnp.where(lane_group == g, d_st, 0.0)
        y = y + y_off * jnp.exp(cum_x)
        zg = p_ref[bb, rows, 0:256].astype(F32)
        y = y * (zg * _sigmoid(zg))
        outs = []
        for g in range(SSD_GROUPS):
            yg = y[:, g * 128:(g + 1) * 128]
            ms = jnp.mean(yg * yg, axis=-1, keepdims=True)
            outs.append(yg * lax.rsqrt(ms + RMS_EPS))
        o_ref[bb, rows, :] = (jnp.concatenate(outs, axis=-1) * ng_ref[...]).astype(o_ref.dtype)

    return one_chunk


def _recurrent_kernel(pg_ref, lr_ref, wlr_ref, blr_ref, gng_ref,
                      ps_ref, dt_ref, cw_ref, cb_ref, alog_ref, dtb_ref, dsk_ref, sng_ref,
                      og_ref, os_ref, gst_ref, sst_ref):
    gla_rows = _gla_setup(pg_ref, lr_ref, wlr_ref, blr_ref, gng_ref, og_ref, gst_ref)
    ssd_chunk = _ssd_setup(ps_ref, dt_ref, cw_ref, cb_ref, alog_ref, dtb_ref, dsk_ref, sng_ref,
                           os_ref, sst_ref)
    per = GLA_ROWS // SSD_CHUNK

    def body(n, carry):
        rows = pl.ds(pl.multiple_of(n * GLA_ROWS, GLA_ROWS), GLA_ROWS)
        for bb in range(pg_ref.shape[0]):
            gla_rows(bb, rows)
            for j in range(per):
                ssd_chunk(n * per + j, bb)
        return carry

    lax.fori_loop(0, pg_ref.shape[1] // GLA_ROWS, body, 0)


def _recurrent_mixers(pg, plr, w_lr_pad, b_lr, gla_norm_g4,
                      ps, pdt, conv_w, conv_b, a_log_c, dt_bias_c, d_x, ssd_norm_g):
    b, s, _ = pg.shape
    nb = REC_SEQS
    seq = lambda i: (i, 0, 0)
    full2 = lambda i: (0, 0)
    return pl.pallas_call(
        _recurrent_kernel,
        grid=(b // nb,),
        in_specs=[pl.BlockSpec((nb, s, pg.shape[2]), seq),
                  pl.BlockSpec((nb, s, LANES), seq),
                  pl.BlockSpec((LANES, LANES), full2),
                  pl.BlockSpec((1, LANES), full2),
                  pl.BlockSpec((1, W_MIX), full2),
                  pl.BlockSpec((nb, s, ps.shape[2]), seq),
                  pl.BlockSpec((nb, s, LANES), seq),
                  pl.BlockSpec((SSD_CONV_K, 3 * W_MIX), full2),
                  pl.BlockSpec((1, 3 * W_MIX), full2),
                  pl.BlockSpec((1, LANES), full2),
                  pl.BlockSpec((1, LANES), full2),
                  pl.BlockSpec((1, W_MIX), full2),
                  pl.BlockSpec((1, W_MIX), full2)],
        out_specs=[pl.BlockSpec((nb, s, W_MIX), seq), pl.BlockSpec((nb, s, W_MIX), seq)],
        out_shape=[jax.ShapeDtypeStruct((b, s, W_MIX), BF16),
                   jax.ShapeDtypeStruct((b, s, W_MIX), BF16)],
        scratch_shapes=[pltpu.VMEM((nb, GLA_HEADS * GLA_DV, GLA_HEADS * GLA_DK), F32),
                        pltpu.VMEM((nb, SSD_GROUPS, SSD_STATE, W_MIX), F32)],
        compiler_params=_cparams(("parallel",)),
        name="gla_ssd_mixers",
    )(pg, plr, w_lr_pad, b_lr, gla_norm_g4, ps, pdt, conv_w, conv_b, a_log_c, dt_bias_c, d_x,
      ssd_norm_g)


DIFF_TQ = 256
DIFF_TK = 256
LOG2E = 1.4426950408889634
DIFF_VPAD = DIFF_DV + 16


def _diff_kernel(q_ref, k_ref, v_ref, lam_ref, ng_ref, o_ref,
                 vt_ref, qs_ref, st_ref, m_ref, acc_ref, *, lam_init):
    tq, tk = DIFF_TQ, DIFF_TK
    nh, dv = DIFF_HEADS, DIFF_DV
    nhc = 2 * nh
    s_len = k_ref.shape[1]
    i = pl.program_id(1)

    @pl.when(i == 0)
    def _():
        for cblk in range(s_len // tk):
            cols = slice(cblk * tk, (cblk + 1) * tk)
            vt = v_ref[0, cols, :].astype(F32).T.astype(BF16)
            for h in range(nh):
                vt_ref[h, 0:dv, cols] = vt[h * dv:(h + 1) * dv]
        vt_ref[:, dv:, :] = jnp.ones((nh, DIFF_VPAD - dv, s_len), BF16)

    q = q_ref[0].astype(F32) * (DIFF_DQK ** -0.5 * LOG2E)
    qlane = lax.broadcasted_iota(jnp.int32, (1, W_MIX), 1) // DIFF_DQK
    for hc in range(nhc):
        qs_ref[hc * tq:(hc + 1) * tq, :] = jnp.where(qlane == hc, q, 0.0).astype(BF16)
    m_ref[...] = jnp.full_like(m_ref, -jnp.inf)
    acc_ref[...] = jnp.zeros_like(acc_ref)
    krow = lax.broadcasted_iota(jnp.int32, (tk, nhc * tq), 0)
    qcol = lax.broadcasted_iota(jnp.int32, (tk, nhc * tq), 1) % tq
    diag_ok = krow <= qcol

    def scores(j, slot):
        k0 = pl.multiple_of(j * tk, tk)
        st_ref[slot] = _dot_nt(k_ref[0, pl.ds(k0, tk), :], qs_ref[...])

    def softmax_pv(j, slot, masked):
        k0 = pl.multiple_of(j * tk, tk)
        st = st_ref[slot]
        if masked:
            st = jnp.where(diag_ok, st, -jnp.inf)
        m_prev = m_ref[...]
        m_new = jnp.maximum(m_prev, jnp.max(st, axis=0, keepdims=True))
        alpha = jnp.exp2(m_prev - m_new)
        p = jnp.exp2(st - m_new)
        m_ref[...] = m_new
        pb = p.astype(BF16)
        for hc in range(nhc):
            h = hc // 2
            lanes = slice(hc * tq, (hc + 1) * tq)
            pv = jnp.dot(vt_ref[h, :, pl.ds(k0, tk)], pb[:, lanes],
                         preferred_element_type=F32)
            acc_ref[hc] = acc_ref[hc] * alpha[:, lanes] + pv

    scores(0, 0)
    n_pairs = i // 2

    def pair_step(u, carry):
        scores(2 * u + 1, 1)
        softmax_pv(2 * u, 0, False)
        scores(2 * u + 2, 0)
        softmax_pv(2 * u + 1, 1, False)
        return carry

    lax.fori_loop(0, n_pairs, pair_step, 0)

    @pl.when(i % 2 == 0)
    def _():
        softmax_pv(i, 0, True)

    @pl.when(i % 2 == 1)
    def _():
        scores(i, 1)
        softmax_pv(i - 1, 0, False)
        softmax_pv(i, 1, True)

    lam = (jnp.exp(jnp.sum(lam_ref[0:1, :] * lam_ref[1:2, :], axis=-1, keepdims=True))
           - jnp.exp(jnp.sum(lam_ref[2:3, :] * lam_ref[3:4, :], axis=-1, keepdims=True))
           + lam_init)
    heads = []
    for h in range(nh):
        o1 = acc_ref[2 * h, 0:dv] / acc_ref[2 * h, dv:dv + 1]
        o2 = acc_ref[2 * h + 1, 0:dv] / acc_ref[2 * h + 1, dv:dv + 1]
        oh = o1 - lam * o2
        ms = jnp.mean(oh * oh, axis=0, keepdims=True)
        heads.append(oh * lax.rsqrt(ms + RMS_EPS))
    o = jnp.concatenate(heads, axis=0).T
    o_ref[0] = (o * ng_ref[...] * (1.0 - lam_init)).astype(o_ref.dtype)


def _diff_mixer(pd, lam_vecs, norm_g4, lam_init):
    b, s, _ = pd.shape
    tq = DIFF_TQ
    return pl.pallas_call(
        functools.partial(_diff_kernel, lam_init=lam_init),
        grid=(b, s // tq),
        in_specs=[pl.BlockSpec((1, tq, W_MIX), lambda bi, i: (bi, i, 0)),
                  pl.BlockSpec((1, s, W_MIX), lambda bi, i: (bi, 0, 1)),
                  pl.BlockSpec((1, s, W_MIX), lambda bi, i: (bi, 0, 2)),
                  pl.BlockSpec((4, LANES), lambda bi, i: (0, 0)),
                  pl.BlockSpec((1, W_MIX), lambda bi, i: (0, 0))],
        out_specs=pl.BlockSpec((1, tq, W_MIX), lambda bi, i: (bi, i, 0)),
        out_shape=jax.ShapeDtypeStruct((b, s, W_MIX), BF16),
        scratch_shapes=[pltpu.VMEM((DIFF_HEADS, DIFF_VPAD, s), BF16),
                        pltpu.VMEM((2 * DIFF_HEADS * tq, W_MIX), BF16),
                        pltpu.VMEM((2, DIFF_TK, 2 * DIFF_HEADS * tq), F32),
                        pltpu.VMEM((1, 2 * DIFF_HEADS * tq), F32),
                        pltpu.VMEM((2 * DIFF_HEADS, DIFF_VPAD, tq), F32)],
        compiler_params=_cparams(("parallel", "arbitrary")),
        name="diff_attn",
    )(pd, pd, pd, lam_vecs, norm_g4)


def _oproj_kernel(yc_ref, yg_ref, ys_ref, yd_ref, x_ref, wo_ref, g_ref, b_ref, wr_ref,
                  xo_ref, xp_ref, route_ref, cnt_ref):
    mix = jnp.concatenate([yc_ref[...], yg_ref[...], ys_ref[...], yd_ref[...]], axis=-1)
    h = ALPHA * x_ref[...] + jnp.dot(mix, wo_ref[...], preferred_element_type=F32)
    xn = _layer_norm(h, g_ref[...], b_ref[...])
    xo_ref[...] = xn
    xp_ref[...] = _pack_halves(xn)

    xn_hi, xn_lo = _split_bf16(xn, 2)
    both = _dot(xn_hi, wr_ref[...])
    logits = both[:, 0:LANES] + both[:, LANES:2 * LANES] + _dot(xn_lo, wr_ref[:, 0:LANES])
    lane = lax.broadcasted_iota(jnp.int32, logits.shape, 1).astype(F32)
    neg = -jnp.inf
    big = float(LANES)
    lg = jnp.where(lane < N_GROUPS, logits, neg)
    mg = jnp.max(lg, axis=-1, keepdims=True)
    sg = jnp.sum(jnp.exp(lg - mg), axis=-1, keepdims=True)
    grp = jnp.min(jnp.where(lg == mg, lane, big), axis=-1, keepdims=True)
    p_grp = 1.0 / sg
    lo = N_GROUPS + EXPERTS_PER_GROUP * grp
    in_g = jnp.logical_and(lane >= lo, lane < lo + EXPERTS_PER_GROUP)
    le = jnp.where(in_g, logits, neg)
    me = jnp.max(le, axis=-1, keepdims=True)
    ee = jnp.exp(le - me)
    pe = ee / jnp.sum(ee, axis=-1, keepdims=True)
    pe = jnp.where(in_g, pe, -1.0)
    p1 = jnp.max(pe, axis=-1, keepdims=True)
    i1 = jnp.min(jnp.where(pe == p1, lane, big), axis=-1, keepdims=True)
    pe2 = jnp.where(lane == i1, -1.0, pe)
    p2 = jnp.max(pe2, axis=-1, keepdims=True)
    i2 = jnp.min(jnp.where(pe2 == p2, lane, big), axis=-1, keepdims=True)
    den = p1 + p2
    g1 = p_grp * p1 / den
    g2 = p_grp * p2 / den
    e1 = i1 - N_GROUPS
    e2 = i2 - N_GROUPS
    route_ref[...] = jnp.where(lane == 0, e1, jnp.where(lane == 1, e2, jnp.where(
        lane == 2, g1, jnp.where(lane == 3, g2, 0.0))))

    @pl.when(pl.program_id(0) == 0)
    def _():
        cnt_ref[...] = jnp.zeros_like(cnt_ref)

    hits = jnp.where(lane == e1, 1.0, 0.0) + jnp.where(lane == e2, 1.0, 0.0)
    cnt_ref[...] += jnp.sum(hits, axis=0, keepdims=True)


def _out_proj(ys, x2d, w_o, layer, ln_g, ln_b, w_route):
    t = x2d.shape[0]
    tm = 1024
    row = lambda i: (i, 0)
    full = lambda i: (0, 0)
    return pl.pallas_call(
        _oproj_kernel,
        grid=(t // tm,),
        in_specs=[pl.BlockSpec((tm, W_MIX), row)] * 4 + [
            pl.BlockSpec((tm, D_MODEL), row),
            pl.BlockSpec((None, D_MODEL, D_MODEL), lambda i: (layer, 0, 0)),
            pl.BlockSpec((1, D_MODEL), full),
            pl.BlockSpec((1, D_MODEL), full),
            pl.BlockSpec((D_MODEL, 2 * LANES), full)],
        out_specs=[pl.BlockSpec((tm, D_MODEL), row), pl.BlockSpec((tm, D_MODEL // 2), row),
                   pl.BlockSpec((tm, LANES), row), pl.BlockSpec((1, LANES), full)],
        out_shape=[jax.ShapeDtypeStruct((t, D_MODEL), F32),
                   jax.ShapeDtypeStruct((t, D_MODEL // 2), U32),
                   jax.ShapeDtypeStruct((t, LANES), F32),
                   jax.ShapeDtypeStruct((1, LANES), F32)],
        compiler_params=_cparams(("arbitrary",)),
        name="out_proj_ln_router",
    )(*ys, x2d, w_o, ln_g, ln_b, w_route)


PLAN_TILE = 512


def _plan_kernel(route_ref, cnt_ref, dest_ref, meta_ref, carry_ref, pstart_ref):
    tm = route_ref.shape[0]
    lane = lax.broadcasted_iota(jnp.int32, (1, LANES), 1).astype(F32)

    @pl.when(pl.program_id(0) == 0)
    def _():
        cnt = cnt_ref[...]
        padded = jnp.ceil(cnt * (1.0 / MOE_BLK)) * MOE_BLK
        li = lax.broadcasted_iota(jnp.int32, (LANES, LANES), 0)
        lj = lax.broadcasted_iota(jnp.int32, (LANES, LANES), 1)
        before = (li < lj).astype(F32)
        pstart = jnp.dot(jnp.broadcast_to(padded, (8, LANES)), before, precision=HI,
                         preferred_element_type=F32)[0:1]
        pstart_ref[...] = pstart
        carry_ref[...] = jnp.zeros_like(carry_ref)
        meta_ref[...] = jnp.concatenate(
            [pstart + padded, pstart, cnt, jnp.zeros((5, LANES), F32)], axis=0)

    oh0 = jnp.where(lane == route_ref[:, 0:1], 1.0, 0.0)
    oh1 = jnp.where(lane == route_ref[:, 1:2], 1.0, 0.0)
    both = oh0 + oh1
    ri = lax.broadcasted_iota(jnp.int32, (tm, tm), 0)
    ci = lax.broadcasted_iota(jnp.int32, (tm, tm), 1)
    earlier = (ci < ri).astype(BF16)
    base = (jnp.dot(earlier, both.astype(BF16), preferred_element_type=F32)
            + carry_ref[...] + pstart_ref[...])
    d0 = jnp.sum(oh0 * base, axis=-1, keepdims=True)
    d1 = jnp.sum(oh1 * base, axis=-1, keepdims=True)
    dest = jnp.where(lane == 0, d0, jnp.where(lane == 1, d1, 0.0))
    dest_ref[...] = dest.T[0:2, :].astype(jnp.int32)
    carry_ref[...] += jnp.sum(both, axis=0, keepdims=True)


def _dispatch_plan(route, cnt, n_blocks):
    t = route.shape[0]
    tm = PLAN_TILE
    blk = MOE_BLK
    dest, meta = pl.pallas_call(
        _plan_kernel,
        grid=(t // tm,),
        in_specs=[pl.BlockSpec((tm, LANES), lambda i: (i, 0)),
                  pl.BlockSpec((1, LANES), lambda i: (0, 0))],
        out_specs=[pl.BlockSpec((2, tm), lambda i: (0, i)),
                   pl.BlockSpec((8, LANES), lambda i: (0, 0))],
        out_shape=[jax.ShapeDtypeStruct((2, t), jnp.int32),
                   jax.ShapeDtypeStruct((8, LANES), F32)],
        scratch_shapes=[pltpu.VMEM((1, LANES), F32), pltpu.VMEM((1, LANES), F32)],
        compiler_params=_cparams(("arbitrary",)),
        name="moe_plan",
    )(route, cnt)
    meta_i = meta[:, :N_EXPERTS].astype(jnp.int32)
    pad_end, pad_start, seg_end = meta_i[0], meta_i[1], meta_i[1] + meta_i[2]
    starts = jnp.arange(n_blocks, dtype=jnp.int32)[:, None] * blk
    member = jnp.logical_and(starts >= pad_start[None, :], starts < pad_end[None, :])
    expert_ids = jnp.arange(N_EXPERTS, dtype=jnp.int32)[None, :]
    block_e = jnp.where(jnp.any(member, axis=1), jnp.sum(jnp.where(member, expert_ids, 0), axis=1),
                        N_EXPERTS - 1)
    n_used = (pad_end[N_EXPERTS - 1] // blk).reshape(1)
    n_valid = jnp.clip(jnp.sum(jnp.where(member, seg_end[None, :], 0), axis=1) - starts[:, 0],
                       0, blk)
    return dest.reshape(2 * t), block_e, n_used, n_valid


def _ffn_kernel(be_ref, nu_ref, nv_ref, xs_ref, wg_ref, wu_ref, wd_ref, y_ref,
                wgb_ref, wub_ref, wdb_ref):
    i = pl.program_id(0)
    prev = be_ref[jnp.maximum(i - 1, 0)]

    @pl.when(jnp.logical_or(i == 0, be_ref[i] != prev))
    def _():
        wgb_ref[...] = wg_ref[...].astype(BF16)
        wub_ref[...] = wu_ref[...].astype(BF16)
        wdb_ref[...] = wd_ref[...].astype(BF16)

    @pl.when(i < nu_ref[0])
    def _():
        half = xs_ref.shape[0] // 2
        for r in range(2):
            rows = slice(r * half, (r + 1) * half)
            row = lax.broadcasted_iota(jnp.int32, (half, 1), 0) + r * half
            xp = jnp.where(row < nv_ref[i], xs_ref[rows, :], U32(0))
            x_hi, x_lo = _unpack_halves(xp)
            xb = jnp.concatenate([x_hi.astype(BF16), x_lo.astype(BF16)], axis=1)
            a = jnp.dot(xb, wgb_ref[...], preferred_element_type=F32)
            u = jnp.dot(xb, wub_ref[...], preferred_element_type=F32)
            h = (a * _sigmoid(a) * u).astype(BF16)
            y_ref[rows, :] = _pack_halves(jnp.dot(h, wdb_ref[...], preferred_element_type=F32))

    @pl.when(i >= nu_ref[0])
    def _():
        y_ref[...] = jnp.zeros_like(y_ref)


def _expert_ffn(xs, block_e, n_used, n_valid, w_gate, w_up, w_down, layer):
    n_slots = xs.shape[0]
    blk = MOE_BLK
    w_map = lambda i, be, nu, nv: (layer, be[i], 0, 0)
    grid_spec = pltpu.PrefetchScalarGridSpec(
        num_scalar_prefetch=3,
        grid=(n_slots // blk,),
        in_specs=[pl.BlockSpec((blk, D_MODEL // 2),
                               lambda i, be, nu, nv: (jnp.minimum(i, nu[0] - 1), 0)),
                  pl.BlockSpec((None, None, D_MODEL, D_EXPERT), w_map),
                  pl.BlockSpec((None, None, D_MODEL, D_EXPERT), w_map),
                  pl.BlockSpec((None, None, D_EXPERT, D_MODEL), w_map)],
        out_specs=pl.BlockSpec((blk, D_MODEL // 2), lambda i, be, nu, nv: (i, 0)),
        scratch_shapes=[pltpu.VMEM((D_MODEL, D_EXPERT), BF16),
                        pltpu.VMEM((D_MODEL, D_EXPERT), BF16),
                        pltpu.VMEM((D_EXPERT, D_MODEL), BF16)],
    )
    return pl.pallas_call(
        _ffn_kernel,
        grid_spec=grid_spec,
        out_shape=jax.ShapeDtypeStruct((n_slots, D_MODEL // 2), U32),
        compiler_params=_cparams(("arbitrary",)),
        name="expert_ffn",
    )(block_e, n_used, n_valid, xs, w_gate, w_up, w_down)


SC_CORES = 2
SC_SUBCORES = 16
SC_ROWS = 64


def _sc_gather_rows(table, idx):
    b = idx.shape[0]
    d = table.shape[1]
    per_w = b // (SC_CORES * SC_SUBCORES)
    mesh = plsc.VectorSubcoreMesh(core_axis_name="c", subcore_axis_name="s")

    n_chunks = per_w // SC_ROWS

    @functools.partial(
        pl.kernel, mesh=mesh,
        out_type=jax.ShapeDtypeStruct((b, d), table.dtype),
        scratch_types=[pltpu.VMEM((SC_ROWS,), jnp.int32), pltpu.VMEM((SC_ROWS,), jnp.int32),
                       pltpu.VMEM((SC_ROWS, d), table.dtype),
                       pltpu.VMEM((SC_ROWS, d), table.dtype),
                       pltpu.SemaphoreType.DMA, pltpu.SemaphoreType.DMA,
                       pltpu.SemaphoreType.DMA, pltpu.SemaphoreType.DMA],
        name="sc_gather_rows",
    )
    def gather(table_hbm, idx_hbm, out_hbm, idx0, idx1, rows0, rows1, gs0, gs1, ws0, ws1):
        idx_v, rows_v, gsem, wsem = (idx0, idx1), (rows0, rows1), (gs0, gs1), (ws0, ws1)
        wid = lax.axis_index("s") * SC_CORES + lax.axis_index("c")
        base = wid * per_w

        def rows_of(c):
            return pl.ds(pl.multiple_of(base + c * SC_ROWS, SC_ROWS), SC_ROWS)

        def start_gather(c, s):
            pltpu.sync_copy(idx_hbm.at[rows_of(c)], idx_v[s])
            pltpu.async_copy(table_hbm.at[idx_v[s]], rows_v[s], gsem[s])

        def write_back(c, s):
            pltpu.make_async_copy(table_hbm.at[idx_v[s]], rows_v[s], gsem[s]).wait()
            pltpu.async_copy(rows_v[s], out_hbm.at[rows_of(c)], wsem[s]).wait()

        start_gather(0, 0)

        @pl.loop(0, n_chunks, step=2)
        def _(c):
            start_gather(c + 1, 1)
            write_back(c, 0)

            @pl.when(c + 2 < n_chunks)
            def _():
                start_gather(c + 2, 0)

            write_back(c + 1, 1)

    return gather(table, idx)


def _sc_scatter_rows(x2d, idx, n_slots):
    t, d = x2d.shape
    per_w = t // (SC_CORES * SC_SUBCORES)
    mesh = plsc.VectorSubcoreMesh(core_axis_name="c", subcore_axis_name="s")

    @functools.partial(
        pl.kernel, mesh=mesh,
        out_type=jax.ShapeDtypeStruct((n_slots, d), x2d.dtype),
        scratch_types=[pltpu.VMEM((SC_ROWS,), jnp.int32), pltpu.VMEM((SC_ROWS,), jnp.int32),
                       pltpu.VMEM((SC_ROWS, d), x2d.dtype),
                       pltpu.SemaphoreType.DMA, pltpu.SemaphoreType.DMA],
        name="sc_scatter_rows",
    )
    def scatter(x_hbm, idx_hbm, out_hbm, idx0, idx1, rows_v, s0, s1):
        wid = lax.axis_index("s") * SC_CORES + lax.axis_index("c")
        base = wid * per_w

        @pl.loop(0, per_w // SC_ROWS)
        def _(c):
            off = pl.multiple_of(base + c * SC_ROWS, SC_ROWS)
            pltpu.sync_copy(x_hbm.at[pl.ds(off, SC_ROWS)], rows_v)
            pltpu.sync_copy(idx_hbm.at[pl.ds(off, SC_ROWS)], idx0)
            pltpu.sync_copy(idx_hbm.at[pl.ds(t + off, SC_ROWS)], idx1)
            cp0 = pltpu.async_copy(rows_v, out_hbm.at[idx0], s0)
            cp1 = pltpu.async_copy(rows_v, out_hbm.at[idx1], s1)
            cp0.wait()
            cp1.wait()

    return scatter(x2d, idx)


def _combine_dense_kernel(route_ref, x_ref, y0_ref, y1_ref, g_ref, b_ref, o_ref):
    y0 = jnp.concatenate(_unpack_halves(y0_ref[...]), axis=1)
    y1 = jnp.concatenate(_unpack_halves(y1_ref[...]), axis=1)
    moe = route_ref[:, 2:3] * y0 + route_ref[:, 3:4] * y1
    h = ALPHA * x_ref[...] + moe
    o_ref[...] = _layer_norm(h, g_ref[...], b_ref[...])


def _combine_dense(route, x2d, yg, ln_g, ln_b):
    t = x2d.shape[0]
    tm = 1024
    nt = t // tm
    row = lambda i: (i, 0)
    full = lambda i: (0, 0)
    return pl.pallas_call(
        _combine_dense_kernel,
        grid=(nt,),
        in_specs=[pl.BlockSpec((tm, LANES), row),
                  pl.BlockSpec((tm, D_MODEL), row),
                  pl.BlockSpec((tm, D_MODEL // 2), row),
                  pl.BlockSpec((tm, D_MODEL // 2), lambda i: (i + nt, 0)),
                  pl.BlockSpec((1, D_MODEL), full),
                  pl.BlockSpec((1, D_MODEL), full)],
        out_specs=pl.BlockSpec((tm, D_MODEL), row),
        out_shape=jax.ShapeDtypeStruct((t, D_MODEL), F32),
        compiler_params=_cparams(("parallel",)),
        name="moe_combine_dense",
    )(route, x2d, yg, yg, ln_g, ln_b)


def _pad_cols(w, n):
    return jnp.pad(w, [(0, 0)] * (w.ndim - 1) + [(0, n - w.shape[-1])])


def kernel(x, w_in, conv_w, gla_w_lr, gla_b_lr, gla_norm_g, ssd_conv_w, ssd_conv_b, ssd_a_log,
           ssd_d, ssd_dt_bias, ssd_norm_g, diff_lq1, diff_lk1, diff_lq2, diff_lk2, diff_norm_g,
           w_o, ln1_g, ln1_b, router_g, router_e, w_gate, w_up, w_down, ln2_g, ln2_b):
    bsz, seq, d = x.shape
    t = bsz * seq
    n_assign = 2 * t
    n_blocks = (n_assign + N_EXPERTS * (MOE_BLK - 1)) // MOE_BLK + 1
    n_slots = n_blocks * MOE_BLK
    x2d = x.reshape(t, d)
    w_in_r = jnp.concatenate([w_in[..., 0:768], _pad_cols(w_in[..., 768:1552], 896),
                              _pad_cols(w_in[..., 1552:2580], 1152), w_in[..., 2580:3348]],
                             axis=-1).astype(BF16)
    w_o_b = w_o.astype(BF16)
    for l in range(DEPTH):
        pc, pg, plr, ps, pdt, pd = [p.reshape(bsz, seq, -1) for p in _in_proj(x2d, w_in_r, l)]

        y_conv = _conv_mixer(pc, conv_w[l])
        w_lr_pad = jnp.pad(gla_w_lr[l], ((0, LANES - GLA_RANK), (0, 0)))
        pad4 = lambda v: jnp.pad(v, (0, LANES - SSD_HEADS)).reshape(1, LANES)
        y_gla, y_ssd = _recurrent_mixers(
            pg, plr, w_lr_pad, gla_b_lr[l].reshape(1, -1),
            jnp.tile(gla_norm_g[l], GLA_HEADS).reshape(1, -1),
            ps, pdt, ssd_conv_w[l], ssd_conv_b[l].reshape(1, -1),
            pad4(ssd_a_log[l]), pad4(ssd_dt_bias[l]),
            jnp.repeat(ssd_d[l], SSD_HEADDIM).reshape(1, -1), ssd_norm_g[l].reshape(1, -1))
        lam_vecs = jnp.pad(jnp.stack([diff_lq1[l], diff_lk1[l], diff_lq2[l], diff_lk2[l]]),
                           ((0, 0), (0, LANES - DIFF_DQK)))
        lam_init = 0.8 - 0.6 * math.exp(-0.3 * l)
        y_diff = _diff_mixer(pd, lam_vecs,
                             jnp.tile(diff_norm_g[l], DIFF_HEADS).reshape(1, -1), lam_init)

        w_route = _pad_cols(jnp.concatenate(
            [router_g[l], router_e[l].reshape(d, N_EXPERTS)], axis=1), LANES)
        w_route_hi = w_route.astype(BF16)
        w_route = jnp.concatenate(
            [w_route_hi, (w_route - w_route_hi.astype(F32)).astype(BF16)], axis=1)
        ys = [y.reshape(t, W_MIX) for y in (y_conv, y_gla, y_ssd, y_diff)]
        xn, xn_p, route, cnt = _out_proj(ys, x2d, w_o_b, l, ln1_g[l].reshape(1, -1),
                                         ln1_b[l].reshape(1, -1), w_route)

        idx, block_e, n_used, n_valid = _dispatch_plan(route, cnt, n_blocks)
        xs = _sc_scatter_rows(xn_p, idx, n_slots)
        y = _expert_ffn(xs, block_e, n_used, n_valid, w_gate, w_up, w_down, l)
        yg = _sc_gather_rows(y, idx)
        x2d = _combine_dense(route, xn, yg, ln2_g[l].reshape(1, -1), ln2_b[l].reshape(1, -1))
    return x2d.reshape(bsz, seq, d)
```

```python
import functools
import math

import jax
import jax.numpy as jnp
from jax import lax
from jax.experimental import pallas as pl
from jax.experimental.pallas import tpu as pltpu
from jax.experimental.pallas import tpu_sc as plsc

F32 = jnp.float32
BF16 = jnp.bfloat16
HI = lax.Precision.HIGHEST

D_MODEL = 1024
DEPTH = 2
W_MIX = 256
GLA_HEADS, GLA_DK, GLA_DV, GLA_RANK, GLA_TAU, GLA_CHUNK = 4, 32, 64, 16, 16.0, 64
GLA_ROWS = 256
REC_SEQS = 2
SSD_HEADS, SSD_GROUPS, SSD_HEADDIM, SSD_STATE, SSD_CONV_K, SSD_CHUNK = 4, 2, 64, 128, 4, 128
DIFF_HEADS, DIFF_DQK, DIFF_DV = 4, 32, 64
N_GROUPS, EXPERTS_PER_GROUP, N_EXPERTS, D_EXPERT = 4, 8, 32, 512
ALPHA = (2 * DEPTH) ** 0.25
LN_EPS = 1e-5
RMS_EPS = 1e-6

LANES = 128
SUBLANES = 8
PROJ_WIDTHS = (768, 768, 128, 1024, 128, 768)
PROJ_SRC_OFFSETS = (0, 768, 1536, 1552, 2576, 2580)
PROJ_SRC_WIDTHS = (768, 768, GLA_RANK, 1024, SSD_HEADS, 768)
PROJ_DTYPES = (BF16, BF16, F32, BF16, F32, BF16)
VMEM_LIMIT = 56 * 1024 * 1024

MOE_BLK = 512
TOK_TILE = 256


def _cparams(sem):
    return pltpu.CompilerParams(dimension_semantics=sem, vmem_limit_bytes=VMEM_LIMIT)


def _sigmoid(x):
    return 1.0 / (1.0 + jnp.exp(-x))


def _softplus(x):
    return jnp.maximum(x, 0.0) + jnp.log(1.0 + jnp.exp(-jnp.abs(x)))


def _layer_norm(h, g, b):
    mu = jnp.mean(h, axis=-1, keepdims=True)
    d = h - mu
    var = jnp.mean(d * d, axis=-1, keepdims=True)
    return d * lax.rsqrt(var + LN_EPS) * g + b


def _dot_nt(a, b):
    return lax.dot_general(a, b, (((1,), (1,)), ((), ())), preferred_element_type=F32)


def _dot_tn(a, b, precision=None):
    return lax.dot_general(a, b, (((0,), (0,)), ((), ())), preferred_element_type=F32,
                           precision=precision)


def _split_bf16(x, parts):
    out = []
    for _ in range(parts - 1):
        hi = x.astype(BF16)
        out.append(hi)
        x = x - hi.astype(F32)
    out.append(x.astype(BF16))
    return out


def _dot(a, b):
    return jnp.dot(a, b, preferred_element_type=F32)


U32 = jnp.uint32


def _pack_halves(x):
    w = x.shape[1] // 2
    hi = lax.bitcast_convert_type(x[:, :w].astype(BF16).astype(F32), U32)
    lo = lax.bitcast_convert_type(x[:, w:].astype(BF16).astype(F32), U32)
    return hi | lax.shift_right_logical(lo, U32(16))


def _unpack_halves(p):
    hi = lax.bitcast_convert_type(p & U32(0xFFFF0000), F32)
    lo = lax.bitcast_convert_type(lax.shift_left(p, U32(16)), F32)
    return hi, lo


def _dot_split_lhs(a, b_exact, parts, dot=_dot):
    acc = None
    for term in _split_bf16(a, parts):
        d = dot(term, b_exact)
        acc = d if acc is None else acc + d
    return acc


def _dot_split_rhs(a_exact, b, parts):
    acc = None
    for term in _split_bf16(b, parts):
        d = jnp.dot(a_exact, term, preferred_element_type=F32)
        acc = d if acc is None else acc + d
    return acc


def _proj_kernel(x_ref, w_ref, *refs):
    o_refs, w_scr = refs[:-1], refs[-1]

    @pl.when(pl.program_id(0) == 0)
    def _():
        w_scr[...] = jnp.zeros_like(w_scr)
        dst = 0
        for src, n_src, n_dst in zip(PROJ_SRC_OFFSETS, PROJ_SRC_WIDTHS, PROJ_WIDTHS):
            w_scr[:, dst:dst + n_src] = w_ref[:, src:src + n_src].astype(BF16)
            dst += n_dst

    xb = x_ref[...].astype(BF16)
    off = 0
    for o_ref in o_refs:
        n = o_ref.shape[-1]
        o_ref[...] = jnp.dot(xb, w_scr[:, off:off + n],
                             preferred_element_type=F32).astype(o_ref.dtype)
        off += n


def _in_proj(x2d, w_in, layer):
    t = x2d.shape[0]
    tm = 1024
    return pl.pallas_call(
        _proj_kernel,
        grid=(t // tm,),
        in_specs=[pl.BlockSpec((tm, D_MODEL), lambda i: (i, 0)),
                  pl.BlockSpec((None, D_MODEL, w_in.shape[2]), lambda i: (layer, 0, 0),
                               pl.Buffered(1))],
        out_specs=[pl.BlockSpec((tm, n), lambda i: (i, 0)) for n in PROJ_WIDTHS],
        out_shape=[jax.ShapeDtypeStruct((t, n), dt) for n, dt in zip(PROJ_WIDTHS, PROJ_DTYPES)],
        scratch_shapes=[pltpu.VMEM((D_MODEL, sum(PROJ_WIDTHS)), BF16)],
        compiler_params=_cparams(("arbitrary",)),
        name="in_proj",
    )(x2d, w_in)


def _conv_kernel(p_ref, w_ref, o_ref):
    u = p_ref[0, :, 0:W_MIX].astype(F32)
    gb = p_ref[0, :, W_MIX:2 * W_MIX].astype(F32)
    gc = p_ref[0, :, 2 * W_MIX:3 * W_MIX].astype(F32)
    cu = gc * u
    row = lax.broadcasted_iota(jnp.int32, cu.shape, 0)
    acc = cu * w_ref[2:3, :]
    for s in (1, 2):
        sh = jnp.where(row >= s, pltpu.roll(cu, s, axis=0), 0.0)
        acc = acc + sh * w_ref[2 - s:3 - s, :]
    o_ref[0] = (gb * acc).astype(o_ref.dtype)


def _conv_mixer(pc, conv_w):
    b, s, _ = pc.shape
    return pl.pallas_call(
        _conv_kernel,
        grid=(b,),
        in_specs=[pl.BlockSpec((1, s, 3 * W_MIX), lambda i: (i, 0, 0)),
                  pl.BlockSpec((3, W_MIX), lambda i: (0, 0))],
        out_specs=pl.BlockSpec((1, s, W_MIX), lambda i: (i, 0, 0)),
        out_shape=jax.ShapeDtypeStruct((b, s, W_MIX), BF16),
        compiler_params=_cparams(("parallel",)),
        name="conv_mixer",
    )(pc, conv_w)


def _gla_setup(p_ref, lr_ref, wlr_ref, blr_ref, ng_ref, o_ref, st_ref):
    c = GLA_CHUNK
    s_len = p_ref.shape[1]
    nh, dk, dv = GLA_HEADS, GLA_DK, GLA_DV
    st_ref[...] = jnp.zeros_like(st_ref)

    rb = GLA_ROWS
    ncb = rb // c
    ri = lax.broadcasted_iota(jnp.int32, (rb, rb), 0)
    ci = lax.broadcasted_iota(jnp.int32, (rb, rb), 1)
    tri = (ci <= ri).astype(BF16)
    klane_head = lax.broadcasted_iota(jnp.int32, (1, nh * dk), 1) // dk
    vlane_head = lax.broadcasted_iota(jnp.int32, (1, nh * dv), 1) // dv
    strow_head = lax.broadcasted_iota(jnp.int32, (nh * dv, 1), 0) // dv
    st_mask = strow_head == klane_head
    r4 = lax.broadcasted_iota(jnp.int32, (nh * c, c), 0) % c
    c4 = lax.broadcasted_iota(jnp.int32, (nh * c, c), 1)
    causal4 = c4 <= r4
    gi = lax.broadcasted_iota(jnp.int32, (nh * dv, nh * dv), 0) // dv
    gj = lax.broadcasted_iota(jnp.int32, (nh * dv, nh * dv), 1) // dv
    gmean = jnp.where(gi == gj, 1.0 / dv, 0.0).astype(BF16)
    wlr_hi, wlr_lo = _split_bf16(wlr_ref[...], 2)

    def one_seq(bb, rows):
        q = p_ref[bb, rows, 0:128].astype(F32) * (dk ** -0.5)
        k = p_ref[bb, rows, 128:256].astype(F32)
        vb = p_ref[bb, rows, 256:512]
        g = p_ref[bb, rows, 512:768].astype(F32)
        lr = lr_ref[bb, rows, :]
        lr_hi, lr_lo = _split_bf16(lr, 2)
        z = (jnp.dot(lr_hi, wlr_hi, preferred_element_type=F32)
             + jnp.dot(lr_hi, wlr_lo, preferred_element_type=F32)
             + jnp.dot(lr_lo, wlr_hi, preferred_element_type=F32)) + blr_ref[...]
        log_a = (jnp.minimum(z, 0.0) - jnp.log(1.0 + jnp.exp(-jnp.abs(z)))) * (1.0 / GLA_TAU)
        cumb = _dot_split_rhs(tri, log_a, 3)
        ends = [cumb[(j + 1) * c - 1:(j + 1) * c, :] for j in range(ncb)]
        starts = [jnp.zeros_like(ends[0])] + ends[:-1]
        cum = cumb - jnp.concatenate([jnp.broadcast_to(s0, (c, nh * dk)) for s0 in starts], axis=0)
        lasts = [e - s0 for e, s0 in zip(ends, starts)]
        cl = jnp.concatenate([jnp.broadcast_to(x, (c, nh * dk)) for x in lasts], axis=0)
        q_dec = q * jnp.exp(cum)
        k_inv = (k * jnp.exp(-cum)).astype(BF16)
        k_end = (k * jnp.exp(cl - cum)).astype(BF16)
        st = st_ref[bb]
        outs = []
        for j in range(ncb):
            sl = slice(j * c, (j + 1) * c)
            qd = q_dec[sl]
            qs = jnp.concatenate([jnp.where(klane_head == h, qd, 0.0) for h in range(nh)],
                                 axis=0).astype(BF16)
            att = jnp.where(causal4, _dot_nt(qs, k_inv[sl]), 0.0)
            r = jnp.dot(att.astype(BF16), vb[sl], preferred_element_type=F32)
            o = jnp.where(vlane_head == 0, r[0:c], 0.0)
            for h in range(1, nh):
                o = o + jnp.where(vlane_head == h, r[h * c:(h + 1) * c], 0.0)
            outs.append(o + _dot_nt(qd.astype(BF16), st.astype(BF16)))
            d_st = _dot_tn(vb[sl], k_end[sl])
            st = st * jnp.exp(lasts[j]) + jnp.where(st_mask, d_st, 0.0)
        st_ref[bb] = st
        o = jnp.concatenate(outs, axis=0)
        ms = _dot_split_lhs(o * o, gmean, 2)
        o = o * lax.rsqrt(ms + RMS_EPS) * ng_ref[...]
        o_ref[bb, rows, :] = (o * (g * _sigmoid(g))).astype(o_ref.dtype)

    return one_seq


def _ssd_setup(p_ref, dt_ref, cw_ref, cb_ref, alog_ref, dtb_ref, dsk_ref, ng_ref, o_ref, st_ref):
    c = SSD_CHUNK
    s_len = p_ref.shape[1]
    n_st = SSD_STATE
    st_ref[...] = jnp.zeros_like(st_ref)

    ri = lax.broadcasted_iota(jnp.int32, (c, c), 0)
    ci = lax.broadcasted_iota(jnp.int32, (c, c), 1)
    causal = ci <= ri
    tri = causal.astype(BF16)
    upper = (ri <= ci).astype(BF16)
    lane_head = lax.broadcasted_iota(jnp.int32, (1, W_MIX), 1) // SSD_HEADDIM
    lane_group = lane_head // (SSD_HEADS // SSD_GROUPS)
    eh = lax.broadcasted_iota(jnp.int32, (LANES, W_MIX), 0)
    el = lax.broadcasted_iota(jnp.int32, (LANES, W_MIX), 1) // SSD_HEADDIM
    expand = (eh == el).astype(BF16)
    row8 = lax.broadcasted_iota(jnp.int32, (8, 3 * W_MIX), 0)
    a_c = -jnp.exp(alog_ref[...])

    def one_chunk(n, bb):
        r0 = pl.multiple_of(n * c, c)
        rows = pl.ds(r0, c)
        cur = p_ref[bb, rows, 256:1024].astype(F32)
        p0 = pl.multiple_of(jnp.maximum(r0 - 2 * SUBLANES, 0), 2 * SUBLANES)
        prev8 = p_ref[bb, pl.ds(p0, 2 * SUBLANES), 256:1024].astype(F32)[SUBLANES:]
        prev8 = jnp.where(n > 0, prev8, 0.0)
        acc = cur * cw_ref[3:4, :] + cb_ref[...]
        for s in (1, 2, 3):
            sh = pltpu.roll(cur, s, axis=0)
            top = jnp.where(row8 < s, pltpu.roll(prev8, s, axis=0), sh[0:8])
            sh = jnp.concatenate([top, sh[8:]], axis=0)
            acc = acc + sh * cw_ref[3 - s:4 - s, :]
        xbc = acc * _sigmoid(acc)
        x = xbc[:, 0:256]
        bm = xbc[:, 256:512].astype(BF16)
        cm = xbc[:, 512:768].astype(BF16)

        dt_c = _softplus(dt_ref[bb, rows, :] + dtb_ref[...])
        da_c = dt_c * a_c
        cum_c = _dot_split_rhs(tri, da_c, 3)
        cum_r = _dot_split_lhs(da_c, upper, 3, dot=_dot_tn)
        both_x = _dot_split_lhs(jnp.concatenate([dt_c, cum_c], axis=0), expand, 3)
        dt_x = both_x[0:c]
        cum_x = both_x[c:2 * c]
        cl_x = cum_x[c - 1:c, :]
        x_dt = x * dt_x
        x_dt_b = x_dt.astype(BF16)
        xw_b = (x_dt * jnp.exp(cl_x - cum_x)).astype(BF16)

        y = x * dsk_ref[...]
        y_off = jnp.zeros((c, W_MIX), F32)
        for g in range(SSD_GROUPS):
            bg = bm[:, g * n_st:(g + 1) * n_st]
            cg = cm[:, g * n_st:(g + 1) * n_st]
            cb = _dot_nt(cg, bg)
            for r in range(SSD_HEADS // SSD_GROUPS):
                h = g * (SSD_HEADS // SSD_GROUPS) + r
                diff = cum_c[:, h:h + 1] - cum_r[h:h + 1, :]
                dec = jnp.exp(jnp.where(causal, diff, -jnp.inf))
                m = (cb * dec).astype(BF16)
                yh = jnp.dot(m, x_dt_b, preferred_element_type=F32)
                y = y + jnp.where(lane_head == h, yh, 0.0)
            st = st_ref[bb, g]
            y_off = y_off + jnp.where(lane_group == g,
                                      jnp.dot(cg, st.astype(BF16), preferred_element_type=F32), 0.0)
            d_st = _dot_tn(bg, xw_b)
            st_ref[bb, g] = st * jnp.exp(cl_x) + jnp.where(lane_group == g, d_st, 0.0)
        y = y + y_off * jnp.exp(cum_x)
        zg = p_ref[bb, rows, 0:256].astype(F32)
        y = y * (zg * _sigmoid(zg))
        outs = []
        for g in range(SSD_GROUPS):
            yg = y[:, g * 128:(g + 1) * 128]
            ms = jnp.mean(yg * yg, axis=-1, keepdims=True)
            outs.append(yg * lax.rsqrt(ms + RMS_EPS))
        o_ref[bb, rows, :] = (jnp.concatenate(outs, axis=-1) * ng_ref[...]).astype(o_ref.dtype)

    return one_chunk


def _recurrent_kernel(pg_ref, lr_ref, wlr_ref, blr_ref, gng_ref,
                      ps_ref, dt_ref, cw_ref, cb_ref, alog_ref, dtb_ref, dsk_ref, sng_ref,
                      og_ref, os_ref, gst_ref, sst_ref):
    gla_rows = _gla_setup(pg_ref, lr_ref, wlr_ref, blr_ref, gng_ref, og_ref, gst_ref)
    ssd_chunk = _ssd_setup(ps_ref, dt_ref, cw_ref, cb_ref, alog_ref, dtb_ref, dsk_ref, sng_ref,
                           os_ref, sst_ref)
    per = GLA_ROWS // SSD_CHUNK

    def body(n, carry):
        rows = pl.ds(pl.multiple_of(n * GLA_ROWS, GLA_ROWS), GLA_ROWS)
        for bb in range(pg_ref.shape[0]):
            gla_rows(bb, rows)
            for j in range(per):
                ssd_chunk(n * per + j, bb)
        return carry

    lax.fori_loop(0, pg_ref.shape[1] // GLA_ROWS, body, 0)


def _recurrent_mixers(pg, plr, w_lr_pad, b_lr, gla_norm_g4,
                      ps, pdt, conv_w, conv_b, a_log_c, dt_bias_c, d_x, ssd_norm_g):
    b, s, _ = pg.shape
    nb = REC_SEQS
    seq = lambda i: (i, 0, 0)
    full2 = lambda i: (0, 0)
    return pl.pallas_call(
        _recurrent_kernel,
        grid=(b // nb,),
        in_specs=[pl.BlockSpec((nb, s, pg.shape[2]), seq),
                  pl.BlockSpec((nb, s, LANES), seq),
                  pl.BlockSpec((LANES, LANES), full2),
                  pl.BlockSpec((1, LANES), full2),
                  pl.BlockSpec((1, W_MIX), full2),
                  pl.BlockSpec((nb, s, ps.shape[2]), seq),
                  pl.BlockSpec((nb, s, LANES), seq),
                  pl.BlockSpec((SSD_CONV_K, 3 * W_MIX), full2),
                  pl.BlockSpec((1, 3 * W_MIX), full2),
                  pl.BlockSpec((1, LANES), full2),
                  pl.BlockSpec((1, LANES), full2),
                  pl.BlockSpec((1, W_MIX), full2),
                  pl.BlockSpec((1, W_MIX), full2)],
        out_specs=[pl.BlockSpec((nb, s, W_MIX), seq), pl.BlockSpec((nb, s, W_MIX), seq)],
        out_shape=[jax.ShapeDtypeStruct((b, s, W_MIX), BF16),
                   jax.ShapeDtypeStruct((b, s, W_MIX), BF16)],
        scratch_shapes=[pltpu.VMEM((nb, GLA_HEADS * GLA_DV, GLA_HEADS * GLA_DK), F32),
                        pltpu.VMEM((nb, SSD_GROUPS, SSD_STATE, W_MIX), F32)],
        compiler_params=_cparams(("parallel",)),
        name="gla_ssd_mixers",
    )(pg, plr, w_lr_pad, b_lr, gla_norm_g4, ps, pdt, conv_w, conv_b, a_log_c, dt_bias_c, d_x,
      ssd_norm_g)


DIFF_TQ = 256
DIFF_TK = 256
LOG2E = 1.4426950408889634
DIFF_VPAD = DIFF_DV + 16


def _diff_kernel(q_ref, k_ref, v_ref, lam_ref, ng_ref, o_ref,
                 vt_ref, qs_ref, st_ref, m_ref, acc_ref, *, lam_init):
    tq, tk = DIFF_TQ, DIFF_TK
    nh, dv = DIFF_HEADS, DIFF_DV
    nhc = 2 * nh
    s_len = k_ref.shape[1]
    i = pl.program_id(1)

    @pl.when(i == 0)
    def _():
        for cblk in range(s_len // tk):
            cols = slice(cblk * tk, (cblk + 1) * tk)
            vt = v_ref[0, cols, :].astype(F32).T.astype(BF16)
            for h in range(nh):
                vt_ref[h, 0:dv, cols] = vt[h * dv:(h + 1) * dv]
        vt_ref[:, dv:, :] = jnp.ones((nh, DIFF_VPAD - dv, s_len), BF16)

    q = q_ref[0].astype(F32) * (DIFF_DQK ** -0.5 * LOG2E)
    qlane = lax.broadcasted_iota(jnp.int32, (1, W_MIX), 1) // DIFF_DQK
    for hc in range(nhc):
        qs_ref[hc * tq:(hc + 1) * tq, :] = jnp.where(qlane == hc, q, 0.0).astype(BF16)
    m_ref[...] = jnp.full_like(m_ref, -jnp.inf)
    acc_ref[...] = jnp.zeros_like(acc_ref)
    krow = lax.broadcasted_iota(jnp.int32, (tk, nhc * tq), 0)
    qcol = lax.broadcasted_iota(jnp.int32, (tk, nhc * tq), 1) % tq
    diag_ok = krow <= qcol

    def scores(j, slot):
        k0 = pl.multiple_of(j * tk, tk)
        st_ref[slot] = _dot_nt(k_ref[0, pl.ds(k0, tk), :], qs_ref[...])

    def softmax_pv(j, slot, masked):
        k0 = pl.multiple_of(j * tk, tk)
        st = st_ref[slot]
        if masked:
            st = jnp.where(diag_ok, st, -jnp.inf)
        m_prev = m_ref[...]
        m_new = jnp.maximum(m_prev, jnp.max(st, axis=0, keepdims=True))
        alpha = jnp.exp2(m_prev - m_new)
        p = jnp.exp2(st - m_new)
        m_ref[...] = m_new
        pb = p.astype(BF16)
        for hc in range(nhc):
            h = hc // 2
            lanes = slice(hc * tq, (hc + 1) * tq)
            pv = jnp.dot(vt_ref[h, :, pl.ds(k0, tk)], pb[:, lanes],
                         preferred_element_type=F32)
            acc_ref[hc] = acc_ref[hc] * alpha[:, lanes] + pv

    scores(0, 0)
    n_pairs = i // 2

    def pair_step(u, carry):
        scores(2 * u + 1, 1)
        softmax_pv(2 * u, 0, False)
        scores(2 * u + 2, 0)
        softmax_pv(2 * u + 1, 1, False)
        return carry

    lax.fori_loop(0, n_pairs, pair_step, 0)

    @pl.when(i % 2 == 0)
    def _():
        softmax_pv(i, 0, True)

    @pl.when(i % 2 == 1)
    def _():
        scores(i, 1)
        softmax_pv(i - 1, 0, False)
        softmax_pv(i, 1, True)

    lam = (jnp.exp(jnp.sum(lam_ref[0:1, :] * lam_ref[1:2, :], axis=-1, keepdims=True))
           - jnp.exp(jnp.sum(lam_ref[2:3, :] * lam_ref[3:4, :], axis=-1, keepdims=True))
           + lam_init)
    heads = []
    for h in range(nh):
        o1 = acc_ref[2 * h, 0:dv] / acc_ref[2 * h, dv:dv + 1]
        o2 = acc_ref[2 * h + 1, 0:dv] / acc_ref[2 * h + 1, dv:dv + 1]
        oh = o1 - lam * o2
        ms = jnp.mean(oh * oh, axis=0, keepdims=True)
        heads.append(oh * lax.rsqrt(ms + RMS_EPS))
    o = jnp.concatenate(heads, axis=0).T
    o_ref[0] = (o * ng_ref[...] * (1.0 - lam_init)).astype(o_ref.dtype)


def _diff_mixer(pd, lam_vecs, norm_g4, lam_init):
    b, s, _ = pd.shape
    tq = DIFF_TQ
    return pl.pallas_call(
        functools.partial(_diff_kernel, lam_init=lam_init),
        grid=(b, s // tq),
        in_specs=[pl.BlockSpec((1, tq, W_MIX), lambda bi, i: (bi, i, 0)),
                  pl.BlockSpec((1, s, W_MIX), lambda bi, i: (bi, 0, 1)),
                  pl.BlockSpec((1, s, W_MIX), lambda bi, i: (bi, 0, 2)),
                  pl.BlockSpec((4, LANES), lambda bi, i: (0, 0)),
                  pl.BlockSpec((1, W_MIX), lambda bi, i: (0, 0))],
        out_specs=pl.BlockSpec((1, tq, W_MIX), lambda bi, i: (bi, i, 0)),
        out_shape=jax.ShapeDtypeStruct((b, s, W_MIX), BF16),
        scratch_shapes=[pltpu.VMEM((DIFF_HEADS, DIFF_VPAD, s), BF16),
                        pltpu.VMEM((2 * DIFF_HEADS * tq, W_MIX), BF16),
                        pltpu.VMEM((2, DIFF_TK, 2 * DIFF_HEADS * tq), F32),
                        pltpu.VMEM((1, 2 * DIFF_HEADS * tq), F32),
                        pltpu.VMEM((2 * DIFF_HEADS, DIFF_VPAD, tq), F32)],
        compiler_params=_cparams(("parallel", "arbitrary")),
        name="diff_attn",
    )(pd, pd, pd, lam_vecs, norm_g4)


def _oproj_kernel(yc_ref, yg_ref, ys_ref, yd_ref, x_ref, wo_ref, g_ref, b_ref, wr_ref,
                  xo_ref, xp_ref, route_ref, cnt_ref):
    mix = jnp.concatenate([yc_ref[...], yg_ref[...], ys_ref[...], yd_ref[...]], axis=-1)
    h = ALPHA * x_ref[...] + jnp.dot(mix, wo_ref[...], preferred_element_type=F32)
    xn = _layer_norm(h, g_ref[...], b_ref[...])
    xo_ref[...] = xn
    xp_ref[...] = _pack_halves(xn)

    xn_hi, xn_lo = _split_bf16(xn, 2)
    both = _dot(xn_hi, wr_ref[...])
    logits = both[:, 0:LANES] + both[:, LANES:2 * LANES] + _dot(xn_lo, wr_ref[:, 0:LANES])
    lane = lax.broadcasted_iota(jnp.int32, logits.shape, 1).astype(F32)
    neg = -jnp.inf
    big = float(LANES)
    lg = jnp.where(lane < N_GROUPS, logits, neg)
    mg = jnp.max(lg, axis=-1, keepdims=True)
    sg = jnp.sum(jnp.exp(lg - mg), axis=-1, keepdims=True)
    grp = jnp.min(jnp.where(lg == mg, lane, big), axis=-1, keepdims=True)
    p_grp = 1.0 / sg
    lo = N_GROUPS + EXPERTS_PER_GROUP * grp
    in_g = jnp.logical_and(lane >= lo, lane < lo + EXPERTS_PER_GROUP)
    le = jnp.where(in_g, logits, neg)
    me = jnp.max(le, axis=-1, keepdims=True)
    ee = jnp.exp(le - me)
    pe = ee / jnp.sum(ee, axis=-1, keepdims=True)
    pe = jnp.where(in_g, pe, -1.0)
    p1 = jnp.max(pe, axis=-1, keepdims=True)
    i1 = jnp.min(jnp.where(pe == p1, lane, big), axis=-1, keepdims=True)
    pe2 = jnp.where(lane == i1, -1.0, pe)
    p2 = jnp.max(pe2, axis=-1, keepdims=True)
    i2 = jnp.min(jnp.where(pe2 == p2, lane, big), axis=-1, keepdims=True)
    den = p1 + p2
    g1 = p_grp * p1 / den
    g2 = p_grp * p2 / den
    e1 = i1 - N_GROUPS
    e2 = i2 - N_GROUPS
    route_ref[...] = jnp.where(lane == 0, e1, jnp.where(lane == 1, e2, jnp.where(
        lane == 2, g1, jnp.where(lane == 3, g2, 0.0))))

    @pl.when(pl.program_id(0) == 0)
    def _():
        cnt_ref[...] = jnp.zeros_like(cnt_ref)

    hits = jnp.where(lane == e1, 1.0, 0.0) + jnp.where(lane == e2, 1.0, 0.0)
    cnt_ref[...] += jnp.sum(hits, axis=0, keepdims=True)


def _out_proj(ys, x2d, w_o, layer, ln_g, ln_b, w_route):
    t = x2d.shape[0]
    tm = 1024
    row = lambda i: (i, 0)
    full = lambda i: (0, 0)
    return pl.pallas_call(
        _oproj_kernel,
        grid=(t // tm,),
        in_specs=[pl.BlockSpec((tm, W_MIX), row)] * 4 + [
            pl.BlockSpec((tm, D_MODEL), row),
            pl.BlockSpec((None, D_MODEL, D_MODEL), lambda i: (layer, 0, 0)),
            pl.BlockSpec((1, D_MODEL), full),
            pl.BlockSpec((1, D_MODEL), full),
            pl.BlockSpec((D_MODEL, 2 * LANES), full)],
        out_specs=[pl.BlockSpec((tm, D_MODEL), row), pl.BlockSpec((tm, D_MODEL // 2), row),
                   pl.BlockSpec((tm, LANES), row), pl.BlockSpec((1, LANES), full)],
        out_shape=[jax.ShapeDtypeStruct((t, D_MODEL), F32),
                   jax.ShapeDtypeStruct((t, D_MODEL // 2), U32),
                   jax.ShapeDtypeStruct((t, LANES), F32),
                   jax.ShapeDtypeStruct((1, LANES), F32)],
        compiler_params=_cparams(("arbitrary",)),
        name="out_proj_ln_router",
    )(*ys, x2d, w_o, ln_g, ln_b, w_route)


PLAN_TILE = 512


def _plan_kernel(route_ref, cnt_ref, dest_ref, meta_ref, carry_ref, pstart_ref):
    tm = route_ref.shape[0]
    lane = lax.broadcasted_iota(jnp.int32, (1, LANES), 1).astype(F32)

    @pl.when(pl.program_id(0) == 0)
    def _():
        cnt = cnt_ref[...]
        padded = jnp.ceil(cnt * (1.0 / MOE_BLK)) * MOE_BLK
        li = lax.broadcasted_iota(jnp.int32, (LANES, LANES), 0)
        lj = lax.broadcasted_iota(jnp.int32, (LANES, LANES), 1)
        before = (li < lj).astype(F32)
        pstart = jnp.dot(jnp.broadcast_to(padded, (8, LANES)), before, precision=HI,
                         preferred_element_type=F32)[0:1]
        pstart_ref[...] = pstart
        carry_ref[...] = jnp.zeros_like(carry_ref)
        meta_ref[...] = jnp.concatenate(
            [pstart + padded, pstart, cnt, jnp.zeros((5, LANES), F32)], axis=0)

    oh0 = jnp.where(lane == route_ref[:, 0:1], 1.0, 0.0)
    oh1 = jnp.where(lane == route_ref[:, 1:2], 1.0, 0.0)
    both = oh0 + oh1
    ri = lax.broadcasted_iota(jnp.int32, (tm, tm), 0)
    ci = lax.broadcasted_iota(jnp.int32, (tm, tm), 1)
    earlier = (ci < ri).astype(BF16)
    base = (jnp.dot(earlier, both.astype(BF16), preferred_element_type=F32)
            + carry_ref[...] + pstart_ref[...])
    d0 = jnp.sum(oh0 * base, axis=-1, keepdims=True)
    d1 = jnp.sum(oh1 * base, axis=-1, keepdims=True)
    dest = jnp.where(lane == 0, d0, jnp.where(lane == 1, d1, 0.0))
    dest_ref[...] = dest.T[0:2, :].astype(jnp.int32)
    carry_ref[...] += jnp.sum(both, axis=0, keepdims=True)


def _dispatch_plan(route, cnt, n_blocks):
    t = route.shape[0]
    tm = PLAN_TILE
    blk = MOE_BLK
    dest, meta = pl.pallas_call(
        _plan_kernel,
        grid=(t // tm,),
        in_specs=[pl.BlockSpec((tm, LANES), lambda i: (i, 0)),
                  pl.BlockSpec((1, LANES), lambda i: (0, 0))],
        out_specs=[pl.BlockSpec((2, tm), lambda i: (0, i)),
                   pl.BlockSpec((8, LANES), lambda i: (0, 0))],
        out_shape=[jax.ShapeDtypeStruct((2, t), jnp.int32),
                   jax.ShapeDtypeStruct((8, LANES), F32)],
        scratch_shapes=[pltpu.VMEM((1, LANES), F32), pltpu.VMEM((1, LANES), F32)],
        compiler_params=_cparams(("arbitrary",)),
        name="moe_plan",
    )(route, cnt)
    meta_i = meta[:, :N_EXPERTS].astype(jnp.int32)
    pad_end, pad_start, seg_end = meta_i[0], meta_i[1], meta_i[1] + meta_i[2]
    starts = jnp.arange(n_blocks, dtype=jnp.int32)[:, None] * blk
    member = jnp.logical_and(starts >= pad_start[None, :], starts < pad_end[None, :])
    expert_ids = jnp.arange(N_EXPERTS, dtype=jnp.int32)[None, :]
    block_e = jnp.where(jnp.any(member, axis=1), jnp.sum(jnp.where(member, expert_ids, 0), axis=1),
                        N_EXPERTS - 1)
    n_used = (pad_end[N_EXPERTS - 1] // blk).reshape(1)
    n_valid = jnp.clip(jnp.sum(jnp.where(member, seg_end[None, :], 0), axis=1) - starts[:, 0],
                       0, blk)
    return dest.reshape(2 * t), block_e, n_used, n_valid


def _ffn_kernel(be_ref, nu_ref, nv_ref, xs_ref, wg_ref, wu_ref, wd_ref, y_ref,
                wgb_ref, wub_ref, wdb_ref):
    i = pl.program_id(0)
    prev = be_ref[jnp.maximum(i - 1, 0)]

    @pl.when(jnp.logical_or(i == 0, be_ref[i] != prev))
    def _():
        wgb_ref[...] = wg_ref[...].astype(BF16)
        wub_ref[...] = wu_ref[...].astype(BF16)
        wdb_ref[...] = wd_ref[...].astype(BF16)

    @pl.when(i < nu_ref[0])
    def _():
        half = xs_ref.shape[0] // 2
        for r in range(2):
            rows = slice(r * half, (r + 1) * half)
            row = lax.broadcasted_iota(jnp.int32, (half, 1), 0) + r * half
            xp = jnp.where(row < nv_ref[i], xs_ref[rows, :], U32(0))
            x_hi, x_lo = _unpack_halves(xp)
            xb = jnp.concatenate([x_hi.astype(BF16), x_lo.astype(BF16)], axis=1)
            a = jnp.dot(xb, wgb_ref[...], preferred_element_type=F32)
            u = jnp.dot(xb, wub_ref[...], preferred_element_type=F32)
            h = (a * _sigmoid(a) * u).astype(BF16)
            y_ref[rows, :] = _pack_halves(jnp.dot(h, wdb_ref[...], preferred_element_type=F32))

    @pl.when(i >= nu_ref[0])
    def _():
        y_ref[...] = jnp.zeros_like(y_ref)


def _expert_ffn(xs, block_e, n_used, n_valid, w_gate, w_up, w_down, layer):
    n_slots = xs.shape[0]
    blk = MOE_BLK
    w_map = lambda i, be, nu, nv: (layer, be[i], 0, 0)
    grid_spec = pltpu.PrefetchScalarGridSpec(
        num_scalar_prefetch=3,
        grid=(n_slots // blk,),
        in_specs=[pl.BlockSpec((blk, D_MODEL // 2),
                               lambda i, be, nu, nv: (jnp.minimum(i, nu[0] - 1), 0)),
                  pl.BlockSpec((None, None, D_MODEL, D_EXPERT), w_map),
                  pl.BlockSpec((None, None, D_MODEL, D_EXPERT), w_map),
                  pl.BlockSpec((None, None, D_EXPERT, D_MODEL), w_map)],
        out_specs=pl.BlockSpec((blk, D_MODEL // 2), lambda i, be, nu, nv: (i, 0)),
        scratch_shapes=[pltpu.VMEM((D_MODEL, D_EXPERT), BF16),
                        pltpu.VMEM((D_MODEL, D_EXPERT), BF16),
                        pltpu.VMEM((D_EXPERT, D_MODEL), BF16)],
    )
    return pl.pallas_call(
        _ffn_kernel,
        grid_spec=grid_spec,
        out_shape=jax.ShapeDtypeStruct((n_slots, D_MODEL // 2), U32),
        compiler_params=_cparams(("arbitrary",)),
        name="expert_ffn",
    )(block_e, n_used, n_valid, xs, w_gate, w_up, w_down)


SC_CORES = 2
SC_SUBCORES = 16
SC_ROWS = 64


def _sc_gather_rows(table, idx):
    b = idx.shape[0]
    d = table.shape[1]
    per_w = b // (SC_CORES * SC_SUBCORES)
    mesh = plsc.VectorSubcoreMesh(core_axis_name="c", subcore_axis_name="s")

    n_chunks = per_w // SC_ROWS

    @functools.partial(
        pl.kernel, mesh=mesh,
        out_type=jax.ShapeDtypeStruct((b, d), table.dtype),
        scratch_types=[pltpu.VMEM((SC_ROWS,), jnp.int32), pltpu.VMEM((SC_ROWS,), jnp.int32),
                       pltpu.VMEM((SC_ROWS, d), table.dtype),
                       pltpu.VMEM((SC_ROWS, d), table.dtype),
                       pltpu.SemaphoreType.DMA, pltpu.SemaphoreType.DMA,
                       pltpu.SemaphoreType.DMA, pltpu.SemaphoreType.DMA],
        name="sc_gather_rows",
    )
    def gather(table_hbm, idx_hbm, out_hbm, idx0, idx1, rows0, rows1, gs0, gs1, ws0, ws1):
        idx_v, rows_v, gsem, wsem = (idx0, idx1), (rows0, rows1), (gs0, gs1), (ws0, ws1)
        wid = lax.axis_index("s") * SC_CORES + lax.axis_index("c")
        base = wid * per_w

        def rows_of(c):
            return pl.ds(pl.multiple_of(base + c * SC_ROWS, SC_ROWS), SC_ROWS)

        def start_gather(c, s):
            pltpu.sync_copy(idx_hbm.at[rows_of(c)], idx_v[s])
            pltpu.async_copy(table_hbm.at[idx_v[s]], rows_v[s], gsem[s])

        def write_back(c, s):
            pltpu.make_async_copy(table_hbm.at[idx_v[s]], rows_v[s], gsem[s]).wait()
            pltpu.async_copy(rows_v[s], out_hbm.at[rows_of(c)], wsem[s]).wait()

        start_gather(0, 0)

        @pl.loop(0, n_chunks, step=2)
        def _(c):
            start_gather(c + 1, 1)
            write_back(c, 0)

            @pl.when(c + 2 < n_chunks)
            def _():
                start_gather(c + 2, 0)

            write_back(c + 1, 1)

    return gather(table, idx)


def _sc_scatter_rows(x2d, idx, n_slots):
    t, d = x2d.shape
    per_w = t // (SC_CORES * SC_SUBCORES)
    mesh = plsc.VectorSubcoreMesh(core_axis_name="c", subcore_axis_name="s")

    @functools.partial(
        pl.kernel, mesh=mesh,
        out_type=jax.ShapeDtypeStruct((n_slots, d), x2d.dtype),
        scratch_types=[pltpu.VMEM((SC_ROWS,), jnp.int32), pltpu.VMEM((SC_ROWS,), jnp.int32),
                       pltpu.VMEM((SC_ROWS, d), x2d.dtype),
                       pltpu.SemaphoreType.DMA, pltpu.SemaphoreType.DMA],
        name="sc_scatter_rows",
    )
    def scatter(x_hbm, idx_hbm, out_hbm, idx0, idx1, rows_v, s0, s1):
        wid = lax.axis_index("s") * SC_CORES + lax.axis_index("c")
        base = wid * per_w

        @pl.loop(0, per_w // SC_ROWS)
        def _(c):
            off = pl.multiple_of(base + c * SC_ROWS, SC_ROWS)
            pltpu.sync_copy(x_hbm.at[pl.ds(off, SC_ROWS)], rows_v)
            pltpu.sync_copy(idx_hbm.at[pl.ds(off, SC_ROWS)], idx0)
            pltpu.sync_copy(idx_hbm.at[pl.ds(t + off, SC_ROWS)], idx1)
            cp0 = pltpu.async_copy(rows_v, out_hbm.at[idx0], s0)
            cp1 = pltpu.async_copy(rows_v, out_hbm.at[idx1], s1)
            cp0.wait()
            cp1.wait()

    return scatter(x2d, idx)


def _combine_dense_kernel(route_ref, x_ref, y0_ref, y1_ref, g_ref, b_ref, o_ref):
    y0 = jnp.concatenate(_unpack_halves(y0_ref[...]), axis=1)
    y1 = jnp.concatenate(_unpack_halves(y1_ref[...]), axis=1)
    moe = route_ref[:, 2:3] * y0 + route_ref[:, 3:4] * y1
    h = ALPHA * x_ref[...] + moe
    o_ref[...] = _layer_norm(h, g_ref[...], b_ref[...])


def _combine_dense(route, x2d, yg, ln_g, ln_b):
    t = x2d.shape[0]
    tm = 1024
    nt = t // tm
    row = lambda i: (i, 0)
    full = lambda i: (0, 0)
    return pl.pallas_call(
        _combine_dense_kernel,
        grid=(nt,),
        in_specs=[pl.BlockSpec((tm, LANES), row),
                  pl.BlockSpec((tm, D_MODEL), row),
                  pl.BlockSpec((tm, D_MODEL // 2), row),
                  pl.BlockSpec((tm, D_MODEL // 2), lambda i: (i + nt, 0)),
                  pl.BlockSpec((1, D_MODEL), full),
                  pl.BlockSpec((1, D_MODEL), full)],
        out_specs=pl.BlockSpec((tm, D_MODEL), row),
        out_shape=jax.ShapeDtypeStruct((t, D_MODEL), F32),
        compiler_params=_cparams(("parallel",)),
        name="moe_combine_dense",
    )(route, x2d, yg, yg, ln_g, ln_b)


def _pad_cols(w, n):
    return jnp.pad(w, [(0, 0)] * (w.ndim - 1) + [(0, n - w.shape[-1])])


def kernel(x, w_in, conv_w, gla_w_lr, gla_b_lr, gla_norm_g, ssd_conv_w, ssd_conv_b, ssd_a_log,
           ssd_d, ssd_dt_bias, ssd_norm_g, diff_lq1, diff_lk1, diff_lq2, diff_lk2, diff_norm_g,
           w_o, ln1_g, ln1_b, router_g, router_e, w_gate, w_up, w_down, ln2_g, ln2_b):
    bsz, seq, d = x.shape
    t = bsz * seq
    n_assign = 2 * t
    n_blocks = (n_assign + N_EXPERTS * (MOE_BLK - 1)) // MOE_BLK + 1
    n_slots = n_blocks * MOE_BLK
    x2d = x.reshape(t, d)
    w_o_b = w_o.astype(BF16)
    for l in range(DEPTH):
        pc, pg, plr, ps, pdt, pd = [p.reshape(bsz, seq, -1) for p in _in_proj(x2d, w_in, l)]

        y_conv = _conv_mixer(pc, conv_w[l])
        w_lr_pad = jnp.pad(gla_w_lr[l], ((0, LANES - GLA_RANK), (0, 0)))
        pad4 = lambda v: jnp.pad(v, (0, LANES - SSD_HEADS)).reshape(1, LANES)
        y_gla, y_ssd = _recurrent_mixers(
            pg, plr, w_lr_pad, gla_b_lr[l].reshape(1, -1),
            jnp.tile(gla_norm_g[l], GLA_HEADS).reshape(1, -1),
            ps, pdt, ssd_conv_w[l], ssd_conv_b[l].reshape(1, -1),
            pad4(ssd_a_log[l]), pad4(ssd_dt_bias[l]),
            jnp.repeat(ssd_d[l], SSD_HEADDIM).reshape(1, -1), ssd_norm_g[l].reshape(1, -1))
        lam_vecs = jnp.pad(jnp.stack([diff_lq1[l], diff_lk1[l], diff_lq2[l], diff_lk2[l]]),
                           ((0, 0), (0, LANES - DIFF_DQK)))
        lam_init = 0.8 - 0.6 * math.exp(-0.3 * l)
        y_diff = _diff_mixer(pd, lam_vecs,
                             jnp.tile(diff_norm_g[l], DIFF_HEADS).reshape(1, -1), lam_init)

        w_route = _pad_cols(jnp.concatenate(
            [router_g[l], router_e[l].reshape(d, N_EXPERTS)], axis=1), LANES)
        w_route_hi = w_route.astype(BF16)
        w_route = jnp.concatenate(
            [w_route_hi, (w_route - w_route_hi.astype(F32)).astype(BF16)], axis=1)
        ys = [y.reshape(t, W_MIX) for y in (y_conv, y_gla, y_ssd, y_diff)]
        xn, xn_p, route, cnt = _out_proj(ys, x2d, w_o_b, l, ln1_g[l].reshape(1, -1),
                                         ln1_b[l].reshape(1, -1), w_route)

        idx, block_e, n_used, n_valid = _dispatch_plan(route, cnt, n_blocks)
        xs = _sc_scatter_rows(xn_p, idx, n_slots)
        y = _expert_ffn(xs, block_e, n_used, n_valid, w_gate, w_up, w_down, l)
        yg = _sc_gather_rows(y, idx)
        x2d = _combine_dense(route, xn, yg, ln2_g[l].reshape(1, -1), ln2_b[l].reshape(1, -1))
    return x2d.reshape(bsz, seq, d)
```

```python
import functools
import math

import jax
import jax.numpy as jnp
from jax import lax
from jax.experimental import pallas as pl
from jax.experimental.pallas import tpu as pltpu
from jax.experimental.pallas import tpu_sc as plsc

F32 = jnp.float32
BF16 = jnp.bfloat16
HI = lax.Precision.HIGHEST

D_MODEL = 1024
DEPTH = 2
W_MIX = 256
GLA_HEADS, GLA_DK, GLA_DV, GLA_RANK, GLA_TAU, GLA_CHUNK = 4, 32, 64, 16, 16.0, 64
GLA_ROWS = 256
REC_SEQS = 2
SSD_HEADS, SSD_GROUPS, SSD_HEADDIM, SSD_STATE, SSD_CONV_K, SSD_CHUNK = 4, 2, 64, 128, 4, 128
DIFF_HEADS, DIFF_DQK, DIFF_DV = 4, 32, 64
N_GROUPS, EXPERTS_PER_GROUP, N_EXPERTS, D_EXPERT = 4, 8, 32, 512
ALPHA = (2 * DEPTH) ** 0.25
LN_EPS = 1e-5
RMS_EPS = 1e-6

LANES = 128
SUBLANES = 8
PROJ_WIDTHS = (768, 768, 128, 1024, 128, 768)
PROJ_SRC_OFFSETS = (0, 768, 1536, 1552, 2576, 2580)
PROJ_SRC_WIDTHS = (768, 768, GLA_RANK, 1024, SSD_HEADS, 768)
PROJ_DTYPES = (BF16, BF16, F32, BF16, F32, BF16)
VMEM_LIMIT = 56 * 1024 * 1024

MOE_BLK = 512


def _cparams(sem):
    return pltpu.CompilerParams(dimension_semantics=sem, vmem_limit_bytes=VMEM_LIMIT)


def _sigmoid(x):
    return 1.0 / (1.0 + jnp.exp(-x))


def _softplus(x):
    return jnp.maximum(x, 0.0) + jnp.log(1.0 + jnp.exp(-jnp.abs(x)))


def _layer_norm(h, g, b):
    mu = jnp.mean(h, axis=-1, keepdims=True)
    d = h - mu
    var = jnp.mean(d * d, axis=-1, keepdims=True)
    return d * lax.rsqrt(var + LN_EPS) * g + b


def _dot_nt(a, b):
    return lax.dot_general(a, b, (((1,), (1,)), ((), ())), preferred_element_type=F32)


def _dot_tn(a, b, precision=None):
    return lax.dot_general(a, b, (((0,), (0,)), ((), ())), preferred_element_type=F32,
                           precision=precision)


def _split_bf16(x, parts):
    out = []
    for _ in range(parts - 1):
        hi = x.astype(BF16)
        out.append(hi)
        x = x - hi.astype(F32)
    out.append(x.astype(BF16))
    return out


def _dot(a, b):
    return jnp.dot(a, b, preferred_element_type=F32)


U32 = jnp.uint32


def _pack_halves(x):
    w = x.shape[1] // 2
    hi = lax.bitcast_convert_type(x[:, :w].astype(BF16).astype(F32), U32)
    lo = lax.bitcast_convert_type(x[:, w:].astype(BF16).astype(F32), U32)
    return hi | lax.shift_right_logical(lo, U32(16))


def _unpack_halves(p):
    hi = lax.bitcast_convert_type(p & U32(0xFFFF0000), F32)
    lo = lax.bitcast_convert_type(lax.shift_left(p, U32(16)), F32)
    return hi, lo


def _dot_split_lhs(a, b_exact, parts, dot=_dot):
    acc = None
    for term in _split_bf16(a, parts):
        d = dot(term, b_exact)
        acc = d if acc is None else acc + d
    return acc


def _dot_split_rhs(a_exact, b, parts):
    acc = None
    for term in _split_bf16(b, parts):
        d = jnp.dot(a_exact, term, preferred_element_type=F32)
        acc = d if acc is None else acc + d
    return acc


def _proj_kernel(x_ref, w_ref, *refs):
    o_refs, w_scr = refs[:-1], refs[-1]

    @pl.when(pl.program_id(0) == 0)
    def _():
        w_scr[...] = jnp.zeros_like(w_scr)
        dst = 0
        for src, n_src, n_dst in zip(PROJ_SRC_OFFSETS, PROJ_SRC_WIDTHS, PROJ_WIDTHS):
            w_scr[:, dst:dst + n_src] = w_ref[:, src:src + n_src].astype(BF16)
            dst += n_dst

    xb = x_ref[...].astype(BF16)
    off = 0
    for o_ref in o_refs:
        n = o_ref.shape[-1]
        o_ref[...] = jnp.dot(xb, w_scr[:, off:off + n],
                             preferred_element_type=F32).astype(o_ref.dtype)
        off += n


def _in_proj(x2d, w_in, layer):
    t = x2d.shape[0]
    tm = 1024
    return pl.pallas_call(
        _proj_kernel,
        grid=(t // tm,),
        in_specs=[pl.BlockSpec((tm, D_MODEL), lambda i: (i, 0)),
                  pl.BlockSpec((None, D_MODEL, w_in.shape[2]), lambda i: (layer, 0, 0),
                               pl.Buffered(1))],
        out_specs=[pl.BlockSpec((tm, n), lambda i: (i, 0)) for n in PROJ_WIDTHS],
        out_shape=[jax.ShapeDtypeStruct((t, n), dt) for n, dt in zip(PROJ_WIDTHS, PROJ_DTYPES)],
        scratch_shapes=[pltpu.VMEM((D_MODEL, sum(PROJ_WIDTHS)), BF16)],
        compiler_params=_cparams(("arbitrary",)),
        name="in_proj",
    )(x2d, w_in)


def _conv_kernel(p_ref, w_ref, o_ref):
    u = p_ref[0, :, 0:W_MIX].astype(F32)
    gb = p_ref[0, :, W_MIX:2 * W_MIX].astype(F32)
    gc = p_ref[0, :, 2 * W_MIX:3 * W_MIX].astype(F32)
    cu = gc * u
    row = lax.broadcasted_iota(jnp.int32, cu.shape, 0)
    acc = cu * w_ref[2:3, :]
    for s in (1, 2):
        sh = jnp.where(row >= s, pltpu.roll(cu, s, axis=0), 0.0)
        acc = acc + sh * w_ref[2 - s:3 - s, :]
    o_ref[0] = (gb * acc).astype(o_ref.dtype)


def _conv_mixer(pc, conv_w):
    b, s, _ = pc.shape
    return pl.pallas_call(
        _conv_kernel,
        grid=(b,),
        in_specs=[pl.BlockSpec((1, s, 3 * W_MIX), lambda i: (i, 0, 0)),
                  pl.BlockSpec((3, W_MIX), lambda i: (0, 0))],
        out_specs=pl.BlockSpec((1, s, W_MIX), lambda i: (i, 0, 0)),
        out_shape=jax.ShapeDtypeStruct((b, s, W_MIX), BF16),
        compiler_params=_cparams(("parallel",)),
        name="conv_mixer",
    )(pc, conv_w)


def _gla_setup(p_ref, lr_ref, wlr_ref, blr_ref, ng_ref, o_ref, st_ref):
    c = GLA_CHUNK
    s_len = p_ref.shape[1]
    nh, dk, dv = GLA_HEADS, GLA_DK, GLA_DV
    st_ref[...] = jnp.zeros_like(st_ref)

    rb = GLA_ROWS
    ncb = rb // c
    ri = lax.broadcasted_iota(jnp.int32, (rb, rb), 0)
    ci = lax.broadcasted_iota(jnp.int32, (rb, rb), 1)
    tri = (ci <= ri).astype(BF16)
    klane_head = lax.broadcasted_iota(jnp.int32, (1, nh * dk), 1) // dk
    vlane_head = lax.broadcasted_iota(jnp.int32, (1, nh * dv), 1) // dv
    strow_head = lax.broadcasted_iota(jnp.int32, (nh * dv, 1), 0) // dv
    st_mask = strow_head == klane_head
    r4 = lax.broadcasted_iota(jnp.int32, (nh * c, c), 0) % c
    c4 = lax.broadcasted_iota(jnp.int32, (nh * c, c), 1)
    causal4 = c4 <= r4
    gi = lax.broadcasted_iota(jnp.int32, (nh * dv, nh * dv), 0) // dv
    gj = lax.broadcasted_iota(jnp.int32, (nh * dv, nh * dv), 1) // dv
    gmean = jnp.where(gi == gj, 1.0 / dv, 0.0).astype(BF16)
    wlr_hi, wlr_lo = _split_bf16(wlr_ref[...], 2)

    def one_seq(bb, rows):
        q = p_ref[bb, rows, 0:128].astype(F32) * (dk ** -0.5)
        k = p_ref[bb, rows, 128:256].astype(F32)
        vb = p_ref[bb, rows, 256:512]
        g = p_ref[bb, rows, 512:768].astype(F32)
        lr = lr_ref[bb, rows, :]
        lr_hi, lr_lo = _split_bf16(lr, 2)
        z = (jnp.dot(lr_hi, wlr_hi, preferred_element_type=F32)
             + jnp.dot(lr_hi, wlr_lo, preferred_element_type=F32)
             + jnp.dot(lr_lo, wlr_hi, preferred_element_type=F32)) + blr_ref[...]
        log_a = (jnp.minimum(z, 0.0) - jnp.log(1.0 + jnp.exp(-jnp.abs(z)))) * (1.0 / GLA_TAU)
        cumb = _dot_split_rhs(tri, log_a, 3)
        ends = [cumb[(j + 1) * c - 1:(j + 1) * c, :] for j in range(ncb)]
        starts = [jnp.zeros_like(ends[0])] + ends[:-1]
        cum = cumb - jnp.concatenate([jnp.broadcast_to(s0, (c, nh * dk)) for s0 in starts], axis=0)
        lasts = [e - s0 for e, s0 in zip(ends, starts)]
        cl = jnp.concatenate([jnp.broadcast_to(x, (c, nh * dk)) for x in lasts], axis=0)
        q_dec = q * jnp.exp(cum)
        k_inv = (k * jnp.exp(-cum)).astype(BF16)
        k_end = (k * jnp.exp(cl - cum)).astype(BF16)
        st = st_ref[bb]
        outs = []
        for j in range(ncb):
            sl = slice(j * c, (j + 1) * c)
            qd = q_dec[sl]
            qs = jnp.concatenate([jnp.where(klane_head == h, qd, 0.0) for h in range(nh)],
                                 axis=0).astype(BF16)
            att = jnp.where(causal4, _dot_nt(qs, k_inv[sl]), 0.0)
            r = jnp.dot(att.astype(BF16), vb[sl], preferred_element_type=F32)
            o = jnp.where(vlane_head == 0, r[0:c], 0.0)
            for h in range(1, nh):
                o = o + jnp.where(vlane_head == h, r[h * c:(h + 1) * c], 0.0)
            outs.append(o + _dot_nt(qd.astype(BF16), st.astype(BF16)))
            d_st = _dot_tn(vb[sl], k_end[sl])
            st = st * jnp.exp(lasts[j]) + jnp.where(st_mask, d_st, 0.0)
        st_ref[bb] = st
        o = jnp.concatenate(outs, axis=0)
        ms = _dot_split_lhs(o * o, gmean, 2)
        o = o * lax.rsqrt(ms + RMS_EPS) * ng_ref[...]
        o_ref[bb, rows, :] = (o * (g * _sigmoid(g))).astype(o_ref.dtype)

    return one_seq


def _ssd_setup(p_ref, dt_ref, cw_ref, cb_ref, alog_ref, dtb_ref, dsk_ref, ng_ref, o_ref, st_ref):
    c = SSD_CHUNK
    s_len = p_ref.shape[1]
    n_st = SSD_STATE
    st_ref[...] = jnp.zeros_like(st_ref)

    ri = lax.broadcasted_iota(jnp.int32, (c, c), 0)
    ci = lax.broadcasted_iota(jnp.int32, (c, c), 1)
    causal = ci <= ri
    tri = causal.astype(BF16)
    upper = (ri <= ci).astype(BF16)
    lane_head = lax.broadcasted_iota(jnp.int32, (1, W_MIX), 1) // SSD_HEADDIM
    lane_group = lane_head // (SSD_HEADS // SSD_GROUPS)
    eh = lax.broadcasted_iota(jnp.int32, (LANES, W_MIX), 0)
    el = lax.broadcasted_iota(jnp.int32, (LANES, W_MIX), 1) // SSD_HEADDIM
    expand = (eh == el).astype(BF16)
    row8 = lax.broadcasted_iota(jnp.int32, (8, 3 * W_MIX), 0)
    a_c = -jnp.exp(alog_ref[...])

    def one_chunk(n, bb):
        r0 = pl.multiple_of(n * c, c)
        rows = pl.ds(r0, c)
        cur = p_ref[bb, rows, 256:1024].astype(F32)
        p0 = pl.multiple_of(jnp.maximum(r0 - 2 * SUBLANES, 0), 2 * SUBLANES)
        prev8 = p_ref[bb, pl.ds(p0, 2 * SUBLANES), 256:1024].astype(F32)[SUBLANES:]
        prev8 = jnp.where(n > 0, prev8, 0.0)
        acc = cur * cw_ref[3:4, :] + cb_ref[...]
        for s in (1, 2, 3):
            sh = pltpu.roll(cur, s, axis=0)
            top = jnp.where(row8 < s, pltpu.roll(prev8, s, axis=0), sh[0:8])
            sh = jnp.concatenate([top, sh[8:]], axis=0)
            acc = acc + sh * cw_ref[3 - s:4 - s, :]
        xbc = acc * _sigmoid(acc)
        x = xbc[:, 0:256]
        bm = xbc[:, 256:512].astype(BF16)
        cm = xbc[:, 512:768].astype(BF16)

        dt_c = _softplus(dt_ref[bb, rows, :] + dtb_ref[...])
        da_c = dt_c * a_c
        cum_c = _dot_split_rhs(tri, da_c, 3)
        cum_r = _dot_split_lhs(da_c, upper, 3, dot=_dot_tn)
        both_x = _dot_split_lhs(jnp.concatenate([dt_c, cum_c], axis=0), expand, 3)
        dt_x = both_x[0:c]
        cum_x = both_x[c:2 * c]
        cl_x = cum_x[c - 1:c, :]
        x_dt = x * dt_x
        x_dt_b = x_dt.astype(BF16)
        xw_b = (x_dt * jnp.exp(cl_x - cum_x)).astype(BF16)

        y = x * dsk_ref[...]
        y_off = jnp.zeros((c, W_MIX), F32)
        for g in range(SSD_GROUPS):
            bg = bm[:, g * n_st:(g + 1) * n_st]
            cg = cm[:, g * n_st:(g + 1) * n_st]
            cb = _dot_nt(cg, bg)
            for r in range(SSD_HEADS // SSD_GROUPS):
                h = g * (SSD_HEADS // SSD_GROUPS) + r
                diff = cum_c[:, h:h + 1] - cum_r[h:h + 1, :]
                dec = jnp.exp(jnp.where(causal, diff, -jnp.inf))
                m = (cb * dec).astype(BF16)
                yh = jnp.dot(m, x_dt_b, preferred_element_type=F32)
                y = y + jnp.where(lane_head == h, yh, 0.0)
            st = st_ref[bb, g]
            y_off = y_off + jnp.where(lane_group == g,
                                      jnp.dot(cg, st.astype(BF16), preferred_element_type=F32), 0.0)
            d_st = _dot_tn(bg, xw_b)
            st_ref[bb, g] = st * jnp.exp(cl_x) + jnp.where(lane_group == g, d_st, 0.0)
        y = y + y_off * jnp.exp(cum_x)
        zg = p_ref[bb, rows, 0:256].astype(F32)
        y = y * (zg * _sigmoid(zg))
        outs = []
        for g in range(SSD_GROUPS):
            yg = y[:, g * 128:(g + 1) * 128]
            ms = jnp.mean(yg * yg, axis=-1, keepdims=True)
            outs.append(yg * lax.rsqrt(ms + RMS_EPS))
        o_ref[bb, rows, :] = (jnp.concatenate(outs, axis=-1) * ng_ref[...]).astype(o_ref.dtype)

    return one_chunk


def _recurrent_kernel(pg_ref, lr_ref, wlr_ref, blr_ref, gng_ref,
                      ps_ref, dt_ref, cw_ref, cb_ref, alog_ref, dtb_ref, dsk_ref, sng_ref,
                      og_ref, os_ref, gst_ref, sst_ref):
    gla_rows = _gla_setup(pg_ref, lr_ref, wlr_ref, blr_ref, gng_ref, og_ref, gst_ref)
    ssd_chunk = _ssd_setup(ps_ref, dt_ref, cw_ref, cb_ref, alog_ref, dtb_ref, dsk_ref, sng_ref,
                           os_ref, sst_ref)
    per = GLA_ROWS // SSD_CHUNK

    def body(n, carry):
        rows = pl.ds(pl.multiple_of(n * GLA_ROWS, GLA_ROWS), GLA_ROWS)
        for bb in range(pg_ref.shape[0]):
            gla_rows(bb, rows)
            for j in range(per):
                ssd_chunk(n * per + j, bb)
        return carry

    lax.fori_loop(0, pg_ref.shape[1] // GLA_ROWS, body, 0)


def _recurrent_mixers(pg, plr, w_lr_pad, b_lr, gla_norm_g4,
                      ps, pdt, conv_w, conv_b, a_log_c, dt_bias_c, d_x, ssd_norm_g):
    b, s, _ = pg.shape
    nb = REC_SEQS
    seq = lambda i: (i, 0, 0)
    full2 = lambda i: (0, 0)
    return pl.pallas_call(
        _recurrent_kernel,
        grid=(b // nb,),
        in_specs=[pl.BlockSpec((nb, s, pg.shape[2]), seq),
                  pl.BlockSpec((nb, s, LANES), seq),
                  pl.BlockSpec((LANES, LANES), full2),
                  pl.BlockSpec((1, LANES), full2),
                  pl.BlockSpec((1, W_MIX), full2),
                  pl.BlockSpec((nb, s, ps.shape[2]), seq),
                  pl.BlockSpec((nb, s, LANES), seq),
                  pl.BlockSpec((SSD_CONV_K, 3 * W_MIX), full2),
                  pl.BlockSpec((1, 3 * W_MIX), full2),
                  pl.BlockSpec((1, LANES), full2),
                  pl.BlockSpec((1, LANES), full2),
                  pl.BlockSpec((1, W_MIX), full2),
                  pl.BlockSpec((1, W_MIX), full2)],
        out_specs=[pl.BlockSpec((nb, s, W_MIX), seq), pl.BlockSpec((nb, s, W_MIX), seq)],
        out_shape=[jax.ShapeDtypeStruct((b, s, W_MIX), BF16),
                   jax.ShapeDtypeStruct((b, s, W_MIX), BF16)],
        scratch_shapes=[pltpu.VMEM((nb, GLA_HEADS * GLA_DV, GLA_HEADS * GLA_DK), F32),
                        pltpu.VMEM((nb, SSD_GROUPS, SSD_STATE, W_MIX), F32)],
        compiler_params=_cparams(("parallel",)),
        name="gla_ssd_mixers",
    )(pg, plr, w_lr_pad, b_lr, gla_norm_g4, ps, pdt, conv_w, conv_b, a_log_c, dt_bias_c, d_x,
      ssd_norm_g)


DIFF_TQ = 256
DIFF_TK = 256
LOG2E = 1.4426950408889634
DIFF_VPAD = DIFF_DV + 16
DIFF_SEQS = 2


def _diff_kernel(q_ref, k_ref, v_ref, lam_ref, ng_ref, o_ref,
                 vt_ref, qs_ref, st_ref, m_ref, acc_ref, *, lam_init):
    tq, tk = DIFF_TQ, DIFF_TK
    nh, dv = DIFF_HEADS, DIFF_DV
    nhc = 2 * nh
    s_len = k_ref.shape[1]
    i = pl.program_id(1)
    seqs = range(q_ref.shape[0])

    @pl.when(i == 0)
    def _():
        for bb in seqs:
            for cblk in range(s_len // tk):
                cols = slice(cblk * tk, (cblk + 1) * tk)
                vt = v_ref[bb, cols, :].astype(F32).T.astype(BF16)
                for h in range(nh):
                    vt_ref[bb, h, 0:dv, cols] = vt[h * dv:(h + 1) * dv]
        vt_ref[:, :, dv:, :] = jnp.ones((len(seqs), nh, DIFF_VPAD - dv, s_len), BF16)

    qlane = lax.broadcasted_iota(jnp.int32, (1, W_MIX), 1) // DIFF_DQK
    for bb in seqs:
        q = q_ref[bb].astype(F32) * (DIFF_DQK ** -0.5 * LOG2E)
        for hc in range(nhc):
            qs_ref[bb, hc * tq:(hc + 1) * tq, :] = jnp.where(qlane == hc, q, 0.0).astype(BF16)
    m_ref[...] = jnp.full_like(m_ref, -jnp.inf)
    acc_ref[...] = jnp.zeros_like(acc_ref)
    krow = lax.broadcasted_iota(jnp.int32, (tk, nhc * tq), 0)
    qcol = lax.broadcasted_iota(jnp.int32, (tk, nhc * tq), 1) % tq
    diag_ok = krow <= qcol

    def scores(j, slot):
        k0 = pl.multiple_of(j * tk, tk)
        for bb in seqs:
            st_ref[bb, slot] = _dot_nt(k_ref[bb, pl.ds(k0, tk), :], qs_ref[bb])

    def softmax_pv(j, slot, masked):
        for bb in seqs:
            softmax_pv_seq(bb, j, slot, masked)

    def softmax_pv_seq(bb, j, slot, masked):
        k0 = pl.multiple_of(j * tk, tk)
        st = st_ref[bb, slot]
        if masked:
            st = jnp.where(diag_ok, st, -jnp.inf)
        m_prev = m_ref[bb]
        m_new = jnp.maximum(m_prev, jnp.max(st, axis=0, keepdims=True))
        alpha = jnp.exp2(m_prev - m_new)
        p = jnp.exp2(st - m_new)
        m_ref[bb] = m_new
        pb = p.astype(BF16)
        for hc in range(nhc):
            h = hc // 2
            lanes = slice(hc * tq, (hc + 1) * tq)
            pv = jnp.dot(vt_ref[bb, h, :, pl.ds(k0, tk)], pb[:, lanes],
                         preferred_element_type=F32)
            acc_ref[bb, hc] = acc_ref[bb, hc] * alpha[:, lanes] + pv

    scores(0, 0)
    n_pairs = i // 2

    def pair_step(u, carry):
        scores(2 * u + 1, 1)
        softmax_pv(2 * u, 0, False)
        scores(2 * u + 2, 0)
        softmax_pv(2 * u + 1, 1, False)
        return carry

    lax.fori_loop(0, n_pairs, pair_step, 0)

    @pl.when(i % 2 == 0)
    def _():
        softmax_pv(i, 0, True)

    @pl.when(i % 2 == 1)
    def _():
        scores(i, 1)
        softmax_pv(i - 1, 0, False)
        softmax_pv(i, 1, True)

    lam = (jnp.exp(jnp.sum(lam_ref[0:1, :] * lam_ref[1:2, :], axis=-1, keepdims=True))
           - jnp.exp(jnp.sum(lam_ref[2:3, :] * lam_ref[3:4, :], axis=-1, keepdims=True))
           + lam_init)
    for bb in seqs:
        heads = []
        for h in range(nh):
            o1 = acc_ref[bb, 2 * h, 0:dv] / acc_ref[bb, 2 * h, dv:dv + 1]
            o2 = acc_ref[bb, 2 * h + 1, 0:dv] / acc_ref[bb, 2 * h + 1, dv:dv + 1]
            oh = o1 - lam * o2
            ms = jnp.mean(oh * oh, axis=0, keepdims=True)
            heads.append(oh * lax.rsqrt(ms + RMS_EPS))
        o = jnp.concatenate(heads, axis=0).T
        o_ref[bb] = (o * ng_ref[...] * (1.0 - lam_init)).astype(o_ref.dtype)


def _diff_mixer(pd, lam_vecs, norm_g4, lam_init):
    b, s, _ = pd.shape
    tq = DIFF_TQ
    nb = DIFF_SEQS
    return pl.pallas_call(
        functools.partial(_diff_kernel, lam_init=lam_init),
        grid=(b // nb, s // tq),
        in_specs=[pl.BlockSpec((nb, tq, W_MIX), lambda bi, i: (bi, i, 0)),
                  pl.BlockSpec((nb, s, W_MIX), lambda bi, i: (bi, 0, 1)),
                  pl.BlockSpec((nb, s, W_MIX), lambda bi, i: (bi, 0, 2)),
                  pl.BlockSpec((4, LANES), lambda bi, i: (0, 0)),
                  pl.BlockSpec((1, W_MIX), lambda bi, i: (0, 0))],
        out_specs=pl.BlockSpec((nb, tq, W_MIX), lambda bi, i: (bi, i, 0)),
        out_shape=jax.ShapeDtypeStruct((b, s, W_MIX), BF16),
        scratch_shapes=[pltpu.VMEM((nb, DIFF_HEADS, DIFF_VPAD, s), BF16),
                        pltpu.VMEM((nb, 2 * DIFF_HEADS * tq, W_MIX), BF16),
                        pltpu.VMEM((nb, 2, DIFF_TK, 2 * DIFF_HEADS * tq), F32),
                        pltpu.VMEM((nb, 1, 2 * DIFF_HEADS * tq), F32),
                        pltpu.VMEM((nb, 2 * DIFF_HEADS, DIFF_VPAD, tq), F32)],
        compiler_params=_cparams(("parallel", "arbitrary")),
        name="diff_attn",
    )(pd, pd, pd, lam_vecs, norm_g4)


def _oproj_kernel(yc_ref, yg_ref, ys_ref, yd_ref, x_ref, wo_ref, g_ref, b_ref, wr_ref,
                  xo_ref, xp_ref, route_ref, cnt_ref):
    mix = jnp.concatenate([yc_ref[...], yg_ref[...], ys_ref[...], yd_ref[...]], axis=-1)
    h = ALPHA * x_ref[...] + jnp.dot(mix, wo_ref[...], preferred_element_type=F32)
    xn = _layer_norm(h, g_ref[...], b_ref[...])
    xo_ref[...] = xn
    xp_ref[...] = _pack_halves(xn)

    xn_hi, xn_lo = _split_bf16(xn, 2)
    both = _dot(xn_hi, wr_ref[...])
    logits = both[:, 0:LANES] + both[:, LANES:2 * LANES] + _dot(xn_lo, wr_ref[:, 0:LANES])
    lane = lax.broadcasted_iota(jnp.int32, logits.shape, 1).astype(F32)
    neg = -jnp.inf
    big = float(LANES)
    lg = jnp.where(lane < N_GROUPS, logits, neg)
    mg = jnp.max(lg, axis=-1, keepdims=True)
    sg = jnp.sum(jnp.exp(lg - mg), axis=-1, keepdims=True)
    grp = jnp.min(jnp.where(lg == mg, lane, big), axis=-1, keepdims=True)
    p_grp = 1.0 / sg
    lo = N_GROUPS + EXPERTS_PER_GROUP * grp
    in_g = jnp.logical_and(lane >= lo, lane < lo + EXPERTS_PER_GROUP)
    le = jnp.where(in_g, logits, neg)
    me = jnp.max(le, axis=-1, keepdims=True)
    ee = jnp.exp(le - me)
    pe = ee / jnp.sum(ee, axis=-1, keepdims=True)
    pe = jnp.where(in_g, pe, -1.0)
    p1 = jnp.max(pe, axis=-1, keepdims=True)
    i1 = jnp.min(jnp.where(pe == p1, lane, big), axis=-1, keepdims=True)
    pe2 = jnp.where(lane == i1, -1.0, pe)
    p2 = jnp.max(pe2, axis=-1, keepdims=True)
    i2 = jnp.min(jnp.where(pe2 == p2, lane, big), axis=-1, keepdims=True)
    den = p1 + p2
    g1 = p_grp * p1 / den
    g2 = p_grp * p2 / den
    e1 = i1 - N_GROUPS
    e2 = i2 - N_GROUPS
    route_ref[...] = jnp.where(lane == 0, e1, jnp.where(lane == 1, e2, jnp.where(
        lane == 2, g1, jnp.where(lane == 3, g2, 0.0))))

    @pl.when(pl.program_id(0) == 0)
    def _():
        cnt_ref[...] = jnp.zeros_like(cnt_ref)

    hits = jnp.where(lane == e1, 1.0, 0.0) + jnp.where(lane == e2, 1.0, 0.0)
    cnt_ref[...] += jnp.sum(hits, axis=0, keepdims=True)


def _out_proj(ys, x2d, w_o, layer, ln_g, ln_b, w_route):
    t = x2d.shape[0]
    tm = 1024
    row = lambda i: (i, 0)
    full = lambda i: (0, 0)
    return pl.pallas_call(
        _oproj_kernel,
        grid=(t // tm,),
        in_specs=[pl.BlockSpec((tm, W_MIX), row)] * 4 + [
            pl.BlockSpec((tm, D_MODEL), row),
            pl.BlockSpec((None, D_MODEL, D_MODEL), lambda i: (layer, 0, 0)),
            pl.BlockSpec((1, D_MODEL), full),
            pl.BlockSpec((1, D_MODEL), full),
            pl.BlockSpec((D_MODEL, 2 * LANES), full)],
        out_specs=[pl.BlockSpec((tm, D_MODEL), row), pl.BlockSpec((tm, D_MODEL // 2), row),
                   pl.BlockSpec((tm, LANES), row), pl.BlockSpec((1, LANES), full)],
        out_shape=[jax.ShapeDtypeStruct((t, D_MODEL), F32),
                   jax.ShapeDtypeStruct((t, D_MODEL // 2), U32),
                   jax.ShapeDtypeStruct((t, LANES), F32),
                   jax.ShapeDtypeStruct((1, LANES), F32)],
        compiler_params=_cparams(("arbitrary",)),
        name="out_proj_ln_router",
    )(*ys, x2d, w_o, ln_g, ln_b, w_route)


PLAN_TILE = 512


def _plan_kernel(route_ref, cnt_ref, dest_ref, meta_ref, carry_ref, pstart_ref):
    tm = route_ref.shape[0]
    lane = lax.broadcasted_iota(jnp.int32, (1, LANES), 1).astype(F32)

    @pl.when(pl.program_id(0) == 0)
    def _():
        cnt = cnt_ref[...]
        padded = jnp.ceil(cnt * (1.0 / MOE_BLK)) * MOE_BLK
        li = lax.broadcasted_iota(jnp.int32, (LANES, LANES), 0)
        lj = lax.broadcasted_iota(jnp.int32, (LANES, LANES), 1)
        before = (li < lj).astype(F32)
        pstart = jnp.dot(jnp.broadcast_to(padded, (8, LANES)), before, precision=HI,
                         preferred_element_type=F32)[0:1]
        pstart_ref[...] = pstart
        carry_ref[...] = jnp.zeros_like(carry_ref)
        meta_ref[...] = jnp.concatenate(
            [pstart + padded, pstart, cnt, jnp.zeros((5, LANES), F32)], axis=0)

    oh0 = jnp.where(lane == route_ref[:, 0:1], 1.0, 0.0)
    oh1 = jnp.where(lane == route_ref[:, 1:2], 1.0, 0.0)
    both = oh0 + oh1
    ri = lax.broadcasted_iota(jnp.int32, (tm, tm), 0)
    ci = lax.broadcasted_iota(jnp.int32, (tm, tm), 1)
    earlier = (ci < ri).astype(BF16)
    base = (jnp.dot(earlier, both.astype(BF16), preferred_element_type=F32)
            + carry_ref[...] + pstart_ref[...])
    d0 = jnp.sum(oh0 * base, axis=-1, keepdims=True)
    d1 = jnp.sum(oh1 * base, axis=-1, keepdims=True)
    dest = jnp.where(lane == 0, d0, jnp.where(lane == 1, d1, 0.0))
    dest_ref[...] = dest.T[0:2, :].astype(jnp.int32)
    carry_ref[...] += jnp.sum(both, axis=0, keepdims=True)


def _dispatch_plan(route, cnt, n_blocks):
    t = route.shape[0]
    tm = PLAN_TILE
    blk = MOE_BLK
    dest, meta = pl.pallas_call(
        _plan_kernel,
        grid=(t // tm,),
        in_specs=[pl.BlockSpec((tm, LANES), lambda i: (i, 0)),
                  pl.BlockSpec((1, LANES), lambda i: (0, 0))],
        out_specs=[pl.BlockSpec((2, tm), lambda i: (0, i)),
                   pl.BlockSpec((8, LANES), lambda i: (0, 0))],
        out_shape=[jax.ShapeDtypeStruct((2, t), jnp.int32),
                   jax.ShapeDtypeStruct((8, LANES), F32)],
        scratch_shapes=[pltpu.VMEM((1, LANES), F32), pltpu.VMEM((1, LANES), F32)],
        compiler_params=_cparams(("arbitrary",)),
        name="moe_plan",
    )(route, cnt)
    meta_i = meta[:, :N_EXPERTS].astype(jnp.int32)
    pad_end, pad_start, seg_end = meta_i[0], meta_i[1], meta_i[1] + meta_i[2]
    starts = jnp.arange(n_blocks, dtype=jnp.int32)[:, None] * blk
    member = jnp.logical_and(starts >= pad_start[None, :], starts < pad_end[None, :])
    expert_ids = jnp.arange(N_EXPERTS, dtype=jnp.int32)[None, :]
    block_e = jnp.where(jnp.any(member, axis=1), jnp.sum(jnp.where(member, expert_ids, 0), axis=1),
                        N_EXPERTS - 1)
    n_used = (pad_end[N_EXPERTS - 1] // blk).reshape(1)
    n_valid = jnp.clip(jnp.sum(jnp.where(member, seg_end[None, :], 0), axis=1) - starts[:, 0],
                       0, blk)
    return dest.reshape(2 * t), block_e, n_used, n_valid


def _ffn_kernel(be_ref, nu_ref, nv_ref, xs_ref, wg_ref, wu_ref, wd_ref, y_ref,
                wgb_ref, wub_ref, wdb_ref):
    i = pl.program_id(0)
    prev = be_ref[jnp.maximum(i - 1, 0)]

    @pl.when(jnp.logical_or(i == 0, be_ref[i] != prev))
    def _():
        wgb_ref[...] = wg_ref[...].astype(BF16)
        wub_ref[...] = wu_ref[...].astype(BF16)
        wdb_ref[...] = wd_ref[...].astype(BF16)

    @pl.when(i < nu_ref[0])
    def _():
        half = xs_ref.shape[0] // 2
        for r in range(2):
            rows = slice(r * half, (r + 1) * half)
            row = lax.broadcasted_iota(jnp.int32, (half, 1), 0) + r * half
            xp = jnp.where(row < nv_ref[i], xs_ref[rows, :], U32(0))
            x_hi, x_lo = _unpack_halves(xp)
            xb = jnp.concatenate([x_hi.astype(BF16), x_lo.astype(BF16)], axis=1)
            a = jnp.dot(xb, wgb_ref[...], preferred_element_type=F32)
            u = jnp.dot(xb, wub_ref[...], preferred_element_type=F32)
            h = (a * _sigmoid(a) * u).astype(BF16)
            y_ref[rows, :] = _pack_halves(jnp.dot(h, wdb_ref[...], preferred_element_type=F32))

    @pl.when(i >= nu_ref[0])
    def _():
        y_ref[...] = jnp.zeros_like(y_ref)


def _expert_ffn(xs, block_e, n_used, n_valid, w_gate, w_up, w_down, layer):
    n_slots = xs.shape[0]
    blk = MOE_BLK
    w_map = lambda i, be, nu, nv: (layer, be[i], 0, 0)
    grid_spec = pltpu.PrefetchScalarGridSpec(
        num_scalar_prefetch=3,
        grid=(n_slots // blk,),
        in_specs=[pl.BlockSpec((blk, D_MODEL // 2),
                               lambda i, be, nu, nv: (jnp.minimum(i, nu[0] - 1), 0)),
                  pl.BlockSpec((None, None, D_MODEL, D_EXPERT), w_map),
                  pl.BlockSpec((None, None, D_MODEL, D_EXPERT), w_map),
                  pl.BlockSpec((None, None, D_EXPERT, D_MODEL), w_map)],
        out_specs=pl.BlockSpec((blk, D_MODEL // 2), lambda i, be, nu, nv: (i, 0)),
        scratch_shapes=[pltpu.VMEM((D_MODEL, D_EXPERT), BF16),
                        pltpu.VMEM((D_MODEL, D_EXPERT), BF16),
                        pltpu.VMEM((D_EXPERT, D_MODEL), BF16)],
    )
    return pl.pallas_call(
        _ffn_kernel,
        grid_spec=grid_spec,
        out_shape=jax.ShapeDtypeStruct((n_slots, D_MODEL // 2), U32),
        compiler_params=_cparams(("arbitrary",)),
        name="expert_ffn",
    )(block_e, n_used, n_valid, xs, w_gate, w_up, w_down)


SC_CORES = 2
SC_SUBCORES = 16
SC_ROWS = 64


def _sc_gather_rows(table, idx):
    b = idx.shape[0]
    d = table.shape[1]
    per_w = b // (SC_CORES * SC_SUBCORES)
    mesh = plsc.VectorSubcoreMesh(core_axis_name="c", subcore_axis_name="s")

    n_chunks = per_w // SC_ROWS

    @functools.partial(
        pl.kernel, mesh=mesh,
        out_type=jax.ShapeDtypeStruct((b, d), table.dtype),
        scratch_types=[pltpu.VMEM((SC_ROWS,), jnp.int32), pltpu.VMEM((SC_ROWS,), jnp.int32),
                       pltpu.VMEM((SC_ROWS, d), table.dtype),
                       pltpu.VMEM((SC_ROWS, d), table.dtype),
                       pltpu.SemaphoreType.DMA, pltpu.SemaphoreType.DMA,
                       pltpu.SemaphoreType.DMA, pltpu.SemaphoreType.DMA],
        name="sc_gather_rows",
    )
    def gather(table_hbm, idx_hbm, out_hbm, idx0, idx1, rows0, rows1, gs0, gs1, ws0, ws1):
        idx_v, rows_v, gsem, wsem = (idx0, idx1), (rows0, rows1), (gs0, gs1), (ws0, ws1)
        wid = lax.axis_index("s") * SC_CORES + lax.axis_index("c")
        base = wid * per_w

        def rows_of(c):
            return pl.ds(pl.multiple_of(base + c * SC_ROWS, SC_ROWS), SC_ROWS)

        def start_gather(c, s):
            pltpu.sync_copy(idx_hbm.at[rows_of(c)], idx_v[s])
            pltpu.async_copy(table_hbm.at[idx_v[s]], rows_v[s], gsem[s])

        def write_back(c, s):
            pltpu.make_async_copy(table_hbm.at[idx_v[s]], rows_v[s], gsem[s]).wait()
            pltpu.async_copy(rows_v[s], out_hbm.at[rows_of(c)], wsem[s]).wait()

        start_gather(0, 0)

        @pl.loop(0, n_chunks, step=2)
        def _(c):
            start_gather(c + 1, 1)
            write_back(c, 0)

            @pl.when(c + 2 < n_chunks)
            def _():
                start_gather(c + 2, 0)

            write_back(c + 1, 1)

    return gather(table, idx)


def _sc_scatter_rows(x2d, idx, n_slots):
    t, d = x2d.shape
    per_w = t // (SC_CORES * SC_SUBCORES)
    mesh = plsc.VectorSubcoreMesh(core_axis_name="c", subcore_axis_name="s")

    @functools.partial(
        pl.kernel, mesh=mesh,
        out_type=jax.ShapeDtypeStruct((n_slots, d), x2d.dtype),
        scratch_types=[pltpu.VMEM((SC_ROWS,), jnp.int32), pltpu.VMEM((SC_ROWS,), jnp.int32),
                       pltpu.VMEM((SC_ROWS, d), x2d.dtype),
                       pltpu.SemaphoreType.DMA, pltpu.SemaphoreType.DMA],
        name="sc_scatter_rows",
    )
    def scatter(x_hbm, idx_hbm, out_hbm, idx0, idx1, rows_v, s0, s1):
        wid = lax.axis_index("s") * SC_CORES + lax.axis_index("c")
        base = wid * per_w

        @pl.loop(0, per_w // SC_ROWS)
        def _(c):
            off = pl.multiple_of(base + c * SC_ROWS, SC_ROWS)
            pltpu.sync_copy(x_hbm.at[pl.ds(off, SC_ROWS)], rows_v)
            pltpu.sync_copy(idx_hbm.at[pl.ds(off, SC_ROWS)], idx0)
            pltpu.sync_copy(idx_hbm.at[pl.ds(t + off, SC_ROWS)], idx1)
            cp0 = pltpu.async_copy(rows_v, out_hbm.at[idx0], s0)
            cp1 = pltpu.async_copy(rows_v, out_hbm.at[idx1], s1)
            cp0.wait()
            cp1.wait()

    return scatter(x2d, idx)


def _combine_dense_kernel(route_ref, x_ref, y0_ref, y1_ref, g_ref, b_ref, o_ref):
    y0 = jnp.concatenate(_unpack_halves(y0_ref[...]), axis=1)
    y1 = jnp.concatenate(_unpack_halves(y1_ref[...]), axis=1)
    moe = route_ref[:, 2:3] * y0 + route_ref[:, 3:4] * y1
    h = ALPHA * x_ref[...] + moe
    o_ref[...] = _layer_norm(h, g_ref[...], b_ref[...])


def _combine_dense(route, x2d, yg, ln_g, ln_b):
    t = x2d.shape[0]
    tm = 1024
    nt = t // tm
    row = lambda i: (i, 0)
    full = lambda i: (0, 0)
    return pl.pallas_call(
        _combine_dense_kernel,
        grid=(nt,),
        in_specs=[pl.BlockSpec((tm, LANES), row),
                  pl.BlockSpec((tm, D_MODEL), row),
                  pl.BlockSpec((tm, D_MODEL // 2), row),
                  pl.BlockSpec((tm, D_MODEL // 2), lambda i: (i + nt, 0)),
                  pl.BlockSpec((1, D_MODEL), full),
                  pl.BlockSpec((1, D_MODEL), full)],
        out_specs=pl.BlockSpec((tm, D_MODEL), row),
        out_shape=jax.ShapeDtypeStruct((t, D_MODEL), F32),
        compiler_params=_cparams(("parallel",)),
        name="moe_combine_dense",
    )(route, x2d, yg, yg, ln_g, ln_b)


def _pad_cols(w, n):
    return jnp.pad(w, [(0, 0)] * (w.ndim - 1) + [(0, n - w.shape[-1])])


def kernel(x, w_in, conv_w, gla_w_lr, gla_b_lr, gla_norm_g, ssd_conv_w, ssd_conv_b, ssd_a_log,
           ssd_d, ssd_dt_bias, ssd_norm_g, diff_lq1, diff_lk1, diff_lq2, diff_lk2, diff_norm_g,
           w_o, ln1_g, ln1_b, router_g, router_e, w_gate, w_up, w_down, ln2_g, ln2_b):
    bsz, seq, d = x.shape
    t = bsz * seq
    n_assign = 2 * t
    n_blocks = (n_assign + N_EXPERTS * (MOE_BLK - 1)) // MOE_BLK + 1
    n_slots = n_blocks * MOE_BLK
    x2d = x.reshape(t, d)
    w_o_b = w_o.astype(BF16)
    for l in range(DEPTH):
        pc, pg, plr, ps, pdt, pd = [p.reshape(bsz, seq, -1) for p in _in_proj(x2d, w_in, l)]

        y_conv = _conv_mixer(pc, conv_w[l])
        w_lr_pad = jnp.pad(gla_w_lr[l], ((0, LANES - GLA_RANK), (0, 0)))
        pad4 = lambda v: jnp.pad(v, (0, LANES - SSD_HEADS)).reshape(1, LANES)
        y_gla, y_ssd = _recurrent_mixers(
            pg, plr, w_lr_pad, gla_b_lr[l].reshape(1, -1),
            jnp.tile(gla_norm_g[l], GLA_HEADS).reshape(1, -1),
            ps, pdt, ssd_conv_w[l], ssd_conv_b[l].reshape(1, -1),
            pad4(ssd_a_log[l]), pad4(ssd_dt_bias[l]),
            jnp.repeat(ssd_d[l], SSD_HEADDIM).reshape(1, -1), ssd_norm_g[l].reshape(1, -1))
        lam_vecs = jnp.pad(jnp.stack([diff_lq1[l], diff_lk1[l], diff_lq2[l], diff_lk2[l]]),
                           ((0, 0), (0, LANES - DIFF_DQK)))
        lam_init = 0.8 - 0.6 * math.exp(-0.3 * l)
        y_diff = _diff_mixer(pd, lam_vecs,
                             jnp.tile(diff_norm_g[l], DIFF_HEADS).reshape(1, -1), lam_init)

        w_route = _pad_cols(jnp.concatenate(
            [router_g[l], router_e[l].reshape(d, N_EXPERTS)], axis=1), LANES)
        w_route_hi = w_route.astype(BF16)
        w_route = jnp.concatenate(
            [w_route_hi, (w_route - w_route_hi.astype(F32)).astype(BF16)], axis=1)
        ys = [y.reshape(t, W_MIX) for y in (y_conv, y_gla, y_ssd, y_diff)]
        xn, xn_p, route, cnt = _out_proj(ys, x2d, w_o_b, l, ln1_g[l].reshape(1, -1),
                                         ln1_b[l].reshape(1, -1), w_route)

        idx, block_e, n_used, n_valid = _dispatch_plan(route, cnt, n_blocks)
        xs = _sc_scatter_rows(xn_p, idx, n_slots)
        y = _expert_ffn(xs, block_e, n_used, n_valid, w_gate, w_up, w_down, l)
        yg = _sc_gather_rows(y, idx)
        x2d = _combine_dense(route, xn, yg, ln2_g[l].reshape(1, -1), ln2_b[l].reshape(1, -1))
    return x2d.reshape(bsz, seq, d)
```

```python
import functools
import math

import jax
import jax.numpy as jnp
from jax import lax
from jax.experimental import pallas as pl
from jax.experimental.pallas import tpu as pltpu
from jax.experimental.pallas import tpu_sc as plsc

F32 = jnp.float32
BF16 = jnp.bfloat16
HI = lax.Precision.HIGHEST

D_MODEL = 1024
DEPTH = 2
W_MIX = 256
GLA_HEADS, GLA_DK, GLA_DV, GLA_RANK, GLA_TAU, GLA_CHUNK = 4, 32, 64, 16, 16.0, 64
GLA_ROWS = 256
REC_SEQS = 2
SSD_HEADS, SSD_GROUPS, SSD_HEADDIM, SSD_STATE, SSD_CONV_K, SSD_CHUNK = 4, 2, 64, 128, 4, 128
DIFF_HEADS, DIFF_DQK, DIFF_DV = 4, 32, 64
N_GROUPS, EXPERTS_PER_GROUP, N_EXPERTS, D_EXPERT = 4, 8, 32, 512
ALPHA = (2 * DEPTH) ** 0.25
LN_EPS = 1e-5
RMS_EPS = 1e-6

LANES = 128
SUBLANES = 8
PROJ_WIDTHS = (768, 768, 128, 1024, 128, 768)
PROJ_SRC_OFFSETS = (0, 768, 1536, 1552, 2576, 2580)
PROJ_SRC_WIDTHS = (768, 768, GLA_RANK, 1024, SSD_HEADS, 768)
PROJ_DTYPES = (BF16, BF16, F32, BF16, F32, BF16)
VMEM_LIMIT = 56 * 1024 * 1024

MOE_BLK = 512


def _cparams(sem):
    return pltpu.CompilerParams(dimension_semantics=sem, vmem_limit_bytes=VMEM_LIMIT)


def _sigmoid(x):
    return 1.0 / (1.0 + jnp.exp(-x))


def _softplus(x):
    return jnp.maximum(x, 0.0) + jnp.log(1.0 + jnp.exp(-jnp.abs(x)))


def _layer_norm(h, g, b):
    mu = jnp.mean(h, axis=-1, keepdims=True)
    d = h - mu
    var = jnp.mean(d * d, axis=-1, keepdims=True)
    return d * lax.rsqrt(var + LN_EPS) * g + b


def _dot_nt(a, b):
    return lax.dot_general(a, b, (((1,), (1,)), ((), ())), preferred_element_type=F32)


def _dot_tn(a, b, precision=None):
    return lax.dot_general(a, b, (((0,), (0,)), ((), ())), preferred_element_type=F32,
                           precision=precision)


def _split_bf16(x, parts):
    out = []
    for _ in range(parts - 1):
        hi = x.astype(BF16)
        out.append(hi)
        x = x - hi.astype(F32)
    out.append(x.astype(BF16))
    return out


def _dot(a, b):
    return jnp.dot(a, b, preferred_element_type=F32)


U32 = jnp.uint32


def _pack_halves(x):
    w = x.shape[1] // 2
    hi = lax.bitcast_convert_type(x[:, :w].astype(BF16).astype(F32), U32)
    lo = lax.bitcast_convert_type(x[:, w:].astype(BF16).astype(F32), U32)
    return hi | lax.shift_right_logical(lo, U32(16))


def _unpack_halves(p):
    hi = lax.bitcast_convert_type(p & U32(0xFFFF0000), F32)
    lo = lax.bitcast_convert_type(lax.shift_left(p, U32(16)), F32)
    return hi, lo


def _dot_split_lhs(a, b_exact, parts, dot=_dot):
    acc = None
    for term in _split_bf16(a, parts):
        d = dot(term, b_exact)
        acc = d if acc is None else acc + d
    return acc


def _dot_split_rhs(a_exact, b, parts):
    acc = None
    for term in _split_bf16(b, parts):
        d = jnp.dot(a_exact, term, preferred_element_type=F32)
        acc = d if acc is None else acc + d
    return acc


def _proj_kernel(x_ref, w_ref, *refs):
    o_refs, w_scr = refs[:-1], refs[-1]

    @pl.when(pl.program_id(0) == 0)
    def _():
        w_scr[...] = jnp.zeros_like(w_scr)
        dst = 0
        for src, n_src, n_dst in zip(PROJ_SRC_OFFSETS, PROJ_SRC_WIDTHS, PROJ_WIDTHS):
            w_scr[:, dst:dst + n_src] = w_ref[:, src:src + n_src].astype(BF16)
            dst += n_dst

    xb = x_ref[...].astype(BF16)
    off = 0
    for o_ref in o_refs:
        n = o_ref.shape[-1]
        o_ref[...] = jnp.dot(xb, w_scr[:, off:off + n],
                             preferred_element_type=F32).astype(o_ref.dtype)
        off += n


def _in_proj(x2d, w_in, layer):
    t = x2d.shape[0]
    tm = 1024
    return pl.pallas_call(
        _proj_kernel,
        grid=(t // tm,),
        in_specs=[pl.BlockSpec((tm, D_MODEL), lambda i: (i, 0)),
                  pl.BlockSpec((None, D_MODEL, w_in.shape[2]), lambda i: (layer, 0, 0),
                               pl.Buffered(1))],
        out_specs=[pl.BlockSpec((tm, n), lambda i: (i, 0)) for n in PROJ_WIDTHS],
        out_shape=[jax.ShapeDtypeStruct((t, n), dt) for n, dt in zip(PROJ_WIDTHS, PROJ_DTYPES)],
        scratch_shapes=[pltpu.VMEM((D_MODEL, sum(PROJ_WIDTHS)), BF16)],
        compiler_params=_cparams(("arbitrary",)),
        name="in_proj",
    )(x2d, w_in)


def _conv_kernel(p_ref, w_ref, o_ref):
    u = p_ref[0, :, 0:W_MIX].astype(F32)
    gb = p_ref[0, :, W_MIX:2 * W_MIX].astype(F32)
    gc = p_ref[0, :, 2 * W_MIX:3 * W_MIX].astype(F32)
    cu = gc * u
    row = lax.broadcasted_iota(jnp.int32, cu.shape, 0)
    acc = cu * w_ref[2:3, :]
    for s in (1, 2):
        sh = jnp.where(row >= s, pltpu.roll(cu, s, axis=0), 0.0)
        acc = acc + sh * w_ref[2 - s:3 - s, :]
    o_ref[0] = (gb * acc).astype(o_ref.dtype)


def _conv_mixer(pc, conv_w):
    b, s, _ = pc.shape
    return pl.pallas_call(
        _conv_kernel,
        grid=(b,),
        in_specs=[pl.BlockSpec((1, s, 3 * W_MIX), lambda i: (i, 0, 0)),
                  pl.BlockSpec((3, W_MIX), lambda i: (0, 0))],
        out_specs=pl.BlockSpec((1, s, W_MIX), lambda i: (i, 0, 0)),
        out_shape=jax.ShapeDtypeStruct((b, s, W_MIX), BF16),
        compiler_params=_cparams(("parallel",)),
        name="conv_mixer",
    )(pc, conv_w)


def _gla_setup(p_ref, lr_ref, wlr_ref, blr_ref, ng_ref, o_ref, st_ref):
    c = GLA_CHUNK
    s_len = p_ref.shape[1]
    nh, dk, dv = GLA_HEADS, GLA_DK, GLA_DV
    st_ref[...] = jnp.zeros_like(st_ref)

    rb = GLA_ROWS
    ncb = rb // c
    ri = lax.broadcasted_iota(jnp.int32, (rb, rb), 0)
    ci = lax.broadcasted_iota(jnp.int32, (rb, rb), 1)
    tri = (ci <= ri).astype(BF16)
    klane_head = lax.broadcasted_iota(jnp.int32, (1, nh * dk), 1) // dk
    vlane_head = lax.broadcasted_iota(jnp.int32, (1, nh * dv), 1) // dv
    strow_head = lax.broadcasted_iota(jnp.int32, (nh * dv, 1), 0) // dv
    st_mask = strow_head == klane_head
    r4 = lax.broadcasted_iota(jnp.int32, (nh * c, c), 0) % c
    c4 = lax.broadcasted_iota(jnp.int32, (nh * c, c), 1)
    causal4 = c4 <= r4
    gi = lax.broadcasted_iota(jnp.int32, (nh * dv, nh * dv), 0) // dv
    gj = lax.broadcasted_iota(jnp.int32, (nh * dv, nh * dv), 1) // dv
    gmean = jnp.where(gi == gj, 1.0 / dv, 0.0).astype(BF16)
    wlr_hi, wlr_lo = _split_bf16(wlr_ref[...], 2)

    def one_seq(bb, rows):
        q = p_ref[bb, rows, 0:128].astype(F32) * (dk ** -0.5)
        k = p_ref[bb, rows, 128:256].astype(F32)
        vb = p_ref[bb, rows, 256:512]
        g = p_ref[bb, rows, 512:768].astype(F32)
        lr = lr_ref[bb, rows, :]
        lr_hi, lr_lo = _split_bf16(lr, 2)
        z = (jnp.dot(lr_hi, wlr_hi, preferred_element_type=F32)
             + jnp.dot(lr_hi, wlr_lo, preferred_element_type=F32)
             + jnp.dot(lr_lo, wlr_hi, preferred_element_type=F32)) + blr_ref[...]
        log_a = (jnp.minimum(z, 0.0) - jnp.log(1.0 + jnp.exp(-jnp.abs(z)))) * (1.0 / GLA_TAU)
        cumb = _dot_split_rhs(tri, log_a, 3)
        ends = [cumb[(j + 1) * c - 1:(j + 1) * c, :] for j in range(ncb)]
        starts = [jnp.zeros_like(ends[0])] + ends[:-1]
        cum = cumb - jnp.concatenate([jnp.broadcast_to(s0, (c, nh * dk)) for s0 in starts], axis=0)
        lasts = [e - s0 for e, s0 in zip(ends, starts)]
        cl = jnp.concatenate([jnp.broadcast_to(x, (c, nh * dk)) for x in lasts], axis=0)
        q_dec = q * jnp.exp(cum)
        k_inv = (k * jnp.exp(-cum)).astype(BF16)
        k_end = (k * jnp.exp(cl - cum)).astype(BF16)
        st = st_ref[bb]
        outs = []
        for j in range(ncb):
            sl = slice(j * c, (j + 1) * c)
            qd = q_dec[sl]
            qs = jnp.concatenate([jnp.where(klane_head == h, qd, 0.0) for h in range(nh)],
                                 axis=0).astype(BF16)
            att = jnp.where(causal4, _dot_nt(qs, k_inv[sl]), 0.0)
            r = jnp.dot(att.astype(BF16), vb[sl], preferred_element_type=F32)
            o = jnp.where(vlane_head == 0, r[0:c], 0.0)
            for h in range(1, nh):
                o = o + jnp.where(vlane_head == h, r[h * c:(h + 1) * c], 0.0)
            outs.append(o + _dot_nt(qd.astype(BF16), st.astype(BF16)))
            d_st = _dot_tn(vb[sl], k_end[sl])
            st = st * jnp.exp(lasts[j]) + jnp.where(st_mask, d_st, 0.0)
        st_ref[bb] = st
        o = jnp.concatenate(outs, axis=0)
        ms = _dot_split_lhs(o * o, gmean, 2)
        o = o * lax.rsqrt(ms + RMS_EPS) * ng_ref[...]
        o_ref[bb, rows, :] = (o * (g * _sigmoid(g))).astype(o_ref.dtype)

    return one_seq


def _ssd_setup(p_ref, dt_ref, cw_ref, cb_ref, alog_ref, dtb_ref, dsk_ref, ng_ref, o_ref, st_ref):
    c = SSD_CHUNK
    s_len = p_ref.shape[1]
    n_st = SSD_STATE
    st_ref[...] = jnp.zeros_like(st_ref)

    ri = lax.broadcasted_iota(jnp.int32, (c, c), 0)
    ci = lax.broadcasted_iota(jnp.int32, (c, c), 1)
    causal = ci <= ri
    tri = causal.astype(BF16)
    upper = (ri <= ci).astype(BF16)
    lane_head = lax.broadcasted_iota(jnp.int32, (1, W_MIX), 1) // SSD_HEADDIM
    lane_group = lane_head // (SSD_HEADS // SSD_GROUPS)
    eh = lax.broadcasted_iota(jnp.int32, (LANES, W_MIX), 0)
    el = lax.broadcasted_iota(jnp.int32, (LANES, W_MIX), 1) // SSD_HEADDIM
    expand = (eh == el).astype(BF16)
    row8 = lax.broadcasted_iota(jnp.int32, (8, 3 * W_MIX), 0)
    a_c = -jnp.exp(alog_ref[...])

    def one_chunk(n, bb):
        r0 = pl.multiple_of(n * c, c)
        rows = pl.ds(r0, c)
        cur = p_ref[bb, rows, 256:1024].astype(F32)
        p0 = pl.multiple_of(jnp.maximum(r0 - 2 * SUBLANES, 0), 2 * SUBLANES)
        prev8 = p_ref[bb, pl.ds(p0, 2 * SUBLANES), 256:1024].astype(F32)[SUBLANES:]
        prev8 = jnp.where(n > 0, prev8, 0.0)
        acc = cur * cw_ref[3:4, :] + cb_ref[...]
        for s in (1, 2, 3):
            sh = pltpu.roll(cur, s, axis=0)
            top = jnp.where(row8 < s, pltpu.roll(prev8, s, axis=0), sh[0:8])
            sh = jnp.concatenate([top, sh[8:]], axis=0)
            acc = acc + sh * cw_ref[3 - s:4 - s, :]
        xbc = acc * _sigmoid(acc)
        x = xbc[:, 0:256]
        bm = xbc[:, 256:512].astype(BF16)
        cm = xbc[:, 512:768].astype(BF16)

        dt_c = _softplus(dt_ref[bb, rows, :] + dtb_ref[...])
        da_c = dt_c * a_c
        cum_c = _dot_split_rhs(tri, da_c, 3)
        cum_r = _dot_split_lhs(da_c, upper, 3, dot=_dot_tn)
        both_x = _dot_split_lhs(jnp.concatenate([dt_c, cum_c], axis=0), expand, 3)
        dt_x = both_x[0:c]
        cum_x = both_x[c:2 * c]
        cl_x = cum_x[c - 1:c, :]
        x_dt = x * dt_x
        x_dt_b = x_dt.astype(BF16)
        xw_b = (x_dt * jnp.exp(cl_x - cum_x)).astype(BF16)

        y = x * dsk_ref[...]
        y_off = jnp.zeros((c, W_MIX), F32)
        for g in range(SSD_GROUPS):
            bg = bm[:, g * n_st:(g + 1) * n_st]
            cg = cm[:, g * n_st:(g + 1) * n_st]
            cb = _dot_nt(cg, bg)
            for r in range(SSD_HEADS // SSD_GROUPS):
                h = g * (SSD_HEADS // SSD_GROUPS) + r
                diff = cum_c[:, h:h + 1] - cum_r[h:h + 1, :]
                dec = jnp.exp(jnp.where(causal, diff, -jnp.inf))
                m = (cb * dec).astype(BF16)
                yh = jnp.dot(m, x_dt_b, preferred_element_type=F32)
                y = y + jnp.where(lane_head == h, yh, 0.0)
            st = st_ref[bb, g]
            y_off = y_off + jnp.where(lane_group == g,
                                      jnp.dot(cg, st.astype(BF16), preferred_element_type=F32), 0.0)
            d_st = _dot_tn(bg, xw_b)
            st_ref[bb, g] = st * jnp.exp(cl_x) + jnp.where(lane_group == g, d_st, 0.0)
        y = y + y_off * jnp.exp(cum_x)
        zg = p_ref[bb, rows, 0:256].astype(F32)
        y = y * (zg * _sigmoid(zg))
        outs = []
        for g in range(SSD_GROUPS):
            yg = y[:, g * 128:(g + 1) * 128]
            ms = jnp.mean(yg * yg, axis=-1, keepdims=True)
            outs.append(yg * lax.rsqrt(ms + RMS_EPS))
        o_ref[bb, rows, :] = (jnp.concatenate(outs, axis=-1) * ng_ref[...]).astype(o_ref.dtype)

    return one_chunk


def _recurrent_kernel(pg_ref, lr_ref, wlr_ref, blr_ref, gng_ref,
                      ps_ref, dt_ref, cw_ref, cb_ref, alog_ref, dtb_ref, dsk_ref, sng_ref,
                      og_ref, os_ref, gst_ref, sst_ref):
    gla_rows = _gla_setup(pg_ref, lr_ref, wlr_ref, blr_ref, gng_ref, og_ref, gst_ref)
    ssd_chunk = _ssd_setup(ps_ref, dt_ref, cw_ref, cb_ref, alog_ref, dtb_ref, dsk_ref, sng_ref,
                           os_ref, sst_ref)
    per = GLA_ROWS // SSD_CHUNK

    def body(n, carry):
        rows = pl.ds(pl.multiple_of(n * GLA_ROWS, GLA_ROWS), GLA_ROWS)
        for bb in range(pg_ref.shape[0]):
            gla_rows(bb, rows)
            for j in range(per):
                ssd_chunk(n * per + j, bb)
        return carry

    lax.fori_loop(0, pg_ref.shape[1] // GLA_ROWS, body, 0)


def _recurrent_mixers(pg, plr, w_lr_pad, b_lr, gla_norm_g4,
                      ps, pdt, conv_w, conv_b, a_log_c, dt_bias_c, d_x, ssd_norm_g):
    b, s, _ = pg.shape
    nb = REC_SEQS
    seq = lambda i: (i, 0, 0)
    full2 = lambda i: (0, 0)
    return pl.pallas_call(
        _recurrent_kernel,
        grid=(b // nb,),
        in_specs=[pl.BlockSpec((nb, s, pg.shape[2]), seq),
                  pl.BlockSpec((nb, s, LANES), seq),
                  pl.BlockSpec((LANES, LANES), full2),
                  pl.BlockSpec((1, LANES), full2),
                  pl.BlockSpec((1, W_MIX), full2),
                  pl.BlockSpec((nb, s, ps.shape[2]), seq),
                  pl.BlockSpec((nb, s, LANES), seq),
                  pl.BlockSpec((SSD_CONV_K, 3 * W_MIX), full2),
                  pl.BlockSpec((1, 3 * W_MIX), full2),
                  pl.BlockSpec((1, LANES), full2),
                  pl.BlockSpec((1, LANES), full2),
                  pl.BlockSpec((1, W_MIX), full2),
                  pl.BlockSpec((1, W_MIX), full2)],
        out_specs=[pl.BlockSpec((nb, s, W_MIX), seq), pl.BlockSpec((nb, s, W_MIX), seq)],
        out_shape=[jax.ShapeDtypeStruct((b, s, W_MIX), BF16),
                   jax.ShapeDtypeStruct((b, s, W_MIX), BF16)],
        scratch_shapes=[pltpu.VMEM((nb, GLA_HEADS * GLA_DV, GLA_HEADS * GLA_DK), F32),
                        pltpu.VMEM((nb, SSD_GROUPS, SSD_STATE, W_MIX), F32)],
        compiler_params=_cparams(("parallel",)),
        name="gla_ssd_mixers",
    )(pg, plr, w_lr_pad, b_lr, gla_norm_g4, ps, pdt, conv_w, conv_b, a_log_c, dt_bias_c, d_x,
      ssd_norm_g)


DIFF_TQ = 256
DIFF_TK = 256
LOG2E = 1.4426950408889634
DIFF_VPAD = DIFF_DV + 16
DIFF_SEQS = 4


def _diff_kernel(q_ref, k_ref, v_ref, lam_ref, ng_ref, o_ref,
                 vt_ref, qs_ref, st_ref, m_ref, acc_ref, *, lam_init):
    tq, tk = DIFF_TQ, DIFF_TK
    nh, dv = DIFF_HEADS, DIFF_DV
    nhc = 2 * nh
    s_len = k_ref.shape[1]
    i = pl.program_id(1)
    seqs = range(q_ref.shape[0])

    @pl.when(i == 0)
    def _():
        for bb in seqs:
            for cblk in range(s_len // tk):
                cols = slice(cblk * tk, (cblk + 1) * tk)
                vt = v_ref[bb, cols, :].astype(F32).T.astype(BF16)
                for h in range(nh):
                    vt_ref[bb, h, 0:dv, cols] = vt[h * dv:(h + 1) * dv]
        vt_ref[:, :, dv:, :] = jnp.ones((len(seqs), nh, DIFF_VPAD - dv, s_len), BF16)

    qlane = lax.broadcasted_iota(jnp.int32, (1, W_MIX), 1) // DIFF_DQK
    for bb in seqs:
        q = q_ref[bb].astype(F32) * (DIFF_DQK ** -0.5 * LOG2E)
        for hc in range(nhc):
            qs_ref[bb, hc * tq:(hc + 1) * tq, :] = jnp.where(qlane == hc, q, 0.0).astype(BF16)
    m_ref[...] = jnp.full_like(m_ref, -jnp.inf)
    acc_ref[...] = jnp.zeros_like(acc_ref)
    krow = lax.broadcasted_iota(jnp.int32, (tk, nhc * tq), 0)
    qcol = lax.broadcasted_iota(jnp.int32, (tk, nhc * tq), 1) % tq
    diag_ok = krow <= qcol

    def scores(j, slot):
        k0 = pl.multiple_of(j * tk, tk)
        for bb in seqs:
            st_ref[bb, slot] = _dot_nt(k_ref[bb, pl.ds(k0, tk), :], qs_ref[bb])

    def softmax_pv(j, slot, masked):
        for bb in seqs:
            softmax_pv_seq(bb, j, slot, masked)

    def softmax_pv_seq(bb, j, slot, masked):
        k0 = pl.multiple_of(j * tk, tk)
        st = st_ref[bb, slot]
        if masked:
            st = jnp.where(diag_ok, st, -jnp.inf)
        m_prev = m_ref[bb]
        m_new = jnp.maximum(m_prev, jnp.max(st, axis=0, keepdims=True))
        alpha = jnp.exp2(m_prev - m_new)
        p = jnp.exp2(st - m_new)
        m_ref[bb] = m_new
        pb = p.astype(BF16)
        for hc in range(nhc):
            h = hc // 2
            lanes = slice(hc * tq, (hc + 1) * tq)
            pv = jnp.dot(vt_ref[bb, h, :, pl.ds(k0, tk)], pb[:, lanes],
                         preferred_element_type=F32)
            acc_ref[bb, hc] = acc_ref[bb, hc] * alpha[:, lanes] + pv

    scores(0, 0)
    n_pairs = i // 2

    def pair_step(u, carry):
        scores(2 * u + 1, 1)
        softmax_pv(2 * u, 0, False)
        scores(2 * u + 2, 0)
        softmax_pv(2 * u + 1, 1, False)
        return carry

    lax.fori_loop(0, n_pairs, pair_step, 0)

    @pl.when(i % 2 == 0)
    def _():
        softmax_pv(i, 0, True)

    @pl.when(i % 2 == 1)
    def _():
        scores(i, 1)
        softmax_pv(i - 1, 0, False)
        softmax_pv(i, 1, True)

    lam = (jnp.exp(jnp.sum(lam_ref[0:1, :] * lam_ref[1:2, :], axis=-1, keepdims=True))
           - jnp.exp(jnp.sum(lam_ref[2:3, :] * lam_ref[3:4, :], axis=-1, keepdims=True))
           + lam_init)
    for bb in seqs:
        heads = []
        for h in range(nh):
            o1 = acc_ref[bb, 2 * h, 0:dv] / acc_ref[bb, 2 * h, dv:dv + 1]
            o2 = acc_ref[bb, 2 * h + 1, 0:dv] / acc_ref[bb, 2 * h + 1, dv:dv + 1]
            oh = o1 - lam * o2
            ms = jnp.mean(oh * oh, axis=0, keepdims=True)
            heads.append(oh * lax.rsqrt(ms + RMS_EPS))
        o = jnp.concatenate(heads, axis=0).T
        o_ref[bb] = (o * ng_ref[...] * (1.0 - lam_init)).astype(o_ref.dtype)


def _diff_mixer(pd, lam_vecs, norm_g4, lam_init):
    b, s, _ = pd.shape
    tq = DIFF_TQ
    nb = DIFF_SEQS
    return pl.pallas_call(
        functools.partial(_diff_kernel, lam_init=lam_init),
        grid=(b // nb, s // tq),
        in_specs=[pl.BlockSpec((nb, tq, W_MIX), lambda bi, i: (bi, i, 0)),
                  pl.BlockSpec((nb, s, W_MIX), lambda bi, i: (bi, 0, 1)),
                  pl.BlockSpec((nb, s, W_MIX), lambda bi, i: (bi, 0, 2)),
                  pl.BlockSpec((4, LANES), lambda bi, i: (0, 0)),
                  pl.BlockSpec((1, W_MIX), lambda bi, i: (0, 0))],
        out_specs=pl.BlockSpec((nb, tq, W_MIX), lambda bi, i: (bi, i, 0)),
        out_shape=jax.ShapeDtypeStruct((b, s, W_MIX), BF16),
        scratch_shapes=[pltpu.VMEM((nb, DIFF_HEADS, DIFF_VPAD, s), BF16),
                        pltpu.VMEM((nb, 2 * DIFF_HEADS * tq, W_MIX), BF16),
                        pltpu.VMEM((nb, 2, DIFF_TK, 2 * DIFF_HEADS * tq), F32),
                        pltpu.VMEM((nb, 1, 2 * DIFF_HEADS * tq), F32),
                        pltpu.VMEM((nb, 2 * DIFF_HEADS, DIFF_VPAD, tq), F32)],
        compiler_params=_cparams(("parallel", "arbitrary")),
        name="diff_attn",
    )(pd, pd, pd, lam_vecs, norm_g4)


def _oproj_kernel(yc_ref, yg_ref, ys_ref, yd_ref, x_ref, wo_ref, g_ref, b_ref, wr_ref,
                  xo_ref, xp_ref, route_ref, cnt_ref):
    mix = jnp.concatenate([yc_ref[...], yg_ref[...], ys_ref[...], yd_ref[...]], axis=-1)
    h = ALPHA * x_ref[...] + jnp.dot(mix, wo_ref[...], preferred_element_type=F32)
    xn = _layer_norm(h, g_ref[...], b_ref[...])
    xo_ref[...] = xn
    xp_ref[...] = _pack_halves(xn)

    xn_hi, xn_lo = _split_bf16(xn, 2)
    both = _dot(xn_hi, wr_ref[...])
    logits = both[:, 0:LANES] + both[:, LANES:2 * LANES] + _dot(xn_lo, wr_ref[:, 0:LANES])
    lane = lax.broadcasted_iota(jnp.int32, logits.shape, 1).astype(F32)
    neg = -jnp.inf
    big = float(LANES)
    lg = jnp.where(lane < N_GROUPS, logits, neg)
    mg = jnp.max(lg, axis=-1, keepdims=True)
    sg = jnp.sum(jnp.exp(lg - mg), axis=-1, keepdims=True)
    grp = jnp.min(jnp.where(lg == mg, lane, big), axis=-1, keepdims=True)
    p_grp = 1.0 / sg
    lo = N_GROUPS + EXPERTS_PER_GROUP * grp
    in_g = jnp.logical_and(lane >= lo, lane < lo + EXPERTS_PER_GROUP)
    le = jnp.where(in_g, logits, neg)
    me = jnp.max(le, axis=-1, keepdims=True)
    ee = jnp.exp(le - me)
    pe = ee / jnp.sum(ee, axis=-1, keepdims=True)
    pe = jnp.where(in_g, pe, -1.0)
    p1 = jnp.max(pe, axis=-1, keepdims=True)
    i1 = jnp.min(jnp.where(pe == p1, lane, big), axis=-1, keepdims=True)
    pe2 = jnp.where(lane == i1, -1.0, pe)
    p2 = jnp.max(pe2, axis=-1, keepdims=True)
    i2 = jnp.min(jnp.where(pe2 == p2, lane, big), axis=-1, keepdims=True)
    den = p1 + p2
    g1 = p_grp * p1 / den
    g2 = p_grp * p2 / den
    e1 = i1 - N_GROUPS
    e2 = i2 - N_GROUPS
    route_ref[...] = jnp.where(lane == 0, e1, jnp.where(lane == 1, e2, jnp.where(
        lane == 2, g1, jnp.where(lane == 3, g2, 0.0))))

    @pl.when(pl.program_id(0) == 0)
    def _():
        cnt_ref[...] = jnp.zeros_like(cnt_ref)

    hits = jnp.where(lane == e1, 1.0, 0.0) + jnp.where(lane == e2, 1.0, 0.0)
    cnt_ref[...] += jnp.sum(hits, axis=0, keepdims=True)


def _out_proj(ys, x2d, w_o, layer, ln_g, ln_b, w_route):
    t = x2d.shape[0]
    tm = 1024
    row = lambda i: (i, 0)
    full = lambda i: (0, 0)
    return pl.pallas_call(
        _oproj_kernel,
        grid=(t // tm,),
        in_specs=[pl.BlockSpec((tm, W_MIX), row)] * 4 + [
            pl.BlockSpec((tm, D_MODEL), row),
            pl.BlockSpec((None, D_MODEL, D_MODEL), lambda i: (layer, 0, 0)),
            pl.BlockSpec((1, D_MODEL), full),
            pl.BlockSpec((1, D_MODEL), full),
            pl.BlockSpec((D_MODEL, 2 * LANES), full)],
        out_specs=[pl.BlockSpec((tm, D_MODEL), row), pl.BlockSpec((tm, D_MODEL // 2), row),
                   pl.BlockSpec((tm, LANES), row), pl.BlockSpec((1, LANES), full)],
        out_shape=[jax.ShapeDtypeStruct((t, D_MODEL), F32),
                   jax.ShapeDtypeStruct((t, D_MODEL // 2), U32),
                   jax.ShapeDtypeStruct((t, LANES), F32),
                   jax.ShapeDtypeStruct((1, LANES), F32)],
        compiler_params=_cparams(("arbitrary",)),
        name="out_proj_ln_router",
    )(*ys, x2d, w_o, ln_g, ln_b, w_route)


PLAN_TILE = 512


def _plan_kernel(route_ref, cnt_ref, dest_ref, meta_ref, carry_ref, pstart_ref):
    tm = route_ref.shape[0]
    lane = lax.broadcasted_iota(jnp.int32, (1, LANES), 1).astype(F32)

    @pl.when(pl.program_id(0) == 0)
    def _():
        cnt = cnt_ref[...]
        padded = jnp.ceil(cnt * (1.0 / MOE_BLK)) * MOE_BLK
        li = lax.broadcasted_iota(jnp.int32, (LANES, LANES), 0)
        lj = lax.broadcasted_iota(jnp.int32, (LANES, LANES), 1)
        before = (li < lj).astype(F32)
        pstart = jnp.dot(jnp.broadcast_to(padded, (8, LANES)), before, precision=HI,
                         preferred_element_type=F32)[0:1]
        pstart_ref[...] = pstart
        carry_ref[...] = jnp.zeros_like(carry_ref)
        meta_ref[...] = jnp.concatenate(
            [pstart + padded, pstart, cnt, jnp.zeros((5, LANES), F32)], axis=0)

    oh0 = jnp.where(lane == route_ref[:, 0:1], 1.0, 0.0)
    oh1 = jnp.where(lane == route_ref[:, 1:2], 1.0, 0.0)
    both = oh0 + oh1
    ri = lax.broadcasted_iota(jnp.int32, (tm, tm), 0)
    ci = lax.broadcasted_iota(jnp.int32, (tm, tm), 1)
    earlier = (ci < ri).astype(BF16)
    base = (jnp.dot(earlier, both.astype(BF16), preferred_element_type=F32)
            + carry_ref[...] + pstart_ref[...])
    d0 = jnp.sum(oh0 * base, axis=-1, keepdims=True)
    d1 = jnp.sum(oh1 * base, axis=-1, keepdims=True)
    dest = jnp.where(lane == 0, d0, jnp.where(lane == 1, d1, 0.0))
    dest_ref[...] = dest.T[0:2, :].astype(jnp.int32)
    carry_ref[...] += jnp.sum(both, axis=0, keepdims=True)


def _dispatch_plan(route, cnt, n_blocks):
    t = route.shape[0]
    tm = PLAN_TILE
    blk = MOE_BLK
    dest, meta = pl.pallas_call(
        _plan_kernel,
        grid=(t // tm,),
        in_specs=[pl.BlockSpec((tm, LANES), lambda i: (i, 0)),
                  pl.BlockSpec((1, LANES), lambda i: (0, 0))],
        out_specs=[pl.BlockSpec((2, tm), lambda i: (0, i)),
                   pl.BlockSpec((8, LANES), lambda i: (0, 0))],
        out_shape=[jax.ShapeDtypeStruct((2, t), jnp.int32),
                   jax.ShapeDtypeStruct((8, LANES), F32)],
        scratch_shapes=[pltpu.VMEM((1, LANES), F32), pltpu.VMEM((1, LANES), F32)],
        compiler_params=_cparams(("arbitrary",)),
        name="moe_plan",
    )(route, cnt)
    meta_i = meta[:, :N_EXPERTS].astype(jnp.int32)
    pad_end, pad_start, seg_end = meta_i[0], meta_i[1], meta_i[1] + meta_i[2]
    starts = jnp.arange(n_blocks, dtype=jnp.int32)[:, None] * blk
    member = jnp.logical_and(starts >= pad_start[None, :], starts < pad_end[None, :])
    expert_ids = jnp.arange(N_EXPERTS, dtype=jnp.int32)[None, :]
    block_e = jnp.where(jnp.any(member, axis=1), jnp.sum(jnp.where(member, expert_ids, 0), axis=1),
                        N_EXPERTS - 1)
    n_used = (pad_end[N_EXPERTS - 1] // blk).reshape(1)
    n_valid = jnp.clip(jnp.sum(jnp.where(member, seg_end[None, :], 0), axis=1) - starts[:, 0],
                       0, blk)
    return dest.reshape(2 * t), block_e, n_used, n_valid


def _ffn_kernel(be_ref, nu_ref, nv_ref, xs_ref, wg_ref, wu_ref, wd_ref, y_ref,
                wgb_ref, wub_ref, wdb_ref):
    i = pl.program_id(0)
    prev = be_ref[jnp.maximum(i - 1, 0)]

    @pl.when(jnp.logical_or(i == 0, be_ref[i] != prev))
    def _():
        wgb_ref[...] = wg_ref[...].astype(BF16)
        wub_ref[...] = wu_ref[...].astype(BF16)
        wdb_ref[...] = wd_ref[...].astype(BF16)

    @pl.when(i < nu_ref[0])
    def _():
        half = xs_ref.shape[0] // 2
        for r in range(2):
            rows = slice(r * half, (r + 1) * half)
            row = lax.broadcasted_iota(jnp.int32, (half, 1), 0) + r * half
            xp = jnp.where(row < nv_ref[i], xs_ref[rows, :], U32(0))
            x_hi, x_lo = _unpack_halves(xp)
            xb = jnp.concatenate([x_hi.astype(BF16), x_lo.astype(BF16)], axis=1)
            a = jnp.dot(xb, wgb_ref[...], preferred_element_type=F32)
            u = jnp.dot(xb, wub_ref[...], preferred_element_type=F32)
            h = (a * _sigmoid(a) * u).astype(BF16)
            y_ref[rows, :] = _pack_halves(jnp.dot(h, wdb_ref[...], preferred_element_type=F32))

    @pl.when(i >= nu_ref[0])
    def _():
        y_ref[...] = jnp.zeros_like(y_ref)


def _expert_ffn(xs, block_e, n_used, n_valid, w_gate, w_up, w_down, layer):
    n_slots = xs.shape[0]
    blk = MOE_BLK
    w_map = lambda i, be, nu, nv: (layer, be[i], 0, 0)
    grid_spec = pltpu.PrefetchScalarGridSpec(
        num_scalar_prefetch=3,
        grid=(n_slots // blk,),
        in_specs=[pl.BlockSpec((blk, D_MODEL // 2),
                               lambda i, be, nu, nv: (jnp.minimum(i, nu[0] - 1), 0)),
                  pl.BlockSpec((None, None, D_MODEL, D_EXPERT), w_map),
                  pl.BlockSpec((None, None, D_MODEL, D_EXPERT), w_map),
                  pl.BlockSpec((None, None, D_EXPERT, D_MODEL), w_map)],
        out_specs=pl.BlockSpec((blk, D_MODEL // 2), lambda i, be, nu, nv: (i, 0)),
        scratch_shapes=[pltpu.VMEM((D_MODEL, D_EXPERT), BF16),
                        pltpu.VMEM((D_MODEL, D_EXPERT), BF16),
                        pltpu.VMEM((D_EXPERT, D_MODEL), BF16)],
    )
    return pl.pallas_call(
        _ffn_kernel,
        grid_spec=grid_spec,
        out_shape=jax.ShapeDtypeStruct((n_slots, D_MODEL // 2), U32),
        compiler_params=_cparams(("arbitrary",)),
        name="expert_ffn",
    )(block_e, n_used, n_valid, xs, w_gate, w_up, w_down)


SC_CORES = 2
SC_SUBCORES = 16
SC_ROWS = 64


def _sc_gather_rows(table, idx):
    b = idx.shape[0]
    d = table.shape[1]
    per_w = b // (SC_CORES * SC_SUBCORES)
    mesh = plsc.VectorSubcoreMesh(core_axis_name="c", subcore_axis_name="s")

    n_chunks = per_w // SC_ROWS

    @functools.partial(
        pl.kernel, mesh=mesh,
        out_type=jax.ShapeDtypeStruct((b, d), table.dtype),
        scratch_types=[pltpu.VMEM((SC_ROWS,), jnp.int32), pltpu.VMEM((SC_ROWS,), jnp.int32),
                       pltpu.VMEM((SC_ROWS, d), table.dtype),
                       pltpu.VMEM((SC_ROWS, d), table.dtype),
                       pltpu.SemaphoreType.DMA, pltpu.SemaphoreType.DMA,
                       pltpu.SemaphoreType.DMA, pltpu.SemaphoreType.DMA],
        name="sc_gather_rows",
    )
    def gather(table_hbm, idx_hbm, out_hbm, idx0, idx1, rows0, rows1, gs0, gs1, ws0, ws1):
        idx_v, rows_v, gsem, wsem = (idx0, idx1), (rows0, rows1), (gs0, gs1), (ws0, ws1)
        wid = lax.axis_index("s") * SC_CORES + lax.axis_index("c")
        base = wid * per_w

        def rows_of(c):
            return pl.ds(pl.multiple_of(base + c * SC_ROWS, SC_ROWS), SC_ROWS)

        def start_gather(c, s):
            pltpu.sync_copy(idx_hbm.at[rows_of(c)], idx_v[s])
            pltpu.async_copy(table_hbm.at[idx_v[s]], rows_v[s], gsem[s])

        def write_back(c, s):
            pltpu.make_async_copy(table_hbm.at[idx_v[s]], rows_v[s], gsem[s]).wait()
            pltpu.async_copy(rows_v[s], out_hbm.at[rows_of(c)], wsem[s]).wait()

        start_gather(0, 0)

        @pl.loop(0, n_chunks, step=2)
        def _(c):
            start_gather(c + 1, 1)
            write_back(c, 0)

            @pl.when(c + 2 < n_chunks)
            def _():
                start_gather(c + 2, 0)

            write_back(c + 1, 1)

    return gather(table, idx)


def _sc_scatter_rows(x2d, idx, n_slots):
    t, d = x2d.shape
    per_w = t // (SC_CORES * SC_SUBCORES)
    mesh = plsc.VectorSubcoreMesh(core_axis_name="c", subcore_axis_name="s")

    @functools.partial(
        pl.kernel, mesh=mesh,
        out_type=jax.ShapeDtypeStruct((n_slots, d), x2d.dtype),
        scratch_types=[pltpu.VMEM((SC_ROWS,), jnp.int32), pltpu.VMEM((SC_ROWS,), jnp.int32),
                       pltpu.VMEM((SC_ROWS, d), x2d.dtype),
                       pltpu.SemaphoreType.DMA, pltpu.SemaphoreType.DMA],
        name="sc_scatter_rows",
    )
    def scatter(x_hbm, idx_hbm, out_hbm, idx0, idx1, rows_v, s0, s1):
        wid = lax.axis_index("s") * SC_CORES + lax.axis_index("c")
        base = wid * per_w

        @pl.loop(0, per_w // SC_ROWS)
        def _(c):
            off = pl.multiple_of(base + c * SC_ROWS, SC_ROWS)
            pltpu.sync_copy(x_hbm.at[pl.ds(off, SC_ROWS)], rows_v)
            pltpu.sync_copy(idx_hbm.at[pl.ds(off, SC_ROWS)], idx0)
            pltpu.sync_copy(idx_hbm.at[pl.ds(t + off, SC_ROWS)], idx1)
            cp0 = pltpu.async_copy(rows_v, out_hbm.at[idx0], s0)
            cp1 = pltpu.async_copy(rows_v, out_hbm.at[idx1], s1)
            cp0.wait()
            cp1.wait()

    return scatter(x2d, idx)


def _combine_dense_kernel(route_ref, x_ref, y0_ref, y1_ref, g_ref, b_ref, o_ref):
    y0 = jnp.concatenate(_unpack_halves(y0_ref[...]), axis=1)
    y1 = jnp.concatenate(_unpack_halves(y1_ref[...]), axis=1)
    moe = route_ref[:, 2:3] * y0 + route_ref[:, 3:4] * y1
    h = ALPHA * x_ref[...] + moe
    o_ref[...] = _layer_norm(h, g_ref[...], b_ref[...])


def _combine_dense(route, x2d, yg, ln_g, ln_b):
    t = x2d.shape[0]
    tm = 1024
    nt = t // tm
    row = lambda i: (i, 0)
    full = lambda i: (0, 0)
    return pl.pallas_call(
        _combine_dense_kernel,
        grid=(nt,),
        in_specs=[pl.BlockSpec((tm, LANES), row),
                  pl.BlockSpec((tm, D_MODEL), row),
                  pl.BlockSpec((tm, D_MODEL // 2), row),
                  pl.BlockSpec((tm, D_MODEL // 2), lambda i: (i + nt, 0)),
                  pl.BlockSpec((1, D_MODEL), full),
                  pl.BlockSpec((1, D_MODEL), full)],
        out_specs=pl.BlockSpec((tm, D_MODEL), row),
        out_shape=jax.ShapeDtypeStruct((t, D_MODEL), F32),
        compiler_params=_cparams(("parallel",)),
        name="moe_combine_dense",
    )(route, x2d, yg, yg, ln_g, ln_b)


def _pad_cols(w, n):
    return jnp.pad(w, [(0, 0)] * (w.ndim - 1) + [(0, n - w.shape[-1])])


def kernel(x, w_in, conv_w, gla_w_lr, gla_b_lr, gla_norm_g, ssd_conv_w, ssd_conv_b, ssd_a_log,
           ssd_d, ssd_dt_bias, ssd_norm_g, diff_lq1, diff_lk1, diff_lq2, diff_lk2, diff_norm_g,
           w_o, ln1_g, ln1_b, router_g, router_e, w_gate, w_up, w_down, ln2_g, ln2_b):
    bsz, seq, d = x.shape
    t = bsz * seq
    n_assign = 2 * t
    n_blocks = (n_assign + N_EXPERTS * (MOE_BLK - 1)) // MOE_BLK + 1
    n_slots = n_blocks * MOE_BLK
    x2d = x.reshape(t, d)
    w_o_b = w_o.astype(BF16)
    for l in range(DEPTH):
        pc, pg, plr, ps, pdt, pd = [p.reshape(bsz, seq, -1) for p in _in_proj(x2d, w_in, l)]

        y_conv = _conv_mixer(pc, conv_w[l])
        w_lr_pad = jnp.pad(gla_w_lr[l], ((0, LANES - GLA_RANK), (0, 0)))
        pad4 = lambda v: jnp.pad(v, (0, LANES - SSD_HEADS)).reshape(1, LANES)
        y_gla, y_ssd = _recurrent_mixers(
            pg, plr, w_lr_pad, gla_b_lr[l].reshape(1, -1),
            jnp.tile(gla_norm_g[l], GLA_HEADS).reshape(1, -1),
            ps, pdt, ssd_conv_w[l], ssd_conv_b[l].reshape(1, -1),
            pad4(ssd_a_log[l]), pad4(ssd_dt_bias[l]),
            jnp.repeat(ssd_d[l], SSD_HEADDIM).reshape(1, -1), ssd_norm_g[l].reshape(1, -1))
        lam_vecs = jnp.pad(jnp.stack([diff_lq1[l], diff_lk1[l], diff_lq2[l], diff_lk2[l]]),
                           ((0, 0), (0, LANES - DIFF_DQK)))
        lam_init = 0.8 - 0.6 * math.exp(-0.3 * l)
        y_diff = _diff_mixer(pd, lam_vecs,
                             jnp.tile(diff_norm_g[l], DIFF_HEADS).reshape(1, -1), lam_init)

        w_route = _pad_cols(jnp.concatenate(
            [router_g[l], router_e[l].reshape(d, N_EXPERTS)], axis=1), LANES)
        w_route_hi = w_route.astype(BF16)
        w_route = jnp.concatenate(
            [w_route_hi, (w_route - w_route_hi.astype(F32)).astype(BF16)], axis=1)
        ys = [y.reshape(t, W_MIX) for y in (y_conv, y_gla, y_ssd, y_diff)]
        xn, xn_p, route, cnt = _out_proj(ys, x2d, w_o_b, l, ln1_g[l].reshape(1, -1),
                                         ln1_b[l].reshape(1, -1), w_route)

        idx, block_e, n_used, n_valid = _dispatch_plan(route, cnt, n_blocks)
        xs = _sc_scatter_rows(xn_p, idx, n_slots)
        y = _expert_ffn(xs, block_e, n_used, n_valid, w_gate, w_up, w_down, l)
        yg = _sc_gather_rows(y, idx)
        x2d = _combine_dense(route, xn, yg, ln2_g[l].reshape(1, -1), ln2_b[l].reshape(1, -1))
    return x2d.reshape(bsz, seq, d)
```

```python
import functools
import math

import jax
import jax.numpy as jnp
from jax import lax
from jax.experimental import pallas as pl
from jax.experimental.pallas import tpu as pltpu
from jax.experimental.pallas import tpu_sc as plsc

F32 = jnp.float32
BF16 = jnp.bfloat16
HI = lax.Precision.HIGHEST

D_MODEL = 1024
DEPTH = 2
W_MIX = 256
GLA_HEADS, GLA_DK, GLA_DV, GLA_RANK, GLA_TAU, GLA_CHUNK = 4, 32, 64, 16, 16.0, 64
GLA_ROWS = 256
REC_SEQS = 4
SSD_HEADS, SSD_GROUPS, SSD_HEADDIM, SSD_STATE, SSD_CONV_K, SSD_CHUNK = 4, 2, 64, 128, 4, 128
DIFF_HEADS, DIFF_DQK, DIFF_DV = 4, 32, 64
N_GROUPS, EXPERTS_PER_GROUP, N_EXPERTS, D_EXPERT = 4, 8, 32, 512
ALPHA = (2 * DEPTH) ** 0.25
LN_EPS = 1e-5
RMS_EPS = 1e-6

LANES = 128
SUBLANES = 8
PROJ_WIDTHS = (768, 768, 128, 1024, 128, 768)
PROJ_SRC_OFFSETS = (0, 768, 1536, 1552, 2576, 2580)
PROJ_SRC_WIDTHS = (768, 768, GLA_RANK, 1024, SSD_HEADS, 768)
PROJ_DTYPES = (BF16, BF16, F32, BF16, F32, BF16)
VMEM_LIMIT = 56 * 1024 * 1024

MOE_BLK = 512


def _cparams(sem):
    return pltpu.CompilerParams(dimension_semantics=sem, vmem_limit_bytes=VMEM_LIMIT)


def _sigmoid(x):
    return 1.0 / (1.0 + jnp.exp(-x))


def _softplus(x):
    return jnp.maximum(x, 0.0) + jnp.log(1.0 + jnp.exp(-jnp.abs(x)))


def _layer_norm(h, g, b):
    mu = jnp.mean(h, axis=-1, keepdims=True)
    d = h - mu
    var = jnp.mean(d * d, axis=-1, keepdims=True)
    return d * lax.rsqrt(var + LN_EPS) * g + b


def _dot_nt(a, b):
    return lax.dot_general(a, b, (((1,), (1,)), ((), ())), preferred_element_type=F32)


def _dot_tn(a, b, precision=None):
    return lax.dot_general(a, b, (((0,), (0,)), ((), ())), preferred_element_type=F32,
                           precision=precision)


def _split_bf16(x, parts):
    out = []
    for _ in range(parts - 1):
        hi = x.astype(BF16)
        out.append(hi)
        x = x - hi.astype(F32)
    out.append(x.astype(BF16))
    return out


def _dot(a, b):
    return jnp.dot(a, b, preferred_element_type=F32)


U32 = jnp.uint32


def _pack_halves(x):
    w = x.shape[1] // 2
    hi = lax.bitcast_convert_type(x[:, :w].astype(BF16).astype(F32), U32)
    lo = lax.bitcast_convert_type(x[:, w:].astype(BF16).astype(F32), U32)
    return hi | lax.shift_right_logical(lo, U32(16))


def _unpack_halves(p):
    hi = lax.bitcast_convert_type(p & U32(0xFFFF0000), F32)
    lo = lax.bitcast_convert_type(lax.shift_left(p, U32(16)), F32)
    return hi, lo


def _dot_split_lhs(a, b_exact, parts, dot=_dot):
    acc = None
    for term in _split_bf16(a, parts):
        d = dot(term, b_exact)
        acc = d if acc is None else acc + d
    return acc


def _dot_split_rhs(a_exact, b, parts):
    acc = None
    for term in _split_bf16(b, parts):
        d = jnp.dot(a_exact, term, preferred_element_type=F32)
        acc = d if acc is None else acc + d
    return acc


def _proj_kernel(x_ref, w_ref, *refs):
    o_refs, w_scr = refs[:-1], refs[-1]

    @pl.when(pl.program_id(0) == 0)
    def _():
        w_scr[...] = jnp.zeros_like(w_scr)
        dst = 0
        for src, n_src, n_dst in zip(PROJ_SRC_OFFSETS, PROJ_SRC_WIDTHS, PROJ_WIDTHS):
            w_scr[:, dst:dst + n_src] = w_ref[:, src:src + n_src].astype(BF16)
            dst += n_dst

    xb = x_ref[...].astype(BF16)
    off = 0
    for o_ref in o_refs:
        n = o_ref.shape[-1]
        o_ref[...] = jnp.dot(xb, w_scr[:, off:off + n],
                             preferred_element_type=F32).astype(o_ref.dtype)
        off += n


def _in_proj(x2d, w_in, layer):
    t = x2d.shape[0]
    tm = 1024
    return pl.pallas_call(
        _proj_kernel,
        grid=(t // tm,),
        in_specs=[pl.BlockSpec((tm, D_MODEL), lambda i: (i, 0)),
                  pl.BlockSpec((None, D_MODEL, w_in.shape[2]), lambda i: (layer, 0, 0),
                               pl.Buffered(1))],
        out_specs=[pl.BlockSpec((tm, n), lambda i: (i, 0)) for n in PROJ_WIDTHS],
        out_shape=[jax.ShapeDtypeStruct((t, n), dt) for n, dt in zip(PROJ_WIDTHS, PROJ_DTYPES)],
        scratch_shapes=[pltpu.VMEM((D_MODEL, sum(PROJ_WIDTHS)), BF16)],
        compiler_params=_cparams(("arbitrary",)),
        name="in_proj",
    )(x2d, w_in)


def _conv_kernel(p_ref, w_ref, o_ref):
    u = p_ref[0, :, 0:W_MIX].astype(F32)
    gb = p_ref[0, :, W_MIX:2 * W_MIX].astype(F32)
    gc = p_ref[0, :, 2 * W_MIX:3 * W_MIX].astype(F32)
    cu = gc * u
    row = lax.broadcasted_iota(jnp.int32, cu.shape, 0)
    acc = cu * w_ref[2:3, :]
    for s in (1, 2):
        sh = jnp.where(row >= s, pltpu.roll(cu, s, axis=0), 0.0)
        acc = acc + sh * w_ref[2 - s:3 - s, :]
    o_ref[0] = (gb * acc).astype(o_ref.dtype)


def _conv_mixer(pc, conv_w):
    b, s, _ = pc.shape
    return pl.pallas_call(
        _conv_kernel,
        grid=(b,),
        in_specs=[pl.BlockSpec((1, s, 3 * W_MIX), lambda i: (i, 0, 0)),
                  pl.BlockSpec((3, W_MIX), lambda i: (0, 0))],
        out_specs=pl.BlockSpec((1, s, W_MIX), lambda i: (i, 0, 0)),
        out_shape=jax.ShapeDtypeStruct((b, s, W_MIX), BF16),
        compiler_params=_cparams(("parallel",)),
        name="conv_mixer",
    )(pc, conv_w)


def _gla_setup(p_ref, lr_ref, wlr_ref, blr_ref, ng_ref, o_ref, st_ref):
    c = GLA_CHUNK
    nh, dk, dv = GLA_HEADS, GLA_DK, GLA_DV

    rb = GLA_ROWS
    ncb = rb // c
    ri = lax.broadcasted_iota(jnp.int32, (rb, rb), 0)
    ci = lax.broadcasted_iota(jnp.int32, (rb, rb), 1)
    tri = (ci <= ri).astype(BF16)
    klane_head = lax.broadcasted_iota(jnp.int32, (1, nh * dk), 1) // dk
    vlane_head = lax.broadcasted_iota(jnp.int32, (1, nh * dv), 1) // dv
    strow_head = lax.broadcasted_iota(jnp.int32, (nh * dv, 1), 0) // dv
    st_mask = strow_head == klane_head
    r4 = lax.broadcasted_iota(jnp.int32, (nh * c, c), 0) % c
    c4 = lax.broadcasted_iota(jnp.int32, (nh * c, c), 1)
    causal4 = c4 <= r4
    gi = lax.broadcasted_iota(jnp.int32, (nh * dv, nh * dv), 0) // dv
    gj = lax.broadcasted_iota(jnp.int32, (nh * dv, nh * dv), 1) // dv
    gmean = jnp.where(gi == gj, 1.0 / dv, 0.0).astype(BF16)
    wlr_hi, wlr_lo = _split_bf16(wlr_ref[...], 2)

    def one_seq(bb, rows):
        q = p_ref[bb, rows, 0:128].astype(F32) * (dk ** -0.5)
        k = p_ref[bb, rows, 128:256].astype(F32)
        vb = p_ref[bb, rows, 256:512]
        g = p_ref[bb, rows, 512:768].astype(F32)
        lr = lr_ref[bb, rows, :]
        lr_hi, lr_lo = _split_bf16(lr, 2)
        z = (jnp.dot(lr_hi, wlr_hi, preferred_element_type=F32)
             + jnp.dot(lr_hi, wlr_lo, preferred_element_type=F32)
             + jnp.dot(lr_lo, wlr_hi, preferred_element_type=F32)) + blr_ref[...]
        log_a = (jnp.minimum(z, 0.0) - jnp.log(1.0 + jnp.exp(-jnp.abs(z)))) * (1.0 / GLA_TAU)
        cumb = _dot_split_rhs(tri, log_a, 3)
        ends = [cumb[(j + 1) * c - 1:(j + 1) * c, :] for j in range(ncb)]
        starts = [jnp.zeros_like(ends[0])] + ends[:-1]
        cum = cumb - jnp.concatenate([jnp.broadcast_to(s0, (c, nh * dk)) for s0 in starts], axis=0)
        lasts = [e - s0 for e, s0 in zip(ends, starts)]
        cl = jnp.concatenate([jnp.broadcast_to(x, (c, nh * dk)) for x in lasts], axis=0)
        q_dec = q * jnp.exp(cum)
        k_inv = (k * jnp.exp(-cum)).astype(BF16)
        k_end = (k * jnp.exp(cl - cum)).astype(BF16)
        st = st_ref[bb]
        outs = []
        for j in range(ncb):
            sl = slice(j * c, (j + 1) * c)
            qd = q_dec[sl]
            qs = jnp.concatenate([jnp.where(klane_head == h, qd, 0.0) for h in range(nh)],
                                 axis=0).astype(BF16)
            att = jnp.where(causal4, _dot_nt(qs, k_inv[sl]), 0.0)
            r = jnp.dot(att.astype(BF16), vb[sl], preferred_element_type=F32)
            o = jnp.where(vlane_head == 0, r[0:c], 0.0)
            for h in range(1, nh):
                o = o + jnp.where(vlane_head == h, r[h * c:(h + 1) * c], 0.0)
            outs.append(o + _dot_nt(qd.astype(BF16), st.astype(BF16)))
            d_st = _dot_tn(vb[sl], k_end[sl])
            st = st * jnp.exp(lasts[j]) + jnp.where(st_mask, d_st, 0.0)
        st_ref[bb] = st
        o = jnp.concatenate(outs, axis=0)
        ms = _dot_split_lhs(o * o, gmean, 2)
        o = o * lax.rsqrt(ms + RMS_EPS) * ng_ref[...]
        o_ref[bb, rows, :] = (o * (g * _sigmoid(g))).astype(o_ref.dtype)

    return one_seq


def _ssd_setup(p_ref, dt_ref, cw_ref, cb_ref, alog_ref, dtb_ref, dsk_ref, ng_ref, o_ref, st_ref,
               halo_ref):
    c = SSD_CHUNK
    n_st = SSD_STATE
    halo = 2 * SUBLANES

    ri = lax.broadcasted_iota(jnp.int32, (c, c), 0)
    ci = lax.broadcasted_iota(jnp.int32, (c, c), 1)
    causal = ci <= ri
    tri = causal.astype(BF16)
    upper = (ri <= ci).astype(BF16)
    lane_head = lax.broadcasted_iota(jnp.int32, (1, W_MIX), 1) // SSD_HEADDIM
    lane_group = lane_head // (SSD_HEADS // SSD_GROUPS)
    eh = lax.broadcasted_iota(jnp.int32, (LANES, W_MIX), 0)
    el = lax.broadcasted_iota(jnp.int32, (LANES, W_MIX), 1) // SSD_HEADDIM
    expand = (eh == el).astype(BF16)
    row8 = lax.broadcasted_iota(jnp.int32, (8, 3 * W_MIX), 0)
    a_c = -jnp.exp(alog_ref[...])

    def one_chunk(n, bb):
        rows = slice(n * c, (n + 1) * c)
        cur = p_ref[bb, rows, 256:1024].astype(F32)
        before = halo_ref[bb] if n == 0 else p_ref[bb, n * c - halo:n * c, 256:1024]
        prev8 = before.astype(F32)[SUBLANES:]
        acc = cur * cw_ref[3:4, :] + cb_ref[...]
        for s in (1, 2, 3):
            sh = pltpu.roll(cur, s, axis=0)
            top = jnp.where(row8 < s, pltpu.roll(prev8, s, axis=0), sh[0:8])
            sh = jnp.concatenate([top, sh[8:]], axis=0)
            acc = acc + sh * cw_ref[3 - s:4 - s, :]
        xbc = acc * _sigmoid(acc)
        x = xbc[:, 0:256]
        bm = xbc[:, 256:512].astype(BF16)
        cm = xbc[:, 512:768].astype(BF16)

        dt_c = _softplus(dt_ref[bb, rows, :] + dtb_ref[...])
        da_c = dt_c * a_c
        cum_c = _dot_split_rhs(tri, da_c, 3)
        cum_r = _dot_split_lhs(da_c, upper, 3, dot=_dot_tn)
        both_x = _dot_split_lhs(jnp.concatenate([dt_c, cum_c], axis=0), expand, 3)
        dt_x = both_x[0:c]
        cum_x = both_x[c:2 * c]
        cl_x = cum_x[c - 1:c, :]
        x_dt = x * dt_x
        x_dt_b = x_dt.astype(BF16)
        xw_b = (x_dt * jnp.exp(cl_x - cum_x)).astype(BF16)

        y = x * dsk_ref[...]
        y_off = jnp.zeros((c, W_MIX), F32)
        for g in range(SSD_GROUPS):
            bg = bm[:, g * n_st:(g + 1) * n_st]
            cg = cm[:, g * n_st:(g + 1) * n_st]
            cb = _dot_nt(cg, bg)
            for r in range(SSD_HEADS // SSD_GROUPS):
                h = g * (SSD_HEADS // SSD_GROUPS) + r
                diff = cum_c[:, h:h + 1] - cum_r[h:h + 1, :]
                dec = jnp.exp(jnp.where(causal, diff, -jnp.inf))
                m = (cb * dec).astype(BF16)
                yh = jnp.dot(m, x_dt_b, preferred_element_type=F32)
                y = y + jnp.where(lane_head == h, yh, 0.0)
            st = st_ref[bb, g]
            y_off = y_off + jnp.where(lane_group == g,
                                      jnp.dot(cg, st.astype(BF16), preferred_element_type=F32), 0.0)
            d_st = _dot_tn(bg, xw_b)
            st_ref[bb, g] = st * jnp.exp(cl_x) + jnp.where(lane_group == g, d_st, 0.0)
        y = y + y_off * jnp.exp(cum_x)
        zg = p_ref[bb, rows, 0:256].astype(F32)
        y = y * (zg * _sigmoid(zg))
        outs = []
        for g in range(SSD_GROUPS):
            yg = y[:, g * 128:(g + 1) * 128]
            ms = jnp.mean(yg * yg, axis=-1, keepdims=True)
            outs.append(yg * lax.rsqrt(ms + RMS_EPS))
        o_ref[bb, rows, :] = (jnp.concatenate(outs, axis=-1) * ng_ref[...]).astype(o_ref.dtype)

    return one_chunk


def _recurrent_kernel(pg_ref, lr_ref, wlr_ref, blr_ref, gng_ref,
                      ps_ref, dt_ref, cw_ref, cb_ref, alog_ref, dtb_ref, dsk_ref, sng_ref,
                      og_ref, os_ref, gst_ref, sst_ref, halo_ref):
    @pl.when(pl.program_id(1) == 0)
    def _():
        gst_ref[...] = jnp.zeros_like(gst_ref)
        sst_ref[...] = jnp.zeros_like(sst_ref)
        halo_ref[...] = jnp.zeros_like(halo_ref)

    gla_rows = _gla_setup(pg_ref, lr_ref, wlr_ref, blr_ref, gng_ref, og_ref, gst_ref)
    ssd_chunk = _ssd_setup(ps_ref, dt_ref, cw_ref, cb_ref, alog_ref, dtb_ref, dsk_ref, sng_ref,
                           os_ref, sst_ref, halo_ref)
    for bb in range(pg_ref.shape[0]):
        gla_rows(bb, slice(0, GLA_ROWS))
        for n in range(GLA_ROWS // SSD_CHUNK):
            ssd_chunk(n, bb)
        halo_ref[bb] = ps_ref[bb, GLA_ROWS - halo_ref.shape[1]:GLA_ROWS, 256:1024]


def _recurrent_mixers(pg, plr, w_lr_pad, b_lr, gla_norm_g4,
                      ps, pdt, conv_w, conv_b, a_log_c, dt_bias_c, d_x, ssd_norm_g):
    b, s, _ = pg.shape
    nb = REC_SEQS
    rb = GLA_ROWS
    seq = lambda i, j: (i, j, 0)
    full2 = lambda i, j: (0, 0)
    return pl.pallas_call(
        _recurrent_kernel,
        grid=(b // nb, s // rb),
        in_specs=[pl.BlockSpec((nb, rb, pg.shape[2]), seq),
                  pl.BlockSpec((nb, rb, LANES), seq),
                  pl.BlockSpec((LANES, LANES), full2),
                  pl.BlockSpec((1, LANES), full2),
                  pl.BlockSpec((1, W_MIX), full2),
                  pl.BlockSpec((nb, rb, ps.shape[2]), seq),
                  pl.BlockSpec((nb, rb, LANES), seq),
                  pl.BlockSpec((SSD_CONV_K, 3 * W_MIX), full2),
                  pl.BlockSpec((1, 3 * W_MIX), full2),
                  pl.BlockSpec((1, LANES), full2),
                  pl.BlockSpec((1, LANES), full2),
                  pl.BlockSpec((1, W_MIX), full2),
                  pl.BlockSpec((1, W_MIX), full2)],
        out_specs=[pl.BlockSpec((nb, rb, W_MIX), seq), pl.BlockSpec((nb, rb, W_MIX), seq)],
        out_shape=[jax.ShapeDtypeStruct((b, s, W_MIX), BF16),
                   jax.ShapeDtypeStruct((b, s, W_MIX), BF16)],
        scratch_shapes=[pltpu.VMEM((nb, GLA_HEADS * GLA_DV, GLA_HEADS * GLA_DK), F32),
                        pltpu.VMEM((nb, SSD_GROUPS, SSD_STATE, W_MIX), F32),
                        pltpu.VMEM((nb, 2 * SUBLANES, 3 * W_MIX), BF16)],
        compiler_params=_cparams(("parallel", "arbitrary")),
        name="gla_ssd_mixers",
    )(pg, plr, w_lr_pad, b_lr, gla_norm_g4, ps, pdt, conv_w, conv_b, a_log_c, dt_bias_c, d_x,
      ssd_norm_g)


DIFF_TQ = 256
DIFF_TK = 256
LOG2E = 1.4426950408889634
DIFF_VPAD = DIFF_DV + 16
DIFF_SEQS = 4


def _diff_kernel(q_ref, k_ref, v_ref, lam_ref, ng_ref, o_ref,
                 vt_ref, qs_ref, st_ref, m_ref, acc_ref, *, lam_init):
    tq, tk = DIFF_TQ, DIFF_TK
    nh, dv = DIFF_HEADS, DIFF_DV
    nhc = 2 * nh
    s_len = k_ref.shape[1]
    i = pl.program_id(1)
    seqs = range(q_ref.shape[0])

    @pl.when(i == 0)
    def _():
        for bb in seqs:
            for cblk in range(s_len // tk):
                cols = slice(cblk * tk, (cblk + 1) * tk)
                vt = v_ref[bb, cols, :].astype(F32).T.astype(BF16)
                for h in range(nh):
                    vt_ref[bb, h, 0:dv, cols] = vt[h * dv:(h + 1) * dv]
        vt_ref[:, :, dv:, :] = jnp.ones((len(seqs), nh, DIFF_VPAD - dv, s_len), BF16)

    qlane = lax.broadcasted_iota(jnp.int32, (1, W_MIX), 1) // DIFF_DQK
    for bb in seqs:
        q = q_ref[bb].astype(F32) * (DIFF_DQK ** -0.5 * LOG2E)
        for hc in range(nhc):
            qs_ref[bb, hc * tq:(hc + 1) * tq, :] = jnp.where(qlane == hc, q, 0.0).astype(BF16)
    m_ref[...] = jnp.full_like(m_ref, -jnp.inf)
    acc_ref[...] = jnp.zeros_like(acc_ref)
    krow = lax.broadcasted_iota(jnp.int32, (tk, nhc * tq), 0)
    qcol = lax.broadcasted_iota(jnp.int32, (tk, nhc * tq), 1) % tq
    diag_ok = krow <= qcol

    def scores(j, slot):
        k0 = pl.multiple_of(j * tk, tk)
        for bb in seqs:
            st_ref[bb, slot] = _dot_nt(k_ref[bb, pl.ds(k0, tk), :], qs_ref[bb])

    def softmax_pv(j, slot, masked):
        for bb in seqs:
            softmax_pv_seq(bb, j, slot, masked)

    def softmax_pv_seq(bb, j, slot, masked):
        k0 = pl.multiple_of(j * tk, tk)
        st = st_ref[bb, slot]
        if masked:
            st = jnp.where(diag_ok, st, -jnp.inf)
        m_prev = m_ref[bb]
        m_new = jnp.maximum(m_prev, jnp.max(st, axis=0, keepdims=True))
        alpha = jnp.exp2(m_prev - m_new)
        p = jnp.exp2(st - m_new)
        m_ref[bb] = m_new
        pb = p.astype(BF16)
        for hc in range(nhc):
            h = hc // 2
            lanes = slice(hc * tq, (hc + 1) * tq)
            pv = jnp.dot(vt_ref[bb, h, :, pl.ds(k0, tk)], pb[:, lanes],
                         preferred_element_type=F32)
            acc_ref[bb, hc] = acc_ref[bb, hc] * alpha[:, lanes] + pv

    scores(0, 0)
    n_pairs = i // 2

    def pair_step(u, carry):
        scores(2 * u + 1, 1)
        softmax_pv(2 * u, 0, False)
        scores(2 * u + 2, 0)
        softmax_pv(2 * u + 1, 1, False)
        return carry

    lax.fori_loop(0, n_pairs, pair_step, 0)

    @pl.when(i % 2 == 0)
    def _():
        softmax_pv(i, 0, True)

    @pl.when(i % 2 == 1)
    def _():
        scores(i, 1)
        softmax_pv(i - 1, 0, False)
        softmax_pv(i, 1, True)

    lam = (jnp.exp(jnp.sum(lam_ref[0:1, :] * lam_ref[1:2, :], axis=-1, keepdims=True))
           - jnp.exp(jnp.sum(lam_ref[2:3, :] * lam_ref[3:4, :], axis=-1, keepdims=True))
           + lam_init)
    for bb in seqs:
        heads = []
        for h in range(nh):
            o1 = acc_ref[bb, 2 * h, 0:dv] / acc_ref[bb, 2 * h, dv:dv + 1]
            o2 = acc_ref[bb, 2 * h + 1, 0:dv] / acc_ref[bb, 2 * h + 1, dv:dv + 1]
            oh = o1 - lam * o2
            ms = jnp.mean(oh * oh, axis=0, keepdims=True)
            heads.append(oh * lax.rsqrt(ms + RMS_EPS))
        o = jnp.concatenate(heads, axis=0).T
        o_ref[bb] = (o * ng_ref[...] * (1.0 - lam_init)).astype(o_ref.dtype)


def _diff_mixer(pd, lam_vecs, norm_g4, lam_init):
    b, s, _ = pd.shape
    tq = DIFF_TQ
    nb = DIFF_SEQS
    return pl.pallas_call(
        functools.partial(_diff_kernel, lam_init=lam_init),
        grid=(b // nb, s // tq),
        in_specs=[pl.BlockSpec((nb, tq, W_MIX), lambda bi, i: (bi, i, 0)),
                  pl.BlockSpec((nb, s, W_MIX), lambda bi, i: (bi, 0, 1)),
                  pl.BlockSpec((nb, s, W_MIX), lambda bi, i: (bi, 0, 2)),
                  pl.BlockSpec((4, LANES), lambda bi, i: (0, 0)),
                  pl.BlockSpec((1, W_MIX), lambda bi, i: (0, 0))],
        out_specs=pl.BlockSpec((nb, tq, W_MIX), lambda bi, i: (bi, i, 0)),
        out_shape=jax.ShapeDtypeStruct((b, s, W_MIX), BF16),
        scratch_shapes=[pltpu.VMEM((nb, DIFF_HEADS, DIFF_VPAD, s), BF16),
                        pltpu.VMEM((nb, 2 * DIFF_HEADS * tq, W_MIX), BF16),
                        pltpu.VMEM((nb, 2, DIFF_TK, 2 * DIFF_HEADS * tq), F32),
                        pltpu.VMEM((nb, 1, 2 * DIFF_HEADS * tq), F32),
                        pltpu.VMEM((nb, 2 * DIFF_HEADS, DIFF_VPAD, tq), F32)],
        compiler_params=_cparams(("parallel", "arbitrary")),
        name="diff_attn",
    )(pd, pd, pd, lam_vecs, norm_g4)


def _oproj_kernel(yc_ref, yg_ref, ys_ref, yd_ref, x_ref, wo_ref, g_ref, b_ref, wr_ref,
                  xo_ref, xp_ref, route_ref, cnt_ref):
    mix = jnp.concatenate([yc_ref[...], yg_ref[...], ys_ref[...], yd_ref[...]], axis=-1)
    h = ALPHA * x_ref[...] + jnp.dot(mix, wo_ref[...], preferred_element_type=F32)
    xn = _layer_norm(h, g_ref[...], b_ref[...])
    xo_ref[...] = xn
    xp_ref[...] = _pack_halves(xn)

    xn_hi, xn_lo = _split_bf16(xn, 2)
    both = _dot(xn_hi, wr_ref[...])
    logits = both[:, 0:LANES] + both[:, LANES:2 * LANES] + _dot(xn_lo, wr_ref[:, 0:LANES])
    lane = lax.broadcasted_iota(jnp.int32, logits.shape, 1).astype(F32)
    neg = -jnp.inf
    big = float(LANES)
    lg = jnp.where(lane < N_GROUPS, logits, neg)
    mg = jnp.max(lg, axis=-1, keepdims=True)
    sg = jnp.sum(jnp.exp(lg - mg), axis=-1, keepdims=True)
    grp = jnp.min(jnp.where(lg == mg, lane, big), axis=-1, keepdims=True)
    p_grp = 1.0 / sg
    lo = N_GROUPS + EXPERTS_PER_GROUP * grp
    in_g = jnp.logical_and(lane >= lo, lane < lo + EXPERTS_PER_GROUP)
    le = jnp.where(in_g, logits, neg)
    me = jnp.max(le, axis=-1, keepdims=True)
    ee = jnp.exp(le - me)
    pe = ee / jnp.sum(ee, axis=-1, keepdims=True)
    pe = jnp.where(in_g, pe, -1.0)
    p1 = jnp.max(pe, axis=-1, keepdims=True)
    i1 = jnp.min(jnp.where(pe == p1, lane, big), axis=-1, keepdims=True)
    pe2 = jnp.where(lane == i1, -1.0, pe)
    p2 = jnp.max(pe2, axis=-1, keepdims=True)
    i2 = jnp.min(jnp.where(pe2 == p2, lane, big), axis=-1, keepdims=True)
    den = p1 + p2
    g1 = p_grp * p1 / den
    g2 = p_grp * p2 / den
    e1 = i1 - N_GROUPS
    e2 = i2 - N_GROUPS
    route_ref[...] = jnp.where(lane == 0, e1, jnp.where(lane == 1, e2, jnp.where(
        lane == 2, g1, jnp.where(lane == 3, g2, 0.0))))

    @pl.when(pl.program_id(0) == 0)
    def _():
        cnt_ref[...] = jnp.zeros_like(cnt_ref)

    hits = jnp.where(lane == e1, 1.0, 0.0) + jnp.where(lane == e2, 1.0, 0.0)
    cnt_ref[...] += jnp.sum(hits, axis=0, keepdims=True)


def _out_proj(ys, x2d, w_o, layer, ln_g, ln_b, w_route):
    t = x2d.shape[0]
    tm = 1024
    row = lambda i: (i, 0)
    full = lambda i: (0, 0)
    return pl.pallas_call(
        _oproj_kernel,
        grid=(t // tm,),
        in_specs=[pl.BlockSpec((tm, W_MIX), row)] * 4 + [
            pl.BlockSpec((tm, D_MODEL), row),
            pl.BlockSpec((None, D_MODEL, D_MODEL), lambda i: (layer, 0, 0)),
            pl.BlockSpec((1, D_MODEL), full),
            pl.BlockSpec((1, D_MODEL), full),
            pl.BlockSpec((D_MODEL, 2 * LANES), full)],
        out_specs=[pl.BlockSpec((tm, D_MODEL), row), pl.BlockSpec((tm, D_MODEL // 2), row),
                   pl.BlockSpec((tm, LANES), row), pl.BlockSpec((1, LANES), full)],
        out_shape=[jax.ShapeDtypeStruct((t, D_MODEL), F32),
                   jax.ShapeDtypeStruct((t, D_MODEL // 2), U32),
                   jax.ShapeDtypeStruct((t, LANES), F32),
                   jax.ShapeDtypeStruct((1, LANES), F32)],
        compiler_params=_cparams(("arbitrary",)),
        name="out_proj_ln_router",
    )(*ys, x2d, w_o, ln_g, ln_b, w_route)


PLAN_TILE = 512


def _plan_kernel(route_ref, cnt_ref, dest_ref, meta_ref, carry_ref, pstart_ref):
    tm = route_ref.shape[0]
    lane = lax.broadcasted_iota(jnp.int32, (1, LANES), 1).astype(F32)

    @pl.when(pl.program_id(0) == 0)
    def _():
        cnt = cnt_ref[...]
        padded = jnp.ceil(cnt * (1.0 / MOE_BLK)) * MOE_BLK
        li = lax.broadcasted_iota(jnp.int32, (LANES, LANES), 0)
        lj = lax.broadcasted_iota(jnp.int32, (LANES, LANES), 1)
        before = (li < lj).astype(F32)
        pstart = jnp.dot(jnp.broadcast_to(padded, (8, LANES)), before, precision=HI,
                         preferred_element_type=F32)[0:1]
        pstart_ref[...] = pstart
        carry_ref[...] = jnp.zeros_like(carry_ref)
        meta_ref[...] = jnp.concatenate(
            [pstart + padded, pstart, cnt, jnp.zeros((5, LANES), F32)], axis=0)

    oh0 = jnp.where(lane == route_ref[:, 0:1], 1.0, 0.0)
    oh1 = jnp.where(lane == route_ref[:, 1:2], 1.0, 0.0)
    both = oh0 + oh1
    ri = lax.broadcasted_iota(jnp.int32, (tm, tm), 0)
    ci = lax.broadcasted_iota(jnp.int32, (tm, tm), 1)
    earlier = (ci < ri).astype(BF16)
    base = (jnp.dot(earlier, both.astype(BF16), preferred_element_type=F32)
            + carry_ref[...] + pstart_ref[...])
    d0 = jnp.sum(oh0 * base, axis=-1, keepdims=True)
    d1 = jnp.sum(oh1 * base, axis=-1, keepdims=True)
    dest = jnp.where(lane == 0, d0, jnp.where(lane == 1, d1, 0.0))
    dest_ref[...] = dest.T[0:2, :].astype(jnp.int32)
    carry_ref[...] += jnp.sum(both, axis=0, keepdims=True)


def _dispatch_plan(route, cnt, n_blocks):
    t = route.shape[0]
    tm = PLAN_TILE
    blk = MOE_BLK
    dest, meta = pl.pallas_call(
        _plan_kernel,
        grid=(t // tm,),
        in_specs=[pl.BlockSpec((tm, LANES), lambda i: (i, 0)),
                  pl.BlockSpec((1, LANES), lambda i: (0, 0))],
        out_specs=[pl.BlockSpec((2, tm), lambda i: (0, i)),
                   pl.BlockSpec((8, LANES), lambda i: (0, 0))],
        out_shape=[jax.ShapeDtypeStruct((2, t), jnp.int32),
                   jax.ShapeDtypeStruct((8, LANES), F32)],
        scratch_shapes=[pltpu.VMEM((1, LANES), F32), pltpu.VMEM((1, LANES), F32)],
        compiler_params=_cparams(("arbitrary",)),
        name="moe_plan",
    )(route, cnt)
    meta_i = meta[:, :N_EXPERTS].astype(jnp.int32)
    pad_end, pad_start, seg_end = meta_i[0], meta_i[1], meta_i[1] + meta_i[2]
    starts = jnp.arange(n_blocks, dtype=jnp.int32)[:, None] * blk
    member = jnp.logical_and(starts >= pad_start[None, :], starts < pad_end[None, :])
    expert_ids = jnp.arange(N_EXPERTS, dtype=jnp.int32)[None, :]
    block_e = jnp.where(jnp.any(member, axis=1), jnp.sum(jnp.where(member, expert_ids, 0), axis=1),
                        N_EXPERTS - 1)
    n_used = (pad_end[N_EXPERTS - 1] // blk).reshape(1)
    n_valid = jnp.clip(jnp.sum(jnp.where(member, seg_end[None, :], 0), axis=1) - starts[:, 0],
                       0, blk)
    return dest.reshape(2 * t), block_e, n_used, n_valid


def _ffn_kernel(be_ref, nu_ref, nv_ref, xs_ref, wg_ref, wu_ref, wd_ref, y_ref,
                wgb_ref, wub_ref, wdb_ref):
    i = pl.program_id(0)
    prev = be_ref[jnp.maximum(i - 1, 0)]

    @pl.when(jnp.logical_or(i == 0, be_ref[i] != prev))
    def _():
        wgb_ref[...] = wg_ref[...].astype(BF16)
        wub_ref[...] = wu_ref[...].astype(BF16)
        wdb_ref[...] = wd_ref[...].astype(BF16)

    @pl.when(i < nu_ref[0])
    def _():
        half = xs_ref.shape[0] // 2
        for r in range(2):
            rows = slice(r * half, (r + 1) * half)
            row = lax.broadcasted_iota(jnp.int32, (half, 1), 0) + r * half
            xp = jnp.where(row < nv_ref[i], xs_ref[rows, :], U32(0))
            x_hi, x_lo = _unpack_halves(xp)
            xb = jnp.concatenate([x_hi.astype(BF16), x_lo.astype(BF16)], axis=1)
            a = jnp.dot(xb, wgb_ref[...], preferred_element_type=F32)
            u = jnp.dot(xb, wub_ref[...], preferred_element_type=F32)
            h = (a * _sigmoid(a) * u).astype(BF16)
            y_ref[rows, :] = _pack_halves(jnp.dot(h, wdb_ref[...], preferred_element_type=F32))

    @pl.when(i >= nu_ref[0])
    def _():
        y_ref[...] = jnp.zeros_like(y_ref)


def _expert_ffn(xs, block_e, n_used, n_valid, w_gate, w_up, w_down, layer):
    n_slots = xs.shape[0]
    blk = MOE_BLK
    w_map = lambda i, be, nu, nv: (layer, be[i], 0, 0)
    grid_spec = pltpu.PrefetchScalarGridSpec(
        num_scalar_prefetch=3,
        grid=(n_slots // blk,),
        in_specs=[pl.BlockSpec((blk, D_MODEL // 2),
                               lambda i, be, nu, nv: (jnp.minimum(i, nu[0] - 1), 0)),
                  pl.BlockSpec((None, None, D_MODEL, D_EXPERT), w_map),
                  pl.BlockSpec((None, None, D_MODEL, D_EXPERT), w_map),
                  pl.BlockSpec((None, None, D_EXPERT, D_MODEL), w_map)],
        out_specs=pl.BlockSpec((blk, D_MODEL // 2), lambda i, be, nu, nv: (i, 0)),
        scratch_shapes=[pltpu.VMEM((D_MODEL, D_EXPERT), BF16),
                        pltpu.VMEM((D_MODEL, D_EXPERT), BF16),
                        pltpu.VMEM((D_EXPERT, D_MODEL), BF16)],
    )
    return pl.pallas_call(
        _ffn_kernel,
        grid_spec=grid_spec,
        out_shape=jax.ShapeDtypeStruct((n_slots, D_MODEL // 2), U32),
        compiler_params=_cparams(("arbitrary",)),
        name="expert_ffn",
    )(block_e, n_used, n_valid, xs, w_gate, w_up, w_down)


SC_CORES = 2
SC_SUBCORES = 16
SC_ROWS = 64


def _sc_gather_rows(table, idx):
    b = idx.shape[0]
    d = table.shape[1]
    per_w = b // (SC_CORES * SC_SUBCORES)
    mesh = plsc.VectorSubcoreMesh(core_axis_name="c", subcore_axis_name="s")

    n_chunks = per_w // SC_ROWS

    @functools.partial(
        pl.kernel, mesh=mesh,
        out_type=jax.ShapeDtypeStruct((b, d), table.dtype),
        scratch_types=[pltpu.VMEM((SC_ROWS,), jnp.int32), pltpu.VMEM((SC_ROWS,), jnp.int32),
                       pltpu.VMEM((SC_ROWS, d), table.dtype),
                       pltpu.VMEM((SC_ROWS, d), table.dtype),
                       pltpu.SemaphoreType.DMA, pltpu.SemaphoreType.DMA,
                       pltpu.SemaphoreType.DMA, pltpu.SemaphoreType.DMA],
        name="sc_gather_rows",
    )
    def gather(table_hbm, idx_hbm, out_hbm, idx0, idx1, rows0, rows1, gs0, gs1, ws0, ws1):
        idx_v, rows_v, gsem, wsem = (idx0, idx1), (rows0, rows1), (gs0, gs1), (ws0, ws1)
        wid = lax.axis_index("s") * SC_CORES + lax.axis_index("c")
        base = wid * per_w

        def rows_of(c):
            return pl.ds(pl.multiple_of(base + c * SC_ROWS, SC_ROWS), SC_ROWS)

        def start_gather(c, s):
            pltpu.sync_copy(idx_hbm.at[rows_of(c)], idx_v[s])
            pltpu.async_copy(table_hbm.at[idx_v[s]], rows_v[s], gsem[s])

        def write_back(c, s):
            pltpu.make_async_copy(table_hbm.at[idx_v[s]], rows_v[s], gsem[s]).wait()
            pltpu.async_copy(rows_v[s], out_hbm.at[rows_of(c)], wsem[s]).wait()

        start_gather(0, 0)

        @pl.loop(0, n_chunks, step=2)
        def _(c):
            start_gather(c + 1, 1)
            write_back(c, 0)

            @pl.when(c + 2 < n_chunks)
            def _():
                start_gather(c + 2, 0)

            write_back(c + 1, 1)

    return gather(table, idx)


def _sc_scatter_rows(x2d, idx, n_slots):
    t, d = x2d.shape
    per_w = t // (SC_CORES * SC_SUBCORES)
    mesh = plsc.VectorSubcoreMesh(core_axis_name="c", subcore_axis_name="s")

    @functools.partial(
        pl.kernel, mesh=mesh,
        out_type=jax.ShapeDtypeStruct((n_slots, d), x2d.dtype),
        scratch_types=[pltpu.VMEM((SC_ROWS,), jnp.int32), pltpu.VMEM((SC_ROWS,), jnp.int32),
                       pltpu.VMEM((SC_ROWS, d), x2d.dtype),
                       pltpu.SemaphoreType.DMA, pltpu.SemaphoreType.DMA],
        name="sc_scatter_rows",
    )
    def scatter(x_hbm, idx_hbm, out_hbm, idx0, idx1, rows_v, s0, s1):
        wid = lax.axis_index("s") * SC_CORES + lax.axis_index("c")
        base = wid * per_w

        @pl.loop(0, per_w // SC_ROWS)
        def _(c):
            off = pl.multiple_of(base + c * SC_ROWS, SC_ROWS)
            pltpu.sync_copy(x_hbm.at[pl.ds(off, SC_ROWS)], rows_v)
            pltpu.sync_copy(idx_hbm.at[pl.ds(off, SC_ROWS)], idx0)
            pltpu.sync_copy(idx_hbm.at[pl.ds(t + off, SC_ROWS)], idx1)
            cp0 = pltpu.async_copy(rows_v, out_hbm.at[idx0], s0)
            cp1 = pltpu.async_copy(rows_v, out_hbm.at[idx1], s1)
            cp0.wait()
            cp1.wait()

    return scatter(x2d, idx)


def _combine_dense_kernel(route_ref, x_ref, y0_ref, y1_ref, g_ref, b_ref, o_ref):
    y0 = jnp.concatenate(_unpack_halves(y0_ref[...]), axis=1)
    y1 = jnp.concatenate(_unpack_halves(y1_ref[...]), axis=1)
    moe = route_ref[:, 2:3] * y0 + route_ref[:, 3:4] * y1
    h = ALPHA * x_ref[...] + moe
    o_ref[...] = _layer_norm(h, g_ref[...], b_ref[...])


def _combine_dense(route, x2d, yg, ln_g, ln_b):
    t = x2d.shape[0]
    tm = 1024
    nt = t // tm
    row = lambda i: (i, 0)
    full = lambda i: (0, 0)
    return pl.pallas_call(
        _combine_dense_kernel,
        grid=(nt,),
        in_specs=[pl.BlockSpec((tm, LANES), row),
                  pl.BlockSpec((tm, D_MODEL), row),
                  pl.BlockSpec((tm, D_MODEL // 2), row),
                  pl.BlockSpec((tm, D_MODEL // 2), lambda i: (i + nt, 0)),
                  pl.BlockSpec((1, D_MODEL), full),
                  pl.BlockSpec((1, D_MODEL), full)],
        out_specs=pl.BlockSpec((tm, D_MODEL), row),
        out_shape=jax.ShapeDtypeStruct((t, D_MODEL), F32),
        compiler_params=_cparams(("parallel",)),
        name="moe_combine_dense",
    )(route, x2d, yg, yg, ln_g, ln_b)


def _pad_cols(w, n):
    return jnp.pad(w, [(0, 0)] * (w.ndim - 1) + [(0, n - w.shape[-1])])


def kernel(x, w_in, conv_w, gla_w_lr, gla_b_lr, gla_norm_g, ssd_conv_w, ssd_conv_b, ssd_a_log,
           ssd_d, ssd_dt_bias, ssd_norm_g, diff_lq1, diff_lk1, diff_lq2, diff_lk2, diff_norm_g,
           w_o, ln1_g, ln1_b, router_g, router_e, w_gate, w_up, w_down, ln2_g, ln2_b):
    bsz, seq, d = x.shape
    t = bsz * seq
    n_assign = 2 * t
    n_blocks = (n_assign + N_EXPERTS * (MOE_BLK - 1)) // MOE_BLK + 1
    n_slots = n_blocks * MOE_BLK
    x2d = x.reshape(t, d)
    w_o_b = w_o.astype(BF16)
    for l in range(DEPTH):
        pc, pg, plr, ps, pdt, pd = [p.reshape(bsz, seq, -1) for p in _in_proj(x2d, w_in, l)]

        y_conv = _conv_mixer(pc, conv_w[l])
        w_lr_pad = jnp.pad(gla_w_lr[l], ((0, LANES - GLA_RANK), (0, 0)))
        pad4 = lambda v: jnp.pad(v, (0, LANES - SSD_HEADS)).reshape(1, LANES)
        y_gla, y_ssd = _recurrent_mixers(
            pg, plr, w_lr_pad, gla_b_lr[l].reshape(1, -1),
            jnp.tile(gla_norm_g[l], GLA_HEADS).reshape(1, -1),
            ps, pdt, ssd_conv_w[l], ssd_conv_b[l].reshape(1, -1),
            pad4(ssd_a_log[l]), pad4(ssd_dt_bias[l]),
            jnp.repeat(ssd_d[l], SSD_HEADDIM).reshape(1, -1), ssd_norm_g[l].reshape(1, -1))
        lam_vecs = jnp.pad(jnp.stack([diff_lq1[l], diff_lk1[l], diff_lq2[l], diff_lk2[l]]),
                           ((0, 0), (0, LANES - DIFF_DQK)))
        lam_init = 0.8 - 0.6 * math.exp(-0.3 * l)
        y_diff = _diff_mixer(pd, lam_vecs,
                             jnp.tile(diff_norm_g[l], DIFF_HEADS).reshape(1, -1), lam_init)

        w_route = _pad_cols(jnp.concatenate(
            [router_g[l], router_e[l].reshape(d, N_EXPERTS)], axis=1), LANES)
        w_route_hi = w_route.astype(BF16)
        w_route = jnp.concatenate(
            [w_route_hi, (w_route - w_route_hi.astype(F32)).astype(BF16)], axis=1)
        ys = [y.reshape(t, W_MIX) for y in (y_conv, y_gla, y_ssd, y_diff)]
        xn, xn_p, route, cnt = _out_proj(ys, x2d, w_o_b, l, ln1_g[l].reshape(1, -1),
                                         ln1_b[l].reshape(1, -1), w_route)

        idx, block_e, n_used, n_valid = _dispatch_plan(route, cnt, n_blocks)
        xs = _sc_scatter_rows(xn_p, idx, n_slots)
        y = _expert_ffn(xs, block_e, n_used, n_valid, w_gate, w_up, w_down, l)
        yg = _sc_gather_rows(y, idx)
        x2d = _combine_dense(route, xn, yg, ln2_g[l].reshape(1, -1), ln2_b[l].reshape(1, -1))
    return x2d.reshape(bsz, seq, d)
```

```python
import functools
import math

import jax
import jax.numpy as jnp
from jax import lax
from jax.experimental import pallas as pl
from jax.experimental.pallas import tpu as pltpu
from jax.experimental.pallas import tpu_sc as plsc

F32 = jnp.float32
BF16 = jnp.bfloat16
HI = lax.Precision.HIGHEST

D_MODEL = 1024
DEPTH = 2
W_MIX = 256
GLA_HEADS, GLA_DK, GLA_DV, GLA_RANK, GLA_TAU, GLA_CHUNK = 4, 32, 64, 16, 16.0, 64
GLA_ROWS = 256
REC_SEQS = 8
SSD_HEADS, SSD_GROUPS, SSD_HEADDIM, SSD_STATE, SSD_CONV_K, SSD_CHUNK = 4, 2, 64, 128, 4, 128
DIFF_HEADS, DIFF_DQK, DIFF_DV = 4, 32, 64
N_GROUPS, EXPERTS_PER_GROUP, N_EXPERTS, D_EXPERT = 4, 8, 32, 512
ALPHA = (2 * DEPTH) ** 0.25
LN_EPS = 1e-5
RMS_EPS = 1e-6

LANES = 128
SUBLANES = 8
PROJ_WIDTHS = (768, 768, 128, 1024, 128, 768)
PROJ_SRC_OFFSETS = (0, 768, 1536, 1552, 2576, 2580)
PROJ_SRC_WIDTHS = (768, 768, GLA_RANK, 1024, SSD_HEADS, 768)
PROJ_DTYPES = (BF16, BF16, F32, BF16, F32, BF16)
VMEM_LIMIT = 56 * 1024 * 1024

MOE_BLK = 512


def _cparams(sem):
    return pltpu.CompilerParams(dimension_semantics=sem, vmem_limit_bytes=VMEM_LIMIT)


def _sigmoid(x):
    return 1.0 / (1.0 + jnp.exp(-x))


def _softplus(x):
    return jnp.maximum(x, 0.0) + jnp.log(1.0 + jnp.exp(-jnp.abs(x)))


def _layer_norm(h, g, b):
    mu = jnp.mean(h, axis=-1, keepdims=True)
    d = h - mu
    var = jnp.mean(d * d, axis=-1, keepdims=True)
    return d * lax.rsqrt(var + LN_EPS) * g + b


def _dot_nt(a, b):
    return lax.dot_general(a, b, (((1,), (1,)), ((), ())), preferred_element_type=F32)


def _dot_tn(a, b, precision=None):
    return lax.dot_general(a, b, (((0,), (0,)), ((), ())), preferred_element_type=F32,
                           precision=precision)


def _split_bf16(x, parts):
    out = []
    for _ in range(parts - 1):
        hi = x.astype(BF16)
        out.append(hi)
        x = x - hi.astype(F32)
    out.append(x.astype(BF16))
    return out


def _dot(a, b):
    return jnp.dot(a, b, preferred_element_type=F32)


U32 = jnp.uint32


def _pack_halves(x):
    w = x.shape[1] // 2
    hi = lax.bitcast_convert_type(x[:, :w].astype(BF16).astype(F32), U32)
    lo = lax.bitcast_convert_type(x[:, w:].astype(BF16).astype(F32), U32)
    return hi | lax.shift_right_logical(lo, U32(16))


def _unpack_halves(p):
    hi = lax.bitcast_convert_type(p & U32(0xFFFF0000), F32)
    lo = lax.bitcast_convert_type(lax.shift_left(p, U32(16)), F32)
    return hi, lo


def _dot_split_lhs(a, b_exact, parts, dot=_dot):
    acc = None
    for term in _split_bf16(a, parts):
        d = dot(term, b_exact)
        acc = d if acc is None else acc + d
    return acc


def _dot_split_rhs(a_exact, b, parts):
    acc = None
    for term in _split_bf16(b, parts):
        d = jnp.dot(a_exact, term, preferred_element_type=F32)
        acc = d if acc is None else acc + d
    return acc


def _proj_kernel(x_ref, w_ref, *refs):
    o_refs, w_scr = refs[:-1], refs[-1]

    @pl.when(pl.program_id(0) == 0)
    def _():
        w_scr[...] = jnp.zeros_like(w_scr)
        dst = 0
        for src, n_src, n_dst in zip(PROJ_SRC_OFFSETS, PROJ_SRC_WIDTHS, PROJ_WIDTHS):
            w_scr[:, dst:dst + n_src] = w_ref[:, src:src + n_src].astype(BF16)
            dst += n_dst

    xb = x_ref[...].astype(BF16)
    off = 0
    for o_ref in o_refs:
        n = o_ref.shape[-1]
        o_ref[...] = jnp.dot(xb, w_scr[:, off:off + n],
                             preferred_element_type=F32).astype(o_ref.dtype)
        off += n


def _in_proj(x2d, w_in, layer):
    t = x2d.shape[0]
    tm = 1024
    return pl.pallas_call(
        _proj_kernel,
        grid=(t // tm,),
        in_specs=[pl.BlockSpec((tm, D_MODEL), lambda i: (i, 0)),
                  pl.BlockSpec((None, D_MODEL, w_in.shape[2]), lambda i: (layer, 0, 0),
                               pl.Buffered(1))],
        out_specs=[pl.BlockSpec((tm, n), lambda i: (i, 0)) for n in PROJ_WIDTHS],
        out_shape=[jax.ShapeDtypeStruct((t, n), dt) for n, dt in zip(PROJ_WIDTHS, PROJ_DTYPES)],
        scratch_shapes=[pltpu.VMEM((D_MODEL, sum(PROJ_WIDTHS)), BF16)],
        compiler_params=_cparams(("arbitrary",)),
        name="in_proj",
    )(x2d, w_in)


def _conv_kernel(p_ref, w_ref, o_ref):
    u = p_ref[0, :, 0:W_MIX].astype(F32)
    gb = p_ref[0, :, W_MIX:2 * W_MIX].astype(F32)
    gc = p_ref[0, :, 2 * W_MIX:3 * W_MIX].astype(F32)
    cu = gc * u
    row = lax.broadcasted_iota(jnp.int32, cu.shape, 0)
    acc = cu * w_ref[2:3, :]
    for s in (1, 2):
        sh = jnp.where(row >= s, pltpu.roll(cu, s, axis=0), 0.0)
        acc = acc + sh * w_ref[2 - s:3 - s, :]
    o_ref[0] = (gb * acc).astype(o_ref.dtype)


def _conv_mixer(pc, conv_w):
    b, s, _ = pc.shape
    return pl.pallas_call(
        _conv_kernel,
        grid=(b,),
        in_specs=[pl.BlockSpec((1, s, 3 * W_MIX), lambda i: (i, 0, 0)),
                  pl.BlockSpec((3, W_MIX), lambda i: (0, 0))],
        out_specs=pl.BlockSpec((1, s, W_MIX), lambda i: (i, 0, 0)),
        out_shape=jax.ShapeDtypeStruct((b, s, W_MIX), BF16),
        compiler_params=_cparams(("parallel",)),
        name="conv_mixer",
    )(pc, conv_w)


def _gla_setup(p_ref, lr_ref, wlr_ref, blr_ref, ng_ref, o_ref, st_ref):
    c = GLA_CHUNK
    nh, dk, dv = GLA_HEADS, GLA_DK, GLA_DV

    rb = GLA_ROWS
    ncb = rb // c
    ri = lax.broadcasted_iota(jnp.int32, (rb, rb), 0)
    ci = lax.broadcasted_iota(jnp.int32, (rb, rb), 1)
    tri = (ci <= ri).astype(BF16)
    klane_head = lax.broadcasted_iota(jnp.int32, (1, nh * dk), 1) // dk
    vlane_head = lax.broadcasted_iota(jnp.int32, (1, nh * dv), 1) // dv
    strow_head = lax.broadcasted_iota(jnp.int32, (nh * dv, 1), 0) // dv
    st_mask = strow_head == klane_head
    r4 = lax.broadcasted_iota(jnp.int32, (nh * c, c), 0) % c
    c4 = lax.broadcasted_iota(jnp.int32, (nh * c, c), 1)
    causal4 = c4 <= r4
    gi = lax.broadcasted_iota(jnp.int32, (nh * dv, nh * dv), 0) // dv
    gj = lax.broadcasted_iota(jnp.int32, (nh * dv, nh * dv), 1) // dv
    gmean = jnp.where(gi == gj, 1.0 / dv, 0.0).astype(BF16)
    wlr_hi, wlr_lo = _split_bf16(wlr_ref[...], 2)

    def one_seq(bb, rows):
        q = p_ref[bb, rows, 0:128].astype(F32) * (dk ** -0.5)
        k = p_ref[bb, rows, 128:256].astype(F32)
        vb = p_ref[bb, rows, 256:512]
        g = p_ref[bb, rows, 512:768].astype(F32)
        lr = lr_ref[bb, rows, :]
        lr_hi, lr_lo = _split_bf16(lr, 2)
        z = (jnp.dot(lr_hi, wlr_hi, preferred_element_type=F32)
             + jnp.dot(lr_hi, wlr_lo, preferred_element_type=F32)
             + jnp.dot(lr_lo, wlr_hi, preferred_element_type=F32)) + blr_ref[...]
        log_a = (jnp.minimum(z, 0.0) - jnp.log(1.0 + jnp.exp(-jnp.abs(z)))) * (1.0 / GLA_TAU)
        cumb = _dot_split_rhs(tri, log_a, 3)
        ends = [cumb[(j + 1) * c - 1:(j + 1) * c, :] for j in range(ncb)]
        starts = [jnp.zeros_like(ends[0])] + ends[:-1]
        cum = cumb - jnp.concatenate([jnp.broadcast_to(s0, (c, nh * dk)) for s0 in starts], axis=0)
        lasts = [e - s0 for e, s0 in zip(ends, starts)]
        cl = jnp.concatenate([jnp.broadcast_to(x, (c, nh * dk)) for x in lasts], axis=0)
        q_dec = q * jnp.exp(cum)
        k_inv = (k * jnp.exp(-cum)).astype(BF16)
        k_end = (k * jnp.exp(cl - cum)).astype(BF16)
        st = st_ref[bb]
        outs = []
        for j in range(ncb):
            sl = slice(j * c, (j + 1) * c)
            qd = q_dec[sl]
            qs = jnp.concatenate([jnp.where(klane_head == h, qd, 0.0) for h in range(nh)],
                                 axis=0).astype(BF16)
            att = jnp.where(causal4, _dot_nt(qs, k_inv[sl]), 0.0)
            r = jnp.dot(att.astype(BF16), vb[sl], preferred_element_type=F32)
            o = jnp.where(vlane_head == 0, r[0:c], 0.0)
            for h in range(1, nh):
                o = o + jnp.where(vlane_head == h, r[h * c:(h + 1) * c], 0.0)
            outs.append(o + _dot_nt(qd.astype(BF16), st.astype(BF16)))
            d_st = _dot_tn(vb[sl], k_end[sl])
            st = st * jnp.exp(lasts[j]) + jnp.where(st_mask, d_st, 0.0)
        st_ref[bb] = st
        o = jnp.concatenate(outs, axis=0)
        ms = _dot_split_lhs(o * o, gmean, 2)
        o = o * lax.rsqrt(ms + RMS_EPS) * ng_ref[...]
        o_ref[bb, rows, :] = (o * (g * _sigmoid(g))).astype(o_ref.dtype)

    return one_seq


def _ssd_setup(p_ref, dt_ref, cw_ref, cb_ref, alog_ref, dtb_ref, dsk_ref, ng_ref, o_ref, st_ref,
               halo_ref):
    c = SSD_CHUNK
    n_st = SSD_STATE
    halo = 2 * SUBLANES

    ri = lax.broadcasted_iota(jnp.int32, (c, c), 0)
    ci = lax.broadcasted_iota(jnp.int32, (c, c), 1)
    causal = ci <= ri
    tri = causal.astype(BF16)
    upper = (ri <= ci).astype(BF16)
    lane_head = lax.broadcasted_iota(jnp.int32, (1, W_MIX), 1) // SSD_HEADDIM
    lane_group = lane_head // (SSD_HEADS // SSD_GROUPS)
    eh = lax.broadcasted_iota(jnp.int32, (LANES, W_MIX), 0)
    el = lax.broadcasted_iota(jnp.int32, (LANES, W_MIX), 1) // SSD_HEADDIM
    expand = (eh == el).astype(BF16)
    row8 = lax.broadcasted_iota(jnp.int32, (8, 3 * W_MIX), 0)
    a_c = -jnp.exp(alog_ref[...])

    def one_chunk(n, bb):
        rows = slice(n * c, (n + 1) * c)
        cur = p_ref[bb, rows, 256:1024].astype(F32)
        before = halo_ref[bb] if n == 0 else p_ref[bb, n * c - halo:n * c, 256:1024]
        prev8 = before.astype(F32)[SUBLANES:]
        acc = cur * cw_ref[3:4, :] + cb_ref[...]
        for s in (1, 2, 3):
            sh = pltpu.roll(cur, s, axis=0)
            top = jnp.where(row8 < s, pltpu.roll(prev8, s, axis=0), sh[0:8])
            sh = jnp.concatenate([top, sh[8:]], axis=0)
            acc = acc + sh * cw_ref[3 - s:4 - s, :]
        xbc = acc * _sigmoid(acc)
        x = xbc[:, 0:256]
        bm = xbc[:, 256:512].astype(BF16)
        cm = xbc[:, 512:768].astype(BF16)

        dt_c = _softplus(dt_ref[bb, rows, :] + dtb_ref[...])
        da_c = dt_c * a_c
        cum_c = _dot_split_rhs(tri, da_c, 3)
        cum_r = _dot_split_lhs(da_c, upper, 3, dot=_dot_tn)
        both_x = _dot_split_lhs(jnp.concatenate([dt_c, cum_c], axis=0), expand, 3)
        dt_x = both_x[0:c]
        cum_x = both_x[c:2 * c]
        cl_x = cum_x[c - 1:c, :]
        x_dt = x * dt_x
        x_dt_b = x_dt.astype(BF16)
        xw_b = (x_dt * jnp.exp(cl_x - cum_x)).astype(BF16)

        y = x * dsk_ref[...]
        y_off = jnp.zeros((c, W_MIX), F32)
        for g in range(SSD_GROUPS):
            bg = bm[:, g * n_st:(g + 1) * n_st]
            cg = cm[:, g * n_st:(g + 1) * n_st]
            cb = _dot_nt(cg, bg)
            for r in range(SSD_HEADS // SSD_GROUPS):
                h = g * (SSD_HEADS // SSD_GROUPS) + r
                diff = cum_c[:, h:h + 1] - cum_r[h:h + 1, :]
                dec = jnp.exp(jnp.where(causal, diff, -jnp.inf))
                m = (cb * dec).astype(BF16)
                yh = jnp.dot(m, x_dt_b, preferred_element_type=F32)
                y = y + jnp.where(lane_head == h, yh, 0.0)
            st = st_ref[bb, g]
            y_off = y_off + jnp.where(lane_group == g,
                                      jnp.dot(cg, st.astype(BF16), preferred_element_type=F32), 0.0)
            d_st = _dot_tn(bg, xw_b)
            st_ref[bb, g] = st * jnp.exp(cl_x) + jnp.where(lane_group == g, d_st, 0.0)
        y = y + y_off * jnp.exp(cum_x)
        zg = p_ref[bb, rows, 0:256].astype(F32)
        y = y * (zg * _sigmoid(zg))
        outs = []
        for g in range(SSD_GROUPS):
            yg = y[:, g * 128:(g + 1) * 128]
            ms = jnp.mean(yg * yg, axis=-1, keepdims=True)
            outs.append(yg * lax.rsqrt(ms + RMS_EPS))
        o_ref[bb, rows, :] = (jnp.concatenate(outs, axis=-1) * ng_ref[...]).astype(o_ref.dtype)

    return one_chunk


def _recurrent_kernel(pg_ref, lr_ref, wlr_ref, blr_ref, gng_ref,
                      ps_ref, dt_ref, cw_ref, cb_ref, alog_ref, dtb_ref, dsk_ref, sng_ref,
                      og_ref, os_ref, gst_ref, sst_ref, halo_ref):
    @pl.when(pl.program_id(1) == 0)
    def _():
        gst_ref[...] = jnp.zeros_like(gst_ref)
        sst_ref[...] = jnp.zeros_like(sst_ref)
        halo_ref[...] = jnp.zeros_like(halo_ref)

    gla_rows = _gla_setup(pg_ref, lr_ref, wlr_ref, blr_ref, gng_ref, og_ref, gst_ref)
    ssd_chunk = _ssd_setup(ps_ref, dt_ref, cw_ref, cb_ref, alog_ref, dtb_ref, dsk_ref, sng_ref,
                           os_ref, sst_ref, halo_ref)
    for bb in range(pg_ref.shape[0]):
        gla_rows(bb, slice(0, GLA_ROWS))
        for n in range(GLA_ROWS // SSD_CHUNK):
            ssd_chunk(n, bb)
        halo_ref[bb] = ps_ref[bb, GLA_ROWS - halo_ref.shape[1]:GLA_ROWS, 256:1024]


def _recurrent_mixers(pg, plr, w_lr_pad, b_lr, gla_norm_g4,
                      ps, pdt, conv_w, conv_b, a_log_c, dt_bias_c, d_x, ssd_norm_g):
    b, s, _ = pg.shape
    nb = REC_SEQS
    rb = GLA_ROWS
    seq = lambda i, j: (i, j, 0)
    full2 = lambda i, j: (0, 0)
    return pl.pallas_call(
        _recurrent_kernel,
        grid=(b // nb, s // rb),
        in_specs=[pl.BlockSpec((nb, rb, pg.shape[2]), seq),
                  pl.BlockSpec((nb, rb, LANES), seq),
                  pl.BlockSpec((LANES, LANES), full2),
                  pl.BlockSpec((1, LANES), full2),
                  pl.BlockSpec((1, W_MIX), full2),
                  pl.BlockSpec((nb, rb, ps.shape[2]), seq),
                  pl.BlockSpec((nb, rb, LANES), seq),
                  pl.BlockSpec((SSD_CONV_K, 3 * W_MIX), full2),
                  pl.BlockSpec((1, 3 * W_MIX), full2),
                  pl.BlockSpec((1, LANES), full2),
                  pl.BlockSpec((1, LANES), full2),
                  pl.BlockSpec((1, W_MIX), full2),
                  pl.BlockSpec((1, W_MIX), full2)],
        out_specs=[pl.BlockSpec((nb, rb, W_MIX), seq), pl.BlockSpec((nb, rb, W_MIX), seq)],
        out_shape=[jax.ShapeDtypeStruct((b, s, W_MIX), BF16),
                   jax.ShapeDtypeStruct((b, s, W_MIX), BF16)],
        scratch_shapes=[pltpu.VMEM((nb, GLA_HEADS * GLA_DV, GLA_HEADS * GLA_DK), F32),
                        pltpu.VMEM((nb, SSD_GROUPS, SSD_STATE, W_MIX), F32),
                        pltpu.VMEM((nb, 2 * SUBLANES, 3 * W_MIX), BF16)],
        compiler_params=_cparams(("parallel", "arbitrary")),
        name="gla_ssd_mixers",
    )(pg, plr, w_lr_pad, b_lr, gla_norm_g4, ps, pdt, conv_w, conv_b, a_log_c, dt_bias_c, d_x,
      ssd_norm_g)


DIFF_TQ = 256
DIFF_TK = 256
LOG2E = 1.4426950408889634
DIFF_VPAD = DIFF_DV + 16
DIFF_SEQS = 4


def _diff_kernel(q_ref, k_ref, v_ref, lam_ref, ng_ref, o_ref,
                 vt_ref, qs_ref, st_ref, m_ref, acc_ref, *, lam_init):
    tq, tk = DIFF_TQ, DIFF_TK
    nh, dv = DIFF_HEADS, DIFF_DV
    nhc = 2 * nh
    s_len = k_ref.shape[1]
    i = pl.program_id(1)
    seqs = range(q_ref.shape[0])

    @pl.when(i == 0)
    def _():
        for bb in seqs:
            for cblk in range(s_len // tk):
                cols = slice(cblk * tk, (cblk + 1) * tk)
                vt = v_ref[bb, cols, :].astype(F32).T.astype(BF16)
                for h in range(nh):
                    vt_ref[bb, h, 0:dv, cols] = vt[h * dv:(h + 1) * dv]
        vt_ref[:, :, dv:, :] = jnp.ones((len(seqs), nh, DIFF_VPAD - dv, s_len), BF16)

    qlane = lax.broadcasted_iota(jnp.int32, (1, W_MIX), 1) // DIFF_DQK
    for bb in seqs:
        q = q_ref[bb].astype(F32) * (DIFF_DQK ** -0.5 * LOG2E)
        for hc in range(nhc):
            qs_ref[bb, hc * tq:(hc + 1) * tq, :] = jnp.where(qlane == hc, q, 0.0).astype(BF16)
    m_ref[...] = jnp.full_like(m_ref, -jnp.inf)
    acc_ref[...] = jnp.zeros_like(acc_ref)
    krow = lax.broadcasted_iota(jnp.int32, (tk, nhc * tq), 0)
    qcol = lax.broadcasted_iota(jnp.int32, (tk, nhc * tq), 1) % tq
    diag_ok = krow <= qcol

    def scores(j, slot):
        k0 = pl.multiple_of(j * tk, tk)
        for bb in seqs:
            st_ref[bb, slot] = _dot_nt(k_ref[bb, pl.ds(k0, tk), :], qs_ref[bb])

    def softmax_pv(j, slot, masked):
        for bb in seqs:
            softmax_pv_seq(bb, j, slot, masked)

    def softmax_pv_seq(bb, j, slot, masked):
        k0 = pl.multiple_of(j * tk, tk)
        st = st_ref[bb, slot]
        if masked:
            st = jnp.where(diag_ok, st, -jnp.inf)
        m_prev = m_ref[bb]
        m_new = jnp.maximum(m_prev, jnp.max(st, axis=0, keepdims=True))
        alpha = jnp.exp2(m_prev - m_new)
        p = jnp.exp2(st - m_new)
        m_ref[bb] = m_new
        pb = p.astype(BF16)
        for hc in range(nhc):
            h = hc // 2
            lanes = slice(hc * tq, (hc + 1) * tq)
            pv = jnp.dot(vt_ref[bb, h, :, pl.ds(k0, tk)], pb[:, lanes],
                         preferred_element_type=F32)
            acc_ref[bb, hc] = acc_ref[bb, hc] * alpha[:, lanes] + pv

    scores(0, 0)
    n_pairs = i // 2

    def pair_step(u, carry):
        scores(2 * u + 1, 1)
        softmax_pv(2 * u, 0, False)
        scores(2 * u + 2, 0)
        softmax_pv(2 * u + 1, 1, False)
        return carry

    lax.fori_loop(0, n_pairs, pair_step, 0)

    @pl.when(i % 2 == 0)
    def _():
        softmax_pv(i, 0, True)

    @pl.when(i % 2 == 1)
    def _():
        scores(i, 1)
        softmax_pv(i - 1, 0, False)
        softmax_pv(i, 1, True)

    lam = (jnp.exp(jnp.sum(lam_ref[0:1, :] * lam_ref[1:2, :], axis=-1, keepdims=True))
           - jnp.exp(jnp.sum(lam_ref[2:3, :] * lam_ref[3:4, :], axis=-1, keepdims=True))
           + lam_init)
    for bb in seqs:
        heads = []
        for h in range(nh):
            o1 = acc_ref[bb, 2 * h, 0:dv] / acc_ref[bb, 2 * h, dv:dv + 1]
            o2 = acc_ref[bb, 2 * h + 1, 0:dv] / acc_ref[bb, 2 * h + 1, dv:dv + 1]
            oh = o1 - lam * o2
            ms = jnp.mean(oh * oh, axis=0, keepdims=True)
            heads.append(oh * lax.rsqrt(ms + RMS_EPS))
        o = jnp.concatenate(heads, axis=0).T
        o_ref[bb] = (o * ng_ref[...] * (1.0 - lam_init)).astype(o_ref.dtype)


def _diff_mixer(pd, lam_vecs, norm_g4, lam_init):
    b, s, _ = pd.shape
    tq = DIFF_TQ
    nb = DIFF_SEQS
    return pl.pallas_call(
        functools.partial(_diff_kernel, lam_init=lam_init),
        grid=(b // nb, s // tq),
        in_specs=[pl.BlockSpec((nb, tq, W_MIX), lambda bi, i: (bi, i, 0)),
                  pl.BlockSpec((nb, s, W_MIX), lambda bi, i: (bi, 0, 1)),
                  pl.BlockSpec((nb, s, W_MIX), lambda bi, i: (bi, 0, 2)),
                  pl.BlockSpec((4, LANES), lambda bi, i: (0, 0)),
                  pl.BlockSpec((1, W_MIX), lambda bi, i: (0, 0))],
        out_specs=pl.BlockSpec((nb, tq, W_MIX), lambda bi, i: (bi, i, 0)),
        out_shape=jax.ShapeDtypeStruct((b, s, W_MIX), BF16),
        scratch_shapes=[pltpu.VMEM((nb, DIFF_HEADS, DIFF_VPAD, s), BF16),
                        pltpu.VMEM((nb, 2 * DIFF_HEADS * tq, W_MIX), BF16),
                        pltpu.VMEM((nb, 2, DIFF_TK, 2 * DIFF_HEADS * tq), F32),
                        pltpu.VMEM((nb, 1, 2 * DIFF_HEADS * tq), F32),
                        pltpu.VMEM((nb, 2 * DIFF_HEADS, DIFF_VPAD, tq), F32)],
        compiler_params=_cparams(("parallel", "arbitrary")),
        name="diff_attn",
    )(pd, pd, pd, lam_vecs, norm_g4)


def _oproj_kernel(yc_ref, yg_ref, ys_ref, yd_ref, x_ref, wo_ref, g_ref, b_ref, wr_ref,
                  xo_ref, xp_ref, route_ref, cnt_ref):
    mix = jnp.concatenate([yc_ref[...], yg_ref[...], ys_ref[...], yd_ref[...]], axis=-1)
    h = ALPHA * x_ref[...] + jnp.dot(mix, wo_ref[...], preferred_element_type=F32)
    xn = _layer_norm(h, g_ref[...], b_ref[...])
    xo_ref[...] = xn
    xp_ref[...] = _pack_halves(xn)

    xn_hi, xn_lo = _split_bf16(xn, 2)
    both = _dot(xn_hi, wr_ref[...])
    logits = both[:, 0:LANES] + both[:, LANES:2 * LANES] + _dot(xn_lo, wr_ref[:, 0:LANES])
    lane = lax.broadcasted_iota(jnp.int32, logits.shape, 1).astype(F32)
    neg = -jnp.inf
    big = float(LANES)
    lg = jnp.where(lane < N_GROUPS, logits, neg)
    mg = jnp.max(lg, axis=-1, keepdims=True)
    sg = jnp.sum(jnp.exp(lg - mg), axis=-1, keepdims=True)
    grp = jnp.min(jnp.where(lg == mg, lane, big), axis=-1, keepdims=True)
    p_grp = 1.0 / sg
    lo = N_GROUPS + EXPERTS_PER_GROUP * grp
    in_g = jnp.logical_and(lane >= lo, lane < lo + EXPERTS_PER_GROUP)
    le = jnp.where(in_g, logits, neg)
    me = jnp.max(le, axis=-1, keepdims=True)
    ee = jnp.exp(le - me)
    pe = ee / jnp.sum(ee, axis=-1, keepdims=True)
    pe = jnp.where(in_g, pe, -1.0)
    p1 = jnp.max(pe, axis=-1, keepdims=True)
    i1 = jnp.min(jnp.where(pe == p1, lane, big), axis=-1, keepdims=True)
    pe2 = jnp.where(lane == i1, -1.0, pe)
    p2 = jnp.max(pe2, axis=-1, keepdims=True)
    i2 = jnp.min(jnp.where(pe2 == p2, lane, big), axis=-1, keepdims=True)
    den = p1 + p2
    g1 = p_grp * p1 / den
    g2 = p_grp * p2 / den
    e1 = i1 - N_GROUPS
    e2 = i2 - N_GROUPS
    route_ref[...] = jnp.where(lane == 0, e1, jnp.where(lane == 1, e2, jnp.where(
        lane == 2, g1, jnp.where(lane == 3, g2, 0.0))))

    @pl.when(pl.program_id(0) == 0)
    def _():
        cnt_ref[...] = jnp.zeros_like(cnt_ref)

    hits = jnp.where(lane == e1, 1.0, 0.0) + jnp.where(lane == e2, 1.0, 0.0)
    cnt_ref[...] += jnp.sum(hits, axis=0, keepdims=True)


def _out_proj(ys, x2d, w_o, layer, ln_g, ln_b, w_route):
    t = x2d.shape[0]
    tm = 1024
    row = lambda i: (i, 0)
    full = lambda i: (0, 0)
    return pl.pallas_call(
        _oproj_kernel,
        grid=(t // tm,),
        in_specs=[pl.BlockSpec((tm, W_MIX), row)] * 4 + [
            pl.BlockSpec((tm, D_MODEL), row),
            pl.BlockSpec((None, D_MODEL, D_MODEL), lambda i: (layer, 0, 0)),
            pl.BlockSpec((1, D_MODEL), full),
            pl.BlockSpec((1, D_MODEL), full),
            pl.BlockSpec((D_MODEL, 2 * LANES), full)],
        out_specs=[pl.BlockSpec((tm, D_MODEL), row), pl.BlockSpec((tm, D_MODEL // 2), row),
                   pl.BlockSpec((tm, LANES), row), pl.BlockSpec((1, LANES), full)],
        out_shape=[jax.ShapeDtypeStruct((t, D_MODEL), F32),
                   jax.ShapeDtypeStruct((t, D_MODEL // 2), U32),
                   jax.ShapeDtypeStruct((t, LANES), F32),
                   jax.ShapeDtypeStruct((1, LANES), F32)],
        compiler_params=_cparams(("arbitrary",)),
        name="out_proj_ln_router",
    )(*ys, x2d, w_o, ln_g, ln_b, w_route)


PLAN_TILE = 512


def _plan_kernel(route_ref, cnt_ref, dest_ref, meta_ref, carry_ref, pstart_ref):
    tm = route_ref.shape[0]
    lane = lax.broadcasted_iota(jnp.int32, (1, LANES), 1).astype(F32)

    @pl.when(pl.program_id(0) == 0)
    def _():
        cnt = cnt_ref[...]
        padded = jnp.ceil(cnt * (1.0 / MOE_BLK)) * MOE_BLK
        li = lax.broadcasted_iota(jnp.int32, (LANES, LANES), 0)
        lj = lax.broadcasted_iota(jnp.int32, (LANES, LANES), 1)
        before = (li < lj).astype(F32)
        pstart = jnp.dot(jnp.broadcast_to(padded, (8, LANES)), before, precision=HI,
                         preferred_element_type=F32)[0:1]
        pstart_ref[...] = pstart
        carry_ref[...] = jnp.zeros_like(carry_ref)
        meta_ref[...] = jnp.concatenate(
            [pstart + padded, pstart, cnt, jnp.zeros((5, LANES), F32)], axis=0)

    oh0 = jnp.where(lane == route_ref[:, 0:1], 1.0, 0.0)
    oh1 = jnp.where(lane == route_ref[:, 1:2], 1.0, 0.0)
    both = oh0 + oh1
    ri = lax.broadcasted_iota(jnp.int32, (tm, tm), 0)
    ci = lax.broadcasted_iota(jnp.int32, (tm, tm), 1)
    earlier = (ci < ri).astype(BF16)
    base = (jnp.dot(earlier, both.astype(BF16), preferred_element_type=F32)
            + carry_ref[...] + pstart_ref[...])
    d0 = jnp.sum(oh0 * base, axis=-1, keepdims=True)
    d1 = jnp.sum(oh1 * base, axis=-1, keepdims=True)
    dest = jnp.where(lane == 0, d0, jnp.where(lane == 1, d1, 0.0))
    dest_ref[...] = dest.T[0:2, :].astype(jnp.int32)
    carry_ref[...] += jnp.sum(both, axis=0, keepdims=True)


def _dispatch_plan(route, cnt, n_blocks):
    t = route.shape[0]
    tm = PLAN_TILE
    blk = MOE_BLK
    dest, meta = pl.pallas_call(
        _plan_kernel,
        grid=(t // tm,),
        in_specs=[pl.BlockSpec((tm, LANES), lambda i: (i, 0)),
                  pl.BlockSpec((1, LANES), lambda i: (0, 0))],
        out_specs=[pl.BlockSpec((2, tm), lambda i: (0, i)),
                   pl.BlockSpec((8, LANES), lambda i: (0, 0))],
        out_shape=[jax.ShapeDtypeStruct((2, t), jnp.int32),
                   jax.ShapeDtypeStruct((8, LANES), F32)],
        scratch_shapes=[pltpu.VMEM((1, LANES), F32), pltpu.VMEM((1, LANES), F32)],
        compiler_params=_cparams(("arbitrary",)),
        name="moe_plan",
    )(route, cnt)
    meta_i = meta[:, :N_EXPERTS].astype(jnp.int32)
    pad_end, pad_start, seg_end = meta_i[0], meta_i[1], meta_i[1] + meta_i[2]
    starts = jnp.arange(n_blocks, dtype=jnp.int32)[:, None] * blk
    member = jnp.logical_and(starts >= pad_start[None, :], starts < pad_end[None, :])
    expert_ids = jnp.arange(N_EXPERTS, dtype=jnp.int32)[None, :]
    block_e = jnp.where(jnp.any(member, axis=1), jnp.sum(jnp.where(member, expert_ids, 0), axis=1),
                        N_EXPERTS - 1)
    n_used = (pad_end[N_EXPERTS - 1] // blk).reshape(1)
    n_valid = jnp.clip(jnp.sum(jnp.where(member, seg_end[None, :], 0), axis=1) - starts[:, 0],
                       0, blk)
    return dest.reshape(2 * t), block_e, n_used, n_valid


def _ffn_kernel(be_ref, nu_ref, nv_ref, xs_ref, wg_ref, wu_ref, wd_ref, y_ref,
                wgb_ref, wub_ref, wdb_ref):
    i = pl.program_id(0)
    prev = be_ref[jnp.maximum(i - 1, 0)]

    @pl.when(jnp.logical_or(i == 0, be_ref[i] != prev))
    def _():
        wgb_ref[...] = wg_ref[...].astype(BF16)
        wub_ref[...] = wu_ref[...].astype(BF16)
        wdb_ref[...] = wd_ref[...].astype(BF16)

    @pl.when(i < nu_ref[0])
    def _():
        half = xs_ref.shape[0] // 2
        for r in range(2):
            rows = slice(r * half, (r + 1) * half)
            row = lax.broadcasted_iota(jnp.int32, (half, 1), 0) + r * half
            xp = jnp.where(row < nv_ref[i], xs_ref[rows, :], U32(0))
            x_hi, x_lo = _unpack_halves(xp)
            xb = jnp.concatenate([x_hi.astype(BF16), x_lo.astype(BF16)], axis=1)
            a = jnp.dot(xb, wgb_ref[...], preferred_element_type=F32)
            u = jnp.dot(xb, wub_ref[...], preferred_element_type=F32)
            h = (a * _sigmoid(a) * u).astype(BF16)
            y_ref[rows, :] = _pack_halves(jnp.dot(h, wdb_ref[...], preferred_element_type=F32))

    @pl.when(i >= nu_ref[0])
    def _():
        y_ref[...] = jnp.zeros_like(y_ref)


def _expert_ffn(xs, block_e, n_used, n_valid, w_gate, w_up, w_down, layer):
    n_slots = xs.shape[0]
    blk = MOE_BLK
    w_map = lambda i, be, nu, nv: (layer, be[i], 0, 0)
    grid_spec = pltpu.PrefetchScalarGridSpec(
        num_scalar_prefetch=3,
        grid=(n_slots // blk,),
        in_specs=[pl.BlockSpec((blk, D_MODEL // 2),
                               lambda i, be, nu, nv: (jnp.minimum(i, nu[0] - 1), 0)),
                  pl.BlockSpec((None, None, D_MODEL, D_EXPERT), w_map),
                  pl.BlockSpec((None, None, D_MODEL, D_EXPERT), w_map),
                  pl.BlockSpec((None, None, D_EXPERT, D_MODEL), w_map)],
        out_specs=pl.BlockSpec((blk, D_MODEL // 2), lambda i, be, nu, nv: (i, 0)),
        scratch_shapes=[pltpu.VMEM((D_MODEL, D_EXPERT), BF16),
                        pltpu.VMEM((D_MODEL, D_EXPERT), BF16),
                        pltpu.VMEM((D_EXPERT, D_MODEL), BF16)],
    )
    return pl.pallas_call(
        _ffn_kernel,
        grid_spec=grid_spec,
        out_shape=jax.ShapeDtypeStruct((n_slots, D_MODEL // 2), U32),
        compiler_params=_cparams(("arbitrary",)),
        name="expert_ffn",
    )(block_e, n_used, n_valid, xs, w_gate, w_up, w_down)


SC_CORES = 2
SC_SUBCORES = 16
SC_ROWS = 64


def _sc_gather_rows(table, idx):
    b = idx.shape[0]
    d = table.shape[1]
    per_w = b // (SC_CORES * SC_SUBCORES)
    mesh = plsc.VectorSubcoreMesh(core_axis_name="c", subcore_axis_name="s")

    n_chunks = per_w // SC_ROWS

    @functools.partial(
        pl.kernel, mesh=mesh,
        out_type=jax.ShapeDtypeStruct((b, d), table.dtype),
        scratch_types=[pltpu.VMEM((SC_ROWS,), jnp.int32), pltpu.VMEM((SC_ROWS,), jnp.int32),
                       pltpu.VMEM((SC_ROWS, d), table.dtype),
                       pltpu.VMEM((SC_ROWS, d), table.dtype),
                       pltpu.SemaphoreType.DMA, pltpu.SemaphoreType.DMA,
                       pltpu.SemaphoreType.DMA, pltpu.SemaphoreType.DMA],
        name="sc_gather_rows",
    )
    def gather(table_hbm, idx_hbm, out_hbm, idx0, idx1, rows0, rows1, gs0, gs1, ws0, ws1):
        idx_v, rows_v, gsem, wsem = (idx0, idx1), (rows0, rows1), (gs0, gs1), (ws0, ws1)
        wid = lax.axis_index("s") * SC_CORES + lax.axis_index("c")
        base = wid * per_w

        def rows_of(c):
            return pl.ds(pl.multiple_of(base + c * SC_ROWS, SC_ROWS), SC_ROWS)

        def start_gather(c, s):
            pltpu.sync_copy(idx_hbm.at[rows_of(c)], idx_v[s])
            pltpu.async_copy(table_hbm.at[idx_v[s]], rows_v[s], gsem[s])

        def write_back(c, s):
            pltpu.make_async_copy(table_hbm.at[idx_v[s]], rows_v[s], gsem[s]).wait()
            pltpu.async_copy(rows_v[s], out_hbm.at[rows_of(c)], wsem[s]).wait()

        start_gather(0, 0)

        @pl.loop(0, n_chunks, step=2)
        def _(c):
            start_gather(c + 1, 1)
            write_back(c, 0)

            @pl.when(c + 2 < n_chunks)
            def _():
                start_gather(c + 2, 0)

            write_back(c + 1, 1)

    return gather(table, idx)


def _sc_scatter_rows(x2d, idx, n_slots):
    t, d = x2d.shape
    per_w = t // (SC_CORES * SC_SUBCORES)
    mesh = plsc.VectorSubcoreMesh(core_axis_name="c", subcore_axis_name="s")

    @functools.partial(
        pl.kernel, mesh=mesh,
        out_type=jax.ShapeDtypeStruct((n_slots, d), x2d.dtype),
        scratch_types=[pltpu.VMEM((SC_ROWS,), jnp.int32), pltpu.VMEM((SC_ROWS,), jnp.int32),
                       pltpu.VMEM((SC_ROWS, d), x2d.dtype),
                       pltpu.SemaphoreType.DMA, pltpu.SemaphoreType.DMA],
        name="sc_scatter_rows",
    )
    def scatter(x_hbm, idx_hbm, out_hbm, idx0, idx1, rows_v, s0, s1):
        wid = lax.axis_index("s") * SC_CORES + lax.axis_index("c")
        base = wid * per_w

        @pl.loop(0, per_w // SC_ROWS)
        def _(c):
            off = pl.multiple_of(base + c * SC_ROWS, SC_ROWS)
            pltpu.sync_copy(x_hbm.at[pl.ds(off, SC_ROWS)], rows_v)
            pltpu.sync_copy(idx_hbm.at[pl.ds(off, SC_ROWS)], idx0)
            pltpu.sync_copy(idx_hbm.at[pl.ds(t + off, SC_ROWS)], idx1)
            cp0 = pltpu.async_copy(rows_v, out_hbm.at[idx0], s0)
            cp1 = pltpu.async_copy(rows_v, out_hbm.at[idx1], s1)
            cp0.wait()
            cp1.wait()

    return scatter(x2d, idx)


def _combine_dense_kernel(route_ref, x_ref, y0_ref, y1_ref, g_ref, b_ref, o_ref):
    y0 = jnp.concatenate(_unpack_halves(y0_ref[...]), axis=1)
    y1 = jnp.concatenate(_unpack_halves(y1_ref[...]), axis=1)
    moe = route_ref[:, 2:3] * y0 + route_ref[:, 3:4] * y1
    h = ALPHA * x_ref[...] + moe
    o_ref[...] = _layer_norm(h, g_ref[...], b_ref[...])


def _combine_dense(route, x2d, yg, ln_g, ln_b):
    t = x2d.shape[0]
    tm = 1024
    nt = t // tm
    row = lambda i: (i, 0)
    full = lambda i: (0, 0)
    return pl.pallas_call(
        _combine_dense_kernel,
        grid=(nt,),
        in_specs=[pl.BlockSpec((tm, LANES), row),
                  pl.BlockSpec((tm, D_MODEL), row),
                  pl.BlockSpec((tm, D_MODEL // 2), row),
                  pl.BlockSpec((tm, D_MODEL // 2), lambda i: (i + nt, 0)),
                  pl.BlockSpec((1, D_MODEL), full),
                  pl.BlockSpec((1, D_MODEL), full)],
        out_specs=pl.BlockSpec((tm, D_MODEL), row),
        out_shape=jax.ShapeDtypeStruct((t, D_MODEL), F32),
        compiler_params=_cparams(("parallel",)),
        name="moe_combine_dense",
    )(route, x2d, yg, yg, ln_g, ln_b)


def _pad_cols(w, n):
    return jnp.pad(w, [(0, 0)] * (w.ndim - 1) + [(0, n - w.shape[-1])])


def kernel(x, w_in, conv_w, gla_w_lr, gla_b_lr, gla_norm_g, ssd_conv_w, ssd_conv_b, ssd_a_log,
           ssd_d, ssd_dt_bias, ssd_norm_g, diff_lq1, diff_lk1, diff_lq2, diff_lk2, diff_norm_g,
           w_o, ln1_g, ln1_b, router_g, router_e, w_gate, w_up, w_down, ln2_g, ln2_b):
    bsz, seq, d = x.shape
    t = bsz * seq
    n_assign = 2 * t
    n_blocks = (n_assign + N_EXPERTS * (MOE_BLK - 1)) // MOE_BLK + 1
    n_slots = n_blocks * MOE_BLK
    x2d = x.reshape(t, d)
    w_o_b = w_o.astype(BF16)
    for l in range(DEPTH):
        pc, pg, plr, ps, pdt, pd = [p.reshape(bsz, seq, -1) for p in _in_proj(x2d, w_in, l)]

        y_conv = _conv_mixer(pc, conv_w[l])
        w_lr_pad = jnp.pad(gla_w_lr[l], ((0, LANES - GLA_RANK), (0, 0)))
        pad4 = lambda v: jnp.pad(v, (0, LANES - SSD_HEADS)).reshape(1, LANES)
        y_gla, y_ssd = _recurrent_mixers(
            pg, plr, w_lr_pad, gla_b_lr[l].reshape(1, -1),
            jnp.tile(gla_norm_g[l], GLA_HEADS).reshape(1, -1),
            ps, pdt, ssd_conv_w[l], ssd_conv_b[l].reshape(1, -1),
            pad4(ssd_a_log[l]), pad4(ssd_dt_bias[l]),
            jnp.repeat(ssd_d[l], SSD_HEADDIM).reshape(1, -1), ssd_norm_g[l].reshape(1, -1))
        lam_vecs = jnp.pad(jnp.stack([diff_lq1[l], diff_lk1[l], diff_lq2[l], diff_lk2[l]]),
                           ((0, 0), (0, LANES - DIFF_DQK)))
        lam_init = 0.8 - 0.6 * math.exp(-0.3 * l)
        y_diff = _diff_mixer(pd, lam_vecs,
                             jnp.tile(diff_norm_g[l], DIFF_HEADS).reshape(1, -1), lam_init)

        w_route = _pad_cols(jnp.concatenate(
            [router_g[l], router_e[l].reshape(d, N_EXPERTS)], axis=1), LANES)
        w_route_hi = w_route.astype(BF16)
        w_route = jnp.concatenate(
            [w_route_hi, (w_route - w_route_hi.astype(F32)).astype(BF16)], axis=1)
        ys = [y.reshape(t, W_MIX) for y in (y_conv, y_gla, y_ssd, y_diff)]
        xn, xn_p, route, cnt = _out_proj(ys, x2d, w_o_b, l, ln1_g[l].reshape(1, -1),
                                         ln1_b[l].reshape(1, -1), w_route)

        idx, block_e, n_used, n_valid = _dispatch_plan(route, cnt, n_blocks)
        xs = _sc_scatter_rows(xn_p, idx, n_slots)
        y = _expert_ffn(xs, block_e, n_used, n_valid, w_gate, w_up, w_down, l)
        yg = _sc_gather_rows(y, idx)
        x2d = _combine_dense(route, xn, yg, ln2_g[l].reshape(1, -1), ln2_b[l].reshape(1, -1))
    return x2d.reshape(bsz, seq, d)
```

```python
import functools
import math

import jax
import jax.numpy as jnp
from jax import lax
from jax.experimental import pallas as pl
from jax.experimental.pallas import tpu as pltpu
from jax.experimental.pallas import tpu_sc as plsc

F32 = jnp.float32
BF16 = jnp.bfloat16
HI = lax.Precision.HIGHEST

D_MODEL = 1024
DEPTH = 2
W_MIX = 256
GLA_HEADS, GLA_DK, GLA_DV, GLA_RANK, GLA_TAU, GLA_CHUNK = 4, 32, 64, 16, 16.0, 64
GLA_ROWS = 256
REC_SEQS = 4
SSD_HEADS, SSD_GROUPS, SSD_HEADDIM, SSD_STATE, SSD_CONV_K, SSD_CHUNK = 4, 2, 64, 128, 4, 128
DIFF_HEADS, DIFF_DQK, DIFF_DV = 4, 32, 64
N_GROUPS, EXPERTS_PER_GROUP, N_EXPERTS, D_EXPERT = 4, 8, 32, 512
ALPHA = (2 * DEPTH) ** 0.25
LN_EPS = 1e-5
RMS_EPS = 1e-6

LANES = 128
SUBLANES = 8
PROJ_WIDTHS = (768, 768, 128, 1024, 128, 768)
PROJ_SRC_OFFSETS = (0, 768, 1536, 1552, 2576, 2580)
PROJ_SRC_WIDTHS = (768, 768, GLA_RANK, 1024, SSD_HEADS, 768)
PROJ_DTYPES = (BF16, BF16, F32, BF16, F32, BF16)
VMEM_LIMIT = 56 * 1024 * 1024

MOE_BLK = 512


def _cparams(sem):
    return pltpu.CompilerParams(dimension_semantics=sem, vmem_limit_bytes=VMEM_LIMIT)


def _sigmoid(x):
    return 1.0 / (1.0 + jnp.exp(-x))


def _softplus(x):
    return jnp.maximum(x, 0.0) + jnp.log(1.0 + jnp.exp(-jnp.abs(x)))


def _layer_norm(h, g, b):
    mu = jnp.mean(h, axis=-1, keepdims=True)
    d = h - mu
    var = jnp.mean(d * d, axis=-1, keepdims=True)
    return d * lax.rsqrt(var + LN_EPS) * g + b


def _dot_nt(a, b):
    return lax.dot_general(a, b, (((1,), (1,)), ((), ())), preferred_element_type=F32)


def _dot_tn(a, b, precision=None):
    return lax.dot_general(a, b, (((0,), (0,)), ((), ())), preferred_element_type=F32,
                           precision=precision)


def _split_bf16(x, parts):
    out = []
    for _ in range(parts - 1):
        hi = x.astype(BF16)
        out.append(hi)
        x = x - hi.astype(F32)
    out.append(x.astype(BF16))
    return out


def _dot(a, b):
    return jnp.dot(a, b, preferred_element_type=F32)


U32 = jnp.uint32


def _pack_halves(x):
    w = x.shape[1] // 2
    hi = lax.bitcast_convert_type(x[:, :w].astype(BF16).astype(F32), U32)
    lo = lax.bitcast_convert_type(x[:, w:].astype(BF16).astype(F32), U32)
    return hi | lax.shift_right_logical(lo, U32(16))


def _unpack_halves(p):
    hi = lax.bitcast_convert_type(p & U32(0xFFFF0000), F32)
    lo = lax.bitcast_convert_type(lax.shift_left(p, U32(16)), F32)
    return hi, lo


def _dot_split_lhs(a, b_exact, parts, dot=_dot):
    acc = None
    for term in _split_bf16(a, parts):
        d = dot(term, b_exact)
        acc = d if acc is None else acc + d
    return acc


def _dot_split_rhs(a_exact, b, parts):
    acc = None
    for term in _split_bf16(b, parts):
        d = jnp.dot(a_exact, term, preferred_element_type=F32)
        acc = d if acc is None else acc + d
    return acc


def _proj_kernel(x_ref, w_ref, *refs):
    o_refs, w_scr = refs[:-1], refs[-1]

    @pl.when(pl.program_id(0) == 0)
    def _():
        w_scr[...] = jnp.zeros_like(w_scr)
        dst = 0
        for src, n_src, n_dst in zip(PROJ_SRC_OFFSETS, PROJ_SRC_WIDTHS, PROJ_WIDTHS):
            w_scr[:, dst:dst + n_src] = w_ref[:, src:src + n_src].astype(BF16)
            dst += n_dst

    xb = x_ref[...].astype(BF16)
    off = 0
    for o_ref in o_refs:
        n = o_ref.shape[-1]
        o_ref[...] = jnp.dot(xb, w_scr[:, off:off + n],
                             preferred_element_type=F32).astype(o_ref.dtype)
        off += n


def _in_proj(x2d, w_in, layer):
    t = x2d.shape[0]
    tm = 1024
    return pl.pallas_call(
        _proj_kernel,
        grid=(t // tm,),
        in_specs=[pl.BlockSpec((tm, D_MODEL), lambda i: (i, 0)),
                  pl.BlockSpec((None, D_MODEL, w_in.shape[2]), lambda i: (layer, 0, 0),
                               pl.Buffered(1))],
        out_specs=[pl.BlockSpec((tm, n), lambda i: (i, 0)) for n in PROJ_WIDTHS],
        out_shape=[jax.ShapeDtypeStruct((t, n), dt) for n, dt in zip(PROJ_WIDTHS, PROJ_DTYPES)],
        scratch_shapes=[pltpu.VMEM((D_MODEL, sum(PROJ_WIDTHS)), BF16)],
        compiler_params=_cparams(("arbitrary",)),
        name="in_proj",
    )(x2d, w_in)


def _gla_setup(p_ref, lr_ref, wlr_ref, blr_ref, ng_ref, o_ref, st_ref):
    c = GLA_CHUNK
    nh, dk, dv = GLA_HEADS, GLA_DK, GLA_DV

    rb = GLA_ROWS
    ncb = rb // c
    ri = lax.broadcasted_iota(jnp.int32, (rb, rb), 0)
    ci = lax.broadcasted_iota(jnp.int32, (rb, rb), 1)
    tri = (ci <= ri).astype(BF16)
    klane_head = lax.broadcasted_iota(jnp.int32, (1, nh * dk), 1) // dk
    vlane_head = lax.broadcasted_iota(jnp.int32, (1, nh * dv), 1) // dv
    strow_head = lax.broadcasted_iota(jnp.int32, (nh * dv, 1), 0) // dv
    st_mask = strow_head == klane_head
    r4 = lax.broadcasted_iota(jnp.int32, (nh * c, c), 0) % c
    c4 = lax.broadcasted_iota(jnp.int32, (nh * c, c), 1)
    causal4 = c4 <= r4
    gi = lax.broadcasted_iota(jnp.int32, (nh * dv, nh * dv), 0) // dv
    gj = lax.broadcasted_iota(jnp.int32, (nh * dv, nh * dv), 1) // dv
    gmean = jnp.where(gi == gj, 1.0 / dv, 0.0).astype(BF16)
    wlr_hi, wlr_lo = _split_bf16(wlr_ref[...], 2)

    def one_seq(bb, rows):
        q = p_ref[bb, rows, 0:128].astype(F32) * (dk ** -0.5)
        k = p_ref[bb, rows, 128:256].astype(F32)
        vb = p_ref[bb, rows, 256:512]
        g = p_ref[bb, rows, 512:768].astype(F32)
        lr = lr_ref[bb, rows, :]
        lr_hi, lr_lo = _split_bf16(lr, 2)
        z = (jnp.dot(lr_hi, wlr_hi, preferred_element_type=F32)
             + jnp.dot(lr_hi, wlr_lo, preferred_element_type=F32)
             + jnp.dot(lr_lo, wlr_hi, preferred_element_type=F32)) + blr_ref[...]
        log_a = (jnp.minimum(z, 0.0) - jnp.log(1.0 + jnp.exp(-jnp.abs(z)))) * (1.0 / GLA_TAU)
        cumb = _dot_split_rhs(tri, log_a, 3)
        ends = [cumb[(j + 1) * c - 1:(j + 1) * c, :] for j in range(ncb)]
        starts = [jnp.zeros_like(ends[0])] + ends[:-1]
        cum = cumb - jnp.concatenate([jnp.broadcast_to(s0, (c, nh * dk)) for s0 in starts], axis=0)
        lasts = [e - s0 for e, s0 in zip(ends, starts)]
        cl = jnp.concatenate([jnp.broadcast_to(x, (c, nh * dk)) for x in lasts], axis=0)
        q_dec = q * jnp.exp(cum)
        k_inv = (k * jnp.exp(-cum)).astype(BF16)
        k_end = (k * jnp.exp(cl - cum)).astype(BF16)
        st = st_ref[bb]
        outs = []
        for j in range(ncb):
            sl = slice(j * c, (j + 1) * c)
            qd = q_dec[sl]
            qs = jnp.concatenate([jnp.where(klane_head == h, qd, 0.0) for h in range(nh)],
                                 axis=0).astype(BF16)
            att = jnp.where(causal4, _dot_nt(qs, k_inv[sl]), 0.0)
            r = jnp.dot(att.astype(BF16), vb[sl], preferred_element_type=F32)
            o = jnp.where(vlane_head == 0, r[0:c], 0.0)
            for h in range(1, nh):
                o = o + jnp.where(vlane_head == h, r[h * c:(h + 1) * c], 0.0)
            outs.append(o + _dot_nt(qd.astype(BF16), st.astype(BF16)))
            d_st = _dot_tn(vb[sl], k_end[sl])
            st = st * jnp.exp(lasts[j]) + jnp.where(st_mask, d_st, 0.0)
        st_ref[bb] = st
        o = jnp.concatenate(outs, axis=0)
        ms = _dot_split_lhs(o * o, gmean, 2)
        o = o * lax.rsqrt(ms + RMS_EPS) * ng_ref[...]
        o_ref[bb, rows, :] = (o * (g * _sigmoid(g))).astype(o_ref.dtype)

    return one_seq


def _ssd_setup(p_ref, dt_ref, cw_ref, cb_ref, alog_ref, dtb_ref, dsk_ref, ng_ref, o_ref, st_ref,
               halo_ref):
    c = SSD_CHUNK
    n_st = SSD_STATE
    halo = 2 * SUBLANES

    ri = lax.broadcasted_iota(jnp.int32, (c, c), 0)
    ci = lax.broadcasted_iota(jnp.int32, (c, c), 1)
    causal = ci <= ri
    tri = causal.astype(BF16)
    upper = (ri <= ci).astype(BF16)
    lane_head = lax.broadcasted_iota(jnp.int32, (1, W_MIX), 1) // SSD_HEADDIM
    lane_group = lane_head // (SSD_HEADS // SSD_GROUPS)
    eh = lax.broadcasted_iota(jnp.int32, (LANES, W_MIX), 0)
    el = lax.broadcasted_iota(jnp.int32, (LANES, W_MIX), 1) // SSD_HEADDIM
    expand = (eh == el).astype(BF16)
    row8 = lax.broadcasted_iota(jnp.int32, (8, 3 * W_MIX), 0)
    a_c = -jnp.exp(alog_ref[...])

    def one_chunk(n, bb):
        rows = slice(n * c, (n + 1) * c)
        cur = p_ref[bb, rows, 256:1024].astype(F32)
        before = halo_ref[bb] if n == 0 else p_ref[bb, n * c - halo:n * c, 256:1024]
        prev8 = before.astype(F32)[SUBLANES:]
        acc = cur * cw_ref[3:4, :] + cb_ref[...]
        for s in (1, 2, 3):
            sh = pltpu.roll(cur, s, axis=0)
            top = jnp.where(row8 < s, pltpu.roll(prev8, s, axis=0), sh[0:8])
            sh = jnp.concatenate([top, sh[8:]], axis=0)
            acc = acc + sh * cw_ref[3 - s:4 - s, :]
        xbc = acc * _sigmoid(acc)
        x = xbc[:, 0:256]
        bm = xbc[:, 256:512].astype(BF16)
        cm = xbc[:, 512:768].astype(BF16)

        dt_c = _softplus(dt_ref[bb, rows, :] + dtb_ref[...])
        da_c = dt_c * a_c
        cum_c = _dot_split_rhs(tri, da_c, 3)
        cum_r = _dot_split_lhs(da_c, upper, 3, dot=_dot_tn)
        both_x = _dot_split_lhs(jnp.concatenate([dt_c, cum_c], axis=0), expand, 3)
        dt_x = both_x[0:c]
        cum_x = both_x[c:2 * c]
        cl_x = cum_x[c - 1:c, :]
        x_dt = x * dt_x
        x_dt_b = x_dt.astype(BF16)
        xw_b = (x_dt * jnp.exp(cl_x - cum_x)).astype(BF16)

        y = x * dsk_ref[...]
        y_off = jnp.zeros((c, W_MIX), F32)
        for g in range(SSD_GROUPS):
            bg = bm[:, g * n_st:(g + 1) * n_st]
            cg = cm[:, g * n_st:(g + 1) * n_st]
            cb = _dot_nt(cg, bg)
            for r in range(SSD_HEADS // SSD_GROUPS):
                h = g * (SSD_HEADS // SSD_GROUPS) + r
                diff = cum_c[:, h:h + 1] - cum_r[h:h + 1, :]
                dec = jnp.exp(jnp.where(causal, diff, -jnp.inf))
                m = (cb * dec).astype(BF16)
                yh = jnp.dot(m, x_dt_b, preferred_element_type=F32)
                y = y + jnp.where(lane_head == h, yh, 0.0)
            st = st_ref[bb, g]
            y_off = y_off + jnp.where(lane_group == g,
                                      jnp.dot(cg, st.astype(BF16), preferred_element_type=F32), 0.0)
            d_st = _dot_tn(bg, xw_b)
            st_ref[bb, g] = st * jnp.exp(cl_x) + jnp.where(lane_group == g, d_st, 0.0)
        y = y + y_off * jnp.exp(cum_x)
        zg = p_ref[bb, rows, 0:256].astype(F32)
        y = y * (zg * _sigmoid(zg))
        outs = []
        for g in range(SSD_GROUPS):
            yg = y[:, g * 128:(g + 1) * 128]
            ms = jnp.mean(yg * yg, axis=-1, keepdims=True)
            outs.append(yg * lax.rsqrt(ms + RMS_EPS))
        o_ref[bb, rows, :] = (jnp.concatenate(outs, axis=-1) * ng_ref[...]).astype(o_ref.dtype)

    return one_chunk


def _recurrent_kernel(pc_ref, ccw_ref, pg_ref, lr_ref, wlr_ref, blr_ref, gng_ref,
                      ps_ref, dt_ref, cw_ref, cb_ref, alog_ref, dtb_ref, dsk_ref, sng_ref,
                      oc_ref, og_ref, os_ref, gst_ref, sst_ref, halo_ref, chalo_ref):
    @pl.when(pl.program_id(1) == 0)
    def _():
        gst_ref[...] = jnp.zeros_like(gst_ref)
        sst_ref[...] = jnp.zeros_like(sst_ref)
        halo_ref[...] = jnp.zeros_like(halo_ref)
        chalo_ref[...] = jnp.zeros_like(chalo_ref)

    gla_rows = _gla_setup(pg_ref, lr_ref, wlr_ref, blr_ref, gng_ref, og_ref, gst_ref)
    ssd_chunk = _ssd_setup(ps_ref, dt_ref, cw_ref, cb_ref, alog_ref, dtb_ref, dsk_ref, sng_ref,
                           os_ref, sst_ref, halo_ref)
    row8 = lax.broadcasted_iota(jnp.int32, (SUBLANES, W_MIX), 0)
    for bb in range(pg_ref.shape[0]):
        gla_rows(bb, slice(0, GLA_ROWS))
        for n in range(GLA_ROWS // SSD_CHUNK):
            ssd_chunk(n, bb)
        halo_ref[bb] = ps_ref[bb, GLA_ROWS - halo_ref.shape[1]:GLA_ROWS, 256:1024]

        u = pc_ref[bb, :, 0:W_MIX].astype(F32)
        gb = pc_ref[bb, :, W_MIX:2 * W_MIX].astype(F32)
        gc = pc_ref[bb, :, 2 * W_MIX:3 * W_MIX].astype(F32)
        cu = gc * u
        prev8 = chalo_ref[bb]
        acc = cu * ccw_ref[2:3, :]
        for s in (1, 2):
            sh = pltpu.roll(cu, s, axis=0)
            top = jnp.where(row8 < s, pltpu.roll(prev8, s, axis=0), sh[0:SUBLANES])
            acc = acc + jnp.concatenate([top, sh[SUBLANES:]], axis=0) * ccw_ref[2 - s:3 - s, :]
        oc_ref[bb] = (gb * acc).astype(oc_ref.dtype)
        chalo_ref[bb] = cu[GLA_ROWS - SUBLANES:GLA_ROWS]


def _recurrent_mixers(pc, sconv_w, pg, plr, w_lr_pad, b_lr, gla_norm_g4,
                      ps, pdt, conv_w, conv_b, a_log_c, dt_bias_c, d_x, ssd_norm_g):
    b, s, _ = pg.shape
    nb = REC_SEQS
    rb = GLA_ROWS
    seq = lambda i, j: (i, j, 0)
    full2 = lambda i, j: (0, 0)
    return pl.pallas_call(
        _recurrent_kernel,
        grid=(b // nb, s // rb),
        in_specs=[pl.BlockSpec((nb, rb, pc.shape[2]), seq),
                  pl.BlockSpec((3, W_MIX), full2),
                  pl.BlockSpec((nb, rb, pg.shape[2]), seq),
                  pl.BlockSpec((nb, rb, LANES), seq),
                  pl.BlockSpec((LANES, LANES), full2),
                  pl.BlockSpec((1, LANES), full2),
                  pl.BlockSpec((1, W_MIX), full2),
                  pl.BlockSpec((nb, rb, ps.shape[2]), seq),
                  pl.BlockSpec((nb, rb, LANES), seq),
                  pl.BlockSpec((SSD_CONV_K, 3 * W_MIX), full2),
                  pl.BlockSpec((1, 3 * W_MIX), full2),
                  pl.BlockSpec((1, LANES), full2),
                  pl.BlockSpec((1, LANES), full2),
                  pl.BlockSpec((1, W_MIX), full2),
                  pl.BlockSpec((1, W_MIX), full2)],
        out_specs=[pl.BlockSpec((nb, rb, W_MIX), seq)] * 3,
        out_shape=[jax.ShapeDtypeStruct((b, s, W_MIX), BF16)] * 3,
        scratch_shapes=[pltpu.VMEM((nb, GLA_HEADS * GLA_DV, GLA_HEADS * GLA_DK), F32),
                        pltpu.VMEM((nb, SSD_GROUPS, SSD_STATE, W_MIX), F32),
                        pltpu.VMEM((nb, 2 * SUBLANES, 3 * W_MIX), BF16),
                        pltpu.VMEM((nb, SUBLANES, W_MIX), F32)],
        compiler_params=_cparams(("parallel", "arbitrary")),
        name="conv_gla_ssd_mixers",
    )(pc, sconv_w, pg, plr, w_lr_pad, b_lr, gla_norm_g4, ps, pdt, conv_w, conv_b, a_log_c,
      dt_bias_c, d_x, ssd_norm_g)


DIFF_TQ = 256
DIFF_TK = 256
LOG2E = 1.4426950408889634
DIFF_VPAD = DIFF_DV + 16
DIFF_SEQS = 4


def _diff_kernel(q_ref, k_ref, v_ref, lam_ref, ng_ref, o_ref,
                 vt_ref, qs_ref, st_ref, m_ref, acc_ref, *, lam_init):
    tq, tk = DIFF_TQ, DIFF_TK
    nh, dv = DIFF_HEADS, DIFF_DV
    nhc = 2 * nh
    s_len = k_ref.shape[1]
    i = pl.program_id(1)
    seqs = range(q_ref.shape[0])

    @pl.when(i == 0)
    def _():
        for bb in seqs:
            for cblk in range(s_len // tk):
                cols = slice(cblk * tk, (cblk + 1) * tk)
                vt = v_ref[bb, cols, :].astype(F32).T.astype(BF16)
                for h in range(nh):
                    vt_ref[bb, h, 0:dv, cols] = vt[h * dv:(h + 1) * dv]
        vt_ref[:, :, dv:, :] = jnp.ones((len(seqs), nh, DIFF_VPAD - dv, s_len), BF16)

    qlane = lax.broadcasted_iota(jnp.int32, (1, W_MIX), 1) // DIFF_DQK
    for bb in seqs:
        q = q_ref[bb].astype(F32) * (DIFF_DQK ** -0.5 * LOG2E)
        for hc in range(nhc):
            qs_ref[bb, hc * tq:(hc + 1) * tq, :] = jnp.where(qlane == hc, q, 0.0).astype(BF16)
    m_ref[...] = jnp.full_like(m_ref, -jnp.inf)
    acc_ref[...] = jnp.zeros_like(acc_ref)
    krow = lax.broadcasted_iota(jnp.int32, (tk, nhc * tq), 0)
    qcol = lax.broadcasted_iota(jnp.int32, (tk, nhc * tq), 1) % tq
    diag_ok = krow <= qcol

    def scores(j, slot):
        k0 = pl.multiple_of(j * tk, tk)
        for bb in seqs:
            st_ref[bb, slot] = _dot_nt(k_ref[bb, pl.ds(k0, tk), :], qs_ref[bb])

    def softmax_pv(j, slot, masked):
        for bb in seqs:
            softmax_pv_seq(bb, j, slot, masked)

    def softmax_pv_seq(bb, j, slot, masked):
        k0 = pl.multiple_of(j * tk, tk)
        st = st_ref[bb, slot]
        if masked:
            st = jnp.where(diag_ok, st, -jnp.inf)
        m_prev = m_ref[bb]
        m_new = jnp.maximum(m_prev, jnp.max(st, axis=0, keepdims=True))
        alpha = jnp.exp2(m_prev - m_new)
        p = jnp.exp2(st - m_new)
        m_ref[bb] = m_new
        pb = p.astype(BF16)
        for hc in range(nhc):
            h = hc // 2
            lanes = slice(hc * tq, (hc + 1) * tq)
            pv = jnp.dot(vt_ref[bb, h, :, pl.ds(k0, tk)], pb[:, lanes],
                         preferred_element_type=F32)
            acc_ref[bb, hc] = acc_ref[bb, hc] * alpha[:, lanes] + pv

    scores(0, 0)
    n_pairs = i // 2

    def pair_step(u, carry):
        scores(2 * u + 1, 1)
        softmax_pv(2 * u, 0, False)
        scores(2 * u + 2, 0)
        softmax_pv(2 * u + 1, 1, False)
        return carry

    lax.fori_loop(0, n_pairs, pair_step, 0)

    @pl.when(i % 2 == 0)
    def _():
        softmax_pv(i, 0, True)

    @pl.when(i % 2 == 1)
    def _():
        scores(i, 1)
        softmax_pv(i - 1, 0, False)
        softmax_pv(i, 1, True)

    lam = (jnp.exp(jnp.sum(lam_ref[0:1, :] * lam_ref[1:2, :], axis=-1, keepdims=True))
           - jnp.exp(jnp.sum(lam_ref[2:3, :] * lam_ref[3:4, :], axis=-1, keepdims=True))
           + lam_init)
    for bb in seqs:
        heads = []
        for h in range(nh):
            o1 = acc_ref[bb, 2 * h, 0:dv] / acc_ref[bb, 2 * h, dv:dv + 1]
            o2 = acc_ref[bb, 2 * h + 1, 0:dv] / acc_ref[bb, 2 * h + 1, dv:dv + 1]
            oh = o1 - lam * o2
            ms = jnp.mean(oh * oh, axis=0, keepdims=True)
            heads.append(oh * lax.rsqrt(ms + RMS_EPS))
        o = jnp.concatenate(heads, axis=0).T
        o_ref[bb] = (o * ng_ref[...] * (1.0 - lam_init)).astype(o_ref.dtype)


def _diff_mixer(pd, lam_vecs, norm_g4, lam_init):
    b, s, _ = pd.shape
    tq = DIFF_TQ
    nb = DIFF_SEQS
    return pl.pallas_call(
        functools.partial(_diff_kernel, lam_init=lam_init),
        grid=(b // nb, s // tq),
        in_specs=[pl.BlockSpec((nb, tq, W_MIX), lambda bi, i: (bi, i, 0)),
                  pl.BlockSpec((nb, s, W_MIX), lambda bi, i: (bi, 0, 1)),
                  pl.BlockSpec((nb, s, W_MIX), lambda bi, i: (bi, 0, 2)),
                  pl.BlockSpec((4, LANES), lambda bi, i: (0, 0)),
                  pl.BlockSpec((1, W_MIX), lambda bi, i: (0, 0))],
        out_specs=pl.BlockSpec((nb, tq, W_MIX), lambda bi, i: (bi, i, 0)),
        out_shape=jax.ShapeDtypeStruct((b, s, W_MIX), BF16),
        scratch_shapes=[pltpu.VMEM((nb, DIFF_HEADS, DIFF_VPAD, s), BF16),
                        pltpu.VMEM((nb, 2 * DIFF_HEADS * tq, W_MIX), BF16),
                        pltpu.VMEM((nb, 2, DIFF_TK, 2 * DIFF_HEADS * tq), F32),
                        pltpu.VMEM((nb, 1, 2 * DIFF_HEADS * tq), F32),
                        pltpu.VMEM((nb, 2 * DIFF_HEADS, DIFF_VPAD, tq), F32)],
        compiler_params=_cparams(("parallel", "arbitrary")),
        name="diff_attn",
    )(pd, pd, pd, lam_vecs, norm_g4)


def _oproj_kernel(yc_ref, yg_ref, ys_ref, yd_ref, x_ref, wo_ref, g_ref, b_ref, wr_ref,
                  xo_ref, xp_ref, route_ref, cnt_ref):
    mix = jnp.concatenate([yc_ref[...], yg_ref[...], ys_ref[...], yd_ref[...]], axis=-1)
    h = ALPHA * x_ref[...] + jnp.dot(mix, wo_ref[...], preferred_element_type=F32)
    xn = _layer_norm(h, g_ref[...], b_ref[...])
    xo_ref[...] = xn
    xp_ref[...] = _pack_halves(xn)

    xn_hi, xn_lo = _split_bf16(xn, 2)
    both = _dot(xn_hi, wr_ref[...])
    logits = both[:, 0:LANES] + both[:, LANES:2 * LANES] + _dot(xn_lo, wr_ref[:, 0:LANES])
    lane = lax.broadcasted_iota(jnp.int32, logits.shape, 1).astype(F32)
    neg = -jnp.inf
    big = float(LANES)
    lg = jnp.where(lane < N_GROUPS, logits, neg)
    mg = jnp.max(lg, axis=-1, keepdims=True)
    sg = jnp.sum(jnp.exp(lg - mg), axis=-1, keepdims=True)
    grp = jnp.min(jnp.where(lg == mg, lane, big), axis=-1, keepdims=True)
    p_grp = 1.0 / sg
    lo = N_GROUPS + EXPERTS_PER_GROUP * grp
    in_g = jnp.logical_and(lane >= lo, lane < lo + EXPERTS_PER_GROUP)
    le = jnp.where(in_g, logits, neg)
    me = jnp.max(le, axis=-1, keepdims=True)
    ee = jnp.exp(le - me)
    pe = ee / jnp.sum(ee, axis=-1, keepdims=True)
    pe = jnp.where(in_g, pe, -1.0)
    p1 = jnp.max(pe, axis=-1, keepdims=True)
    i1 = jnp.min(jnp.where(pe == p1, lane, big), axis=-1, keepdims=True)
    pe2 = jnp.where(lane == i1, -1.0, pe)
    p2 = jnp.max(pe2, axis=-1, keepdims=True)
    i2 = jnp.min(jnp.where(pe2 == p2, lane, big), axis=-1, keepdims=True)
    den = p1 + p2
    g1 = p_grp * p1 / den
    g2 = p_grp * p2 / den
    e1 = i1 - N_GROUPS
    e2 = i2 - N_GROUPS
    route_ref[...] = jnp.where(lane == 0, e1, jnp.where(lane == 1, e2, jnp.where(
        lane == 2, g1, jnp.where(lane == 3, g2, 0.0))))

    @pl.when(pl.program_id(0) == 0)
    def _():
        cnt_ref[...] = jnp.zeros_like(cnt_ref)

    hits = jnp.where(lane == e1, 1.0, 0.0) + jnp.where(lane == e2, 1.0, 0.0)
    cnt_ref[...] += jnp.sum(hits, axis=0, keepdims=True)


def _out_proj(ys, x2d, w_o, layer, ln_g, ln_b, w_route):
    t = x2d.shape[0]
    tm = 1024
    row = lambda i: (i, 0)
    full = lambda i: (0, 0)
    return pl.pallas_call(
        _oproj_kernel,
        grid=(t // tm,),
        in_specs=[pl.BlockSpec((tm, W_MIX), row)] * 4 + [
            pl.BlockSpec((tm, D_MODEL), row),
            pl.BlockSpec((None, D_MODEL, D_MODEL), lambda i: (layer, 0, 0)),
            pl.BlockSpec((1, D_MODEL), full),
            pl.BlockSpec((1, D_MODEL), full),
            pl.BlockSpec((D_MODEL, 2 * LANES), full)],
        out_specs=[pl.BlockSpec((tm, D_MODEL), row), pl.BlockSpec((tm, D_MODEL // 2), row),
                   pl.BlockSpec((tm, LANES), row), pl.BlockSpec((1, LANES), full)],
        out_shape=[jax.ShapeDtypeStruct((t, D_MODEL), F32),
                   jax.ShapeDtypeStruct((t, D_MODEL // 2), U32),
                   jax.ShapeDtypeStruct((t, LANES), F32),
                   jax.ShapeDtypeStruct((1, LANES), F32)],
        compiler_params=_cparams(("arbitrary",)),
        name="out_proj_ln_router",
    )(*ys, x2d, w_o, ln_g, ln_b, w_route)


PLAN_TILE = 512


def _plan_kernel(route_ref, cnt_ref, dest_ref, meta_ref, carry_ref, pstart_ref):
    tm = route_ref.shape[0]
    lane = lax.broadcasted_iota(jnp.int32, (1, LANES), 1).astype(F32)

    @pl.when(pl.program_id(0) == 0)
    def _():
        cnt = cnt_ref[...]
        padded = jnp.ceil(cnt * (1.0 / MOE_BLK)) * MOE_BLK
        li = lax.broadcasted_iota(jnp.int32, (LANES, LANES), 0)
        lj = lax.broadcasted_iota(jnp.int32, (LANES, LANES), 1)
        before = (li < lj).astype(F32)
        pstart = jnp.dot(jnp.broadcast_to(padded, (8, LANES)), before, precision=HI,
                         preferred_element_type=F32)[0:1]
        pstart_ref[...] = pstart
        carry_ref[...] = jnp.zeros_like(carry_ref)
        meta_ref[...] = jnp.concatenate(
            [pstart + padded, pstart, cnt, jnp.zeros((5, LANES), F32)], axis=0)

    oh0 = jnp.where(lane == route_ref[:, 0:1], 1.0, 0.0)
    oh1 = jnp.where(lane == route_ref[:, 1:2], 1.0, 0.0)
    both = oh0 + oh1
    ri = lax.broadcasted_iota(jnp.int32, (tm, tm), 0)
    ci = lax.broadcasted_iota(jnp.int32, (tm, tm), 1)
    earlier = (ci < ri).astype(BF16)
    base = (jnp.dot(earlier, both.astype(BF16), preferred_element_type=F32)
            + carry_ref[...] + pstart_ref[...])
    d0 = jnp.sum(oh0 * base, axis=-1, keepdims=True)
    d1 = jnp.sum(oh1 * base, axis=-1, keepdims=True)
    dest = jnp.where(lane == 0, d0, jnp.where(lane == 1, d1, 0.0))
    dest_ref[...] = dest.T[0:2, :].astype(jnp.int32)
    carry_ref[...] += jnp.sum(both, axis=0, keepdims=True)


def _dispatch_plan(route, cnt, n_blocks):
    t = route.shape[0]
    tm = PLAN_TILE
    blk = MOE_BLK
    dest, meta = pl.pallas_call(
        _plan_kernel,
        grid=(t // tm,),
        in_specs=[pl.BlockSpec((tm, LANES), lambda i: (i, 0)),
                  pl.BlockSpec((1, LANES), lambda i: (0, 0))],
        out_specs=[pl.BlockSpec((2, tm), lambda i: (0, i)),
                   pl.BlockSpec((8, LANES), lambda i: (0, 0))],
        out_shape=[jax.ShapeDtypeStruct((2, t), jnp.int32),
                   jax.ShapeDtypeStruct((8, LANES), F32)],
        scratch_shapes=[pltpu.VMEM((1, LANES), F32), pltpu.VMEM((1, LANES), F32)],
        compiler_params=_cparams(("arbitrary",)),
        name="moe_plan",
    )(route, cnt)
    meta_i = meta[:, :N_EXPERTS].astype(jnp.int32)
    pad_end, pad_start, seg_end = meta_i[0], meta_i[1], meta_i[1] + meta_i[2]
    starts = jnp.arange(n_blocks, dtype=jnp.int32)[:, None] * blk
    member = jnp.logical_and(starts >= pad_start[None, :], starts < pad_end[None, :])
    expert_ids = jnp.arange(N_EXPERTS, dtype=jnp.int32)[None, :]
    block_e = jnp.where(jnp.any(member, axis=1), jnp.sum(jnp.where(member, expert_ids, 0), axis=1),
                        N_EXPERTS - 1)
    n_used = (pad_end[N_EXPERTS - 1] // blk).reshape(1)
    n_valid = jnp.clip(jnp.sum(jnp.where(member, seg_end[None, :], 0), axis=1) - starts[:, 0],
                       0, blk)
    return dest.reshape(2 * t), block_e, n_used, n_valid


def _ffn_kernel(be_ref, nu_ref, nv_ref, xs_ref, wg_ref, wu_ref, wd_ref, y_ref,
                wgb_ref, wub_ref, wdb_ref):
    i = pl.program_id(0)
    prev = be_ref[jnp.maximum(i - 1, 0)]

    @pl.when(jnp.logical_or(i == 0, be_ref[i] != prev))
    def _():
        wgb_ref[...] = wg_ref[...].astype(BF16)
        wub_ref[...] = wu_ref[...].astype(BF16)
        wdb_ref[...] = wd_ref[...].astype(BF16)

    @pl.when(i < nu_ref[0])
    def _():
        half = xs_ref.shape[0] // 2
        for r in range(2):
            rows = slice(r * half, (r + 1) * half)
            row = lax.broadcasted_iota(jnp.int32, (half, 1), 0) + r * half
            xp = jnp.where(row < nv_ref[i], xs_ref[rows, :], U32(0))
            x_hi, x_lo = _unpack_halves(xp)
            xb = jnp.concatenate([x_hi.astype(BF16), x_lo.astype(BF16)], axis=1)
            a = jnp.dot(xb, wgb_ref[...], preferred_element_type=F32)
            u = jnp.dot(xb, wub_ref[...], preferred_element_type=F32)
            h = (a * _sigmoid(a) * u).astype(BF16)
            y_ref[rows, :] = _pack_halves(jnp.dot(h, wdb_ref[...], preferred_element_type=F32))

    @pl.when(i >= nu_ref[0])
    def _():
        y_ref[...] = jnp.zeros_like(y_ref)


def _expert_ffn(xs, block_e, n_used, n_valid, w_gate, w_up, w_down, layer):
    n_slots = xs.shape[0]
    blk = MOE_BLK
    w_map = lambda i, be, nu, nv: (layer, be[i], 0, 0)
    grid_spec = pltpu.PrefetchScalarGridSpec(
        num_scalar_prefetch=3,
        grid=(n_slots // blk,),
        in_specs=[pl.BlockSpec((blk, D_MODEL // 2),
                               lambda i, be, nu, nv: (jnp.minimum(i, nu[0] - 1), 0)),
                  pl.BlockSpec((None, None, D_MODEL, D_EXPERT), w_map),
                  pl.BlockSpec((None, None, D_MODEL, D_EXPERT), w_map),
                  pl.BlockSpec((None, None, D_EXPERT, D_MODEL), w_map)],
        out_specs=pl.BlockSpec((blk, D_MODEL // 2), lambda i, be, nu, nv: (i, 0)),
        scratch_shapes=[pltpu.VMEM((D_MODEL, D_EXPERT), BF16),
                        pltpu.VMEM((D_MODEL, D_EXPERT), BF16),
                        pltpu.VMEM((D_EXPERT, D_MODEL), BF16)],
    )
    return pl.pallas_call(
        _ffn_kernel,
        grid_spec=grid_spec,
        out_shape=jax.ShapeDtypeStruct((n_slots, D_MODEL // 2), U32),
        compiler_params=_cparams(("arbitrary",)),
        name="expert_ffn",
    )(block_e, n_used, n_valid, xs, w_gate, w_up, w_down)


SC_CORES = 2
SC_SUBCORES = 16
SC_ROWS = 64


def _sc_gather_rows(table, idx):
    b = idx.shape[0]
    d = table.shape[1]
    per_w = b // (SC_CORES * SC_SUBCORES)
    mesh = plsc.VectorSubcoreMesh(core_axis_name="c", subcore_axis_name="s")

    n_chunks = per_w // SC_ROWS

    @functools.partial(
        pl.kernel, mesh=mesh,
        out_type=jax.ShapeDtypeStruct((b, d), table.dtype),
        scratch_types=[pltpu.VMEM((SC_ROWS,), jnp.int32), pltpu.VMEM((SC_ROWS,), jnp.int32),
                       pltpu.VMEM((SC_ROWS, d), table.dtype),
                       pltpu.VMEM((SC_ROWS, d), table.dtype),
                       pltpu.SemaphoreType.DMA, pltpu.SemaphoreType.DMA,
                       pltpu.SemaphoreType.DMA, pltpu.SemaphoreType.DMA],
        name="sc_gather_rows",
    )
    def gather(table_hbm, idx_hbm, out_hbm, idx0, idx1, rows0, rows1, gs0, gs1, ws0, ws1):
        idx_v, rows_v, gsem, wsem = (idx0, idx1), (rows0, rows1), (gs0, gs1), (ws0, ws1)
        wid = lax.axis_index("s") * SC_CORES + lax.axis_index("c")
        base = wid * per_w

        def rows_of(c):
            return pl.ds(pl.multiple_of(base + c * SC_ROWS, SC_ROWS), SC_ROWS)

        def start_gather(c, s):
            pltpu.sync_copy(idx_hbm.at[rows_of(c)], idx_v[s])
            pltpu.async_copy(table_hbm.at[idx_v[s]], rows_v[s], gsem[s])

        def write_back(c, s):
            pltpu.make_async_copy(table_hbm.at[idx_v[s]], rows_v[s], gsem[s]).wait()
            pltpu.async_copy(rows_v[s], out_hbm.at[rows_of(c)], wsem[s]).wait()

        start_gather(0, 0)

        @pl.loop(0, n_chunks, step=2)
        def _(c):
            start_gather(c + 1, 1)
            write_back(c, 0)

            @pl.when(c + 2 < n_chunks)
            def _():
                start_gather(c + 2, 0)

            write_back(c + 1, 1)

    return gather(table, idx)


def _sc_scatter_rows(x2d, idx, n_slots):
    t, d = x2d.shape
    per_w = t // (SC_CORES * SC_SUBCORES)
    mesh = plsc.VectorSubcoreMesh(core_axis_name="c", subcore_axis_name="s")

    @functools.partial(
        pl.kernel, mesh=mesh,
        out_type=jax.ShapeDtypeStruct((n_slots, d), x2d.dtype),
        scratch_types=[pltpu.VMEM((SC_ROWS,), jnp.int32), pltpu.VMEM((SC_ROWS,), jnp.int32),
                       pltpu.VMEM((SC_ROWS, d), x2d.dtype),
                       pltpu.SemaphoreType.DMA, pltpu.SemaphoreType.DMA],
        name="sc_scatter_rows",
    )
    def scatter(x_hbm, idx_hbm, out_hbm, idx0, idx1, rows_v, s0, s1):
        wid = lax.axis_index("s") * SC_CORES + lax.axis_index("c")
        base = wid * per_w

        @pl.loop(0, per_w // SC_ROWS)
        def _(c):
            off = pl.multiple_of(base + c * SC_ROWS, SC_ROWS)
            pltpu.sync_copy(x_hbm.at[pl.ds(off, SC_ROWS)], rows_v)
            pltpu.sync_copy(idx_hbm.at[pl.ds(off, SC_ROWS)], idx0)
            pltpu.sync_copy(idx_hbm.at[pl.ds(t + off, SC_ROWS)], idx1)
            cp0 = pltpu.async_copy(rows_v, out_hbm.at[idx0], s0)
            cp1 = pltpu.async_copy(rows_v, out_hbm.at[idx1], s1)
            cp0.wait()
            cp1.wait()

    return scatter(x2d, idx)


def _combine_dense_kernel(route_ref, x_ref, y0_ref, y1_ref, g_ref, b_ref, o_ref):
    y0 = jnp.concatenate(_unpack_halves(y0_ref[...]), axis=1)
    y1 = jnp.concatenate(_unpack_halves(y1_ref[...]), axis=1)
    moe = route_ref[:, 2:3] * y0 + route_ref[:, 3:4] * y1
    h = ALPHA * x_ref[...] + moe
    o_ref[...] = _layer_norm(h, g_ref[...], b_ref[...])


def _combine_dense(route, x2d, yg, ln_g, ln_b):
    t = x2d.shape[0]
    tm = 1024
    nt = t // tm
    row = lambda i: (i, 0)
    full = lambda i: (0, 0)
    return pl.pallas_call(
        _combine_dense_kernel,
        grid=(nt,),
        in_specs=[pl.BlockSpec((tm, LANES), row),
                  pl.BlockSpec((tm, D_MODEL), row),
                  pl.BlockSpec((tm, D_MODEL // 2), row),
                  pl.BlockSpec((tm, D_MODEL // 2), lambda i: (i + nt, 0)),
                  pl.BlockSpec((1, D_MODEL), full),
                  pl.BlockSpec((1, D_MODEL), full)],
        out_specs=pl.BlockSpec((tm, D_MODEL), row),
        out_shape=jax.ShapeDtypeStruct((t, D_MODEL), F32),
        compiler_params=_cparams(("parallel",)),
        name="moe_combine_dense",
    )(route, x2d, yg, yg, ln_g, ln_b)


def _pad_cols(w, n):
    return jnp.pad(w, [(0, 0)] * (w.ndim - 1) + [(0, n - w.shape[-1])])


def kernel(x, w_in, conv_w, gla_w_lr, gla_b_lr, gla_norm_g, ssd_conv_w, ssd_conv_b, ssd_a_log,
           ssd_d, ssd_dt_bias, ssd_norm_g, diff_lq1, diff_lk1, diff_lq2, diff_lk2, diff_norm_g,
           w_o, ln1_g, ln1_b, router_g, router_e, w_gate, w_up, w_down, ln2_g, ln2_b):
    bsz, seq, d = x.shape
    t = bsz * seq
    n_assign = 2 * t
    n_blocks = (n_assign + N_EXPERTS * (MOE_BLK - 1)) // MOE_BLK + 1
    n_slots = n_blocks * MOE_BLK
    x2d = x.reshape(t, d)
    w_o_b = w_o.astype(BF16)
    for l in range(DEPTH):
        pc, pg, plr, ps, pdt, pd = [p.reshape(bsz, seq, -1) for p in _in_proj(x2d, w_in, l)]

        w_lr_pad = jnp.pad(gla_w_lr[l], ((0, LANES - GLA_RANK), (0, 0)))
        pad4 = lambda v: jnp.pad(v, (0, LANES - SSD_HEADS)).reshape(1, LANES)
        y_conv, y_gla, y_ssd = _recurrent_mixers(
            pc, conv_w[l], pg, plr, w_lr_pad, gla_b_lr[l].reshape(1, -1),
            jnp.tile(gla_norm_g[l], GLA_HEADS).reshape(1, -1),
            ps, pdt, ssd_conv_w[l], ssd_conv_b[l].reshape(1, -1),
            pad4(ssd_a_log[l]), pad4(ssd_dt_bias[l]),
            jnp.repeat(ssd_d[l], SSD_HEADDIM).reshape(1, -1), ssd_norm_g[l].reshape(1, -1))
        lam_vecs = jnp.pad(jnp.stack([diff_lq1[l], diff_lk1[l], diff_lq2[l], diff_lk2[l]]),
                           ((0, 0), (0, LANES - DIFF_DQK)))
        lam_init = 0.8 - 0.6 * math.exp(-0.3 * l)
        y_diff = _diff_mixer(pd, lam_vecs,
                             jnp.tile(diff_norm_g[l], DIFF_HEADS).reshape(1, -1), lam_init)

        w_route = _pad_cols(jnp.concatenate(
            [router_g[l], router_e[l].reshape(d, N_EXPERTS)], axis=1), LANES)
        w_route_hi = w_route.astype(BF16)
        w_route = jnp.concatenate(
            [w_route_hi, (w_route - w_route_hi.astype(F32)).astype(BF16)], axis=1)
        ys = [y.reshape(t, W_MIX) for y in (y_conv, y_gla, y_ssd, y_diff)]
        xn, xn_p, route, cnt = _out_proj(ys, x2d, w_o_b, l, ln1_g[l].reshape(1, -1),
                                         ln1_b[l].reshape(1, -1), w_route)

        idx, block_e, n_used, n_valid = _dispatch_plan(route, cnt, n_blocks)
        xs = _sc_scatter_rows(xn_p, idx, n_slots)
        y = _expert_ffn(xs, block_e, n_used, n_valid, w_gate, w_up, w_down, l)
        yg = _sc_gather_rows(y, idx)
        x2d = _combine_dense(route, xn, yg, ln2_g[l].reshape(1, -1), ln2_b[l].reshape(1, -1))
    return x2d.reshape(bsz, seq, d)
```

```python
import functools
import math

import jax
import jax.numpy as jnp
from jax import lax
from jax.experimental import pallas as pl
from jax.experimental.pallas import tpu as pltpu
from jax.experimental.pallas import tpu_sc as plsc

F32 = jnp.float32
BF16 = jnp.bfloat16
HI = lax.Precision.HIGHEST

D_MODEL = 1024
DEPTH = 2
W_MIX = 256
GLA_HEADS, GLA_DK, GLA_DV, GLA_RANK, GLA_TAU, GLA_CHUNK = 4, 32, 64, 16, 16.0, 64
GLA_ROWS = 256
REC_SEQS = 4
SSD_HEADS, SSD_GROUPS, SSD_HEADDIM, SSD_STATE, SSD_CONV_K, SSD_CHUNK = 4, 2, 64, 128, 4, 128
DIFF_HEADS, DIFF_DQK, DIFF_DV = 4, 32, 64
N_GROUPS, EXPERTS_PER_GROUP, N_EXPERTS, D_EXPERT = 4, 8, 32, 512
ALPHA = (2 * DEPTH) ** 0.25
LN_EPS = 1e-5
RMS_EPS = 1e-6

LANES = 128
SUBLANES = 8
PROJ_WIDTHS = (768, 768, 128, 1024, 128, 768)
PROJ_SRC_OFFSETS = (0, 768, 1536, 1552, 2576, 2580)
PROJ_SRC_WIDTHS = (768, 768, GLA_RANK, 1024, SSD_HEADS, 768)
PROJ_DTYPES = (BF16, BF16, F32, BF16, F32, BF16)
VMEM_LIMIT = 56 * 1024 * 1024

MOE_BLK = 512
OPROJ_PARTS = 2


def _cparams(sem):
    return pltpu.CompilerParams(dimension_semantics=sem, vmem_limit_bytes=VMEM_LIMIT)


def _sigmoid(x):
    return 1.0 / (1.0 + jnp.exp(-x))


def _softplus(x):
    return jnp.maximum(x, 0.0) + jnp.log(1.0 + jnp.exp(-jnp.abs(x)))


def _layer_norm(h, g, b):
    mu = jnp.mean(h, axis=-1, keepdims=True)
    d = h - mu
    var = jnp.mean(d * d, axis=-1, keepdims=True)
    return d * lax.rsqrt(var + LN_EPS) * g + b


def _dot_nt(a, b):
    return lax.dot_general(a, b, (((1,), (1,)), ((), ())), preferred_element_type=F32)


def _dot_tn(a, b, precision=None):
    return lax.dot_general(a, b, (((0,), (0,)), ((), ())), preferred_element_type=F32,
                           precision=precision)


def _split_bf16(x, parts):
    out = []
    for _ in range(parts - 1):
        hi = x.astype(BF16)
        out.append(hi)
        x = x - hi.astype(F32)
    out.append(x.astype(BF16))
    return out


def _dot(a, b):
    return jnp.dot(a, b, preferred_element_type=F32)


U32 = jnp.uint32


def _pack_halves(x):
    w = x.shape[1] // 2
    hi = lax.bitcast_convert_type(x[:, :w].astype(BF16).astype(F32), U32)
    lo = lax.bitcast_convert_type(x[:, w:].astype(BF16).astype(F32), U32)
    return hi | lax.shift_right_logical(lo, U32(16))


def _unpack_halves(p):
    hi = lax.bitcast_convert_type(p & U32(0xFFFF0000), F32)
    lo = lax.bitcast_convert_type(lax.shift_left(p, U32(16)), F32)
    return hi, lo


def _dot_split_lhs(a, b_exact, parts, dot=_dot):
    acc = None
    for term in _split_bf16(a, parts):
        d = dot(term, b_exact)
        acc = d if acc is None else acc + d
    return acc


def _dot_split_rhs(a_exact, b, parts):
    acc = None
    for term in _split_bf16(b, parts):
        d = jnp.dot(a_exact, term, preferred_element_type=F32)
        acc = d if acc is None else acc + d
    return acc


def _proj_kernel(x_ref, w_ref, *refs):
    o_refs, w_scr = refs[:-1], refs[-1]

    @pl.when(pl.program_id(0) == 0)
    def _():
        w_scr[...] = jnp.zeros_like(w_scr)
        dst = 0
        for src, n_src, n_dst in zip(PROJ_SRC_OFFSETS, PROJ_SRC_WIDTHS, PROJ_WIDTHS):
            w_scr[:, dst:dst + n_src] = w_ref[:, src:src + n_src].astype(BF16)
            dst += n_dst

    xb = x_ref[...].astype(BF16)
    off = 0
    for o_ref in o_refs:
        n = o_ref.shape[-1]
        o_ref[...] = jnp.dot(xb, w_scr[:, off:off + n],
                             preferred_element_type=F32).astype(o_ref.dtype)
        off += n


def _in_proj(x2d, w_in, layer):
    t = x2d.shape[0]
    tm = 1024
    return pl.pallas_call(
        _proj_kernel,
        grid=(t // tm,),
        in_specs=[pl.BlockSpec((tm, D_MODEL), lambda i: (i, 0)),
                  pl.BlockSpec((None, D_MODEL, w_in.shape[2]), lambda i: (layer, 0, 0),
                               pl.Buffered(1))],
        out_specs=[pl.BlockSpec((tm, n), lambda i: (i, 0)) for n in PROJ_WIDTHS],
        out_shape=[jax.ShapeDtypeStruct((t, n), dt) for n, dt in zip(PROJ_WIDTHS, PROJ_DTYPES)],
        scratch_shapes=[pltpu.VMEM((D_MODEL, sum(PROJ_WIDTHS)), BF16)],
        compiler_params=_cparams(("arbitrary",)),
        name="in_proj",
    )(x2d, w_in)


def _gla_setup(p_ref, lr_ref, wlr_ref, blr_ref, ng_ref, o_ref, st_ref):
    c = GLA_CHUNK
    nh, dk, dv = GLA_HEADS, GLA_DK, GLA_DV

    rb = GLA_ROWS
    ncb = rb // c
    ri = lax.broadcasted_iota(jnp.int32, (rb, rb), 0)
    ci = lax.broadcasted_iota(jnp.int32, (rb, rb), 1)
    tri = (ci <= ri).astype(BF16)
    klane_head = lax.broadcasted_iota(jnp.int32, (1, nh * dk), 1) // dk
    vlane_head = lax.broadcasted_iota(jnp.int32, (1, nh * dv), 1) // dv
    strow_head = lax.broadcasted_iota(jnp.int32, (nh * dv, 1), 0) // dv
    st_mask = strow_head == klane_head
    r4 = lax.broadcasted_iota(jnp.int32, (nh * c, c), 0) % c
    c4 = lax.broadcasted_iota(jnp.int32, (nh * c, c), 1)
    causal4 = c4 <= r4
    gi = lax.broadcasted_iota(jnp.int32, (nh * dv, nh * dv), 0) // dv
    gj = lax.broadcasted_iota(jnp.int32, (nh * dv, nh * dv), 1) // dv
    gmean = jnp.where(gi == gj, 1.0 / dv, 0.0).astype(BF16)
    wlr_hi, wlr_lo = _split_bf16(wlr_ref[...], 2)

    def one_seq(bb, rows):
        q = p_ref[bb, rows, 0:128].astype(F32) * (dk ** -0.5)
        k = p_ref[bb, rows, 128:256].astype(F32)
        vb = p_ref[bb, rows, 256:512]
        g = p_ref[bb, rows, 512:768].astype(F32)
        lr = lr_ref[bb, rows, :]
        lr_hi, lr_lo = _split_bf16(lr, 2)
        z = (jnp.dot(lr_hi, wlr_hi, preferred_element_type=F32)
             + jnp.dot(lr_hi, wlr_lo, preferred_element_type=F32)
             + jnp.dot(lr_lo, wlr_hi, preferred_element_type=F32)) + blr_ref[...]
        log_a = (jnp.minimum(z, 0.0) - jnp.log(1.0 + jnp.exp(-jnp.abs(z)))) * (1.0 / GLA_TAU)
        cumb = _dot_split_rhs(tri, log_a, 3)
        ends = [cumb[(j + 1) * c - 1:(j + 1) * c, :] for j in range(ncb)]
        starts = [jnp.zeros_like(ends[0])] + ends[:-1]
        cum = cumb - jnp.concatenate([jnp.broadcast_to(s0, (c, nh * dk)) for s0 in starts], axis=0)
        lasts = [e - s0 for e, s0 in zip(ends, starts)]
        cl = jnp.concatenate([jnp.broadcast_to(x, (c, nh * dk)) for x in lasts], axis=0)
        q_dec = q * jnp.exp(cum)
        k_inv = (k * jnp.exp(-cum)).astype(BF16)
        k_end = (k * jnp.exp(cl - cum)).astype(BF16)
        st = st_ref[bb]
        outs = []
        for j in range(ncb):
            sl = slice(j * c, (j + 1) * c)
            qd = q_dec[sl]
            qs = jnp.concatenate([jnp.where(klane_head == h, qd, 0.0) for h in range(nh)],
                                 axis=0).astype(BF16)
            att = jnp.where(causal4, _dot_nt(qs, k_inv[sl]), 0.0)
            r = jnp.dot(att.astype(BF16), vb[sl], preferred_element_type=F32)
            o = jnp.where(vlane_head == 0, r[0:c], 0.0)
            for h in range(1, nh):
                o = o + jnp.where(vlane_head == h, r[h * c:(h + 1) * c], 0.0)
            outs.append(o + _dot_nt(qd.astype(BF16), st.astype(BF16)))
            d_st = _dot_tn(vb[sl], k_end[sl])
            st = st * jnp.exp(lasts[j]) + jnp.where(st_mask, d_st, 0.0)
        st_ref[bb] = st
        o = jnp.concatenate(outs, axis=0)
        ms = _dot_split_lhs(o * o, gmean, 2)
        o = o * lax.rsqrt(ms + RMS_EPS) * ng_ref[...]
        o_ref[bb, rows, :] = (o * (g * _sigmoid(g))).astype(o_ref.dtype)

    return one_seq


def _ssd_setup(p_ref, dt_ref, cw_ref, cb_ref, alog_ref, dtb_ref, dsk_ref, ng_ref, o_ref, st_ref,
               halo_ref):
    c = SSD_CHUNK
    n_st = SSD_STATE
    halo = 2 * SUBLANES

    ri = lax.broadcasted_iota(jnp.int32, (c, c), 0)
    ci = lax.broadcasted_iota(jnp.int32, (c, c), 1)
    causal = ci <= ri
    tri = causal.astype(BF16)
    upper = (ri <= ci).astype(BF16)
    lane_head = lax.broadcasted_iota(jnp.int32, (1, W_MIX), 1) // SSD_HEADDIM
    lane_group = lane_head // (SSD_HEADS // SSD_GROUPS)
    eh = lax.broadcasted_iota(jnp.int32, (LANES, W_MIX), 0)
    el = lax.broadcasted_iota(jnp.int32, (LANES, W_MIX), 1) // SSD_HEADDIM
    expand = (eh == el).astype(BF16)
    row8 = lax.broadcasted_iota(jnp.int32, (8, 3 * W_MIX), 0)
    a_c = -jnp.exp(alog_ref[...])

    def one_chunk(n, bb):
        rows = slice(n * c, (n + 1) * c)
        cur = p_ref[bb, rows, 256:1024].astype(F32)
        before = halo_ref[bb] if n == 0 else p_ref[bb, n * c - halo:n * c, 256:1024]
        prev8 = before.astype(F32)[SUBLANES:]
        acc = cur * cw_ref[3:4, :] + cb_ref[...]
        for s in (1, 2, 3):
            sh = pltpu.roll(cur, s, axis=0)
            top = jnp.where(row8 < s, pltpu.roll(prev8, s, axis=0), sh[0:8])
            sh = jnp.concatenate([top, sh[8:]], axis=0)
            acc = acc + sh * cw_ref[3 - s:4 - s, :]
        xbc = acc * _sigmoid(acc)
        x = xbc[:, 0:256]
        bm = xbc[:, 256:512].astype(BF16)
        cm = xbc[:, 512:768].astype(BF16)

        dt_c = _softplus(dt_ref[bb, rows, :] + dtb_ref[...])
        da_c = dt_c * a_c
        cum_c = _dot_split_rhs(tri, da_c, 3)
        cum_r = _dot_split_lhs(da_c, upper, 3, dot=_dot_tn)
        both_x = _dot_split_lhs(jnp.concatenate([dt_c, cum_c], axis=0), expand, 3)
        dt_x = both_x[0:c]
        cum_x = both_x[c:2 * c]
        cl_x = cum_x[c - 1:c, :]
        x_dt = x * dt_x
        x_dt_b = x_dt.astype(BF16)
        xw_b = (x_dt * jnp.exp(cl_x - cum_x)).astype(BF16)

        y = x * dsk_ref[...]
        y_off = jnp.zeros((c, W_MIX), F32)
        for g in range(SSD_GROUPS):
            bg = bm[:, g * n_st:(g + 1) * n_st]
            cg = cm[:, g * n_st:(g + 1) * n_st]
            cb = _dot_nt(cg, bg)
            for r in range(SSD_HEADS // SSD_GROUPS):
                h = g * (SSD_HEADS // SSD_GROUPS) + r
                diff = cum_c[:, h:h + 1] - cum_r[h:h + 1, :]
                dec = jnp.exp(jnp.where(causal, diff, -jnp.inf))
                m = (cb * dec).astype(BF16)
                yh = jnp.dot(m, x_dt_b, preferred_element_type=F32)
                y = y + jnp.where(lane_head == h, yh, 0.0)
            st = st_ref[bb, g]
            y_off = y_off + jnp.where(lane_group == g,
                                      jnp.dot(cg, st.astype(BF16), preferred_element_type=F32), 0.0)
            d_st = _dot_tn(bg, xw_b)
            st_ref[bb, g] = st * jnp.exp(cl_x) + jnp.where(lane_group == g, d_st, 0.0)
        y = y + y_off * jnp.exp(cum_x)
        zg = p_ref[bb, rows, 0:256].astype(F32)
        y = y * (zg * _sigmoid(zg))
        outs = []
        for g in range(SSD_GROUPS):
            yg = y[:, g * 128:(g + 1) * 128]
            ms = jnp.mean(yg * yg, axis=-1, keepdims=True)
            outs.append(yg * lax.rsqrt(ms + RMS_EPS))
        o_ref[bb, rows, :] = (jnp.concatenate(outs, axis=-1) * ng_ref[...]).astype(o_ref.dtype)

    return one_chunk


def _recurrent_kernel(pc_ref, ccw_ref, pg_ref, lr_ref, wlr_ref, blr_ref, gng_ref,
                      ps_ref, dt_ref, cw_ref, cb_ref, alog_ref, dtb_ref, dsk_ref, sng_ref,
                      oc_ref, og_ref, os_ref, gst_ref, sst_ref, halo_ref, chalo_ref):
    @pl.when(pl.program_id(1) == 0)
    def _():
        gst_ref[...] = jnp.zeros_like(gst_ref)
        sst_ref[...] = jnp.zeros_like(sst_ref)
        halo_ref[...] = jnp.zeros_like(halo_ref)
        chalo_ref[...] = jnp.zeros_like(chalo_ref)

    gla_rows = _gla_setup(pg_ref, lr_ref, wlr_ref, blr_ref, gng_ref, og_ref, gst_ref)
    ssd_chunk = _ssd_setup(ps_ref, dt_ref, cw_ref, cb_ref, alog_ref, dtb_ref, dsk_ref, sng_ref,
                           os_ref, sst_ref, halo_ref)
    row8 = lax.broadcasted_iota(jnp.int32, (SUBLANES, W_MIX), 0)
    for bb in range(pg_ref.shape[0]):
        gla_rows(bb, slice(0, GLA_ROWS))
        for n in range(GLA_ROWS // SSD_CHUNK):
            ssd_chunk(n, bb)
        halo_ref[bb] = ps_ref[bb, GLA_ROWS - halo_ref.shape[1]:GLA_ROWS, 256:1024]

        u = pc_ref[bb, :, 0:W_MIX].astype(F32)
        gb = pc_ref[bb, :, W_MIX:2 * W_MIX].astype(F32)
        gc = pc_ref[bb, :, 2 * W_MIX:3 * W_MIX].astype(F32)
        cu = gc * u
        prev8 = chalo_ref[bb]
        acc = cu * ccw_ref[2:3, :]
        for s in (1, 2):
            sh = pltpu.roll(cu, s, axis=0)
            top = jnp.where(row8 < s, pltpu.roll(prev8, s, axis=0), sh[0:SUBLANES])
            acc = acc + jnp.concatenate([top, sh[SUBLANES:]], axis=0) * ccw_ref[2 - s:3 - s, :]
        oc_ref[bb] = (gb * acc).astype(oc_ref.dtype)
        chalo_ref[bb] = cu[GLA_ROWS - SUBLANES:GLA_ROWS]


def _recurrent_mixers(pc, sconv_w, pg, plr, w_lr_pad, b_lr, gla_norm_g4,
                      ps, pdt, conv_w, conv_b, a_log_c, dt_bias_c, d_x, ssd_norm_g):
    b, s, _ = pg.shape
    nb = REC_SEQS
    rb = GLA_ROWS
    seq = lambda i, j: (i, j, 0)
    full2 = lambda i, j: (0, 0)
    return pl.pallas_call(
        _recurrent_kernel,
        grid=(b // nb, s // rb),
        in_specs=[pl.BlockSpec((nb, rb, pc.shape[2]), seq),
                  pl.BlockSpec((3, W_MIX), full2),
                  pl.BlockSpec((nb, rb, pg.shape[2]), seq),
                  pl.BlockSpec((nb, rb, LANES), seq),
                  pl.BlockSpec((LANES, LANES), full2),
                  pl.BlockSpec((1, LANES), full2),
                  pl.BlockSpec((1, W_MIX), full2),
                  pl.BlockSpec((nb, rb, ps.shape[2]), seq),
                  pl.BlockSpec((nb, rb, LANES), seq),
                  pl.BlockSpec((SSD_CONV_K, 3 * W_MIX), full2),
                  pl.BlockSpec((1, 3 * W_MIX), full2),
                  pl.BlockSpec((1, LANES), full2),
                  pl.BlockSpec((1, LANES), full2),
                  pl.BlockSpec((1, W_MIX), full2),
                  pl.BlockSpec((1, W_MIX), full2)],
        out_specs=[pl.BlockSpec((nb, rb, W_MIX), seq)] * 3,
        out_shape=[jax.ShapeDtypeStruct((b, s, W_MIX), BF16)] * 3,
        scratch_shapes=[pltpu.VMEM((nb, GLA_HEADS * GLA_DV, GLA_HEADS * GLA_DK), F32),
                        pltpu.VMEM((nb, SSD_GROUPS, SSD_STATE, W_MIX), F32),
                        pltpu.VMEM((nb, 2 * SUBLANES, 3 * W_MIX), BF16),
                        pltpu.VMEM((nb, SUBLANES, W_MIX), F32)],
        compiler_params=_cparams(("parallel", "arbitrary")),
        name="conv_gla_ssd_mixers",
    )(pc, sconv_w, pg, plr, w_lr_pad, b_lr, gla_norm_g4, ps, pdt, conv_w, conv_b, a_log_c,
      dt_bias_c, d_x, ssd_norm_g)


DIFF_TQ = 256
DIFF_TK = 256
LOG2E = 1.4426950408889634
DIFF_VPAD = DIFF_DV + 16
DIFF_SEQS = 4


def _diff_kernel(q_ref, k_ref, v_ref, lam_ref, ng_ref, o_ref,
                 vt_ref, qs_ref, st_ref, m_ref, acc_ref, *, lam_init):
    tq, tk = DIFF_TQ, DIFF_TK
    nh, dv = DIFF_HEADS, DIFF_DV
    nhc = 2 * nh
    s_len = k_ref.shape[1]
    i = pl.program_id(1)
    seqs = range(q_ref.shape[0])

    @pl.when(i == 0)
    def _():
        for bb in seqs:
            for cblk in range(s_len // tk):
                cols = slice(cblk * tk, (cblk + 1) * tk)
                vt = v_ref[bb, cols, :].astype(F32).T.astype(BF16)
                for h in range(nh):
                    vt_ref[bb, h, 0:dv, cols] = vt[h * dv:(h + 1) * dv]
        vt_ref[:, :, dv:, :] = jnp.ones((len(seqs), nh, DIFF_VPAD - dv, s_len), BF16)

    qlane = lax.broadcasted_iota(jnp.int32, (1, W_MIX), 1) // DIFF_DQK
    for bb in seqs:
        q = q_ref[bb].astype(F32) * (DIFF_DQK ** -0.5 * LOG2E)
        for hc in range(nhc):
            qs_ref[bb, hc * tq:(hc + 1) * tq, :] = jnp.where(qlane == hc, q, 0.0).astype(BF16)
    m_ref[...] = jnp.full_like(m_ref, -jnp.inf)
    acc_ref[...] = jnp.zeros_like(acc_ref)
    krow = lax.broadcasted_iota(jnp.int32, (tk, nhc * tq), 0)
    qcol = lax.broadcasted_iota(jnp.int32, (tk, nhc * tq), 1) % tq
    diag_ok = krow <= qcol

    def scores(j, slot):
        k0 = pl.multiple_of(j * tk, tk)
        for bb in seqs:
            st_ref[bb, slot] = _dot_nt(k_ref[bb, pl.ds(k0, tk), :], qs_ref[bb])

    def softmax_pv(j, slot, masked):
        for bb in seqs:
            softmax_pv_seq(bb, j, slot, masked)

    def softmax_pv_seq(bb, j, slot, masked):
        k0 = pl.multiple_of(j * tk, tk)
        st = st_ref[bb, slot]
        if masked:
            st = jnp.where(diag_ok, st, -jnp.inf)
        m_prev = m_ref[bb]
        m_new = jnp.maximum(m_prev, jnp.max(st, axis=0, keepdims=True))
        alpha = jnp.exp2(m_prev - m_new)
        p = jnp.exp2(st - m_new)
        m_ref[bb] = m_new
        pb = p.astype(BF16)
        for hc in range(nhc):
            h = hc // 2
            lanes = slice(hc * tq, (hc + 1) * tq)
            pv = jnp.dot(vt_ref[bb, h, :, pl.ds(k0, tk)], pb[:, lanes],
                         preferred_element_type=F32)
            acc_ref[bb, hc] = acc_ref[bb, hc] * alpha[:, lanes] + pv

    scores(0, 0)
    n_pairs = i // 2

    def pair_step(u, carry):
        scores(2 * u + 1, 1)
        softmax_pv(2 * u, 0, False)
        scores(2 * u + 2, 0)
        softmax_pv(2 * u + 1, 1, False)
        return carry

    lax.fori_loop(0, n_pairs, pair_step, 0)

    @pl.when(i % 2 == 0)
    def _():
        softmax_pv(i, 0, True)

    @pl.when(i % 2 == 1)
    def _():
        scores(i, 1)
        softmax_pv(i - 1, 0, False)
        softmax_pv(i, 1, True)

    lam = (jnp.exp(jnp.sum(lam_ref[0:1, :] * lam_ref[1:2, :], axis=-1, keepdims=True))
           - jnp.exp(jnp.sum(lam_ref[2:3, :] * lam_ref[3:4, :], axis=-1, keepdims=True))
           + lam_init)
    for bb in seqs:
        heads = []
        for h in range(nh):
            o1 = acc_ref[bb, 2 * h, 0:dv] / acc_ref[bb, 2 * h, dv:dv + 1]
            o2 = acc_ref[bb, 2 * h + 1, 0:dv] / acc_ref[bb, 2 * h + 1, dv:dv + 1]
            oh = o1 - lam * o2
            ms = jnp.mean(oh * oh, axis=0, keepdims=True)
            heads.append(oh * lax.rsqrt(ms + RMS_EPS))
        o = jnp.concatenate(heads, axis=0).T
        o_ref[bb] = (o * ng_ref[...] * (1.0 - lam_init)).astype(o_ref.dtype)


def _diff_mixer(pd, lam_vecs, norm_g4, lam_init):
    b, s, _ = pd.shape
    tq = DIFF_TQ
    nb = DIFF_SEQS
    return pl.pallas_call(
        functools.partial(_diff_kernel, lam_init=lam_init),
        grid=(b // nb, s // tq),
        in_specs=[pl.BlockSpec((nb, tq, W_MIX), lambda bi, i: (bi, i, 0)),
                  pl.BlockSpec((nb, s, W_MIX), lambda bi, i: (bi, 0, 1)),
                  pl.BlockSpec((nb, s, W_MIX), lambda bi, i: (bi, 0, 2)),
                  pl.BlockSpec((4, LANES), lambda bi, i: (0, 0)),
                  pl.BlockSpec((1, W_MIX), lambda bi, i: (0, 0))],
        out_specs=pl.BlockSpec((nb, tq, W_MIX), lambda bi, i: (bi, i, 0)),
        out_shape=jax.ShapeDtypeStruct((b, s, W_MIX), BF16),
        scratch_shapes=[pltpu.VMEM((nb, DIFF_HEADS, DIFF_VPAD, s), BF16),
                        pltpu.VMEM((nb, 2 * DIFF_HEADS * tq, W_MIX), BF16),
                        pltpu.VMEM((nb, 2, DIFF_TK, 2 * DIFF_HEADS * tq), F32),
                        pltpu.VMEM((nb, 1, 2 * DIFF_HEADS * tq), F32),
                        pltpu.VMEM((nb, 2 * DIFF_HEADS, DIFF_VPAD, tq), F32)],
        compiler_params=_cparams(("parallel", "arbitrary")),
        name="diff_attn",
    )(pd, pd, pd, lam_vecs, norm_g4)


def _oproj_kernel(yc_ref, yg_ref, ys_ref, yd_ref, x_ref, wo_ref, g_ref, b_ref, wr_ref,
                  xo_ref, xp_ref, route_ref, cnt_ref):
    @pl.when(pl.program_id(0) == 0)
    def _():
        cnt_ref[...] = jnp.zeros_like(cnt_ref)

    part = x_ref.shape[0] // OPROJ_PARTS
    hits_sum = jnp.zeros(cnt_ref.shape, F32)
    for r in range(OPROJ_PARTS):
        rows = slice(r * part, (r + 1) * part)
        mix = jnp.concatenate([yc_ref[rows, :], yg_ref[rows, :], ys_ref[rows, :], yd_ref[rows, :]],
                              axis=-1)
        h = ALPHA * x_ref[rows, :] + jnp.dot(mix, wo_ref[...], preferred_element_type=F32)
        xn = _layer_norm(h, g_ref[...], b_ref[...])
        xo_ref[rows, :] = xn
        xp_ref[rows, :] = _pack_halves(xn)

        xn_hi, xn_lo = _split_bf16(xn, 2)
        both = _dot(xn_hi, wr_ref[...])
        logits = both[:, 0:LANES] + both[:, LANES:2 * LANES] + _dot(xn_lo, wr_ref[:, 0:LANES])
        lane = lax.broadcasted_iota(jnp.int32, logits.shape, 1).astype(F32)
        neg = -jnp.inf
        big = float(LANES)
        lg = jnp.where(lane < N_GROUPS, logits, neg)
        mg = jnp.max(lg, axis=-1, keepdims=True)
        sg = jnp.sum(jnp.exp(lg - mg), axis=-1, keepdims=True)
        grp = jnp.min(jnp.where(lg == mg, lane, big), axis=-1, keepdims=True)
        p_grp = 1.0 / sg
        lo = N_GROUPS + EXPERTS_PER_GROUP * grp
        in_g = jnp.logical_and(lane >= lo, lane < lo + EXPERTS_PER_GROUP)
        le = jnp.where(in_g, logits, neg)
        me = jnp.max(le, axis=-1, keepdims=True)
        ee = jnp.exp(le - me)
        pe = ee / jnp.sum(ee, axis=-1, keepdims=True)
        pe = jnp.where(in_g, pe, -1.0)
        p1 = jnp.max(pe, axis=-1, keepdims=True)
        i1 = jnp.min(jnp.where(pe == p1, lane, big), axis=-1, keepdims=True)
        pe2 = jnp.where(lane == i1, -1.0, pe)
        p2 = jnp.max(pe2, axis=-1, keepdims=True)
        i2 = jnp.min(jnp.where(pe2 == p2, lane, big), axis=-1, keepdims=True)
        den = p1 + p2
        g1 = p_grp * p1 / den
        g2 = p_grp * p2 / den
        e1 = i1 - N_GROUPS
        e2 = i2 - N_GROUPS
        route_ref[rows, :] = jnp.where(lane == 0, e1, jnp.where(lane == 1, e2, jnp.where(
            lane == 2, g1, jnp.where(lane == 3, g2, 0.0))))
        hits = jnp.where(lane == e1, 1.0, 0.0) + jnp.where(lane == e2, 1.0, 0.0)
        hits_sum = hits_sum + jnp.sum(hits, axis=0, keepdims=True)
    cnt_ref[...] += hits_sum


def _out_proj(ys, x2d, w_o, layer, ln_g, ln_b, w_route):
    t = x2d.shape[0]
    tm = 1024
    row = lambda i: (i, 0)
    full = lambda i: (0, 0)
    return pl.pallas_call(
        _oproj_kernel,
        grid=(t // tm,),
        in_specs=[pl.BlockSpec((tm, W_MIX), row)] * 4 + [
            pl.BlockSpec((tm, D_MODEL), row),
            pl.BlockSpec((None, D_MODEL, D_MODEL), lambda i: (layer, 0, 0)),
            pl.BlockSpec((1, D_MODEL), full),
            pl.BlockSpec((1, D_MODEL), full),
            pl.BlockSpec((D_MODEL, 2 * LANES), full)],
        out_specs=[pl.BlockSpec((tm, D_MODEL), row), pl.BlockSpec((tm, D_MODEL // 2), row),
                   pl.BlockSpec((tm, LANES), row), pl.BlockSpec((1, LANES), full)],
        out_shape=[jax.ShapeDtypeStruct((t, D_MODEL), F32),
                   jax.ShapeDtypeStruct((t, D_MODEL // 2), U32),
                   jax.ShapeDtypeStruct((t, LANES), F32),
                   jax.ShapeDtypeStruct((1, LANES), F32)],
        compiler_params=_cparams(("arbitrary",)),
        name="out_proj_ln_router",
    )(*ys, x2d, w_o, ln_g, ln_b, w_route)


PLAN_TILE = 512


def _plan_kernel(route_ref, cnt_ref, dest_ref, meta_ref, carry_ref, pstart_ref):
    tm = route_ref.shape[0]
    lane = lax.broadcasted_iota(jnp.int32, (1, LANES), 1).astype(F32)

    @pl.when(pl.program_id(0) == 0)
    def _():
        cnt = cnt_ref[...]
        padded = jnp.ceil(cnt * (1.0 / MOE_BLK)) * MOE_BLK
        li = lax.broadcasted_iota(jnp.int32, (LANES, LANES), 0)
        lj = lax.broadcasted_iota(jnp.int32, (LANES, LANES), 1)
        before = (li < lj).astype(F32)
        pstart = jnp.dot(jnp.broadcast_to(padded, (8, LANES)), before, precision=HI,
                         preferred_element_type=F32)[0:1]
        pstart_ref[...] = pstart
        carry_ref[...] = jnp.zeros_like(carry_ref)
        meta_ref[...] = jnp.concatenate(
            [pstart + padded, pstart, cnt, jnp.zeros((5, LANES), F32)], axis=0)

    oh0 = jnp.where(lane == route_ref[:, 0:1], 1.0, 0.0)
    oh1 = jnp.where(lane == route_ref[:, 1:2], 1.0, 0.0)
    both = oh0 + oh1
    ri = lax.broadcasted_iota(jnp.int32, (tm, tm), 0)
    ci = lax.broadcasted_iota(jnp.int32, (tm, tm), 1)
    earlier = (ci < ri).astype(BF16)
    base = (jnp.dot(earlier, both.astype(BF16), preferred_element_type=F32)
            + carry_ref[...] + pstart_ref[...])
    d0 = jnp.sum(oh0 * base, axis=-1, keepdims=True)
    d1 = jnp.sum(oh1 * base, axis=-1, keepdims=True)
    dest = jnp.where(lane == 0, d0, jnp.where(lane == 1, d1, 0.0))
    dest_ref[...] = dest.T[0:2, :].astype(jnp.int32)
    carry_ref[...] += jnp.sum(both, axis=0, keepdims=True)


def _dispatch_plan(route, cnt, n_blocks):
    t = route.shape[0]
    tm = PLAN_TILE
    blk = MOE_BLK
    dest, meta = pl.pallas_call(
        _plan_kernel,
        grid=(t // tm,),
        in_specs=[pl.BlockSpec((tm, LANES), lambda i: (i, 0)),
                  pl.BlockSpec((1, LANES), lambda i: (0, 0))],
        out_specs=[pl.BlockSpec((2, tm), lambda i: (0, i)),
                   pl.BlockSpec((8, LANES), lambda i: (0, 0))],
        out_shape=[jax.ShapeDtypeStruct((2, t), jnp.int32),
                   jax.ShapeDtypeStruct((8, LANES), F32)],
        scratch_shapes=[pltpu.VMEM((1, LANES), F32), pltpu.VMEM((1, LANES), F32)],
        compiler_params=_cparams(("arbitrary",)),
        name="moe_plan",
    )(route, cnt)
    meta_i = meta[:, :N_EXPERTS].astype(jnp.int32)
    pad_end, pad_start, seg_end = meta_i[0], meta_i[1], meta_i[1] + meta_i[2]
    starts = jnp.arange(n_blocks, dtype=jnp.int32)[:, None] * blk
    member = jnp.logical_and(starts >= pad_start[None, :], starts < pad_end[None, :])
    expert_ids = jnp.arange(N_EXPERTS, dtype=jnp.int32)[None, :]
    block_e = jnp.where(jnp.any(member, axis=1), jnp.sum(jnp.where(member, expert_ids, 0), axis=1),
                        N_EXPERTS - 1)
    n_used = (pad_end[N_EXPERTS - 1] // blk).reshape(1)
    n_valid = jnp.clip(jnp.sum(jnp.where(member, seg_end[None, :], 0), axis=1) - starts[:, 0],
                       0, blk)
    return dest.reshape(2 * t), block_e, n_used, n_valid


def _ffn_kernel(be_ref, nu_ref, nv_ref, xs_ref, wg_ref, wu_ref, wd_ref, y_ref,
                wgb_ref, wub_ref, wdb_ref):
    i = pl.program_id(0)
    prev = be_ref[jnp.maximum(i - 1, 0)]

    @pl.when(jnp.logical_or(i == 0, be_ref[i] != prev))
    def _():
        wgb_ref[...] = wg_ref[...].astype(BF16)
        wub_ref[...] = wu_ref[...].astype(BF16)
        wdb_ref[...] = wd_ref[...].astype(BF16)

    @pl.when(i < nu_ref[0])
    def _():
        half = xs_ref.shape[0] // 2
        for r in range(2):
            rows = slice(r * half, (r + 1) * half)
            row = lax.broadcasted_iota(jnp.int32, (half, 1), 0) + r * half
            xp = jnp.where(row < nv_ref[i], xs_ref[rows, :], U32(0))
            x_hi, x_lo = _unpack_halves(xp)
            xb = jnp.concatenate([x_hi.astype(BF16), x_lo.astype(BF16)], axis=1)
            a = jnp.dot(xb, wgb_ref[...], preferred_element_type=F32)
            u = jnp.dot(xb, wub_ref[...], preferred_element_type=F32)
            h = (a * _sigmoid(a) * u).astype(BF16)
            y_ref[rows, :] = _pack_halves(jnp.dot(h, wdb_ref[...], preferred_element_type=F32))

    @pl.when(i >= nu_ref[0])
    def _():
        y_ref[...] = jnp.zeros_like(y_ref)


def _expert_ffn(xs, block_e, n_used, n_valid, w_gate, w_up, w_down, layer):
    n_slots = xs.shape[0]
    blk = MOE_BLK
    w_map = lambda i, be, nu, nv: (layer, be[i], 0, 0)
    grid_spec = pltpu.PrefetchScalarGridSpec(
        num_scalar_prefetch=3,
        grid=(n_slots // blk,),
        in_specs=[pl.BlockSpec((blk, D_MODEL // 2),
                               lambda i, be, nu, nv: (jnp.minimum(i, nu[0] - 1), 0)),
                  pl.BlockSpec((None, None, D_MODEL, D_EXPERT), w_map),
                  pl.BlockSpec((None, None, D_MODEL, D_EXPERT), w_map),
                  pl.BlockSpec((None, None, D_EXPERT, D_MODEL), w_map)],
        out_specs=pl.BlockSpec((blk, D_MODEL // 2), lambda i, be, nu, nv: (i, 0)),
        scratch_shapes=[pltpu.VMEM((D_MODEL, D_EXPERT), BF16),
                        pltpu.VMEM((D_MODEL, D_EXPERT), BF16),
                        pltpu.VMEM((D_EXPERT, D_MODEL), BF16)],
    )
    return pl.pallas_call(
        _ffn_kernel,
        grid_spec=grid_spec,
        out_shape=jax.ShapeDtypeStruct((n_slots, D_MODEL // 2), U32),
        compiler_params=_cparams(("arbitrary",)),
        name="expert_ffn",
    )(block_e, n_used, n_valid, xs, w_gate, w_up, w_down)


SC_CORES = 2
SC_SUBCORES = 16
SC_ROWS = 64


def _sc_gather_rows(table, idx):
    b = idx.shape[0]
    d = table.shape[1]
    per_w = b // (SC_CORES * SC_SUBCORES)
    mesh = plsc.VectorSubcoreMesh(core_axis_name="c", subcore_axis_name="s")

    n_chunks = per_w // SC_ROWS

    @functools.partial(
        pl.kernel, mesh=mesh,
        out_type=jax.ShapeDtypeStruct((b, d), table.dtype),
        scratch_types=[pltpu.VMEM((SC_ROWS,), jnp.int32), pltpu.VMEM((SC_ROWS,), jnp.int32),
                       pltpu.VMEM((SC_ROWS, d), table.dtype),
                       pltpu.VMEM((SC_ROWS, d), table.dtype),
                       pltpu.SemaphoreType.DMA, pltpu.SemaphoreType.DMA,
                       pltpu.SemaphoreType.DMA, pltpu.SemaphoreType.DMA],
        name="sc_gather_rows",
    )
    def gather(table_hbm, idx_hbm, out_hbm, idx0, idx1, rows0, rows1, gs0, gs1, ws0, ws1):
        idx_v, rows_v, gsem, wsem = (idx0, idx1), (rows0, rows1), (gs0, gs1), (ws0, ws1)
        wid = lax.axis_index("s") * SC_CORES + lax.axis_index("c")
        base = wid * per_w

        def rows_of(c):
            return pl.ds(pl.multiple_of(base + c * SC_ROWS, SC_ROWS), SC_ROWS)

        def start_gather(c, s):
            pltpu.sync_copy(idx_hbm.at[rows_of(c)], idx_v[s])
            pltpu.async_copy(table_hbm.at[idx_v[s]], rows_v[s], gsem[s])

        def write_back(c, s):
            pltpu.make_async_copy(table_hbm.at[idx_v[s]], rows_v[s], gsem[s]).wait()
            pltpu.async_copy(rows_v[s], out_hbm.at[rows_of(c)], wsem[s]).wait()

        start_gather(0, 0)

        @pl.loop(0, n_chunks, step=2)
        def _(c):
            start_gather(c + 1, 1)
            write_back(c, 0)

            @pl.when(c + 2 < n_chunks)
            def _():
                start_gather(c + 2, 0)

            write_back(c + 1, 1)

    return gather(table, idx)


def _sc_scatter_rows(x2d, idx, n_slots):
    t, d = x2d.shape
    per_w = t // (SC_CORES * SC_SUBCORES)
    mesh = plsc.VectorSubcoreMesh(core_axis_name="c", subcore_axis_name="s")

    @functools.partial(
        pl.kernel, mesh=mesh,
        out_type=jax.ShapeDtypeStruct((n_slots, d), x2d.dtype),
        scratch_types=[pltpu.VMEM((SC_ROWS,), jnp.int32), pltpu.VMEM((SC_ROWS,), jnp.int32),
                       pltpu.VMEM((SC_ROWS, d), x2d.dtype),
                       pltpu.SemaphoreType.DMA, pltpu.SemaphoreType.DMA],
        name="sc_scatter_rows",
    )
    def scatter(x_hbm, idx_hbm, out_hbm, idx0, idx1, rows_v, s0, s1):
        wid = lax.axis_index("s") * SC_CORES + lax.axis_index("c")
        base = wid * per_w

        @pl.loop(0, per_w // SC_ROWS)
        def _(c):
            off = pl.multiple_of(base + c * SC_ROWS, SC_ROWS)
            pltpu.sync_copy(x_hbm.at[pl.ds(off, SC_ROWS)], rows_v)
            pltpu.sync_copy(idx_hbm.at[pl.ds(off, SC_ROWS)], idx0)
            pltpu.sync_copy(idx_hbm.at[pl.ds(t + off, SC_ROWS)], idx1)
            cp0 = pltpu.async_copy(rows_v, out_hbm.at[idx0], s0)
            cp1 = pltpu.async_copy(rows_v, out_hbm.at[idx1], s1)
            cp0.wait()
            cp1.wait()

    return scatter(x2d, idx)


def _combine_dense_kernel(route_ref, x_ref, y0_ref, y1_ref, g_ref, b_ref, o_ref):
    y0 = jnp.concatenate(_unpack_halves(y0_ref[...]), axis=1)
    y1 = jnp.concatenate(_unpack_halves(y1_ref[...]), axis=1)
    moe = route_ref[:, 2:3] * y0 + route_ref[:, 3:4] * y1
    h = ALPHA * x_ref[...] + moe
    o_ref[...] = _layer_norm(h, g_ref[...], b_ref[...])


def _combine_dense(route, x2d, yg, ln_g, ln_b):
    t = x2d.shape[0]
    tm = 1024
    nt = t // tm
    row = lambda i: (i, 0)
    full = lambda i: (0, 0)
    return pl.pallas_call(
        _combine_dense_kernel,
        grid=(nt,),
        in_specs=[pl.BlockSpec((tm, LANES), row),
                  pl.BlockSpec((tm, D_MODEL), row),
                  pl.BlockSpec((tm, D_MODEL // 2), row),
                  pl.BlockSpec((tm, D_MODEL // 2), lambda i: (i + nt, 0)),
                  pl.BlockSpec((1, D_MODEL), full),
                  pl.BlockSpec((1, D_MODEL), full)],
        out_specs=pl.BlockSpec((tm, D_MODEL), row),
        out_shape=jax.ShapeDtypeStruct((t, D_MODEL), F32),
        compiler_params=_cparams(("parallel",)),
        name="moe_combine_dense",
    )(route, x2d, yg, yg, ln_g, ln_b)


def _pad_cols(w, n):
    return jnp.pad(w, [(0, 0)] * (w.ndim - 1) + [(0, n - w.shape[-1])])


def kernel(x, w_in, conv_w, gla_w_lr, gla_b_lr, gla_norm_g, ssd_conv_w, ssd_conv_b, ssd_a_log,
           ssd_d, ssd_dt_bias, ssd_norm_g, diff_lq1, diff_lk1, diff_lq2, diff_lk2, diff_norm_g,
           w_o, ln1_g, ln1_b, router_g, router_e, w_gate, w_up, w_down, ln2_g, ln2_b):
    bsz, seq, d = x.shape
    t = bsz * seq
    n_assign = 2 * t
    n_blocks = (n_assign + N_EXPERTS * (MOE_BLK - 1)) // MOE_BLK + 1
    n_slots = n_blocks * MOE_BLK
    x2d = x.reshape(t, d)
    w_o_b = w_o.astype(BF16)
    for l in range(DEPTH):
        pc, pg, plr, ps, pdt, pd = [p.reshape(bsz, seq, -1) for p in _in_proj(x2d, w_in, l)]

        w_lr_pad = jnp.pad(gla_w_lr[l], ((0, LANES - GLA_RANK), (0, 0)))
        pad4 = lambda v: jnp.pad(v, (0, LANES - SSD_HEADS)).reshape(1, LANES)
        y_conv, y_gla, y_ssd = _recurrent_mixers(
            pc, conv_w[l], pg, plr, w_lr_pad, gla_b_lr[l].reshape(1, -1),
            jnp.tile(gla_norm_g[l], GLA_HEADS).reshape(1, -1),
            ps, pdt, ssd_conv_w[l], ssd_conv_b[l].reshape(1, -1),
            pad4(ssd_a_log[l]), pad4(ssd_dt_bias[l]),
            jnp.repeat(ssd_d[l], SSD_HEADDIM).reshape(1, -1), ssd_norm_g[l].reshape(1, -1))
        lam_vecs = jnp.pad(jnp.stack([diff_lq1[l], diff_lk1[l], diff_lq2[l], diff_lk2[l]]),
                           ((0, 0), (0, LANES - DIFF_DQK)))
        lam_init = 0.8 - 0.6 * math.exp(-0.3 * l)
        y_diff = _diff_mixer(pd, lam_vecs,
                             jnp.tile(diff_norm_g[l], DIFF_HEADS).reshape(1, -1), lam_init)

        w_route = _pad_cols(jnp.concatenate(
            [router_g[l], router_e[l].reshape(d, N_EXPERTS)], axis=1), LANES)
        w_route_hi = w_route.astype(BF16)
        w_route = jnp.concatenate(
            [w_route_hi, (w_route - w_route_hi.astype(F32)).astype(BF16)], axis=1)
        ys = [y.reshape(t, W_MIX) for y in (y_conv, y_gla, y_ssd, y_diff)]
        xn, xn_p, route, cnt = _out_proj(ys, x2d, w_o_b, l, ln1_g[l].reshape(1, -1),
                                         ln1_b[l].reshape(1, -1), w_route)

        idx, block_e, n_used, n_valid = _dispatch_plan(route, cnt, n_blocks)
        xs = _sc_scatter_rows(xn_p, idx, n_slots)
        y = _expert_ffn(xs, block_e, n_used, n_valid, w_gate, w_up, w_down, l)
        yg = _sc_gather_rows(y, idx)
        x2d = _combine_dense(route, xn, yg, ln2_g[l].reshape(1, -1), ln2_b[l].reshape(1, -1))
    return x2d.reshape(bsz, seq, d)
```

```python
import functools
import math

import jax
import jax.numpy as jnp
from jax import lax
from jax.experimental import pallas as pl
from jax.experimental.pallas import tpu as pltpu
from jax.experimental.pallas import tpu_sc as plsc

F32 = jnp.float32
BF16 = jnp.bfloat16
HI = lax.Precision.HIGHEST

D_MODEL = 1024
DEPTH = 2
W_MIX = 256
GLA_HEADS, GLA_DK, GLA_DV, GLA_RANK, GLA_TAU, GLA_CHUNK = 4, 32, 64, 16, 16.0, 64
GLA_ROWS = 256
REC_SEQS = 4
SSD_HEADS, SSD_GROUPS, SSD_HEADDIM, SSD_STATE, SSD_CONV_K, SSD_CHUNK = 4, 2, 64, 128, 4, 128
DIFF_HEADS, DIFF_DQK, DIFF_DV = 4, 32, 64
N_GROUPS, EXPERTS_PER_GROUP, N_EXPERTS, D_EXPERT = 4, 8, 32, 512
ALPHA = (2 * DEPTH) ** 0.25
LN_EPS = 1e-5
RMS_EPS = 1e-6

LANES = 128
SUBLANES = 8
PROJ_WIDTHS = (768, 768, 128, 1024, 128, 768)
PROJ_SRC_OFFSETS = (0, 768, 1536, 1552, 2576, 2580)
PROJ_SRC_WIDTHS = (768, 768, GLA_RANK, 1024, SSD_HEADS, 768)
PROJ_DTYPES = (BF16, BF16, F32, BF16, F32, BF16)
VMEM_LIMIT = 56 * 1024 * 1024

MOE_BLK = 512


def _cparams(sem):
    return pltpu.CompilerParams(dimension_semantics=sem, vmem_limit_bytes=VMEM_LIMIT)


def _sigmoid(x):
    return 1.0 / (1.0 + jnp.exp(-x))


def _softplus(x):
    return jnp.maximum(x, 0.0) + jnp.log(1.0 + jnp.exp(-jnp.abs(x)))


def _layer_norm(h, g, b):
    mu = jnp.mean(h, axis=-1, keepdims=True)
    d = h - mu
    var = jnp.mean(d * d, axis=-1, keepdims=True)
    return d * lax.rsqrt(var + LN_EPS) * g + b


def _dot_nt(a, b):
    return lax.dot_general(a, b, (((1,), (1,)), ((), ())), preferred_element_type=F32)


def _dot_tn(a, b, precision=None):
    return lax.dot_general(a, b, (((0,), (0,)), ((), ())), preferred_element_type=F32,
                           precision=precision)


def _split_bf16(x, parts):
    out = []
    for _ in range(parts - 1):
        hi = x.astype(BF16)
        out.append(hi)
        x = x - hi.astype(F32)
    out.append(x.astype(BF16))
    return out


def _dot(a, b):
    return jnp.dot(a, b, preferred_element_type=F32)


U32 = jnp.uint32


def _pack_halves(x):
    w = x.shape[1] // 2
    hi = lax.bitcast_convert_type(x[:, :w].astype(BF16).astype(F32), U32)
    lo = lax.bitcast_convert_type(x[:, w:].astype(BF16).astype(F32), U32)
    return hi | lax.shift_right_logical(lo, U32(16))


def _unpack_halves(p):
    hi = lax.bitcast_convert_type(p & U32(0xFFFF0000), F32)
    lo = lax.bitcast_convert_type(lax.shift_left(p, U32(16)), F32)
    return hi, lo


def _dot_split_lhs(a, b_exact, parts, dot=_dot):
    acc = None
    for term in _split_bf16(a, parts):
        d = dot(term, b_exact)
        acc = d if acc is None else acc + d
    return acc


def _dot_split_rhs(a_exact, b, parts):
    acc = None
    for term in _split_bf16(b, parts):
        d = jnp.dot(a_exact, term, preferred_element_type=F32)
        acc = d if acc is None else acc + d
    return acc


def _proj_kernel(x_ref, w_ref, *refs):
    o_refs, w_scr = refs[:-1], refs[-1]

    @pl.when(pl.program_id(0) == 0)
    def _():
        w_scr[...] = jnp.zeros_like(w_scr)
        dst = 0
        for src, n_src, n_dst in zip(PROJ_SRC_OFFSETS, PROJ_SRC_WIDTHS, PROJ_WIDTHS):
            w_scr[:, dst:dst + n_src] = w_ref[:, src:src + n_src].astype(BF16)
            dst += n_dst

    xb = x_ref[...].astype(BF16)
    off = 0
    for o_ref in o_refs:
        n = o_ref.shape[-1]
        o_ref[...] = jnp.dot(xb, w_scr[:, off:off + n],
                             preferred_element_type=F32).astype(o_ref.dtype)
        off += n


def _in_proj(x2d, w_in, layer):
    t = x2d.shape[0]
    tm = 1024
    return pl.pallas_call(
        _proj_kernel,
        grid=(t // tm,),
        in_specs=[pl.BlockSpec((tm, D_MODEL), lambda i: (i, 0)),
                  pl.BlockSpec((None, D_MODEL, w_in.shape[2]), lambda i: (layer, 0, 0),
                               pl.Buffered(1))],
        out_specs=[pl.BlockSpec((tm, n), lambda i: (i, 0)) for n in PROJ_WIDTHS],
        out_shape=[jax.ShapeDtypeStruct((t, n), dt) for n, dt in zip(PROJ_WIDTHS, PROJ_DTYPES)],
        scratch_shapes=[pltpu.VMEM((D_MODEL, sum(PROJ_WIDTHS)), BF16)],
        compiler_params=_cparams(("arbitrary",)),
        name="in_proj",
    )(x2d, w_in)


def _gla_setup(p_ref, lr_ref, wlr_ref, blr_ref, ng_ref, o_ref, st_ref):
    c = GLA_CHUNK
    nh, dk, dv = GLA_HEADS, GLA_DK, GLA_DV

    rb = GLA_ROWS
    ncb = rb // c
    ri = lax.broadcasted_iota(jnp.int32, (rb, rb), 0)
    ci = lax.broadcasted_iota(jnp.int32, (rb, rb), 1)
    tri = (ci <= ri).astype(BF16)
    klane_head = lax.broadcasted_iota(jnp.int32, (1, nh * dk), 1) // dk
    vlane_head = lax.broadcasted_iota(jnp.int32, (1, nh * dv), 1) // dv
    strow_head = lax.broadcasted_iota(jnp.int32, (nh * dv, 1), 0) // dv
    st_mask = strow_head == klane_head
    r4 = lax.broadcasted_iota(jnp.int32, (nh * c, c), 0) % c
    c4 = lax.broadcasted_iota(jnp.int32, (nh * c, c), 1)
    causal4 = c4 <= r4
    gi = lax.broadcasted_iota(jnp.int32, (nh * dv, nh * dv), 0) // dv
    gj = lax.broadcasted_iota(jnp.int32, (nh * dv, nh * dv), 1) // dv
    gmean = jnp.where(gi == gj, 1.0 / dv, 0.0).astype(BF16)
    wlr_hi, wlr_lo = _split_bf16(wlr_ref[...], 2)

    def one_seq(bb, rows):
        q = p_ref[bb, rows, 0:128].astype(F32) * (dk ** -0.5)
        k = p_ref[bb, rows, 128:256].astype(F32)
        vb = p_ref[bb, rows, 256:512]
        g = p_ref[bb, rows, 512:768].astype(F32)
        lr = lr_ref[bb, rows, :]
        lr_hi, lr_lo = _split_bf16(lr, 2)
        z = (jnp.dot(lr_hi, wlr_hi, preferred_element_type=F32)
             + jnp.dot(lr_hi, wlr_lo, preferred_element_type=F32)
             + jnp.dot(lr_lo, wlr_hi, preferred_element_type=F32)) + blr_ref[...]
        log_a = (jnp.minimum(z, 0.0) - jnp.log(1.0 + jnp.exp(-jnp.abs(z)))) * (1.0 / GLA_TAU)
        cumb = _dot_split_rhs(tri, log_a, 3)
        ends = [cumb[(j + 1) * c - 1:(j + 1) * c, :] for j in range(ncb)]
        starts = [jnp.zeros_like(ends[0])] + ends[:-1]
        cum = cumb - jnp.concatenate([jnp.broadcast_to(s0, (c, nh * dk)) for s0 in starts], axis=0)
        lasts = [e - s0 for e, s0 in zip(ends, starts)]
        cl = jnp.concatenate([jnp.broadcast_to(x, (c, nh * dk)) for x in lasts], axis=0)
        q_dec = q * jnp.exp(cum)
        k_inv = (k * jnp.exp(-cum)).astype(BF16)
        k_end = (k * jnp.exp(cl - cum)).astype(BF16)
        st = st_ref[bb]
        outs = []
        for j in range(ncb):
            sl = slice(j * c, (j + 1) * c)
            qd = q_dec[sl]
            qs = jnp.concatenate([jnp.where(klane_head == h, qd, 0.0) for h in range(nh)],
                                 axis=0).astype(BF16)
            att = jnp.where(causal4, _dot_nt(qs, k_inv[sl]), 0.0)
            r = jnp.dot(att.astype(BF16), vb[sl], preferred_element_type=F32)
            o = jnp.where(vlane_head == 0, r[0:c], 0.0)
            for h in range(1, nh):
                o = o + jnp.where(vlane_head == h, r[h * c:(h + 1) * c], 0.0)
            outs.append(o + _dot_nt(qd.astype(BF16), st.astype(BF16)))
            d_st = _dot_tn(vb[sl], k_end[sl])
            st = st * jnp.exp(lasts[j]) + jnp.where(st_mask, d_st, 0.0)
        st_ref[bb] = st
        o = jnp.concatenate(outs, axis=0)
        ms = _dot_split_lhs(o * o, gmean, 2)
        o = o * lax.rsqrt(ms + RMS_EPS) * ng_ref[...]
        o_ref[bb, rows, :] = (o * (g * _sigmoid(g))).astype(o_ref.dtype)

    return one_seq


def _ssd_setup(p_ref, dt_ref, cw_ref, cb_ref, alog_ref, dtb_ref, dsk_ref, ng_ref, o_ref, st_ref,
               halo_ref):
    c = SSD_CHUNK
    n_st = SSD_STATE
    halo = 2 * SUBLANES

    ri = lax.broadcasted_iota(jnp.int32, (c, c), 0)
    ci = lax.broadcasted_iota(jnp.int32, (c, c), 1)
    causal = ci <= ri
    tri = causal.astype(BF16)
    upper = (ri <= ci).astype(BF16)
    lane_head = lax.broadcasted_iota(jnp.int32, (1, W_MIX), 1) // SSD_HEADDIM
    lane_group = lane_head // (SSD_HEADS // SSD_GROUPS)
    eh = lax.broadcasted_iota(jnp.int32, (LANES, W_MIX), 0)
    el = lax.broadcasted_iota(jnp.int32, (LANES, W_MIX), 1) // SSD_HEADDIM
    expand = (eh == el).astype(BF16)
    row8 = lax.broadcasted_iota(jnp.int32, (8, 3 * W_MIX), 0)
    a_c = -jnp.exp(alog_ref[...])

    def one_chunk(n, bb):
        rows = slice(n * c, (n + 1) * c)
        cur = p_ref[bb, rows, 256:1024].astype(F32)
        before = halo_ref[bb] if n == 0 else p_ref[bb, n * c - halo:n * c, 256:1024]
        prev8 = before.astype(F32)[SUBLANES:]
        acc = cur * cw_ref[3:4, :] + cb_ref[...]
        for s in (1, 2, 3):
            sh = pltpu.roll(cur, s, axis=0)
            top = jnp.where(row8 < s, pltpu.roll(prev8, s, axis=0), sh[0:8])
            sh = jnp.concatenate([top, sh[8:]], axis=0)
            acc = acc + sh * cw_ref[3 - s:4 - s, :]
        xbc = acc * _sigmoid(acc)
        x = xbc[:, 0:256]
        bm = xbc[:, 256:512].astype(BF16)
        cm = xbc[:, 512:768].astype(BF16)

        dt_c = _softplus(dt_ref[bb, rows, :] + dtb_ref[...])
        da_c = dt_c * a_c
        cum_c = _dot_split_rhs(tri, da_c, 3)
        cum_r = _dot_split_lhs(da_c, upper, 3, dot=_dot_tn)
        both_x = _dot_split_lhs(jnp.concatenate([dt_c, cum_c], axis=0), expand, 3)
        dt_x = both_x[0:c]
        cum_x = both_x[c:2 * c]
        cl_x = cum_x[c - 1:c, :]
        x_dt = x * dt_x
        x_dt_b = x_dt.astype(BF16)
        xw_b = (x_dt * jnp.exp(cl_x - cum_x)).astype(BF16)

        y = x * dsk_ref[...]
        y_off = jnp.zeros((c, W_MIX), F32)
        for g in range(SSD_GROUPS):
            bg = bm[:, g * n_st:(g + 1) * n_st]
            cg = cm[:, g * n_st:(g + 1) * n_st]
            cb = _dot_nt(cg, bg)
            for r in range(SSD_HEADS // SSD_GROUPS):
                h = g * (SSD_HEADS // SSD_GROUPS) + r
                diff = cum_c[:, h:h + 1] - cum_r[h:h + 1, :]
                dec = jnp.exp(jnp.where(causal, diff, -jnp.inf))
                m = (cb * dec).astype(BF16)
                yh = jnp.dot(m, x_dt_b, preferred_element_type=F32)
                y = y + jnp.where(lane_head == h, yh, 0.0)
            st = st_ref[bb, g]
            y_off = y_off + jnp.where(lane_group == g,
                                      jnp.dot(cg, st.astype(BF16), preferred_element_type=F32), 0.0)
            d_st = _dot_tn(bg, xw_b)
            st_ref[bb, g] = st * jnp.exp(cl_x) + jnp.where(lane_group == g, d_st, 0.0)
        y = y + y_off * jnp.exp(cum_x)
        zg = p_ref[bb, rows, 0:256].astype(F32)
        y = y * (zg * _sigmoid(zg))
        outs = []
        for g in range(SSD_GROUPS):
            yg = y[:, g * 128:(g + 1) * 128]
            ms = jnp.mean(yg * yg, axis=-1, keepdims=True)
            outs.append(yg * lax.rsqrt(ms + RMS_EPS))
        o_ref[bb, rows, :] = (jnp.concatenate(outs, axis=-1) * ng_ref[...]).astype(o_ref.dtype)

    return one_chunk


def _recurrent_kernel(pc_ref, ccw_ref, pg_ref, lr_ref, wlr_ref, blr_ref, gng_ref,
                      ps_ref, dt_ref, cw_ref, cb_ref, alog_ref, dtb_ref, dsk_ref, sng_ref,
                      oc_ref, og_ref, os_ref, gst_ref, sst_ref, halo_ref, chalo_ref):
    @pl.when(pl.program_id(1) == 0)
    def _():
        gst_ref[...] = jnp.zeros_like(gst_ref)
        sst_ref[...] = jnp.zeros_like(sst_ref)
        halo_ref[...] = jnp.zeros_like(halo_ref)
        chalo_ref[...] = jnp.zeros_like(chalo_ref)

    gla_rows = _gla_setup(pg_ref, lr_ref, wlr_ref, blr_ref, gng_ref, og_ref, gst_ref)
    ssd_chunk = _ssd_setup(ps_ref, dt_ref, cw_ref, cb_ref, alog_ref, dtb_ref, dsk_ref, sng_ref,
                           os_ref, sst_ref, halo_ref)
    row8 = lax.broadcasted_iota(jnp.int32, (SUBLANES, W_MIX), 0)
    for bb in range(pg_ref.shape[0]):
        gla_rows(bb, slice(0, GLA_ROWS))
        for n in range(GLA_ROWS // SSD_CHUNK):
            ssd_chunk(n, bb)
        halo_ref[bb] = ps_ref[bb, GLA_ROWS - halo_ref.shape[1]:GLA_ROWS, 256:1024]

        u = pc_ref[bb, :, 0:W_MIX].astype(F32)
        gb = pc_ref[bb, :, W_MIX:2 * W_MIX].astype(F32)
        gc = pc_ref[bb, :, 2 * W_MIX:3 * W_MIX].astype(F32)
        cu = gc * u
        prev8 = chalo_ref[bb]
        acc = cu * ccw_ref[2:3, :]
        for s in (1, 2):
            sh = pltpu.roll(cu, s, axis=0)
            top = jnp.where(row8 < s, pltpu.roll(prev8, s, axis=0), sh[0:SUBLANES])
            acc = acc + jnp.concatenate([top, sh[SUBLANES:]], axis=0) * ccw_ref[2 - s:3 - s, :]
        oc_ref[bb] = (gb * acc).astype(oc_ref.dtype)
        chalo_ref[bb] = cu[GLA_ROWS - SUBLANES:GLA_ROWS]


def _recurrent_mixers(pc, sconv_w, pg, plr, w_lr_pad, b_lr, gla_norm_g4,
                      ps, pdt, conv_w, conv_b, a_log_c, dt_bias_c, d_x, ssd_norm_g):
    b, s, _ = pg.shape
    nb = REC_SEQS
    rb = GLA_ROWS
    seq = lambda i, j: (i, j, 0)
    full2 = lambda i, j: (0, 0)
    return pl.pallas_call(
        _recurrent_kernel,
        grid=(b // nb, s // rb),
        in_specs=[pl.BlockSpec((nb, rb, pc.shape[2]), seq),
                  pl.BlockSpec((3, W_MIX), full2),
                  pl.BlockSpec((nb, rb, pg.shape[2]), seq),
                  pl.BlockSpec((nb, rb, LANES), seq),
                  pl.BlockSpec((LANES, LANES), full2),
                  pl.BlockSpec((1, LANES), full2),
                  pl.BlockSpec((1, W_MIX), full2),
                  pl.BlockSpec((nb, rb, ps.shape[2]), seq),
                  pl.BlockSpec((nb, rb, LANES), seq),
                  pl.BlockSpec((SSD_CONV_K, 3 * W_MIX), full2),
                  pl.BlockSpec((1, 3 * W_MIX), full2),
                  pl.BlockSpec((1, LANES), full2),
                  pl.BlockSpec((1, LANES), full2),
                  pl.BlockSpec((1, W_MIX), full2),
                  pl.BlockSpec((1, W_MIX), full2)],
        out_specs=[pl.BlockSpec((nb, rb, W_MIX), seq)] * 3,
        out_shape=[jax.ShapeDtypeStruct((b, s, W_MIX), BF16)] * 3,
        scratch_shapes=[pltpu.VMEM((nb, GLA_HEADS * GLA_DV, GLA_HEADS * GLA_DK), F32),
                        pltpu.VMEM((nb, SSD_GROUPS, SSD_STATE, W_MIX), F32),
                        pltpu.VMEM((nb, 2 * SUBLANES, 3 * W_MIX), BF16),
                        pltpu.VMEM((nb, SUBLANES, W_MIX), F32)],
        compiler_params=_cparams(("parallel", "arbitrary")),
        name="conv_gla_ssd_mixers",
    )(pc, sconv_w, pg, plr, w_lr_pad, b_lr, gla_norm_g4, ps, pdt, conv_w, conv_b, a_log_c,
      dt_bias_c, d_x, ssd_norm_g)


DIFF_TQ = 256
DIFF_TK = 256
LOG2E = 1.4426950408889634
DIFF_VPAD = DIFF_DV + 16
DIFF_SEQS = 4


def _diff_kernel(q_ref, k_ref, v_ref, lam_ref, ng_ref, o_ref,
                 vt_ref, qs_ref, st_ref, m_ref, acc_ref, *, lam_init):
    tq, tk = DIFF_TQ, DIFF_TK
    nh, dv = DIFF_HEADS, DIFF_DV
    nhc = 2 * nh
    s_len = k_ref.shape[1]
    i = pl.program_id(1)
    seqs = range(q_ref.shape[0])

    @pl.when(i == 0)
    def _():
        for bb in seqs:
            for cblk in range(s_len // tk):
                cols = slice(cblk * tk, (cblk + 1) * tk)
                vt = v_ref[bb, cols, :].astype(F32).T.astype(BF16)
                for h in range(nh):
                    vt_ref[bb, h, 0:dv, cols] = vt[h * dv:(h + 1) * dv]
        vt_ref[:, :, dv:, :] = jnp.ones((len(seqs), nh, DIFF_VPAD - dv, s_len), BF16)

    qlane = lax.broadcasted_iota(jnp.int32, (1, W_MIX), 1) // DIFF_DQK
    for bb in seqs:
        q = q_ref[bb].astype(F32) * (DIFF_DQK ** -0.5 * LOG2E)
        for hc in range(nhc):
            qs_ref[bb, hc * tq:(hc + 1) * tq, :] = jnp.where(qlane == hc, q, 0.0).astype(BF16)
    m_ref[...] = jnp.full_like(m_ref, -jnp.inf)
    acc_ref[...] = jnp.zeros_like(acc_ref)
    krow = lax.broadcasted_iota(jnp.int32, (tk, nhc * tq), 0)
    qcol = lax.broadcasted_iota(jnp.int32, (tk, nhc * tq), 1) % tq
    diag_ok = krow <= qcol

    def scores(j, slot):
        k0 = pl.multiple_of(j * tk, tk)
        for bb in seqs:
            st_ref[bb, slot] = _dot_nt(k_ref[bb, pl.ds(k0, tk), :], qs_ref[bb])

    def softmax_pv(j, slot, masked):
        for bb in seqs:
            softmax_pv_seq(bb, j, slot, masked)

    def softmax_pv_seq(bb, j, slot, masked):
        k0 = pl.multiple_of(j * tk, tk)
        st = st_ref[bb, slot]
        if masked:
            st = jnp.where(diag_ok, st, -jnp.inf)
        m_prev = m_ref[bb]
        m_new = jnp.maximum(m_prev, jnp.max(st, axis=0, keepdims=True))
        alpha = jnp.exp2(m_prev - m_new)
        p = jnp.exp2(st - m_new)
        m_ref[bb] = m_new
        pb = p.astype(BF16)
        for hc in range(nhc):
            h = hc // 2
            lanes = slice(hc * tq, (hc + 1) * tq)
            pv = jnp.dot(vt_ref[bb, h, :, pl.ds(k0, tk)], pb[:, lanes],
                         preferred_element_type=F32)
            acc_ref[bb, hc] = acc_ref[bb, hc] * alpha[:, lanes] + pv

    scores(0, 0)
    n_pairs = i // 2

    def pair_step(u, carry):
        scores(2 * u + 1, 1)
        softmax_pv(2 * u, 0, False)
        scores(2 * u + 2, 0)
        softmax_pv(2 * u + 1, 1, False)
        return carry

    lax.fori_loop(0, n_pairs, pair_step, 0)

    @pl.when(i % 2 == 0)
    def _():
        softmax_pv(i, 0, True)

    @pl.when(i % 2 == 1)
    def _():
        scores(i, 1)
        softmax_pv(i - 1, 0, False)
        softmax_pv(i, 1, True)

    lam = (jnp.exp(jnp.sum(lam_ref[0:1, :] * lam_ref[1:2, :], axis=-1, keepdims=True))
           - jnp.exp(jnp.sum(lam_ref[2:3, :] * lam_ref[3:4, :], axis=-1, keepdims=True))
           + lam_init)
    for bb in seqs:
        heads = []
        for h in range(nh):
            o1 = acc_ref[bb, 2 * h, 0:dv] / acc_ref[bb, 2 * h, dv:dv + 1]
            o2 = acc_ref[bb, 2 * h + 1, 0:dv] / acc_ref[bb, 2 * h + 1, dv:dv + 1]
            oh = o1 - lam * o2
            ms = jnp.mean(oh * oh, axis=0, keepdims=True)
            heads.append(oh * lax.rsqrt(ms + RMS_EPS))
        o = jnp.concatenate(heads, axis=0).T
        o_ref[bb] = (o * ng_ref[...] * (1.0 - lam_init)).astype(o_ref.dtype)


def _diff_mixer(pd, lam_vecs, norm_g4, lam_init):
    b, s, _ = pd.shape
    tq = DIFF_TQ
    nb = DIFF_SEQS
    return pl.pallas_call(
        functools.partial(_diff_kernel, lam_init=lam_init),
        grid=(b // nb, s // tq),
        in_specs=[pl.BlockSpec((nb, tq, W_MIX), lambda bi, i: (bi, i, 0)),
                  pl.BlockSpec((nb, s, W_MIX), lambda bi, i: (bi, 0, 1)),
                  pl.BlockSpec((nb, s, W_MIX), lambda bi, i: (bi, 0, 2)),
                  pl.BlockSpec((4, LANES), lambda bi, i: (0, 0)),
                  pl.BlockSpec((1, W_MIX), lambda bi, i: (0, 0))],
        out_specs=pl.BlockSpec((nb, tq, W_MIX), lambda bi, i: (bi, i, 0)),
        out_shape=jax.ShapeDtypeStruct((b, s, W_MIX), BF16),
        scratch_shapes=[pltpu.VMEM((nb, DIFF_HEADS, DIFF_VPAD, s), BF16),
                        pltpu.VMEM((nb, 2 * DIFF_HEADS * tq, W_MIX), BF16),
                        pltpu.VMEM((nb, 2, DIFF_TK, 2 * DIFF_HEADS * tq), F32),
                        pltpu.VMEM((nb, 1, 2 * DIFF_HEADS * tq), F32),
                        pltpu.VMEM((nb, 2 * DIFF_HEADS, DIFF_VPAD, tq), F32)],
        compiler_params=_cparams(("parallel", "arbitrary")),
        name="diff_attn",
    )(pd, pd, pd, lam_vecs, norm_g4)


def _oproj_kernel(yc_ref, yg_ref, ys_ref, yd_ref, x_ref, wo_ref, g_ref, b_ref, wr_ref,
                  xo_ref, xp_ref, route_ref, cnt_ref):
    mix = jnp.concatenate([yc_ref[...], yg_ref[...], ys_ref[...], yd_ref[...]], axis=-1)
    h = ALPHA * x_ref[...] + jnp.dot(mix, wo_ref[...], preferred_element_type=F32)
    xn = _layer_norm(h, g_ref[...], b_ref[...])
    xo_ref[...] = xn
    xp_ref[...] = _pack_halves(xn)

    xn_hi, xn_lo = _split_bf16(xn, 2)
    both = _dot(xn_hi, wr_ref[...])
    logits = both[:, 0:LANES] + both[:, LANES:2 * LANES] + _dot(xn_lo, wr_ref[:, 0:LANES])
    lane = lax.broadcasted_iota(jnp.int32, logits.shape, 1).astype(F32)
    neg = -jnp.inf
    big = float(LANES)
    lg = jnp.where(lane < N_GROUPS, logits, neg)
    mg = jnp.max(lg, axis=-1, keepdims=True)
    sg = jnp.sum(jnp.exp(lg - mg), axis=-1, keepdims=True)
    grp = jnp.min(jnp.where(lg == mg, lane, big), axis=-1, keepdims=True)
    p_grp = 1.0 / sg
    lo = N_GROUPS + EXPERTS_PER_GROUP * grp
    in_g = jnp.logical_and(lane >= lo, lane < lo + EXPERTS_PER_GROUP)
    le = jnp.where(in_g, logits, neg)
    me = jnp.max(le, axis=-1, keepdims=True)
    ee = jnp.exp(le - me)
    pe = ee / jnp.sum(ee, axis=-1, keepdims=True)
    pe = jnp.where(in_g, pe, -1.0)
    p1 = jnp.max(pe, axis=-1, keepdims=True)
    i1 = jnp.min(jnp.where(pe == p1, lane, big), axis=-1, keepdims=True)
    pe2 = jnp.where(lane == i1, -1.0, pe)
    p2 = jnp.max(pe2, axis=-1, keepdims=True)
    i2 = jnp.min(jnp.where(pe2 == p2, lane, big), axis=-1, keepdims=True)
    den = p1 + p2
    g1 = p_grp * p1 / den
    g2 = p_grp * p2 / den
    e1 = i1 - N_GROUPS
    e2 = i2 - N_GROUPS
    route_ref[...] = jnp.where(lane == 0, e1, jnp.where(lane == 1, e2, jnp.where(
        lane == 2, g1, jnp.where(lane == 3, g2, 0.0))))

    @pl.when(pl.program_id(0) == 0)
    def _():
        cnt_ref[...] = jnp.zeros_like(cnt_ref)

    hits = jnp.where(lane == e1, 1.0, 0.0) + jnp.where(lane == e2, 1.0, 0.0)
    cnt_ref[...] += jnp.sum(hits, axis=0, keepdims=True)


def _out_proj(ys, x2d, w_o, layer, ln_g, ln_b, w_route):
    t = x2d.shape[0]
    tm = 1024
    row = lambda i: (i, 0)
    full = lambda i: (0, 0)
    return pl.pallas_call(
        _oproj_kernel,
        grid=(t // tm,),
        in_specs=[pl.BlockSpec((tm, W_MIX), row)] * 4 + [
            pl.BlockSpec((tm, D_MODEL), row),
            pl.BlockSpec((None, D_MODEL, D_MODEL), lambda i: (layer, 0, 0)),
            pl.BlockSpec((1, D_MODEL), full),
            pl.BlockSpec((1, D_MODEL), full),
            pl.BlockSpec((D_MODEL, 2 * LANES), full)],
        out_specs=[pl.BlockSpec((tm, D_MODEL), row), pl.BlockSpec((tm, D_MODEL // 2), row),
                   pl.BlockSpec((tm, LANES), row), pl.BlockSpec((1, LANES), full)],
        out_shape=[jax.ShapeDtypeStruct((t, D_MODEL), F32),
                   jax.ShapeDtypeStruct((t, D_MODEL // 2), U32),
                   jax.ShapeDtypeStruct((t, LANES), F32),
                   jax.ShapeDtypeStruct((1, LANES), F32)],
        compiler_params=_cparams(("arbitrary",)),
        name="out_proj_ln_router",
    )(*ys, x2d, w_o, ln_g, ln_b, w_route)


PLAN_TILE = 1024


def _plan_kernel(route_ref, cnt_ref, dest_ref, meta_ref, carry_ref, pstart_ref):
    tm = route_ref.shape[0]
    lane = lax.broadcasted_iota(jnp.int32, (1, LANES), 1).astype(F32)

    @pl.when(pl.program_id(0) == 0)
    def _():
        cnt = cnt_ref[...]
        padded = jnp.ceil(cnt * (1.0 / MOE_BLK)) * MOE_BLK
        li = lax.broadcasted_iota(jnp.int32, (LANES, LANES), 0)
        lj = lax.broadcasted_iota(jnp.int32, (LANES, LANES), 1)
        before = (li < lj).astype(F32)
        pstart = jnp.dot(jnp.broadcast_to(padded, (8, LANES)), before, precision=HI,
                         preferred_element_type=F32)[0:1]
        pstart_ref[...] = pstart
        carry_ref[...] = jnp.zeros_like(carry_ref)
        meta_ref[...] = jnp.concatenate(
            [pstart + padded, pstart, cnt, jnp.zeros((5, LANES), F32)], axis=0)

    oh0 = jnp.where(lane == route_ref[:, 0:1], 1.0, 0.0)
    oh1 = jnp.where(lane == route_ref[:, 1:2], 1.0, 0.0)
    both = oh0 + oh1
    ri = lax.broadcasted_iota(jnp.int32, (tm, tm), 0)
    ci = lax.broadcasted_iota(jnp.int32, (tm, tm), 1)
    earlier = (ci < ri).astype(BF16)
    base = (jnp.dot(earlier, both.astype(BF16), preferred_element_type=F32)
            + carry_ref[...] + pstart_ref[...])
    d0 = jnp.sum(oh0 * base, axis=-1, keepdims=True)
    d1 = jnp.sum(oh1 * base, axis=-1, keepdims=True)
    dest = jnp.where(lane == 0, d0, jnp.where(lane == 1, d1, 0.0))
    dest_ref[...] = dest.T[0:2, :].astype(jnp.int32)
    carry_ref[...] += jnp.sum(both, axis=0, keepdims=True)


def _dispatch_plan(route, cnt, n_blocks):
    t = route.shape[0]
    tm = PLAN_TILE
    blk = MOE_BLK
    dest, meta = pl.pallas_call(
        _plan_kernel,
        grid=(t // tm,),
        in_specs=[pl.BlockSpec((tm, LANES), lambda i: (i, 0)),
                  pl.BlockSpec((1, LANES), lambda i: (0, 0))],
        out_specs=[pl.BlockSpec((2, tm), lambda i: (0, i)),
                   pl.BlockSpec((8, LANES), lambda i: (0, 0))],
        out_shape=[jax.ShapeDtypeStruct((2, t), jnp.int32),
                   jax.ShapeDtypeStruct((8, LANES), F32)],
        scratch_shapes=[pltpu.VMEM((1, LANES), F32), pltpu.VMEM((1, LANES), F32)],
        compiler_params=_cparams(("arbitrary",)),
        name="moe_plan",
    )(route, cnt)
    meta_i = meta[:, :N_EXPERTS].astype(jnp.int32)
    pad_end, pad_start, seg_end = meta_i[0], meta_i[1], meta_i[1] + meta_i[2]
    starts = jnp.arange(n_blocks, dtype=jnp.int32)[:, None] * blk
    member = jnp.logical_and(starts >= pad_start[None, :], starts < pad_end[None, :])
    expert_ids = jnp.arange(N_EXPERTS, dtype=jnp.int32)[None, :]
    block_e = jnp.where(jnp.any(member, axis=1), jnp.sum(jnp.where(member, expert_ids, 0), axis=1),
                        N_EXPERTS - 1)
    n_used = (pad_end[N_EXPERTS - 1] // blk).reshape(1)
    n_valid = jnp.clip(jnp.sum(jnp.where(member, seg_end[None, :], 0), axis=1) - starts[:, 0],
                       0, blk)
    return dest.reshape(2 * t), block_e, n_used, n_valid


def _ffn_kernel(be_ref, nu_ref, nv_ref, xs_ref, wg_ref, wu_ref, wd_ref, y_ref,
                wgb_ref, wub_ref, wdb_ref):
    i = pl.program_id(0)
    prev = be_ref[jnp.maximum(i - 1, 0)]

    @pl.when(jnp.logical_or(i == 0, be_ref[i] != prev))
    def _():
        wgb_ref[...] = wg_ref[...].astype(BF16)
        wub_ref[...] = wu_ref[...].astype(BF16)
        wdb_ref[...] = wd_ref[...].astype(BF16)

    @pl.when(i < nu_ref[0])
    def _():
        half = xs_ref.shape[0] // 2
        for r in range(2):
            rows = slice(r * half, (r + 1) * half)
            row = lax.broadcasted_iota(jnp.int32, (half, 1), 0) + r * half
            xp = jnp.where(row < nv_ref[i], xs_ref[rows, :], U32(0))
            x_hi, x_lo = _unpack_halves(xp)
            xb = jnp.concatenate([x_hi.astype(BF16), x_lo.astype(BF16)], axis=1)
            a = jnp.dot(xb, wgb_ref[...], preferred_element_type=F32)
            u = jnp.dot(xb, wub_ref[...], preferred_element_type=F32)
            h = (a * _sigmoid(a) * u).astype(BF16)
            y_ref[rows, :] = _pack_halves(jnp.dot(h, wdb_ref[...], preferred_element_type=F32))

    @pl.when(i >= nu_ref[0])
    def _():
        y_ref[...] = jnp.zeros_like(y_ref)


def _expert_ffn(xs, block_e, n_used, n_valid, w_gate, w_up, w_down, layer):
    n_slots = xs.shape[0]
    blk = MOE_BLK
    w_map = lambda i, be, nu, nv: (layer, be[i], 0, 0)
    grid_spec = pltpu.PrefetchScalarGridSpec(
        num_scalar_prefetch=3,
        grid=(n_slots // blk,),
        in_specs=[pl.BlockSpec((blk, D_MODEL // 2),
                               lambda i, be, nu, nv: (jnp.minimum(i, nu[0] - 1), 0)),
                  pl.BlockSpec((None, None, D_MODEL, D_EXPERT), w_map),
                  pl.BlockSpec((None, None, D_MODEL, D_EXPERT), w_map),
                  pl.BlockSpec((None, None, D_EXPERT, D_MODEL), w_map)],
        out_specs=pl.BlockSpec((blk, D_MODEL // 2), lambda i, be, nu, nv: (i, 0)),
        scratch_shapes=[pltpu.VMEM((D_MODEL, D_EXPERT), BF16),
                        pltpu.VMEM((D_MODEL, D_EXPERT), BF16),
                        pltpu.VMEM((D_EXPERT, D_MODEL), BF16)],
    )
    return pl.pallas_call(
        _ffn_kernel,
        grid_spec=grid_spec,
        out_shape=jax.ShapeDtypeStruct((n_slots, D_MODEL // 2), U32),
        compiler_params=_cparams(("arbitrary",)),
        name="expert_ffn",
    )(block_e, n_used, n_valid, xs, w_gate, w_up, w_down)


SC_CORES = 2
SC_SUBCORES = 16
SC_ROWS = 64


def _sc_gather_rows(table, idx):
    b = idx.shape[0]
    d = table.shape[1]
    per_w = b // (SC_CORES * SC_SUBCORES)
    mesh = plsc.VectorSubcoreMesh(core_axis_name="c", subcore_axis_name="s")

    n_chunks = per_w // SC_ROWS

    @functools.partial(
        pl.kernel, mesh=mesh,
        out_type=jax.ShapeDtypeStruct((b, d), table.dtype),
        scratch_types=[pltpu.VMEM((SC_ROWS,), jnp.int32), pltpu.VMEM((SC_ROWS,), jnp.int32),
                       pltpu.VMEM((SC_ROWS, d), table.dtype),
                       pltpu.VMEM((SC_ROWS, d), table.dtype),
                       pltpu.SemaphoreType.DMA, pltpu.SemaphoreType.DMA,
                       pltpu.SemaphoreType.DMA, pltpu.SemaphoreType.DMA],
        name="sc_gather_rows",
    )
    def gather(table_hbm, idx_hbm, out_hbm, idx0, idx1, rows0, rows1, gs0, gs1, ws0, ws1):
        idx_v, rows_v, gsem, wsem = (idx0, idx1), (rows0, rows1), (gs0, gs1), (ws0, ws1)
        wid = lax.axis_index("s") * SC_CORES + lax.axis_index("c")
        base = wid * per_w

        def rows_of(c):
            return pl.ds(pl.multiple_of(base + c * SC_ROWS, SC_ROWS), SC_ROWS)

        def start_gather(c, s):
            pltpu.sync_copy(idx_hbm.at[rows_of(c)], idx_v[s])
            pltpu.async_copy(table_hbm.at[idx_v[s]], rows_v[s], gsem[s])

        def write_back(c, s):
            pltpu.make_async_copy(table_hbm.at[idx_v[s]], rows_v[s], gsem[s]).wait()
            pltpu.async_copy(rows_v[s], out_hbm.at[rows_of(c)], wsem[s]).wait()

        start_gather(0, 0)

        @pl.loop(0, n_chunks, step=2)
        def _(c):
            start_gather(c + 1, 1)
            write_back(c, 0)

            @pl.when(c + 2 < n_chunks)
            def _():
                start_gather(c + 2, 0)

            write_back(c + 1, 1)

    return gather(table, idx)


def _sc_scatter_rows(x2d, idx, n_slots):
    t, d = x2d.shape
    per_w = t // (SC_CORES * SC_SUBCORES)
    mesh = plsc.VectorSubcoreMesh(core_axis_name="c", subcore_axis_name="s")

    @functools.partial(
        pl.kernel, mesh=mesh,
        out_type=jax.ShapeDtypeStruct((n_slots, d), x2d.dtype),
        scratch_types=[pltpu.VMEM((SC_ROWS,), jnp.int32), pltpu.VMEM((SC_ROWS,), jnp.int32),
                       pltpu.VMEM((SC_ROWS, d), x2d.dtype),
                       pltpu.SemaphoreType.DMA, pltpu.SemaphoreType.DMA],
        name="sc_scatter_rows",
    )
    def scatter(x_hbm, idx_hbm, out_hbm, idx0, idx1, rows_v, s0, s1):
        wid = lax.axis_index("s") * SC_CORES + lax.axis_index("c")
        base = wid * per_w

        @pl.loop(0, per_w // SC_ROWS)
        def _(c):
            off = pl.multiple_of(base + c * SC_ROWS, SC_ROWS)
            pltpu.sync_copy(x_hbm.at[pl.ds(off, SC_ROWS)], rows_v)
            pltpu.sync_copy(idx_hbm.at[pl.ds(off, SC_ROWS)], idx0)
            pltpu.sync_copy(idx_hbm.at[pl.ds(t + off, SC_ROWS)], idx1)
            cp0 = pltpu.async_copy(rows_v, out_hbm.at[idx0], s0)
            cp1 = pltpu.async_copy(rows_v, out_hbm.at[idx1], s1)
            cp0.wait()
            cp1.wait()

    return scatter(x2d, idx)


def _combine_dense_kernel(route_ref, x_ref, y0_ref, y1_ref, g_ref, b_ref, o_ref):
    y0 = jnp.concatenate(_unpack_halves(y0_ref[...]), axis=1)
    y1 = jnp.concatenate(_unpack_halves(y1_ref[...]), axis=1)
    moe = route_ref[:, 2:3] * y0 + route_ref[:, 3:4] * y1
    h = ALPHA * x_ref[...] + moe
    o_ref[...] = _layer_norm(h, g_ref[...], b_ref[...])


def _combine_dense(route, x2d, yg, ln_g, ln_b):
    t = x2d.shape[0]
    tm = 1024
    nt = t // tm
    row = lambda i: (i, 0)
    full = lambda i: (0, 0)
    return pl.pallas_call(
        _combine_dense_kernel,
        grid=(nt,),
        in_specs=[pl.BlockSpec((tm, LANES), row),
                  pl.BlockSpec((tm, D_MODEL), row),
                  pl.BlockSpec((tm, D_MODEL // 2), row),
                  pl.BlockSpec((tm, D_MODEL // 2), lambda i: (i + nt, 0)),
                  pl.BlockSpec((1, D_MODEL), full),
                  pl.BlockSpec((1, D_MODEL), full)],
        out_specs=pl.BlockSpec((tm, D_MODEL), row),
        out_shape=jax.ShapeDtypeStruct((t, D_MODEL), F32),
        compiler_params=_cparams(("parallel",)),
        name="moe_combine_dense",
    )(route, x2d, yg, yg, ln_g, ln_b)


def _pad_cols(w, n):
    return jnp.pad(w, [(0, 0)] * (w.ndim - 1) + [(0, n - w.shape[-1])])


def kernel(x, w_in, conv_w, gla_w_lr, gla_b_lr, gla_norm_g, ssd_conv_w, ssd_conv_b, ssd_a_log,
           ssd_d, ssd_dt_bias, ssd_norm_g, diff_lq1, diff_lk1, diff_lq2, diff_lk2, diff_norm_g,
           w_o, ln1_g, ln1_b, router_g, router_e, w_gate, w_up, w_down, ln2_g, ln2_b):
    bsz, seq, d = x.shape
    t = bsz * seq
    n_assign = 2 * t
    n_blocks = (n_assign + N_EXPERTS * (MOE_BLK - 1)) // MOE_BLK + 1
    n_slots = n_blocks * MOE_BLK
    x2d = x.reshape(t, d)
    w_o_b = w_o.astype(BF16)
    for l in range(DEPTH):
        pc, pg, plr, ps, pdt, pd = [p.reshape(bsz, seq, -1) for p in _in_proj(x2d, w_in, l)]

        w_lr_pad = jnp.pad(gla_w_lr[l], ((0, LANES - GLA_RANK), (0, 0)))
        pad4 = lambda v: jnp.pad(v, (0, LANES - SSD_HEADS)).reshape(1, LANES)
        y_conv, y_gla, y_ssd = _recurrent_mixers(
            pc, conv_w[l], pg, plr, w_lr_pad, gla_b_lr[l].reshape(1, -1),
            jnp.tile(gla_norm_g[l], GLA_HEADS).reshape(1, -1),
            ps, pdt, ssd_conv_w[l], ssd_conv_b[l].reshape(1, -1),
            pad4(ssd_a_log[l]), pad4(ssd_dt_bias[l]),
            jnp.repeat(ssd_d[l], SSD_HEADDIM).reshape(1, -1), ssd_norm_g[l].reshape(1, -1))
        lam_vecs = jnp.pad(jnp.stack([diff_lq1[l], diff_lk1[l], diff_lq2[l], diff_lk2[l]]),
                           ((0, 0), (0, LANES - DIFF_DQK)))
        lam_init = 0.8 - 0.6 * math.exp(-0.3 * l)
        y_diff = _diff_mixer(pd, lam_vecs,
                             jnp.tile(diff_norm_g[l], DIFF_HEADS).reshape(1, -1), lam_init)

        w_route = _pad_cols(jnp.concatenate(
            [router_g[l], router_e[l].reshape(d, N_EXPERTS)], axis=1), LANES)
        w_route_hi = w_route.astype(BF16)
        w_route = jnp.concatenate(
            [w_route_hi, (w_route - w_route_hi.astype(F32)).astype(BF16)], axis=1)
        ys = [y.reshape(t, W_MIX) for y in (y_conv, y_gla, y_ssd, y_diff)]
        xn, xn_p, route, cnt = _out_proj(ys, x2d, w_o_b, l, ln1_g[l].reshape(1, -1),
                                         ln1_b[l].reshape(1, -1), w_route)

        idx, block_e, n_used, n_valid = _dispatch_plan(route, cnt, n_blocks)
        xs = _sc_scatter_rows(xn_p, idx, n_slots)
        y = _expert_ffn(xs, block_e, n_used, n_valid, w_gate, w_up, w_down, l)
        yg = _sc_gather_rows(y, idx)
        x2d = _combine_dense(route, xn, yg, ln2_g[l].reshape(1, -1), ln2_b[l].reshape(1, -1))
    return x2d.reshape(bsz, seq, d)
```

```python
import functools
import math

import jax
import jax.numpy as jnp
from jax import lax
from jax.experimental import pallas as pl
from jax.experimental.pallas import tpu as pltpu
from jax.experimental.pallas import tpu_sc as plsc

F32 = jnp.float32
BF16 = jnp.bfloat16
HI = lax.Precision.HIGHEST

D_MODEL = 1024
DEPTH = 2
W_MIX = 256
GLA_HEADS, GLA_DK, GLA_DV, GLA_RANK, GLA_TAU, GLA_CHUNK = 4, 32, 64, 16, 16.0, 64
GLA_ROWS = 256
REC_SEQS = 4
SSD_HEADS, SSD_GROUPS, SSD_HEADDIM, SSD_STATE, SSD_CONV_K, SSD_CHUNK = 4, 2, 64, 128, 4, 128
DIFF_HEADS, DIFF_DQK, DIFF_DV = 4, 32, 64
N_GROUPS, EXPERTS_PER_GROUP, N_EXPERTS, D_EXPERT = 4, 8, 32, 512
ALPHA = (2 * DEPTH) ** 0.25
LN_EPS = 1e-5
RMS_EPS = 1e-6

LANES = 128
SUBLANES = 8
PROJ_WIDTHS = (768, 768, 128, 1024, 128, 768)
PROJ_SRC_OFFSETS = (0, 768, 1536, 1552, 2576, 2580)
PROJ_SRC_WIDTHS = (768, 768, GLA_RANK, 1024, SSD_HEADS, 768)
PROJ_DTYPES = (BF16, BF16, F32, BF16, F32, BF16)
VMEM_LIMIT = 56 * 1024 * 1024

MOE_BLK = 512


def _cparams(sem):
    return pltpu.CompilerParams(dimension_semantics=sem, vmem_limit_bytes=VMEM_LIMIT)


def _sigmoid(x):
    return 1.0 / (1.0 + jnp.exp(-x))


def _softplus(x):
    return jnp.maximum(x, 0.0) + jnp.log(1.0 + jnp.exp(-jnp.abs(x)))


def _layer_norm(h, g, b):
    mu = jnp.mean(h, axis=-1, keepdims=True)
    d = h - mu
    var = jnp.mean(d * d, axis=-1, keepdims=True)
    return d * lax.rsqrt(var + LN_EPS) * g + b


def _dot_nt(a, b):
    return lax.dot_general(a, b, (((1,), (1,)), ((), ())), preferred_element_type=F32)


def _dot_tn(a, b, precision=None):
    return lax.dot_general(a, b, (((0,), (0,)), ((), ())), preferred_element_type=F32,
                           precision=precision)


def _split_bf16(x, parts):
    out = []
    for _ in range(parts - 1):
        hi = x.astype(BF16)
        out.append(hi)
        x = x - hi.astype(F32)
    out.append(x.astype(BF16))
    return out


def _dot(a, b):
    return jnp.dot(a, b, preferred_element_type=F32)


U32 = jnp.uint32


def _pack_halves(x):
    w = x.shape[1] // 2
    hi = lax.bitcast_convert_type(x[:, :w].astype(BF16).astype(F32), U32)
    lo = lax.bitcast_convert_type(x[:, w:].astype(BF16).astype(F32), U32)
    return hi | lax.shift_right_logical(lo, U32(16))


def _unpack_halves(p):
    hi = lax.bitcast_convert_type(p & U32(0xFFFF0000), F32)
    lo = lax.bitcast_convert_type(lax.shift_left(p, U32(16)), F32)
    return hi, lo


def _dot_split_lhs(a, b_exact, parts, dot=_dot):
    acc = None
    for term in _split_bf16(a, parts):
        d = dot(term, b_exact)
        acc = d if acc is None else acc + d
    return acc


def _dot_split_rhs(a_exact, b, parts):
    acc = None
    for term in _split_bf16(b, parts):
        d = jnp.dot(a_exact, term, preferred_element_type=F32)
        acc = d if acc is None else acc + d
    return acc


def _proj_kernel(x_ref, w_ref, *o_refs):
    xb = x_ref[...].astype(BF16)
    off = 0
    for o_ref in o_refs:
        n = o_ref.shape[-1]
        o_ref[...] = jnp.dot(xb, w_ref[:, off:off + n],
                             preferred_element_type=F32).astype(o_ref.dtype)
        off += n


def _in_proj(x2d, w_r, layer):
    t = x2d.shape[0]
    tm = 1024
    ncol = sum(PROJ_WIDTHS)
    return pl.pallas_call(
        _proj_kernel,
        grid=(t // tm,),
        in_specs=[pl.BlockSpec((tm, D_MODEL), lambda i: (i, 0)),
                  pl.BlockSpec((None, D_MODEL, ncol), lambda i: (layer, 0, 0))],
        out_specs=[pl.BlockSpec((tm, n), lambda i: (i, 0)) for n in PROJ_WIDTHS],
        out_shape=[jax.ShapeDtypeStruct((t, n), dt) for n, dt in zip(PROJ_WIDTHS, PROJ_DTYPES)],
        compiler_params=_cparams(("parallel",)),
        name="in_proj",
    )(x2d, w_r)


def _gla_setup(p_ref, lr_ref, wlr_ref, blr_ref, ng_ref, o_ref, st_ref):
    c = GLA_CHUNK
    nh, dk, dv = GLA_HEADS, GLA_DK, GLA_DV

    rb = GLA_ROWS
    ncb = rb // c
    ri = lax.broadcasted_iota(jnp.int32, (rb, rb), 0)
    ci = lax.broadcasted_iota(jnp.int32, (rb, rb), 1)
    tri = (ci <= ri).astype(BF16)
    klane_head = lax.broadcasted_iota(jnp.int32, (1, nh * dk), 1) // dk
    vlane_head = lax.broadcasted_iota(jnp.int32, (1, nh * dv), 1) // dv
    strow_head = lax.broadcasted_iota(jnp.int32, (nh * dv, 1), 0) // dv
    st_mask = strow_head == klane_head
    r4 = lax.broadcasted_iota(jnp.int32, (nh * c, c), 0) % c
    c4 = lax.broadcasted_iota(jnp.int32, (nh * c, c), 1)
    causal4 = c4 <= r4
    gi = lax.broadcasted_iota(jnp.int32, (nh * dv, nh * dv), 0) // dv
    gj = lax.broadcasted_iota(jnp.int32, (nh * dv, nh * dv), 1) // dv
    gmean = jnp.where(gi == gj, 1.0 / dv, 0.0).astype(BF16)
    wlr_hi, wlr_lo = _split_bf16(wlr_ref[...], 2)

    def one_seq(bb, rows):
        q = p_ref[bb, rows, 0:128].astype(F32) * (dk ** -0.5)
        k = p_ref[bb, rows, 128:256].astype(F32)
        vb = p_ref[bb, rows, 256:512]
        g = p_ref[bb, rows, 512:768].astype(F32)
        lr = lr_ref[bb, rows, :]
        lr_hi, lr_lo = _split_bf16(lr, 2)
        z = (jnp.dot(lr_hi, wlr_hi, preferred_element_type=F32)
             + jnp.dot(lr_hi, wlr_lo, preferred_element_type=F32)
             + jnp.dot(lr_lo, wlr_hi, preferred_element_type=F32)) + blr_ref[...]
        log_a = (jnp.minimum(z, 0.0) - jnp.log(1.0 + jnp.exp(-jnp.abs(z)))) * (1.0 / GLA_TAU)
        cumb = _dot_split_rhs(tri, log_a, 3)
        ends = [cumb[(j + 1) * c - 1:(j + 1) * c, :] for j in range(ncb)]
        starts = [jnp.zeros_like(ends[0])] + ends[:-1]
        cum = cumb - jnp.concatenate([jnp.broadcast_to(s0, (c, nh * dk)) for s0 in starts], axis=0)
        lasts = [e - s0 for e, s0 in zip(ends, starts)]
        cl = jnp.concatenate([jnp.broadcast_to(x, (c, nh * dk)) for x in lasts], axis=0)
        q_dec = q * jnp.exp(cum)
        k_inv = (k * jnp.exp(-cum)).astype(BF16)
        k_end = (k * jnp.exp(cl - cum)).astype(BF16)
        st = st_ref[bb]
        outs = []
        for j in range(ncb):
            sl = slice(j * c, (j + 1) * c)
            qd = q_dec[sl]
            qs = jnp.concatenate([jnp.where(klane_head == h, qd, 0.0) for h in range(nh)],
                                 axis=0).astype(BF16)
            att = jnp.where(causal4, _dot_nt(qs, k_inv[sl]), 0.0)
            r = jnp.dot(att.astype(BF16), vb[sl], preferred_element_type=F32)
            o = jnp.where(vlane_head == 0, r[0:c], 0.0)
            for h in range(1, nh):
                o = o + jnp.where(vlane_head == h, r[h * c:(h + 1) * c], 0.0)
            outs.append(o + _dot_nt(qd.astype(BF16), st.astype(BF16)))
            d_st = _dot_tn(vb[sl], k_end[sl])
            st = st * jnp.exp(lasts[j]) + jnp.where(st_mask, d_st, 0.0)
        st_ref[bb] = st
        o = jnp.concatenate(outs, axis=0)
        ms = _dot_split_lhs(o * o, gmean, 2)
        o = o * lax.rsqrt(ms + RMS_EPS) * ng_ref[...]
        o_ref[bb, rows, :] = (o * (g * _sigmoid(g))).astype(o_ref.dtype)

    return one_seq


def _ssd_setup(p_ref, dt_ref, cw_ref, cb_ref, alog_ref, dtb_ref, dsk_ref, ng_ref, o_ref, st_ref,
               halo_ref):
    c = SSD_CHUNK
    n_st = SSD_STATE
    halo = 2 * SUBLANES

    ri = lax.broadcasted_iota(jnp.int32, (c, c), 0)
    ci = lax.broadcasted_iota(jnp.int32, (c, c), 1)
    causal = ci <= ri
    tri = causal.astype(BF16)
    upper = (ri <= ci).astype(BF16)
    lane_head = lax.broadcasted_iota(jnp.int32, (1, W_MIX), 1) // SSD_HEADDIM
    lane_group = lane_head // (SSD_HEADS // SSD_GROUPS)
    eh = lax.broadcasted_iota(jnp.int32, (LANES, W_MIX), 0)
    el = lax.broadcasted_iota(jnp.int32, (LANES, W_MIX), 1) // SSD_HEADDIM
    expand = (eh == el).astype(BF16)
    row8 = lax.broadcasted_iota(jnp.int32, (8, 3 * W_MIX), 0)
    a_c = -jnp.exp(alog_ref[...])

    def one_chunk(n, bb):
        rows = slice(n * c, (n + 1) * c)
        cur = p_ref[bb, rows, 256:1024].astype(F32)
        before = halo_ref[bb] if n == 0 else p_ref[bb, n * c - halo:n * c, 256:1024]
        prev8 = before.astype(F32)[SUBLANES:]
        acc = cur * cw_ref[3:4, :] + cb_ref[...]
        for s in (1, 2, 3):
            sh = pltpu.roll(cur, s, axis=0)
            top = jnp.where(row8 < s, pltpu.roll(prev8, s, axis=0), sh[0:8])
            sh = jnp.concatenate([top, sh[8:]], axis=0)
            acc = acc + sh * cw_ref[3 - s:4 - s, :]
        xbc = acc * _sigmoid(acc)
        x = xbc[:, 0:256]
        bm = xbc[:, 256:512].astype(BF16)
        cm = xbc[:, 512:768].astype(BF16)

        dt_c = _softplus(dt_ref[bb, rows, :] + dtb_ref[...])
        da_c = dt_c * a_c
        cum_c = _dot_split_rhs(tri, da_c, 3)
        cum_r = _dot_split_lhs(da_c, upper, 3, dot=_dot_tn)
        both_x = _dot_split_lhs(jnp.concatenate([dt_c, cum_c], axis=0), expand, 3)
        dt_x = both_x[0:c]
        cum_x = both_x[c:2 * c]
        cl_x = cum_x[c - 1:c, :]
        x_dt = x * dt_x
        x_dt_b = x_dt.astype(BF16)
        xw_b = (x_dt * jnp.exp(cl_x - cum_x)).astype(BF16)

        y = x * dsk_ref[...]
        y_off = jnp.zeros((c, W_MIX), F32)
        for g in range(SSD_GROUPS):
            bg = bm[:, g * n_st:(g + 1) * n_st]
            cg = cm[:, g * n_st:(g + 1) * n_st]
            cb = _dot_nt(cg, bg)
            for r in range(SSD_HEADS // SSD_GROUPS):
                h = g * (SSD_HEADS // SSD_GROUPS) + r
                diff = cum_c[:, h:h + 1] - cum_r[h:h + 1, :]
                dec = jnp.exp(jnp.where(causal, diff, -jnp.inf))
                m = (cb * dec).astype(BF16)
                yh = jnp.dot(m, x_dt_b, preferred_element_type=F32)
                y = y + jnp.where(lane_head == h, yh, 0.0)
            st = st_ref[bb, g]
            y_off = y_off + jnp.where(lane_group == g,
                                      jnp.dot(cg, st.astype(BF16), preferred_element_type=F32), 0.0)
            d_st = _dot_tn(bg, xw_b)
            st_ref[bb, g] = st * jnp.exp(cl_x) + jnp.where(lane_group == g, d_st, 0.0)
        y = y + y_off * jnp.exp(cum_x)
        zg = p_ref[bb, rows, 0:256].astype(F32)
        y = y * (zg * _sigmoid(zg))
        outs = []
        for g in range(SSD_GROUPS):
            yg = y[:, g * 128:(g + 1) * 128]
            ms = jnp.mean(yg * yg, axis=-1, keepdims=True)
            outs.append(yg * lax.rsqrt(ms + RMS_EPS))
        o_ref[bb, rows, :] = (jnp.concatenate(outs, axis=-1) * ng_ref[...]).astype(o_ref.dtype)

    return one_chunk


def _recurrent_kernel(pc_ref, ccw_ref, pg_ref, lr_ref, wlr_ref, blr_ref, gng_ref,
                      ps_ref, dt_ref, cw_ref, cb_ref, alog_ref, dtb_ref, dsk_ref, sng_ref,
                      oc_ref, og_ref, os_ref, gst_ref, sst_ref, halo_ref, chalo_ref):
    @pl.when(pl.program_id(1) == 0)
    def _():
        gst_ref[...] = jnp.zeros_like(gst_ref)
        sst_ref[...] = jnp.zeros_like(sst_ref)
        halo_ref[...] = jnp.zeros_like(halo_ref)
        chalo_ref[...] = jnp.zeros_like(chalo_ref)

    gla_rows = _gla_setup(pg_ref, lr_ref, wlr_ref, blr_ref, gng_ref, og_ref, gst_ref)
    ssd_chunk = _ssd_setup(ps_ref, dt_ref, cw_ref, cb_ref, alog_ref, dtb_ref, dsk_ref, sng_ref,
                           os_ref, sst_ref, halo_ref)
    row8 = lax.broadcasted_iota(jnp.int32, (SUBLANES, W_MIX), 0)
    for bb in range(pg_ref.shape[0]):
        gla_rows(bb, slice(0, GLA_ROWS))
        for n in range(GLA_ROWS // SSD_CHUNK):
            ssd_chunk(n, bb)
        halo_ref[bb] = ps_ref[bb, GLA_ROWS - halo_ref.shape[1]:GLA_ROWS, 256:1024]

        u = pc_ref[bb, :, 0:W_MIX].astype(F32)
        gb = pc_ref[bb, :, W_MIX:2 * W_MIX].astype(F32)
        gc = pc_ref[bb, :, 2 * W_MIX:3 * W_MIX].astype(F32)
        cu = gc * u
        prev8 = chalo_ref[bb]
        acc = cu * ccw_ref[2:3, :]
        for s in (1, 2):
            sh = pltpu.roll(cu, s, axis=0)
            top = jnp.where(row8 < s, pltpu.roll(prev8, s, axis=0), sh[0:SUBLANES])
            acc = acc + jnp.concatenate([top, sh[SUBLANES:]], axis=0) * ccw_ref[2 - s:3 - s, :]
        oc_ref[bb] = (gb * acc).astype(oc_ref.dtype)
        chalo_ref[bb] = cu[GLA_ROWS - SUBLANES:GLA_ROWS]


def _recurrent_mixers(pc, sconv_w, pg, plr, w_lr_pad, b_lr, gla_norm_g4,
                      ps, pdt, conv_w, conv_b, a_log_c, dt_bias_c, d_x, ssd_norm_g):
    b, s, _ = pg.shape
    nb = REC_SEQS
    rb = GLA_ROWS
    seq = lambda i, j: (i, j, 0)
    full2 = lambda i, j: (0, 0)
    return pl.pallas_call(
        _recurrent_kernel,
        grid=(b // nb, s // rb),
        in_specs=[pl.BlockSpec((nb, rb, pc.shape[2]), seq),
                  pl.BlockSpec((3, W_MIX), full2),
                  pl.BlockSpec((nb, rb, pg.shape[2]), seq),
                  pl.BlockSpec((nb, rb, LANES), seq),
                  pl.BlockSpec((LANES, LANES), full2),
                  pl.BlockSpec((1, LANES), full2),
                  pl.BlockSpec((1, W_MIX), full2),
                  pl.BlockSpec((nb, rb, ps.shape[2]), seq),
                  pl.BlockSpec((nb, rb, LANES), seq),
                  pl.BlockSpec((SSD_CONV_K, 3 * W_MIX), full2),
                  pl.BlockSpec((1, 3 * W_MIX), full2),
                  pl.BlockSpec((1, LANES), full2),
                  pl.BlockSpec((1, LANES), full2),
                  pl.BlockSpec((1, W_MIX), full2),
                  pl.BlockSpec((1, W_MIX), full2)],
        out_specs=[pl.BlockSpec((nb, rb, W_MIX), seq)] * 3,
        out_shape=[jax.ShapeDtypeStruct((b, s, W_MIX), BF16)] * 3,
        scratch_shapes=[pltpu.VMEM((nb, GLA_HEADS * GLA_DV, GLA_HEADS * GLA_DK), F32),
                        pltpu.VMEM((nb, SSD_GROUPS, SSD_STATE, W_MIX), F32),
                        pltpu.VMEM((nb, 2 * SUBLANES, 3 * W_MIX), BF16),
                        pltpu.VMEM((nb, SUBLANES, W_MIX), F32)],
        compiler_params=_cparams(("parallel", "arbitrary")),
        name="conv_gla_ssd_mixers",
    )(pc, sconv_w, pg, plr, w_lr_pad, b_lr, gla_norm_g4, ps, pdt, conv_w, conv_b, a_log_c,
      dt_bias_c, d_x, ssd_norm_g)


DIFF_TQ = 256
DIFF_TK = 256
LOG2E = 1.4426950408889634
DIFF_VPAD = DIFF_DV + 16
DIFF_SEQS = 4


def _diff_kernel(q_ref, k_ref, v_ref, lam_ref, ng_ref, o_ref,
                 vt_ref, qs_ref, st_ref, m_ref, acc_ref, *, lam_init):
    tq, tk = DIFF_TQ, DIFF_TK
    nh, dv = DIFF_HEADS, DIFF_DV
    nhc = 2 * nh
    s_len = k_ref.shape[1]
    i = pl.program_id(1)
    seqs = range(q_ref.shape[0])

    @pl.when(i == 0)
    def _():
        for bb in seqs:
            for cblk in range(s_len // tk):
                cols = slice(cblk * tk, (cblk + 1) * tk)
                vt = v_ref[bb, cols, :].astype(F32).T.astype(BF16)
                for h in range(nh):
                    vt_ref[bb, h, 0:dv, cols] = vt[h * dv:(h + 1) * dv]
        vt_ref[:, :, dv:, :] = jnp.ones((len(seqs), nh, DIFF_VPAD - dv, s_len), BF16)

    qlane = lax.broadcasted_iota(jnp.int32, (1, W_MIX), 1) // DIFF_DQK
    for bb in seqs:
        q = q_ref[bb].astype(F32) * (DIFF_DQK ** -0.5 * LOG2E)
        for hc in range(nhc):
            qs_ref[bb, hc * tq:(hc + 1) * tq, :] = jnp.where(qlane == hc, q, 0.0).astype(BF16)
    m_ref[...] = jnp.full_like(m_ref, -jnp.inf)
    acc_ref[...] = jnp.zeros_like(acc_ref)
    krow = lax.broadcasted_iota(jnp.int32, (tk, nhc * tq), 0)
    qcol = lax.broadcasted_iota(jnp.int32, (tk, nhc * tq), 1) % tq
    diag_ok = krow <= qcol

    def scores(j, slot):
        k0 = pl.multiple_of(j * tk, tk)
        for bb in seqs:
            st_ref[bb, slot] = _dot_nt(k_ref[bb, pl.ds(k0, tk), :], qs_ref[bb])

    def softmax_pv(j, slot, masked):
        for bb in seqs:
            softmax_pv_seq(bb, j, slot, masked)

    def softmax_pv_seq(bb, j, slot, masked):
        k0 = pl.multiple_of(j * tk, tk)
        st = st_ref[bb, slot]
        if masked:
            st = jnp.where(diag_ok, st, -jnp.inf)
        m_prev = m_ref[bb]
        m_new = jnp.maximum(m_prev, jnp.max(st, axis=0, keepdims=True))
        alpha = jnp.exp2(m_prev - m_new)
        p = jnp.exp2(st - m_new)
        m_ref[bb] = m_new
        pb = p.astype(BF16)
        for hc in range(nhc):
            h = hc // 2
            lanes = slice(hc * tq, (hc + 1) * tq)
            pv = jnp.dot(vt_ref[bb, h, :, pl.ds(k0, tk)], pb[:, lanes],
                         preferred_element_type=F32)
            acc_ref[bb, hc] = acc_ref[bb, hc] * alpha[:, lanes] + pv

    scores(0, 0)
    n_pairs = i // 2

    def pair_step(u, carry):
        scores(2 * u + 1, 1)
        softmax_pv(2 * u, 0, False)
        scores(2 * u + 2, 0)
        softmax_pv(2 * u + 1, 1, False)
        return carry

    lax.fori_loop(0, n_pairs, pair_step, 0)

    @pl.when(i % 2 == 0)
    def _():
        softmax_pv(i, 0, True)

    @pl.when(i % 2 == 1)
    def _():
        scores(i, 1)
        softmax_pv(i - 1, 0, False)
        softmax_pv(i, 1, True)

    lam = (jnp.exp(jnp.sum(lam_ref[0:1, :] * lam_ref[1:2, :], axis=-1, keepdims=True))
           - jnp.exp(jnp.sum(lam_ref[2:3, :] * lam_ref[3:4, :], axis=-1, keepdims=True))
           + lam_init)
    for bb in seqs:
        heads = []
        for h in range(nh):
            o1 = acc_ref[bb, 2 * h, 0:dv] / acc_ref[bb, 2 * h, dv:dv + 1]
            o2 = acc_ref[bb, 2 * h + 1, 0:dv] / acc_ref[bb, 2 * h + 1, dv:dv + 1]
            oh = o1 - lam * o2
            ms = jnp.mean(oh * oh, axis=0, keepdims=True)
            heads.append(oh * lax.rsqrt(ms + RMS_EPS))
        o = jnp.concatenate(heads, axis=0).T
        o_ref[bb] = (o * ng_ref[...] * (1.0 - lam_init)).astype(o_ref.dtype)


def _diff_mixer(pd, lam_vecs, norm_g4, lam_init):
    b, s, _ = pd.shape
    tq = DIFF_TQ
    nb = DIFF_SEQS
    return pl.pallas_call(
        functools.partial(_diff_kernel, lam_init=lam_init),
        grid=(b // nb, s // tq),
        in_specs=[pl.BlockSpec((nb, tq, W_MIX), lambda bi, i: (bi, i, 0)),
                  pl.BlockSpec((nb, s, W_MIX), lambda bi, i: (bi, 0, 1)),
                  pl.BlockSpec((nb, s, W_MIX), lambda bi, i: (bi, 0, 2)),
                  pl.BlockSpec((4, LANES), lambda bi, i: (0, 0)),
                  pl.BlockSpec((1, W_MIX), lambda bi, i: (0, 0))],
        out_specs=pl.BlockSpec((nb, tq, W_MIX), lambda bi, i: (bi, i, 0)),
        out_shape=jax.ShapeDtypeStruct((b, s, W_MIX), BF16),
        scratch_shapes=[pltpu.VMEM((nb, DIFF_HEADS, DIFF_VPAD, s), BF16),
                        pltpu.VMEM((nb, 2 * DIFF_HEADS * tq, W_MIX), BF16),
                        pltpu.VMEM((nb, 2, DIFF_TK, 2 * DIFF_HEADS * tq), F32),
                        pltpu.VMEM((nb, 1, 2 * DIFF_HEADS * tq), F32),
                        pltpu.VMEM((nb, 2 * DIFF_HEADS, DIFF_VPAD, tq), F32)],
        compiler_params=_cparams(("parallel", "arbitrary")),
        name="diff_attn",
    )(pd, pd, pd, lam_vecs, norm_g4)


def _oproj_kernel(yc_ref, yg_ref, ys_ref, yd_ref, x_ref, wo_ref, g_ref, b_ref, wr_ref,
                  xo_ref, xp_ref, route_ref, cnt_ref):
    mix = jnp.concatenate([yc_ref[...], yg_ref[...], ys_ref[...], yd_ref[...]], axis=-1)
    h = ALPHA * x_ref[...] + jnp.dot(mix, wo_ref[...], preferred_element_type=F32)
    xn = _layer_norm(h, g_ref[...], b_ref[...])
    xo_ref[...] = xn
    xp_ref[...] = _pack_halves(xn)

    xn_hi, xn_lo = _split_bf16(xn, 2)
    both = _dot(xn_hi, wr_ref[...])
    logits = both[:, 0:LANES] + both[:, LANES:2 * LANES] + _dot(xn_lo, wr_ref[:, 0:LANES])
    lane = lax.broadcasted_iota(jnp.int32, logits.shape, 1).astype(F32)
    neg = -jnp.inf
    big = float(LANES)
    lg = jnp.where(lane < N_GROUPS, logits, neg)
    mg = jnp.max(lg, axis=-1, keepdims=True)
    sg = jnp.sum(jnp.exp(lg - mg), axis=-1, keepdims=True)
    grp = jnp.min(jnp.where(lg == mg, lane, big), axis=-1, keepdims=True)
    p_grp = 1.0 / sg
    lo = N_GROUPS + EXPERTS_PER_GROUP * grp
    in_g = jnp.logical_and(lane >= lo, lane < lo + EXPERTS_PER_GROUP)
    le = jnp.where(in_g, logits, neg)
    me = jnp.max(le, axis=-1, keepdims=True)
    ee = jnp.exp(le - me)
    pe = ee / jnp.sum(ee, axis=-1, keepdims=True)
    pe = jnp.where(in_g, pe, -1.0)
    p1 = jnp.max(pe, axis=-1, keepdims=True)
    i1 = jnp.min(jnp.where(pe == p1, lane, big), axis=-1, keepdims=True)
    pe2 = jnp.where(lane == i1, -1.0, pe)
    p2 = jnp.max(pe2, axis=-1, keepdims=True)
    i2 = jnp.min(jnp.where(pe2 == p2, lane, big), axis=-1, keepdims=True)
    den = p1 + p2
    g1 = p_grp * p1 / den
    g2 = p_grp * p2 / den
    e1 = i1 - N_GROUPS
    e2 = i2 - N_GROUPS
    route_ref[...] = jnp.where(lane == 0, e1, jnp.where(lane == 1, e2, jnp.where(
        lane == 2, g1, jnp.where(lane == 3, g2, 0.0))))

    @pl.when(pl.program_id(0) == 0)
    def _():
        cnt_ref[...] = jnp.zeros_like(cnt_ref)

    hits = jnp.where(lane == e1, 1.0, 0.0) + jnp.where(lane == e2, 1.0, 0.0)
    cnt_ref[...] += jnp.sum(hits, axis=0, keepdims=True)


def _out_proj(ys, x2d, w_o, layer, ln_g, ln_b, w_route):
    t = x2d.shape[0]
    tm = 1024
    row = lambda i: (i, 0)
    full = lambda i: (0, 0)
    return pl.pallas_call(
        _oproj_kernel,
        grid=(t // tm,),
        in_specs=[pl.BlockSpec((tm, W_MIX), row)] * 4 + [
            pl.BlockSpec((tm, D_MODEL), row),
            pl.BlockSpec((None, D_MODEL, D_MODEL), lambda i: (layer, 0, 0)),
            pl.BlockSpec((1, D_MODEL), full),
            pl.BlockSpec((1, D_MODEL), full),
            pl.BlockSpec((D_MODEL, 2 * LANES), full)],
        out_specs=[pl.BlockSpec((tm, D_MODEL), row), pl.BlockSpec((tm, D_MODEL // 2), row),
                   pl.BlockSpec((tm, LANES), row), pl.BlockSpec((1, LANES), full)],
        out_shape=[jax.ShapeDtypeStruct((t, D_MODEL), F32),
                   jax.ShapeDtypeStruct((t, D_MODEL // 2), U32),
                   jax.ShapeDtypeStruct((t, LANES), F32),
                   jax.ShapeDtypeStruct((1, LANES), F32)],
        compiler_params=_cparams(("arbitrary",)),
        name="out_proj_ln_router",
    )(*ys, x2d, w_o, ln_g, ln_b, w_route)


PLAN_TILE = 512


def _plan_kernel(route_ref, cnt_ref, dest_ref, meta_ref, carry_ref, pstart_ref):
    tm = route_ref.shape[0]
    lane = lax.broadcasted_iota(jnp.int32, (1, LANES), 1).astype(F32)

    @pl.when(pl.program_id(0) == 0)
    def _():
        cnt = cnt_ref[...]
        padded = jnp.ceil(cnt * (1.0 / MOE_BLK)) * MOE_BLK
        li = lax.broadcasted_iota(jnp.int32, (LANES, LANES), 0)
        lj = lax.broadcasted_iota(jnp.int32, (LANES, LANES), 1)
        before = (li < lj).astype(F32)
        pstart = jnp.dot(jnp.broadcast_to(padded, (8, LANES)), before, precision=HI,
                         preferred_element_type=F32)[0:1]
        pstart_ref[...] = pstart
        carry_ref[...] = jnp.zeros_like(carry_ref)
        meta_ref[...] = jnp.concatenate(
            [pstart + padded, pstart, cnt, jnp.zeros((5, LANES), F32)], axis=0)

    oh0 = jnp.where(lane == route_ref[:, 0:1], 1.0, 0.0)
    oh1 = jnp.where(lane == route_ref[:, 1:2], 1.0, 0.0)
    both = oh0 + oh1
    ri = lax.broadcasted_iota(jnp.int32, (tm, tm), 0)
    ci = lax.broadcasted_iota(jnp.int32, (tm, tm), 1)
    earlier = (ci < ri).astype(BF16)
    base = (jnp.dot(earlier, both.astype(BF16), preferred_element_type=F32)
            + carry_ref[...] + pstart_ref[...])
    d0 = jnp.sum(oh0 * base, axis=-1, keepdims=True)
    d1 = jnp.sum(oh1 * base, axis=-1, keepdims=True)
    dest = jnp.where(lane == 0, d0, jnp.where(lane == 1, d1, 0.0))
    dest_ref[...] = dest.T[0:2, :].astype(jnp.int32)
    carry_ref[...] += jnp.sum(both, axis=0, keepdims=True)


def _dispatch_plan(route, cnt, n_blocks):
    t = route.shape[0]
    tm = PLAN_TILE
    blk = MOE_BLK
    dest, meta = pl.pallas_call(
        _plan_kernel,
        grid=(t // tm,),
        in_specs=[pl.BlockSpec((tm, LANES), lambda i: (i, 0)),
                  pl.BlockSpec((1, LANES), lambda i: (0, 0))],
        out_specs=[pl.BlockSpec((2, tm), lambda i: (0, i)),
                   pl.BlockSpec((8, LANES), lambda i: (0, 0))],
        out_shape=[jax.ShapeDtypeStruct((2, t), jnp.int32),
                   jax.ShapeDtypeStruct((8, LANES), F32)],
        scratch_shapes=[pltpu.VMEM((1, LANES), F32), pltpu.VMEM((1, LANES), F32)],
        compiler_params=_cparams(("arbitrary",)),
        name="moe_plan",
    )(route, cnt)
    meta_i = meta[:, :N_EXPERTS].astype(jnp.int32)
    pad_end, pad_start, seg_end = meta_i[0], meta_i[1], meta_i[1] + meta_i[2]
    starts = jnp.arange(n_blocks, dtype=jnp.int32)[:, None] * blk
    member = jnp.logical_and(starts >= pad_start[None, :], starts < pad_end[None, :])
    expert_ids = jnp.arange(N_EXPERTS, dtype=jnp.int32)[None, :]
    block_e = jnp.where(jnp.any(member, axis=1), jnp.sum(jnp.where(member, expert_ids, 0), axis=1),
                        N_EXPERTS - 1)
    n_used = (pad_end[N_EXPERTS - 1] // blk).reshape(1)
    n_valid = jnp.clip(jnp.sum(jnp.where(member, seg_end[None, :], 0), axis=1) - starts[:, 0],
                       0, blk)
    return dest.reshape(2 * t), block_e, n_used, n_valid


def _ffn_kernel(be_ref, nu_ref, nv_ref, xs_ref, wg_ref, wu_ref, wd_ref, y_ref,
                wgb_ref, wub_ref, wdb_ref):
    i = pl.program_id(0)
    prev = be_ref[jnp.maximum(i - 1, 0)]

    @pl.when(jnp.logical_or(i == 0, be_ref[i] != prev))
    def _():
        wgb_ref[...] = wg_ref[...].astype(BF16)
        wub_ref[...] = wu_ref[...].astype(BF16)
        wdb_ref[...] = wd_ref[...].astype(BF16)

    @pl.when(i < nu_ref[0])
    def _():
        half = xs_ref.shape[0] // 2
        for r in range(2):
            rows = slice(r * half, (r + 1) * half)
            row = lax.broadcasted_iota(jnp.int32, (half, 1), 0) + r * half
            xp = jnp.where(row < nv_ref[i], xs_ref[rows, :], U32(0))
            x_hi, x_lo = _unpack_halves(xp)
            xb = jnp.concatenate([x_hi.astype(BF16), x_lo.astype(BF16)], axis=1)
            a = jnp.dot(xb, wgb_ref[...], preferred_element_type=F32)
            u = jnp.dot(xb, wub_ref[...], preferred_element_type=F32)
            h = (a * _sigmoid(a) * u).astype(BF16)
            y_ref[rows, :] = _pack_halves(jnp.dot(h, wdb_ref[...], preferred_element_type=F32))

    @pl.when(i >= nu_ref[0])
    def _():
        y_ref[...] = jnp.zeros_like(y_ref)


def _expert_ffn(xs, block_e, n_used, n_valid, w_gate, w_up, w_down, layer):
    n_slots = xs.shape[0]
    blk = MOE_BLK
    w_map = lambda i, be, nu, nv: (layer, be[i], 0, 0)
    grid_spec = pltpu.PrefetchScalarGridSpec(
        num_scalar_prefetch=3,
        grid=(n_slots // blk,),
        in_specs=[pl.BlockSpec((blk, D_MODEL // 2),
                               lambda i, be, nu, nv: (jnp.minimum(i, nu[0] - 1), 0)),
                  pl.BlockSpec((None, None, D_MODEL, D_EXPERT), w_map),
                  pl.BlockSpec((None, None, D_MODEL, D_EXPERT), w_map),
                  pl.BlockSpec((None, None, D_EXPERT, D_MODEL), w_map)],
        out_specs=pl.BlockSpec((blk, D_MODEL // 2), lambda i, be, nu, nv: (i, 0)),
        scratch_shapes=[pltpu.VMEM((D_MODEL, D_EXPERT), BF16),
                        pltpu.VMEM((D_MODEL, D_EXPERT), BF16),
                        pltpu.VMEM((D_EXPERT, D_MODEL), BF16)],
    )
    return pl.pallas_call(
        _ffn_kernel,
        grid_spec=grid_spec,
        out_shape=jax.ShapeDtypeStruct((n_slots, D_MODEL // 2), U32),
        compiler_params=_cparams(("arbitrary",)),
        name="expert_ffn",
    )(block_e, n_used, n_valid, xs, w_gate, w_up, w_down)


SC_CORES = 2
SC_SUBCORES = 16
SC_ROWS = 64


def _sc_gather_rows(table, idx):
    b = idx.shape[0]
    d = table.shape[1]
    per_w = b // (SC_CORES * SC_SUBCORES)
    mesh = plsc.VectorSubcoreMesh(core_axis_name="c", subcore_axis_name="s")

    n_chunks = per_w // SC_ROWS

    @functools.partial(
        pl.kernel, mesh=mesh,
        out_type=jax.ShapeDtypeStruct((b, d), table.dtype),
        scratch_types=[pltpu.VMEM((SC_ROWS,), jnp.int32), pltpu.VMEM((SC_ROWS,), jnp.int32),
                       pltpu.VMEM((SC_ROWS, d), table.dtype),
                       pltpu.VMEM((SC_ROWS, d), table.dtype),
                       pltpu.SemaphoreType.DMA, pltpu.SemaphoreType.DMA,
                       pltpu.SemaphoreType.DMA, pltpu.SemaphoreType.DMA],
        name="sc_gather_rows",
    )
    def gather(table_hbm, idx_hbm, out_hbm, idx0, idx1, rows0, rows1, gs0, gs1, ws0, ws1):
        idx_v, rows_v, gsem, wsem = (idx0, idx1), (rows0, rows1), (gs0, gs1), (ws0, ws1)
        wid = lax.axis_index("s") * SC_CORES + lax.axis_index("c")
        base = wid * per_w

        def rows_of(c):
            return pl.ds(pl.multiple_of(base + c * SC_ROWS, SC_ROWS), SC_ROWS)

        def start_gather(c, s):
            pltpu.sync_copy(idx_hbm.at[rows_of(c)], idx_v[s])
            pltpu.async_copy(table_hbm.at[idx_v[s]], rows_v[s], gsem[s])

        def write_back(c, s):
            pltpu.make_async_copy(table_hbm.at[idx_v[s]], rows_v[s], gsem[s]).wait()
            pltpu.async_copy(rows_v[s], out_hbm.at[rows_of(c)], wsem[s]).wait()

        start_gather(0, 0)

        @pl.loop(0, n_chunks, step=2)
        def _(c):
            start_gather(c + 1, 1)
            write_back(c, 0)

            @pl.when(c + 2 < n_chunks)
            def _():
                start_gather(c + 2, 0)

            write_back(c + 1, 1)

    return gather(table, idx)


def _sc_scatter_rows(x2d, idx, n_slots):
    t, d = x2d.shape
    per_w = t // (SC_CORES * SC_SUBCORES)
    mesh = plsc.VectorSubcoreMesh(core_axis_name="c", subcore_axis_name="s")

    @functools.partial(
        pl.kernel, mesh=mesh,
        out_type=jax.ShapeDtypeStruct((n_slots, d), x2d.dtype),
        scratch_types=[pltpu.VMEM((SC_ROWS,), jnp.int32), pltpu.VMEM((SC_ROWS,), jnp.int32),
                       pltpu.VMEM((SC_ROWS, d), x2d.dtype),
                       pltpu.SemaphoreType.DMA, pltpu.SemaphoreType.DMA],
        name="sc_scatter_rows",
    )
    def scatter(x_hbm, idx_hbm, out_hbm, idx0, idx1, rows_v, s0, s1):
        wid = lax.axis_index("s") * SC_CORES + lax.axis_index("c")
        base = wid * per_w

        @pl.loop(0, per_w // SC_ROWS)
        def _(c):
            off = pl.multiple_of(base + c * SC_ROWS, SC_ROWS)
            pltpu.sync_copy(x_hbm.at[pl.ds(off, SC_ROWS)], rows_v)
            pltpu.sync_copy(idx_hbm.at[pl.ds(off, SC_ROWS)], idx0)
            pltpu.sync_copy(idx_hbm.at[pl.ds(t + off, SC_ROWS)], idx1)
            cp0 = pltpu.async_copy(rows_v, out_hbm.at[idx0], s0)
            cp1 = pltpu.async_copy(rows_v, out_hbm.at[idx1], s1)
            cp0.wait()
            cp1.wait()

    return scatter(x2d, idx)


def _combine_dense_kernel(route_ref, x_ref, y0_ref, y1_ref, g_ref, b_ref, o_ref):
    y0 = jnp.concatenate(_unpack_halves(y0_ref[...]), axis=1)
    y1 = jnp.concatenate(_unpack_halves(y1_ref[...]), axis=1)
    moe = route_ref[:, 2:3] * y0 + route_ref[:, 3:4] * y1
    h = ALPHA * x_ref[...] + moe
    o_ref[...] = _layer_norm(h, g_ref[...], b_ref[...])


def _combine_dense(route, x2d, yg, ln_g, ln_b):
    t = x2d.shape[0]
    tm = 1024
    nt = t // tm
    row = lambda i: (i, 0)
    full = lambda i: (0, 0)
    return pl.pallas_call(
        _combine_dense_kernel,
        grid=(nt,),
        in_specs=[pl.BlockSpec((tm, LANES), row),
                  pl.BlockSpec((tm, D_MODEL), row),
                  pl.BlockSpec((tm, D_MODEL // 2), row),
                  pl.BlockSpec((tm, D_MODEL // 2), lambda i: (i + nt, 0)),
                  pl.BlockSpec((1, D_MODEL), full),
                  pl.BlockSpec((1, D_MODEL), full)],
        out_specs=pl.BlockSpec((tm, D_MODEL), row),
        out_shape=jax.ShapeDtypeStruct((t, D_MODEL), F32),
        compiler_params=_cparams(("parallel",)),
        name="moe_combine_dense",
    )(route, x2d, yg, yg, ln_g, ln_b)


def _pad_cols(w, n):
    return jnp.pad(w, [(0, 0)] * (w.ndim - 1) + [(0, n - w.shape[-1])])


def kernel(x, w_in, conv_w, gla_w_lr, gla_b_lr, gla_norm_g, ssd_conv_w, ssd_conv_b, ssd_a_log,
           ssd_d, ssd_dt_bias, ssd_norm_g, diff_lq1, diff_lk1, diff_lq2, diff_lk2, diff_norm_g,
           w_o, ln1_g, ln1_b, router_g, router_e, w_gate, w_up, w_down, ln2_g, ln2_b):
    bsz, seq, d = x.shape
    t = bsz * seq
    n_assign = 2 * t
    n_blocks = (n_assign + N_EXPERTS * (MOE_BLK - 1)) // MOE_BLK + 1
    n_slots = n_blocks * MOE_BLK
    x2d = x.reshape(t, d)
    w_in_r = jnp.concatenate(
        [_pad_cols(w_in[..., src:src + n_src], n_dst)
         for src, n_src, n_dst in zip(PROJ_SRC_OFFSETS, PROJ_SRC_WIDTHS, PROJ_WIDTHS)],
        axis=-1).astype(BF16)
    w_o_b = w_o.astype(BF16)
    for l in range(DEPTH):
        pc, pg, plr, ps, pdt, pd = [p.reshape(bsz, seq, -1) for p in _in_proj(x2d, w_in_r, l)]

        w_lr_pad = jnp.pad(gla_w_lr[l], ((0, LANES - GLA_RANK), (0, 0)))
        pad4 = lambda v: jnp.pad(v, (0, LANES - SSD_HEADS)).reshape(1, LANES)
        y_conv, y_gla, y_ssd = _recurrent_mixers(
            pc, conv_w[l], pg, plr, w_lr_pad, gla_b_lr[l].reshape(1, -1),
            jnp.tile(gla_norm_g[l], GLA_HEADS).reshape(1, -1),
            ps, pdt, ssd_conv_w[l], ssd_conv_b[l].reshape(1, -1),
            pad4(ssd_a_log[l]), pad4(ssd_dt_bias[l]),
            jnp.repeat(ssd_d[l], SSD_HEADDIM).reshape(1, -1), ssd_norm_g[l].reshape(1, -1))
        lam_vecs = jnp.pad(jnp.stack([diff_lq1[l], diff_lk1[l], diff_lq2[l], diff_lk2[l]]),
                           ((0, 0), (0, LANES - DIFF_DQK)))
        lam_init = 0.8 - 0.6 * math.exp(-0.3 * l)
        y_diff = _diff_mixer(pd, lam_vecs,
                             jnp.tile(diff_norm_g[l], DIFF_HEADS).reshape(1, -1), lam_init)

        w_route = _pad_cols(jnp.concatenate(
            [router_g[l], router_e[l].reshape(d, N_EXPERTS)], axis=1), LANES)
        w_route_hi = w_route.astype(BF16)
        w_route = jnp.concatenate(
            [w_route_hi, (w_route - w_route_hi.astype(F32)).astype(BF16)], axis=1)
        ys = [y.reshape(t, W_MIX) for y in (y_conv, y_gla, y_ssd, y_diff)]
        xn, xn_p, route, cnt = _out_proj(ys, x2d, w_o_b, l, ln1_g[l].reshape(1, -1),
                                         ln1_b[l].reshape(1, -1), w_route)

        idx, block_e, n_used, n_valid = _dispatch_plan(route, cnt, n_blocks)
        xs = _sc_scatter_rows(xn_p, idx, n_slots)
        y = _expert_ffn(xs, block_e, n_used, n_valid, w_gate, w_up, w_down, l)
        yg = _sc_gather_rows(y, idx)
        x2d = _combine_dense(route, xn, yg, ln2_g[l].reshape(1, -1), ln2_b[l].reshape(1, -1))
    return x2d.reshape(bsz, seq, d)
```

```python
import functools
import math

import jax
import jax.numpy as jnp
from jax import lax
from jax.experimental import pallas as pl
from jax.experimental.pallas import tpu as pltpu
from jax.experimental.pallas import tpu_sc as plsc

F32 = jnp.float32
BF16 = jnp.bfloat16
HI = lax.Precision.HIGHEST

D_MODEL = 1024
DEPTH = 2
W_MIX = 256
GLA_HEADS, GLA_DK, GLA_DV, GLA_RANK, GLA_TAU, GLA_CHUNK = 4, 32, 64, 16, 16.0, 64
GLA_ROWS = 256
REC_SEQS = 4
SSD_HEADS, SSD_GROUPS, SSD_HEADDIM, SSD_STATE, SSD_CONV_K, SSD_CHUNK = 4, 2, 64, 128, 4, 128
DIFF_HEADS, DIFF_DQK, DIFF_DV = 4, 32, 64
N_GROUPS, EXPERTS_PER_GROUP, N_EXPERTS, D_EXPERT = 4, 8, 32, 512
ALPHA = (2 * DEPTH) ** 0.25
LN_EPS = 1e-5
RMS_EPS = 1e-6

LANES = 128
SUBLANES = 8
PROJ_WIDTHS = (768, 768, 128, 1024, 128, 768)
PROJ_SRC_OFFSETS = (0, 768, 1536, 1552, 2576, 2580)
PROJ_SRC_WIDTHS = (768, 768, GLA_RANK, 1024, SSD_HEADS, 768)
PROJ_DTYPES = (BF16, BF16, F32, BF16, F32, BF16)
VMEM_LIMIT = 56 * 1024 * 1024

MOE_BLK = 512
OPROJ_PARTS = 2


def _cparams(sem):
    return pltpu.CompilerParams(dimension_semantics=sem, vmem_limit_bytes=VMEM_LIMIT)


def _sigmoid(x):
    return 1.0 / (1.0 + jnp.exp(-x))


def _softplus(x):
    return jnp.maximum(x, 0.0) + jnp.log(1.0 + jnp.exp(-jnp.abs(x)))


def _layer_norm(h, g, b):
    mu = jnp.mean(h, axis=-1, keepdims=True)
    d = h - mu
    var = jnp.mean(d * d, axis=-1, keepdims=True)
    return d * lax.rsqrt(var + LN_EPS) * g + b


def _dot_nt(a, b):
    return lax.dot_general(a, b, (((1,), (1,)), ((), ())), preferred_element_type=F32)


def _dot_tn(a, b, precision=None):
    return lax.dot_general(a, b, (((0,), (0,)), ((), ())), preferred_element_type=F32,
                           precision=precision)


def _split_bf16(x, parts):
    out = []
    for _ in range(parts - 1):
        hi = x.astype(BF16)
        out.append(hi)
        x = x - hi.astype(F32)
    out.append(x.astype(BF16))
    return out


def _dot(a, b):
    return jnp.dot(a, b, preferred_element_type=F32)


U32 = jnp.uint32


def _pack_halves(x):
    w = x.shape[1] // 2
    hi = lax.bitcast_convert_type(x[:, :w].astype(BF16).astype(F32), U32)
    lo = lax.bitcast_convert_type(x[:, w:].astype(BF16).astype(F32), U32)
    return hi | lax.shift_right_logical(lo, U32(16))


def _unpack_halves(p):
    hi = lax.bitcast_convert_type(p & U32(0xFFFF0000), F32)
    lo = lax.bitcast_convert_type(lax.shift_left(p, U32(16)), F32)
    return hi, lo


def _dot_split_lhs(a, b_exact, parts, dot=_dot):
    acc = None
    for term in _split_bf16(a, parts):
        d = dot(term, b_exact)
        acc = d if acc is None else acc + d
    return acc


def _dot_split_rhs(a_exact, b, parts):
    acc = None
    for term in _split_bf16(b, parts):
        d = jnp.dot(a_exact, term, preferred_element_type=F32)
        acc = d if acc is None else acc + d
    return acc


def _proj_kernel(x_ref, w_ref, *refs):
    o_refs, w_scr = refs[:-1], refs[-1]

    @pl.when(pl.program_id(0) == 0)
    def _():
        w_scr[...] = jnp.zeros_like(w_scr)
        dst = 0
        for src, n_src, n_dst in zip(PROJ_SRC_OFFSETS, PROJ_SRC_WIDTHS, PROJ_WIDTHS):
            w_scr[:, dst:dst + n_src] = w_ref[:, src:src + n_src].astype(BF16)
            dst += n_dst

    xb = x_ref[...].astype(BF16)
    off = 0
    for o_ref in o_refs:
        n = o_ref.shape[-1]
        o_ref[...] = jnp.dot(xb, w_scr[:, off:off + n],
                             preferred_element_type=F32).astype(o_ref.dtype)
        off += n


def _in_proj(x2d, w_in, layer):
    t = x2d.shape[0]
    tm = 1024
    return pl.pallas_call(
        _proj_kernel,
        grid=(t // tm,),
        in_specs=[pl.BlockSpec((tm, D_MODEL), lambda i: (i, 0)),
                  pl.BlockSpec((None, D_MODEL, w_in.shape[2]), lambda i: (layer, 0, 0),
                               pl.Buffered(1))],
        out_specs=[pl.BlockSpec((tm, n), lambda i: (i, 0)) for n in PROJ_WIDTHS],
        out_shape=[jax.ShapeDtypeStruct((t, n), dt) for n, dt in zip(PROJ_WIDTHS, PROJ_DTYPES)],
        scratch_shapes=[pltpu.VMEM((D_MODEL, sum(PROJ_WIDTHS)), BF16)],
        compiler_params=_cparams(("arbitrary",)),
        name="in_proj",
    )(x2d, w_in)


def _gla_setup(p_ref, lr_ref, wlr_ref, blr_ref, ng_ref, o_ref, st_ref):
    c = GLA_CHUNK
    nh, dk, dv = GLA_HEADS, GLA_DK, GLA_DV

    rb = GLA_ROWS
    ncb = rb // c
    ri = lax.broadcasted_iota(jnp.int32, (rb, rb), 0)
    ci = lax.broadcasted_iota(jnp.int32, (rb, rb), 1)
    tri = (ci <= ri).astype(BF16)
    klane_head = lax.broadcasted_iota(jnp.int32, (1, nh * dk), 1) // dk
    vlane_head = lax.broadcasted_iota(jnp.int32, (1, nh * dv), 1) // dv
    strow_head = lax.broadcasted_iota(jnp.int32, (nh * dv, 1), 0) // dv
    st_mask = strow_head == klane_head
    r4 = lax.broadcasted_iota(jnp.int32, (nh * c, c), 0) % c
    c4 = lax.broadcasted_iota(jnp.int32, (nh * c, c), 1)
    causal4 = c4 <= r4
    gi = lax.broadcasted_iota(jnp.int32, (nh * dv, nh * dv), 0) // dv
    gj = lax.broadcasted_iota(jnp.int32, (nh * dv, nh * dv), 1) // dv
    gmean = jnp.where(gi == gj, 1.0 / dv, 0.0).astype(BF16)
    wlr_hi, wlr_lo = _split_bf16(wlr_ref[...], 2)

    def one_seq(bb, rows):
        q = p_ref[bb, rows, 0:128].astype(F32) * (dk ** -0.5)
        k = p_ref[bb, rows, 128:256].astype(F32)
        vb = p_ref[bb, rows, 256:512]
        g = p_ref[bb, rows, 512:768].astype(F32)
        lr = lr_ref[bb, rows, :]
        lr_hi, lr_lo = _split_bf16(lr, 2)
        z = (jnp.dot(lr_hi, wlr_hi, preferred_element_type=F32)
             + jnp.dot(lr_hi, wlr_lo, preferred_element_type=F32)
             + jnp.dot(lr_lo, wlr_hi, preferred_element_type=F32)) + blr_ref[...]
        log_a = (jnp.minimum(z, 0.0) - jnp.log(1.0 + jnp.exp(-jnp.abs(z)))) * (1.0 / GLA_TAU)
        cumb = _dot_split_rhs(tri, log_a, 3)
        ends = [cumb[(j + 1) * c - 1:(j + 1) * c, :] for j in range(ncb)]
        starts = [jnp.zeros_like(ends[0])] + ends[:-1]
        cum = cumb - jnp.concatenate([jnp.broadcast_to(s0, (c, nh * dk)) for s0 in starts], axis=0)
        lasts = [e - s0 for e, s0 in zip(ends, starts)]
        cl = jnp.concatenate([jnp.broadcast_to(x, (c, nh * dk)) for x in lasts], axis=0)
        q_dec = q * jnp.exp(cum)
        k_inv = (k * jnp.exp(-cum)).astype(BF16)
        k_end = (k * jnp.exp(cl - cum)).astype(BF16)
        st = st_ref[bb]
        outs = []
        for j in range(ncb):
            sl = slice(j * c, (j + 1) * c)
            qd = q_dec[sl]
            qs = jnp.concatenate([jnp.where(klane_head == h, qd, 0.0) for h in range(nh)],
                                 axis=0).astype(BF16)
            att = jnp.where(causal4, _dot_nt(qs, k_inv[sl]), 0.0)
            r = jnp.dot(att.astype(BF16), vb[sl], preferred_element_type=F32)
            o = jnp.where(vlane_head == 0, r[0:c], 0.0)
            for h in range(1, nh):
                o = o + jnp.where(vlane_head == h, r[h * c:(h + 1) * c], 0.0)
            outs.append(o + _dot_nt(qd.astype(BF16), st.astype(BF16)))
            d_st = _dot_tn(vb[sl], k_end[sl])
            st = st * jnp.exp(lasts[j]) + jnp.where(st_mask, d_st, 0.0)
        st_ref[bb] = st
        o = jnp.concatenate(outs, axis=0)
        ms = _dot_split_lhs(o * o, gmean, 2)
        o = o * lax.rsqrt(ms + RMS_EPS) * ng_ref[...]
        o_ref[bb, rows, :] = (o * (g * _sigmoid(g))).astype(o_ref.dtype)

    return one_seq


def _ssd_setup(p_ref, dt_ref, cw_ref, cb_ref, alog_ref, dtb_ref, dsk_ref, ng_ref, o_ref, st_ref,
               halo_ref):
    c = SSD_CHUNK
    n_st = SSD_STATE
    halo = 2 * SUBLANES

    ri = lax.broadcasted_iota(jnp.int32, (c, c), 0)
    ci = lax.broadcasted_iota(jnp.int32, (c, c), 1)
    causal = ci <= ri
    tri = causal.astype(BF16)
    upper = (ri <= ci).astype(BF16)
    lane_head = lax.broadcasted_iota(jnp.int32, (1, W_MIX), 1) // SSD_HEADDIM
    lane_group = lane_head // (SSD_HEADS // SSD_GROUPS)
    eh = lax.broadcasted_iota(jnp.int32, (LANES, W_MIX), 0)
    el = lax.broadcasted_iota(jnp.int32, (LANES, W_MIX), 1) // SSD_HEADDIM
    expand = (eh == el).astype(BF16)
    row8 = lax.broadcasted_iota(jnp.int32, (8, 3 * W_MIX), 0)
    a_c = -jnp.exp(alog_ref[...])

    def one_chunk(n, bb):
        rows = slice(n * c, (n + 1) * c)
        cur = p_ref[bb, rows, 256:1024].astype(F32)
        before = halo_ref[bb] if n == 0 else p_ref[bb, n * c - halo:n * c, 256:1024]
        prev8 = before.astype(F32)[SUBLANES:]
        acc = cur * cw_ref[3:4, :] + cb_ref[...]
        for s in (1, 2, 3):
            sh = pltpu.roll(cur, s, axis=0)
            top = jnp.where(row8 < s, pltpu.roll(prev8, s, axis=0), sh[0:8])
            sh = jnp.concatenate([top, sh[8:]], axis=0)
            acc = acc + sh * cw_ref[3 - s:4 - s, :]
        xbc = acc * _sigmoid(acc)
        x = xbc[:, 0:256]
        bm = xbc[:, 256:512].astype(BF16)
        cm = xbc[:, 512:768].astype(BF16)

        dt_c = _softplus(dt_ref[bb, rows, :] + dtb_ref[...])
        da_c = dt_c * a_c
        cum_c = _dot_split_rhs(tri, da_c, 3)
        cum_r = _dot_split_lhs(da_c, upper, 3, dot=_dot_tn)
        both_x = _dot_split_lhs(jnp.concatenate([dt_c, cum_c], axis=0), expand, 3)
        dt_x = both_x[0:c]
        cum_x = both_x[c:2 * c]
        cl_x = cum_x[c - 1:c, :]
        x_dt = x * dt_x
        x_dt_b = x_dt.astype(BF16)
        xw_b = (x_dt * jnp.exp(cl_x - cum_x)).astype(BF16)

        y = x * dsk_ref[...]
        y_off = jnp.zeros((c, W_MIX), F32)
        for g in range(SSD_GROUPS):
            bg = bm[:, g * n_st:(g + 1) * n_st]
            cg = cm[:, g * n_st:(g + 1) * n_st]
            cb = _dot_nt(cg, bg)
            for r in range(SSD_HEADS // SSD_GROUPS):
                h = g * (SSD_HEADS // SSD_GROUPS) + r
                diff = cum_c[:, h:h + 1] - cum_r[h:h + 1, :]
                dec = jnp.exp(jnp.where(causal, diff, -jnp.inf))
                m = (cb * dec).astype(BF16)
                yh = jnp.dot(m, x_dt_b, preferred_element_type=F32)
                y = y + jnp.where(lane_head == h, yh, 0.0)
            st = st_ref[bb, g]
            y_off = y_off + jnp.where(lane_group == g,
                                      jnp.dot(cg, st.astype(BF16), preferred_element_type=F32), 0.0)
            d_st = _dot_tn(bg, xw_b)
            st_ref[bb, g] = st * jnp.exp(cl_x) + jnp.where(lane_group == g, d_st, 0.0)
        y = y + y_off * jnp.exp(cum_x)
        zg = p_ref[bb, rows, 0:256].astype(F32)
        y = y * (zg * _sigmoid(zg))
        outs = []
        for g in range(SSD_GROUPS):
            yg = y[:, g * 128:(g + 1) * 128]
            ms = jnp.mean(yg * yg, axis=-1, keepdims=True)
            outs.append(yg * lax.rsqrt(ms + RMS_EPS))
        o_ref[bb, rows, :] = (jnp.concatenate(outs, axis=-1) * ng_ref[...]).astype(o_ref.dtype)

    return one_chunk


def _recurrent_kernel(pc_ref, ccw_ref, pg_ref, lr_ref, wlr_ref, blr_ref, gng_ref,
                      ps_ref, dt_ref, cw_ref, cb_ref, alog_ref, dtb_ref, dsk_ref, sng_ref,
                      oc_ref, og_ref, os_ref, gst_ref, sst_ref, halo_ref, chalo_ref):
    @pl.when(pl.program_id(1) == 0)
    def _():
        gst_ref[...] = jnp.zeros_like(gst_ref)
        sst_ref[...] = jnp.zeros_like(sst_ref)
        halo_ref[...] = jnp.zeros_like(halo_ref)
        chalo_ref[...] = jnp.zeros_like(chalo_ref)

    gla_rows = _gla_setup(pg_ref, lr_ref, wlr_ref, blr_ref, gng_ref, og_ref, gst_ref)
    ssd_chunk = _ssd_setup(ps_ref, dt_ref, cw_ref, cb_ref, alog_ref, dtb_ref, dsk_ref, sng_ref,
                           os_ref, sst_ref, halo_ref)
    row8 = lax.broadcasted_iota(jnp.int32, (SUBLANES, W_MIX), 0)
    for bb in range(pg_ref.shape[0]):
        gla_rows(bb, slice(0, GLA_ROWS))
        for n in range(GLA_ROWS // SSD_CHUNK):
            ssd_chunk(n, bb)
        halo_ref[bb] = ps_ref[bb, GLA_ROWS - halo_ref.shape[1]:GLA_ROWS, 256:1024]

        u = pc_ref[bb, :, 0:W_MIX].astype(F32)
        gb = pc_ref[bb, :, W_MIX:2 * W_MIX].astype(F32)
        gc = pc_ref[bb, :, 2 * W_MIX:3 * W_MIX].astype(F32)
        cu = gc * u
        prev8 = chalo_ref[bb]
        acc = cu * ccw_ref[2:3, :]
        for s in (1, 2):
            sh = pltpu.roll(cu, s, axis=0)
            top = jnp.where(row8 < s, pltpu.roll(prev8, s, axis=0), sh[0:SUBLANES])
            acc = acc + jnp.concatenate([top, sh[SUBLANES:]], axis=0) * ccw_ref[2 - s:3 - s, :]
        oc_ref[bb] = (gb * acc).astype(oc_ref.dtype)
        chalo_ref[bb] = cu[GLA_ROWS - SUBLANES:GLA_ROWS]


def _recurrent_mixers(pc, sconv_w, pg, plr, w_lr_pad, b_lr, gla_norm_g4,
                      ps, pdt, conv_w, conv_b, a_log_c, dt_bias_c, d_x, ssd_norm_g):
    b, s, _ = pg.shape
    nb = REC_SEQS
    rb = GLA_ROWS
    seq = lambda i, j: (i, j, 0)
    full2 = lambda i, j: (0, 0)
    return pl.pallas_call(
        _recurrent_kernel,
        grid=(b // nb, s // rb),
        in_specs=[pl.BlockSpec((nb, rb, pc.shape[2]), seq),
                  pl.BlockSpec((3, W_MIX), full2),
                  pl.BlockSpec((nb, rb, pg.shape[2]), seq),
                  pl.BlockSpec((nb, rb, LANES), seq),
                  pl.BlockSpec((LANES, LANES), full2),
                  pl.BlockSpec((1, LANES), full2),
                  pl.BlockSpec((1, W_MIX), full2),
                  pl.BlockSpec((nb, rb, ps.shape[2]), seq),
                  pl.BlockSpec((nb, rb, LANES), seq),
                  pl.BlockSpec((SSD_CONV_K, 3 * W_MIX), full2),
                  pl.BlockSpec((1, 3 * W_MIX), full2),
                  pl.BlockSpec((1, LANES), full2),
                  pl.BlockSpec((1, LANES), full2),
                  pl.BlockSpec((1, W_MIX), full2),
                  pl.BlockSpec((1, W_MIX), full2)],
        out_specs=[pl.BlockSpec((nb, rb, W_MIX), seq)] * 3,
        out_shape=[jax.ShapeDtypeStruct((b, s, W_MIX), BF16)] * 3,
        scratch_shapes=[pltpu.VMEM((nb, GLA_HEADS * GLA_DV, GLA_HEADS * GLA_DK), F32),
                        pltpu.VMEM((nb, SSD_GROUPS, SSD_STATE, W_MIX), F32),
                        pltpu.VMEM((nb, 2 * SUBLANES, 3 * W_MIX), BF16),
                        pltpu.VMEM((nb, SUBLANES, W_MIX), F32)],
        compiler_params=_cparams(("parallel", "arbitrary")),
        name="conv_gla_ssd_mixers",
    )(pc, sconv_w, pg, plr, w_lr_pad, b_lr, gla_norm_g4, ps, pdt, conv_w, conv_b, a_log_c,
      dt_bias_c, d_x, ssd_norm_g)


DIFF_TQ = 256
DIFF_TK = 256
LOG2E = 1.4426950408889634
DIFF_VPAD = DIFF_DV + 16
DIFF_SEQS = 4


def _diff_kernel(q_ref, k_ref, v_ref, lam_ref, ng_ref, o_ref,
                 vt_ref, qs_ref, st_ref, m_ref, acc_ref, *, lam_init):
    tq, tk = DIFF_TQ, DIFF_TK
    nh, dv = DIFF_HEADS, DIFF_DV
    nhc = 2 * nh
    s_len = k_ref.shape[1]
    i = pl.program_id(1)
    seqs = range(q_ref.shape[0])

    @pl.when(i == 0)
    def _():
        for bb in seqs:
            for cblk in range(s_len // tk):
                cols = slice(cblk * tk, (cblk + 1) * tk)
                vt = v_ref[bb, cols, :].astype(F32).T.astype(BF16)
                for h in range(nh):
                    vt_ref[bb, h, 0:dv, cols] = vt[h * dv:(h + 1) * dv]
        vt_ref[:, :, dv:, :] = jnp.ones((len(seqs), nh, DIFF_VPAD - dv, s_len), BF16)

    qlane = lax.broadcasted_iota(jnp.int32, (1, W_MIX), 1) // DIFF_DQK
    for bb in seqs:
        q = q_ref[bb].astype(F32) * (DIFF_DQK ** -0.5 * LOG2E)
        for hc in range(nhc):
            qs_ref[bb, hc * tq:(hc + 1) * tq, :] = jnp.where(qlane == hc, q, 0.0).astype(BF16)
    m_ref[...] = jnp.full_like(m_ref, -jnp.inf)
    acc_ref[...] = jnp.zeros_like(acc_ref)
    krow = lax.broadcasted_iota(jnp.int32, (tk, nhc * tq), 0)
    qcol = lax.broadcasted_iota(jnp.int32, (tk, nhc * tq), 1) % tq
    diag_ok = krow <= qcol

    def scores(j, slot):
        k0 = pl.multiple_of(j * tk, tk)
        for bb in seqs:
            st_ref[bb, slot] = _dot_nt(k_ref[bb, pl.ds(k0, tk), :], qs_ref[bb])

    def softmax_pv(j, slot, masked):
        for bb in seqs:
            softmax_pv_seq(bb, j, slot, masked)

    def softmax_pv_seq(bb, j, slot, masked):
        k0 = pl.multiple_of(j * tk, tk)
        st = st_ref[bb, slot]
        if masked:
            st = jnp.where(diag_ok, st, -jnp.inf)
        m_prev = m_ref[bb]
        m_new = jnp.maximum(m_prev, jnp.max(st, axis=0, keepdims=True))
        alpha = jnp.exp2(m_prev - m_new)
        p = jnp.exp2(st - m_new)
        m_ref[bb] = m_new
        pb = p.astype(BF16)
        for hc in range(nhc):
            h = hc // 2
            lanes = slice(hc * tq, (hc + 1) * tq)
            pv = jnp.dot(vt_ref[bb, h, :, pl.ds(k0, tk)], pb[:, lanes],
                         preferred_element_type=F32)
            acc_ref[bb, hc] = acc_ref[bb, hc] * alpha[:, lanes] + pv

    scores(0, 0)
    n_pairs = i // 2

    def pair_step(u, carry):
        scores(2 * u + 1, 1)
        softmax_pv(2 * u, 0, False)
        scores(2 * u + 2, 0)
        softmax_pv(2 * u + 1, 1, False)
        return carry

    lax.fori_loop(0, n_pairs, pair_step, 0)

    @pl.when(i % 2 == 0)
    def _():
        softmax_pv(i, 0, True)

    @pl.when(i % 2 == 1)
    def _():
        scores(i, 1)
        softmax_pv(i - 1, 0, False)
        softmax_pv(i, 1, True)

    lam = (jnp.exp(jnp.sum(lam_ref[0:1, :] * lam_ref[1:2, :], axis=-1, keepdims=True))
           - jnp.exp(jnp.sum(lam_ref[2:3, :] * lam_ref[3:4, :], axis=-1, keepdims=True))
           + lam_init)
    for bb in seqs:
        heads = []
        for h in range(nh):
            o1 = acc_ref[bb, 2 * h, 0:dv] / acc_ref[bb, 2 * h, dv:dv + 1]
            o2 = acc_ref[bb, 2 * h + 1, 0:dv] / acc_ref[bb, 2 * h + 1, dv:dv + 1]
            oh = o1 - lam * o2
            ms = jnp.mean(oh * oh, axis=0, keepdims=True)
            heads.append(oh * lax.rsqrt(ms + RMS_EPS))
        o = jnp.concatenate(heads, axis=0).T
        o_ref[bb] = (o * ng_ref[...] * (1.0 - lam_init)).astype(o_ref.dtype)


def _diff_mixer(pd, lam_vecs, norm_g4, lam_init):
    b, s, _ = pd.shape
    tq = DIFF_TQ
    nb = DIFF_SEQS
    return pl.pallas_call(
        functools.partial(_diff_kernel, lam_init=lam_init),
        grid=(b // nb, s // tq),
        in_specs=[pl.BlockSpec((nb, tq, W_MIX), lambda bi, i: (bi, i, 0)),
                  pl.BlockSpec((nb, s, W_MIX), lambda bi, i: (bi, 0, 1)),
                  pl.BlockSpec((nb, s, W_MIX), lambda bi, i: (bi, 0, 2)),
                  pl.BlockSpec((4, LANES), lambda bi, i: (0, 0)),
                  pl.BlockSpec((1, W_MIX), lambda bi, i: (0, 0))],
        out_specs=pl.BlockSpec((nb, tq, W_MIX), lambda bi, i: (bi, i, 0)),
        out_shape=jax.ShapeDtypeStruct((b, s, W_MIX), BF16),
        scratch_shapes=[pltpu.VMEM((nb, DIFF_HEADS, DIFF_VPAD, s), BF16),
                        pltpu.VMEM((nb, 2 * DIFF_HEADS * tq, W_MIX), BF16),
                        pltpu.VMEM((nb, 2, DIFF_TK, 2 * DIFF_HEADS * tq), F32),
                        pltpu.VMEM((nb, 1, 2 * DIFF_HEADS * tq), F32),
                        pltpu.VMEM((nb, 2 * DIFF_HEADS, DIFF_VPAD, tq), F32)],
        compiler_params=_cparams(("parallel", "arbitrary")),
        name="diff_attn",
    )(pd, pd, pd, lam_vecs, norm_g4)


def _oproj_kernel(yc_ref, yg_ref, ys_ref, yd_ref, x_ref, wo_ref, g_ref, b_ref, wr_ref,
                  xo_ref, xp_ref, route_ref, cnt_ref):
    @pl.when(pl.program_id(0) == 0)
    def _():
        cnt_ref[...] = jnp.zeros_like(cnt_ref)

    part = x_ref.shape[0] // OPROJ_PARTS
    hits_sum = jnp.zeros(cnt_ref.shape, F32)
    for r in range(OPROJ_PARTS):
        rows = slice(r * part, (r + 1) * part)
        mix = jnp.concatenate([yc_ref[rows, :], yg_ref[rows, :], ys_ref[rows, :], yd_ref[rows, :]],
                              axis=-1)
        h = ALPHA * x_ref[rows, :] + jnp.dot(mix, wo_ref[...], preferred_element_type=F32)
        xn = _layer_norm(h, g_ref[...], b_ref[...])
        xo_ref[rows, :] = xn
        xp_ref[rows, :] = _pack_halves(xn)

        xn_hi, xn_lo = _split_bf16(xn, 2)
        both = _dot(xn_hi, wr_ref[...])
        logits = both[:, 0:LANES] + both[:, LANES:2 * LANES] + _dot(xn_lo, wr_ref[:, 0:LANES])
        lane = lax.broadcasted_iota(jnp.int32, logits.shape, 1).astype(F32)
        neg = -jnp.inf
        big = float(LANES)
        lg = jnp.where(lane < N_GROUPS, logits, neg)
        mg = jnp.max(lg, axis=-1, keepdims=True)
        sg = jnp.sum(jnp.exp(lg - mg), axis=-1, keepdims=True)
        grp = jnp.min(jnp.where(lg == mg, lane, big), axis=-1, keepdims=True)
        p_grp = 1.0 / sg
        lo = N_GROUPS + EXPERTS_PER_GROUP * grp
        in_g = jnp.logical_and(lane >= lo, lane < lo + EXPERTS_PER_GROUP)
        le = jnp.where(in_g, logits, neg)
        me = jnp.max(le, axis=-1, keepdims=True)
        ee = jnp.exp(le - me)
        pe = ee / jnp.sum(ee, axis=-1, keepdims=True)
        pe = jnp.where(in_g, pe, -1.0)
        p1 = jnp.max(pe, axis=-1, keepdims=True)
        i1 = jnp.min(jnp.where(pe == p1, lane, big), axis=-1, keepdims=True)
        pe2 = jnp.where(lane == i1, -1.0, pe)
        p2 = jnp.max(pe2, axis=-1, keepdims=True)
        i2 = jnp.min(jnp.where(pe2 == p2, lane, big), axis=-1, keepdims=True)
        den = p1 + p2
        g1 = p_grp * p1 / den
        g2 = p_grp * p2 / den
        e1 = i1 - N_GROUPS
        e2 = i2 - N_GROUPS
        route_ref[rows, :] = jnp.where(lane == 0, e1, jnp.where(lane == 1, e2, jnp.where(
            lane == 2, g1, jnp.where(lane == 3, g2, 0.0))))
        hits = jnp.where(lane == e1, 1.0, 0.0) + jnp.where(lane == e2, 1.0, 0.0)
        hits_sum = hits_sum + jnp.sum(hits, axis=0, keepdims=True)
    cnt_ref[...] += hits_sum


def _out_proj(ys, x2d, w_o, layer, ln_g, ln_b, w_route):
    t = x2d.shape[0]
    tm = 1024
    row = lambda i: (i, 0)
    full = lambda i: (0, 0)
    return pl.pallas_call(
        _oproj_kernel,
        grid=(t // tm,),
        in_specs=[pl.BlockSpec((tm, W_MIX), row)] * 4 + [
            pl.BlockSpec((tm, D_MODEL), row),
            pl.BlockSpec((None, D_MODEL, D_MODEL), lambda i: (layer, 0, 0)),
            pl.BlockSpec((1, D_MODEL), full),
            pl.BlockSpec((1, D_MODEL), full),
            pl.BlockSpec((D_MODEL, 2 * LANES), full)],
        out_specs=[pl.BlockSpec((tm, D_MODEL), row), pl.BlockSpec((tm, D_MODEL // 2), row),
                   pl.BlockSpec((tm, LANES), row), pl.BlockSpec((1, LANES), full)],
        out_shape=[jax.ShapeDtypeStruct((t, D_MODEL), F32),
                   jax.ShapeDtypeStruct((t, D_MODEL // 2), U32),
                   jax.ShapeDtypeStruct((t, LANES), F32),
                   jax.ShapeDtypeStruct((1, LANES), F32)],
        compiler_params=_cparams(("arbitrary",)),
        name="out_proj_ln_router",
    )(*ys, x2d, w_o, ln_g, ln_b, w_route)


PLAN_TILE = 1024


def _plan_kernel(route_ref, cnt_ref, dest_ref, meta_ref, carry_ref, pstart_ref):
    tm = route_ref.shape[0]
    lane = lax.broadcasted_iota(jnp.int32, (1, LANES), 1).astype(F32)

    @pl.when(pl.program_id(0) == 0)
    def _():
        cnt = cnt_ref[...]
        padded = jnp.ceil(cnt * (1.0 / MOE_BLK)) * MOE_BLK
        li = lax.broadcasted_iota(jnp.int32, (LANES, LANES), 0)
        lj = lax.broadcasted_iota(jnp.int32, (LANES, LANES), 1)
        before = (li < lj).astype(F32)
        pstart = jnp.dot(jnp.broadcast_to(padded, (8, LANES)), before, precision=HI,
                         preferred_element_type=F32)[0:1]
        pstart_ref[...] = pstart
        carry_ref[...] = jnp.zeros_like(carry_ref)
        meta_ref[...] = jnp.concatenate(
            [pstart + padded, pstart, cnt, jnp.zeros((5, LANES), F32)], axis=0)

    oh0 = jnp.where(lane == route_ref[:, 0:1], 1.0, 0.0)
    oh1 = jnp.where(lane == route_ref[:, 1:2], 1.0, 0.0)
    both = oh0 + oh1
    ri = lax.broadcasted_iota(jnp.int32, (tm, tm), 0)
    ci = lax.broadcasted_iota(jnp.int32, (tm, tm), 1)
    earlier = (ci < ri).astype(BF16)
    base = (jnp.dot(earlier, both.astype(BF16), preferred_element_type=F32)
            + carry_ref[...] + pstart_ref[...])
    d0 = jnp.sum(oh0 * base, axis=-1, keepdims=True)
    d1 = jnp.sum(oh1 * base, axis=-1, keepdims=True)
    dest = jnp.where(lane == 0, d0, jnp.where(lane == 1, d1, 0.0))
    dest_ref[...] = dest.T[0:2, :].astype(jnp.int32)
    carry_ref[...] += jnp.sum(both, axis=0, keepdims=True)


def _dispatch_plan(route, cnt, n_blocks):
    t = route.shape[0]
    tm = PLAN_TILE
    blk = MOE_BLK
    dest, meta = pl.pallas_call(
        _plan_kernel,
        grid=(t // tm,),
        in_specs=[pl.BlockSpec((tm, LANES), lambda i: (i, 0)),
                  pl.BlockSpec((1, LANES), lambda i: (0, 0))],
        out_specs=[pl.BlockSpec((2, tm), lambda i: (0, i)),
                   pl.BlockSpec((8, LANES), lambda i: (0, 0))],
        out_shape=[jax.ShapeDtypeStruct((2, t), jnp.int32),
                   jax.ShapeDtypeStruct((8, LANES), F32)],
        scratch_shapes=[pltpu.VMEM((1, LANES), F32), pltpu.VMEM((1, LANES), F32)],
        compiler_params=_cparams(("arbitrary",)),
        name="moe_plan",
    )(route, cnt)
    meta_i = meta[:, :N_EXPERTS].astype(jnp.int32)
    pad_end, pad_start, seg_end = meta_i[0], meta_i[1], meta_i[1] + meta_i[2]
    starts = jnp.arange(n_blocks, dtype=jnp.int32)[:, None] * blk
    member = jnp.logical_and(starts >= pad_start[None, :], starts < pad_end[None, :])
    expert_ids = jnp.arange(N_EXPERTS, dtype=jnp.int32)[None, :]
    block_e = jnp.where(jnp.any(member, axis=1), jnp.sum(jnp.where(member, expert_ids, 0), axis=1),
                        N_EXPERTS - 1)
    n_used = (pad_end[N_EXPERTS - 1] // blk).reshape(1)
    n_valid = jnp.clip(jnp.sum(jnp.where(member, seg_end[None, :], 0), axis=1) - starts[:, 0],
                       0, blk)
    return dest.reshape(2 * t), block_e, n_used, n_valid


def _ffn_kernel(be_ref, nu_ref, nv_ref, xs_ref, wg_ref, wu_ref, wd_ref, y_ref,
                wgb_ref, wub_ref, wdb_ref):
    i = pl.program_id(0)
    prev = be_ref[jnp.maximum(i - 1, 0)]

    @pl.when(jnp.logical_or(i == 0, be_ref[i] != prev))
    def _():
        wgb_ref[...] = wg_ref[...].astype(BF16)
        wub_ref[...] = wu_ref[...].astype(BF16)
        wdb_ref[...] = wd_ref[...].astype(BF16)

    @pl.when(i < nu_ref[0])
    def _():
        half = xs_ref.shape[0] // 2
        for r in range(2):
            rows = slice(r * half, (r + 1) * half)
            row = lax.broadcasted_iota(jnp.int32, (half, 1), 0) + r * half
            xp = jnp.where(row < nv_ref[i], xs_ref[rows, :], U32(0))
            x_hi, x_lo = _unpack_halves(xp)
            xb = jnp.concatenate([x_hi.astype(BF16), x_lo.astype(BF16)], axis=1)
            a = jnp.dot(xb, wgb_ref[...], preferred_element_type=F32)
            u = jnp.dot(xb, wub_ref[...], preferred_element_type=F32)
            h = (a * _sigmoid(a) * u).astype(BF16)
            y_ref[rows, :] = _pack_halves(jnp.dot(h, wdb_ref[...], preferred_element_type=F32))

    @pl.when(i >= nu_ref[0])
    def _():
        y_ref[...] = jnp.zeros_like(y_ref)


def _expert_ffn(xs, block_e, n_used, n_valid, w_gate, w_up, w_down, layer):
    n_slots = xs.shape[0]
    blk = MOE_BLK
    w_map = lambda i, be, nu, nv: (layer, be[i], 0, 0)
    grid_spec = pltpu.PrefetchScalarGridSpec(
        num_scalar_prefetch=3,
        grid=(n_slots // blk,),
        in_specs=[pl.BlockSpec((blk, D_MODEL // 2),
                               lambda i, be, nu, nv: (jnp.minimum(i, nu[0] - 1), 0)),
                  pl.BlockSpec((None, None, D_MODEL, D_EXPERT), w_map),
                  pl.BlockSpec((None, None, D_MODEL, D_EXPERT), w_map),
                  pl.BlockSpec((None, None, D_EXPERT, D_MODEL), w_map)],
        out_specs=pl.BlockSpec((blk, D_MODEL // 2), lambda i, be, nu, nv: (i, 0)),
        scratch_shapes=[pltpu.VMEM((D_MODEL, D_EXPERT), BF16),
                        pltpu.VMEM((D_MODEL, D_EXPERT), BF16),
                        pltpu.VMEM((D_EXPERT, D_MODEL), BF16)],
    )
    return pl.pallas_call(
        _ffn_kernel,
        grid_spec=grid_spec,
        out_shape=jax.ShapeDtypeStruct((n_slots, D_MODEL // 2), U32),
        compiler_params=_cparams(("arbitrary",)),
        name="expert_ffn",
    )(block_e, n_used, n_valid, xs, w_gate, w_up, w_down)


SC_CORES = 2
SC_SUBCORES = 16
SC_ROWS = 64


def _sc_gather_rows(table, idx):
    b = idx.shape[0]
    d = table.shape[1]
    per_w = b // (SC_CORES * SC_SUBCORES)
    mesh = plsc.VectorSubcoreMesh(core_axis_name="c", subcore_axis_name="s")

    n_chunks = per_w // SC_ROWS

    @functools.partial(
        pl.kernel, mesh=mesh,
        out_type=jax.ShapeDtypeStruct((b, d), table.dtype),
        scratch_types=[pltpu.VMEM((SC_ROWS,), jnp.int32), pltpu.VMEM((SC_ROWS,), jnp.int32),
                       pltpu.VMEM((SC_ROWS, d), table.dtype),
                       pltpu.VMEM((SC_ROWS, d), table.dtype),
                       pltpu.SemaphoreType.DMA, pltpu.SemaphoreType.DMA,
                       pltpu.SemaphoreType.DMA, pltpu.SemaphoreType.DMA],
        name="sc_gather_rows",
    )
    def gather(table_hbm, idx_hbm, out_hbm, idx0, idx1, rows0, rows1, gs0, gs1, ws0, ws1):
        idx_v, rows_v, gsem, wsem = (idx0, idx1), (rows0, rows1), (gs0, gs1), (ws0, ws1)
        wid = lax.axis_index("s") * SC_CORES + lax.axis_index("c")
        base = wid * per_w

        def rows_of(c):
            return pl.ds(pl.multiple_of(base + c * SC_ROWS, SC_ROWS), SC_ROWS)

        def start_gather(c, s):
            pltpu.sync_copy(idx_hbm.at[rows_of(c)], idx_v[s])
            pltpu.async_copy(table_hbm.at[idx_v[s]], rows_v[s], gsem[s])

        def write_back(c, s):
            pltpu.make_async_copy(table_hbm.at[idx_v[s]], rows_v[s], gsem[s]).wait()
            pltpu.async_copy(rows_v[s], out_hbm.at[rows_of(c)], wsem[s]).wait()

        start_gather(0, 0)

        @pl.loop(0, n_chunks, step=2)
        def _(c):
            start_gather(c + 1, 1)
            write_back(c, 0)

            @pl.when(c + 2 < n_chunks)
            def _():
                start_gather(c + 2, 0)

            write_back(c + 1, 1)

    return gather(table, idx)


def _sc_scatter_rows(x2d, idx, n_slots):
    t, d = x2d.shape
    per_w = t // (SC_CORES * SC_SUBCORES)
    mesh = plsc.VectorSubcoreMesh(core_axis_name="c", subcore_axis_name="s")

    @functools.partial(
        pl.kernel, mesh=mesh,
        out_type=jax.ShapeDtypeStruct((n_slots, d), x2d.dtype),
        scratch_types=[pltpu.VMEM((SC_ROWS,), jnp.int32), pltpu.VMEM((SC_ROWS,), jnp.int32),
                       pltpu.VMEM((SC_ROWS, d), x2d.dtype),
                       pltpu.SemaphoreType.DMA, pltpu.SemaphoreType.DMA],
        name="sc_scatter_rows",
    )
    def scatter(x_hbm, idx_hbm, out_hbm, idx0, idx1, rows_v, s0, s1):
        wid = lax.axis_index("s") * SC_CORES + lax.axis_index("c")
        base = wid * per_w

        @pl.loop(0, per_w // SC_ROWS)
        def _(c):
            off = pl.multiple_of(base + c * SC_ROWS, SC_ROWS)
            pltpu.sync_copy(x_hbm.at[pl.ds(off, SC_ROWS)], rows_v)
            pltpu.sync_copy(idx_hbm.at[pl.ds(off, SC_ROWS)], idx0)
            pltpu.sync_copy(idx_hbm.at[pl.ds(t + off, SC_ROWS)], idx1)
            cp0 = pltpu.async_copy(rows_v, out_hbm.at[idx0], s0)
            cp1 = pltpu.async_copy(rows_v, out_hbm.at[idx1], s1)
            cp0.wait()
            cp1.wait()

    return scatter(x2d, idx)


def _combine_dense_kernel(route_ref, x_ref, y0_ref, y1_ref, g_ref, b_ref, o_ref):
    y0 = jnp.concatenate(_unpack_halves(y0_ref[...]), axis=1)
    y1 = jnp.concatenate(_unpack_halves(y1_ref[...]), axis=1)
    moe = route_ref[:, 2:3] * y0 + route_ref[:, 3:4] * y1
    h = ALPHA * x_ref[...] + moe
    o_ref[...] = _layer_norm(h, g_ref[...], b_ref[...])


def _combine_dense(route, x2d, yg, ln_g, ln_b):
    t = x2d.shape[0]
    tm = 1024
    nt = t // tm
    row = lambda i: (i, 0)
    full = lambda i: (0, 0)
    return pl.pallas_call(
        _combine_dense_kernel,
        grid=(nt,),
        in_specs=[pl.BlockSpec((tm, LANES), row),
                  pl.BlockSpec((tm, D_MODEL), row),
                  pl.BlockSpec((tm, D_MODEL // 2), row),
                  pl.BlockSpec((tm, D_MODEL // 2), lambda i: (i + nt, 0)),
                  pl.BlockSpec((1, D_MODEL), full),
                  pl.BlockSpec((1, D_MODEL), full)],
        out_specs=pl.BlockSpec((tm, D_MODEL), row),
        out_shape=jax.ShapeDtypeStruct((t, D_MODEL), F32),
        compiler_params=_cparams(("parallel",)),
        name="moe_combine_dense",
    )(route, x2d, yg, yg, ln_g, ln_b)


def _pad_cols(w, n):
    return jnp.pad(w, [(0, 0)] * (w.ndim - 1) + [(0, n - w.shape[-1])])


def kernel(x, w_in, conv_w, gla_w_lr, gla_b_lr, gla_norm_g, ssd_conv_w, ssd_conv_b, ssd_a_log,
           ssd_d, ssd_dt_bias, ssd_norm_g, diff_lq1, diff_lk1, diff_lq2, diff_lk2, diff_norm_g,
           w_o, ln1_g, ln1_b, router_g, router_e, w_gate, w_up, w_down, ln2_g, ln2_b):
    bsz, seq, d = x.shape
    t = bsz * seq
    n_assign = 2 * t
    n_blocks = (n_assign + N_EXPERTS * (MOE_BLK - 1)) // MOE_BLK + 1
    n_slots = n_blocks * MOE_BLK
    x2d = x.reshape(t, d)
    w_o_b = w_o.astype(BF16)
    for l in range(DEPTH):
        pc, pg, plr, ps, pdt, pd = [p.reshape(bsz, seq, -1) for p in _in_proj(x2d, w_in, l)]

        w_lr_pad = jnp.pad(gla_w_lr[l], ((0, LANES - GLA_RANK), (0, 0)))
        pad4 = lambda v: jnp.pad(v, (0, LANES - SSD_HEADS)).reshape(1, LANES)
        y_conv, y_gla, y_ssd = _recurrent_mixers(
            pc, conv_w[l], pg, plr, w_lr_pad, gla_b_lr[l].reshape(1, -1),
            jnp.tile(gla_norm_g[l], GLA_HEADS).reshape(1, -1),
            ps, pdt, ssd_conv_w[l], ssd_conv_b[l].reshape(1, -1),
            pad4(ssd_a_log[l]), pad4(ssd_dt_bias[l]),
            jnp.repeat(ssd_d[l], SSD_HEADDIM).reshape(1, -1), ssd_norm_g[l].reshape(1, -1))
        lam_vecs = jnp.pad(jnp.stack([diff_lq1[l], diff_lk1[l], diff_lq2[l], diff_lk2[l]]),
                           ((0, 0), (0, LANES - DIFF_DQK)))
        lam_init = 0.8 - 0.6 * math.exp(-0.3 * l)
        y_diff = _diff_mixer(pd, lam_vecs,
                             jnp.tile(diff_norm_g[l], DIFF_HEADS).reshape(1, -1), lam_init)

        w_route = _pad_cols(jnp.concatenate(
            [router_g[l], router_e[l].reshape(d, N_EXPERTS)], axis=1), LANES)
        w_route_hi = w_route.astype(BF16)
        w_route = jnp.concatenate(
            [w_route_hi, (w_route - w_route_hi.astype(F32)).astype(BF16)], axis=1)
        ys = [y.reshape(t, W_MIX) for y in (y_conv, y_gla, y_ssd, y_diff)]
        xn, xn_p, route, cnt = _out_proj(ys, x2d, w_o_b, l, ln1_g[l].reshape(1, -1),
                                         ln1_b[l].reshape(1, -1), w_route)

        idx, block_e, n_used, n_valid = _dispatch_plan(route, cnt, n_blocks)
        xs = _sc_scatter_rows(xn_p, idx, n_slots)
        y = _expert_ffn(xs, block_e, n_used, n_valid, w_gate, w_up, w_down, l)
        yg = _sc_gather_rows(y, idx)
        x2d = _combine_dense(route, xn, yg, ln2_g[l].reshape(1, -1), ln2_b[l].reshape(1, -1))
    return x2d.reshape(bsz, seq, d)
```

```python
import functools
import math

import jax
import jax.numpy as jnp
from jax import lax
from jax.experimental import pallas as pl
from jax.experimental.pallas import tpu as pltpu
from jax.experimental.pallas import tpu_sc as plsc

F32 = jnp.float32
BF16 = jnp.bfloat16
HI = lax.Precision.HIGHEST

D_MODEL = 1024
DEPTH = 2
W_MIX = 256
GLA_HEADS, GLA_DK, GLA_DV, GLA_RANK, GLA_TAU, GLA_CHUNK = 4, 32, 64, 16, 16.0, 64
GLA_ROWS = 256
REC_SEQS = 4
SSD_HEADS, SSD_GROUPS, SSD_HEADDIM, SSD_STATE, SSD_CONV_K, SSD_CHUNK = 4, 2, 64, 128, 4, 128
DIFF_HEADS, DIFF_DQK, DIFF_DV = 4, 32, 64
N_GROUPS, EXPERTS_PER_GROUP, N_EXPERTS, D_EXPERT = 4, 8, 32, 512
ALPHA = (2 * DEPTH) ** 0.25
LN_EPS = 1e-5
RMS_EPS = 1e-6

LANES = 128
SUBLANES = 8
PROJ_WIDTHS = (768, 768, 128, 1024, 128, 768)
PROJ_SRC_OFFSETS = (0, 768, 1536, 1552, 2576, 2580)
PROJ_SRC_WIDTHS = (768, 768, GLA_RANK, 1024, SSD_HEADS, 768)
PROJ_DTYPES = (BF16, BF16, F32, BF16, F32, BF16)
VMEM_LIMIT = 56 * 1024 * 1024

MOE_BLK = 512
OPROJ_PARTS = 2


def _cparams(sem):
    return pltpu.CompilerParams(dimension_semantics=sem, vmem_limit_bytes=VMEM_LIMIT)


def _sigmoid(x):
    return 1.0 / (1.0 + jnp.exp(-x))


def _softplus(x):
    return jnp.maximum(x, 0.0) + jnp.log(1.0 + jnp.exp(-jnp.abs(x)))


def _layer_norm(h, g, b):
    mu = jnp.mean(h, axis=-1, keepdims=True)
    d = h - mu
    var = jnp.mean(d * d, axis=-1, keepdims=True)
    return d * lax.rsqrt(var + LN_EPS) * g + b


def _dot_nt(a, b):
    return lax.dot_general(a, b, (((1,), (1,)), ((), ())), preferred_element_type=F32)


def _dot_tn(a, b, precision=None):
    return lax.dot_general(a, b, (((0,), (0,)), ((), ())), preferred_element_type=F32,
                           precision=precision)


def _split_bf16(x, parts):
    out = []
    for _ in range(parts - 1):
        hi = x.astype(BF16)
        out.append(hi)
        x = x - hi.astype(F32)
    out.append(x.astype(BF16))
    return out


def _dot(a, b):
    return jnp.dot(a, b, preferred_element_type=F32)


U32 = jnp.uint32


def _pack_halves(x):
    w = x.shape[1] // 2
    hi = lax.bitcast_convert_type(x[:, :w].astype(BF16).astype(F32), U32)
    lo = lax.bitcast_convert_type(x[:, w:].astype(BF16).astype(F32), U32)
    return hi | lax.shift_right_logical(lo, U32(16))


def _unpack_halves(p):
    hi = lax.bitcast_convert_type(p & U32(0xFFFF0000), F32)
    lo = lax.bitcast_convert_type(lax.shift_left(p, U32(16)), F32)
    return hi, lo


def _dot_split_lhs(a, b_exact, parts, dot=_dot):
    acc = None
    for term in _split_bf16(a, parts):
        d = dot(term, b_exact)
        acc = d if acc is None else acc + d
    return acc


def _dot_split_rhs(a_exact, b, parts):
    acc = None
    for term in _split_bf16(b, parts):
        d = jnp.dot(a_exact, term, preferred_element_type=F32)
        acc = d if acc is None else acc + d
    return acc


def _proj_kernel(layer, x_ref, w_ref, *refs):
    o_refs, w_scr, wf_scr, sem = refs[:-3], refs[-3], refs[-2], refs[-1]

    @pl.when(pl.program_id(0) == 0)
    def _():
        fetch = pltpu.make_async_copy(w_ref.at[layer], wf_scr, sem)
        fetch.start()
        w_scr[...] = jnp.zeros_like(w_scr)
        fetch.wait()
        dst = 0
        for src, n_src, n_dst in zip(PROJ_SRC_OFFSETS, PROJ_SRC_WIDTHS, PROJ_WIDTHS):
            w_scr[:, dst:dst + n_src] = wf_scr[:, src:src + n_src].astype(BF16)
            dst += n_dst

    xb = x_ref[...].astype(BF16)
    off = 0
    for o_ref in o_refs:
        n = o_ref.shape[-1]
        o_ref[...] = jnp.dot(xb, w_scr[:, off:off + n],
                             preferred_element_type=F32).astype(o_ref.dtype)
        off += n


def _in_proj(x2d, w_in, layer):
    t = x2d.shape[0]
    tm = 1024
    return pl.pallas_call(
        functools.partial(_proj_kernel, layer),
        grid=(t // tm,),
        in_specs=[pl.BlockSpec((tm, D_MODEL), lambda i: (i, 0)),
                  pl.BlockSpec(memory_space=pl.ANY)],
        out_specs=[pl.BlockSpec((tm, n), lambda i: (i, 0)) for n in PROJ_WIDTHS],
        out_shape=[jax.ShapeDtypeStruct((t, n), dt) for n, dt in zip(PROJ_WIDTHS, PROJ_DTYPES)],
        scratch_shapes=[pltpu.VMEM((D_MODEL, sum(PROJ_WIDTHS)), BF16),
                        pltpu.VMEM((D_MODEL, w_in.shape[2]), F32),
                        pltpu.SemaphoreType.DMA(())],
        compiler_params=_cparams(("arbitrary",)),
        name="in_proj",
    )(x2d, w_in)


def _gla_setup(p_ref, lr_ref, wlr_ref, blr_ref, ng_ref, o_ref, st_ref):
    c = GLA_CHUNK
    nh, dk, dv = GLA_HEADS, GLA_DK, GLA_DV

    rb = GLA_ROWS
    ncb = rb // c
    ri = lax.broadcasted_iota(jnp.int32, (rb, rb), 0)
    ci = lax.broadcasted_iota(jnp.int32, (rb, rb), 1)
    tri = (ci <= ri).astype(BF16)
    klane_head = lax.broadcasted_iota(jnp.int32, (1, nh * dk), 1) // dk
    vlane_head = lax.broadcasted_iota(jnp.int32, (1, nh * dv), 1) // dv
    strow_head = lax.broadcasted_iota(jnp.int32, (nh * dv, 1), 0) // dv
    st_mask = strow_head == klane_head
    r4 = lax.broadcasted_iota(jnp.int32, (nh * c, c), 0) % c
    c4 = lax.broadcasted_iota(jnp.int32, (nh * c, c), 1)
    causal4 = c4 <= r4
    gi = lax.broadcasted_iota(jnp.int32, (nh * dv, nh * dv), 0) // dv
    gj = lax.broadcasted_iota(jnp.int32, (nh * dv, nh * dv), 1) // dv
    gmean = jnp.where(gi == gj, 1.0 / dv, 0.0).astype(BF16)
    wlr_hi, wlr_lo = _split_bf16(wlr_ref[...], 2)

    def one_seq(bb, rows):
        q = p_ref[bb, rows, 0:128].astype(F32) * (dk ** -0.5)
        k = p_ref[bb, rows, 128:256].astype(F32)
        vb = p_ref[bb, rows, 256:512]
        g = p_ref[bb, rows, 512:768].astype(F32)
        lr = lr_ref[bb, rows, :]
        lr_hi, lr_lo = _split_bf16(lr, 2)
        z = (jnp.dot(lr_hi, wlr_hi, preferred_element_type=F32)
             + jnp.dot(lr_hi, wlr_lo, preferred_element_type=F32)
             + jnp.dot(lr_lo, wlr_hi, preferred_element_type=F32)) + blr_ref[...]
        log_a = (jnp.minimum(z, 0.0) - jnp.log(1.0 + jnp.exp(-jnp.abs(z)))) * (1.0 / GLA_TAU)
        cumb = _dot_split_rhs(tri, log_a, 3)
        ends = [cumb[(j + 1) * c - 1:(j + 1) * c, :] for j in range(ncb)]
        starts = [jnp.zeros_like(ends[0])] + ends[:-1]
        cum = cumb - jnp.concatenate([jnp.broadcast_to(s0, (c, nh * dk)) for s0 in starts], axis=0)
        lasts = [e - s0 for e, s0 in zip(ends, starts)]
        cl = jnp.concatenate([jnp.broadcast_to(x, (c, nh * dk)) for x in lasts], axis=0)
        q_dec = q * jnp.exp(cum)
        k_inv = (k * jnp.exp(-cum)).astype(BF16)
        k_end = (k * jnp.exp(cl - cum)).astype(BF16)
        st = st_ref[bb]
        outs = []
        for j in range(ncb):
            sl = slice(j * c, (j + 1) * c)
            qd = q_dec[sl]
            qs = jnp.concatenate([jnp.where(klane_head == h, qd, 0.0) for h in range(nh)],
                                 axis=0).astype(BF16)
            att = jnp.where(causal4, _dot_nt(qs, k_inv[sl]), 0.0)
            r = jnp.dot(att.astype(BF16), vb[sl], preferred_element_type=F32)
            o = jnp.where(vlane_head == 0, r[0:c], 0.0)
            for h in range(1, nh):
                o = o + jnp.where(vlane_head == h, r[h * c:(h + 1) * c], 0.0)
            outs.append(o + _dot_nt(qd.astype(BF16), st.astype(BF16)))
            d_st = _dot_tn(vb[sl], k_end[sl])
            st = st * jnp.exp(lasts[j]) + jnp.where(st_mask, d_st, 0.0)
        st_ref[bb] = st
        o = jnp.concatenate(outs, axis=0)
        ms = _dot_split_lhs(o * o, gmean, 2)
        o = o * lax.rsqrt(ms + RMS_EPS) * ng_ref[...]
        o_ref[bb, rows, :] = (o * (g * _sigmoid(g))).astype(o_ref.dtype)

    return one_seq


def _ssd_setup(p_ref, dt_ref, cw_ref, cb_ref, alog_ref, dtb_ref, dsk_ref, ng_ref, o_ref, st_ref,
               halo_ref):
    c = SSD_CHUNK
    n_st = SSD_STATE
    halo = 2 * SUBLANES

    ri = lax.broadcasted_iota(jnp.int32, (c, c), 0)
    ci = lax.broadcasted_iota(jnp.int32, (c, c), 1)
    causal = ci <= ri
    tri = causal.astype(BF16)
    upper = (ri <= ci).astype(BF16)
    lane_head = lax.broadcasted_iota(jnp.int32, (1, W_MIX), 1) // SSD_HEADDIM
    lane_group = lane_head // (SSD_HEADS // SSD_GROUPS)
    eh = lax.broadcasted_iota(jnp.int32, (LANES, W_MIX), 0)
    el = lax.broadcasted_iota(jnp.int32, (LANES, W_MIX), 1) // SSD_HEADDIM
    expand = (eh == el).astype(BF16)
    row8 = lax.broadcasted_iota(jnp.int32, (8, 3 * W_MIX), 0)
    a_c = -jnp.exp(alog_ref[...])

    def one_chunk(n, bb):
        rows = slice(n * c, (n + 1) * c)
        cur = p_ref[bb, rows, 256:1024].astype(F32)
        before = halo_ref[bb] if n == 0 else p_ref[bb, n * c - halo:n * c, 256:1024]
        prev8 = before.astype(F32)[SUBLANES:]
        acc = cur * cw_ref[3:4, :] + cb_ref[...]
        for s in (1, 2, 3):
            sh = pltpu.roll(cur, s, axis=0)
            top = jnp.where(row8 < s, pltpu.roll(prev8, s, axis=0), sh[0:8])
            sh = jnp.concatenate([top, sh[8:]], axis=0)
            acc = acc + sh * cw_ref[3 - s:4 - s, :]
        xbc = acc * _sigmoid(acc)
        x = xbc[:, 0:256]
        bm = xbc[:, 256:512].astype(BF16)
        cm = xbc[:, 512:768].astype(BF16)

        dt_c = _softplus(dt_ref[bb, rows, :] + dtb_ref[...])
        da_c = dt_c * a_c
        cum_c = _dot_split_rhs(tri, da_c, 3)
        cum_r = _dot_split_lhs(da_c, upper, 3, dot=_dot_tn)
        both_x = _dot_split_lhs(jnp.concatenate([dt_c, cum_c], axis=0), expand, 3)
        dt_x = both_x[0:c]
        cum_x = both_x[c:2 * c]
        cl_x = cum_x[c - 1:c, :]
        x_dt = x * dt_x
        x_dt_b = x_dt.astype(BF16)
        xw_b = (x_dt * jnp.exp(cl_x - cum_x)).astype(BF16)

        y = x * dsk_ref[...]
        y_off = jnp.zeros((c, W_MIX), F32)
        for g in range(SSD_GROUPS):
            bg = bm[:, g * n_st:(g + 1) * n_st]
            cg = cm[:, g * n_st:(g + 1) * n_st]
            cb = _dot_nt(cg, bg)
            for r in range(SSD_HEADS // SSD_GROUPS):
                h = g * (SSD_HEADS // SSD_GROUPS) + r
                diff = cum_c[:, h:h + 1] - cum_r[h:h + 1, :]
                dec = jnp.exp(jnp.where(causal, diff, -jnp.inf))
                m = (cb * dec).astype(BF16)
                yh = jnp.dot(m, x_dt_b, preferred_element_type=F32)
                y = y + jnp.where(lane_head == h, yh, 0.0)
            st = st_ref[bb, g]
            y_off = y_off + jnp.where(lane_group == g,
                                      jnp.dot(cg, st.astype(BF16), preferred_element_type=F32), 0.0)
            d_st = _dot_tn(bg, xw_b)
            st_ref[bb, g] = st * jnp.exp(cl_x) + jnp.where(lane_group == g, d_st, 0.0)
        y = y + y_off * jnp.exp(cum_x)
        zg = p_ref[bb, rows, 0:256].astype(F32)
        y = y * (zg * _sigmoid(zg))
        outs = []
        for g in range(SSD_GROUPS):
            yg = y[:, g * 128:(g + 1) * 128]
            ms = jnp.mean(yg * yg, axis=-1, keepdims=True)
            outs.append(yg * lax.rsqrt(ms + RMS_EPS))
        o_ref[bb, rows, :] = (jnp.concatenate(outs, axis=-1) * ng_ref[...]).astype(o_ref.dtype)

    return one_chunk


def _recurrent_kernel(pc_ref, ccw_ref, pg_ref, lr_ref, wlr_ref, blr_ref, gng_ref,
                      ps_ref, dt_ref, cw_ref, cb_ref, alog_ref, dtb_ref, dsk_ref, sng_ref,
                      oc_ref, og_ref, os_ref, gst_ref, sst_ref, halo_ref, chalo_ref):
    @pl.when(pl.program_id(1) == 0)
    def _():
        gst_ref[...] = jnp.zeros_like(gst_ref)
        sst_ref[...] = jnp.zeros_like(sst_ref)
        halo_ref[...] = jnp.zeros_like(halo_ref)
        chalo_ref[...] = jnp.zeros_like(chalo_ref)

    gla_rows = _gla_setup(pg_ref, lr_ref, wlr_ref, blr_ref, gng_ref, og_ref, gst_ref)
    ssd_chunk = _ssd_setup(ps_ref, dt_ref, cw_ref, cb_ref, alog_ref, dtb_ref, dsk_ref, sng_ref,
                           os_ref, sst_ref, halo_ref)
    row8 = lax.broadcasted_iota(jnp.int32, (SUBLANES, W_MIX), 0)
    for bb in range(pg_ref.shape[0]):
        gla_rows(bb, slice(0, GLA_ROWS))
        for n in range(GLA_ROWS // SSD_CHUNK):
            ssd_chunk(n, bb)
        halo_ref[bb] = ps_ref[bb, GLA_ROWS - halo_ref.shape[1]:GLA_ROWS, 256:1024]

        u = pc_ref[bb, :, 0:W_MIX].astype(F32)
        gb = pc_ref[bb, :, W_MIX:2 * W_MIX].astype(F32)
        gc = pc_ref[bb, :, 2 * W_MIX:3 * W_MIX].astype(F32)
        cu = gc * u
        prev8 = chalo_ref[bb]
        acc = cu * ccw_ref[2:3, :]
        for s in (1, 2):
            sh = pltpu.roll(cu, s, axis=0)
            top = jnp.where(row8 < s, pltpu.roll(prev8, s, axis=0), sh[0:SUBLANES])
            acc = acc + jnp.concatenate([top, sh[SUBLANES:]], axis=0) * ccw_ref[2 - s:3 - s, :]
        oc_ref[bb] = (gb * acc).astype(oc_ref.dtype)
        chalo_ref[bb] = cu[GLA_ROWS - SUBLANES:GLA_ROWS]


def _recurrent_mixers(pc, sconv_w, pg, plr, w_lr_pad, b_lr, gla_norm_g4,
                      ps, pdt, conv_w, conv_b, a_log_c, dt_bias_c, d_x, ssd_norm_g):
    b, s, _ = pg.shape
    nb = REC_SEQS
    rb = GLA_ROWS
    seq = lambda i, j: (i, j, 0)
    full2 = lambda i, j: (0, 0)
    return pl.pallas_call(
        _recurrent_kernel,
        grid=(b // nb, s // rb),
        in_specs=[pl.BlockSpec((nb, rb, pc.shape[2]), seq),
                  pl.BlockSpec((3, W_MIX), full2),
                  pl.BlockSpec((nb, rb, pg.shape[2]), seq),
                  pl.BlockSpec((nb, rb, LANES), seq),
                  pl.BlockSpec((LANES, LANES), full2),
                  pl.BlockSpec((1, LANES), full2),
                  pl.BlockSpec((1, W_MIX), full2),
                  pl.BlockSpec((nb, rb, ps.shape[2]), seq),
                  pl.BlockSpec((nb, rb, LANES), seq),
                  pl.BlockSpec((SSD_CONV_K, 3 * W_MIX), full2),
                  pl.BlockSpec((1, 3 * W_MIX), full2),
                  pl.BlockSpec((1, LANES), full2),
                  pl.BlockSpec((1, LANES), full2),
                  pl.BlockSpec((1, W_MIX), full2),
                  pl.BlockSpec((1, W_MIX), full2)],
        out_specs=[pl.BlockSpec((nb, rb, W_MIX), seq)] * 3,
        out_shape=[jax.ShapeDtypeStruct((b, s, W_MIX), BF16)] * 3,
        scratch_shapes=[pltpu.VMEM((nb, GLA_HEADS * GLA_DV, GLA_HEADS * GLA_DK), F32),
                        pltpu.VMEM((nb, SSD_GROUPS, SSD_STATE, W_MIX), F32),
                        pltpu.VMEM((nb, 2 * SUBLANES, 3 * W_MIX), BF16),
                        pltpu.VMEM((nb, SUBLANES, W_MIX), F32)],
        compiler_params=_cparams(("parallel", "arbitrary")),
        name="conv_gla_ssd_mixers",
    )(pc, sconv_w, pg, plr, w_lr_pad, b_lr, gla_norm_g4, ps, pdt, conv_w, conv_b, a_log_c,
      dt_bias_c, d_x, ssd_norm_g)


DIFF_TQ = 256
DIFF_TK = 256
LOG2E = 1.4426950408889634
DIFF_VPAD = DIFF_DV + 16
DIFF_SEQS = 4


def _diff_kernel(q_ref, k_ref, v_ref, lam_ref, ng_ref, o_ref,
                 vt_ref, qs_ref, st_ref, m_ref, acc_ref, *, lam_init):
    tq, tk = DIFF_TQ, DIFF_TK
    nh, dv = DIFF_HEADS, DIFF_DV
    nhc = 2 * nh
    s_len = k_ref.shape[1]
    i = pl.program_id(1)
    seqs = range(q_ref.shape[0])

    @pl.when(i == 0)
    def _():
        for bb in seqs:
            for cblk in range(s_len // tk):
                cols = slice(cblk * tk, (cblk + 1) * tk)
                vt = v_ref[bb, cols, :].astype(F32).T.astype(BF16)
                for h in range(nh):
                    vt_ref[bb, h, 0:dv, cols] = vt[h * dv:(h + 1) * dv]
        vt_ref[:, :, dv:, :] = jnp.ones((len(seqs), nh, DIFF_VPAD - dv, s_len), BF16)

    qlane = lax.broadcasted_iota(jnp.int32, (1, W_MIX), 1) // DIFF_DQK
    for bb in seqs:
        q = q_ref[bb].astype(F32) * (DIFF_DQK ** -0.5 * LOG2E)
        for hc in range(nhc):
            qs_ref[bb, hc * tq:(hc + 1) * tq, :] = jnp.where(qlane == hc, q, 0.0).astype(BF16)
    m_ref[...] = jnp.full_like(m_ref, -jnp.inf)
    acc_ref[...] = jnp.zeros_like(acc_ref)
    krow = lax.broadcasted_iota(jnp.int32, (tk, nhc * tq), 0)
    qcol = lax.broadcasted_iota(jnp.int32, (tk, nhc * tq), 1) % tq
    diag_ok = krow <= qcol

    def scores(j, slot):
        k0 = pl.multiple_of(j * tk, tk)
        for bb in seqs:
            st_ref[bb, slot] = _dot_nt(k_ref[bb, pl.ds(k0, tk), :], qs_ref[bb])

    def softmax_pv(j, slot, masked):
        for bb in seqs:
            softmax_pv_seq(bb, j, slot, masked)

    def softmax_pv_seq(bb, j, slot, masked):
        k0 = pl.multiple_of(j * tk, tk)
        st = st_ref[bb, slot]
        if masked:
            st = jnp.where(diag_ok, st, -jnp.inf)
        m_prev = m_ref[bb]
        m_new = jnp.maximum(m_prev, jnp.max(st, axis=0, keepdims=True))
        alpha = jnp.exp2(m_prev - m_new)
        p = jnp.exp2(st - m_new)
        m_ref[bb] = m_new
        pb = p.astype(BF16)
        for hc in range(nhc):
            h = hc // 2
            lanes = slice(hc * tq, (hc + 1) * tq)
            pv = jnp.dot(vt_ref[bb, h, :, pl.ds(k0, tk)], pb[:, lanes],
                         preferred_element_type=F32)
            acc_ref[bb, hc] = acc_ref[bb, hc] * alpha[:, lanes] + pv

    scores(0, 0)
    n_pairs = i // 2

    def pair_step(u, carry):
        scores(2 * u + 1, 1)
        softmax_pv(2 * u, 0, False)
        scores(2 * u + 2, 0)
        softmax_pv(2 * u + 1, 1, False)
        return carry

    lax.fori_loop(0, n_pairs, pair_step, 0)

    @pl.when(i % 2 == 0)
    def _():
        softmax_pv(i, 0, True)

    @pl.when(i % 2 == 1)
    def _():
        scores(i, 1)
        softmax_pv(i - 1, 0, False)
        softmax_pv(i, 1, True)

    lam = (jnp.exp(jnp.sum(lam_ref[0:1, :] * lam_ref[1:2, :], axis=-1, keepdims=True))
           - jnp.exp(jnp.sum(lam_ref[2:3, :] * lam_ref[3:4, :], axis=-1, keepdims=True))
           + lam_init)
    for bb in seqs:
        heads = []
        for h in range(nh):
            o1 = acc_ref[bb, 2 * h, 0:dv] / acc_ref[bb, 2 * h, dv:dv + 1]
            o2 = acc_ref[bb, 2 * h + 1, 0:dv] / acc_ref[bb, 2 * h + 1, dv:dv + 1]
            oh = o1 - lam * o2
            ms = jnp.mean(oh * oh, axis=0, keepdims=True)
            heads.append(oh * lax.rsqrt(ms + RMS_EPS))
        o = jnp.concatenate(heads, axis=0).T
        o_ref[bb] = (o * ng_ref[...] * (1.0 - lam_init)).astype(o_ref.dtype)


def _diff_mixer(pd, lam_vecs, norm_g4, lam_init):
    b, s, _ = pd.shape
    tq = DIFF_TQ
    nb = DIFF_SEQS
    return pl.pallas_call(
        functools.partial(_diff_kernel, lam_init=lam_init),
        grid=(b // nb, s // tq),
        in_specs=[pl.BlockSpec((nb, tq, W_MIX), lambda bi, i: (bi, i, 0)),
                  pl.BlockSpec((nb, s, W_MIX), lambda bi, i: (bi, 0, 1)),
                  pl.BlockSpec((nb, s, W_MIX), lambda bi, i: (bi, 0, 2)),
                  pl.BlockSpec((4, LANES), lambda bi, i: (0, 0)),
                  pl.BlockSpec((1, W_MIX), lambda bi, i: (0, 0))],
        out_specs=pl.BlockSpec((nb, tq, W_MIX), lambda bi, i: (bi, i, 0)),
        out_shape=jax.ShapeDtypeStruct((b, s, W_MIX), BF16),
        scratch_shapes=[pltpu.VMEM((nb, DIFF_HEADS, DIFF_VPAD, s), BF16),
                        pltpu.VMEM((nb, 2 * DIFF_HEADS * tq, W_MIX), BF16),
                        pltpu.VMEM((nb, 2, DIFF_TK, 2 * DIFF_HEADS * tq), F32),
                        pltpu.VMEM((nb, 1, 2 * DIFF_HEADS * tq), F32),
                        pltpu.VMEM((nb, 2 * DIFF_HEADS, DIFF_VPAD, tq), F32)],
        compiler_params=_cparams(("parallel", "arbitrary")),
        name="diff_attn",
    )(pd, pd, pd, lam_vecs, norm_g4)


def _oproj_kernel(yc_ref, yg_ref, ys_ref, yd_ref, x_ref, wo_ref, g_ref, b_ref, wr_ref,
                  xo_ref, xp_ref, route_ref, cnt_ref):
    @pl.when(pl.program_id(0) == 0)
    def _():
        cnt_ref[...] = jnp.zeros_like(cnt_ref)

    part = x_ref.shape[0] // OPROJ_PARTS
    hits_sum = jnp.zeros(cnt_ref.shape, F32)
    for r in range(OPROJ_PARTS):
        rows = slice(r * part, (r + 1) * part)
        mix = jnp.concatenate([yc_ref[rows, :], yg_ref[rows, :], ys_ref[rows, :], yd_ref[rows, :]],
                              axis=-1)
        h = ALPHA * x_ref[rows, :] + jnp.dot(mix, wo_ref[...], preferred_element_type=F32)
        xn = _layer_norm(h, g_ref[...], b_ref[...])
        xo_ref[rows, :] = xn
        xp_ref[rows, :] = _pack_halves(xn)

        xn_hi, xn_lo = _split_bf16(xn, 2)
        both = _dot(xn_hi, wr_ref[...])
        logits = both[:, 0:LANES] + both[:, LANES:2 * LANES] + _dot(xn_lo, wr_ref[:, 0:LANES])
        lane = lax.broadcasted_iota(jnp.int32, logits.shape, 1).astype(F32)
        neg = -jnp.inf
        big = float(LANES)
        lg = jnp.where(lane < N_GROUPS, logits, neg)
        mg = jnp.max(lg, axis=-1, keepdims=True)
        sg = jnp.sum(jnp.exp(lg - mg), axis=-1, keepdims=True)
        grp = jnp.min(jnp.where(lg == mg, lane, big), axis=-1, keepdims=True)
        p_grp = 1.0 / sg
        lo = N_GROUPS + EXPERTS_PER_GROUP * grp
        in_g = jnp.logical_and(lane >= lo, lane < lo + EXPERTS_PER_GROUP)
        le = jnp.where(in_g, logits, neg)
        me = jnp.max(le, axis=-1, keepdims=True)
        ee = jnp.exp(le - me)
        pe = ee / jnp.sum(ee, axis=-1, keepdims=True)
        pe = jnp.where(in_g, pe, -1.0)
        p1 = jnp.max(pe, axis=-1, keepdims=True)
        i1 = jnp.min(jnp.where(pe == p1, lane, big), axis=-1, keepdims=True)
        pe2 = jnp.where(lane == i1, -1.0, pe)
        p2 = jnp.max(pe2, axis=-1, keepdims=True)
        i2 = jnp.min(jnp.where(pe2 == p2, lane, big), axis=-1, keepdims=True)
        den = p1 + p2
        g1 = p_grp * p1 / den
        g2 = p_grp * p2 / den
        e1 = i1 - N_GROUPS
        e2 = i2 - N_GROUPS
        route_ref[rows, :] = jnp.where(lane == 0, e1, jnp.where(lane == 1, e2, jnp.where(
            lane == 2, g1, jnp.where(lane == 3, g2, 0.0))))
        hits = jnp.where(lane == e1, 1.0, 0.0) + jnp.where(lane == e2, 1.0, 0.0)
        hits_sum = hits_sum + jnp.sum(hits, axis=0, keepdims=True)
    cnt_ref[...] += hits_sum


def _out_proj(ys, x2d, w_o, layer, ln_g, ln_b, w_route):
    t = x2d.shape[0]
    tm = 1024
    row = lambda i: (i, 0)
    full = lambda i: (0, 0)
    return pl.pallas_call(
        _oproj_kernel,
        grid=(t // tm,),
        in_specs=[pl.BlockSpec((tm, W_MIX), row)] * 4 + [
            pl.BlockSpec((tm, D_MODEL), row),
            pl.BlockSpec((None, D_MODEL, D_MODEL), lambda i: (layer, 0, 0)),
            pl.BlockSpec((1, D_MODEL), full),
            pl.BlockSpec((1, D_MODEL), full),
            pl.BlockSpec((D_MODEL, 2 * LANES), full)],
        out_specs=[pl.BlockSpec((tm, D_MODEL), row), pl.BlockSpec((tm, D_MODEL // 2), row),
                   pl.BlockSpec((tm, LANES), row), pl.BlockSpec((1, LANES), full)],
        out_shape=[jax.ShapeDtypeStruct((t, D_MODEL), F32),
                   jax.ShapeDtypeStruct((t, D_MODEL // 2), U32),
                   jax.ShapeDtypeStruct((t, LANES), F32),
                   jax.ShapeDtypeStruct((1, LANES), F32)],
        compiler_params=_cparams(("arbitrary",)),
        name="out_proj_ln_router",
    )(*ys, x2d, w_o, ln_g, ln_b, w_route)


PLAN_TILE = 1024


def _plan_kernel(route_ref, cnt_ref, dest_ref, meta_ref, carry_ref, pstart_ref):
    tm = route_ref.shape[0]
    lane = lax.broadcasted_iota(jnp.int32, (1, LANES), 1).astype(F32)

    @pl.when(pl.program_id(0) == 0)
    def _():
        cnt = cnt_ref[...]
        padded = jnp.ceil(cnt * (1.0 / MOE_BLK)) * MOE_BLK
        li = lax.broadcasted_iota(jnp.int32, (LANES, LANES), 0)
        lj = lax.broadcasted_iota(jnp.int32, (LANES, LANES), 1)
        before = (li < lj).astype(F32)
        pstart = jnp.dot(jnp.broadcast_to(padded, (8, LANES)), before, precision=HI,
                         preferred_element_type=F32)[0:1]
        pstart_ref[...] = pstart
        carry_ref[...] = jnp.zeros_like(carry_ref)
        meta_ref[...] = jnp.concatenate(
            [pstart + padded, pstart, cnt, jnp.zeros((5, LANES), F32)], axis=0)

    oh0 = jnp.where(lane == route_ref[:, 0:1], 1.0, 0.0)
    oh1 = jnp.where(lane == route_ref[:, 1:2], 1.0, 0.0)
    both = oh0 + oh1
    ri = lax.broadcasted_iota(jnp.int32, (tm, tm), 0)
    ci = lax.broadcasted_iota(jnp.int32, (tm, tm), 1)
    earlier = (ci < ri).astype(BF16)
    base = (jnp.dot(earlier, both.astype(BF16), preferred_element_type=F32)
            + carry_ref[...] + pstart_ref[...])
    d0 = jnp.sum(oh0 * base, axis=-1, keepdims=True)
    d1 = jnp.sum(oh1 * base, axis=-1, keepdims=True)
    dest = jnp.where(lane == 0, d0, jnp.where(lane == 1, d1, 0.0))
    dest_ref[...] = dest.T[0:2, :].astype(jnp.int32)
    carry_ref[...] += jnp.sum(both, axis=0, keepdims=True)


def _dispatch_plan(route, cnt, n_blocks):
    t = route.shape[0]
    tm = PLAN_TILE
    blk = MOE_BLK
    dest, meta = pl.pallas_call(
        _plan_kernel,
        grid=(t // tm,),
        in_specs=[pl.BlockSpec((tm, LANES), lambda i: (i, 0)),
                  pl.BlockSpec((1, LANES), lambda i: (0, 0))],
        out_specs=[pl.BlockSpec((2, tm), lambda i: (0, i)),
                   pl.BlockSpec((8, LANES), lambda i: (0, 0))],
        out_shape=[jax.ShapeDtypeStruct((2, t), jnp.int32),
                   jax.ShapeDtypeStruct((8, LANES), F32)],
        scratch_shapes=[pltpu.VMEM((1, LANES), F32), pltpu.VMEM((1, LANES), F32)],
        compiler_params=_cparams(("arbitrary",)),
        name="moe_plan",
    )(route, cnt)
    meta_i = meta[:, :N_EXPERTS].astype(jnp.int32)
    pad_end, pad_start, seg_end = meta_i[0], meta_i[1], meta_i[1] + meta_i[2]
    starts = jnp.arange(n_blocks, dtype=jnp.int32)[:, None] * blk
    member = jnp.logical_and(starts >= pad_start[None, :], starts < pad_end[None, :])
    expert_ids = jnp.arange(N_EXPERTS, dtype=jnp.int32)[None, :]
    block_e = jnp.where(jnp.any(member, axis=1), jnp.sum(jnp.where(member, expert_ids, 0), axis=1),
                        N_EXPERTS - 1)
    n_used = (pad_end[N_EXPERTS - 1] // blk).reshape(1)
    n_valid = jnp.clip(jnp.sum(jnp.where(member, seg_end[None, :], 0), axis=1) - starts[:, 0],
                       0, blk)
    return dest.reshape(2 * t), block_e, n_used, n_valid


def _ffn_kernel(be_ref, nu_ref, nv_ref, xs_ref, wg_ref, wu_ref, wd_ref, y_ref,
                wgb_ref, wub_ref, wdb_ref):
    i = pl.program_id(0)
    prev = be_ref[jnp.maximum(i - 1, 0)]

    @pl.when(jnp.logical_or(i == 0, be_ref[i] != prev))
    def _():
        wgb_ref[...] = wg_ref[...].astype(BF16)
        wub_ref[...] = wu_ref[...].astype(BF16)
        wdb_ref[...] = wd_ref[...].astype(BF16)

    @pl.when(i < nu_ref[0])
    def _():
        half = xs_ref.shape[0] // 2
        for r in range(2):
            rows = slice(r * half, (r + 1) * half)
            row = lax.broadcasted_iota(jnp.int32, (half, 1), 0) + r * half
            xp = jnp.where(row < nv_ref[i], xs_ref[rows, :], U32(0))
            x_hi, x_lo = _unpack_halves(xp)
            xb = jnp.concatenate([x_hi.astype(BF16), x_lo.astype(BF16)], axis=1)
            a = jnp.dot(xb, wgb_ref[...], preferred_element_type=F32)
            u = jnp.dot(xb, wub_ref[...], preferred_element_type=F32)
            h = (a * _sigmoid(a) * u).astype(BF16)
            y_ref[rows, :] = _pack_halves(jnp.dot(h, wdb_ref[...], preferred_element_type=F32))

    @pl.when(i >= nu_ref[0])
    def _():
        y_ref[...] = jnp.zeros_like(y_ref)


def _expert_ffn(xs, block_e, n_used, n_valid, w_gate, w_up, w_down, layer):
    n_slots = xs.shape[0]
    blk = MOE_BLK
    w_map = lambda i, be, nu, nv: (layer, be[i], 0, 0)
    grid_spec = pltpu.PrefetchScalarGridSpec(
        num_scalar_prefetch=3,
        grid=(n_slots // blk,),
        in_specs=[pl.BlockSpec((blk, D_MODEL // 2),
                               lambda i, be, nu, nv: (jnp.minimum(i, nu[0] - 1), 0)),
                  pl.BlockSpec((None, None, D_MODEL, D_EXPERT), w_map),
                  pl.BlockSpec((None, None, D_MODEL, D_EXPERT), w_map),
                  pl.BlockSpec((None, None, D_EXPERT, D_MODEL), w_map)],
        out_specs=pl.BlockSpec((blk, D_MODEL // 2), lambda i, be, nu, nv: (i, 0)),
        scratch_shapes=[pltpu.VMEM((D_MODEL, D_EXPERT), BF16),
                        pltpu.VMEM((D_MODEL, D_EXPERT), BF16),
                        pltpu.VMEM((D_EXPERT, D_MODEL), BF16)],
    )
    return pl.pallas_call(
        _ffn_kernel,
        grid_spec=grid_spec,
        out_shape=jax.ShapeDtypeStruct((n_slots, D_MODEL // 2), U32),
        compiler_params=_cparams(("arbitrary",)),
        name="expert_ffn",
    )(block_e, n_used, n_valid, xs, w_gate, w_up, w_down)


SC_CORES = 2
SC_SUBCORES = 16
SC_ROWS = 64


def _sc_gather_rows(table, idx):
    b = idx.shape[0]
    d = table.shape[1]
    per_w = b // (SC_CORES * SC_SUBCORES)
    mesh = plsc.VectorSubcoreMesh(core_axis_name="c", subcore_axis_name="s")

    n_chunks = per_w // SC_ROWS

    @functools.partial(
        pl.kernel, mesh=mesh,
        out_type=jax.ShapeDtypeStruct((b, d), table.dtype),
        scratch_types=[pltpu.VMEM((SC_ROWS,), jnp.int32), pltpu.VMEM((SC_ROWS,), jnp.int32),
                       pltpu.VMEM((SC_ROWS, d), table.dtype),
                       pltpu.VMEM((SC_ROWS, d), table.dtype),
                       pltpu.SemaphoreType.DMA, pltpu.SemaphoreType.DMA,
                       pltpu.SemaphoreType.DMA, pltpu.SemaphoreType.DMA],
        name="sc_gather_rows",
    )
    def gather(table_hbm, idx_hbm, out_hbm, idx0, idx1, rows0, rows1, gs0, gs1, ws0, ws1):
        idx_v, rows_v, gsem, wsem = (idx0, idx1), (rows0, rows1), (gs0, gs1), (ws0, ws1)
        wid = lax.axis_index("s") * SC_CORES + lax.axis_index("c")
        base = wid * per_w

        def rows_of(c):
            return pl.ds(pl.multiple_of(base + c * SC_ROWS, SC_ROWS), SC_ROWS)

        def start_gather(c, s):
            pltpu.sync_copy(idx_hbm.at[rows_of(c)], idx_v[s])
            pltpu.async_copy(table_hbm.at[idx_v[s]], rows_v[s], gsem[s])

        def write_back(c, s):
            pltpu.make_async_copy(table_hbm.at[idx_v[s]], rows_v[s], gsem[s]).wait()
            pltpu.async_copy(rows_v[s], out_hbm.at[rows_of(c)], wsem[s]).wait()

        start_gather(0, 0)

        @pl.loop(0, n_chunks, step=2)
        def _(c):
            start_gather(c + 1, 1)
            write_back(c, 0)

            @pl.when(c + 2 < n_chunks)
            def _():
                start_gather(c + 2, 0)

            write_back(c + 1, 1)

    return gather(table, idx)


def _sc_scatter_rows(x2d, idx, n_slots):
    t, d = x2d.shape
    per_w = t // (SC_CORES * SC_SUBCORES)
    mesh = plsc.VectorSubcoreMesh(core_axis_name="c", subcore_axis_name="s")

    @functools.partial(
        pl.kernel, mesh=mesh,
        out_type=jax.ShapeDtypeStruct((n_slots, d), x2d.dtype),
        scratch_types=[pltpu.VMEM((SC_ROWS,), jnp.int32), pltpu.VMEM((SC_ROWS,), jnp.int32),
                       pltpu.VMEM((SC_ROWS, d), x2d.dtype),
                       pltpu.SemaphoreType.DMA, pltpu.SemaphoreType.DMA],
        name="sc_scatter_rows",
    )
    def scatter(x_hbm, idx_hbm, out_hbm, idx0, idx1, rows_v, s0, s1):
        wid = lax.axis_index("s") * SC_CORES + lax.axis_index("c")
        base = wid * per_w

        @pl.loop(0, per_w // SC_ROWS)
        def _(c):
            off = pl.multiple_of(base + c * SC_ROWS, SC_ROWS)
            pltpu.sync_copy(x_hbm.at[pl.ds(off, SC_ROWS)], rows_v)
            pltpu.sync_copy(idx_hbm.at[pl.ds(off, SC_ROWS)], idx0)
            pltpu.sync_copy(idx_hbm.at[pl.ds(t + off, SC_ROWS)], idx1)
            cp0 = pltpu.async_copy(rows_v, out_hbm.at[idx0], s0)
            cp1 = pltpu.async_copy(rows_v, out_hbm.at[idx1], s1)
            cp0.wait()
            cp1.wait()

    return scatter(x2d, idx)


def _combine_dense_kernel(route_ref, x_ref, y0_ref, y1_ref, g_ref, b_ref, o_ref):
    y0 = jnp.concatenate(_unpack_halves(y0_ref[...]), axis=1)
    y1 = jnp.concatenate(_unpack_halves(y1_ref[...]), axis=1)
    moe = route_ref[:, 2:3] * y0 + route_ref[:, 3:4] * y1
    h = ALPHA * x_ref[...] + moe
    o_ref[...] = _layer_norm(h, g_ref[...], b_ref[...])


def _combine_dense(route, x2d, yg, ln_g, ln_b):
    t = x2d.shape[0]
    tm = 1024
    nt = t // tm
    row = lambda i: (i, 0)
    full = lambda i: (0, 0)
    return pl.pallas_call(
        _combine_dense_kernel,
        grid=(nt,),
        in_specs=[pl.BlockSpec((tm, LANES), row),
                  pl.BlockSpec((tm, D_MODEL), row),
                  pl.BlockSpec((tm, D_MODEL // 2), row),
                  pl.BlockSpec((tm, D_MODEL // 2), lambda i: (i + nt, 0)),
                  pl.BlockSpec((1, D_MODEL), full),
                  pl.BlockSpec((1, D_MODEL), full)],
        out_specs=pl.BlockSpec((tm, D_MODEL), row),
        out_shape=jax.ShapeDtypeStruct((t, D_MODEL), F32),
        compiler_params=_cparams(("parallel",)),
        name="moe_combine_dense",
    )(route, x2d, yg, yg, ln_g, ln_b)


def _pad_cols(w, n):
    return jnp.pad(w, [(0, 0)] * (w.ndim - 1) + [(0, n - w.shape[-1])])


def kernel(x, w_in, conv_w, gla_w_lr, gla_b_lr, gla_norm_g, ssd_conv_w, ssd_conv_b, ssd_a_log,
           ssd_d, ssd_dt_bias, ssd_norm_g, diff_lq1, diff_lk1, diff_lq2, diff_lk2, diff_norm_g,
           w_o, ln1_g, ln1_b, router_g, router_e, w_gate, w_up, w_down, ln2_g, ln2_b):
    bsz, seq, d = x.shape
    t = bsz * seq
    n_assign = 2 * t
    n_blocks = (n_assign + N_EXPERTS * (MOE_BLK - 1)) // MOE_BLK + 1
    n_slots = n_blocks * MOE_BLK
    x2d = x.reshape(t, d)
    w_o_b = w_o.astype(BF16)
    for l in range(DEPTH):
        pc, pg, plr, ps, pdt, pd = [p.reshape(bsz, seq, -1) for p in _in_proj(x2d, w_in, l)]

        w_lr_pad = jnp.pad(gla_w_lr[l], ((0, LANES - GLA_RANK), (0, 0)))
        pad4 = lambda v: jnp.pad(v, (0, LANES - SSD_HEADS)).reshape(1, LANES)
        y_conv, y_gla, y_ssd = _recurrent_mixers(
            pc, conv_w[l], pg, plr, w_lr_pad, gla_b_lr[l].reshape(1, -1),
            jnp.tile(gla_norm_g[l], GLA_HEADS).reshape(1, -1),
            ps, pdt, ssd_conv_w[l], ssd_conv_b[l].reshape(1, -1),
            pad4(ssd_a_log[l]), pad4(ssd_dt_bias[l]),
            jnp.repeat(ssd_d[l], SSD_HEADDIM).reshape(1, -1), ssd_norm_g[l].reshape(1, -1))
        lam_vecs = jnp.pad(jnp.stack([diff_lq1[l], diff_lk1[l], diff_lq2[l], diff_lk2[l]]),
                           ((0, 0), (0, LANES - DIFF_DQK)))
        lam_init = 0.8 - 0.6 * math.exp(-0.3 * l)
        y_diff = _diff_mixer(pd, lam_vecs,
                             jnp.tile(diff_norm_g[l], DIFF_HEADS).reshape(1, -1), lam_init)

        w_route = _pad_cols(jnp.concatenate(
            [router_g[l], router_e[l].reshape(d, N_EXPERTS)], axis=1), LANES)
        w_route_hi = w_route.astype(BF16)
        w_route = jnp.concatenate(
            [w_route_hi, (w_route - w_route_hi.astype(F32)).astype(BF16)], axis=1)
        ys = [y.reshape(t, W_MIX) for y in (y_conv, y_gla, y_ssd, y_diff)]
        xn, xn_p, route, cnt = _out_proj(ys, x2d, w_o_b, l, ln1_g[l].reshape(1, -1),
                                         ln1_b[l].reshape(1, -1), w_route)

        idx, block_e, n_used, n_valid = _dispatch_plan(route, cnt, n_blocks)
        xs = _sc_scatter_rows(xn_p, idx, n_slots)
        y = _expert_ffn(xs, block_e, n_used, n_valid, w_gate, w_up, w_down, l)
        yg = _sc_gather_rows(y, idx)
        x2d = _combine_dense(route, xn, yg, ln2_g[l].reshape(1, -1), ln2_b[l].reshape(1, -1))
    return x2d.reshape(bsz, seq, d)
```

```python
import functools
import math

import jax
import jax.numpy as jnp
from jax import lax
from jax.experimental import pallas as pl
from jax.experimental.pallas import tpu as pltpu
from jax.experimental.pallas import tpu_sc as plsc

F32 = jnp.float32
BF16 = jnp.bfloat16
HI = lax.Precision.HIGHEST

D_MODEL = 1024
DEPTH = 2
W_MIX = 256
GLA_HEADS, GLA_DK, GLA_DV, GLA_RANK, GLA_TAU, GLA_CHUNK = 4, 32, 64, 16, 16.0, 64
GLA_ROWS = 256
REC_SEQS = 4
SSD_HEADS, SSD_GROUPS, SSD_HEADDIM, SSD_STATE, SSD_CONV_K, SSD_CHUNK = 4, 2, 64, 128, 4, 128
DIFF_HEADS, DIFF_DQK, DIFF_DV = 4, 32, 64
N_GROUPS, EXPERTS_PER_GROUP, N_EXPERTS, D_EXPERT = 4, 8, 32, 512
ALPHA = (2 * DEPTH) ** 0.25
LN_EPS = 1e-5
RMS_EPS = 1e-6

LANES = 128
SUBLANES = 8
PROJ_WIDTHS = (768, 768, 128, 1024, 128, 768)
PROJ_SRC_OFFSETS = (0, 768, 1536, 1552, 2576, 2580)
PROJ_SRC_WIDTHS = (768, 768, GLA_RANK, 1024, SSD_HEADS, 768)
PROJ_DTYPES = (BF16, BF16, F32, BF16, F32, BF16)
VMEM_LIMIT = 56 * 1024 * 1024

MOE_BLK = 512
OPROJ_PARTS = 2


def _cparams(sem):
    return pltpu.CompilerParams(dimension_semantics=sem, vmem_limit_bytes=VMEM_LIMIT)


def _sigmoid(x):
    return 1.0 / (1.0 + jnp.exp(-x))


def _softplus(x):
    return jnp.maximum(x, 0.0) + jnp.log(1.0 + jnp.exp(-jnp.abs(x)))


def _layer_norm(h, g, b):
    mu = jnp.mean(h, axis=-1, keepdims=True)
    d = h - mu
    var = jnp.mean(d * d, axis=-1, keepdims=True)
    return d * lax.rsqrt(var + LN_EPS) * g + b


def _dot_nt(a, b):
    return lax.dot_general(a, b, (((1,), (1,)), ((), ())), preferred_element_type=F32)


def _dot_tn(a, b, precision=None):
    return lax.dot_general(a, b, (((0,), (0,)), ((), ())), preferred_element_type=F32,
                           precision=precision)


def _split_bf16(x, parts):
    out = []
    for _ in range(parts - 1):
        hi = x.astype(BF16)
        out.append(hi)
        x = x - hi.astype(F32)
    out.append(x.astype(BF16))
    return out


def _dot(a, b):
    return jnp.dot(a, b, preferred_element_type=F32)


U32 = jnp.uint32


def _pack_halves(x):
    w = x.shape[1] // 2
    hi = lax.bitcast_convert_type(x[:, :w].astype(BF16).astype(F32), U32)
    lo = lax.bitcast_convert_type(x[:, w:].astype(BF16).astype(F32), U32)
    return hi | lax.shift_right_logical(lo, U32(16))


def _unpack_halves(p):
    hi = lax.bitcast_convert_type(p & U32(0xFFFF0000), F32)
    lo = lax.bitcast_convert_type(lax.shift_left(p, U32(16)), F32)
    return hi, lo


def _dot_split_lhs(a, b_exact, parts, dot=_dot):
    acc = None
    for term in _split_bf16(a, parts):
        d = dot(term, b_exact)
        acc = d if acc is None else acc + d
    return acc


def _dot_split_rhs(a_exact, b, parts):
    acc = None
    for term in _split_bf16(b, parts):
        d = jnp.dot(a_exact, term, preferred_element_type=F32)
        acc = d if acc is None else acc + d
    return acc


def _proj_kernel(layer, x_ref, w_ref, *refs):
    o_refs, w_scr, wf_scr, sem = refs[:-3], refs[-3], refs[-2], refs[-1]

    @pl.when(pl.program_id(0) == 0)
    def _():
        fetch = pltpu.make_async_copy(w_ref.at[:, layer, :], wf_scr, sem)
        fetch.start()
        w_scr[...] = jnp.zeros_like(w_scr)
        fetch.wait()
        dst = 0
        for src, n_src, n_dst in zip(PROJ_SRC_OFFSETS, PROJ_SRC_WIDTHS, PROJ_WIDTHS):
            w_scr[dst:dst + n_src, :] = wf_scr[src:src + n_src, :].astype(BF16)
            dst += n_dst

    xb = x_ref[...].astype(BF16)
    off = 0
    for o_ref in o_refs:
        n = o_ref.shape[-1]
        o_ref[...] = _dot_nt(xb, w_scr[off:off + n, :]).astype(o_ref.dtype)
        off += n


def _in_proj(x2d, w_in, layer):
    t = x2d.shape[0]
    tm = 1024
    return pl.pallas_call(
        functools.partial(_proj_kernel, layer),
        grid=(t // tm,),
        in_specs=[pl.BlockSpec((tm, D_MODEL), lambda i: (i, 0)),
                  pl.BlockSpec(memory_space=pl.ANY)],
        out_specs=[pl.BlockSpec((tm, n), lambda i: (i, 0)) for n in PROJ_WIDTHS],
        out_shape=[jax.ShapeDtypeStruct((t, n), dt) for n, dt in zip(PROJ_WIDTHS, PROJ_DTYPES)],
        scratch_shapes=[pltpu.VMEM((sum(PROJ_WIDTHS), D_MODEL), BF16),
                        pltpu.VMEM((w_in.shape[2], D_MODEL), F32),
                        pltpu.SemaphoreType.DMA(())],
        compiler_params=_cparams(("arbitrary",)),
        name="in_proj",
    )(x2d, jnp.transpose(w_in, (2, 0, 1)))


def _gla_setup(p_ref, lr_ref, wlr_ref, blr_ref, ng_ref, o_ref, st_ref):
    c = GLA_CHUNK
    nh, dk, dv = GLA_HEADS, GLA_DK, GLA_DV

    rb = GLA_ROWS
    ncb = rb // c
    ri = lax.broadcasted_iota(jnp.int32, (rb, rb), 0)
    ci = lax.broadcasted_iota(jnp.int32, (rb, rb), 1)
    tri = (ci <= ri).astype(BF16)
    klane_head = lax.broadcasted_iota(jnp.int32, (1, nh * dk), 1) // dk
    vlane_head = lax.broadcasted_iota(jnp.int32, (1, nh * dv), 1) // dv
    strow_head = lax.broadcasted_iota(jnp.int32, (nh * dv, 1), 0) // dv
    st_mask = strow_head == klane_head
    r4 = lax.broadcasted_iota(jnp.int32, (nh * c, c), 0) % c
    c4 = lax.broadcasted_iota(jnp.int32, (nh * c, c), 1)
    causal4 = c4 <= r4
    gi = lax.broadcasted_iota(jnp.int32, (nh * dv, nh * dv), 0) // dv
    gj = lax.broadcasted_iota(jnp.int32, (nh * dv, nh * dv), 1) // dv
    gmean = jnp.where(gi == gj, 1.0 / dv, 0.0).astype(BF16)
    wlr_hi, wlr_lo = _split_bf16(wlr_ref[...], 2)

    def one_seq(bb, rows):
        q = p_ref[bb, rows, 0:128].astype(F32) * (dk ** -0.5)
        k = p_ref[bb, rows, 128:256].astype(F32)
        vb = p_ref[bb, rows, 256:512]
        g = p_ref[bb, rows, 512:768].astype(F32)
        lr = lr_ref[bb, rows, :]
        lr_hi, lr_lo = _split_bf16(lr, 2)
        z = (jnp.dot(lr_hi, wlr_hi, preferred_element_type=F32)
             + jnp.dot(lr_hi, wlr_lo, preferred_element_type=F32)
             + jnp.dot(lr_lo, wlr_hi, preferred_element_type=F32)) + blr_ref[...]
        log_a = (jnp.minimum(z, 0.0) - jnp.log(1.0 + jnp.exp(-jnp.abs(z)))) * (1.0 / GLA_TAU)
        cumb = _dot_split_rhs(tri, log_a, 3)
        ends = [cumb[(j + 1) * c - 1:(j + 1) * c, :] for j in range(ncb)]
        starts = [jnp.zeros_like(ends[0])] + ends[:-1]
        cum = cumb - jnp.concatenate([jnp.broadcast_to(s0, (c, nh * dk)) for s0 in starts], axis=0)
        lasts = [e - s0 for e, s0 in zip(ends, starts)]
        cl = jnp.concatenate([jnp.broadcast_to(x, (c, nh * dk)) for x in lasts], axis=0)
        q_dec = q * jnp.exp(cum)
        k_inv = (k * jnp.exp(-cum)).astype(BF16)
        k_end = (k * jnp.exp(cl - cum)).astype(BF16)
        st = st_ref[bb]
        outs = []
        for j in range(ncb):
            sl = slice(j * c, (j + 1) * c)
            qd = q_dec[sl]
            qs = jnp.concatenate([jnp.where(klane_head == h, qd, 0.0) for h in range(nh)],
                                 axis=0).astype(BF16)
            att = jnp.where(causal4, _dot_nt(qs, k_inv[sl]), 0.0)
            r = jnp.dot(att.astype(BF16), vb[sl], preferred_element_type=F32)
            o = jnp.where(vlane_head == 0, r[0:c], 0.0)
            for h in range(1, nh):
                o = o + jnp.where(vlane_head == h, r[h * c:(h + 1) * c], 0.0)
            outs.append(o + _dot_nt(qd.astype(BF16), st.astype(BF16)))
            d_st = _dot_tn(vb[sl], k_end[sl])
            st = st * jnp.exp(lasts[j]) + jnp.where(st_mask, d_st, 0.0)
        st_ref[bb] = st
        o = jnp.concatenate(outs, axis=0)
        ms = _dot_split_lhs(o * o, gmean, 2)
        o = o * lax.rsqrt(ms + RMS_EPS) * ng_ref[...]
        o_ref[bb, rows, :] = (o * (g * _sigmoid(g))).astype(o_ref.dtype)

    return one_seq


def _ssd_setup(p_ref, dt_ref, cw_ref, cb_ref, alog_ref, dtb_ref, dsk_ref, ng_ref, o_ref, st_ref,
               halo_ref):
    c = SSD_CHUNK
    n_st = SSD_STATE
    halo = 2 * SUBLANES

    ri = lax.broadcasted_iota(jnp.int32, (c, c), 0)
    ci = lax.broadcasted_iota(jnp.int32, (c, c), 1)
    causal = ci <= ri
    tri = causal.astype(BF16)
    upper = (ri <= ci).astype(BF16)
    lane_head = lax.broadcasted_iota(jnp.int32, (1, W_MIX), 1) // SSD_HEADDIM
    lane_group = lane_head // (SSD_HEADS // SSD_GROUPS)
    eh = lax.broadcasted_iota(jnp.int32, (LANES, W_MIX), 0)
    el = lax.broadcasted_iota(jnp.int32, (LANES, W_MIX), 1) // SSD_HEADDIM
    expand = (eh == el).astype(BF16)
    row8 = lax.broadcasted_iota(jnp.int32, (8, 3 * W_MIX), 0)
    a_c = -jnp.exp(alog_ref[...])

    def one_chunk(n, bb):
        rows = slice(n * c, (n + 1) * c)
        cur = p_ref[bb, rows, 256:1024].astype(F32)
        before = halo_ref[bb] if n == 0 else p_ref[bb, n * c - halo:n * c, 256:1024]
        prev8 = before.astype(F32)[SUBLANES:]
        acc = cur * cw_ref[3:4, :] + cb_ref[...]
        for s in (1, 2, 3):
            sh = pltpu.roll(cur, s, axis=0)
            top = jnp.where(row8 < s, pltpu.roll(prev8, s, axis=0), sh[0:8])
            sh = jnp.concatenate([top, sh[8:]], axis=0)
            acc = acc + sh * cw_ref[3 - s:4 - s, :]
        xbc = acc * _sigmoid(acc)
        x = xbc[:, 0:256]
        bm = xbc[:, 256:512].astype(BF16)
        cm = xbc[:, 512:768].astype(BF16)

        dt_c = _softplus(dt_ref[bb, rows, :] + dtb_ref[...])
        da_c = dt_c * a_c
        cum_c = _dot_split_rhs(tri, da_c, 3)
        cum_r = _dot_split_lhs(da_c, upper, 3, dot=_dot_tn)
        both_x = _dot_split_lhs(jnp.concatenate([dt_c, cum_c], axis=0), expand, 3)
        dt_x = both_x[0:c]
        cum_x = both_x[c:2 * c]
        cl_x = cum_x[c - 1:c, :]
        x_dt = x * dt_x
        x_dt_b = x_dt.astype(BF16)
        xw_b = (x_dt * jnp.exp(cl_x - cum_x)).astype(BF16)

        y = x * dsk_ref[...]
        y_off = jnp.zeros((c, W_MIX), F32)
        for g in range(SSD_GROUPS):
            bg = bm[:, g * n_st:(g + 1) * n_st]
            cg = cm[:, g * n_st:(g + 1) * n_st]
            cb = _dot_nt(cg, bg)
            for r in range(SSD_HEADS // SSD_GROUPS):
                h = g * (SSD_HEADS // SSD_GROUPS) + r
                diff = cum_c[:, h:h + 1] - cum_r[h:h + 1, :]
                dec = jnp.exp(jnp.where(causal, diff, -jnp.inf))
                m = (cb * dec).astype(BF16)
                yh = jnp.dot(m, x_dt_b, preferred_element_type=F32)
                y = y + jnp.where(lane_head == h, yh, 0.0)
            st = st_ref[bb, g]
            y_off = y_off + jnp.where(lane_group == g,
                                      jnp.dot(cg, st.astype(BF16), preferred_element_type=F32), 0.0)
            d_st = _dot_tn(bg, xw_b)
            st_ref[bb, g] = st * jnp.exp(cl_x) + jnp.where(lane_group == g, d_st, 0.0)
        y = y + y_off * jnp.exp(cum_x)
        zg = p_ref[bb, rows, 0:256].astype(F32)
        y = y * (zg * _sigmoid(zg))
        outs = []
        for g in range(SSD_GROUPS):
            yg = y[:, g * 128:(g + 1) * 128]
            ms = jnp.mean(yg * yg, axis=-1, keepdims=True)
            outs.append(yg * lax.rsqrt(ms + RMS_EPS))
        o_ref[bb, rows, :] = (jnp.concatenate(outs, axis=-1) * ng_ref[...]).astype(o_ref.dtype)

    return one_chunk


def _recurrent_kernel(pc_ref, ccw_ref, pg_ref, lr_ref, wlr_ref, blr_ref, gng_ref,
                      ps_ref, dt_ref, cw_ref, cb_ref, alog_ref, dtb_ref, dsk_ref, sng_ref,
                      oc_ref, og_ref, os_ref, gst_ref, sst_ref, halo_ref, chalo_ref):
    @pl.when(pl.program_id(1) == 0)
    def _():
        gst_ref[...] = jnp.zeros_like(gst_ref)
        sst_ref[...] = jnp.zeros_like(sst_ref)
        halo_ref[...] = jnp.zeros_like(halo_ref)
        chalo_ref[...] = jnp.zeros_like(chalo_ref)

    gla_rows = _gla_setup(pg_ref, lr_ref, wlr_ref, blr_ref, gng_ref, og_ref, gst_ref)
    ssd_chunk = _ssd_setup(ps_ref, dt_ref, cw_ref, cb_ref, alog_ref, dtb_ref, dsk_ref, sng_ref,
                           os_ref, sst_ref, halo_ref)
    row8 = lax.broadcasted_iota(jnp.int32, (SUBLANES, W_MIX), 0)
    for bb in range(pg_ref.shape[0]):
        gla_rows(bb, slice(0, GLA_ROWS))
        for n in range(GLA_ROWS // SSD_CHUNK):
            ssd_chunk(n, bb)
        halo_ref[bb] = ps_ref[bb, GLA_ROWS - halo_ref.shape[1]:GLA_ROWS, 256:1024]

        u = pc_ref[bb, :, 0:W_MIX].astype(F32)
        gb = pc_ref[bb, :, W_MIX:2 * W_MIX].astype(F32)
        gc = pc_ref[bb, :, 2 * W_MIX:3 * W_MIX].astype(F32)
        cu = gc * u
        prev8 = chalo_ref[bb]
        acc = cu * ccw_ref[2:3, :]
        for s in (1, 2):
            sh = pltpu.roll(cu, s, axis=0)
            top = jnp.where(row8 < s, pltpu.roll(prev8, s, axis=0), sh[0:SUBLANES])
            acc = acc + jnp.concatenate([top, sh[SUBLANES:]], axis=0) * ccw_ref[2 - s:3 - s, :]
        oc_ref[bb] = (gb * acc).astype(oc_ref.dtype)
        chalo_ref[bb] = cu[GLA_ROWS - SUBLANES:GLA_ROWS]


def _recurrent_mixers(pc, sconv_w, pg, plr, w_lr_pad, b_lr, gla_norm_g4,
                      ps, pdt, conv_w, conv_b, a_log_c, dt_bias_c, d_x, ssd_norm_g):
    b, s, _ = pg.shape
    nb = REC_SEQS
    rb = GLA_ROWS
    seq = lambda i, j: (i, j, 0)
    full2 = lambda i, j: (0, 0)
    return pl.pallas_call(
        _recurrent_kernel,
        grid=(b // nb, s // rb),
        in_specs=[pl.BlockSpec((nb, rb, pc.shape[2]), seq),
                  pl.BlockSpec((3, W_MIX), full2),
                  pl.BlockSpec((nb, rb, pg.shape[2]), seq),
                  pl.BlockSpec((nb, rb, LANES), seq),
                  pl.BlockSpec((LANES, LANES), full2),
                  pl.BlockSpec((1, LANES), full2),
                  pl.BlockSpec((1, W_MIX), full2),
                  pl.BlockSpec((nb, rb, ps.shape[2]), seq),
                  pl.BlockSpec((nb, rb, LANES), seq),
                  pl.BlockSpec((SSD_CONV_K, 3 * W_MIX), full2),
                  pl.BlockSpec((1, 3 * W_MIX), full2),
                  pl.BlockSpec((1, LANES), full2),
                  pl.BlockSpec((1, LANES), full2),
                  pl.BlockSpec((1, W_MIX), full2),
                  pl.BlockSpec((1, W_MIX), full2)],
        out_specs=[pl.BlockSpec((nb, rb, W_MIX), seq)] * 3,
        out_shape=[jax.ShapeDtypeStruct((b, s, W_MIX), BF16)] * 3,
        scratch_shapes=[pltpu.VMEM((nb, GLA_HEADS * GLA_DV, GLA_HEADS * GLA_DK), F32),
                        pltpu.VMEM((nb, SSD_GROUPS, SSD_STATE, W_MIX), F32),
                        pltpu.VMEM((nb, 2 * SUBLANES, 3 * W_MIX), BF16),
                        pltpu.VMEM((nb, SUBLANES, W_MIX), F32)],
        compiler_params=_cparams(("parallel", "arbitrary")),
        name="conv_gla_ssd_mixers",
    )(pc, sconv_w, pg, plr, w_lr_pad, b_lr, gla_norm_g4, ps, pdt, conv_w, conv_b, a_log_c,
      dt_bias_c, d_x, ssd_norm_g)


DIFF_TQ = 256
DIFF_TK = 256
LOG2E = 1.4426950408889634
DIFF_VPAD = DIFF_DV + 16
DIFF_SEQS = 4


def _diff_kernel(q_ref, k_ref, v_ref, lam_ref, ng_ref, o_ref,
                 vt_ref, qs_ref, st_ref, m_ref, acc_ref, *, lam_init):
    tq, tk = DIFF_TQ, DIFF_TK
    nh, dv = DIFF_HEADS, DIFF_DV
    nhc = 2 * nh
    s_len = k_ref.shape[1]
    i = pl.program_id(1)
    seqs = range(q_ref.shape[0])

    @pl.when(i == 0)
    def _():
        for bb in seqs:
            for cblk in range(s_len // tk):
                cols = slice(cblk * tk, (cblk + 1) * tk)
                vt = v_ref[bb, cols, :].astype(F32).T.astype(BF16)
                for h in range(nh):
                    vt_ref[bb, h, 0:dv, cols] = vt[h * dv:(h + 1) * dv]
        vt_ref[:, :, dv:, :] = jnp.ones((len(seqs), nh, DIFF_VPAD - dv, s_len), BF16)

    qlane = lax.broadcasted_iota(jnp.int32, (1, W_MIX), 1) // DIFF_DQK
    for bb in seqs:
        q = q_ref[bb].astype(F32) * (DIFF_DQK ** -0.5 * LOG2E)
        for hc in range(nhc):
            qs_ref[bb, hc * tq:(hc + 1) * tq, :] = jnp.where(qlane == hc, q, 0.0).astype(BF16)
    m_ref[...] = jnp.full_like(m_ref, -jnp.inf)
    acc_ref[...] = jnp.zeros_like(acc_ref)
    krow = lax.broadcasted_iota(jnp.int32, (tk, nhc * tq), 0)
    qcol = lax.broadcasted_iota(jnp.int32, (tk, nhc * tq), 1) % tq
    diag_ok = krow <= qcol

    def scores(j, slot):
        k0 = pl.multiple_of(j * tk, tk)
        for bb in seqs:
            st_ref[bb, slot] = _dot_nt(k_ref[bb, pl.ds(k0, tk), :], qs_ref[bb])

    def softmax_pv(j, slot, masked):
        for bb in seqs:
            softmax_pv_seq(bb, j, slot, masked)

    def softmax_pv_seq(bb, j, slot, masked):
        k0 = pl.multiple_of(j * tk, tk)
        st = st_ref[bb, slot]
        if masked:
            st = jnp.where(diag_ok, st, -jnp.inf)
        m_prev = m_ref[bb]
        m_new = jnp.maximum(m_prev, jnp.max(st, axis=0, keepdims=True))
        alpha = jnp.exp2(m_prev - m_new)
        p = jnp.exp2(st - m_new)
        m_ref[bb] = m_new
        pb = p.astype(BF16)
        for hc in range(nhc):
            h = hc // 2
            lanes = slice(hc * tq, (hc + 1) * tq)
            pv = jnp.dot(vt_ref[bb, h, :, pl.ds(k0, tk)], pb[:, lanes],
                         preferred_element_type=F32)
            acc_ref[bb, hc] = acc_ref[bb, hc] * alpha[:, lanes] + pv

    scores(0, 0)
    n_pairs = i // 2

    def pair_step(u, carry):
        scores(2 * u + 1, 1)
        softmax_pv(2 * u, 0, False)
        scores(2 * u + 2, 0)
        softmax_pv(2 * u + 1, 1, False)
        return carry

    lax.fori_loop(0, n_pairs, pair_step, 0)

    @pl.when(i % 2 == 0)
    def _():
        softmax_pv(i, 0, True)

    @pl.when(i % 2 == 1)
    def _():
        scores(i, 1)
        softmax_pv(i - 1, 0, False)
        softmax_pv(i, 1, True)

    lam = (jnp.exp(jnp.sum(lam_ref[0:1, :] * lam_ref[1:2, :], axis=-1, keepdims=True))
           - jnp.exp(jnp.sum(lam_ref[2:3, :] * lam_ref[3:4, :], axis=-1, keepdims=True))
           + lam_init)
    for bb in seqs:
        heads = []
        for h in range(nh):
            o1 = acc_ref[bb, 2 * h, 0:dv] / acc_ref[bb, 2 * h, dv:dv + 1]
            o2 = acc_ref[bb, 2 * h + 1, 0:dv] / acc_ref[bb, 2 * h + 1, dv:dv + 1]
            oh = o1 - lam * o2
            ms = jnp.mean(oh * oh, axis=0, keepdims=True)
            heads.append(oh * lax.rsqrt(ms + RMS_EPS))
        o = jnp.concatenate(heads, axis=0).T
        o_ref[bb] = (o * ng_ref[...] * (1.0 - lam_init)).astype(o_ref.dtype)


def _diff_mixer(pd, lam_vecs, norm_g4, lam_init):
    b, s, _ = pd.shape
    tq = DIFF_TQ
    nb = DIFF_SEQS
    return pl.pallas_call(
        functools.partial(_diff_kernel, lam_init=lam_init),
        grid=(b // nb, s // tq),
        in_specs=[pl.BlockSpec((nb, tq, W_MIX), lambda bi, i: (bi, i, 0)),
                  pl.BlockSpec((nb, s, W_MIX), lambda bi, i: (bi, 0, 1)),
                  pl.BlockSpec((nb, s, W_MIX), lambda bi, i: (bi, 0, 2)),
                  pl.BlockSpec((4, LANES), lambda bi, i: (0, 0)),
                  pl.BlockSpec((1, W_MIX), lambda bi, i: (0, 0))],
        out_specs=pl.BlockSpec((nb, tq, W_MIX), lambda bi, i: (bi, i, 0)),
        out_shape=jax.ShapeDtypeStruct((b, s, W_MIX), BF16),
        scratch_shapes=[pltpu.VMEM((nb, DIFF_HEADS, DIFF_VPAD, s), BF16),
                        pltpu.VMEM((nb, 2 * DIFF_HEADS * tq, W_MIX), BF16),
                        pltpu.VMEM((nb, 2, DIFF_TK, 2 * DIFF_HEADS * tq), F32),
                        pltpu.VMEM((nb, 1, 2 * DIFF_HEADS * tq), F32),
                        pltpu.VMEM((nb, 2 * DIFF_HEADS, DIFF_VPAD, tq), F32)],
        compiler_params=_cparams(("parallel", "arbitrary")),
        name="diff_attn",
    )(pd, pd, pd, lam_vecs, norm_g4)


def _oproj_kernel(yc_ref, yg_ref, ys_ref, yd_ref, x_ref, wo_ref, g_ref, b_ref, wr_ref,
                  xo_ref, xp_ref, route_ref, cnt_ref):
    @pl.when(pl.program_id(0) == 0)
    def _():
        cnt_ref[...] = jnp.zeros_like(cnt_ref)

    part = x_ref.shape[0] // OPROJ_PARTS
    hits_sum = jnp.zeros(cnt_ref.shape, F32)
    for r in range(OPROJ_PARTS):
        rows = slice(r * part, (r + 1) * part)
        mix = jnp.concatenate([yc_ref[rows, :], yg_ref[rows, :], ys_ref[rows, :], yd_ref[rows, :]],
                              axis=-1)
        h = ALPHA * x_ref[rows, :] + jnp.dot(mix, wo_ref[...], preferred_element_type=F32)
        xn = _layer_norm(h, g_ref[...], b_ref[...])
        xo_ref[rows, :] = xn
        xp_ref[rows, :] = _pack_halves(xn)

        xn_hi, xn_lo = _split_bf16(xn, 2)
        both = _dot(xn_hi, wr_ref[...])
        logits = both[:, 0:LANES] + both[:, LANES:2 * LANES] + _dot(xn_lo, wr_ref[:, 0:LANES])
        lane = lax.broadcasted_iota(jnp.int32, logits.shape, 1).astype(F32)
        neg = -jnp.inf
        big = float(LANES)
        lg = jnp.where(lane < N_GROUPS, logits, neg)
        mg = jnp.max(lg, axis=-1, keepdims=True)
        sg = jnp.sum(jnp.exp(lg - mg), axis=-1, keepdims=True)
        grp = jnp.min(jnp.where(lg == mg, lane, big), axis=-1, keepdims=True)
        p_grp = 1.0 / sg
        lo = N_GROUPS + EXPERTS_PER_GROUP * grp
        in_g = jnp.logical_and(lane >= lo, lane < lo + EXPERTS_PER_GROUP)
        le = jnp.where(in_g, logits, neg)
        me = jnp.max(le, axis=-1, keepdims=True)
        ee = jnp.exp(le - me)
        pe = ee / jnp.sum(ee, axis=-1, keepdims=True)
        pe = jnp.where(in_g, pe, -1.0)
        p1 = jnp.max(pe, axis=-1, keepdims=True)
        i1 = jnp.min(jnp.where(pe == p1, lane, big), axis=-1, keepdims=True)
        pe2 = jnp.where(lane == i1, -1.0, pe)
        p2 = jnp.max(pe2, axis=-1, keepdims=True)
        i2 = jnp.min(jnp.where(pe2 == p2, lane, big), axis=-1, keepdims=True)
        den = p1 + p2
        g1 = p_grp * p1 / den
        g2 = p_grp * p2 / den
        e1 = i1 - N_GROUPS
        e2 = i2 - N_GROUPS
        route_ref[rows, :] = jnp.where(lane == 0, e1, jnp.where(lane == 1, e2, jnp.where(
            lane == 2, g1, jnp.where(lane == 3, g2, 0.0))))
        hits = jnp.where(lane == e1, 1.0, 0.0) + jnp.where(lane == e2, 1.0, 0.0)
        hits_sum = hits_sum + jnp.sum(hits, axis=0, keepdims=True)
    cnt_ref[...] += hits_sum


def _out_proj(ys, x2d, w_o, layer, ln_g, ln_b, w_route):
    t = x2d.shape[0]
    tm = 1024
    row = lambda i: (i, 0)
    full = lambda i: (0, 0)
    return pl.pallas_call(
        _oproj_kernel,
        grid=(t // tm,),
        in_specs=[pl.BlockSpec((tm, W_MIX), row)] * 4 + [
            pl.BlockSpec((tm, D_MODEL), row),
            pl.BlockSpec((None, D_MODEL, D_MODEL), lambda i: (layer, 0, 0)),
            pl.BlockSpec((1, D_MODEL), full),
            pl.BlockSpec((1, D_MODEL), full),
            pl.BlockSpec((D_MODEL, 2 * LANES), full)],
        out_specs=[pl.BlockSpec((tm, D_MODEL), row), pl.BlockSpec((tm, D_MODEL // 2), row),
                   pl.BlockSpec((tm, LANES), row), pl.BlockSpec((1, LANES), full)],
        out_shape=[jax.ShapeDtypeStruct((t, D_MODEL), F32),
                   jax.ShapeDtypeStruct((t, D_MODEL // 2), U32),
                   jax.ShapeDtypeStruct((t, LANES), F32),
                   jax.ShapeDtypeStruct((1, LANES), F32)],
        compiler_params=_cparams(("arbitrary",)),
        name="out_proj_ln_router",
    )(*ys, x2d, w_o, ln_g, ln_b, w_route)


PLAN_TILE = 1024


def _plan_kernel(route_ref, cnt_ref, dest_ref, meta_ref, carry_ref, pstart_ref):
    tm = route_ref.shape[0]
    lane = lax.broadcasted_iota(jnp.int32, (1, LANES), 1).astype(F32)

    @pl.when(pl.program_id(0) == 0)
    def _():
        cnt = cnt_ref[...]
        padded = jnp.ceil(cnt * (1.0 / MOE_BLK)) * MOE_BLK
        li = lax.broadcasted_iota(jnp.int32, (LANES, LANES), 0)
        lj = lax.broadcasted_iota(jnp.int32, (LANES, LANES), 1)
        before = (li < lj).astype(F32)
        pstart = jnp.dot(jnp.broadcast_to(padded, (8, LANES)), before, precision=HI,
                         preferred_element_type=F32)[0:1]
        pstart_ref[...] = pstart
        carry_ref[...] = jnp.zeros_like(carry_ref)
        meta_ref[...] = jnp.concatenate(
            [pstart + padded, pstart, cnt, jnp.zeros((5, LANES), F32)], axis=0)

    oh0 = jnp.where(lane == route_ref[:, 0:1], 1.0, 0.0)
    oh1 = jnp.where(lane == route_ref[:, 1:2], 1.0, 0.0)
    both = oh0 + oh1
    ri = lax.broadcasted_iota(jnp.int32, (tm, tm), 0)
    ci = lax.broadcasted_iota(jnp.int32, (tm, tm), 1)
    earlier = (ci < ri).astype(BF16)
    base = (jnp.dot(earlier, both.astype(BF16), preferred_element_type=F32)
            + carry_ref[...] + pstart_ref[...])
    d0 = jnp.sum(oh0 * base, axis=-1, keepdims=True)
    d1 = jnp.sum(oh1 * base, axis=-1, keepdims=True)
    dest = jnp.where(lane == 0, d0, jnp.where(lane == 1, d1, 0.0))
    dest_ref[...] = dest.T[0:2, :].astype(jnp.int32)
    carry_ref[...] += jnp.sum(both, axis=0, keepdims=True)


def _dispatch_plan(route, cnt, n_blocks):
    t = route.shape[0]
    tm = PLAN_TILE
    blk = MOE_BLK
    dest, meta = pl.pallas_call(
        _plan_kernel,
        grid=(t // tm,),
        in_specs=[pl.BlockSpec((tm, LANES), lambda i: (i, 0)),
                  pl.BlockSpec((1, LANES), lambda i: (0, 0))],
        out_specs=[pl.BlockSpec((2, tm), lambda i: (0, i)),
                   pl.BlockSpec((8, LANES), lambda i: (0, 0))],
        out_shape=[jax.ShapeDtypeStruct((2, t), jnp.int32),
                   jax.ShapeDtypeStruct((8, LANES), F32)],
        scratch_shapes=[pltpu.VMEM((1, LANES), F32), pltpu.VMEM((1, LANES), F32)],
        compiler_params=_cparams(("arbitrary",)),
        name="moe_plan",
    )(route, cnt)
    meta_i = meta[:, :N_EXPERTS].astype(jnp.int32)
    pad_end, pad_start, seg_end = meta_i[0], meta_i[1], meta_i[1] + meta_i[2]
    starts = jnp.arange(n_blocks, dtype=jnp.int32)[:, None] * blk
    member = jnp.logical_and(starts >= pad_start[None, :], starts < pad_end[None, :])
    expert_ids = jnp.arange(N_EXPERTS, dtype=jnp.int32)[None, :]
    block_e = jnp.where(jnp.any(member, axis=1), jnp.sum(jnp.where(member, expert_ids, 0), axis=1),
                        N_EXPERTS - 1)
    n_used = (pad_end[N_EXPERTS - 1] // blk).reshape(1)
    n_valid = jnp.clip(jnp.sum(jnp.where(member, seg_end[None, :], 0), axis=1) - starts[:, 0],
                       0, blk)
    return dest.reshape(2 * t), block_e, n_used, n_valid


def _ffn_kernel(be_ref, nu_ref, nv_ref, xs_ref, wg_ref, wu_ref, wd_ref, y_ref,
                wgb_ref, wub_ref, wdb_ref):
    i = pl.program_id(0)
    prev = be_ref[jnp.maximum(i - 1, 0)]

    @pl.when(jnp.logical_or(i == 0, be_ref[i] != prev))
    def _():
        wgb_ref[...] = wg_ref[...].astype(BF16)
        wub_ref[...] = wu_ref[...].astype(BF16)
        wdb_ref[...] = wd_ref[...].astype(BF16)

    @pl.when(i < nu_ref[0])
    def _():
        half = xs_ref.shape[0] // 2
        for r in range(2):
            rows = slice(r * half, (r + 1) * half)
            row = lax.broadcasted_iota(jnp.int32, (half, 1), 0) + r * half
            xp = jnp.where(row < nv_ref[i], xs_ref[rows, :], U32(0))
            x_hi, x_lo = _unpack_halves(xp)
            xb = jnp.concatenate([x_hi.astype(BF16), x_lo.astype(BF16)], axis=1)
            a = jnp.dot(xb, wgb_ref[...], preferred_element_type=F32)
            u = jnp.dot(xb, wub_ref[...], preferred_element_type=F32)
            h = (a * _sigmoid(a) * u).astype(BF16)
            y_ref[rows, :] = _pack_halves(jnp.dot(h, wdb_ref[...], preferred_element_type=F32))

    @pl.when(i >= nu_ref[0])
    def _():
        y_ref[...] = jnp.zeros_like(y_ref)


def _expert_ffn(xs, block_e, n_used, n_valid, w_gate, w_up, w_down, layer):
    n_slots = xs.shape[0]
    blk = MOE_BLK
    w_map = lambda i, be, nu, nv: (layer, be[i], 0, 0)
    grid_spec = pltpu.PrefetchScalarGridSpec(
        num_scalar_prefetch=3,
        grid=(n_slots // blk,),
        in_specs=[pl.BlockSpec((blk, D_MODEL // 2),
                               lambda i, be, nu, nv: (jnp.minimum(i, nu[0] - 1), 0)),
                  pl.BlockSpec((None, None, D_MODEL, D_EXPERT), w_map),
                  pl.BlockSpec((None, None, D_MODEL, D_EXPERT), w_map),
                  pl.BlockSpec((None, None, D_EXPERT, D_MODEL), w_map)],
        out_specs=pl.BlockSpec((blk, D_MODEL // 2), lambda i, be, nu, nv: (i, 0)),
        scratch_shapes=[pltpu.VMEM((D_MODEL, D_EXPERT), BF16),
                        pltpu.VMEM((D_MODEL, D_EXPERT), BF16),
                        pltpu.VMEM((D_EXPERT, D_MODEL), BF16)],
    )
    return pl.pallas_call(
        _ffn_kernel,
        grid_spec=grid_spec,
        out_shape=jax.ShapeDtypeStruct((n_slots, D_MODEL // 2), U32),
        compiler_params=_cparams(("arbitrary",)),
        name="expert_ffn",
    )(block_e, n_used, n_valid, xs, w_gate, w_up, w_down)


SC_CORES = 2
SC_SUBCORES = 16
SC_ROWS = 64


def _sc_gather_rows(table, idx):
    b = idx.shape[0]
    d = table.shape[1]
    per_w = b // (SC_CORES * SC_SUBCORES)
    mesh = plsc.VectorSubcoreMesh(core_axis_name="c", subcore_axis_name="s")

    n_chunks = per_w // SC_ROWS

    @functools.partial(
        pl.kernel, mesh=mesh,
        out_type=jax.ShapeDtypeStruct((b, d), table.dtype),
        scratch_types=[pltpu.VMEM((SC_ROWS,), jnp.int32), pltpu.VMEM((SC_ROWS,), jnp.int32),
                       pltpu.VMEM((SC_ROWS, d), table.dtype),
                       pltpu.VMEM((SC_ROWS, d), table.dtype),
                       pltpu.SemaphoreType.DMA, pltpu.SemaphoreType.DMA,
                       pltpu.SemaphoreType.DMA, pltpu.SemaphoreType.DMA],
        name="sc_gather_rows",
    )
    def gather(table_hbm, idx_hbm, out_hbm, idx0, idx1, rows0, rows1, gs0, gs1, ws0, ws1):
        idx_v, rows_v, gsem, wsem = (idx0, idx1), (rows0, rows1), (gs0, gs1), (ws0, ws1)
        wid = lax.axis_index("s") * SC_CORES + lax.axis_index("c")
        base = wid * per_w

        def rows_of(c):
            return pl.ds(pl.multiple_of(base + c * SC_ROWS, SC_ROWS), SC_ROWS)

        def start_gather(c, s):
            pltpu.sync_copy(idx_hbm.at[rows_of(c)], idx_v[s])
            pltpu.async_copy(table_hbm.at[idx_v[s]], rows_v[s], gsem[s])

        def write_back(c, s):
            pltpu.make_async_copy(table_hbm.at[idx_v[s]], rows_v[s], gsem[s]).wait()
            pltpu.async_copy(rows_v[s], out_hbm.at[rows_of(c)], wsem[s]).wait()

        start_gather(0, 0)

        @pl.loop(0, n_chunks, step=2)
        def _(c):
            start_gather(c + 1, 1)
            write_back(c, 0)

            @pl.when(c + 2 < n_chunks)
            def _():
                start_gather(c + 2, 0)

            write_back(c + 1, 1)

    return gather(table, idx)


def _sc_scatter_rows(x2d, idx, n_slots):
    t, d = x2d.shape
    per_w = t // (SC_CORES * SC_SUBCORES)
    mesh = plsc.VectorSubcoreMesh(core_axis_name="c", subcore_axis_name="s")

    @functools.partial(
        pl.kernel, mesh=mesh,
        out_type=jax.ShapeDtypeStruct((n_slots, d), x2d.dtype),
        scratch_types=[pltpu.VMEM((SC_ROWS,), jnp.int32), pltpu.VMEM((SC_ROWS,), jnp.int32),
                       pltpu.VMEM((SC_ROWS, d), x2d.dtype),
                       pltpu.SemaphoreType.DMA, pltpu.SemaphoreType.DMA],
        name="sc_scatter_rows",
    )
    def scatter(x_hbm, idx_hbm, out_hbm, idx0, idx1, rows_v, s0, s1):
        wid = lax.axis_index("s") * SC_CORES + lax.axis_index("c")
        base = wid * per_w

        @pl.loop(0, per_w // SC_ROWS)
        def _(c):
            off = pl.multiple_of(base + c * SC_ROWS, SC_ROWS)
            pltpu.sync_copy(x_hbm.at[pl.ds(off, SC_ROWS)], rows_v)
            pltpu.sync_copy(idx_hbm.at[pl.ds(off, SC_ROWS)], idx0)
            pltpu.sync_copy(idx_hbm.at[pl.ds(t + off, SC_ROWS)], idx1)
            cp0 = pltpu.async_copy(rows_v, out_hbm.at[idx0], s0)
            cp1 = pltpu.async_copy(rows_v, out_hbm.at[idx1], s1)
            cp0.wait()
            cp1.wait()

    return scatter(x2d, idx)


def _combine_dense_kernel(route_ref, x_ref, y0_ref, y1_ref, g_ref, b_ref, o_ref):
    y0 = jnp.concatenate(_unpack_halves(y0_ref[...]), axis=1)
    y1 = jnp.concatenate(_unpack_halves(y1_ref[...]), axis=1)
    moe = route_ref[:, 2:3] * y0 + route_ref[:, 3:4] * y1
    h = ALPHA * x_ref[...] + moe
    o_ref[...] = _layer_norm(h, g_ref[...], b_ref[...])


def _combine_dense(route, x2d, yg, ln_g, ln_b):
    t = x2d.shape[0]
    tm = 1024
    nt = t // tm
    row = lambda i: (i, 0)
    full = lambda i: (0, 0)
    return pl.pallas_call(
        _combine_dense_kernel,
        grid=(nt,),
        in_specs=[pl.BlockSpec((tm, LANES), row),
                  pl.BlockSpec((tm, D_MODEL), row),
                  pl.BlockSpec((tm, D_MODEL // 2), row),
                  pl.BlockSpec((tm, D_MODEL // 2), lambda i: (i + nt, 0)),
                  pl.BlockSpec((1, D_MODEL), full),
                  pl.BlockSpec((1, D_MODEL), full)],
        out_specs=pl.BlockSpec((tm, D_MODEL), row),
        out_shape=jax.ShapeDtypeStruct((t, D_MODEL), F32),
        compiler_params=_cparams(("parallel",)),
        name="moe_combine_dense",
    )(route, x2d, yg, yg, ln_g, ln_b)


def _pad_cols(w, n):
    return jnp.pad(w, [(0, 0)] * (w.ndim - 1) + [(0, n - w.shape[-1])])


def kernel(x, w_in, conv_w, gla_w_lr, gla_b_lr, gla_norm_g, ssd_conv_w, ssd_conv_b, ssd_a_log,
           ssd_d, ssd_dt_bias, ssd_norm_g, diff_lq1, diff_lk1, diff_lq2, diff_lk2, diff_norm_g,
           w_o, ln1_g, ln1_b, router_g, router_e, w_gate, w_up, w_down, ln2_g, ln2_b):
    bsz, seq, d = x.shape
    t = bsz * seq
    n_assign = 2 * t
    n_blocks = (n_assign + N_EXPERTS * (MOE_BLK - 1)) // MOE_BLK + 1
    n_slots = n_blocks * MOE_BLK
    x2d = x.reshape(t, d)
    w_o_b = w_o.astype(BF16)
    for l in range(DEPTH):
        pc, pg, plr, ps, pdt, pd = [p.reshape(bsz, seq, -1) for p in _in_proj(x2d, w_in, l)]

        w_lr_pad = jnp.pad(gla_w_lr[l], ((0, LANES - GLA_RANK), (0, 0)))
        pad4 = lambda v: jnp.pad(v, (0, LANES - SSD_HEADS)).reshape(1, LANES)
        y_conv, y_gla, y_ssd = _recurrent_mixers(
            pc, conv_w[l], pg, plr, w_lr_pad, gla_b_lr[l].reshape(1, -1),
            jnp.tile(gla_norm_g[l], GLA_HEADS).reshape(1, -1),
            ps, pdt, ssd_conv_w[l], ssd_conv_b[l].reshape(1, -1),
            pad4(ssd_a_log[l]), pad4(ssd_dt_bias[l]),
            jnp.repeat(ssd_d[l], SSD_HEADDIM).reshape(1, -1), ssd_norm_g[l].reshape(1, -1))
        lam_vecs = jnp.pad(jnp.stack([diff_lq1[l], diff_lk1[l], diff_lq2[l], diff_lk2[l]]),
                           ((0, 0), (0, LANES - DIFF_DQK)))
        lam_init = 0.8 - 0.6 * math.exp(-0.3 * l)
        y_diff = _diff_mixer(pd, lam_vecs,
                             jnp.tile(diff_norm_g[l], DIFF_HEADS).reshape(1, -1), lam_init)

        w_route = _pad_cols(jnp.concatenate(
            [router_g[l], router_e[l].reshape(d, N_EXPERTS)], axis=1), LANES)
        w_route_hi = w_route.astype(BF16)
        w_route = jnp.concatenate(
            [w_route_hi, (w_route - w_route_hi.astype(F32)).astype(BF16)], axis=1)
        ys = [y.reshape(t, W_MIX) for y in (y_conv, y_gla, y_ssd, y_diff)]
        xn, xn_p, route, cnt = _out_proj(ys, x2d, w_o_b, l, ln1_g[l].reshape(1, -1),
                                         ln1_b[l].reshape(1, -1), w_route)

        idx, block_e, n_used, n_valid = _dispatch_plan(route, cnt, n_blocks)
        xs = _sc_scatter_rows(xn_p, idx, n_slots)
        y = _expert_ffn(xs, block_e, n_used, n_valid, w_gate, w_up, w_down, l)
        yg = _sc_gather_rows(y, idx)
        x2d = _combine_dense(route, xn, yg, ln2_g[l].reshape(1, -1), ln2_b[l].reshape(1, -1))
    return x2d.reshape(bsz, seq, d)
```
